```python
import jax
import jax.numpy as jnp
from jax import lax
import numpy as np

D_MODEL = 1024
BATCH = 8
SEQ = 4096
DEPTH = 1

GRID_W = 64
CTX_LEN = 256
GM_WIDTH = 1024
GM_GROUPS = 8
GM_GROUP_DIM = GM_WIDTH // GM_GROUPS
GM_CHUNK = 2 * GRID_W
GDN_HEADS = 8
GDN_DK = 128
GDN_DV = 128
GDN_KD = GDN_HEADS * GDN_DK
GDN_VD = GDN_HEADS * GDN_DV
GDN_CONV = 3
GDN_CHUNK = 64
EPS = 1e-6

OFF_Q = 0
OFF_K = OFF_Q + GDN_KD
OFF_V = OFF_K + GDN_KD
OFF_A = OFF_V + GDN_VD
OFF_B = OFF_A + 2 * GDN_HEADS
OFF_ZB = OFF_B + 2 * GDN_HEADS
OFF_UA = OFF_ZB + GDN_VD
OFF_VA = OFF_UA + GM_WIDTH
OFF_ZA = OFF_VA + GM_WIDTH
OFF_G = OFF_ZA + GM_WIDTH
IN_COLS = OFF_G + 2 * D_MODEL

kernel_name = 'hybrid_gmlp_gdeltanet_prefix_dit'


def _rmsnorm(x, g):
    xf = x.astype(jnp.float32)
    y = xf * lax.rsqrt(jnp.mean(xf * xf, axis=-1, keepdims=True) + EPS)
    return (y * g.astype(jnp.float32)).astype(x.dtype)


def _layernorm(x, g, b):
    xf = x.astype(jnp.float32)
    xc = xf - jnp.mean(xf, axis=-1, keepdims=True)
    y = xc * lax.rsqrt(jnp.mean(xc * xc, axis=-1, keepdims=True) + EPS)
    return (y * g.astype(jnp.float32) + b.astype(jnp.float32)).astype(x.dtype)


def _l2norm(x):
    return x * lax.rsqrt(jnp.sum(x * x, axis=-1, keepdims=True) + EPS)


def _heads(u, d):
    return u.reshape(u.shape[:-1] + (u.shape[-1] // d, d))


def _modulation(cond, w_mod, b_mod):
    m = jax.nn.silu(cond) @ w_mod + b_mod
    m = m.reshape((-1, 1, 3, D_MODEL))
    return m[:, :, 0], m[:, :, 1], m[:, :, 2]


def _projector(h, w_in, full):
    if full:
        p = h @ w_in
        return lambda lo, hi: p[..., lo:hi]
    return lambda lo, hi: h @ w_in[:, lo:hi]


def _conv_silu(u, w):
    pad = w.shape[0] // 2
    y = lax.conv_general_dilated(u, w[:, None, :].astype(u.dtype), window_strides=(1,),
                                 padding=[(pad, pad)], dimension_numbers=('NWC', 'WIO', 'NWC'),
                                 feature_group_count=u.shape[-1])
    return jax.nn.silu(y)


def _gdn_side(col, w_conv, a_log, dt_bias, with_q):
    lo = OFF_Q if with_q else OFF_K
    qkv = _conv_silu(col(lo, OFF_A), w_conv[:, lo - OFF_Q:]).astype(jnp.float32)
    k = _l2norm(_heads(qkv[..., -(GDN_KD + GDN_VD):-GDN_VD], GDN_DK))
    v = _heads(qkv[..., -GDN_VD:], GDN_DV)
    q = _l2norm(_heads(qkv[..., :GDN_KD], GDN_DK)) if with_q else None
    ab = col(OFF_A, OFF_ZB).astype(jnp.float32)
    ab = ab.reshape(ab.shape[:-1] + (2, 2, GDN_HEADS))
    g = -jnp.exp(a_log.astype(jnp.float32)) * jax.nn.softplus(ab[..., 0, :, :] + dt_bias.astype(jnp.float32))
    beta = jax.nn.sigmoid(ab[..., 1, :, :])
    return q, k, v, g, beta


def _chunks(t):
    t = jnp.moveaxis(t, 2, 1)
    t = t.reshape(t.shape[:2] + (-1, GDN_CHUNK) + t.shape[3:])
    return jnp.moveaxis(t, 2, 0)


def _unchunk(t):
    n, b, h, cl, d = t.shape
    return t.transpose(1, 0, 3, 2, 4).reshape(b, n * cl, h, d)


def _gdn_direction(q, k, v, beta, g, s0):
    kc, vc, bc, gc = (_chunks(t) for t in (k, v, beta, g))
    gcum = jnp.cumsum(gc, axis=-1)
    idx = jnp.arange(GDN_CHUNK)
    incl = idx[:, None] >= idx[None, :]
    decay = jnp.exp(jnp.where(incl, gcum[..., :, None] - gcum[..., None, :], -jnp.inf))
    kb = kc * bc[..., None]
    a_mat = jnp.where(idx[:, None] > idx[None, :],
                      jnp.einsum('nbhcd,nbhsd->nbhcs', kb, kc) * decay, 0.0)
    rhs = jnp.concatenate([vc * bc[..., None], kb * jnp.exp(gcum)[..., None]], axis=-1)
    sol = lax.linalg.triangular_solve(a_mat + jnp.eye(GDN_CHUNK, dtype=jnp.float32), rhs,
                                      left_side=True, lower=True)
    u_val, w_key = sol[..., :GDN_DV], sol[..., GDN_DV:]
    g_last = gcum[..., -1]
    k_tail = kc * jnp.exp(g_last[..., None] - gcum)[..., None]

    def advance(S, u_c, w_c, kt_c, gl_c):
        v_new = u_c - jnp.einsum('bhcd,bhde->bhce', w_c, S)
        S_new = S * jnp.exp(gl_c)[..., None, None] + jnp.einsum('bhcd,bhce->bhde', kt_c, v_new)
        return v_new, S_new

    if q is None:
        def state_step(S, xs):
            _, S_new = advance(S, *xs)
            return S_new, None
        s_final, _ = lax.scan(state_step, s0, (u_val, w_key, k_tail, g_last))
        return None, s_final

    qc = _chunks(q) * (GDN_DK ** -0.5)
    attn = jnp.einsum('nbhcd,nbhsd->nbhcs', qc, kc) * decay
    q_dec = qc * jnp.exp(gcum)[..., None]

    def out_step(S, xs):
        u_c, w_c, kt_c, gl_c, qd_c, at_c = xs
        v_new, S_new = advance(S, u_c, w_c, kt_c, gl_c)
        o = jnp.einsum('bhcd,bhde->bhce', qd_c, S) + jnp.einsum('bhcs,bhse->bhce', at_c, v_new)
        return S_new, o

    s_final, o = lax.scan(out_step, s0, (u_val, w_key, k_tail, g_last, q_dec, attn))
    return _unchunk(o), s_final


def _gdn_bidir(q, k, v, g, beta, s0_f, s0_b):
    rev = lambda t: None if t is None else jnp.flip(t, axis=1)
    o_f, s_f = _gdn_direction(q, k, v, beta[:, :, 0], g[:, :, 0], s0_f)
    o_b, s_b = _gdn_direction(rev(q), rev(k), rev(v), rev(beta[:, :, 1]), rev(g[:, :, 1]), s0_b)
    o = None if q is None else o_f + rev(o_b)
    return o, s_f, s_b


def _gmlp(pu, pv, pz, ln_g, ln_b, w_sp, b_sp, n_chunks):
    u = jax.nn.gelu(pu)
    v = _layernorm(jax.nn.gelu(pv), ln_g, ln_b)
    b, l, _ = v.shape
    v = v.reshape(b, n_chunks, GM_CHUNK, GM_GROUPS, GM_GROUP_DIM)
    s = jnp.einsum('gpq,bnqgc->bnpgc', w_sp, v) + jnp.transpose(b_sp)[:, :, None]
    return u * s.reshape(b, l, GM_WIDTH) * jax.nn.silu(pz)


def _stream_out(col, o_gdn, n_chunks, g_onorm, gm_ln_g, gm_ln_b, w_sp, b_sp, w_pa, w_pb, w_out):
    zb_raw = col(OFF_ZB, OFF_UA)
    z_b = _heads(zb_raw, GDN_DV).astype(jnp.float32)
    y_b = _rmsnorm(o_gdn, g_onorm) * jax.nn.silu(z_b)
    y_b = y_b.reshape(y_b.shape[:-2] + (GDN_VD,)).astype(zb_raw.dtype)
    y_a = _gmlp(col(OFF_UA, OFF_VA), col(OFF_VA, OFF_ZA), col(OFF_ZA, OFF_G),
                gm_ln_g, gm_ln_b, w_sp, b_sp, n_chunks)
    gates = jax.nn.sigmoid(col(OFF_G, IN_COLS))
    merged = gates[..., :D_MODEL] * (y_a @ w_pa) + gates[..., D_MODEL:] * (y_b @ w_pb)
    return merged @ w_out


def _hybrid_layer(x, ctx, mod_x, mod_c, n_chunks, update_ctx, g_pre, g_post, w_in, w_conv,
                  a_log, dt_bias, g_onorm, gm_ln_g, gm_ln_b, w_sp, b_sp, w_pa, w_pb, w_out):
    shift_x, scale_x, gate_x = mod_x
    shift_c, scale_c, gate_c = mod_c
    s_zero = jnp.zeros((ctx.shape[0], GDN_HEADS, GDN_DK, GDN_DV), jnp.float32)
    branch_w = (g_onorm, gm_ln_g, gm_ln_b, w_sp, b_sp, w_pa, w_pb, w_out)
    h_c = _rmsnorm(ctx, g_pre) * (1 + scale_c) + shift_c
    col_c = _projector(h_c, w_in, update_ctx)
    q_c, k_c, v_c, g_c, beta_c = _gdn_side(col_c, w_conv, a_log, dt_bias, update_ctx)
    o_c, s_fwd, s_bwd = _gdn_bidir(q_c, k_c, v_c, g_c, beta_c, s_zero, s_zero)
    h_x = _rmsnorm(x, g_pre) * (1 + scale_x) + shift_x
    col_x = _projector(h_x, w_in, True)
    q_x, k_x, v_x, g_x, beta_x = _gdn_side(col_x, w_conv, a_log, dt_bias, True)
    o_x, _, _ = _gdn_bidir(q_x, k_x, v_x, g_x, beta_x, s_fwd, s_bwd)
    x = x + gate_x * _rmsnorm(_stream_out(col_x, o_x, n_chunks, *branch_w), g_post)
    if update_ctx:
        ctx = ctx + gate_c * _rmsnorm(_stream_out(col_c, o_c, ctx.shape[1] // GM_CHUNK, *branch_w), g_post)
    return x, ctx


def _fwd_setup_inputs(seed: int = 0) -> dict:
    key = jax.random.key(seed)
    ks = jax.random.split(key, 20)
    nrm = lambda k, shape, s: s * jax.random.normal(k, shape, jnp.float32)
    d = D_MODEL
    x = nrm(ks[0], (BATCH, SEQ, d), 1.0)
    c = nrm(ks[1], (BATCH, d), 1.0)
    ctx = nrm(ks[2], (BATCH, CTX_LEN, d), 1.0)
    c_ctx = nrm(ks[3], (d,), 1.0)
    w_mod = nrm(ks[4], (DEPTH, d, 3 * d), 0.5 * d ** -0.5)
    b_mod = nrm(ks[5], (DEPTH, 3 * d), 0.02)
    g_pre = 1.0 + nrm(ks[6], (DEPTH, d), 0.05)
    g_post = 1.0 + nrm(ks[7], (DEPTH, d), 0.05)
    w_in = nrm(ks[8], (DEPTH, d, IN_COLS), d ** -0.5)
    w_conv = nrm(ks[9], (DEPTH, GDN_CONV, OFF_A), GDN_CONV ** -0.5)
    a_log = jnp.log(jax.random.uniform(ks[10], (DEPTH, 2, GDN_HEADS), jnp.float32, 1.0, 16.0))
    dt = jnp.exp(jax.random.uniform(ks[11], (DEPTH, 2, GDN_HEADS), jnp.float32,
                                    float(np.log(1e-3)), float(np.log(1e-1))))
    dt_bias = dt + jnp.log(-jnp.expm1(-dt))
    g_onorm = 1.0 + nrm(ks[12], (DEPTH, GDN_DV), 0.05)
    gm_ln_g = 1.0 + nrm(ks[13], (DEPTH, GM_WIDTH), 0.05)
    gm_ln_b = nrm(ks[14], (DEPTH, GM_WIDTH), 0.02)
    w_sp = nrm(ks[15], (DEPTH, GM_GROUPS, GM_CHUNK, GM_CHUNK), GM_CHUNK ** -0.5)
    b_sp = 1.0 + nrm(ks[16], (DEPTH, GM_GROUPS, GM_CHUNK), 0.02)
    w_pa = nrm(ks[17], (DEPTH, GM_WIDTH, d), GM_WIDTH ** -0.5)
    w_pb = nrm(ks[18], (DEPTH, GDN_VD, d), GDN_VD ** -0.5)
    w_out = nrm(ks[19], (DEPTH, d, d), d ** -0.5)
    return {'x': x, 'c': c, 'ctx': ctx, 'c_ctx': c_ctx, 'w_mod': w_mod, 'b_mod': b_mod,
            'g_pre': g_pre, 'g_post': g_post, 'w_in': w_in, 'w_conv': w_conv, 'a_log': a_log,
            'dt_bias': dt_bias, 'g_onorm': g_onorm, 'gm_ln_g': gm_ln_g, 'gm_ln_b': gm_ln_b,
            'w_sp': w_sp, 'b_sp': b_sp, 'w_pa': w_pa, 'w_pb': w_pb, 'w_out': w_out}


def _fwd_reference(x, c, ctx, c_ctx, w_mod, b_mod, g_pre, g_post, w_in, w_conv, a_log, dt_bias,
              g_onorm, gm_ln_g, gm_ln_b, w_sp, b_sp, w_pa, w_pb, w_out):
    rows = x.shape[1] // GRID_W
    n_chunks = rows // (GM_CHUNK // GRID_W)
    for i in range(DEPTH):
        mod_x = _modulation(c, w_mod[i], b_mod[i])
        mod_c = _modulation(c_ctx, w_mod[i], b_mod[i])
        x, ctx = _hybrid_layer(x, ctx, mod_x, mod_c, n_chunks, i + 1 < DEPTH, g_pre[i], g_post[i],
                               w_in[i], w_conv[i], a_log[i], dt_bias[i], g_onorm[i], gm_ln_g[i],
                               gm_ln_b[i], w_sp[i], b_sp[i], w_pa[i], w_pb[i], w_out[i])
    return x


import jax as _jax
import jax.numpy as _jnp

TWIN_FORMAT = 'train_step'
FWD_PARAMS = ['x', 'c', 'ctx', 'c_ctx', 'w_mod', 'b_mod', 'g_pre', 'g_post', 'w_in', 'w_conv', 'a_log', 'dt_bias', 'g_onorm', 'gm_ln_g', 'gm_ln_b', 'w_sp', 'b_sp', 'w_pa', 'w_pb', 'w_out']
TWIN_WEIGHTS = ['c_ctx', 'w_mod', 'b_mod', 'g_pre', 'g_post', 'w_in', 'w_conv', 'a_log', 'dt_bias', 'g_onorm', 'gm_ln_g', 'gm_ln_b', 'w_sp', 'b_sp', 'w_pa', 'w_pb', 'w_out']
TWIN_DIFF_INPUT = 'x'
TWIN_INPUTS = ['x', 'c', 'ctx', 'c_ctx', 'w_mod', 'b_mod', 'g_pre', 'g_post', 'w_in', 'w_conv', 'a_log', 'dt_bias', 'g_onorm', 'gm_ln_g', 'gm_ln_b', 'w_sp', 'b_sp', 'w_pa', 'w_pb', 'w_out', 'loss_target', 'm_c_ctx', 'm_w_mod', 'm_b_mod', 'm_g_pre', 'm_g_post', 'm_w_in', 'm_w_conv', 'm_a_log', 'm_dt_bias', 'm_g_onorm', 'm_gm_ln_g', 'm_gm_ln_b', 'm_w_sp', 'm_b_sp', 'm_w_pa', 'm_w_pb', 'm_w_out', 'v_c_ctx', 'v_w_mod', 'v_b_mod', 'v_g_pre', 'v_g_post', 'v_w_in', 'v_w_conv', 'v_a_log', 'v_dt_bias', 'v_g_onorm', 'v_gm_ln_g', 'v_gm_ln_b', 'v_w_sp', 'v_b_sp', 'v_w_pa', 'v_w_pb', 'v_w_out']
TWIN_OUTPUTS = ['loss', 'grad_x', 'grad_c_ctx', 'grad_w_mod', 'grad_b_mod', 'grad_g_pre', 'grad_g_post', 'grad_w_in', 'grad_w_conv', 'grad_a_log', 'grad_dt_bias', 'grad_g_onorm', 'grad_gm_ln_g', 'grad_gm_ln_b', 'grad_w_sp', 'grad_b_sp', 'grad_w_pa', 'grad_w_pb', 'grad_w_out', 'delta_c_ctx', 'delta_w_mod', 'delta_b_mod', 'delta_g_pre', 'delta_g_post', 'delta_w_in', 'delta_w_conv', 'delta_a_log', 'delta_dt_bias', 'delta_g_onorm', 'delta_gm_ln_g', 'delta_gm_ln_b', 'delta_w_sp', 'delta_b_sp', 'delta_w_pa', 'delta_w_pb', 'delta_w_out', 'new_m_c_ctx', 'new_m_w_mod', 'new_m_b_mod', 'new_m_g_pre', 'new_m_g_post', 'new_m_w_in', 'new_m_w_conv', 'new_m_a_log', 'new_m_dt_bias', 'new_m_g_onorm', 'new_m_gm_ln_g', 'new_m_gm_ln_b', 'new_m_w_sp', 'new_m_b_sp', 'new_m_w_pa', 'new_m_w_pb', 'new_m_w_out', 'new_v_c_ctx', 'new_v_w_mod', 'new_v_b_mod', 'new_v_g_pre', 'new_v_g_post', 'new_v_w_in', 'new_v_w_conv', 'new_v_a_log', 'new_v_dt_bias', 'new_v_g_onorm', 'new_v_gm_ln_g', 'new_v_gm_ln_b', 'new_v_w_sp', 'new_v_b_sp', 'new_v_w_pa', 'new_v_w_pb', 'new_v_w_out']
TWIN_LEAF_KINDS = {'loss': 'loss', 'grad_x': 'grad_x', 'grad_c_ctx': 'grad_w', 'grad_w_mod': 'grad_w', 'grad_b_mod': 'grad_w', 'grad_g_pre': 'grad_w', 'grad_g_post': 'grad_w', 'grad_w_in': 'grad_w', 'grad_w_conv': 'grad_w', 'grad_a_log': 'grad_w', 'grad_dt_bias': 'grad_w', 'grad_g_onorm': 'grad_w', 'grad_gm_ln_g': 'grad_w', 'grad_gm_ln_b': 'grad_w', 'grad_w_sp': 'grad_w', 'grad_b_sp': 'grad_w', 'grad_w_pa': 'grad_w', 'grad_w_pb': 'grad_w', 'grad_w_out': 'grad_w', 'delta_c_ctx': 'delta_w', 'delta_w_mod': 'delta_w', 'delta_b_mod': 'delta_w', 'delta_g_pre': 'delta_w', 'delta_g_post': 'delta_w', 'delta_w_in': 'delta_w', 'delta_w_conv': 'delta_w', 'delta_a_log': 'delta_w', 'delta_dt_bias': 'delta_w', 'delta_g_onorm': 'delta_w', 'delta_gm_ln_g': 'delta_w', 'delta_gm_ln_b': 'delta_w', 'delta_w_sp': 'delta_w', 'delta_b_sp': 'delta_w', 'delta_w_pa': 'delta_w', 'delta_w_pb': 'delta_w', 'delta_w_out': 'delta_w', 'new_m_c_ctx': 'new_m', 'new_m_w_mod': 'new_m', 'new_m_b_mod': 'new_m', 'new_m_g_pre': 'new_m', 'new_m_g_post': 'new_m', 'new_m_w_in': 'new_m', 'new_m_w_conv': 'new_m', 'new_m_a_log': 'new_m', 'new_m_dt_bias': 'new_m', 'new_m_g_onorm': 'new_m', 'new_m_gm_ln_g': 'new_m', 'new_m_gm_ln_b': 'new_m', 'new_m_w_sp': 'new_m', 'new_m_b_sp': 'new_m', 'new_m_w_pa': 'new_m', 'new_m_w_pb': 'new_m', 'new_m_w_out': 'new_m', 'new_v_c_ctx': 'new_v', 'new_v_w_mod': 'new_v', 'new_v_b_mod': 'new_v', 'new_v_g_pre': 'new_v', 'new_v_g_post': 'new_v', 'new_v_w_in': 'new_v', 'new_v_w_conv': 'new_v', 'new_v_a_log': 'new_v', 'new_v_dt_bias': 'new_v', 'new_v_g_onorm': 'new_v', 'new_v_gm_ln_g': 'new_v', 'new_v_gm_ln_b': 'new_v', 'new_v_w_sp': 'new_v', 'new_v_b_sp': 'new_v', 'new_v_w_pa': 'new_v', 'new_v_w_pb': 'new_v', 'new_v_w_out': 'new_v'}


def _forward(args):
    return _fwd_reference(*[args[k] for k in FWD_PARAMS])


def _output_shape():
    def fwd():
        inp = _fwd_setup_inputs(0)
        return _fwd_reference(*[inp[k] for k in FWD_PARAMS])
    out = _jax.eval_shape(fwd)
    return out.shape, out.dtype

N_MICROBATCH = 1
ADAM_LR = 0.001
ADAM_B1 = 0.9
ADAM_B2 = 0.999
ADAM_EPS = 1e-08
ADAM_WD = 0.01
ADAM_STEP = 10
PER_EXAMPLE_BATCH_AXIS = {'x': 0, 'c': 0, 'ctx': 0, 'loss_target': 0}
SHARED_INPUTS = []
_WEIGHT_DTYPES = {'c_ctx': _jnp.float32, 'w_mod': _jnp.float32, 'b_mod': _jnp.float32, 'g_pre': _jnp.float32, 'g_post': _jnp.float32, 'w_in': _jnp.float32, 'w_conv': _jnp.float32, 'a_log': _jnp.float32, 'dt_bias': _jnp.float32, 'g_onorm': _jnp.float32, 'gm_ln_g': _jnp.float32, 'gm_ln_b': _jnp.float32, 'w_sp': _jnp.float32, 'b_sp': _jnp.float32, 'w_pa': _jnp.float32, 'w_pb': _jnp.float32, 'w_out': _jnp.float32}
MOMENT_SCALE = {'c_ctx': 3.926592e-03, 'w_mod': 1.248215e+00, 'b_mod': 2.746204e+00, 'g_pre': 1.529875e-01, 'g_post': 3.327065e+00, 'w_in': 5.069374e-02, 'w_conv': 5.522759e-02, 'a_log': 2.040492e-01, 'dt_bias': 1.973020e-01, 'g_onorm': 2.841126e-01, 'gm_ln_g': 3.847940e-02, 'gm_ln_b': 3.943519e-02, 'w_sp': 3.985324e-02, 'b_sp': 3.959796e-02, 'w_pa': 6.619598e-02, 'w_pb': 9.497846e-02, 'w_out': 1.204595e-01}


def _to_microbatches(a, axis):
    t = _jnp.moveaxis(a, axis, 0)
    t = t.reshape((N_MICROBATCH, t.shape[0] // N_MICROBATCH) + t.shape[1:])
    return _jnp.moveaxis(t, 1, axis + 1)


def setup_inputs(seed: int = 0) -> dict:
    inp = _fwd_setup_inputs(seed)
    key = _jax.random.fold_in(_jax.random.key(seed), 7919)
    shape, _ = _output_shape()
    out = dict(inp)
    out["loss_target"] = _jax.random.normal(_jax.random.fold_in(key, 0), shape, _jnp.float32)
    for i, name in enumerate(TWIN_WEIGHTS):
        w = inp[name].astype(_jnp.float32)
        if MOMENT_SCALE is None:
            s = _jnp.sqrt(_jnp.mean(_jnp.square(w)) + 1e-30)
        else:
            s = MOMENT_SCALE[name]
        km, kv = _jax.random.split(_jax.random.fold_in(key, i + 1))
        out[name] = w
        out["m_" + name] = s * _jax.random.normal(km, w.shape, _jnp.float32)
        out["v_" + name] = (s * s) * _jax.random.uniform(kv, w.shape, _jnp.float32, 0.5, 1.5)
    if N_MICROBATCH > 1:
        for name, axis in PER_EXAMPLE_BATCH_AXIS.items():
            out[name] = _to_microbatches(out[name], axis)
    return {'x': out['x'], 'c': out['c'], 'ctx': out['ctx'], 'c_ctx': out['c_ctx'], 'w_mod': out['w_mod'], 'b_mod': out['b_mod'], 'g_pre': out['g_pre'], 'g_post': out['g_post'], 'w_in': out['w_in'], 'w_conv': out['w_conv'], 'a_log': out['a_log'], 'dt_bias': out['dt_bias'], 'g_onorm': out['g_onorm'], 'gm_ln_g': out['gm_ln_g'], 'gm_ln_b': out['gm_ln_b'], 'w_sp': out['w_sp'], 'b_sp': out['b_sp'], 'w_pa': out['w_pa'], 'w_pb': out['w_pb'], 'w_out': out['w_out'], 'loss_target': out['loss_target'], 'm_c_ctx': out['m_c_ctx'], 'm_w_mod': out['m_w_mod'], 'm_b_mod': out['m_b_mod'], 'm_g_pre': out['m_g_pre'], 'm_g_post': out['m_g_post'], 'm_w_in': out['m_w_in'], 'm_w_conv': out['m_w_conv'], 'm_a_log': out['m_a_log'], 'm_dt_bias': out['m_dt_bias'], 'm_g_onorm': out['m_g_onorm'], 'm_gm_ln_g': out['m_gm_ln_g'], 'm_gm_ln_b': out['m_gm_ln_b'], 'm_w_sp': out['m_w_sp'], 'm_b_sp': out['m_b_sp'], 'm_w_pa': out['m_w_pa'], 'm_w_pb': out['m_w_pb'], 'm_w_out': out['m_w_out'], 'v_c_ctx': out['v_c_ctx'], 'v_w_mod': out['v_w_mod'], 'v_b_mod': out['v_b_mod'], 'v_g_pre': out['v_g_pre'], 'v_g_post': out['v_g_post'], 'v_w_in': out['v_w_in'], 'v_w_conv': out['v_w_conv'], 'v_a_log': out['v_a_log'], 'v_dt_bias': out['v_dt_bias'], 'v_g_onorm': out['v_g_onorm'], 'v_gm_ln_g': out['v_gm_ln_g'], 'v_gm_ln_b': out['v_gm_ln_b'], 'v_w_sp': out['v_w_sp'], 'v_b_sp': out['v_b_sp'], 'v_w_pa': out['v_w_pa'], 'v_w_pb': out['v_w_pb'], 'v_w_out': out['v_w_out']}


def _loss(weights, diff, rest, loss_target):
    with _jax.named_scope("forward"):
        args = {**rest, TWIN_DIFF_INPUT: diff, **{k: w.astype(_WEIGHT_DTYPES[k]) for k, w in weights.items()}}
        y = _forward(args)
    with _jax.named_scope("loss_head"):
        err = _jnp.square(y.astype(_jnp.float32) - loss_target)
        return 0.5 * _jnp.sum(_jnp.mean(err, axis=-1)) if err.ndim else 0.5 * err


def _adamw(w, g, m, v):
    m = ADAM_B1 * m + (1.0 - ADAM_B1) * g
    v = ADAM_B2 * v + (1.0 - ADAM_B2) * _jnp.square(g)
    m_hat = m / (1.0 - ADAM_B1 ** ADAM_STEP)
    v_hat = v / (1.0 - ADAM_B2 ** ADAM_STEP)
    delta = -ADAM_LR * (m_hat / (_jnp.sqrt(v_hat) + ADAM_EPS) + ADAM_WD * w)
    return delta, m, v


def reference(x, c, ctx, c_ctx, w_mod, b_mod, g_pre, g_post, w_in, w_conv, a_log, dt_bias, g_onorm, gm_ln_g, gm_ln_b, w_sp, b_sp, w_pa, w_pb, w_out, loss_target, m_c_ctx, m_w_mod, m_b_mod, m_g_pre, m_g_post, m_w_in, m_w_conv, m_a_log, m_dt_bias, m_g_onorm, m_gm_ln_g, m_gm_ln_b, m_w_sp, m_b_sp, m_w_pa, m_w_pb, m_w_out, v_c_ctx, v_w_mod, v_b_mod, v_g_pre, v_g_post, v_w_in, v_w_conv, v_a_log, v_dt_bias, v_g_onorm, v_gm_ln_g, v_gm_ln_b, v_w_sp, v_b_sp, v_w_pa, v_w_pb, v_w_out):
    given = dict(x=x, c=c, ctx=ctx, c_ctx=c_ctx, w_mod=w_mod, b_mod=b_mod, g_pre=g_pre, g_post=g_post, w_in=w_in, w_conv=w_conv, a_log=a_log, dt_bias=dt_bias, g_onorm=g_onorm, gm_ln_g=gm_ln_g, gm_ln_b=gm_ln_b, w_sp=w_sp, b_sp=b_sp, w_pa=w_pa, w_pb=w_pb, w_out=w_out, loss_target=loss_target, m_c_ctx=m_c_ctx, m_w_mod=m_w_mod, m_b_mod=m_b_mod, m_g_pre=m_g_pre, m_g_post=m_g_post, m_w_in=m_w_in, m_w_conv=m_w_conv, m_a_log=m_a_log, m_dt_bias=m_dt_bias, m_g_onorm=m_g_onorm, m_gm_ln_g=m_gm_ln_g, m_gm_ln_b=m_gm_ln_b, m_w_sp=m_w_sp, m_b_sp=m_b_sp, m_w_pa=m_w_pa, m_w_pb=m_w_pb, m_w_out=m_w_out, v_c_ctx=v_c_ctx, v_w_mod=v_w_mod, v_b_mod=v_b_mod, v_g_pre=v_g_pre, v_g_post=v_g_post, v_w_in=v_w_in, v_w_conv=v_w_conv, v_a_log=v_a_log, v_dt_bias=v_dt_bias, v_g_onorm=v_g_onorm, v_gm_ln_g=v_gm_ln_g, v_gm_ln_b=v_gm_ln_b, v_w_sp=v_w_sp, v_b_sp=v_b_sp, v_w_pa=v_w_pa, v_w_pb=v_w_pb, v_w_out=v_w_out)
    weights = {n: given[n] for n in TWIN_WEIGHTS}
    shared = {n: given[n] for n in SHARED_INPUTS}
    per_example = {n: given[n] for n in ['x', 'c', 'ctx']}
    grad_fn = _jax.value_and_grad(_loss, argnums=(0, 1))

    def one_microbatch(ex, loss_target):
        ex = dict(ex)
        diff = ex.pop(TWIN_DIFF_INPUT)
        return grad_fn(weights, diff, {**shared, **ex}, loss_target)

    if N_MICROBATCH == 1:
        loss, (grad_w, grad_x) = one_microbatch(per_example, given["loss_target"])
    else:
        def body(carry, xs):
            loss_sum, grad_sum = carry
            l_k, (gw_k, gx_k) = one_microbatch(xs[0], xs[1])
            with _jax.named_scope("update"):
                return (loss_sum + l_k, _jax.tree.map(_jnp.add, grad_sum, gw_k)), gx_k

        init = (_jnp.zeros((), _jnp.float32), _jax.tree.map(_jnp.zeros_like, weights))
        (loss, grad_w), grad_x = _jax.lax.scan(body, init, (per_example, given["loss_target"]))
    with _jax.named_scope("update"):
        delta_w, new_m, new_v = {}, {}, {}
        for n in TWIN_WEIGHTS:
            delta_w[n], new_m[n], new_v[n] = _adamw(weights[n], grad_w[n], given["m_" + n], given["v_" + n])
    return (loss, grad_x, *[grad_w[n] for n in TWIN_WEIGHTS], *[delta_w[n] for n in TWIN_WEIGHTS],
            *[new_m[n] for n in TWIN_WEIGHTS], *[new_v[n] for n in TWIN_WEIGHTS])
```

```python
import functools

import jax
import jax.numpy as jnp
from jax import lax
from jax.experimental import pallas as pl
from jax.experimental.pallas import tpu as pltpu

F32 = jnp.float32
_BF = jnp.bfloat16
_HI = lax.Precision.HIGHEST
D = 1024
NH = 8
DH = 128
CH = 64
GC = 128
NREST = 6 * D
NMAIN = NREST + 3 * D
EPS = 1e-6
LANE = 128
NDEV = 8
VMEM_LIMIT = 56 * 1024 * 1024
MESH = pl.DeviceIdType.MESH

ADAM_LR, ADAM_B1, ADAM_B2, ADAM_EPS, ADAM_WD, ADAM_STEP = 0.001, 0.9, 0.999, 1e-08, 0.01, 10

NN = ((1,), (0,))
NT = ((1,), (1,))
TN = ((0,), (0,))


def _dot(a, b, dims=NN, prec=None):
    return lax.dot_general(a, b, (dims, ((), ())), precision=prec, preferred_element_type=F32)


def _mm(a, b, dims=NN):
    return _dot(a.astype(_BF), b.astype(_BF), dims)


def _mmh(a, b, dims=NN):
    return _dot(a.astype(F32), b.astype(F32), dims, _HI)


def _sigmoid(x):
    return 1.0 / (1.0 + jnp.exp(-x))


def _silu_g(x):
    s = _sigmoid(x)
    return x * s, s * (1.0 + x * (1.0 - s))


def _gelu_g(x):
    c = 0.7978845608028654
    t = jnp.tanh(c * (x + 0.044715 * (x * x * x)))
    cdf = 0.5 * (1.0 + t)
    return x * cdf, cdf + 0.5 * x * (1.0 - t * t) * c * (1.0 + 3 * 0.044715 * x * x)


def _softplus(x):
    return jnp.maximum(x, 0.0) + jnp.log(1.0 + jnp.exp(-jnp.abs(x)))


def _params(sem=None):
    return pltpu.CompilerParams(dimension_semantics=sem, vmem_limit_bytes=VMEM_LIMIT)


def _tile(n, pref):
    for t in pref:
        if n % t == 0:
            return t
    return n


def _full(shape):
    nd = len(shape)
    return pl.BlockSpec(shape, lambda *_: (0,) * nd)


def _sds(shape, dt=F32):
    return jax.ShapeDtypeStruct(shape, dt)


def _exchange(arrays, scatter, name):
    n = len(arrays)
    out_shape = tuple(
        _sds(a.shape if scatter else (NDEV,) + a.shape, a.dtype) for a in arrays)

    def body(*refs):
        ins, outs = refs[:n], refs[n:2 * n]
        send_sems, recv_sems, loc_sems = refs[2 * n:]
        x, y, c = lax.axis_index("x"), lax.axis_index("y"), lax.axis_index("c")
        me = 4 * x + 2 * y + c

        def src(a, idx):
            return ins[a].at[idx] if scatter else ins[a]

        local = [pltpu.make_async_copy(src(a, me), outs[a].at[me], loc_sems.at[a]) for a in range(n)]
        for cp in local:
            cp.start()
        copies = []
        for k in range(1, NDEV):
            px = 1 - x if (k >> 2) & 1 else x
            py = 1 - y if (k >> 1) & 1 else y
            pc = 1 - c if k & 1 else c
            pidx = 4 * px + 2 * py + pc
            for a in range(n):
                send = pltpu.make_async_remote_copy(
                    src_ref=src(a, pidx), dst_ref=outs[a].at[me], send_sem=send_sems.at[a, k - 1],
                    recv_sem=recv_sems.at[a, k - 1], device_id=(px, py, pc), device_id_type=MESH)
                send.start()
                arrive = pltpu.make_async_remote_copy(
                    src_ref=src(a, pidx), dst_ref=outs[a].at[pidx], send_sem=send_sems.at[a, k - 1],
                    recv_sem=recv_sems.at[a, k - 1], device_id=(px, py, pc), device_id_type=MESH)
                copies.append(arrive)
        for cp in copies:
            cp.wait()
        for cp in local:
            cp.wait()

    any_spec = pl.BlockSpec(memory_space=pl.ANY)
    return pl.pallas_call(
        body, name=name, out_shape=out_shape,
        in_specs=[any_spec] * n, out_specs=tuple([any_spec] * n),
        scratch_shapes=[pltpu.SemaphoreType.DMA((n, NDEV - 1)), pltpu.SemaphoreType.DMA((n, NDEV - 1)),
                        pltpu.SemaphoreType.DMA((n,))],
    )(*arrays)


def _matmul_nn(a, b, name):
    m, kk = a.shape
    n = b.shape[1]
    tm = _tile(m, (1088, 1024, 640, 512, 256, 128))
    tn = _tile(n, (512, 256, 128))

    def body(a_ref, b_ref, o_ref):
        o_ref[...] = _mm(a_ref[...], b_ref[...])

    return pl.pallas_call(
        body, name=name, out_shape=_sds((m, n)), grid=(n // tn, m // tm),
        in_specs=[pl.BlockSpec((tm, kk), lambda j, i: (i, 0)), pl.BlockSpec((kk, tn), lambda j, i: (0, j))],
        out_specs=pl.BlockSpec((tm, tn), lambda j, i: (i, j)),
        compiler_params=_params(("parallel", "parallel")),
    )(a, b)


def _matmul_tn(a, b, name):
    kk, m = a.shape
    n = b.shape[1]
    tk = _tile(kk, (1088, 1024, 640, 512, 256, 128))
    tn = _tile(n, (1024, 512, 256, 128))
    nk = kk // tk

    def body(a_ref, b_ref, o_ref):
        @pl.when(pl.program_id(1) == 0)
        def _():
            o_ref[...] = jnp.zeros_like(o_ref)

        o_ref[...] += _mm(a_ref[...], b_ref[...], TN)

    return pl.pallas_call(
        body, name=name, out_shape=_sds((m, n)), grid=(n // tn, nk),
        in_specs=[pl.BlockSpec((tk, m), lambda j, k: (k, 0)), pl.BlockSpec((tk, tn), lambda j, k: (k, j))],
        out_specs=pl.BlockSpec((m, tn), lambda j, k: (0, j)),
        compiler_params=_params(("parallel", "arbitrary")),
    )(a, b)


def _dh_matmul(dp, dpab, w_main, w_ab):
    lt = dp.shape[0]
    tm = _tile(lt, (1088, 1024, 640, 512, 256, 128))
    nk = NMAIN // D

    def body(dp_ref, ab_ref, w_ref, wab_ref, o_ref):
        @pl.when(pl.program_id(1) == 0)
        def _():
            o_ref[...] = _mm(ab_ref[...], wab_ref[...], NT)

        o_ref[...] += _mm(dp_ref[...], w_ref[...], NT)

    return pl.pallas_call(
        body, name="dh_matmul", out_shape=_sds((lt, D)), grid=(lt // tm, nk),
        in_specs=[pl.BlockSpec((tm, D), lambda i, k: (i, k)), pl.BlockSpec((tm, LANE), lambda i, k: (i, 0)),
                  pl.BlockSpec((D, D), lambda i, k: (0, k)), _full((D, LANE))],
        out_specs=pl.BlockSpec((tm, D), lambda i, k: (i, 0)),
        compiler_params=_params(("parallel", "arbitrary")),
    )(dp, dpab, w_main, w_ab)


def _modulation(cc, w_mod_g, b_mod):
    ws = w_mod_g.shape[2]

    def body(c_ref, w_ref, b_ref, o_ref):
        s, _ = _silu_g(c_ref[...])
        o_ref[...] = _mm(s, w_ref[0]) + b_ref[...]

    return pl.pallas_call(
        body, name="modulation", out_shape=_sds((8, 3 * D)), grid=(NDEV,),
        in_specs=[_full((8, D)), pl.BlockSpec((1, D, ws), lambda j: (j, 0, 0)), pl.BlockSpec((1, ws), lambda j: (0, j))],
        out_specs=pl.BlockSpec((8, ws), lambda j: (0, j)),
        compiler_params=_params(("parallel",)),
    )(cc, w_mod_g, b_mod)


def _prenorm(xa, mods, g_pre, lc):
    lt = xa.shape[0]
    tm = _tile(lc, (256, 128))
    nct = lc // tm

    def body(x_ref, m_ref, g_ref, o_ref):
        x = x_ref[...]
        is_ctx = pl.program_id(0) < nct
        shift = jnp.where(is_ctx, m_ref[1:2, 0:D], m_ref[0:1, 0:D])
        scale = jnp.where(is_ctx, m_ref[1:2, D:2 * D], m_ref[0:1, D:2 * D])
        r = lax.rsqrt(jnp.mean(x * x, axis=-1, keepdims=True) + EPS)
        o_ref[...] = ((x * r * g_ref[...]) * (1.0 + scale) + shift).astype(o_ref.dtype)

    return pl.pallas_call(
        body, name="prenorm", out_shape=_sds((lt, D), _BF), grid=(lt // tm,),
        in_specs=[pl.BlockSpec((tm, D), lambda i: (i, 0)), _full((8, 3 * D)), _full((1, D))],
        out_specs=pl.BlockSpec((tm, D), lambda i: (i, 0)),
        compiler_params=_params(("parallel",)),
    )(xa, mods, g_pre)


def _prenorm_bwd(xa, dh, dy, mods, g_pre, lc):
    lt = xa.shape[0]
    tm = _tile(lc, (256, 128))
    nct = lc // tm
    nl = (lt - lc) // tm

    def body(x_ref, dh_ref, dy_ref, m_ref, g_ref, gx_ref, vec_ref):
        i = pl.program_id(0)

        @pl.when(i == 0)
        def _():
            vec_ref[...] = jnp.zeros_like(vec_ref)

        x = x_ref[...]
        dh = dh_ref[...]
        g = g_ref[...]
        is_ctx = i < nct
        scale = jnp.where(is_ctx, m_ref[1:2, D:2 * D], m_ref[0:1, D:2 * D])
        r = lax.rsqrt(jnp.mean(x * x, axis=-1, keepdims=True) + EPS)
        n = x * r
        hn = n * g
        dsh = jnp.sum(dh, axis=0, keepdims=True)
        dsc = jnp.sum(dh * hn, axis=0, keepdims=True)
        dhn = dh * (1.0 + scale)
        vec_ref[4:5, :] += jnp.sum(dhn * n, axis=0, keepdims=True)
        dn = dhn * g
        dx = r * (dn - n * jnp.mean(dn * n, axis=-1, keepdims=True))

        @pl.when(is_ctx)
        def _():
            vec_ref[2:3, :] += dsh
            vec_ref[3:4, :] += dsc

        @pl.when(jnp.logical_not(is_ctx))
        def _():
            vec_ref[0:1, :] += dsh
            vec_ref[1:2, :] += dsc
            gx_ref[...] = dy_ref[...] + dx

    xrow = lambda i: (jnp.maximum(i - nct, 0), 0)
    return pl.pallas_call(
        body, name="prenorm_bwd", out_shape=(_sds((nl * tm, D)), _sds((8, D))), grid=(lt // tm,),
        in_specs=[pl.BlockSpec((tm, D), lambda i: (i, 0)), pl.BlockSpec((tm, D), lambda i: (i, 0)),
                  pl.BlockSpec((tm, D), xrow), _full((8, 3 * D)), _full((1, D))],
        out_specs=(pl.BlockSpec((tm, D), xrow), _full((8, D))),
        compiler_params=_params(("arbitrary",)),
    )(xa, dh, dy, mods, g_pre)


def _conv_parts(x, w, lc):
    lt = x.shape[0]
    row = lax.broadcasted_iota(jnp.int32, x.shape, 0)
    first = (row == 0) | (row == lc)
    last = (row == lc - 1) | (row == lt - 1)
    xp = jnp.where(first, 0.0, pltpu.roll(x, 1, 0))
    xn = jnp.where(last, 0.0, pltpu.roll(x, lt - 1, 0))
    y = w[0:1, :] * xp + w[1:2, :] * x + w[2:3, :] * xn
    return xp, xn, y, first, last


def _qkv_fwd(p, w_conv, lc):
    lt = p.shape[0]
    cb0 = NREST // DH

    def body(p_ref, w_ref, o_ref):
        _, _, y, _, _ = _conv_parts(p_ref[...], w_ref[...], lc)
        s, _ = _silu_g(y)
        rs = lax.rsqrt(jnp.sum(s * s, axis=-1, keepdims=True) + EPS)
        o_ref[...] = s * jnp.where(pl.program_id(0) < 2 * NH, rs, 1.0)

    return pl.pallas_call(
        body, name="qkv_fwd", out_shape=_sds((lt, 3 * D)), grid=(3 * NH,),
        in_specs=[pl.BlockSpec((lt, DH), lambda j: (0, cb0 + j)), pl.BlockSpec((3, DH), lambda j: (0, j))],
        out_specs=pl.BlockSpec((lt, DH), lambda j: (0, j)),
        compiler_params=_params(("parallel",)),
    )(p, w_conv)


def _qkv_bwd(p, w_conv, dqkv_f, dqkv_b, dp, lc):
    lt = p.shape[0]
    cb0 = NREST // DH

    def body(p_ref, w_ref, df_ref, db_ref, dp_in, dp_ref, dw_ref):
        del dp_in
        w = w_ref[...]
        xp, xn, y, first, last = _conv_parts(p_ref[...], w, lc)
        s, ds_dy = _silu_g(y)
        dn = df_ref[...] + db_ref[...]
        rs = lax.rsqrt(jnp.sum(s * s, axis=-1, keepdims=True) + EPS)
        nrm = s * rs
        ds_n = rs * (dn - nrm * jnp.sum(dn * nrm, axis=-1, keepdims=True))
        ds = jnp.where(pl.program_id(0) < 2 * NH, ds_n, dn)
        dy = ds * ds_dy
        dw_ref[0:1, :] = jnp.sum(dy * xp, axis=0, keepdims=True)
        dw_ref[1:2, :] = jnp.sum(dy * p_ref[...], axis=0, keepdims=True)
        dw_ref[2:3, :] = jnp.sum(dy * xn, axis=0, keepdims=True)
        dyn = jnp.where(last, 0.0, pltpu.roll(dy, lt - 1, 0))
        dyp = jnp.where(first, 0.0, pltpu.roll(dy, 1, 0))
        dp_ref[...] = (w[1:2, :] * dy + w[0:1, :] * dyn + w[2:3, :] * dyp).astype(dp_ref.dtype)

    return pl.pallas_call(
        body, name="qkv_bwd", out_shape=(_sds(dp.shape, dp.dtype), _sds((3, 3 * D))), grid=(3 * NH,),
        in_specs=[pl.BlockSpec((lt, DH), lambda j: (0, cb0 + j)), pl.BlockSpec((3, DH), lambda j: (0, j)),
                  pl.BlockSpec((lt, DH), lambda j: (0, j)), pl.BlockSpec((lt, DH), lambda j: (0, j)),
                  pl.BlockSpec(memory_space=pl.ANY)],
        out_specs=(pl.BlockSpec((lt, DH), lambda j: (0, cb0 + j)), pl.BlockSpec((3, DH), lambda j: (0, j))),
        input_output_aliases={4: 0},
        compiler_params=_params(("parallel",)),
    )(p, w_conv, dqkv_f, dqkv_b, dp)


def _masks(d):
    ri = lax.broadcasted_iota(jnp.int32, (CH, CH), 0)
    ci = lax.broadcasted_iota(jnp.int32, (CH, CH), 1)
    incl = (ri >= ci) if d == 0 else (ri <= ci)
    strict = (ri > ci) if d == 0 else (ri < ci)
    incl_t = (ri <= ci) if d == 0 else (ri >= ci)
    return incl, strict, incl_t, ri == ci


def _decays(d, ab, abt, alog_r, dtb_r, alog_c, dtb_c, incl, incl_t):
    g_full = -jnp.exp(alog_r) * _softplus(ab + dtb_r)
    beta_full = _sigmoid(ab)
    gc_full = _mmh(incl.astype(F32), g_full)
    gl_full = jnp.sum(g_full, axis=0, keepdims=True)
    gt_full = -jnp.exp(alog_c) * _softplus(abt + dtb_c)
    gct = _mmh(gt_full, incl_t.astype(F32))
    return g_full, beta_full, gc_full, gl_full, gt_full, gct


def _lane_onehot(idx, n=LANE):
    return (lax.broadcasted_iota(jnp.int32, (1, n), 1) == idx).astype(F32)


def _head_scalars(d, h, beta_full, gc_full, gl_full, gct):
    idx = d * NH + h
    oh = _lane_onehot(idx)
    gcol = jnp.sum(gc_full * oh, axis=-1, keepdims=True)
    bcol = jnp.sum(beta_full * _lane_onehot(2 * NH + idx), axis=-1, keepdims=True)
    gl = jnp.sum(gl_full * oh, axis=-1, keepdims=True)
    grow = gct[idx:idx + 1, :]
    return gcol, grow, bcol, gl


def _solve_inverse(a, eye):
    n = -a
    t = jnp.where(eye, 1.0, 0.0) + n
    p = n
    for _ in range(5):
        p = _mmh(p, p)
        t = t + _mmh(t, p)
    return t


def _chunk_local(qh, kh, vh, gcol, grow, bcol, gl, incl, strict):
    decay = jnp.where(incl, jnp.exp(gcol - grow), 0.0)
    kb = kh * bcol
    a = jnp.where(strict, _mm(kb, kh, NT) * decay, 0.0)
    egc = jnp.exp(gcol)
    rhs_u = vh * bcol
    rhs_w = kb * egc
    qs = qh * (DH ** -0.5)
    attn = jnp.where(incl, _mm(qs, kh, NT) * decay, 0.0)
    etail = jnp.exp(gl - gcol)
    return decay, kb, a, egc, rhs_u, rhs_w, qs, attn, etail


def _scan_specs(lt, lc, bwd_pass):
    nch = lt // CH
    ncc = lc // CH
    if not bwd_pass:
        cf = lambda s: s
        cb = lambda s: jnp.where(s < ncc, ncc - 1 - s, nch + ncc - 1 - s)
    else:
        cf = lambda s: nch - 1 - s
        cb = lambda s: jnp.where(s < nch - ncc, ncc + s, s - (nch - ncc))
    return nch, cf, cb


def _gdn_fwd(qkv, pab, abt, alog_r, dtb_r, alog_c, dtb_c, lc):
    lt = qkv.shape[0]
    nch, cf, cb = _scan_specs(lt, lc, False)

    def body(qf, kf, vf, abf, abtf, qb, kb_, vb, abb, abtb, ar, dr, ac, dc,
             of_ref, ob_ref, sf_ref, sb_ref, tf_ref, tb_ref, s_scr):
        @pl.when(pl.program_id(0) == 0)
        def _():
            s_scr[...] = jnp.zeros_like(s_scr)

        for d, (q_r, k_r, v_r, ab_r, abt_r, o_ref, sh_ref, th_ref) in enumerate(
                ((qf, kf, vf, abf, abtf, of_ref, sf_ref, tf_ref), (qb, kb_, vb, abb, abtb, ob_ref, sb_ref, tb_ref))):
            incl, strict, incl_t, eye = _masks(d)
            _, beta_full, gc_full, gl_full, _, gct = _decays(
                d, ab_r[...], abt_r[0], ar[...], dr[...], ac[...], dc[...], incl, incl_t)
            for h in range(NH):
                sl = slice(h * DH, (h + 1) * DH)
                qh, kh, vh = q_r[:, sl], k_r[:, sl], v_r[:, sl]
                gcol, grow, bcol, gl = _head_scalars(d, h, beta_full, gc_full, gl_full, gct)
                _, _, a, egc, rhs_u, rhs_w, qs, attn, etail = _chunk_local(qh, kh, vh, gcol, grow, bcol, gl, incl, strict)
                t = _solve_inverse(a, eye)
                u = _mmh(t, rhs_u)
                w = _mmh(t, rhs_w)
                s = s_scr[d, h]
                v_new = u - _mm(w, s)
                o_ref[:, sl] = _mm(qs * egc, s) + _mm(attn, v_new)
                sh_ref[0, h] = s
                th_ref[0, h] = t
                s_scr[d, h] = s * jnp.exp(gl) + _mm(kh * etail, v_new, TN)

    def row(c, col):
        return pl.BlockSpec((CH, D), lambda s: (c(s), col))

    def chunk_in(c):
        return [row(c, 0), row(c, 1), row(c, 2), pl.BlockSpec((CH, LANE), lambda s: (c(s), 0)),
                pl.BlockSpec((1, 4 * NH, CH), lambda s: (c(s), 0, 0))]

    def hist(c, n):
        return pl.BlockSpec((1, NH, n, n), lambda s: (c(s), 0, 0, 0))

    small = [_full((1, LANE)), _full((1, LANE)), _full((4 * NH, 1)), _full((4 * NH, 1))]
    return pl.pallas_call(
        body, name="gdn_fwd", grid=(nch,),
        out_shape=(_sds((lt, D)), _sds((lt, D)), _sds((nch, NH, DH, DH)), _sds((nch, NH, DH, DH)),
                   _sds((nch, NH, CH, CH)), _sds((nch, NH, CH, CH))),
        in_specs=chunk_in(cf) + chunk_in(cb) + small,
        out_specs=(pl.BlockSpec((CH, D), lambda s: (cf(s), 0)), pl.BlockSpec((CH, D), lambda s: (cb(s), 0)),
                   hist(cf, DH), hist(cb, DH), hist(cf, CH), hist(cb, CH)),
        scratch_shapes=[pltpu.VMEM((2, NH, DH, DH), F32)],
        compiler_params=_params(("arbitrary",)),
    )(qkv, qkv, qkv, pab, abt, qkv, qkv, qkv, pab, abt, alog_r, dtb_r, alog_c, dtb_c)


def _gdn_bwd(qkv, pab, abt, alog_r, dtb_r, alog_c, dtb_c, s_f, s_b, t_f, t_b, do, lc):
    lt = qkv.shape[0]
    nch, cf, cb = _scan_specs(lt, lc, True)

    def body(qf, kf, vf, abf, abtf, sf_ref, tf_ref, dof, qb, kb_, vb, abb, abtb, sb_ref, tb_ref, dob, ar, dr, ac, dc,
             dqf_ref, dqb_ref, dcf_ref, dcb_ref, drf_ref, drb_ref, vcol_ref, vrow_ref, ds_scr):
        @pl.when(pl.program_id(0) == 0)
        def _():
            ds_scr[...] = jnp.zeros_like(ds_scr)
            vcol_ref[...] = jnp.zeros_like(vcol_ref)
            vrow_ref[...] = jnp.zeros_like(vrow_ref)

        alog_r_, dtb_r_, alog_c_, dtb_c_ = ar[...], dr[...], ac[...], dc[...]
        lane2 = lax.broadcasted_iota(jnp.int32, (1, LANE), 1)
        for d, (q_r, k_r, v_r, ab_r, abt_r, sh_ref, th_ref, do_r, dq_ref, dcol_ref, drow_ref) in enumerate(
                ((qf, kf, vf, abf, abtf, sf_ref, tf_ref, dof, dqf_ref, dcf_ref, drf_ref),
                 (qb, kb_, vb, abb, abtb, sb_ref, tb_ref, dob, dqb_ref, dcb_ref, drb_ref))):
            incl, strict, incl_t, eye = _masks(d)
            ab, abt = ab_r[...], abt_r[0]
            g_full, beta_full, gc_full, gl_full, gt_full, gct = _decays(
                d, ab, abt, alog_r_, dtb_r_, alog_c_, dtb_c_, incl, incl_t)
            dgc_col = jnp.zeros((CH, LANE), F32)
            dgl_row = jnp.zeros((1, LANE), F32)
            dbeta_col = jnp.zeros((CH, LANE), F32)
            dgc_row = jnp.zeros((4 * NH, CH), F32)
            for h in range(NH):
                idx = d * NH + h
                sl = slice(h * DH, (h + 1) * DH)
                qh, kh, vh = q_r[:, sl], k_r[:, sl], v_r[:, sl]
                doh = do_r[:, sl]
                gcol, grow, bcol, gl = _head_scalars(d, h, beta_full, gc_full, gl_full, gct)
                decay, kb, a, egc, rhs_u, rhs_w, qs, attn, etail = _chunk_local(qh, kh, vh, gcol, grow, bcol, gl, incl, strict)
                t = th_ref[0, h]
                s = sh_ref[0, h]
                ds_new = ds_scr[d, h]
                u = _mmh(t, rhs_u)
                w = _mmh(t, rhs_w)
                v_new = u - _mm(w, s)
                q_dec = qs * egc
                k_tail = kh * etail
                egl = jnp.exp(gl)
                dv_new = _mm(attn, doh, TN) + _mm(k_tail, ds_new)
                dattn = jnp.where(incl, _mm(doh, v_new, NT), 0.0)
                dq_dec = _mm(doh, s, NT)
                dk_tail = _mm(v_new, ds_new, NT)
                dgl = jnp.sum(jnp.sum(ds_new * s, axis=0, keepdims=True), axis=-1, keepdims=True) * egl
                ds_scr[d, h] = ds_new * egl + _mm(q_dec, doh, TN) - _mm(w, dv_new, TN)
                dw = -_mm(dv_new, s, NT)
                dr_u = _mmh(t, dv_new, TN)
                dr_w = _mmh(t, dw, TN)
                da = -jnp.where(strict, _mmh(dr_u, u, NT) + _mmh(dr_w, w, NT), 0.0)
                dv = dr_u * bcol
                dbeta = jnp.sum(dr_u * vh, axis=-1, keepdims=True)
                dkb = dr_w * egc
                dgc = jnp.sum(dr_w * rhs_w, axis=-1, keepdims=True)
                m = da * decay
                dkb = dkb + _mm(m, kh)
                dk = _mm(m, kb, TN)
                pa = da * a
                nq = dattn * decay
                dqs = _mm(nq, kh) + dq_dec * egc
                dk = dk + _mm(nq, qs, TN)
                pq = pa + dattn * attn
                dgc = dgc + jnp.sum(pq, axis=-1, keepdims=True) + jnp.sum(dq_dec * q_dec, axis=-1, keepdims=True)
                dgr = -jnp.sum(pq, axis=0, keepdims=True)
                tt = jnp.sum(dk_tail * k_tail, axis=-1, keepdims=True)
                dk = dk + dk_tail * etail + dkb * bcol
                dgc = dgc - tt
                dgl = dgl + jnp.sum(tt, axis=0, keepdims=True)
                dbeta = dbeta + jnp.sum(dkb * kh, axis=-1, keepdims=True)
                dq_ref[:, sl] = dqs * (DH ** -0.5)
                dq_ref[:, D + h * DH:D + (h + 1) * DH] = dk
                dq_ref[:, 2 * D + h * DH:2 * D + (h + 1) * DH] = dv
                oh = _lane_onehot(idx)
                dgc_col = dgc_col + dgc * oh
                dgl_row = dgl_row + dgl * oh
                dbeta_col = dbeta_col + dbeta * _lane_onehot(2 * NH + idx)
                ohc = (lax.broadcasted_iota(jnp.int32, (4 * NH, 1), 0) == idx).astype(F32)
                dgc_row = dgc_row + ohc * dgr
            dg_col = _mmh(incl_t.astype(F32), dgc_col) + dgl_row
            dg_row = _mmh(dgc_row, incl.astype(F32))
            sg_col = _sigmoid(ab + dtb_r_)
            da_col = dg_col * (-jnp.exp(alog_r_)) * sg_col
            dcol_ref[...] = da_col + dbeta_col * beta_full * (1.0 - beta_full)
            da_row = dg_row * (-jnp.exp(alog_c_)) * _sigmoid(abt + dtb_c_)
            drow_ref[0] = da_row
            vcol_ref[0:1, :] += jnp.sum(dg_col * g_full, axis=0, keepdims=True)
            vcol_ref[1:2, :] += jnp.sum(da_col, axis=0, keepdims=True)
            rl = jnp.sum(dg_row * gt_full, axis=-1, keepdims=True)
            rd = jnp.sum(da_row, axis=-1, keepdims=True)
            vrow_ref[...] += jnp.where(lane2 == 0, rl, 0.0) + jnp.where(lane2 == 1, rd, 0.0)

    def row(c, col):
        return pl.BlockSpec((CH, D), lambda s: (c(s), col))

    def hist(c, n):
        return pl.BlockSpec((1, NH, n, n), lambda s: (c(s), 0, 0, 0))

    def chunk_in(c):
        return [row(c, 0), row(c, 1), row(c, 2), pl.BlockSpec((CH, LANE), lambda s: (c(s), 0)),
                pl.BlockSpec((1, 4 * NH, CH), lambda s: (c(s), 0, 0)), hist(c, DH), hist(c, CH), row(c, 0)]

    small = [_full((1, LANE)), _full((1, LANE)), _full((4 * NH, 1)), _full((4 * NH, 1))]
    return pl.pallas_call(
        body, name="gdn_bwd", grid=(nch,),
        out_shape=(_sds((lt, 3 * D)), _sds((lt, 3 * D)), _sds((lt, LANE)), _sds((lt, LANE)),
                   _sds((nch, 4 * NH, CH)), _sds((nch, 4 * NH, CH)), _sds((8, LANE)), _sds((4 * NH, LANE))),
        in_specs=chunk_in(cf) + chunk_in(cb) + small,
        out_specs=(pl.BlockSpec((CH, 3 * D), lambda s: (cf(s), 0)), pl.BlockSpec((CH, 3 * D), lambda s: (cb(s), 0)),
                   pl.BlockSpec((CH, LANE), lambda s: (cf(s), 0)), pl.BlockSpec((CH, LANE), lambda s: (cb(s), 0)),
                   pl.BlockSpec((1, 4 * NH, CH), lambda s: (cf(s), 0, 0)), pl.BlockSpec((1, 4 * NH, CH), lambda s: (cb(s), 0, 0)),
                   _full((8, LANE)), _full((4 * NH, LANE))),
        scratch_shapes=[pltpu.VMEM((2, NH, DH, DH), F32)],
        compiler_params=_params(("arbitrary",)),
    )(qkv, qkv, qkv, pab, abt, s_f, t_f, do, qkv, qkv, qkv, pab, abt, s_b, t_b, do, alog_r, dtb_r, alog_c, dtb_c)


def _post(p, o_f, o_b, x, tgt, w_pa, w_pb, w_out, w_sp, w_spt, b_spb, ln_g, ln_b, g_on, g_post, gate_x, lc):
    lt = p.shape[0]
    l = x.shape[0]
    tm = GC
    nct = lc // tm

    def body(p_ref, of_ref, ob_ref, x_ref, t_ref, wpa, wpb, wout, wsp, wspt, bspb, lng_ref, lnb_ref, gon_ref, gpost_ref, gate_ref,
             dp_ref, do_ref, dy_ref, ya_ref, yb_ref, mg_ref, da_ref, db_ref, dout_ref, dwsp_ref, dbsp_ref, vec_ref):
        i = pl.program_id(0)

        @pl.when(i == 0)
        def _():
            dwsp_ref[...] = jnp.zeros_like(dwsp_ref)
            dbsp_ref[...] = jnp.zeros_like(dbsp_ref)
            vec_ref[...] = jnp.zeros_like(vec_ref)

        @pl.when(i < nct)
        def _():
            dp_ref[...] = jnp.zeros_like(dp_ref)
            do_ref[...] = jnp.zeros_like(do_ref)

        @pl.when(i >= nct)
        def _():
            lng, lnb, gon, gpost, gate = lng_ref[...], lnb_ref[...], gon_ref[...], gpost_ref[...], gate_ref[...]
            zb, ua, va, za, ga, gb = [p_ref[:, j * D:(j + 1) * D] for j in range(6)]
            o = of_ref[...] + ob_ref[...]
            szb, dszb = _silu_g(zb)
            nh_l, r_l = [], []
            for h in range(NH):
                oh = o[:, h * DH:(h + 1) * DH]
                r = lax.rsqrt(jnp.mean(oh * oh, axis=-1, keepdims=True) + EPS)
                nh_l.append(oh * r)
                r_l.append(r)
            nrm_b = jnp.concatenate(nh_l, axis=-1)
            gon_t = jnp.concatenate([gon] * NH, axis=-1)
            y_b = nrm_b * gon_t * szb
            u, du_dua = _gelu_g(ua)
            gv, dgv_dva = _gelu_g(va)
            xc = gv - jnp.mean(gv, axis=-1, keepdims=True)
            rs_ln = lax.rsqrt(jnp.mean(xc * xc, axis=-1, keepdims=True) + EPS)
            vhat = xc * rs_ln
            v = vhat * lng + lnb
            s_sp = jnp.concatenate(
                [_mm(wsp[g], v[:, g * DH:(g + 1) * DH]) + bspb[g] for g in range(NH)], axis=-1)
            sza, dsza = _silu_g(za)
            y_a = u * s_sp * sza
            a_pr = _mm(y_a, wpa[...])
            b_pr = _mm(y_b, wpb[...])
            sga = _sigmoid(ga)
            sgb = _sigmoid(gb)
            merged = sga * a_pr + sgb * b_pr
            out = _mm(merged, wout[...])
            rs_o = lax.rsqrt(jnp.mean(out * out, axis=-1, keepdims=True) + EPS)
            n_o = out * rs_o
            rr = n_o * gpost
            diff = x_ref[...] + gate * rr - t_ref[...]
            vec_ref[5:6, :] += jnp.sum(diff * diff, axis=0, keepdims=True)
            dy = diff * (1.0 / D)
            dy_ref[...] = dy
            vec_ref[0:1, :] += jnp.sum(dy * rr, axis=0, keepdims=True)
            dr = dy * gate
            vec_ref[1:2, :] += jnp.sum(dr * n_o, axis=0, keepdims=True)
            dn_o = dr * gpost
            dout = rs_o * (dn_o - n_o * jnp.mean(dn_o * n_o, axis=-1, keepdims=True))
            dmerged = _mm(dout, wout[...], NT)
            d_a = dmerged * sga
            d_b = dmerged * sgb
            dga = dmerged * a_pr * sga * (1.0 - sga)
            dgb = dmerged * b_pr * sgb * (1.0 - sgb)
            dy_a = _mm(d_a, wpa[...], NT)
            dy_b = _mm(d_b, wpb[...], NT)
            ya_ref[...] = y_a.astype(ya_ref.dtype)
            yb_ref[...] = y_b.astype(yb_ref.dtype)
            mg_ref[...] = merged.astype(mg_ref.dtype)
            da_ref[...] = d_a.astype(da_ref.dtype)
            db_ref[...] = d_b.astype(db_ref.dtype)
            dout_ref[...] = dout.astype(dout_ref.dtype)
            dua = dy_a * s_sp * sza * du_dua
            ds_sp = dy_a * u * sza
            dza = dy_a * u * s_sp * dsza
            dv_l = []
            for g in range(NH):
                ds_g = ds_sp[:, g * DH:(g + 1) * DH]
                dv_l.append(_mm(wspt[g], ds_g))
                dwsp_ref[g] += _mm(ds_g, v[:, g * DH:(g + 1) * DH], NT)
                dbsp_ref[g] += ds_g
            dv = jnp.concatenate(dv_l, axis=-1)
            vec_ref[2:3, :] += jnp.sum(dv * vhat, axis=0, keepdims=True)
            vec_ref[3:4, :] += jnp.sum(dv, axis=0, keepdims=True)
            dvh = dv * lng
            dgv = rs_ln * (dvh - jnp.mean(dvh, axis=-1, keepdims=True) - vhat * jnp.mean(dvh * vhat, axis=-1, keepdims=True))
            dva = dgv * dgv_dva
            dzb = dy_b * nrm_b * gon_t * dszb
            dgon_full = jnp.sum(dy_b * nrm_b * szb, axis=0, keepdims=True)
            dgon = dgon_full[:, 0:DH]
            for h in range(1, NH):
                dgon = dgon + dgon_full[:, h * DH:(h + 1) * DH]
            vec_ref[4:5, 0:DH] += dgon
            dnb = dy_b * gon_t * szb
            do_l = []
            for h in range(NH):
                sl = slice(h * DH, (h + 1) * DH)
                dn_h = dnb[:, sl]
                do_l.append(r_l[h] * (dn_h - nh_l[h] * jnp.mean(dn_h * nh_l[h], axis=-1, keepdims=True)))
            do_ref[...] = jnp.concatenate(do_l, axis=-1)
            for j, val in enumerate((dzb, dua, dva, dza, dga, dgb)):
                dp_ref[:, j * D:(j + 1) * D] = val.astype(dp_ref.dtype)

    xrow = lambda i: (jnp.maximum(i - nct, 0), 0)
    wspec = _full((D, D))
    gspec = _full((NH, GC, GC))
    vspec = _full((1, D))
    bf_out = _sds((l, D), _BF)
    return pl.pallas_call(
        body, name="post", grid=(lt // tm,),
        out_shape=(_sds((lt, NMAIN), _BF), _sds((lt, D)), _sds((l, D)), bf_out, bf_out, bf_out, bf_out, bf_out, bf_out,
                   _sds((NH, GC, GC)), _sds((NH, GC, GC)), _sds((8, D))),
        in_specs=[pl.BlockSpec((tm, NREST), lambda i: (i, 0)), pl.BlockSpec((tm, D), lambda i: (i, 0)),
                  pl.BlockSpec((tm, D), lambda i: (i, 0)), pl.BlockSpec((tm, D), xrow), pl.BlockSpec((tm, D), xrow),
                  wspec, wspec, wspec, gspec, gspec, gspec, vspec, vspec, _full((1, DH)), vspec, vspec],
        out_specs=(pl.BlockSpec((tm, NREST), lambda i: (i, 0)), pl.BlockSpec((tm, D), lambda i: (i, 0)),
                   pl.BlockSpec((tm, D), xrow), pl.BlockSpec((tm, D), xrow), pl.BlockSpec((tm, D), xrow),
                   pl.BlockSpec((tm, D), xrow), pl.BlockSpec((tm, D), xrow), pl.BlockSpec((tm, D), xrow),
                   pl.BlockSpec((tm, D), xrow), gspec, gspec, _full((8, D))),
        compiler_params=_params(("arbitrary",)),
    )(p, o_f, o_b, x, tgt, w_pa, w_pb, w_out, w_sp, w_spt, b_spb, ln_g, ln_b, g_on, g_post, gate_x)


def _sum_parts(parts, name):
    r = parts.shape[1]
    tr = _tile(r, (512, 256, 128, 64, 32, 16, 8))

    def body(p_ref, o_ref):
        acc = p_ref[0]
        for s in range(1, NDEV):
            acc = acc + p_ref[s]
        o_ref[...] = acc

    return pl.pallas_call(
        body, name=name, out_shape=_sds((r, LANE)), grid=(r // tr,),
        in_specs=[pl.BlockSpec((NDEV, tr, LANE), lambda i: (0, i, 0))],
        out_specs=pl.BlockSpec((tr, LANE), lambda i: (i, 0)),
        compiler_params=_params(("parallel",)),
    )(parts)


def _mod_bwd(c_all, c_ctx, dmx, dmc, w_mod_g):
    ws = w_mod_g.shape[2]

    def body(ca_ref, cc_ref, dsh_ref, dmx_ref, dmc_ref, dmc_sh_ref, w_ref, gw_ref, gc_ref, gb_ref):
        sc, _ = _silu_g(ca_ref[...])
        scc, dscc = _silu_g(cc_ref[...])
        dmc_tot = jnp.sum(dmc_ref[...], axis=0, keepdims=True)
        gb_ref[...] = jnp.sum(dmx_ref[...], axis=0, keepdims=True) + dmc_tot
        lhs = jnp.concatenate([sc, jnp.broadcast_to(scc, (8, D))], axis=0)
        rhs = jnp.concatenate([dsh_ref[...], dmc_sh_ref[...]], axis=0)
        gw_ref[...] = _mmh(lhs, rhs, TN)
        acc = jnp.zeros((8, D), F32)
        tot8 = jnp.broadcast_to(dmc_tot, (8, 3 * D))
        for j in range(NDEV):
            acc = acc + _mm(tot8[:, j * ws:(j + 1) * ws], w_ref[j], NT)
        gc_ref[...] = acc[0:1, :] * dscc

    return pl.pallas_call(
        body, name="mod_bwd", out_shape=(_sds((D, ws)), _sds((1, D)), _sds((1, 3 * D))),
        compiler_params=_params(),
    )(c_all, c_ctx, _my_cols(dmx, ws), dmx, dmc, _my_cols(dmc, ws), w_mod_g)


def _my_cols(a, ws):
    me = 4 * lax.axis_index("x") + 2 * lax.axis_index("y") + lax.axis_index("c")
    return lax.dynamic_slice_in_dim(a, me * ws, ws, axis=1)


def _adamw(parts, w, m, v, name):
    s_, r, c = parts.shape
    tr = _tile(r, (128, 64, 32, 16, 8)) if r * c * 4 > (1 << 20) else r
    c1 = 1.0 / (1.0 - ADAM_B1 ** ADAM_STEP)
    c2 = 1.0 / (1.0 - ADAM_B2 ** ADAM_STEP)

    def body(p_ref, w_ref, m_ref, v_ref, g_ref, d_ref, nm_ref, nv_ref):
        g = p_ref[0]
        for s in range(1, s_):
            g = g + p_ref[s]
        m_new = ADAM_B1 * m_ref[...] + (1.0 - ADAM_B1) * g
        v_new = ADAM_B2 * v_ref[...] + (1.0 - ADAM_B2) * (g * g)
        g_ref[...] = g
        nm_ref[...] = m_new
        nv_ref[...] = v_new
        d_ref[...] = -ADAM_LR * ((m_new * c1) / (jnp.sqrt(v_new * c2) + ADAM_EPS) + ADAM_WD * w_ref[...])

    blk = pl.BlockSpec((tr, c), lambda i: (i, 0))
    o = _sds((r, c))
    return pl.pallas_call(
        body, name=name, out_shape=(o, o, o, o), grid=(r // tr,),
        in_specs=[pl.BlockSpec((s_, tr, c), lambda i: (0, i, 0)), blk, blk, blk],
        out_specs=(blk, blk, blk, blk),
        compiler_params=_params(("parallel",)),
    )(parts, w, m, v)


def _rows(a):
    flat = a.reshape(-1)
    n = flat.shape[0]
    r = -(-n // (8 * LANE)) * 8
    return jnp.pad(flat, (0, r * LANE - n)).reshape(r, LANE)


def _pack(items):
    parts, layout, at = [], [], 0
    for name, a in items:
        rws = _rows(a.astype(F32))
        layout.append((name, at, rws.shape[0], a.shape))
        parts.append(rws)
        at += rws.shape[0]
    return jnp.concatenate(parts, axis=0), layout


def _unpack(packed, layout):
    out = {}
    for name, at, r, shape in layout:
        n = 1
        for s in shape:
            n *= s
        out[name] = packed[at:at + r].reshape(-1)[:n].reshape(shape)
    return out


def kernel(x, c, ctx, c_ctx, w_mod, b_mod, g_pre, g_post, w_in, w_conv, a_log, dt_bias, g_onorm, gm_ln_g, gm_ln_b, w_sp, b_sp, w_pa, w_pb, w_out, loss_target, m_c_ctx, m_w_mod, m_b_mod, m_g_pre, m_g_post, m_w_in, m_w_conv, m_a_log, m_dt_bias, m_g_onorm, m_gm_ln_g, m_gm_ln_b, m_w_sp, m_b_sp, m_w_pa, m_w_pb, m_w_out, v_c_ctx, v_w_mod, v_b_mod, v_g_pre, v_g_post, v_w_in, v_w_conv, v_a_log, v_dt_bias, v_g_onorm, v_gm_ln_g, v_gm_ln_b, v_w_sp, v_b_sp, v_w_pa, v_w_pb, v_w_out):
    l = x.shape[1]
    lc = ctx.shape[1]
    lt = l + lc
    nch = lt // CH
    me = 4 * lax.axis_index("x") + 2 * lax.axis_index("y") + lax.axis_index("c")
    wsh = w_in.shape[2]
    n_in = NDEV * wsh
    off_a = 3 * D
    n_ab = 4 * NH

    wg_in, wg_mod, wg_pa, wg_pb, wg_out, wg_conv, c_all = _exchange(
        [w_in[0].astype(_BF), w_mod[0].astype(_BF), w_pa[0].astype(_BF), w_pb[0].astype(_BF), w_out[0].astype(_BF),
         w_conv[0], c], scatter=False, name="gather_weights")
    w_full = jnp.moveaxis(wg_in, 0, 1).reshape(D, n_in)
    w_main = jnp.concatenate([w_full[:, off_a + n_ab:], w_full[:, :off_a]], axis=1)
    w_ab = jnp.pad(w_full[:, off_a:off_a + n_ab], ((0, 0), (0, LANE - n_ab)))
    wf_pa, wf_pb, wf_out = wg_pa.reshape(D, D), wg_pb.reshape(D, D), wg_out.reshape(D, D)
    wconv_full = jnp.moveaxis(wg_conv, 0, 1).reshape(3, 3 * D)
    c_all = c_all.reshape(NDEV, D)

    cc = jnp.concatenate([c, c_ctx.reshape(1, D), jnp.zeros((6, D), F32)], axis=0)
    mods = _modulation(cc, wg_mod, b_mod)
    xa = jnp.concatenate([ctx[0], x[0]], axis=0)
    h = _prenorm(xa, mods, g_pre, lc)
    p = _matmul_nn(h, w_main, "in_proj")
    pab = _matmul_nn(h, w_ab, "in_proj_ab")
    abt = jnp.swapaxes(pab[:, :n_ab].reshape(nch, CH, n_ab), 1, 2)
    alog16, dtb16 = a_log.reshape(1, 2 * NH), dt_bias.reshape(1, 2 * NH)
    alog_r = jnp.pad(alog16, ((0, 0), (0, LANE - 2 * NH)))
    dtb_r = jnp.pad(dtb16, ((0, 0), (0, LANE - 2 * NH)))
    alog_c = jnp.pad(alog16.reshape(2 * NH, 1), ((0, 2 * NH), (0, 0)))
    dtb_c = jnp.pad(dtb16.reshape(2 * NH, 1), ((0, 2 * NH), (0, 0)))
    qkv = _qkv_fwd(p, wconv_full, lc)
    o_f, o_b, s_f, s_b, t_f, t_b = _gdn_fwd(qkv, pab, abt, alog_r, dtb_r, alog_c, dtb_c, lc)

    w_spt = jnp.swapaxes(w_sp[0], 1, 2)
    b_spb = jnp.broadcast_to(b_sp[0][:, :, None], (NH, GC, GC))
    gate_x = mods[0:1, 2 * D:]
    dp, do, dy, ya, yb, mg, d_a, d_b, dout, dwsp, dbsp_l, pvec = _post(
        p, o_f, o_b, x[0], loss_target[0], wf_pa, wf_pb, wf_out, w_sp[0], w_spt, b_spb, gm_ln_g, gm_ln_b,
        g_onorm, g_post, gate_x, lc)

    dqkv_f, dqkv_b, dcol_f, dcol_b, drow_f, drow_b, gvec_c, gvec_r = _gdn_bwd(
        qkv, pab, abt, alog_r, dtb_r, alog_c, dtb_c, s_f, s_b, t_f, t_b, do, lc)
    dp, dwconv = _qkv_bwd(p, wconv_full, dqkv_f, dqkv_b, dp, lc)
    drow = jnp.swapaxes(drow_f + drow_b, 1, 2).reshape(lt, n_ab)
    dpab = (dcol_f + dcol_b + jnp.pad(drow, ((0, 0), (0, LANE - n_ab)))).astype(_BF)
    dh = _dh_matmul(dp, dpab, w_main, w_ab)
    grad_x, nvec = _prenorm_bwd(xa, dh, dy, mods, g_pre, lc)

    dw_main = _matmul_tn(h, dp, "dw_in")
    dw_ab = _matmul_tn(h, dpab, "dw_in_ab")
    dw_full = jnp.concatenate([dw_main[:, NREST:], dw_ab[:, :n_ab], dw_main[:, :NREST]], axis=1)
    dw_chunks = jnp.moveaxis(dw_full.reshape(D, NDEV, wsh), 1, 0)
    dw_pa = _matmul_tn(ya, d_a, "dw_pa").reshape(NDEV, D // NDEV, D)
    dw_pb = _matmul_tn(yb, d_b, "dw_pb").reshape(NDEV, D // NDEV, D)
    dw_out = _matmul_tn(mg, dout, "dw_out").reshape(NDEV, D // NDEV, D)
    r_in, r_pa, r_pb, r_out = _exchange([dw_chunks, dw_pa, dw_pb, dw_out], scatter=True, name="scatter_grads")

    dalog = gvec_c[0, :2 * NH] + gvec_r[:2 * NH, 0]
    ddtb = gvec_c[1, :2 * NH] + gvec_r[:2 * NH, 1]
    dmx = jnp.concatenate([nvec[0], nvec[1], pvec[0]])
    dmc = jnp.concatenate([nvec[2], nvec[3], jnp.zeros((D,), F32)])
    small, lay = _pack([
        ("g_pre", nvec[4]), ("g_post", pvec[1]), ("a_log", dalog), ("dt_bias", ddtb), ("g_onorm", pvec[4, :DH]),
        ("gm_ln_g", pvec[2]), ("gm_ln_b", pvec[3]), ("w_sp", dwsp), ("b_sp", jnp.sum(dbsp_l, axis=-1)),
        ("w_conv", dwconv), ("loss", pvec[5]), ("dmx", dmx), ("dmc", dmc)])
    (small_all,) = _exchange([small], scatter=False, name="gather_small")
    tot = _unpack(_sum_parts(small_all, "sum_small"), lay)
    each = {nm: small_all[:, at:at + r].reshape(NDEV, -1) for nm, at, r, _ in lay if nm in ("dmx", "dmc")}
    g_wmod, g_cctx, g_bmod = _mod_bwd(c_all, c_ctx.reshape(1, D), each["dmx"], each["dmc"], wg_mod)
    loss = 0.5 / D * jnp.sum(tot["loss"])
    ws_conv = w_conv.shape[2]
    g_wconv = lax.dynamic_slice_in_dim(tot["w_conv"], me * ws_conv, ws_conv, axis=1)

    small_names = ["c_ctx", "b_mod", "g_pre", "g_post", "a_log", "dt_bias", "g_onorm", "gm_ln_g", "gm_ln_b",
                   "w_sp", "b_sp", "w_conv"]
    wts = dict(c_ctx=c_ctx, b_mod=b_mod, g_pre=g_pre, g_post=g_post, a_log=a_log, dt_bias=dt_bias, g_onorm=g_onorm,
               gm_ln_g=gm_ln_g, gm_ln_b=gm_ln_b, w_sp=w_sp, b_sp=b_sp, w_conv=w_conv)
    ms = dict(c_ctx=m_c_ctx, b_mod=m_b_mod, g_pre=m_g_pre, g_post=m_g_post, a_log=m_a_log, dt_bias=m_dt_bias,
              g_onorm=m_g_onorm, gm_ln_g=m_gm_ln_g, gm_ln_b=m_gm_ln_b, w_sp=m_w_sp, b_sp=m_b_sp, w_conv=m_w_conv)
    vs = dict(c_ctx=v_c_ctx, b_mod=v_b_mod, g_pre=v_g_pre, g_post=v_g_post, a_log=v_a_log, dt_bias=v_dt_bias,
              g_onorm=v_g_onorm, gm_ln_g=v_gm_ln_g, gm_ln_b=v_gm_ln_b, w_sp=v_w_sp, b_sp=v_b_sp, w_conv=v_w_conv)
    gs = dict(tot)
    gs.update(c_ctx=g_cctx, b_mod=g_bmod, w_conv=g_wconv)
    gpk, play = _pack([(nm, gs[nm].reshape(wts[nm].shape)) for nm in small_names])
    wpk, _ = _pack([(nm, wts[nm]) for nm in small_names])
    mpk, _ = _pack([(nm, ms[nm]) for nm in small_names])
    vpk, _ = _pack([(nm, vs[nm]) for nm in small_names])
    res_small = [_unpack(a, play) for a in _adamw(gpk[None], wpk, mpk, vpk, "adamw_small")]
    res_big = {
        "w_mod": _adamw(g_wmod[None], w_mod[0], m_w_mod[0], v_w_mod[0], "adamw_w_mod"),
        "w_in": _adamw(r_in, w_in[0], m_w_in[0], v_w_in[0], "adamw_w_in"),
        "w_pa": _adamw(r_pa, w_pa[0], m_w_pa[0], v_w_pa[0], "adamw_w_pa"),
        "w_pb": _adamw(r_pb, w_pb[0], m_w_pb[0], v_w_pb[0], "adamw_w_pb"),
        "w_out": _adamw(r_out, w_out[0], m_w_out[0], v_w_out[0], "adamw_w_out"),
    }
    order = ["c_ctx", "w_mod", "b_mod", "g_pre", "g_post", "w_in", "w_conv", "a_log", "dt_bias", "g_onorm",
             "gm_ln_g", "gm_ln_b", "w_sp", "b_sp", "w_pa", "w_pb", "w_out"]
    outs = [loss, grad_x[None]]
    for k in range(4):
        for nm in order:
            if nm in res_big:
                outs.append(res_big[nm][k][None])
            else:
                outs.append(res_small[k][nm])
    return tuple(outs)
```

```python
import functools

import jax
import jax.numpy as jnp
from jax import lax
from jax.experimental import pallas as pl
from jax.experimental.pallas import tpu as pltpu

F32 = jnp.float32
_BF = jnp.bfloat16
_HI = lax.Precision.HIGHEST
D = 1024
NH = 8
DH = 128
CH = 64
GC = 128
NREST = 6 * D
NMAIN = NREST + 3 * D
EPS = 1e-6
LANE = 128
NDEV = 8
VMEM_LIMIT = 56 * 1024 * 1024
MESH = pl.DeviceIdType.MESH

ADAM_LR, ADAM_B1, ADAM_B2, ADAM_EPS, ADAM_WD, ADAM_STEP = 0.001, 0.9, 0.999, 1e-08, 0.01, 10

NN = ((1,), (0,))
NT = ((1,), (1,))
TN = ((0,), (0,))


def _dot(a, b, dims=NN, prec=None):
    return lax.dot_general(a, b, (dims, ((), ())), precision=prec, preferred_element_type=F32)


def _mm(a, b, dims=NN):
    return _dot(a.astype(_BF), b.astype(_BF), dims)


def _mmh(a, b, dims=NN):
    return _dot(a.astype(F32), b.astype(F32), dims, _HI)


def _split(a):
    hi = a.astype(_BF)
    return hi, (a - hi.astype(F32)).astype(_BF)


def _mm3(a, b, dims=NN):
    ah, al = _split(a)
    bh, bl = _split(b)
    return _dot(ah, bh, dims) + (_dot(ah, bl, dims) + _dot(al, bh, dims))


def _sigmoid(x):
    return 1.0 / (1.0 + jnp.exp(-x))


def _silu_g(x):
    s = _sigmoid(x)
    return x * s, s * (1.0 + x * (1.0 - s))


def _gelu_g(x):
    c = 0.7978845608028654
    t = jnp.tanh(c * (x + 0.044715 * (x * x * x)))
    cdf = 0.5 * (1.0 + t)
    return x * cdf, cdf + 0.5 * x * (1.0 - t * t) * c * (1.0 + 3 * 0.044715 * x * x)


def _softplus(x):
    return jnp.maximum(x, 0.0) + jnp.log(1.0 + jnp.exp(-jnp.abs(x)))


def _params(sem=None):
    return pltpu.CompilerParams(dimension_semantics=sem, vmem_limit_bytes=VMEM_LIMIT)


def _tile(n, pref):
    for t in pref:
        if n % t == 0:
            return t
    return n


def _full(shape):
    nd = len(shape)
    return pl.BlockSpec(shape, lambda *_: (0,) * nd)


def _sds(shape, dt=F32):
    return jax.ShapeDtypeStruct(shape, dt)


def _exchange(arrays, scatter, name):
    n = len(arrays)
    out_shape = tuple(
        _sds(a.shape if scatter else (NDEV,) + a.shape, a.dtype) for a in arrays)

    def body(*refs):
        ins, outs = refs[:n], refs[n:2 * n]
        send_sems, recv_sems, loc_sems = refs[2 * n:]
        x, y, c = lax.axis_index("x"), lax.axis_index("y"), lax.axis_index("c")
        me = 4 * x + 2 * y + c

        def src(a, idx):
            return ins[a].at[idx] if scatter else ins[a]

        local = [pltpu.make_async_copy(src(a, me), outs[a].at[me], loc_sems.at[a]) for a in range(n)]
        for cp in local:
            cp.start()
        copies = []
        for k in range(1, NDEV):
            px = 1 - x if (k >> 2) & 1 else x
            py = 1 - y if (k >> 1) & 1 else y
            pc = 1 - c if k & 1 else c
            pidx = 4 * px + 2 * py + pc
            for a in range(n):
                send = pltpu.make_async_remote_copy(
                    src_ref=src(a, pidx), dst_ref=outs[a].at[me], send_sem=send_sems.at[a, k - 1],
                    recv_sem=recv_sems.at[a, k - 1], device_id=(px, py, pc), device_id_type=MESH)
                send.start()
                arrive = pltpu.make_async_remote_copy(
                    src_ref=src(a, pidx), dst_ref=outs[a].at[pidx], send_sem=send_sems.at[a, k - 1],
                    recv_sem=recv_sems.at[a, k - 1], device_id=(px, py, pc), device_id_type=MESH)
                copies.append(arrive)
        for cp in copies:
            cp.wait()
        for cp in local:
            cp.wait()

    any_spec = pl.BlockSpec(memory_space=pl.ANY)
    return pl.pallas_call(
        body, name=name, out_shape=out_shape,
        in_specs=[any_spec] * n, out_specs=tuple([any_spec] * n),
        scratch_shapes=[pltpu.SemaphoreType.DMA((n, NDEV - 1)), pltpu.SemaphoreType.DMA((n, NDEV - 1)),
                        pltpu.SemaphoreType.DMA((n,))],
    )(*arrays)


def _matmul_nn(a, b, name):
    m, kk = a.shape
    n = b.shape[1]
    tm = _tile(m, (1088, 1024, 640, 512, 256, 128))
    tn = _tile(n, (512, 256, 128))

    def body(a_ref, b_ref, o_ref):
        o_ref[...] = _mm(a_ref[...], b_ref[...])

    return pl.pallas_call(
        body, name=name, out_shape=_sds((m, n)), grid=(n // tn, m // tm),
        in_specs=[pl.BlockSpec((tm, kk), lambda j, i: (i, 0)), pl.BlockSpec((kk, tn), lambda j, i: (0, j))],
        out_specs=pl.BlockSpec((tm, tn), lambda j, i: (i, j)),
        compiler_params=_params(("parallel", "parallel")),
    )(a, b)


def _matmul_tn(a, b, name):
    kk, m = a.shape
    n = b.shape[1]
    tk = _tile(kk, (1088, 1024, 640, 512, 256, 128))
    tn = _tile(n, (1024, 512, 256, 128))
    nk = kk // tk

    def body(a_ref, b_ref, o_ref):
        @pl.when(pl.program_id(1) == 0)
        def _():
            o_ref[...] = jnp.zeros_like(o_ref)

        o_ref[...] += _mm(a_ref[...], b_ref[...], TN)

    return pl.pallas_call(
        body, name=name, out_shape=_sds((m, n)), grid=(n // tn, nk),
        in_specs=[pl.BlockSpec((tk, m), lambda j, k: (k, 0)), pl.BlockSpec((tk, tn), lambda j, k: (k, j))],
        out_specs=pl.BlockSpec((m, tn), lambda j, k: (0, j)),
        compiler_params=_params(("parallel", "arbitrary")),
    )(a, b)


def _dh_matmul(dp, dpab, w_main, w_ab):
    lt = dp.shape[0]
    tm = _tile(lt, (1088, 1024, 640, 512, 256, 128))
    nk = NMAIN // D

    def body(dp_ref, ab_ref, w_ref, wab_ref, o_ref):
        @pl.when(pl.program_id(1) == 0)
        def _():
            o_ref[...] = _mm(ab_ref[...], wab_ref[...], NT)

        o_ref[...] += _mm(dp_ref[...], w_ref[...], NT)

    return pl.pallas_call(
        body, name="dh_matmul", out_shape=_sds((lt, D)), grid=(lt // tm, nk),
        in_specs=[pl.BlockSpec((tm, D), lambda i, k: (i, k)), pl.BlockSpec((tm, LANE), lambda i, k: (i, 0)),
                  pl.BlockSpec((D, D), lambda i, k: (0, k)), _full((D, LANE))],
        out_specs=pl.BlockSpec((tm, D), lambda i, k: (i, 0)),
        compiler_params=_params(("parallel", "arbitrary")),
    )(dp, dpab, w_main, w_ab)


def _modulation(cc, w_mod_g, b_mod):
    ws = w_mod_g.shape[2]

    def body(c_ref, w_ref, b_ref, o_ref):
        s, _ = _silu_g(c_ref[...])
        o_ref[...] = _mm(s, w_ref[0]) + b_ref[...]

    return pl.pallas_call(
        body, name="modulation", out_shape=_sds((8, 3 * D)), grid=(NDEV,),
        in_specs=[_full((8, D)), pl.BlockSpec((1, D, ws), lambda j: (j, 0, 0)), pl.BlockSpec((1, ws), lambda j: (0, j))],
        out_specs=pl.BlockSpec((8, ws), lambda j: (0, j)),
        compiler_params=_params(("parallel",)),
    )(cc, w_mod_g, b_mod)


def _prenorm(xa, mods, g_pre, lc):
    lt = xa.shape[0]
    tm = _tile(lc, (256, 128))
    nct = lc // tm

    def body(x_ref, m_ref, g_ref, o_ref):
        x = x_ref[...]
        is_ctx = pl.program_id(0) < nct
        shift = jnp.where(is_ctx, m_ref[1:2, 0:D], m_ref[0:1, 0:D])
        scale = jnp.where(is_ctx, m_ref[1:2, D:2 * D], m_ref[0:1, D:2 * D])
        r = lax.rsqrt(jnp.mean(x * x, axis=-1, keepdims=True) + EPS)
        o_ref[...] = ((x * r * g_ref[...]) * (1.0 + scale) + shift).astype(o_ref.dtype)

    return pl.pallas_call(
        body, name="prenorm", out_shape=_sds((lt, D), _BF), grid=(lt // tm,),
        in_specs=[pl.BlockSpec((tm, D), lambda i: (i, 0)), _full((8, 3 * D)), _full((1, D))],
        out_specs=pl.BlockSpec((tm, D), lambda i: (i, 0)),
        compiler_params=_params(("parallel",)),
    )(xa, mods, g_pre)


def _prenorm_bwd(xa, dh, dy, mods, g_pre, lc):
    lt = xa.shape[0]
    tm = _tile(lc, (256, 128))
    nct = lc // tm
    nl = (lt - lc) // tm

    def body(x_ref, dh_ref, dy_ref, m_ref, g_ref, gx_ref, vec_ref):
        i = pl.program_id(0)

        @pl.when(i == 0)
        def _():
            vec_ref[...] = jnp.zeros_like(vec_ref)

        x = x_ref[...]
        dh = dh_ref[...]
        g = g_ref[...]
        is_ctx = i < nct
        scale = jnp.where(is_ctx, m_ref[1:2, D:2 * D], m_ref[0:1, D:2 * D])
        r = lax.rsqrt(jnp.mean(x * x, axis=-1, keepdims=True) + EPS)
        n = x * r
        hn = n * g
        dsh = jnp.sum(dh, axis=0, keepdims=True)
        dsc = jnp.sum(dh * hn, axis=0, keepdims=True)
        dhn = dh * (1.0 + scale)
        vec_ref[4:5, :] += jnp.sum(dhn * n, axis=0, keepdims=True)
        dn = dhn * g
        dx = r * (dn - n * jnp.mean(dn * n, axis=-1, keepdims=True))

        @pl.when(is_ctx)
        def _():
            vec_ref[2:3, :] += dsh
            vec_ref[3:4, :] += dsc

        @pl.when(jnp.logical_not(is_ctx))
        def _():
            vec_ref[0:1, :] += dsh
            vec_ref[1:2, :] += dsc
            gx_ref[...] = dy_ref[...] + dx

    xrow = lambda i: (jnp.maximum(i - nct, 0), 0)
    return pl.pallas_call(
        body, name="prenorm_bwd", out_shape=(_sds((nl * tm, D)), _sds((8, D))), grid=(lt // tm,),
        in_specs=[pl.BlockSpec((tm, D), lambda i: (i, 0)), pl.BlockSpec((tm, D), lambda i: (i, 0)),
                  pl.BlockSpec((tm, D), xrow), _full((8, 3 * D)), _full((1, D))],
        out_specs=(pl.BlockSpec((tm, D), xrow), _full((8, D))),
        compiler_params=_params(("arbitrary",)),
    )(xa, dh, dy, mods, g_pre)


def _conv_parts(x, w, lc):
    lt = x.shape[0]
    row = lax.broadcasted_iota(jnp.int32, x.shape, 0)
    first = (row == 0) | (row == lc)
    last = (row == lc - 1) | (row == lt - 1)
    xp = jnp.where(first, 0.0, pltpu.roll(x, 1, 0))
    xn = jnp.where(last, 0.0, pltpu.roll(x, lt - 1, 0))
    y = w[0:1, :] * xp + w[1:2, :] * x + w[2:3, :] * xn
    return xp, xn, y, first, last


def _qkv_fwd(p, w_conv, lc):
    lt = p.shape[0]
    cb0 = NREST // DH

    def body(p_ref, w_ref, o_ref):
        _, _, y, _, _ = _conv_parts(p_ref[...], w_ref[...], lc)
        s, _ = _silu_g(y)
        rs = lax.rsqrt(jnp.sum(s * s, axis=-1, keepdims=True) + EPS)
        o_ref[...] = s * jnp.where(pl.program_id(0) < 2 * NH, rs, 1.0)

    return pl.pallas_call(
        body, name="qkv_fwd", out_shape=_sds((lt, 3 * D)), grid=(3 * NH,),
        in_specs=[pl.BlockSpec((lt, DH), lambda j: (0, cb0 + j)), pl.BlockSpec((3, DH), lambda j: (0, j))],
        out_specs=pl.BlockSpec((lt, DH), lambda j: (0, j)),
        compiler_params=_params(("parallel",)),
    )(p, w_conv)


def _qkv_bwd(p, w_conv, dqkv_f, dqkv_b, dp, lc):
    lt = p.shape[0]
    cb0 = NREST // DH

    def body(p_ref, w_ref, df_ref, db_ref, dp_in, dp_ref, dw_ref):
        del dp_in
        w = w_ref[...]
        xp, xn, y, first, last = _conv_parts(p_ref[...], w, lc)
        s, ds_dy = _silu_g(y)
        dn = df_ref[...] + db_ref[...]
        rs = lax.rsqrt(jnp.sum(s * s, axis=-1, keepdims=True) + EPS)
        nrm = s * rs
        ds_n = rs * (dn - nrm * jnp.sum(dn * nrm, axis=-1, keepdims=True))
        ds = jnp.where(pl.program_id(0) < 2 * NH, ds_n, dn)
        dy = ds * ds_dy
        dw_ref[0:1, :] = jnp.sum(dy * xp, axis=0, keepdims=True)
        dw_ref[1:2, :] = jnp.sum(dy * p_ref[...], axis=0, keepdims=True)
        dw_ref[2:3, :] = jnp.sum(dy * xn, axis=0, keepdims=True)
        dyn = jnp.where(last, 0.0, pltpu.roll(dy, lt - 1, 0))
        dyp = jnp.where(first, 0.0, pltpu.roll(dy, 1, 0))
        dp_ref[...] = (w[1:2, :] * dy + w[0:1, :] * dyn + w[2:3, :] * dyp).astype(dp_ref.dtype)

    return pl.pallas_call(
        body, name="qkv_bwd", out_shape=(_sds(dp.shape, dp.dtype), _sds((3, 3 * D))), grid=(3 * NH,),
        in_specs=[pl.BlockSpec((lt, DH), lambda j: (0, cb0 + j)), pl.BlockSpec((3, DH), lambda j: (0, j)),
                  pl.BlockSpec((lt, DH), lambda j: (0, j)), pl.BlockSpec((lt, DH), lambda j: (0, j)),
                  pl.BlockSpec(memory_space=pl.ANY)],
        out_specs=(pl.BlockSpec((lt, DH), lambda j: (0, cb0 + j)), pl.BlockSpec((3, DH), lambda j: (0, j))),
        input_output_aliases={4: 0},
        compiler_params=_params(("parallel",)),
    )(p, w_conv, dqkv_f, dqkv_b, dp)


def _masks(d):
    ri = lax.broadcasted_iota(jnp.int32, (CH, CH), 0)
    ci = lax.broadcasted_iota(jnp.int32, (CH, CH), 1)
    incl = (ri >= ci) if d == 0 else (ri <= ci)
    strict = (ri > ci) if d == 0 else (ri < ci)
    incl_t = (ri <= ci) if d == 0 else (ri >= ci)
    return incl, strict, incl_t, ri == ci


def _decays(d, ab, abt, alog_r, dtb_r, alog_c, dtb_c, incl, incl_t):
    g_full = -jnp.exp(alog_r) * _softplus(ab + dtb_r)
    beta_full = _sigmoid(ab)
    gc_full = _mmh(incl.astype(F32), g_full)
    gl_full = jnp.sum(g_full, axis=0, keepdims=True)
    gt_full = -jnp.exp(alog_c) * _softplus(abt + dtb_c)
    gct = _mmh(gt_full, incl_t.astype(F32))
    return g_full, beta_full, gc_full, gl_full, gt_full, gct


def _lane_onehot(idx, n=LANE):
    return (lax.broadcasted_iota(jnp.int32, (1, n), 1) == idx).astype(F32)


def _head_scalars(d, h, beta_full, gc_full, gl_full, gct):
    idx = d * NH + h
    oh = _lane_onehot(idx)
    gcol = jnp.sum(gc_full * oh, axis=-1, keepdims=True)
    bcol = jnp.sum(beta_full * _lane_onehot(2 * NH + idx), axis=-1, keepdims=True)
    gl = jnp.sum(gl_full * oh, axis=-1, keepdims=True)
    grow = gct[idx:idx + 1, :]
    return gcol, grow, bcol, gl


def _lockstep(gens):
    live = list(gens)
    while live:
        nxt = []
        for g in live:
            try:
                next(g)
                nxt.append(g)
            except StopIteration:
                pass
        live = nxt


def _chunk_local(qh, kh, vh, gcol, grow, bcol, gl, incl, strict):
    decay = jnp.where(incl, jnp.exp(gcol - grow), 0.0)
    kb = kh * bcol
    a = jnp.where(strict, _mm(kb, kh, NT) * decay, 0.0)
    egc = jnp.exp(gcol)
    rhs_u = vh * bcol
    rhs_w = kb * egc
    qs = qh * (DH ** -0.5)
    attn = jnp.where(incl, _mm(qs, kh, NT) * decay, 0.0)
    etail = jnp.exp(gl - gcol)
    return decay, kb, a, egc, rhs_u, rhs_w, qs, attn, etail


def _scan_specs(lt, lc, bwd_pass):
    nch = lt // CH
    ncc = lc // CH
    if not bwd_pass:
        cf = lambda s: s
        cb = lambda s: jnp.where(s < ncc, ncc - 1 - s, nch + ncc - 1 - s)
    else:
        cf = lambda s: nch - 1 - s
        cb = lambda s: jnp.where(s < nch - ncc, ncc + s, s - (nch - ncc))
    return nch, cf, cb


def _gdn_fwd(qkv, pab, abt, alog_r, dtb_r, alog_c, dtb_c, lc):
    lt = qkv.shape[0]
    nch, cf, cb = _scan_specs(lt, lc, False)

    def body(qf, kf, vf, abf, abtf, qb, kb_, vb, abb, abtb, ar, dr, ac, dc,
             of_ref, ob_ref, sf_ref, sb_ref, tf_ref, tb_ref, s_scr):
        @pl.when(pl.program_id(0) == 0)
        def _():
            s_scr[...] = jnp.zeros_like(s_scr)

        def chain(d, h, q_r, k_r, v_r, o_ref, sh_ref, th_ref, masks, decs):
            incl, strict, _, eye = masks
            sl = slice(h * DH, (h + 1) * DH)
            qh, kh, vh = q_r[:, sl], k_r[:, sl], v_r[:, sl]
            gcol, grow, bcol, gl = _head_scalars(d, h, *decs)
            _, _, a, egc, rhs_u, rhs_w, qs, attn, etail = _chunk_local(qh, kh, vh, gcol, grow, bcol, gl, incl, strict)
            yield
            n = -a
            t = jnp.where(eye, 1.0, 0.0) + n
            p = _mm3(n, n)
            yield
            for _ in range(4):
                r = _mm3(jnp.concatenate([t, p], axis=0), p)
                yield
                t = t + r[:CH]
                p = r[CH:]
            t = t + _mm3(t, p)
            yield
            sol = _mm3(t, jnp.concatenate([rhs_u, rhs_w], axis=1))
            u, w = sol[:, :DH], sol[:, DH:]
            s = s_scr[d, h]
            sh_ref[0, h] = s
            th_ref[0, h] = t
            yield
            ws = _mm(jnp.concatenate([w, qs * egc], axis=0), s)
            yield
            v_new = u - ws[:CH]
            o_ref[:, sl] = ws[CH:] + _mm(attn, v_new)
            s_scr[d, h] = s * jnp.exp(gl) + _mm(kh * etail, v_new, TN)

        chains = []
        for d, (q_r, k_r, v_r, ab_r, abt_r, o_ref, sh_ref, th_ref) in enumerate(
                ((qf, kf, vf, abf, abtf, of_ref, sf_ref, tf_ref), (qb, kb_, vb, abb, abtb, ob_ref, sb_ref, tb_ref))):
            masks = _masks(d)
            _, beta_full, gc_full, gl_full, _, gct = _decays(
                d, ab_r[...], abt_r[0], ar[...], dr[...], ac[...], dc[...], masks[0], masks[2])
            for h in range(NH):
                chains.append(chain(d, h, q_r, k_r, v_r, o_ref, sh_ref, th_ref, masks, (beta_full, gc_full, gl_full, gct)))
        _lockstep(chains)

    def row(c, col):
        return pl.BlockSpec((CH, D), lambda s: (c(s), col))

    def chunk_in(c):
        return [row(c, 0), row(c, 1), row(c, 2), pl.BlockSpec((CH, LANE), lambda s: (c(s), 0)),
                pl.BlockSpec((1, 4 * NH, CH), lambda s: (c(s), 0, 0))]

    def hist(c, n):
        return pl.BlockSpec((1, NH, n, n), lambda s: (c(s), 0, 0, 0))

    small = [_full((1, LANE)), _full((1, LANE)), _full((4 * NH, 1)), _full((4 * NH, 1))]
    return pl.pallas_call(
        body, name="gdn_fwd", grid=(nch,),
        out_shape=(_sds((lt, D)), _sds((lt, D)), _sds((nch, NH, DH, DH)), _sds((nch, NH, DH, DH)),
                   _sds((nch, NH, CH, CH)), _sds((nch, NH, CH, CH))),
        in_specs=chunk_in(cf) + chunk_in(cb) + small,
        out_specs=(pl.BlockSpec((CH, D), lambda s: (cf(s), 0)), pl.BlockSpec((CH, D), lambda s: (cb(s), 0)),
                   hist(cf, DH), hist(cb, DH), hist(cf, CH), hist(cb, CH)),
        scratch_shapes=[pltpu.VMEM((2, NH, DH, DH), F32)],
        compiler_params=_params(("arbitrary",)),
    )(qkv, qkv, qkv, pab, abt, qkv, qkv, qkv, pab, abt, alog_r, dtb_r, alog_c, dtb_c)


def _gdn_bwd(qkv, pab, abt, alog_r, dtb_r, alog_c, dtb_c, s_f, s_b, t_f, t_b, do, lc):
    lt = qkv.shape[0]
    nch, cf, cb = _scan_specs(lt, lc, True)

    def body(qf, kf, vf, abf, abtf, sf_ref, tf_ref, dof, qb, kb_, vb, abb, abtb, sb_ref, tb_ref, dob, ar, dr, ac, dc,
             dqf_ref, dqb_ref, dcf_ref, dcb_ref, drf_ref, drb_ref, vcol_ref, vrow_ref, ds_scr):
        @pl.when(pl.program_id(0) == 0)
        def _():
            ds_scr[...] = jnp.zeros_like(ds_scr)
            vcol_ref[...] = jnp.zeros_like(vcol_ref)
            vrow_ref[...] = jnp.zeros_like(vrow_ref)

        alog_r_, dtb_r_, alog_c_, dtb_c_ = ar[...], dr[...], ac[...], dc[...]
        lane2 = lax.broadcasted_iota(jnp.int32, (1, LANE), 1)
        acc = [[], []]

        def chain(d, h, q_r, k_r, v_r, sh_ref, th_ref, do_r, dq_ref, masks, decs):
            incl, strict, _, _ = masks
            idx = d * NH + h
            sl = slice(h * DH, (h + 1) * DH)
            qh, kh, vh = q_r[:, sl], k_r[:, sl], v_r[:, sl]
            doh = do_r[:, sl]
            gcol, grow, bcol, gl = _head_scalars(d, h, *decs)
            decay, kb, a, egc, rhs_u, rhs_w, qs, attn, etail = _chunk_local(qh, kh, vh, gcol, grow, bcol, gl, incl, strict)
            t = th_ref[0, h]
            s = sh_ref[0, h]
            ds_new = ds_scr[d, h]
            sol = _mm3(t, jnp.concatenate([rhs_u, rhs_w], axis=1))
            u, w = sol[:, :DH], sol[:, DH:]
            q_dec = qs * egc
            k_tail = kh * etail
            egl = jnp.exp(gl)
            dv_new = _mm(attn, doh, TN) + _mm(k_tail, ds_new)
            dq_dec = _mm(doh, s, NT)
            dgl = jnp.sum(jnp.sum(ds_new * s, axis=0, keepdims=True), axis=-1, keepdims=True) * egl
            yield
            v_new = u - _mm(w, s)
            dw = -_mm(dv_new, s, NT)
            ds_scr[d, h] = ds_new * egl + _mm(q_dec, doh, TN) - _mm(w, dv_new, TN)
            yield
            dattn = jnp.where(incl, _mm(doh, v_new, NT), 0.0)
            dk_tail = _mm(v_new, ds_new, NT)
            dr = _mm3(t, jnp.concatenate([dv_new, dw], axis=1), TN)
            dr_u, dr_w = dr[:, :DH], dr[:, DH:]
            yield
            da = -jnp.where(strict, _mm3(dr, sol, NT), 0.0)
            nq = dattn * decay
            dqs = _mm(nq, kh) + dq_dec * egc
            dk = _mm(nq, qs, TN)
            yield
            dv = dr_u * bcol
            dbeta = jnp.sum(dr_u * vh, axis=-1, keepdims=True)
            dgc = jnp.sum(dr_w * rhs_w, axis=-1, keepdims=True)
            m = da * decay
            dkb = dr_w * egc + _mm(m, kh)
            dk = dk + _mm(m, kb, TN)
            pq = da * a + dattn * attn
            dgc = dgc + jnp.sum(pq, axis=-1, keepdims=True) + jnp.sum(dq_dec * q_dec, axis=-1, keepdims=True)
            dgr = -jnp.sum(pq, axis=0, keepdims=True)
            tt = jnp.sum(dk_tail * k_tail, axis=-1, keepdims=True)
            dk = dk + dk_tail * etail + dkb * bcol
            dgc = dgc - tt
            dgl = dgl + jnp.sum(tt, axis=0, keepdims=True)
            dbeta = dbeta + jnp.sum(dkb * kh, axis=-1, keepdims=True)
            dq_ref[:, sl] = dqs * (DH ** -0.5)
            dq_ref[:, D + h * DH:D + (h + 1) * DH] = dk
            dq_ref[:, 2 * D + h * DH:2 * D + (h + 1) * DH] = dv
            acc[d].append((idx, dgc, dgl, dbeta, dgr))

        dirs = ((qf, kf, vf, abf, abtf, sf_ref, tf_ref, dof, dqf_ref, dcf_ref, drf_ref),
                (qb, kb_, vb, abb, abtb, sb_ref, tb_ref, dob, dqb_ref, dcb_ref, drb_ref))
        chains, ctx_d = [], []
        for d, (q_r, k_r, v_r, ab_r, abt_r, sh_ref, th_ref, do_r, dq_ref, _, _) in enumerate(dirs):
            masks = _masks(d)
            ab, abt = ab_r[...], abt_r[0]
            g_full, beta_full, gc_full, gl_full, gt_full, gct = _decays(
                d, ab, abt, alog_r_, dtb_r_, alog_c_, dtb_c_, masks[0], masks[2])
            ctx_d.append((masks, ab, abt, g_full, beta_full, gt_full))
            for h in range(NH):
                chains.append(chain(d, h, q_r, k_r, v_r, sh_ref, th_ref, do_r, dq_ref, masks,
                                    (beta_full, gc_full, gl_full, gct)))
        _lockstep(chains)
        for d in range(2):
            (incl, _, incl_t, _), ab, abt, g_full, beta_full, gt_full = ctx_d[d]
            dcol_ref, drow_ref = dirs[d][9], dirs[d][10]
            dgc_col = jnp.zeros((CH, LANE), F32)
            dgl_row = jnp.zeros((1, LANE), F32)
            dbeta_col = jnp.zeros((CH, LANE), F32)
            dgc_row = jnp.zeros((4 * NH, CH), F32)
            for idx, dgc, dgl, dbeta, dgr in acc[d]:
                oh = _lane_onehot(idx)
                dgc_col = dgc_col + dgc * oh
                dgl_row = dgl_row + dgl * oh
                dbeta_col = dbeta_col + dbeta * _lane_onehot(2 * NH + idx)
                ohc = (lax.broadcasted_iota(jnp.int32, (4 * NH, 1), 0) == idx).astype(F32)
                dgc_row = dgc_row + ohc * dgr
            dg_col = _mmh(incl_t.astype(F32), dgc_col) + dgl_row
            dg_row = _mmh(dgc_row, incl.astype(F32))
            sg_col = _sigmoid(ab + dtb_r_)
            da_col = dg_col * (-jnp.exp(alog_r_)) * sg_col
            dcol_ref[...] = da_col + dbeta_col * beta_full * (1.0 - beta_full)
            da_row = dg_row * (-jnp.exp(alog_c_)) * _sigmoid(abt + dtb_c_)
            drow_ref[0] = da_row
            vcol_ref[0:1, :] += jnp.sum(dg_col * g_full, axis=0, keepdims=True)
            vcol_ref[1:2, :] += jnp.sum(da_col, axis=0, keepdims=True)
            rl = jnp.sum(dg_row * gt_full, axis=-1, keepdims=True)
            rd = jnp.sum(da_row, axis=-1, keepdims=True)
            vrow_ref[...] += jnp.where(lane2 == 0, rl, 0.0) + jnp.where(lane2 == 1, rd, 0.0)

    def row(c, col):
        return pl.BlockSpec((CH, D), lambda s: (c(s), col))

    def hist(c, n):
        return pl.BlockSpec((1, NH, n, n), lambda s: (c(s), 0, 0, 0))

    def chunk_in(c):
        return [row(c, 0), row(c, 1), row(c, 2), pl.BlockSpec((CH, LANE), lambda s: (c(s), 0)),
                pl.BlockSpec((1, 4 * NH, CH), lambda s: (c(s), 0, 0)), hist(c, DH), hist(c, CH), row(c, 0)]

    small = [_full((1, LANE)), _full((1, LANE)), _full((4 * NH, 1)), _full((4 * NH, 1))]
    return pl.pallas_call(
        body, name="gdn_bwd", grid=(nch,),
        out_shape=(_sds((lt, 3 * D)), _sds((lt, 3 * D)), _sds((lt, LANE)), _sds((lt, LANE)),
                   _sds((nch, 4 * NH, CH)), _sds((nch, 4 * NH, CH)), _sds((8, LANE)), _sds((4 * NH, LANE))),
        in_specs=chunk_in(cf) + chunk_in(cb) + small,
        out_specs=(pl.BlockSpec((CH, 3 * D), lambda s: (cf(s), 0)), pl.BlockSpec((CH, 3 * D), lambda s: (cb(s), 0)),
                   pl.BlockSpec((CH, LANE), lambda s: (cf(s), 0)), pl.BlockSpec((CH, LANE), lambda s: (cb(s), 0)),
                   pl.BlockSpec((1, 4 * NH, CH), lambda s: (cf(s), 0, 0)), pl.BlockSpec((1, 4 * NH, CH), lambda s: (cb(s), 0, 0)),
                   _full((8, LANE)), _full((4 * NH, LANE))),
        scratch_shapes=[pltpu.VMEM((2, NH, DH, DH), F32)],
        compiler_params=_params(("arbitrary",)),
    )(qkv, qkv, qkv, pab, abt, s_f, t_f, do, qkv, qkv, qkv, pab, abt, s_b, t_b, do, alog_r, dtb_r, alog_c, dtb_c)


def _post(p, o_f, o_b, x, tgt, w_pa, w_pb, w_out, w_sp, w_spt, b_spb, ln_g, ln_b, g_on, g_post, gate_x, lc):
    lt = p.shape[0]
    l = x.shape[0]
    tm = GC
    nct = lc // tm

    def body(p_ref, of_ref, ob_ref, x_ref, t_ref, wpa, wpb, wout, wsp, wspt, bspb, lng_ref, lnb_ref, gon_ref, gpost_ref, gate_ref,
             dp_ref, do_ref, dy_ref, ya_ref, yb_ref, mg_ref, da_ref, db_ref, dout_ref, dwsp_ref, dbsp_ref, vec_ref):
        i = pl.program_id(0)

        @pl.when(i == 0)
        def _():
            dwsp_ref[...] = jnp.zeros_like(dwsp_ref)
            dbsp_ref[...] = jnp.zeros_like(dbsp_ref)
            vec_ref[...] = jnp.zeros_like(vec_ref)

        @pl.when(i < nct)
        def _():
            dp_ref[...] = jnp.zeros_like(dp_ref)
            do_ref[...] = jnp.zeros_like(do_ref)

        @pl.when(i >= nct)
        def _():
            lng, lnb, gon, gpost, gate = lng_ref[...], lnb_ref[...], gon_ref[...], gpost_ref[...], gate_ref[...]
            zb, ua, va, za, ga, gb = [p_ref[:, j * D:(j + 1) * D] for j in range(6)]
            o = of_ref[...] + ob_ref[...]
            szb, dszb = _silu_g(zb)
            nh_l, r_l = [], []
            for h in range(NH):
                oh = o[:, h * DH:(h + 1) * DH]
                r = lax.rsqrt(jnp.mean(oh * oh, axis=-1, keepdims=True) + EPS)
                nh_l.append(oh * r)
                r_l.append(r)
            nrm_b = jnp.concatenate(nh_l, axis=-1)
            gon_t = jnp.concatenate([gon] * NH, axis=-1)
            y_b = nrm_b * gon_t * szb
            u, du_dua = _gelu_g(ua)
            gv, dgv_dva = _gelu_g(va)
            xc = gv - jnp.mean(gv, axis=-1, keepdims=True)
            rs_ln = lax.rsqrt(jnp.mean(xc * xc, axis=-1, keepdims=True) + EPS)
            vhat = xc * rs_ln
            v = vhat * lng + lnb
            s_sp = jnp.concatenate(
                [_mm(wsp[g], v[:, g * DH:(g + 1) * DH]) + bspb[g] for g in range(NH)], axis=-1)
            sza, dsza = _silu_g(za)
            y_a = u * s_sp * sza
            a_pr = _mm(y_a, wpa[...])
            b_pr = _mm(y_b, wpb[...])
            sga = _sigmoid(ga)
            sgb = _sigmoid(gb)
            merged = sga * a_pr + sgb * b_pr
            out = _mm(merged, wout[...])
            rs_o = lax.rsqrt(jnp.mean(out * out, axis=-1, keepdims=True) + EPS)
            n_o = out * rs_o
            rr = n_o * gpost
            diff = x_ref[...] + gate * rr - t_ref[...]
            vec_ref[5:6, :] += jnp.sum(diff * diff, axis=0, keepdims=True)
            dy = diff * (1.0 / D)
            dy_ref[...] = dy
            vec_ref[0:1, :] += jnp.sum(dy * rr, axis=0, keepdims=True)
            dr = dy * gate
            vec_ref[1:2, :] += jnp.sum(dr * n_o, axis=0, keepdims=True)
            dn_o = dr * gpost
            dout = rs_o * (dn_o - n_o * jnp.mean(dn_o * n_o, axis=-1, keepdims=True))
            dmerged = _mm(dout, wout[...], NT)
            d_a = dmerged * sga
            d_b = dmerged * sgb
            dga = dmerged * a_pr * sga * (1.0 - sga)
            dgb = dmerged * b_pr * sgb * (1.0 - sgb)
            dy_a = _mm(d_a, wpa[...], NT)
            dy_b = _mm(d_b, wpb[...], NT)
            ya_ref[...] = y_a.astype(ya_ref.dtype)
            yb_ref[...] = y_b.astype(yb_ref.dtype)
            mg_ref[...] = merged.astype(mg_ref.dtype)
            da_ref[...] = d_a.astype(da_ref.dtype)
            db_ref[...] = d_b.astype(db_ref.dtype)
            dout_ref[...] = dout.astype(dout_ref.dtype)
            dua = dy_a * s_sp * sza * du_dua
            ds_sp = dy_a * u * sza
            dza = dy_a * u * s_sp * dsza
            dv_l = []
            for g in range(NH):
                ds_g = ds_sp[:, g * DH:(g + 1) * DH]
                dv_l.append(_mm(wspt[g], ds_g))
                dwsp_ref[g] += _mm(ds_g, v[:, g * DH:(g + 1) * DH], NT)
                dbsp_ref[g] += ds_g
            dv = jnp.concatenate(dv_l, axis=-1)
            vec_ref[2:3, :] += jnp.sum(dv * vhat, axis=0, keepdims=True)
            vec_ref[3:4, :] += jnp.sum(dv, axis=0, keepdims=True)
            dvh = dv * lng
            dgv = rs_ln * (dvh - jnp.mean(dvh, axis=-1, keepdims=True) - vhat * jnp.mean(dvh * vhat, axis=-1, keepdims=True))
            dva = dgv * dgv_dva
            dzb = dy_b * nrm_b * gon_t * dszb
            dgon_full = jnp.sum(dy_b * nrm_b * szb, axis=0, keepdims=True)
            dgon = dgon_full[:, 0:DH]
            for h in range(1, NH):
                dgon = dgon + dgon_full[:, h * DH:(h + 1) * DH]
            vec_ref[4:5, 0:DH] += dgon
            dnb = dy_b * gon_t * szb
            do_l = []
            for h in range(NH):
                sl = slice(h * DH, (h + 1) * DH)
                dn_h = dnb[:, sl]
                do_l.append(r_l[h] * (dn_h - nh_l[h] * jnp.mean(dn_h * nh_l[h], axis=-1, keepdims=True)))
            do_ref[...] = jnp.concatenate(do_l, axis=-1)
            for j, val in enumerate((dzb, dua, dva, dza, dga, dgb)):
                dp_ref[:, j * D:(j + 1) * D] = val.astype(dp_ref.dtype)

    xrow = lambda i: (jnp.maximum(i - nct, 0), 0)
    wspec = _full((D, D))
    gspec = _full((NH, GC, GC))
    vspec = _full((1, D))
    bf_out = _sds((l, D), _BF)
    return pl.pallas_call(
        body, name="post", grid=(lt // tm,),
        out_shape=(_sds((lt, NMAIN), _BF), _sds((lt, D)), _sds((l, D)), bf_out, bf_out, bf_out, bf_out, bf_out, bf_out,
                   _sds((NH, GC, GC)), _sds((NH, GC, GC)), _sds((8, D))),
        in_specs=[pl.BlockSpec((tm, NREST), lambda i: (i, 0)), pl.BlockSpec((tm, D), lambda i: (i, 0)),
                  pl.BlockSpec((tm, D), lambda i: (i, 0)), pl.BlockSpec((tm, D), xrow), pl.BlockSpec((tm, D), xrow),
                  wspec, wspec, wspec, gspec, gspec, gspec, vspec, vspec, _full((1, DH)), vspec, vspec],
        out_specs=(pl.BlockSpec((tm, NREST), lambda i: (i, 0)), pl.BlockSpec((tm, D), lambda i: (i, 0)),
                   pl.BlockSpec((tm, D), xrow), pl.BlockSpec((tm, D), xrow), pl.BlockSpec((tm, D), xrow),
                   pl.BlockSpec((tm, D), xrow), pl.BlockSpec((tm, D), xrow), pl.BlockSpec((tm, D), xrow),
                   pl.BlockSpec((tm, D), xrow), gspec, gspec, _full((8, D))),
        compiler_params=_params(("arbitrary",)),
    )(p, o_f, o_b, x, tgt, w_pa, w_pb, w_out, w_sp, w_spt, b_spb, ln_g, ln_b, g_on, g_post, gate_x)


def _sum_parts(parts, name):
    r = parts.shape[1]
    tr = _tile(r, (512, 256, 128, 64, 32, 16, 8))

    def body(p_ref, o_ref):
        acc = p_ref[0]
        for s in range(1, NDEV):
            acc = acc + p_ref[s]
        o_ref[...] = acc

    return pl.pallas_call(
        body, name=name, out_shape=_sds((r, LANE)), grid=(r // tr,),
        in_specs=[pl.BlockSpec((NDEV, tr, LANE), lambda i: (0, i, 0))],
        out_specs=pl.BlockSpec((tr, LANE), lambda i: (i, 0)),
        compiler_params=_params(("parallel",)),
    )(parts)


def _mod_bwd(c_all, c_ctx, dmx, dmc, w_mod_g):
    ws = w_mod_g.shape[2]

    def body(ca_ref, cc_ref, dsh_ref, dmx_ref, dmc_ref, dmc_sh_ref, w_ref, gw_ref, gc_ref, gb_ref):
        sc, _ = _silu_g(ca_ref[...])
        scc, dscc = _silu_g(cc_ref[...])
        dmc_tot = jnp.sum(dmc_ref[...], axis=0, keepdims=True)
        gb_ref[...] = jnp.sum(dmx_ref[...], axis=0, keepdims=True) + dmc_tot
        lhs = jnp.concatenate([sc, jnp.broadcast_to(scc, (8, D))], axis=0)
        rhs = jnp.concatenate([dsh_ref[...], dmc_sh_ref[...]], axis=0)
        gw_ref[...] = _mmh(lhs, rhs, TN)
        acc = jnp.zeros((8, D), F32)
        tot8 = jnp.broadcast_to(dmc_tot, (8, 3 * D))
        for j in range(NDEV):
            acc = acc + _mm(tot8[:, j * ws:(j + 1) * ws], w_ref[j], NT)
        gc_ref[...] = acc[0:1, :] * dscc

    return pl.pallas_call(
        body, name="mod_bwd", out_shape=(_sds((D, ws)), _sds((1, D)), _sds((1, 3 * D))),
        compiler_params=_params(),
    )(c_all, c_ctx, _my_cols(dmx, ws), dmx, dmc, _my_cols(dmc, ws), w_mod_g)


def _my_cols(a, ws):
    me = 4 * lax.axis_index("x") + 2 * lax.axis_index("y") + lax.axis_index("c")
    return lax.dynamic_slice_in_dim(a, me * ws, ws, axis=1)


def _adamw(parts, w, m, v, name):
    s_, r, c = parts.shape
    tr = _tile(r, (128, 64, 32, 16, 8)) if r * c * 4 > (1 << 20) else r
    c1 = 1.0 / (1.0 - ADAM_B1 ** ADAM_STEP)
    c2 = 1.0 / (1.0 - ADAM_B2 ** ADAM_STEP)

    def body(p_ref, w_ref, m_ref, v_ref, g_ref, d_ref, nm_ref, nv_ref):
        g = p_ref[0].astype(F32)
        for s in range(1, s_):
            g = g + p_ref[s].astype(F32)
        m_new = ADAM_B1 * m_ref[...] + (1.0 - ADAM_B1) * g
        v_new = ADAM_B2 * v_ref[...] + (1.0 - ADAM_B2) * (g * g)
        g_ref[...] = g
        nm_ref[...] = m_new
        nv_ref[...] = v_new
        d_ref[...] = -ADAM_LR * ((m_new * c1) / (jnp.sqrt(v_new * c2) + ADAM_EPS) + ADAM_WD * w_ref[...])

    blk = pl.BlockSpec((tr, c), lambda i: (i, 0))
    o = _sds((r, c))
    return pl.pallas_call(
        body, name=name, out_shape=(o, o, o, o), grid=(r // tr,),
        in_specs=[pl.BlockSpec((s_, tr, c), lambda i: (0, i, 0)), blk, blk, blk],
        out_specs=(blk, blk, blk, blk),
        compiler_params=_params(("parallel",)),
    )(parts, w, m, v)


def _rows(a):
    flat = a.reshape(-1)
    n = flat.shape[0]
    r = -(-n // (8 * LANE)) * 8
    return jnp.pad(flat, (0, r * LANE - n)).reshape(r, LANE)


def _pack(items):
    parts, layout, at = [], [], 0
    for name, a in items:
        rws = _rows(a.astype(F32))
        layout.append((name, at, rws.shape[0], a.shape))
        parts.append(rws)
        at += rws.shape[0]
    return jnp.concatenate(parts, axis=0), layout


def _unpack(packed, layout):
    out = {}
    for name, at, r, shape in layout:
        n = 1
        for s in shape:
            n *= s
        out[name] = packed[at:at + r].reshape(-1)[:n].reshape(shape)
    return out


def kernel(x, c, ctx, c_ctx, w_mod, b_mod, g_pre, g_post, w_in, w_conv, a_log, dt_bias, g_onorm, gm_ln_g, gm_ln_b, w_sp, b_sp, w_pa, w_pb, w_out, loss_target, m_c_ctx, m_w_mod, m_b_mod, m_g_pre, m_g_post, m_w_in, m_w_conv, m_a_log, m_dt_bias, m_g_onorm, m_gm_ln_g, m_gm_ln_b, m_w_sp, m_b_sp, m_w_pa, m_w_pb, m_w_out, v_c_ctx, v_w_mod, v_b_mod, v_g_pre, v_g_post, v_w_in, v_w_conv, v_a_log, v_dt_bias, v_g_onorm, v_gm_ln_g, v_gm_ln_b, v_w_sp, v_b_sp, v_w_pa, v_w_pb, v_w_out):
    l = x.shape[1]
    lc = ctx.shape[1]
    lt = l + lc
    nch = lt // CH
    me = 4 * lax.axis_index("x") + 2 * lax.axis_index("y") + lax.axis_index("c")
    wsh = w_in.shape[2]
    n_in = NDEV * wsh
    off_a = 3 * D
    n_ab = 4 * NH

    wg_in, wg_mod, wg_pa, wg_pb, wg_out, wg_conv, c_all = _exchange(
        [w_in[0].astype(_BF), w_mod[0].astype(_BF), w_pa[0].astype(_BF), w_pb[0].astype(_BF), w_out[0].astype(_BF),
         w_conv[0], c], scatter=False, name="gather_weights")
    w_full = jnp.moveaxis(wg_in, 0, 1).reshape(D, n_in)
    w_main = jnp.concatenate([w_full[:, off_a + n_ab:], w_full[:, :off_a]], axis=1)
    w_ab = jnp.pad(w_full[:, off_a:off_a + n_ab], ((0, 0), (0, LANE - n_ab)))
    wf_pa, wf_pb, wf_out = wg_pa.reshape(D, D), wg_pb.reshape(D, D), wg_out.reshape(D, D)
    wconv_full = jnp.moveaxis(wg_conv, 0, 1).reshape(3, 3 * D)
    c_all = c_all.reshape(NDEV, D)

    cc = jnp.concatenate([c, c_ctx.reshape(1, D), jnp.zeros((6, D), F32)], axis=0)
    mods = _modulation(cc, wg_mod, b_mod)
    xa = jnp.concatenate([ctx[0], x[0]], axis=0)
    h = _prenorm(xa, mods, g_pre, lc)
    p = _matmul_nn(h, w_main, "in_proj")
    pab = _matmul_nn(h, w_ab, "in_proj_ab")
    abt = jnp.swapaxes(pab[:, :n_ab].reshape(nch, CH, n_ab), 1, 2)
    alog16, dtb16 = a_log.reshape(1, 2 * NH), dt_bias.reshape(1, 2 * NH)
    alog_r = jnp.pad(alog16, ((0, 0), (0, LANE - 2 * NH)))
    dtb_r = jnp.pad(dtb16, ((0, 0), (0, LANE - 2 * NH)))
    alog_c = jnp.pad(alog16.reshape(2 * NH, 1), ((0, 2 * NH), (0, 0)))
    dtb_c = jnp.pad(dtb16.reshape(2 * NH, 1), ((0, 2 * NH), (0, 0)))
    qkv = _qkv_fwd(p, wconv_full, lc)
    o_f, o_b, s_f, s_b, t_f, t_b = _gdn_fwd(qkv, pab, abt, alog_r, dtb_r, alog_c, dtb_c, lc)

    w_spt = jnp.swapaxes(w_sp[0], 1, 2)
    b_spb = jnp.broadcast_to(b_sp[0][:, :, None], (NH, GC, GC))
    gate_x = mods[0:1, 2 * D:]
    dp, do, dy, ya, yb, mg, d_a, d_b, dout, dwsp, dbsp_l, pvec = _post(
        p, o_f, o_b, x[0], loss_target[0], wf_pa, wf_pb, wf_out, w_sp[0], w_spt, b_spb, gm_ln_g, gm_ln_b,
        g_onorm, g_post, gate_x, lc)

    dqkv_f, dqkv_b, dcol_f, dcol_b, drow_f, drow_b, gvec_c, gvec_r = _gdn_bwd(
        qkv, pab, abt, alog_r, dtb_r, alog_c, dtb_c, s_f, s_b, t_f, t_b, do, lc)
    dp, dwconv = _qkv_bwd(p, wconv_full, dqkv_f, dqkv_b, dp, lc)
    drow = jnp.swapaxes(drow_f + drow_b, 1, 2).reshape(lt, n_ab)
    dpab = (dcol_f + dcol_b + jnp.pad(drow, ((0, 0), (0, LANE - n_ab)))).astype(_BF)
    dh = _dh_matmul(dp, dpab, w_main, w_ab)
    grad_x, nvec = _prenorm_bwd(xa, dh, dy, mods, g_pre, lc)

    dw_main = _matmul_tn(h, dp, "dw_in")
    dw_ab = _matmul_tn(h, dpab, "dw_in_ab")
    dw_full = jnp.concatenate([dw_main[:, NREST:], dw_ab[:, :n_ab], dw_main[:, :NREST]], axis=1)
    dw_chunks = jnp.moveaxis(dw_full.reshape(D, NDEV, wsh), 1, 0).astype(_BF)
    dw_pa = _matmul_tn(ya, d_a, "dw_pa").reshape(NDEV, D // NDEV, D).astype(_BF)
    dw_pb = _matmul_tn(yb, d_b, "dw_pb").reshape(NDEV, D // NDEV, D).astype(_BF)
    dw_out = _matmul_tn(mg, dout, "dw_out").reshape(NDEV, D // NDEV, D).astype(_BF)
    r_in, r_pa, r_pb, r_out = _exchange([dw_chunks, dw_pa, dw_pb, dw_out], scatter=True, name="scatter_grads")

    dalog = gvec_c[0, :2 * NH] + gvec_r[:2 * NH, 0]
    ddtb = gvec_c[1, :2 * NH] + gvec_r[:2 * NH, 1]
    dmx = jnp.concatenate([nvec[0], nvec[1], pvec[0]])
    dmc = jnp.concatenate([nvec[2], nvec[3], jnp.zeros((D,), F32)])
    small, lay = _pack([
        ("g_pre", nvec[4]), ("g_post", pvec[1]), ("a_log", dalog), ("dt_bias", ddtb), ("g_onorm", pvec[4, :DH]),
        ("gm_ln_g", pvec[2]), ("gm_ln_b", pvec[3]), ("w_sp", dwsp), ("b_sp", jnp.sum(dbsp_l, axis=-1)),
        ("w_conv", dwconv), ("loss", pvec[5]), ("dmx", dmx), ("dmc", dmc)])
    (small_all,) = _exchange([small], scatter=False, name="gather_small")
    tot = _unpack(_sum_parts(small_all, "sum_small"), lay)
    each = {nm: small_all[:, at:at + r].reshape(NDEV, -1) for nm, at, r, _ in lay if nm in ("dmx", "dmc")}
    g_wmod, g_cctx, g_bmod = _mod_bwd(c_all, c_ctx.reshape(1, D), each["dmx"], each["dmc"], wg_mod)
    loss = 0.5 / D * jnp.sum(tot["loss"])
    ws_conv = w_conv.shape[2]
    g_wconv = lax.dynamic_slice_in_dim(tot["w_conv"], me * ws_conv, ws_conv, axis=1)

    small_names = ["c_ctx", "b_mod", "g_pre", "g_post", "a_log", "dt_bias", "g_onorm", "gm_ln_g", "gm_ln_b",
                   "w_sp", "b_sp", "w_conv"]
    wts = dict(c_ctx=c_ctx, b_mod=b_mod, g_pre=g_pre, g_post=g_post, a_log=a_log, dt_bias=dt_bias, g_onorm=g_onorm,
               gm_ln_g=gm_ln_g, gm_ln_b=gm_ln_b, w_sp=w_sp, b_sp=b_sp, w_conv=w_conv)
    ms = dict(c_ctx=m_c_ctx, b_mod=m_b_mod, g_pre=m_g_pre, g_post=m_g_post, a_log=m_a_log, dt_bias=m_dt_bias,
              g_onorm=m_g_onorm, gm_ln_g=m_gm_ln_g, gm_ln_b=m_gm_ln_b, w_sp=m_w_sp, b_sp=m_b_sp, w_conv=m_w_conv)
    vs = dict(c_ctx=v_c_ctx, b_mod=v_b_mod, g_pre=v_g_pre, g_post=v_g_post, a_log=v_a_log, dt_bias=v_dt_bias,
              g_onorm=v_g_onorm, gm_ln_g=v_gm_ln_g, gm_ln_b=v_gm_ln_b, w_sp=v_w_sp, b_sp=v_b_sp, w_conv=v_w_conv)
    gs = dict(tot)
    gs.update(c_ctx=g_cctx, b_mod=g_bmod, w_conv=g_wconv)
    gpk, play = _pack([(nm, gs[nm].reshape(wts[nm].shape)) for nm in small_names])
    wpk, _ = _pack([(nm, wts[nm]) for nm in small_names])
    mpk, _ = _pack([(nm, ms[nm]) for nm in small_names])
    vpk, _ = _pack([(nm, vs[nm]) for nm in small_names])
    res_small = [_unpack(a, play) for a in _adamw(gpk[None], wpk, mpk, vpk, "adamw_small")]
    res_big = {
        "w_mod": _adamw(g_wmod[None], w_mod[0], m_w_mod[0], v_w_mod[0], "adamw_w_mod"),
        "w_in": _adamw(r_in, w_in[0], m_w_in[0], v_w_in[0], "adamw_w_in"),
        "w_pa": _adamw(r_pa, w_pa[0], m_w_pa[0], v_w_pa[0], "adamw_w_pa"),
        "w_pb": _adamw(r_pb, w_pb[0], m_w_pb[0], v_w_pb[0], "adamw_w_pb"),
        "w_out": _adamw(r_out, w_out[0], m_w_out[0], v_w_out[0], "adamw_w_out"),
    }
    order = ["c_ctx", "w_mod", "b_mod", "g_pre", "g_post", "w_in", "w_conv", "a_log", "dt_bias", "g_onorm",
             "gm_ln_g", "gm_ln_b", "w_sp", "b_sp", "w_pa", "w_pb", "w_out"]
    outs = [loss, grad_x[None]]
    for k in range(4):
        for nm in order:
            if nm in res_big:
                outs.append(res_big[nm][k][None])
            else:
                outs.append(res_small[k][nm])
    return tuple(outs)
```

```python
import functools

import jax
import jax.numpy as jnp
from jax import lax
from jax.experimental import pallas as pl
from jax.experimental.pallas import tpu as pltpu

F32 = jnp.float32
_BF = jnp.bfloat16
_HI = lax.Precision.HIGHEST
D = 1024
NH = 8
DH = 128
CH = 64
GC = 128
NREST = 6 * D
NMAIN = NREST + 3 * D
EPS = 1e-6
LANE = 128
NDEV = 8
VMEM_LIMIT = 56 * 1024 * 1024
MESH = pl.DeviceIdType.MESH

ADAM_LR, ADAM_B1, ADAM_B2, ADAM_EPS, ADAM_WD, ADAM_STEP = 0.001, 0.9, 0.999, 1e-08, 0.01, 10

NN = ((1,), (0,))
NT = ((1,), (1,))
TN = ((0,), (0,))


def _dot(a, b, dims=NN, prec=None):
    return lax.dot_general(a, b, (dims, ((), ())), precision=prec, preferred_element_type=F32)


def _mm(a, b, dims=NN):
    return _dot(a.astype(_BF), b.astype(_BF), dims)


def _mmh(a, b, dims=NN):
    return _dot(a.astype(F32), b.astype(F32), dims, _HI)


def _split(a):
    hi = a.astype(_BF)
    return hi, (a - hi.astype(F32)).astype(_BF)


def _mm3(a, b, dims=NN):
    ah, al = _split(a)
    bh, bl = _split(b)
    return _dot(ah, bh, dims) + (_dot(ah, bl, dims) + _dot(al, bh, dims))


def _sigmoid(x):
    return 1.0 / (1.0 + jnp.exp(-x))


def _silu_g(x):
    s = _sigmoid(x)
    return x * s, s * (1.0 + x * (1.0 - s))


def _gelu_g(x):
    c = 0.7978845608028654
    t = jnp.tanh(c * (x + 0.044715 * (x * x * x)))
    cdf = 0.5 * (1.0 + t)
    return x * cdf, cdf + 0.5 * x * (1.0 - t * t) * c * (1.0 + 3 * 0.044715 * x * x)


def _softplus(x):
    return jnp.maximum(x, 0.0) + jnp.log(1.0 + jnp.exp(-jnp.abs(x)))


def _params(sem=None):
    return pltpu.CompilerParams(dimension_semantics=sem, vmem_limit_bytes=VMEM_LIMIT)


def _tile(n, pref):
    for t in pref:
        if n % t == 0:
            return t
    return n


def _full(shape):
    nd = len(shape)
    return pl.BlockSpec(shape, lambda *_: (0,) * nd)


def _sds(shape, dt=F32):
    return jax.ShapeDtypeStruct(shape, dt)


class _Exchange:
    def __init__(self, specs, split):
        self.specs = list(specs)
        self.split = split
        self.n = len(self.specs)
        self.out_shape = tuple(
            _sds(((NDEV,) + tuple(a.shape)) if k.startswith("gather") else ((NDEV,) + tuple(a.shape[1:])), a.dtype)
            for a, k in self.specs)
        self.scratch = [pltpu.SemaphoreType.DMA((self.n, NDEV - 1)), pltpu.SemaphoreType.DMA((self.n, NDEV - 1)),
                        pltpu.SemaphoreType.DMA((self.n,))]

    def _ok(self, kind, idx):
        if kind.endswith("_lo"):
            return idx < self.split
        if kind.endswith("_hi"):
            return idx >= self.split
        return True

    def _plan(self, ins, outs, sems):
        send_sems, recv_sems, loc_sems = sems
        x, y, c = lax.axis_index("x"), lax.axis_index("y"), lax.axis_index("c")
        me = 4 * x + 2 * y + c
        plan = []

        def src(a, kind, idx):
            if kind.startswith("gather"):
                return ins[a]
            base = self.split if kind.endswith("_hi") else 0
            return ins[a].at[jnp.clip(idx - base, 0, ins[a].shape[0] - 1)]

        for a, (_, kind) in enumerate(self.specs):
            cp = pltpu.make_async_copy(src(a, kind, me), outs[a].at[me], loc_sems.at[a])
            plan.append((self._ok(kind, me), cp.start, cp.wait))
        for k in range(1, NDEV):
            px = 1 - x if (k >> 2) & 1 else x
            py = 1 - y if (k >> 1) & 1 else y
            pc = 1 - c if k & 1 else c
            pidx = 4 * px + 2 * py + pc
            for a, (_, kind) in enumerate(self.specs):
                gather = kind.startswith("gather")
                sems_k = dict(send_sem=send_sems.at[a, k - 1], recv_sem=recv_sems.at[a, k - 1],
                              device_id=(px, py, pc), device_id_type=MESH)
                send = pltpu.make_async_remote_copy(src_ref=src(a, kind, pidx), dst_ref=outs[a].at[me], **sems_k)
                arrive = pltpu.make_async_remote_copy(src_ref=src(a, kind, pidx), dst_ref=outs[a].at[pidx], **sems_k)
                plan.append((self._ok(kind, me if gather else pidx), send.start, send.wait_send))
                plan.append((self._ok(kind, pidx if gather else me), None, arrive.wait_recv))
        return plan

    @staticmethod
    def _when(cond, fn):
        if fn is None:
            return
        if cond is True:
            fn()
        else:
            pl.when(cond)(fn)

    def start(self, ins, outs, sems):
        for cond, start, _ in self._plan(ins, outs, sems):
            self._when(cond, start)

    def wait(self, ins, outs, sems):
        for cond, _, wait in self._plan(ins, outs, sems):
            self._when(cond, wait)


_ANY = pl.BlockSpec(memory_space=pl.ANY)


def _exchange(arrays, kinds, name, split=0, into=None):
    xc = _Exchange(zip(arrays, kinds), split)
    n = xc.n
    into = into or {}
    ni = len(into)

    def body(*refs):
        ins, outs, sems = refs[:n], refs[n + ni:2 * n + ni], refs[2 * n + ni:]
        xc.start(ins, outs, sems)
        xc.wait(ins, outs, sems)

    return pl.pallas_call(
        body, name=name, out_shape=xc.out_shape, in_specs=[_ANY] * (n + ni), out_specs=tuple([_ANY] * n),
        scratch_shapes=xc.scratch, input_output_aliases={n + t: a for t, a in enumerate(into)},
    )(*arrays, *into.values())


def _matmul_nn(a, b, name):
    m, kk = a.shape
    n = b.shape[1]
    tm = _tile(m, (1088, 1024, 640, 512, 256, 128))
    tn = _tile(n, (512, 256, 128))

    def body(a_ref, b_ref, o_ref):
        o_ref[...] = _mm(a_ref[...], b_ref[...])

    return pl.pallas_call(
        body, name=name, out_shape=_sds((m, n)), grid=(n // tn, m // tm),
        in_specs=[pl.BlockSpec((tm, kk), lambda j, i: (i, 0)), pl.BlockSpec((kk, tn), lambda j, i: (0, j))],
        out_specs=pl.BlockSpec((tm, tn), lambda j, i: (i, j)),
        compiler_params=_params(("parallel", "parallel")),
    )(a, b)


def _matmul_tn(a, b, name):
    kk, m = a.shape
    n = b.shape[1]
    tk = _tile(kk, (1088, 1024, 640, 512, 256, 128))
    tn = _tile(n, (1024, 512, 256, 128))
    nk = kk // tk

    def body(a_ref, b_ref, o_ref, acc_ref):
        k = pl.program_id(1)

        @pl.when(k == 0)
        def _():
            acc_ref[...] = jnp.zeros_like(acc_ref)

        acc_ref[...] += _mm(a_ref[...], b_ref[...], TN)

        @pl.when(k == nk - 1)
        def _():
            o_ref[...] = acc_ref[...].astype(o_ref.dtype)

    return pl.pallas_call(
        body, name=name, out_shape=_sds((m, n), _BF), grid=(n // tn, nk),
        in_specs=[pl.BlockSpec((tk, m), lambda j, k: (k, 0)), pl.BlockSpec((tk, tn), lambda j, k: (k, j))],
        out_specs=pl.BlockSpec((m, tn), lambda j, k: (0, j)),
        scratch_shapes=[pltpu.VMEM((m, tn), F32)],
        compiler_params=_params(("parallel", "arbitrary")),
    )(a, b)


def _dh_matmul(dp_rest, dp_qkv, dpab, w_rest, w_qkv, w_ab):
    lt = dp_rest.shape[0]
    tm = _tile(lt, (1088, 1024, 640, 512, 256, 128))
    nr, nq = dp_rest.shape[1] // D, dp_qkv.shape[1] // D

    def body(dr_ref, dq_ref, ab_ref, wr_ref, wq_ref, wab_ref, o_ref):
        k = pl.program_id(1)

        @pl.when(k == 0)
        def _():
            o_ref[...] = _mm(ab_ref[...], wab_ref[...], NT)

        @pl.when(k < nr)
        def _():
            o_ref[...] += _mm(dr_ref[...], wr_ref[...], NT)

        @pl.when(k >= nr)
        def _():
            o_ref[...] += _mm(dq_ref[...], wq_ref[...], NT)

    rk = lambda k: jnp.minimum(k, nr - 1)
    qk = lambda k: jnp.maximum(k - nr, 0)
    return pl.pallas_call(
        body, name="dh_matmul", out_shape=_sds((lt, D)), grid=(lt // tm, nr + nq),
        in_specs=[pl.BlockSpec((tm, D), lambda i, k: (i, rk(k))), pl.BlockSpec((tm, D), lambda i, k: (i, qk(k))),
                  pl.BlockSpec((tm, LANE), lambda i, k: (i, 0)),
                  pl.BlockSpec((D, D), lambda i, k: (0, rk(k))), pl.BlockSpec((D, D), lambda i, k: (0, qk(k))),
                  _full((D, LANE))],
        out_specs=pl.BlockSpec((tm, D), lambda i, k: (i, 0)),
        compiler_params=_params(("parallel", "arbitrary")),
    )(dp_rest, dp_qkv, dpab, w_rest, w_qkv, w_ab)


def _modulation(cc, w_mod_g, b_mod):
    ws = w_mod_g.shape[2]

    def body(c_ref, w_ref, b_ref, o_ref):
        s, _ = _silu_g(c_ref[...])
        o_ref[...] = _mm(s, w_ref[0]) + b_ref[...]

    return pl.pallas_call(
        body, name="modulation", out_shape=_sds((8, 3 * D)), grid=(NDEV,),
        in_specs=[_full((8, D)), pl.BlockSpec((1, D, ws), lambda j: (j, 0, 0)), pl.BlockSpec((1, ws), lambda j: (0, j))],
        out_specs=pl.BlockSpec((8, ws), lambda j: (0, j)),
        compiler_params=_params(("parallel",)),
    )(cc, w_mod_g, b_mod)


def _prenorm(xa, mods, g_pre, lc):
    lt = xa.shape[0]
    tm = _tile(lc, (256, 128))
    nct = lc // tm

    def body(x_ref, m_ref, g_ref, o_ref):
        x = x_ref[...]
        is_ctx = pl.program_id(0) < nct
        shift = jnp.where(is_ctx, m_ref[1:2, 0:D], m_ref[0:1, 0:D])
        scale = jnp.where(is_ctx, m_ref[1:2, D:2 * D], m_ref[0:1, D:2 * D])
        r = lax.rsqrt(jnp.mean(x * x, axis=-1, keepdims=True) + EPS)
        o_ref[...] = ((x * r * g_ref[...]) * (1.0 + scale) + shift).astype(o_ref.dtype)

    return pl.pallas_call(
        body, name="prenorm", out_shape=_sds((lt, D), _BF), grid=(lt // tm,),
        in_specs=[pl.BlockSpec((tm, D), lambda i: (i, 0)), _full((8, 3 * D)), _full((1, D))],
        out_specs=pl.BlockSpec((tm, D), lambda i: (i, 0)),
        compiler_params=_params(("parallel",)),
    )(xa, mods, g_pre)


def _prenorm_bwd(xa, dh, dy, mods, g_pre, lc):
    lt = xa.shape[0]
    tm = _tile(lc, (256, 128))
    nct = lc // tm
    nl = (lt - lc) // tm

    def body(x_ref, dh_ref, dy_ref, m_ref, g_ref, gx_ref, vec_ref):
        i = pl.program_id(0)

        @pl.when(i == 0)
        def _():
            vec_ref[...] = jnp.zeros_like(vec_ref)

        x = x_ref[...]
        dh = dh_ref[...]
        g = g_ref[...]
        is_ctx = i < nct
        scale = jnp.where(is_ctx, m_ref[1:2, D:2 * D], m_ref[0:1, D:2 * D])
        r = lax.rsqrt(jnp.mean(x * x, axis=-1, keepdims=True) + EPS)
        n = x * r
        hn = n * g
        dsh = jnp.sum(dh, axis=0, keepdims=True)
        dsc = jnp.sum(dh * hn, axis=0, keepdims=True)
        dhn = dh * (1.0 + scale)
        vec_ref[4:5, :] += jnp.sum(dhn * n, axis=0, keepdims=True)
        dn = dhn * g
        dx = r * (dn - n * jnp.mean(dn * n, axis=-1, keepdims=True))

        @pl.when(is_ctx)
        def _():
            vec_ref[2:3, :] += dsh
            vec_ref[3:4, :] += dsc

        @pl.when(jnp.logical_not(is_ctx))
        def _():
            vec_ref[0:1, :] += dsh
            vec_ref[1:2, :] += dsc
            gx_ref[...] = dy_ref[...] + dx

    xrow = lambda i: (jnp.maximum(i - nct, 0), 0)
    return pl.pallas_call(
        body, name="prenorm_bwd", out_shape=(_sds((nl * tm, D)), _sds((8, D))), grid=(lt // tm,),
        in_specs=[pl.BlockSpec((tm, D), lambda i: (i, 0)), pl.BlockSpec((tm, D), lambda i: (i, 0)),
                  pl.BlockSpec((tm, D), xrow), _full((8, 3 * D)), _full((1, D))],
        out_specs=(pl.BlockSpec((tm, D), xrow), _full((8, D))),
        compiler_params=_params(("arbitrary",)),
    )(xa, dh, dy, mods, g_pre)


def _conv_parts(x, w, lc):
    lt = x.shape[0]
    row = lax.broadcasted_iota(jnp.int32, x.shape, 0)
    first = (row == 0) | (row == lc)
    last = (row == lc - 1) | (row == lt - 1)
    xp = jnp.where(first, 0.0, pltpu.roll(x, 1, 0))
    xn = jnp.where(last, 0.0, pltpu.roll(x, lt - 1, 0))
    y = w[0:1, :] * xp + w[1:2, :] * x + w[2:3, :] * xn
    return xp, xn, y, first, last


def _qkv_fwd(p, w_conv, lc):
    lt = p.shape[0]

    def body(p_ref, w_ref, o_ref):
        _, _, y, _, _ = _conv_parts(p_ref[...], w_ref[...], lc)
        s, _ = _silu_g(y)
        rs = lax.rsqrt(jnp.sum(s * s, axis=-1, keepdims=True) + EPS)
        o_ref[...] = s * jnp.where(pl.program_id(0) < 2 * NH, rs, 1.0)

    return pl.pallas_call(
        body, name="qkv_fwd", out_shape=_sds((lt, 3 * D)), grid=(3 * NH,),
        in_specs=[pl.BlockSpec((lt, DH), lambda j: (0, j)), pl.BlockSpec((3, DH), lambda j: (0, j))],
        out_specs=pl.BlockSpec((lt, DH), lambda j: (0, j)),
        compiler_params=_params(("parallel",)),
    )(p, w_conv)


def _qkv_bwd(p, w_conv, dqkv_f, dqkv_b, lc):
    lt = p.shape[0]

    def body(p_ref, w_ref, df_ref, db_ref, dp_ref, dw_ref):
        w = w_ref[...]
        xp, xn, y, first, last = _conv_parts(p_ref[...], w, lc)
        s, ds_dy = _silu_g(y)
        dn = df_ref[...] + db_ref[...]
        rs = lax.rsqrt(jnp.sum(s * s, axis=-1, keepdims=True) + EPS)
        nrm = s * rs
        ds_n = rs * (dn - nrm * jnp.sum(dn * nrm, axis=-1, keepdims=True))
        ds = jnp.where(pl.program_id(0) < 2 * NH, ds_n, dn)
        dy = ds * ds_dy
        dw_ref[0:1, :] = jnp.sum(dy * xp, axis=0, keepdims=True)
        dw_ref[1:2, :] = jnp.sum(dy * p_ref[...], axis=0, keepdims=True)
        dw_ref[2:3, :] = jnp.sum(dy * xn, axis=0, keepdims=True)
        dyn = jnp.where(last, 0.0, pltpu.roll(dy, lt - 1, 0))
        dyp = jnp.where(first, 0.0, pltpu.roll(dy, 1, 0))
        dp_ref[...] = (w[1:2, :] * dy + w[0:1, :] * dyn + w[2:3, :] * dyp).astype(dp_ref.dtype)

    return pl.pallas_call(
        body, name="qkv_bwd", out_shape=(_sds((lt, 3 * D), _BF), _sds((3, 3 * D))), grid=(3 * NH,),
        in_specs=[pl.BlockSpec((lt, DH), lambda j: (0, j)), pl.BlockSpec((3, DH), lambda j: (0, j)),
                  pl.BlockSpec((lt, DH), lambda j: (0, j)), pl.BlockSpec((lt, DH), lambda j: (0, j))],
        out_specs=(pl.BlockSpec((lt, DH), lambda j: (0, j)), pl.BlockSpec((3, DH), lambda j: (0, j))),
        compiler_params=_params(("parallel",)),
    )(p, w_conv, dqkv_f, dqkv_b)


def _masks(d):
    ri = lax.broadcasted_iota(jnp.int32, (CH, CH), 0)
    ci = lax.broadcasted_iota(jnp.int32, (CH, CH), 1)
    incl = (ri >= ci) if d == 0 else (ri <= ci)
    strict = (ri > ci) if d == 0 else (ri < ci)
    incl_t = (ri <= ci) if d == 0 else (ri >= ci)
    return incl, strict, incl_t, ri == ci


def _decays(d, ab, abt, alog_r, dtb_r, alog_c, dtb_c, incl, incl_t):
    g_full = -jnp.exp(alog_r) * _softplus(ab + dtb_r)
    beta_full = _sigmoid(ab)
    gc_full = _mmh(incl.astype(F32), g_full)
    gl_full = jnp.sum(g_full, axis=0, keepdims=True)
    gt_full = -jnp.exp(alog_c) * _softplus(abt + dtb_c)
    gct = _mmh(gt_full, incl_t.astype(F32))
    return g_full, beta_full, gc_full, gl_full, gt_full, gct


def _lane_onehot(idx, n=LANE):
    return (lax.broadcasted_iota(jnp.int32, (1, n), 1) == idx).astype(F32)


def _head_scalars(d, h, beta_full, gc_full, gl_full, gct):
    idx = d * NH + h
    oh = _lane_onehot(idx)
    gcol = jnp.sum(gc_full * oh, axis=-1, keepdims=True)
    bcol = jnp.sum(beta_full * _lane_onehot(2 * NH + idx), axis=-1, keepdims=True)
    gl = jnp.sum(gl_full * oh, axis=-1, keepdims=True)
    grow = gct[idx:idx + 1, :]
    return gcol, grow, bcol, gl


def _lockstep(gens):
    live = list(gens)
    while live:
        nxt = []
        for g in live:
            try:
                next(g)
                nxt.append(g)
            except StopIteration:
                pass
        live = nxt


def _chunk_local(qh, kh, vh, gcol, grow, bcol, gl, incl, strict):
    decay = jnp.where(incl, jnp.exp(gcol - grow), 0.0)
    kb = kh * bcol
    a = jnp.where(strict, _mm(kb, kh, NT) * decay, 0.0)
    egc = jnp.exp(gcol)
    rhs_u = vh * bcol
    rhs_w = kb * egc
    qs = qh * (DH ** -0.5)
    attn = jnp.where(incl, _mm(qs, kh, NT) * decay, 0.0)
    etail = jnp.exp(gl - gcol)
    return decay, kb, a, egc, rhs_u, rhs_w, qs, attn, etail


def _scan_specs(lt, lc, bwd_pass):
    nch = lt // CH
    ncc = lc // CH
    if not bwd_pass:
        cf = lambda s: s
        cb = lambda s: jnp.where(s < ncc, ncc - 1 - s, nch + ncc - 1 - s)
    else:
        cf = lambda s: nch - 1 - s
        cb = lambda s: jnp.where(s < nch - ncc, ncc + s, s - (nch - ncc))
    return nch, cf, cb


def _gdn_fwd(qkv, pab, abt, alog_r, dtb_r, alog_c, dtb_c, lc, xc, xc_arrays):
    lt = qkv.shape[0]
    nch, cf, cb = _scan_specs(lt, lc, False)
    nx = xc.n

    def body(*refs):
        qf, kf, vf, abf, abtf, qb, kb_, vb, abb, abtb, ar, dr, ac, dc = refs[:14]
        x_in = refs[14:14 + nx]
        of_ref, ob_ref, sf_ref, sb_ref, tf_ref, tb_ref = refs[14 + nx:20 + nx]
        x_out = refs[20 + nx:20 + 2 * nx]
        s_scr = refs[20 + 2 * nx]
        sems = refs[21 + 2 * nx:]

        @pl.when(pl.program_id(0) == 0)
        def _():
            s_scr[...] = jnp.zeros_like(s_scr)
            xc.start(x_in, x_out, sems)

        def chain(d, h, q_r, k_r, v_r, o_ref, sh_ref, th_ref, masks, decs):
            incl, strict, _, eye = masks
            sl = slice(h * DH, (h + 1) * DH)
            qh, kh, vh = q_r[:, sl], k_r[:, sl], v_r[:, sl]
            gcol, grow, bcol, gl = _head_scalars(d, h, *decs)
            _, _, a, egc, rhs_u, rhs_w, qs, attn, etail = _chunk_local(qh, kh, vh, gcol, grow, bcol, gl, incl, strict)
            yield
            n = -a
            t = jnp.where(eye, 1.0, 0.0) + n
            p = _mm3(n, n)
            yield
            for _ in range(4):
                r = _mm3(jnp.concatenate([t, p], axis=0), p)
                yield
                t = t + r[:CH]
                p = r[CH:]
            t = t + _mm3(t, p)
            yield
            sol = _mm3(t, jnp.concatenate([rhs_u, rhs_w], axis=1))
            u, w = sol[:, :DH], sol[:, DH:]
            s = s_scr[d, h]
            sh_ref[0, h] = s
            th_ref[0, h] = t
            yield
            ws = _mm(jnp.concatenate([w, qs * egc], axis=0), s)
            yield
            v_new = u - ws[:CH]
            o_ref[:, sl] = ws[CH:] + _mm(attn, v_new)
            s_scr[d, h] = s * jnp.exp(gl) + _mm(kh * etail, v_new, TN)

        chains = []
        for d, (q_r, k_r, v_r, ab_r, abt_r, o_ref, sh_ref, th_ref) in enumerate(
                ((qf, kf, vf, abf, abtf, of_ref, sf_ref, tf_ref), (qb, kb_, vb, abb, abtb, ob_ref, sb_ref, tb_ref))):
            masks = _masks(d)
            _, beta_full, gc_full, gl_full, _, gct = _decays(
                d, ab_r[...], abt_r[0], ar[...], dr[...], ac[...], dc[...], masks[0], masks[2])
            for h in range(NH):
                chains.append(chain(d, h, q_r, k_r, v_r, o_ref, sh_ref, th_ref, masks, (beta_full, gc_full, gl_full, gct)))
        _lockstep(chains)

        @pl.when(pl.program_id(0) == nch - 1)
        def _():
            xc.wait(x_in, x_out, sems)

    def row(c, col):
        return pl.BlockSpec((CH, D), lambda s: (c(s), col))

    def chunk_in(c):
        return [row(c, 0), row(c, 1), row(c, 2), pl.BlockSpec((CH, LANE), lambda s: (c(s), 0)),
                pl.BlockSpec((1, 4 * NH, CH), lambda s: (c(s), 0, 0))]

    def hist(c, n):
        return pl.BlockSpec((1, NH, n, n), lambda s: (c(s), 0, 0, 0))

    small = [_full((1, LANE)), _full((1, LANE)), _full((4 * NH, 1)), _full((4 * NH, 1))]
    return pl.pallas_call(
        body, name="gdn_fwd", grid=(nch,),
        out_shape=(_sds((lt, D)), _sds((lt, D)), _sds((nch, NH, DH, DH)), _sds((nch, NH, DH, DH)),
                   _sds((nch, NH, CH, CH)), _sds((nch, NH, CH, CH))) + xc.out_shape,
        in_specs=chunk_in(cf) + chunk_in(cb) + small + [_ANY] * nx,
        out_specs=(pl.BlockSpec((CH, D), lambda s: (cf(s), 0)), pl.BlockSpec((CH, D), lambda s: (cb(s), 0)),
                   hist(cf, DH), hist(cb, DH), hist(cf, CH), hist(cb, CH)) + tuple([_ANY] * nx),
        scratch_shapes=[pltpu.VMEM((2, NH, DH, DH), F32)] + xc.scratch,
        compiler_params=_params(("arbitrary",)),
    )(qkv, qkv, qkv, pab, abt, qkv, qkv, qkv, pab, abt, alog_r, dtb_r, alog_c, dtb_c, *xc_arrays)


def _gdn_bwd(qkv, pab, abt, alog_r, dtb_r, alog_c, dtb_c, s_f, s_b, t_f, t_b, do, lc, xc, xc_arrays):
    lt = qkv.shape[0]
    nch, cf, cb = _scan_specs(lt, lc, True)
    nx = xc.n

    def body(*refs):
        qf, kf, vf, abf, abtf, sf_ref, tf_ref, dof, qb, kb_, vb, abb, abtb, sb_ref, tb_ref, dob, ar, dr, ac, dc = refs[:20]
        x_in = refs[20:20 + nx]
        dqf_ref, dqb_ref, dcf_ref, dcb_ref, drf_ref, drb_ref, vcol_ref, vrow_ref = refs[20 + nx:28 + nx]
        x_out = refs[28 + nx:28 + 2 * nx]
        ds_scr = refs[28 + 2 * nx]
        sems = refs[29 + 2 * nx:]

        @pl.when(pl.program_id(0) == 0)
        def _():
            ds_scr[...] = jnp.zeros_like(ds_scr)
            vcol_ref[...] = jnp.zeros_like(vcol_ref)
            vrow_ref[...] = jnp.zeros_like(vrow_ref)
            xc.start(x_in, x_out, sems)

        alog_r_, dtb_r_, alog_c_, dtb_c_ = ar[...], dr[...], ac[...], dc[...]
        lane2 = lax.broadcasted_iota(jnp.int32, (1, LANE), 1)
        acc = [[], []]

        def chain(d, h, q_r, k_r, v_r, sh_ref, th_ref, do_r, dq_ref, masks, decs):
            incl, strict, _, _ = masks
            idx = d * NH + h
            sl = slice(h * DH, (h + 1) * DH)
            qh, kh, vh = q_r[:, sl], k_r[:, sl], v_r[:, sl]
            doh = do_r[:, sl]
            gcol, grow, bcol, gl = _head_scalars(d, h, *decs)
            decay, kb, a, egc, rhs_u, rhs_w, qs, attn, etail = _chunk_local(qh, kh, vh, gcol, grow, bcol, gl, incl, strict)
            t = th_ref[0, h]
            s = sh_ref[0, h]
            ds_new = ds_scr[d, h]
            sol = _mm3(t, jnp.concatenate([rhs_u, rhs_w], axis=1))
            u, w = sol[:, :DH], sol[:, DH:]
            q_dec = qs * egc
            k_tail = kh * etail
            egl = jnp.exp(gl)
            dv_new = _mm(attn, doh, TN) + _mm(k_tail, ds_new)
            dq_dec = _mm(doh, s, NT)
            dgl = jnp.sum(jnp.sum(ds_new * s, axis=0, keepdims=True), axis=-1, keepdims=True) * egl
            yield
            v_new = u - _mm(w, s)
            dw = -_mm(dv_new, s, NT)
            ds_scr[d, h] = ds_new * egl + _mm(q_dec, doh, TN) - _mm(w, dv_new, TN)
            yield
            dattn = jnp.where(incl, _mm(doh, v_new, NT), 0.0)
            dk_tail = _mm(v_new, ds_new, NT)
            dr = _mm3(t, jnp.concatenate([dv_new, dw], axis=1), TN)
            dr_u, dr_w = dr[:, :DH], dr[:, DH:]
            yield
            da = -jnp.where(strict, _mm3(dr, sol, NT), 0.0)
            nq = dattn * decay
            dqs = _mm(nq, kh) + dq_dec * egc
            dk = _mm(nq, qs, TN)
            yield
            dv = dr_u * bcol
            dbeta = jnp.sum(dr_u * vh, axis=-1, keepdims=True)
            dgc = jnp.sum(dr_w * rhs_w, axis=-1, keepdims=True)
            m = da * decay
            dkb = dr_w * egc + _mm(m, kh)
            dk = dk + _mm(m, kb, TN)
            pq = da * a + dattn * attn
            dgc = dgc + jnp.sum(pq, axis=-1, keepdims=True) + jnp.sum(dq_dec * q_dec, axis=-1, keepdims=True)
            dgr = -jnp.sum(pq, axis=0, keepdims=True)
            tt = jnp.sum(dk_tail * k_tail, axis=-1, keepdims=True)
            dk = dk + dk_tail * etail + dkb * bcol
            dgc = dgc - tt
            dgl = dgl + jnp.sum(tt, axis=0, keepdims=True)
            dbeta = dbeta + jnp.sum(dkb * kh, axis=-1, keepdims=True)
            dq_ref[:, sl] = dqs * (DH ** -0.5)
            dq_ref[:, D + h * DH:D + (h + 1) * DH] = dk
            dq_ref[:, 2 * D + h * DH:2 * D + (h + 1) * DH] = dv
            acc[d].append((idx, dgc, dgl, dbeta, dgr))

        dirs = ((qf, kf, vf, abf, abtf, sf_ref, tf_ref, dof, dqf_ref, dcf_ref, drf_ref),
                (qb, kb_, vb, abb, abtb, sb_ref, tb_ref, dob, dqb_ref, dcb_ref, drb_ref))
        chains, ctx_d = [], []
        for d, (q_r, k_r, v_r, ab_r, abt_r, sh_ref, th_ref, do_r, dq_ref, _, _) in enumerate(dirs):
            masks = _masks(d)
            ab, abt = ab_r[...], abt_r[0]
            g_full, beta_full, gc_full, gl_full, gt_full, gct = _decays(
                d, ab, abt, alog_r_, dtb_r_, alog_c_, dtb_c_, masks[0], masks[2])
            ctx_d.append((masks, ab, abt, g_full, beta_full, gt_full))
            for h in range(NH):
                chains.append(chain(d, h, q_r, k_r, v_r, sh_ref, th_ref, do_r, dq_ref, masks,
                                    (beta_full, gc_full, gl_full, gct)))
        _lockstep(chains)
        for d in range(2):
            (incl, _, incl_t, _), ab, abt, g_full, beta_full, gt_full = ctx_d[d]
            dcol_ref, drow_ref = dirs[d][9], dirs[d][10]
            dgc_col = jnp.zeros((CH, LANE), F32)
            dgl_row = jnp.zeros((1, LANE), F32)
            dbeta_col = jnp.zeros((CH, LANE), F32)
            dgc_row = jnp.zeros((4 * NH, CH), F32)
            for idx, dgc, dgl, dbeta, dgr in acc[d]:
                oh = _lane_onehot(idx)
                dgc_col = dgc_col + dgc * oh
                dgl_row = dgl_row + dgl * oh
                dbeta_col = dbeta_col + dbeta * _lane_onehot(2 * NH + idx)
                ohc = (lax.broadcasted_iota(jnp.int32, (4 * NH, 1), 0) == idx).astype(F32)
                dgc_row = dgc_row + ohc * dgr
            dg_col = _mmh(incl_t.astype(F32), dgc_col) + dgl_row
            dg_row = _mmh(dgc_row, incl.astype(F32))
            sg_col = _sigmoid(ab + dtb_r_)
            da_col = dg_col * (-jnp.exp(alog_r_)) * sg_col
            dcol_ref[...] = da_col + dbeta_col * beta_full * (1.0 - beta_full)
            da_row = dg_row * (-jnp.exp(alog_c_)) * _sigmoid(abt + dtb_c_)
            drow_ref[0] = da_row
            vcol_ref[0:1, :] += jnp.sum(dg_col * g_full, axis=0, keepdims=True)
            vcol_ref[1:2, :] += jnp.sum(da_col, axis=0, keepdims=True)
            rl = jnp.sum(dg_row * gt_full, axis=-1, keepdims=True)
            rd = jnp.sum(da_row, axis=-1, keepdims=True)
            vrow_ref[...] += jnp.where(lane2 == 0, rl, 0.0) + jnp.where(lane2 == 1, rd, 0.0)

        @pl.when(pl.program_id(0) == nch - 1)
        def _():
            xc.wait(x_in, x_out, sems)

    def row(c, col):
        return pl.BlockSpec((CH, D), lambda s: (c(s), col))

    def hist(c, n):
        return pl.BlockSpec((1, NH, n, n), lambda s: (c(s), 0, 0, 0))

    def chunk_in(c):
        return [row(c, 0), row(c, 1), row(c, 2), pl.BlockSpec((CH, LANE), lambda s: (c(s), 0)),
                pl.BlockSpec((1, 4 * NH, CH), lambda s: (c(s), 0, 0)), hist(c, DH), hist(c, CH), row(c, 0)]

    small = [_full((1, LANE)), _full((1, LANE)), _full((4 * NH, 1)), _full((4 * NH, 1))]
    return pl.pallas_call(
        body, name="gdn_bwd", grid=(nch,),
        out_shape=(_sds((lt, 3 * D)), _sds((lt, 3 * D)), _sds((lt, LANE)), _sds((lt, LANE)),
                   _sds((nch, 4 * NH, CH)), _sds((nch, 4 * NH, CH)), _sds((8, LANE)), _sds((4 * NH, LANE))) + xc.out_shape,
        in_specs=chunk_in(cf) + chunk_in(cb) + small + [_ANY] * nx,
        out_specs=(pl.BlockSpec((CH, 3 * D), lambda s: (cf(s), 0)), pl.BlockSpec((CH, 3 * D), lambda s: (cb(s), 0)),
                   pl.BlockSpec((CH, LANE), lambda s: (cf(s), 0)), pl.BlockSpec((CH, LANE), lambda s: (cb(s), 0)),
                   pl.BlockSpec((1, 4 * NH, CH), lambda s: (cf(s), 0, 0)), pl.BlockSpec((1, 4 * NH, CH), lambda s: (cb(s), 0, 0)),
                   _full((8, LANE)), _full((4 * NH, LANE))) + tuple([_ANY] * nx),
        scratch_shapes=[pltpu.VMEM((2, NH, DH, DH), F32)] + xc.scratch,
        compiler_params=_params(("arbitrary",)),
    )(qkv, qkv, qkv, pab, abt, s_f, t_f, do, qkv, qkv, qkv, pab, abt, s_b, t_b, do, alog_r, dtb_r, alog_c, dtb_c,
      *xc_arrays)


def _post(p, o_f, o_b, x, tgt, w_pa, w_pb, w_out, w_sp, w_spt, b_spb, ln_g, ln_b, g_on, g_post, gate_x, lc):
    lt = p.shape[0]
    l = x.shape[0]
    tm = GC
    nct = lc // tm

    def body(p_ref, of_ref, ob_ref, x_ref, t_ref, wpa, wpb, wout, wsp, wspt, bspb, lng_ref, lnb_ref, gon_ref, gpost_ref, gate_ref,
             dp_ref, do_ref, dy_ref, ya_ref, yb_ref, mg_ref, da_ref, db_ref, dout_ref, dwsp_ref, dbsp_ref, vec_ref):
        i = pl.program_id(0)

        @pl.when(i == 0)
        def _():
            dwsp_ref[...] = jnp.zeros_like(dwsp_ref)
            dbsp_ref[...] = jnp.zeros_like(dbsp_ref)
            vec_ref[...] = jnp.zeros_like(vec_ref)

        @pl.when(i < nct)
        def _():
            dp_ref[...] = jnp.zeros_like(dp_ref)
            do_ref[...] = jnp.zeros_like(do_ref)

        @pl.when(i >= nct)
        def _():
            lng, lnb, gon, gpost, gate = lng_ref[...], lnb_ref[...], gon_ref[...], gpost_ref[...], gate_ref[...]
            zb, ua, va, za, ga, gb = [p_ref[:, j * D:(j + 1) * D] for j in range(6)]
            o = of_ref[...] + ob_ref[...]
            szb, dszb = _silu_g(zb)
            nh_l, r_l = [], []
            for h in range(NH):
                oh = o[:, h * DH:(h + 1) * DH]
                r = lax.rsqrt(jnp.mean(oh * oh, axis=-1, keepdims=True) + EPS)
                nh_l.append(oh * r)
                r_l.append(r)
            nrm_b = jnp.concatenate(nh_l, axis=-1)
            gon_t = jnp.concatenate([gon] * NH, axis=-1)
            y_b = nrm_b * gon_t * szb
            u, du_dua = _gelu_g(ua)
            gv, dgv_dva = _gelu_g(va)
            xc = gv - jnp.mean(gv, axis=-1, keepdims=True)
            rs_ln = lax.rsqrt(jnp.mean(xc * xc, axis=-1, keepdims=True) + EPS)
            vhat = xc * rs_ln
            v = vhat * lng + lnb
            s_sp = jnp.concatenate(
                [_mm(wsp[g], v[:, g * DH:(g + 1) * DH]) + bspb[g] for g in range(NH)], axis=-1)
            sza, dsza = _silu_g(za)
            y_a = u * s_sp * sza
            a_pr = _mm(y_a, wpa[...])
            b_pr = _mm(y_b, wpb[...])
            sga = _sigmoid(ga)
            sgb = _sigmoid(gb)
            merged = sga * a_pr + sgb * b_pr
            out = _mm(merged, wout[...])
            rs_o = lax.rsqrt(jnp.mean(out * out, axis=-1, keepdims=True) + EPS)
            n_o = out * rs_o
            rr = n_o * gpost
            diff = x_ref[...] + gate * rr - t_ref[...]
            vec_ref[5:6, :] += jnp.sum(diff * diff, axis=0, keepdims=True)
            dy = diff * (1.0 / D)
            dy_ref[...] = dy
            vec_ref[0:1, :] += jnp.sum(dy * rr, axis=0, keepdims=True)
            dr = dy * gate
            vec_ref[1:2, :] += jnp.sum(dr * n_o, axis=0, keepdims=True)
            dn_o = dr * gpost
            dout = rs_o * (dn_o - n_o * jnp.mean(dn_o * n_o, axis=-1, keepdims=True))
            dmerged = _mm(dout, wout[...], NT)
            d_a = dmerged * sga
            d_b = dmerged * sgb
            dga = dmerged * a_pr * sga * (1.0 - sga)
            dgb = dmerged * b_pr * sgb * (1.0 - sgb)
            dy_a = _mm(d_a, wpa[...], NT)
            dy_b = _mm(d_b, wpb[...], NT)
            ya_ref[...] = y_a.astype(ya_ref.dtype)
            yb_ref[...] = y_b.astype(yb_ref.dtype)
            mg_ref[...] = merged.astype(mg_ref.dtype)
            da_ref[...] = d_a.astype(da_ref.dtype)
            db_ref[...] = d_b.astype(db_ref.dtype)
            dout_ref[...] = dout.astype(dout_ref.dtype)
            dua = dy_a * s_sp * sza * du_dua
            ds_sp = dy_a * u * sza
            dza = dy_a * u * s_sp * dsza
            dv_l = []
            for g in range(NH):
                ds_g = ds_sp[:, g * DH:(g + 1) * DH]
                dv_l.append(_mm(wspt[g], ds_g))
                dwsp_ref[g] += _mm(ds_g, v[:, g * DH:(g + 1) * DH], NT)
                dbsp_ref[g] += ds_g
            dv = jnp.concatenate(dv_l, axis=-1)
            vec_ref[2:3, :] += jnp.sum(dv * vhat, axis=0, keepdims=True)
            vec_ref[3:4, :] += jnp.sum(dv, axis=0, keepdims=True)
            dvh = dv * lng
            dgv = rs_ln * (dvh - jnp.mean(dvh, axis=-1, keepdims=True) - vhat * jnp.mean(dvh * vhat, axis=-1, keepdims=True))
            dva = dgv * dgv_dva
            dzb = dy_b * nrm_b * gon_t * dszb
            dgon_full = jnp.sum(dy_b * nrm_b * szb, axis=0, keepdims=True)
            dgon = dgon_full[:, 0:DH]
            for h in range(1, NH):
                dgon = dgon + dgon_full[:, h * DH:(h + 1) * DH]
            vec_ref[4:5, 0:DH] += dgon
            dnb = dy_b * gon_t * szb
            do_l = []
            for h in range(NH):
                sl = slice(h * DH, (h + 1) * DH)
                dn_h = dnb[:, sl]
                do_l.append(r_l[h] * (dn_h - nh_l[h] * jnp.mean(dn_h * nh_l[h], axis=-1, keepdims=True)))
            do_ref[...] = jnp.concatenate(do_l, axis=-1)
            for j, val in enumerate((dzb, dua, dva, dza, dga, dgb)):
                dp_ref[:, j * D:(j + 1) * D] = val.astype(dp_ref.dtype)

    xrow = lambda i: (jnp.maximum(i - nct, 0), 0)
    wspec = _full((D, D))
    gspec = _full((NH, GC, GC))
    vspec = _full((1, D))
    bf_out = _sds((l, D), _BF)
    return pl.pallas_call(
        body, name="post", grid=(lt // tm,),
        out_shape=(_sds((lt, NREST), _BF), _sds((lt, D)), _sds((l, D)), bf_out, bf_out, bf_out, bf_out, bf_out, bf_out,
                   _sds((NH, GC, GC)), _sds((NH, GC, GC)), _sds((8, D))),
        in_specs=[pl.BlockSpec((tm, NREST), lambda i: (i, 0)), pl.BlockSpec((tm, D), lambda i: (i, 0)),
                  pl.BlockSpec((tm, D), lambda i: (i, 0)), pl.BlockSpec((tm, D), xrow), pl.BlockSpec((tm, D), xrow),
                  wspec, wspec, wspec, gspec, gspec, gspec, vspec, vspec, _full((1, DH)), vspec, vspec],
        out_specs=(pl.BlockSpec((tm, NREST), lambda i: (i, 0)), pl.BlockSpec((tm, D), lambda i: (i, 0)),
                   pl.BlockSpec((tm, D), xrow), pl.BlockSpec((tm, D), xrow), pl.BlockSpec((tm, D), xrow),
                   pl.BlockSpec((tm, D), xrow), pl.BlockSpec((tm, D), xrow), pl.BlockSpec((tm, D), xrow),
                   pl.BlockSpec((tm, D), xrow), gspec, gspec, _full((8, D))),
        compiler_params=_params(("arbitrary",)),
    )(p, o_f, o_b, x, tgt, w_pa, w_pb, w_out, w_sp, w_spt, b_spb, ln_g, ln_b, g_on, g_post, gate_x)


def _sum_parts(parts, name):
    r = parts.shape[1]
    tr = _tile(r, (512, 256, 128, 64, 32, 16, 8))

    def body(p_ref, o_ref):
        acc = p_ref[0]
        for s in range(1, NDEV):
            acc = acc + p_ref[s]
        o_ref[...] = acc

    return pl.pallas_call(
        body, name=name, out_shape=_sds((r, LANE)), grid=(r // tr,),
        in_specs=[pl.BlockSpec((NDEV, tr, LANE), lambda i: (0, i, 0))],
        out_specs=pl.BlockSpec((tr, LANE), lambda i: (i, 0)),
        compiler_params=_params(("parallel",)),
    )(parts)


def _mod_bwd(c_all, c_ctx, dmx, dmc, w_mod_g):
    ws = w_mod_g.shape[2]

    def body(ca_ref, cc_ref, dsh_ref, dmx_ref, dmc_ref, dmc_sh_ref, w_ref, gw_ref, gc_ref, gb_ref):
        sc, _ = _silu_g(ca_ref[...])
        scc, dscc = _silu_g(cc_ref[...])
        dmc_tot = jnp.sum(dmc_ref[...], axis=0, keepdims=True)
        gb_ref[...] = jnp.sum(dmx_ref[...], axis=0, keepdims=True) + dmc_tot
        lhs = jnp.concatenate([sc, jnp.broadcast_to(scc, (8, D))], axis=0)
        rhs = jnp.concatenate([dsh_ref[...], dmc_sh_ref[...]], axis=0)
        gw_ref[...] = _mmh(lhs, rhs, TN)
        acc = jnp.zeros((8, D), F32)
        tot8 = jnp.broadcast_to(dmc_tot, (8, 3 * D))
        for j in range(NDEV):
            acc = acc + _mm(tot8[:, j * ws:(j + 1) * ws], w_ref[j], NT)
        gc_ref[...] = acc[0:1, :] * dscc

    return pl.pallas_call(
        body, name="mod_bwd", out_shape=(_sds((D, ws)), _sds((1, D)), _sds((1, 3 * D))),
        compiler_params=_params(),
    )(c_all, c_ctx, _my_cols(dmx, ws), dmx, dmc, _my_cols(dmc, ws), w_mod_g)


def _my_cols(a, ws):
    me = 4 * lax.axis_index("x") + 2 * lax.axis_index("y") + lax.axis_index("c")
    return lax.dynamic_slice_in_dim(a, me * ws, ws, axis=1)


def _adamw(parts, w, m, v, name):
    s_, r, c = parts.shape
    tr = _tile(r, (128, 64, 32, 16, 8)) if r * c * 4 > (1 << 20) else r
    c1 = 1.0 / (1.0 - ADAM_B1 ** ADAM_STEP)
    c2 = 1.0 / (1.0 - ADAM_B2 ** ADAM_STEP)

    def body(p_ref, w_ref, m_ref, v_ref, g_ref, d_ref, nm_ref, nv_ref):
        g = p_ref[0].astype(F32)
        for s in range(1, s_):
            g = g + p_ref[s].astype(F32)
        m_new = ADAM_B1 * m_ref[...] + (1.0 - ADAM_B1) * g
        v_new = ADAM_B2 * v_ref[...] + (1.0 - ADAM_B2) * (g * g)
        g_ref[...] = g
        nm_ref[...] = m_new
        nv_ref[...] = v_new
        d_ref[...] = -ADAM_LR * ((m_new * c1) / (jnp.sqrt(v_new * c2) + ADAM_EPS) + ADAM_WD * w_ref[...])

    blk = pl.BlockSpec((tr, c), lambda i: (i, 0))
    o = _sds((r, c))
    return pl.pallas_call(
        body, name=name, out_shape=(o, o, o, o), grid=(r // tr,),
        in_specs=[pl.BlockSpec((s_, tr, c), lambda i: (0, i, 0)), blk, blk, blk],
        out_specs=(blk, blk, blk, blk),
        compiler_params=_params(("parallel",)),
    )(parts, w, m, v)


def _rows(a):
    flat = a.reshape(-1)
    n = flat.shape[0]
    r = -(-n // (8 * LANE)) * 8
    return jnp.pad(flat, (0, r * LANE - n)).reshape(r, LANE)


def _pack(items):
    parts, layout, at = [], [], 0
    for name, a in items:
        rws = _rows(a.astype(F32))
        layout.append((name, at, rws.shape[0], a.shape))
        parts.append(rws)
        at += rws.shape[0]
    return jnp.concatenate(parts, axis=0), layout


def _unpack(packed, layout):
    out = {}
    for name, at, r, shape in layout:
        n = 1
        for s in shape:
            n *= s
        out[name] = packed[at:at + r].reshape(-1)[:n].reshape(shape)
    return out


def kernel(x, c, ctx, c_ctx, w_mod, b_mod, g_pre, g_post, w_in, w_conv, a_log, dt_bias, g_onorm, gm_ln_g, gm_ln_b, w_sp, b_sp, w_pa, w_pb, w_out, loss_target, m_c_ctx, m_w_mod, m_b_mod, m_g_pre, m_g_post, m_w_in, m_w_conv, m_a_log, m_dt_bias, m_g_onorm, m_gm_ln_g, m_gm_ln_b, m_w_sp, m_b_sp, m_w_pa, m_w_pb, m_w_out, v_c_ctx, v_w_mod, v_b_mod, v_g_pre, v_g_post, v_w_in, v_w_conv, v_a_log, v_dt_bias, v_g_onorm, v_gm_ln_g, v_gm_ln_b, v_w_sp, v_b_sp, v_w_pa, v_w_pb, v_w_out):
    l = x.shape[1]
    lc = ctx.shape[1]
    lt = l + lc
    nch = lt // CH
    me = 4 * lax.axis_index("x") + 2 * lax.axis_index("y") + lax.axis_index("c")
    wsh = w_in.shape[2]
    off_a = 3 * D
    n_ab = 4 * NH
    jb = off_a // wsh
    o1 = off_a - jb * wsh
    o2 = o1 + n_ab
    assert o2 <= wsh and NREST == (NDEV - jb) * wsh - o2
    split = jb + 1

    w_in_bf = w_in[0].astype(_BF)
    wg_lo, wg_mod, wg_conv, c_all = _exchange(
        [w_in_bf, w_mod[0].astype(_BF), w_conv[0], c], ["gather_lo", "gather", "gather", "gather"],
        "gather_first", split)
    w_qkv = jnp.concatenate([wg_lo[j] for j in range(jb)] + [wg_lo[jb][:, :o1]], axis=1)
    w_ab = jnp.pad(wg_lo[jb][:, o1:o2], ((0, 0), (0, LANE - n_ab)))
    wconv_full = jnp.moveaxis(wg_conv, 0, 1).reshape(3, 3 * D)
    c_all = c_all.reshape(NDEV, D)

    cc = jnp.concatenate([c, c_ctx.reshape(1, D), jnp.zeros((6, D), F32)], axis=0)
    mods = _modulation(cc, wg_mod, b_mod)
    xa = jnp.concatenate([ctx[0], x[0]], axis=0)
    h = _prenorm(xa, mods, g_pre, lc)
    p_qkv = _matmul_nn(h, w_qkv, "in_proj_qkv")
    pab = _matmul_nn(h, w_ab, "in_proj_ab")
    abt = jnp.swapaxes(pab[:, :n_ab].reshape(nch, CH, n_ab), 1, 2)
    alog16, dtb16 = a_log.reshape(1, 2 * NH), dt_bias.reshape(1, 2 * NH)
    alog_r = jnp.pad(alog16, ((0, 0), (0, LANE - 2 * NH)))
    dtb_r = jnp.pad(dtb16, ((0, 0), (0, LANE - 2 * NH)))
    alog_c = jnp.pad(alog16.reshape(2 * NH, 1), ((0, 2 * NH), (0, 0)))
    dtb_c = jnp.pad(dtb16.reshape(2 * NH, 1), ((0, 2 * NH), (0, 0)))
    qkv = _qkv_fwd(p_qkv, wconv_full, lc)
    late = [w_in_bf, w_pa[0].astype(_BF), w_pb[0].astype(_BF), w_out[0].astype(_BF)]
    xc_late = _Exchange(zip(late, ["gather_hi", "gather", "gather", "gather"]), split)
    o_f, o_b, s_f, s_b, t_f, t_b, wg_hi, wg_pa, wg_pb, wg_out = _gdn_fwd(
        qkv, pab, abt, alog_r, dtb_r, alog_c, dtb_c, lc, xc_late, late)
    w_rest = jnp.concatenate([wg_lo[jb][:, o2:]] + [wg_hi[j] for j in range(split, NDEV)], axis=1)
    wf_pa, wf_pb, wf_out = wg_pa.reshape(D, D), wg_pb.reshape(D, D), wg_out.reshape(D, D)
    p_rest = _matmul_nn(h, w_rest, "in_proj_rest")

    w_spt = jnp.swapaxes(w_sp[0], 1, 2)
    b_spb = jnp.broadcast_to(b_sp[0][:, :, None], (NH, GC, GC))
    gate_x = mods[0:1, 2 * D:]
    dp_rest, do, dy, ya, yb, mg, d_a, d_b, dout, dwsp, dbsp_l, pvec = _post(
        p_rest, o_f, o_b, x[0], loss_target[0], wf_pa, wf_pb, wf_out, w_sp[0], w_spt, b_spb, gm_ln_g, gm_ln_b,
        g_onorm, g_post, gate_x, lc)

    dw_rest = _matmul_tn(h, dp_rest, "dw_in_rest")
    o3 = wsh - o2
    chunks_hi = jnp.moveaxis(dw_rest[:, o3:].reshape(D, NDEV - split, wsh), 1, 0)
    dw_pa = _matmul_tn(ya, d_a, "dw_pa").reshape(NDEV, D // NDEV, D)
    dw_pb = _matmul_tn(yb, d_b, "dw_pb").reshape(NDEV, D // NDEV, D)
    dw_out = _matmul_tn(mg, dout, "dw_out").reshape(NDEV, D // NDEV, D)
    early = [chunks_hi, dw_pa, dw_pb, dw_out]
    xc_early = _Exchange(zip(early, ["scatter_hi", "scatter", "scatter", "scatter"]), split)

    dqkv_f, dqkv_b, dcol_f, dcol_b, drow_f, drow_b, gvec_c, gvec_r, r_in, r_pa, r_pb, r_out = _gdn_bwd(
        qkv, pab, abt, alog_r, dtb_r, alog_c, dtb_c, s_f, s_b, t_f, t_b, do, lc, xc_early, early)
    dp_qkv, dwconv = _qkv_bwd(p_qkv, wconv_full, dqkv_f, dqkv_b, lc)
    drow = jnp.swapaxes(drow_f + drow_b, 1, 2).reshape(lt, n_ab)
    dpab = (dcol_f + dcol_b + jnp.pad(drow, ((0, 0), (0, LANE - n_ab)))).astype(_BF)
    dh = _dh_matmul(dp_rest, dp_qkv, dpab, w_rest, w_qkv, w_ab)
    grad_x, nvec = _prenorm_bwd(xa, dh, dy, mods, g_pre, lc)

    dw_qkv = _matmul_tn(h, dp_qkv, "dw_in_qkv")
    dw_ab = _matmul_tn(h, dpab, "dw_in_ab")
    dw_lo = jnp.concatenate([dw_qkv, dw_ab[:, :n_ab], dw_rest[:, :o3]], axis=1)
    chunks_lo = jnp.moveaxis(dw_lo.reshape(D, split, wsh), 1, 0)
    (r_in,) = _exchange([chunks_lo], ["scatter_lo"], "scatter_last", split, into={0: r_in})

    dalog = gvec_c[0, :2 * NH] + gvec_r[:2 * NH, 0]
    ddtb = gvec_c[1, :2 * NH] + gvec_r[:2 * NH, 1]
    dmx = jnp.concatenate([nvec[0], nvec[1], pvec[0]])
    dmc = jnp.concatenate([nvec[2], nvec[3], jnp.zeros((D,), F32)])
    small, lay = _pack([
        ("g_pre", nvec[4]), ("g_post", pvec[1]), ("a_log", dalog), ("dt_bias", ddtb), ("g_onorm", pvec[4, :DH]),
        ("gm_ln_g", pvec[2]), ("gm_ln_b", pvec[3]), ("w_sp", dwsp), ("b_sp", jnp.sum(dbsp_l, axis=-1)),
        ("w_conv", dwconv), ("loss", pvec[5]), ("dmx", dmx), ("dmc", dmc)])
    (small_all,) = _exchange([small], ["gather"], "gather_small")
    tot = _unpack(_sum_parts(small_all, "sum_small"), lay)
    each = {nm: small_all[:, at:at + r].reshape(NDEV, -1) for nm, at, r, _ in lay if nm in ("dmx", "dmc")}
    g_wmod, g_cctx, g_bmod = _mod_bwd(c_all, c_ctx.reshape(1, D), each["dmx"], each["dmc"], wg_mod)
    loss = 0.5 / D * jnp.sum(tot["loss"])
    ws_conv = w_conv.shape[2]
    g_wconv = lax.dynamic_slice_in_dim(tot["w_conv"], me * ws_conv, ws_conv, axis=1)

    small_names = ["c_ctx", "b_mod", "g_pre", "g_post", "a_log", "dt_bias", "g_onorm", "gm_ln_g", "gm_ln_b",
                   "w_sp", "b_sp", "w_conv"]
    wts = dict(c_ctx=c_ctx, b_mod=b_mod, g_pre=g_pre, g_post=g_post, a_log=a_log, dt_bias=dt_bias, g_onorm=g_onorm,
               gm_ln_g=gm_ln_g, gm_ln_b=gm_ln_b, w_sp=w_sp, b_sp=b_sp, w_conv=w_conv)
    ms = dict(c_ctx=m_c_ctx, b_mod=m_b_mod, g_pre=m_g_pre, g_post=m_g_post, a_log=m_a_log, dt_bias=m_dt_bias,
              g_onorm=m_g_onorm, gm_ln_g=m_gm_ln_g, gm_ln_b=m_gm_ln_b, w_sp=m_w_sp, b_sp=m_b_sp, w_conv=m_w_conv)
    vs = dict(c_ctx=v_c_ctx, b_mod=v_b_mod, g_pre=v_g_pre, g_post=v_g_post, a_log=v_a_log, dt_bias=v_dt_bias,
              g_onorm=v_g_onorm, gm_ln_g=v_gm_ln_g, gm_ln_b=v_gm_ln_b, w_sp=v_w_sp, b_sp=v_b_sp, w_conv=v_w_conv)
    gs = dict(tot)
    gs.update(c_ctx=g_cctx, b_mod=g_bmod, w_conv=g_wconv)
    gpk, play = _pack([(nm, gs[nm].reshape(wts[nm].shape)) for nm in small_names])
    wpk, _ = _pack([(nm, wts[nm]) for nm in small_names])
    mpk, _ = _pack([(nm, ms[nm]) for nm in small_names])
    vpk, _ = _pack([(nm, vs[nm]) for nm in small_names])
    res_small = [_unpack(a, play) for a in _adamw(gpk[None], wpk, mpk, vpk, "adamw_small")]
    res_big = {
        "w_mod": _adamw(g_wmod[None], w_mod[0], m_w_mod[0], v_w_mod[0], "adamw_w_mod"),
        "w_in": _adamw(r_in, w_in[0], m_w_in[0], v_w_in[0], "adamw_w_in"),
        "w_pa": _adamw(r_pa, w_pa[0], m_w_pa[0], v_w_pa[0], "adamw_w_pa"),
        "w_pb": _adamw(r_pb, w_pb[0], m_w_pb[0], v_w_pb[0], "adamw_w_pb"),
        "w_out": _adamw(r_out, w_out[0], m_w_out[0], v_w_out[0], "adamw_w_out"),
    }
    order = ["c_ctx", "w_mod", "b_mod", "g_pre", "g_post", "w_in", "w_conv", "a_log", "dt_bias", "g_onorm",
             "gm_ln_g", "gm_ln_b", "w_sp", "b_sp", "w_pa", "w_pb", "w_out"]
    outs = [loss, grad_x[None]]
    for k in range(4):
        for nm in order:
            if nm in res_big:
                outs.append(res_big[nm][k][None])
            else:
                outs.append(res_small[k][nm])
    return tuple(outs)
```

```python
import functools

import jax
import jax.numpy as jnp
from jax import lax
from jax.experimental import pallas as pl
from jax.experimental.pallas import tpu as pltpu

F32 = jnp.float32
_BF = jnp.bfloat16
_HI = lax.Precision.HIGHEST
D = 1024
NH = 8
DH = 128
CH = 64
GC = 128
NREST = 6 * D
NMAIN = NREST + 3 * D
EPS = 1e-6
LANE = 128
NDEV = 8
VMEM_LIMIT = 56 * 1024 * 1024
MESH = pl.DeviceIdType.MESH

ADAM_LR, ADAM_B1, ADAM_B2, ADAM_EPS, ADAM_WD, ADAM_STEP = 0.001, 0.9, 0.999, 1e-08, 0.01, 10

NN = ((1,), (0,))
NT = ((1,), (1,))
TN = ((0,), (0,))


def _dot(a, b, dims=NN, prec=None):
    return lax.dot_general(a, b, (dims, ((), ())), precision=prec, preferred_element_type=F32)


def _mm(a, b, dims=NN):
    return _dot(a.astype(_BF), b.astype(_BF), dims)


def _mmh(a, b, dims=NN):
    return _dot(a.astype(F32), b.astype(F32), dims, _HI)


def _split(a):
    hi = a.astype(_BF)
    return hi, (a - hi.astype(F32)).astype(_BF)


def _mm3(a, b, dims=NN):
    ah, al = _split(a)
    bh, bl = _split(b)
    return _dot(ah, bh, dims) + (_dot(ah, bl, dims) + _dot(al, bh, dims))


def _sigmoid(x):
    return 1.0 / (1.0 + jnp.exp(-x))


def _silu_g(x):
    s = _sigmoid(x)
    return x * s, s * (1.0 + x * (1.0 - s))


def _gelu_g(x):
    c = 0.7978845608028654
    t = jnp.tanh(c * (x + 0.044715 * (x * x * x)))
    cdf = 0.5 * (1.0 + t)
    return x * cdf, cdf + 0.5 * x * (1.0 - t * t) * c * (1.0 + 3 * 0.044715 * x * x)


def _softplus(x):
    return jnp.maximum(x, 0.0) + jnp.log(1.0 + jnp.exp(-jnp.abs(x)))


def _params(sem=None):
    return pltpu.CompilerParams(dimension_semantics=sem, vmem_limit_bytes=VMEM_LIMIT)


def _tile(n, pref):
    for t in pref:
        if n % t == 0:
            return t
    return n


def _full(shape):
    nd = len(shape)
    return pl.BlockSpec(shape, lambda *_: (0,) * nd)


def _sds(shape, dt=F32):
    return jax.ShapeDtypeStruct(shape, dt)


class _Exchange:
    def __init__(self, specs, split):
        self.specs = list(specs)
        self.split = split
        self.n = len(self.specs)
        self.out_shape = tuple(
            _sds(((NDEV,) + tuple(a.shape)) if k.startswith("gather") else ((NDEV,) + tuple(a.shape[1:])), a.dtype)
            for a, k in self.specs)
        self.scratch = [pltpu.SemaphoreType.DMA((self.n, NDEV - 1)), pltpu.SemaphoreType.DMA((self.n, NDEV - 1)),
                        pltpu.SemaphoreType.DMA((self.n,))]

    def _ok(self, kind, idx):
        if kind.endswith("_lo"):
            return idx < self.split
        if kind.endswith("_hi"):
            return idx >= self.split
        return True

    def _phases(self, ins, outs, sems):
        send_sems, recv_sems, loc_sems = sems
        x, y, c = lax.axis_index("x"), lax.axis_index("y"), lax.axis_index("c")
        me = 4 * x + 2 * y + c
        sib = (x, y, 1 - c)
        sib_idx = 4 * x + 2 * y + (1 - c)
        chips = [(1 - x, y), (x, 1 - y), (1 - x, 1 - y)]
        starts, forwards, waits = [], [], []
        for a, (_, kind) in enumerate(self.specs):
            ok = functools.partial(self._ok, kind)
            if kind.startswith("gather"):
                def copy(k, block, to, src=None, a=a):
                    rows = outs[a].at[block]
                    return pltpu.make_async_remote_copy(
                        src_ref=rows if src is None else src, dst_ref=rows, send_sem=send_sems.at[a, k],
                        recv_sem=recv_sems.at[a, k], device_id=to, device_id_type=MESH)

                loc = pltpu.make_async_copy(ins[a], outs[a].at[me], loc_sems.at[a])
                first = [copy(0, me, sib, ins[a])] + [copy(1 + j, me, (*chip, c), ins[a]) for j, chip in enumerate(chips)]
                starts += [(ok(me), loc.start)] + [(ok(me), cp.start) for cp in first]
                waits += [(ok(me), loc.wait)] + [(ok(me), cp.wait_send) for cp in first]
                for j, chip in enumerate(chips):
                    origin = 4 * chip[0] + 2 * chip[1] + c
                    passed = copy(4 + j, origin, sib)
                    forwards += [(ok(origin), copy(1 + j, origin, sib).wait_recv), (ok(origin), passed.start)]
                    waits.append((ok(origin), passed.wait_send))
                    other = 4 * chip[0] + 2 * chip[1] + (1 - c)
                    waits.append((ok(other), copy(4 + j, other, sib).wait_recv))
                waits.append((ok(sib_idx), copy(0, sib_idx, sib).wait_recv))
            else:
                base = self.split if kind.endswith("_hi") else 0

                def src(idx, a=a, base=base):
                    return ins[a].at[jnp.clip(idx - base, 0, ins[a].shape[0] - 1)]

                loc = pltpu.make_async_copy(src(me), outs[a].at[me], loc_sems.at[a])
                starts.append((ok(me), loc.start))
                waits.append((ok(me), loc.wait))
                for k in range(1, NDEV):
                    px = 1 - x if (k >> 2) & 1 else x
                    py = 1 - y if (k >> 1) & 1 else y
                    pc = 1 - c if k & 1 else c
                    pidx = 4 * px + 2 * py + pc
                    sems_k = dict(send_sem=send_sems.at[a, k - 1], recv_sem=recv_sems.at[a, k - 1],
                                  device_id=(px, py, pc), device_id_type=MESH)
                    send = pltpu.make_async_remote_copy(src_ref=src(pidx), dst_ref=outs[a].at[me], **sems_k)
                    arrive = pltpu.make_async_remote_copy(src_ref=src(pidx), dst_ref=outs[a].at[pidx], **sems_k)
                    starts.append((ok(pidx), send.start))
                    waits += [(ok(pidx), send.wait_send), (ok(me), arrive.wait_recv)]
        return starts, forwards, waits

    @staticmethod
    def _run(actions):
        for cond, fn in actions:
            if cond is True:
                fn()
            else:
                pl.when(cond)(fn)

    def start(self, ins, outs, sems):
        self._run(self._phases(ins, outs, sems)[0])

    def forward(self, ins, outs, sems):
        self._run(self._phases(ins, outs, sems)[1])

    def wait(self, ins, outs, sems):
        self._run(self._phases(ins, outs, sems)[2])


_ANY = pl.BlockSpec(memory_space=pl.ANY)


def _exchange(arrays, kinds, name, split=0, into=None):
    xc = _Exchange(zip(arrays, kinds), split)
    n = xc.n
    into = into or {}
    ni = len(into)

    def body(*refs):
        ins, outs, sems = refs[:n], refs[n + ni:2 * n + ni], refs[2 * n + ni:]
        xc.start(ins, outs, sems)
        xc.forward(ins, outs, sems)
        xc.wait(ins, outs, sems)

    return pl.pallas_call(
        body, name=name, out_shape=xc.out_shape, in_specs=[_ANY] * (n + ni), out_specs=tuple([_ANY] * n),
        scratch_shapes=xc.scratch, input_output_aliases={n + t: a for t, a in enumerate(into)},
    )(*arrays, *into.values())


def _matmul_nn(a, b, name):
    m, kk = a.shape
    n = b.shape[1]
    tm = _tile(m, (1088, 1024, 640, 512, 256, 128))
    tn = _tile(n, (512, 256, 128))

    def body(a_ref, b_ref, o_ref):
        o_ref[...] = _mm(a_ref[...], b_ref[...])

    return pl.pallas_call(
        body, name=name, out_shape=_sds((m, n)), grid=(n // tn, m // tm),
        in_specs=[pl.BlockSpec((tm, kk), lambda j, i: (i, 0)), pl.BlockSpec((kk, tn), lambda j, i: (0, j))],
        out_specs=pl.BlockSpec((tm, tn), lambda j, i: (i, j)),
        compiler_params=_params(("parallel", "parallel")),
    )(a, b)


def _matmul_tn(a, b, name):
    kk, m = a.shape
    n = b.shape[1]
    tk = _tile(kk, (1088, 1024, 640, 512, 256, 128))
    tn = _tile(n, (1024, 512, 256, 128))
    nk = kk // tk

    def body(a_ref, b_ref, o_ref, acc_ref):
        k = pl.program_id(1)

        @pl.when(k == 0)
        def _():
            acc_ref[...] = jnp.zeros_like(acc_ref)

        acc_ref[...] += _mm(a_ref[...], b_ref[...], TN)

        @pl.when(k == nk - 1)
        def _():
            o_ref[...] = acc_ref[...].astype(o_ref.dtype)

    return pl.pallas_call(
        body, name=name, out_shape=_sds((m, n), _BF), grid=(n // tn, nk),
        in_specs=[pl.BlockSpec((tk, m), lambda j, k: (k, 0)), pl.BlockSpec((tk, tn), lambda j, k: (k, j))],
        out_specs=pl.BlockSpec((m, tn), lambda j, k: (0, j)),
        scratch_shapes=[pltpu.VMEM((m, tn), F32)],
        compiler_params=_params(("parallel", "arbitrary")),
    )(a, b)


def _dh_matmul(dp_rest, dp_qkv, dpab, w_rest, w_qkv, w_ab, xc, xc_arrays, xc_into):
    lt = dp_rest.shape[0]
    tm = _tile(lt, (1088, 1024, 640, 512, 256, 128))
    nr, nq = dp_rest.shape[1] // D, dp_qkv.shape[1] // D
    nx, ni = xc.n, len(xc_into)
    ni_steps = lt // tm

    def body(*refs):
        dr_ref, dq_ref, ab_ref, wr_ref, wq_ref, wab_ref = refs[:6]
        x_in = refs[6:6 + nx]
        o_ref = refs[6 + nx + ni]
        x_out = refs[7 + nx + ni:7 + 2 * nx + ni]
        sems = refs[7 + 2 * nx + ni:]
        i = pl.program_id(0)
        k = pl.program_id(1)

        @pl.when((i == 0) & (k == 0))
        def _():
            xc.start(x_in, x_out, sems)

        @pl.when(k == 0)
        def _():
            o_ref[...] = _mm(ab_ref[...], wab_ref[...], NT)

        @pl.when(k < nr)
        def _():
            o_ref[...] += _mm(dr_ref[...], wr_ref[...], NT)

        @pl.when(k >= nr)
        def _():
            o_ref[...] += _mm(dq_ref[...], wq_ref[...], NT)

        @pl.when((i == ni_steps - 1) & (k == nr + nq - 1))
        def _():
            xc.wait(x_in, x_out, sems)

    rk = lambda k: jnp.minimum(k, nr - 1)
    qk = lambda k: jnp.maximum(k - nr, 0)
    return pl.pallas_call(
        body, name="dh_matmul", out_shape=(_sds((lt, D)),) + xc.out_shape, grid=(lt // tm, nr + nq),
        in_specs=[pl.BlockSpec((tm, D), lambda i, k: (i, rk(k))), pl.BlockSpec((tm, D), lambda i, k: (i, qk(k))),
                  pl.BlockSpec((tm, LANE), lambda i, k: (i, 0)),
                  pl.BlockSpec((D, D), lambda i, k: (0, rk(k))), pl.BlockSpec((D, D), lambda i, k: (0, qk(k))),
                  _full((D, LANE))] + [_ANY] * (nx + ni),
        out_specs=(pl.BlockSpec((tm, D), lambda i, k: (i, 0)),) + tuple([_ANY] * nx),
        scratch_shapes=xc.scratch, input_output_aliases={6 + nx + t: 1 + a for t, a in enumerate(xc_into)},
        compiler_params=_params(("arbitrary", "arbitrary")),
    )(dp_rest, dp_qkv, dpab, w_rest, w_qkv, w_ab, *xc_arrays, *xc_into.values())


def _modulation(cc, w_mod_g, b_mod):
    ws = w_mod_g.shape[2]

    def body(c_ref, w_ref, b_ref, o_ref):
        s, _ = _silu_g(c_ref[...])
        o_ref[...] = _mm(s, w_ref[0]) + b_ref[...]

    return pl.pallas_call(
        body, name="modulation", out_shape=_sds((8, 3 * D)), grid=(NDEV,),
        in_specs=[_full((8, D)), pl.BlockSpec((1, D, ws), lambda j: (j, 0, 0)), pl.BlockSpec((1, ws), lambda j: (0, j))],
        out_specs=pl.BlockSpec((8, ws), lambda j: (0, j)),
        compiler_params=_params(("parallel",)),
    )(cc, w_mod_g, b_mod)


def _prenorm(xa, mods, g_pre, lc):
    lt = xa.shape[0]
    tm = _tile(lc, (256, 128))
    nct = lc // tm

    def body(x_ref, m_ref, g_ref, o_ref):
        x = x_ref[...]
        is_ctx = pl.program_id(0) < nct
        shift = jnp.where(is_ctx, m_ref[1:2, 0:D], m_ref[0:1, 0:D])
        scale = jnp.where(is_ctx, m_ref[1:2, D:2 * D], m_ref[0:1, D:2 * D])
        r = lax.rsqrt(jnp.mean(x * x, axis=-1, keepdims=True) + EPS)
        o_ref[...] = ((x * r * g_ref[...]) * (1.0 + scale) + shift).astype(o_ref.dtype)

    return pl.pallas_call(
        body, name="prenorm", out_shape=_sds((lt, D), _BF), grid=(lt // tm,),
        in_specs=[pl.BlockSpec((tm, D), lambda i: (i, 0)), _full((8, 3 * D)), _full((1, D))],
        out_specs=pl.BlockSpec((tm, D), lambda i: (i, 0)),
        compiler_params=_params(("parallel",)),
    )(xa, mods, g_pre)


def _prenorm_bwd(xa, dh, dy, mods, g_pre, lc):
    lt = xa.shape[0]
    tm = _tile(lc, (256, 128))
    nct = lc // tm
    nl = (lt - lc) // tm

    def body(x_ref, dh_ref, dy_ref, m_ref, g_ref, gx_ref, vec_ref):
        i = pl.program_id(0)

        @pl.when(i == 0)
        def _():
            vec_ref[...] = jnp.zeros_like(vec_ref)

        x = x_ref[...]
        dh = dh_ref[...]
        g = g_ref[...]
        is_ctx = i < nct
        scale = jnp.where(is_ctx, m_ref[1:2, D:2 * D], m_ref[0:1, D:2 * D])
        r = lax.rsqrt(jnp.mean(x * x, axis=-1, keepdims=True) + EPS)
        n = x * r
        hn = n * g
        dsh = jnp.sum(dh, axis=0, keepdims=True)
        dsc = jnp.sum(dh * hn, axis=0, keepdims=True)
        dhn = dh * (1.0 + scale)
        vec_ref[4:5, :] += jnp.sum(dhn * n, axis=0, keepdims=True)
        dn = dhn * g
        dx = r * (dn - n * jnp.mean(dn * n, axis=-1, keepdims=True))

        @pl.when(is_ctx)
        def _():
            vec_ref[2:3, :] += dsh
            vec_ref[3:4, :] += dsc

        @pl.when(jnp.logical_not(is_ctx))
        def _():
            vec_ref[0:1, :] += dsh
            vec_ref[1:2, :] += dsc
            gx_ref[...] = dy_ref[...] + dx

    xrow = lambda i: (jnp.maximum(i - nct, 0), 0)
    return pl.pallas_call(
        body, name="prenorm_bwd", out_shape=(_sds((nl * tm, D)), _sds((8, D))), grid=(lt // tm,),
        in_specs=[pl.BlockSpec((tm, D), lambda i: (i, 0)), pl.BlockSpec((tm, D), lambda i: (i, 0)),
                  pl.BlockSpec((tm, D), xrow), _full((8, 3 * D)), _full((1, D))],
        out_specs=(pl.BlockSpec((tm, D), xrow), _full((8, D))),
        compiler_params=_params(("arbitrary",)),
    )(xa, dh, dy, mods, g_pre)


def _conv_parts(x, w, lc):
    lt = x.shape[0]
    row = lax.broadcasted_iota(jnp.int32, x.shape, 0)
    first = (row == 0) | (row == lc)
    last = (row == lc - 1) | (row == lt - 1)
    xp = jnp.where(first, 0.0, pltpu.roll(x, 1, 0))
    xn = jnp.where(last, 0.0, pltpu.roll(x, lt - 1, 0))
    y = w[0:1, :] * xp + w[1:2, :] * x + w[2:3, :] * xn
    return xp, xn, y, first, last


def _qkv_fwd(p, w_conv, lc):
    lt = p.shape[0]

    def body(p_ref, w_ref, o_ref):
        _, _, y, _, _ = _conv_parts(p_ref[...], w_ref[...], lc)
        s, _ = _silu_g(y)
        rs = lax.rsqrt(jnp.sum(s * s, axis=-1, keepdims=True) + EPS)
        o_ref[...] = s * jnp.where(pl.program_id(0) < 2 * NH, rs, 1.0)

    return pl.pallas_call(
        body, name="qkv_fwd", out_shape=_sds((lt, 3 * D)), grid=(3 * NH,),
        in_specs=[pl.BlockSpec((lt, DH), lambda j: (0, j)), pl.BlockSpec((3, DH), lambda j: (0, j))],
        out_specs=pl.BlockSpec((lt, DH), lambda j: (0, j)),
        compiler_params=_params(("parallel",)),
    )(p, w_conv)


def _qkv_bwd(p, w_conv, dqkv_f, dqkv_b, lc):
    lt = p.shape[0]

    def body(p_ref, w_ref, df_ref, db_ref, dp_ref, dw_ref):
        w = w_ref[...]
        xp, xn, y, first, last = _conv_parts(p_ref[...], w, lc)
        s, ds_dy = _silu_g(y)
        dn = df_ref[...] + db_ref[...]
        rs = lax.rsqrt(jnp.sum(s * s, axis=-1, keepdims=True) + EPS)
        nrm = s * rs
        ds_n = rs * (dn - nrm * jnp.sum(dn * nrm, axis=-1, keepdims=True))
        ds = jnp.where(pl.program_id(0) < 2 * NH, ds_n, dn)
        dy = ds * ds_dy
        dw_ref[0:1, :] = jnp.sum(dy * xp, axis=0, keepdims=True)
        dw_ref[1:2, :] = jnp.sum(dy * p_ref[...], axis=0, keepdims=True)
        dw_ref[2:3, :] = jnp.sum(dy * xn, axis=0, keepdims=True)
        dyn = jnp.where(last, 0.0, pltpu.roll(dy, lt - 1, 0))
        dyp = jnp.where(first, 0.0, pltpu.roll(dy, 1, 0))
        dp_ref[...] = (w[1:2, :] * dy + w[0:1, :] * dyn + w[2:3, :] * dyp).astype(dp_ref.dtype)

    return pl.pallas_call(
        body, name="qkv_bwd", out_shape=(_sds((lt, 3 * D), _BF), _sds((3, 3 * D))), grid=(3 * NH,),
        in_specs=[pl.BlockSpec((lt, DH), lambda j: (0, j)), pl.BlockSpec((3, DH), lambda j: (0, j)),
                  pl.BlockSpec((lt, DH), lambda j: (0, j)), pl.BlockSpec((lt, DH), lambda j: (0, j))],
        out_specs=(pl.BlockSpec((lt, DH), lambda j: (0, j)), pl.BlockSpec((3, DH), lambda j: (0, j))),
        compiler_params=_params(("parallel",)),
    )(p, w_conv, dqkv_f, dqkv_b)


def _masks(d):
    ri = lax.broadcasted_iota(jnp.int32, (CH, CH), 0)
    ci = lax.broadcasted_iota(jnp.int32, (CH, CH), 1)
    incl = (ri >= ci) if d == 0 else (ri <= ci)
    strict = (ri > ci) if d == 0 else (ri < ci)
    incl_t = (ri <= ci) if d == 0 else (ri >= ci)
    return incl, strict, incl_t, ri == ci


def _decays(d, ab, abt, alog_r, dtb_r, alog_c, dtb_c, incl, incl_t):
    g_full = -jnp.exp(alog_r) * _softplus(ab + dtb_r)
    beta_full = _sigmoid(ab)
    gc_full = _mmh(incl.astype(F32), g_full)
    gl_full = jnp.sum(g_full, axis=0, keepdims=True)
    gt_full = -jnp.exp(alog_c) * _softplus(abt + dtb_c)
    gct = _mmh(gt_full, incl_t.astype(F32))
    return g_full, beta_full, gc_full, gl_full, gt_full, gct


def _lane_onehot(idx, n=LANE):
    return (lax.broadcasted_iota(jnp.int32, (1, n), 1) == idx).astype(F32)


def _head_scalars(d, h, beta_full, gc_full, gl_full, gct):
    idx = d * NH + h
    oh = _lane_onehot(idx)
    gcol = jnp.sum(gc_full * oh, axis=-1, keepdims=True)
    bcol = jnp.sum(beta_full * _lane_onehot(2 * NH + idx), axis=-1, keepdims=True)
    gl = jnp.sum(gl_full * oh, axis=-1, keepdims=True)
    grow = gct[idx:idx + 1, :]
    return gcol, grow, bcol, gl


def _lockstep(gens):
    live = list(gens)
    while live:
        nxt = []
        for g in live:
            try:
                next(g)
                nxt.append(g)
            except StopIteration:
                pass
        live = nxt


def _chunk_local(qh, kh, vh, gcol, grow, bcol, gl, incl, strict):
    decay = jnp.where(incl, jnp.exp(gcol - grow), 0.0)
    kb = kh * bcol
    a = jnp.where(strict, _mm(kb, kh, NT) * decay, 0.0)
    egc = jnp.exp(gcol)
    rhs_u = vh * bcol
    rhs_w = kb * egc
    qs = qh * (DH ** -0.5)
    attn = jnp.where(incl, _mm(qs, kh, NT) * decay, 0.0)
    etail = jnp.exp(gl - gcol)
    return decay, kb, a, egc, rhs_u, rhs_w, qs, attn, etail


def _scan_specs(lt, lc, bwd_pass):
    nch = lt // CH
    ncc = lc // CH
    if not bwd_pass:
        cf = lambda s: s
        cb = lambda s: jnp.where(s < ncc, ncc - 1 - s, nch + ncc - 1 - s)
    else:
        cf = lambda s: nch - 1 - s
        cb = lambda s: jnp.where(s < nch - ncc, ncc + s, s - (nch - ncc))
    return nch, cf, cb


def _gdn_fwd(qkv, pab, abt, alog_r, dtb_r, alog_c, dtb_c, lc, xc, xc_arrays):
    lt = qkv.shape[0]
    nch, cf, cb = _scan_specs(lt, lc, False)
    nx = xc.n

    def body(*refs):
        qf, kf, vf, abf, abtf, qb, kb_, vb, abb, abtb, ar, dr, ac, dc = refs[:14]
        x_in = refs[14:14 + nx]
        of_ref, ob_ref, sf_ref, sb_ref, tf_ref, tb_ref = refs[14 + nx:20 + nx]
        x_out = refs[20 + nx:20 + 2 * nx]
        s_scr = refs[20 + 2 * nx]
        sems = refs[21 + 2 * nx:]

        @pl.when(pl.program_id(0) == 0)
        def _():
            s_scr[...] = jnp.zeros_like(s_scr)
            xc.start(x_in, x_out, sems)

        def chain(d, h, q_r, k_r, v_r, o_ref, sh_ref, th_ref, masks, decs):
            incl, strict, _, eye = masks
            sl = slice(h * DH, (h + 1) * DH)
            qh, kh, vh = q_r[:, sl], k_r[:, sl], v_r[:, sl]
            gcol, grow, bcol, gl = _head_scalars(d, h, *decs)
            _, _, a, egc, rhs_u, rhs_w, qs, attn, etail = _chunk_local(qh, kh, vh, gcol, grow, bcol, gl, incl, strict)
            yield
            n = -a
            t = jnp.where(eye, 1.0, 0.0) + n
            p = _mm3(n, n)
            yield
            for _ in range(4):
                r = _mm3(jnp.concatenate([t, p], axis=0), p)
                yield
                t = t + r[:CH]
                p = r[CH:]
            t = t + _mm3(t, p)
            yield
            sol = _mm3(t, jnp.concatenate([rhs_u, rhs_w], axis=1))
            u, w = sol[:, :DH], sol[:, DH:]
            s = s_scr[d, h]
            sh_ref[0, h] = s
            th_ref[0, h] = t
            yield
            ws = _mm(jnp.concatenate([w, qs * egc], axis=0), s)
            yield
            v_new = u - ws[:CH]
            o_ref[:, sl] = ws[CH:] + _mm(attn, v_new)
            s_scr[d, h] = s * jnp.exp(gl) + _mm(kh * etail, v_new, TN)

        chains = []
        for d, (q_r, k_r, v_r, ab_r, abt_r, o_ref, sh_ref, th_ref) in enumerate(
                ((qf, kf, vf, abf, abtf, of_ref, sf_ref, tf_ref), (qb, kb_, vb, abb, abtb, ob_ref, sb_ref, tb_ref))):
            masks = _masks(d)
            _, beta_full, gc_full, gl_full, _, gct = _decays(
                d, ab_r[...], abt_r[0], ar[...], dr[...], ac[...], dc[...], masks[0], masks[2])
            for h in range(NH):
                chains.append(chain(d, h, q_r, k_r, v_r, o_ref, sh_ref, th_ref, masks, (beta_full, gc_full, gl_full, gct)))
        _lockstep(chains)

        @pl.when(pl.program_id(0) == nch // 2)
        def _():
            xc.forward(x_in, x_out, sems)

        @pl.when(pl.program_id(0) == nch - 1)
        def _():
            xc.wait(x_in, x_out, sems)

    def row(c, col):
        return pl.BlockSpec((CH, D), lambda s: (c(s), col))

    def chunk_in(c):
        return [row(c, 0), row(c, 1), row(c, 2), pl.BlockSpec((CH, LANE), lambda s: (c(s), 0)),
                pl.BlockSpec((1, 4 * NH, CH), lambda s: (c(s), 0, 0))]

    def hist(c, n):
        return pl.BlockSpec((1, NH, n, n), lambda s: (c(s), 0, 0, 0))

    small = [_full((1, LANE)), _full((1, LANE)), _full((4 * NH, 1)), _full((4 * NH, 1))]
    return pl.pallas_call(
        body, name="gdn_fwd", grid=(nch,),
        out_shape=(_sds((lt, D)), _sds((lt, D)), _sds((nch, NH, DH, DH)), _sds((nch, NH, DH, DH)),
                   _sds((nch, NH, CH, CH)), _sds((nch, NH, CH, CH))) + xc.out_shape,
        in_specs=chunk_in(cf) + chunk_in(cb) + small + [_ANY] * nx,
        out_specs=(pl.BlockSpec((CH, D), lambda s: (cf(s), 0)), pl.BlockSpec((CH, D), lambda s: (cb(s), 0)),
                   hist(cf, DH), hist(cb, DH), hist(cf, CH), hist(cb, CH)) + tuple([_ANY] * nx),
        scratch_shapes=[pltpu.VMEM((2, NH, DH, DH), F32)] + xc.scratch,
        compiler_params=_params(("arbitrary",)),
    )(qkv, qkv, qkv, pab, abt, qkv, qkv, qkv, pab, abt, alog_r, dtb_r, alog_c, dtb_c, *xc_arrays)


def _gdn_bwd(qkv, pab, abt, alog_r, dtb_r, alog_c, dtb_c, s_f, s_b, t_f, t_b, do, lc, xc, xc_arrays):
    lt = qkv.shape[0]
    nch, cf, cb = _scan_specs(lt, lc, True)
    nx = xc.n

    def body(*refs):
        qf, kf, vf, abf, abtf, sf_ref, tf_ref, dof, qb, kb_, vb, abb, abtb, sb_ref, tb_ref, dob, ar, dr, ac, dc = refs[:20]
        x_in = refs[20:20 + nx]
        dqf_ref, dqb_ref, dcf_ref, dcb_ref, drf_ref, drb_ref, vcol_ref, vrow_ref = refs[20 + nx:28 + nx]
        x_out = refs[28 + nx:28 + 2 * nx]
        ds_scr = refs[28 + 2 * nx]
        sems = refs[29 + 2 * nx:]

        @pl.when(pl.program_id(0) == 0)
        def _():
            ds_scr[...] = jnp.zeros_like(ds_scr)
            vcol_ref[...] = jnp.zeros_like(vcol_ref)
            vrow_ref[...] = jnp.zeros_like(vrow_ref)
            xc.start(x_in, x_out, sems)

        alog_r_, dtb_r_, alog_c_, dtb_c_ = ar[...], dr[...], ac[...], dc[...]
        lane2 = lax.broadcasted_iota(jnp.int32, (1, LANE), 1)
        acc = [[], []]

        def chain(d, h, q_r, k_r, v_r, sh_ref, th_ref, do_r, dq_ref, masks, decs):
            incl, strict, _, _ = masks
            idx = d * NH + h
            sl = slice(h * DH, (h + 1) * DH)
            qh, kh, vh = q_r[:, sl], k_r[:, sl], v_r[:, sl]
            doh = do_r[:, sl]
            gcol, grow, bcol, gl = _head_scalars(d, h, *decs)
            decay, kb, a, egc, rhs_u, rhs_w, qs, attn, etail = _chunk_local(qh, kh, vh, gcol, grow, bcol, gl, incl, strict)
            t = th_ref[0, h]
            s = sh_ref[0, h]
            ds_new = ds_scr[d, h]
            sol = _mm3(t, jnp.concatenate([rhs_u, rhs_w], axis=1))
            u, w = sol[:, :DH], sol[:, DH:]
            q_dec = qs * egc
            k_tail = kh * etail
            egl = jnp.exp(gl)
            dv_new = _mm(attn, doh, TN) + _mm(k_tail, ds_new)
            dq_dec = _mm(doh, s, NT)
            dgl = jnp.sum(jnp.sum(ds_new * s, axis=0, keepdims=True), axis=-1, keepdims=True) * egl
            yield
            v_new = u - _mm(w, s)
            dw = -_mm(dv_new, s, NT)
            ds_scr[d, h] = ds_new * egl + _mm(q_dec, doh, TN) - _mm(w, dv_new, TN)
            yield
            dattn = jnp.where(incl, _mm(doh, v_new, NT), 0.0)
            dk_tail = _mm(v_new, ds_new, NT)
            dr = _mm3(t, jnp.concatenate([dv_new, dw], axis=1), TN)
            dr_u, dr_w = dr[:, :DH], dr[:, DH:]
            yield
            da = -jnp.where(strict, _mm3(dr, sol, NT), 0.0)
            nq = dattn * decay
            dqs = _mm(nq, kh) + dq_dec * egc
            dk = _mm(nq, qs, TN)
            yield
            dv = dr_u * bcol
            dbeta = jnp.sum(dr_u * vh, axis=-1, keepdims=True)
            dgc = jnp.sum(dr_w * rhs_w, axis=-1, keepdims=True)
            m = da * decay
            dkb = dr_w * egc + _mm(m, kh)
            dk = dk + _mm(m, kb, TN)
            pq = da * a + dattn * attn
            dgc = dgc + jnp.sum(pq, axis=-1, keepdims=True) + jnp.sum(dq_dec * q_dec, axis=-1, keepdims=True)
            dgr = -jnp.sum(pq, axis=0, keepdims=True)
            tt = jnp.sum(dk_tail * k_tail, axis=-1, keepdims=True)
            dk = dk + dk_tail * etail + dkb * bcol
            dgc = dgc - tt
            dgl = dgl + jnp.sum(tt, axis=0, keepdims=True)
            dbeta = dbeta + jnp.sum(dkb * kh, axis=-1, keepdims=True)
            dq_ref[:, sl] = dqs * (DH ** -0.5)
            dq_ref[:, D + h * DH:D + (h + 1) * DH] = dk
            dq_ref[:, 2 * D + h * DH:2 * D + (h + 1) * DH] = dv
            acc[d].append((idx, dgc, dgl, dbeta, dgr))

        dirs = ((qf, kf, vf, abf, abtf, sf_ref, tf_ref, dof, dqf_ref, dcf_ref, drf_ref),
                (qb, kb_, vb, abb, abtb, sb_ref, tb_ref, dob, dqb_ref, dcb_ref, drb_ref))
        chains, ctx_d = [], []
        for d, (q_r, k_r, v_r, ab_r, abt_r, sh_ref, th_ref, do_r, dq_ref, _, _) in enumerate(dirs):
            masks = _masks(d)
            ab, abt = ab_r[...], abt_r[0]
            g_full, beta_full, gc_full, gl_full, gt_full, gct = _decays(
                d, ab, abt, alog_r_, dtb_r_, alog_c_, dtb_c_, masks[0], masks[2])
            ctx_d.append((masks, ab, abt, g_full, beta_full, gt_full))
            for h in range(NH):
                chains.append(chain(d, h, q_r, k_r, v_r, sh_ref, th_ref, do_r, dq_ref, masks,
                                    (beta_full, gc_full, gl_full, gct)))
        _lockstep(chains)
        for d in range(2):
            (incl, _, incl_t, _), ab, abt, g_full, beta_full, gt_full = ctx_d[d]
            dcol_ref, drow_ref = dirs[d][9], dirs[d][10]
            dgc_col = jnp.zeros((CH, LANE), F32)
            dgl_row = jnp.zeros((1, LANE), F32)
            dbeta_col = jnp.zeros((CH, LANE), F32)
            dgc_row = jnp.zeros((4 * NH, CH), F32)
            for idx, dgc, dgl, dbeta, dgr in acc[d]:
                oh = _lane_onehot(idx)
                dgc_col = dgc_col + dgc * oh
                dgl_row = dgl_row + dgl * oh
                dbeta_col = dbeta_col + dbeta * _lane_onehot(2 * NH + idx)
                ohc = (lax.broadcasted_iota(jnp.int32, (4 * NH, 1), 0) == idx).astype(F32)
                dgc_row = dgc_row + ohc * dgr
            dg_col = _mmh(incl_t.astype(F32), dgc_col) + dgl_row
            dg_row = _mmh(dgc_row, incl.astype(F32))
            sg_col = _sigmoid(ab + dtb_r_)
            da_col = dg_col * (-jnp.exp(alog_r_)) * sg_col
            dcol_ref[...] = da_col + dbeta_col * beta_full * (1.0 - beta_full)
            da_row = dg_row * (-jnp.exp(alog_c_)) * _sigmoid(abt + dtb_c_)
            drow_ref[0] = da_row
            vcol_ref[0:1, :] += jnp.sum(dg_col * g_full, axis=0, keepdims=True)
            vcol_ref[1:2, :] += jnp.sum(da_col, axis=0, keepdims=True)
            rl = jnp.sum(dg_row * gt_full, axis=-1, keepdims=True)
            rd = jnp.sum(da_row, axis=-1, keepdims=True)
            vrow_ref[...] += jnp.where(lane2 == 0, rl, 0.0) + jnp.where(lane2 == 1, rd, 0.0)

        @pl.when(pl.program_id(0) == nch // 2)
        def _():
            xc.forward(x_in, x_out, sems)

        @pl.when(pl.program_id(0) == nch - 1)
        def _():
            xc.wait(x_in, x_out, sems)

    def row(c, col):
        return pl.BlockSpec((CH, D), lambda s: (c(s), col))

    def hist(c, n):
        return pl.BlockSpec((1, NH, n, n), lambda s: (c(s), 0, 0, 0))

    def chunk_in(c):
        return [row(c, 0), row(c, 1), row(c, 2), pl.BlockSpec((CH, LANE), lambda s: (c(s), 0)),
                pl.BlockSpec((1, 4 * NH, CH), lambda s: (c(s), 0, 0)), hist(c, DH), hist(c, CH), row(c, 0)]

    small = [_full((1, LANE)), _full((1, LANE)), _full((4 * NH, 1)), _full((4 * NH, 1))]
    return pl.pallas_call(
        body, name="gdn_bwd", grid=(nch,),
        out_shape=(_sds((lt, 3 * D)), _sds((lt, 3 * D)), _sds((lt, LANE)), _sds((lt, LANE)),
                   _sds((nch, 4 * NH, CH)), _sds((nch, 4 * NH, CH)), _sds((8, LANE)), _sds((4 * NH, LANE))) + xc.out_shape,
        in_specs=chunk_in(cf) + chunk_in(cb) + small + [_ANY] * nx,
        out_specs=(pl.BlockSpec((CH, 3 * D), lambda s: (cf(s), 0)), pl.BlockSpec((CH, 3 * D), lambda s: (cb(s), 0)),
                   pl.BlockSpec((CH, LANE), lambda s: (cf(s), 0)), pl.BlockSpec((CH, LANE), lambda s: (cb(s), 0)),
                   pl.BlockSpec((1, 4 * NH, CH), lambda s: (cf(s), 0, 0)), pl.BlockSpec((1, 4 * NH, CH), lambda s: (cb(s), 0, 0)),
                   _full((8, LANE)), _full((4 * NH, LANE))) + tuple([_ANY] * nx),
        scratch_shapes=[pltpu.VMEM((2, NH, DH, DH), F32)] + xc.scratch,
        compiler_params=_params(("arbitrary",)),
    )(qkv, qkv, qkv, pab, abt, s_f, t_f, do, qkv, qkv, qkv, pab, abt, s_b, t_b, do, alog_r, dtb_r, alog_c, dtb_c,
      *xc_arrays)


def _post(p, o_f, o_b, x, tgt, w_pa, w_pb, w_out, w_sp, w_spt, b_spb, ln_g, ln_b, g_on, g_post, gate_x, lc):
    lt = p.shape[0]
    l = x.shape[0]
    tm = GC
    nct = lc // tm

    def body(p_ref, of_ref, ob_ref, x_ref, t_ref, wpa, wpb, wout, wsp, wspt, bspb, lng_ref, lnb_ref, gon_ref, gpost_ref, gate_ref,
             dp_ref, do_ref, dy_ref, ya_ref, yb_ref, mg_ref, da_ref, db_ref, dout_ref, dwsp_ref, dbsp_ref, vec_ref):
        i = pl.program_id(0)

        @pl.when(i == 0)
        def _():
            dwsp_ref[...] = jnp.zeros_like(dwsp_ref)
            dbsp_ref[...] = jnp.zeros_like(dbsp_ref)
            vec_ref[...] = jnp.zeros_like(vec_ref)

        @pl.when(i < nct)
        def _():
            dp_ref[...] = jnp.zeros_like(dp_ref)
            do_ref[...] = jnp.zeros_like(do_ref)

        @pl.when(i >= nct)
        def _():
            lng, lnb, gon, gpost, gate = lng_ref[...], lnb_ref[...], gon_ref[...], gpost_ref[...], gate_ref[...]
            zb, ua, va, za, ga, gb = [p_ref[:, j * D:(j + 1) * D] for j in range(6)]
            o = of_ref[...] + ob_ref[...]
            szb, dszb = _silu_g(zb)
            nh_l, r_l = [], []
            for h in range(NH):
                oh = o[:, h * DH:(h + 1) * DH]
                r = lax.rsqrt(jnp.mean(oh * oh, axis=-1, keepdims=True) + EPS)
                nh_l.append(oh * r)
                r_l.append(r)
            nrm_b = jnp.concatenate(nh_l, axis=-1)
            gon_t = jnp.concatenate([gon] * NH, axis=-1)
            y_b = nrm_b * gon_t * szb
            u, du_dua = _gelu_g(ua)
            gv, dgv_dva = _gelu_g(va)
            xc = gv - jnp.mean(gv, axis=-1, keepdims=True)
            rs_ln = lax.rsqrt(jnp.mean(xc * xc, axis=-1, keepdims=True) + EPS)
            vhat = xc * rs_ln
            v = vhat * lng + lnb
            s_sp = jnp.concatenate(
                [_mm(wsp[g], v[:, g * DH:(g + 1) * DH]) + bspb[g] for g in range(NH)], axis=-1)
            sza, dsza = _silu_g(za)
            y_a = u * s_sp * sza
            a_pr = _mm(y_a, wpa[...])
            b_pr = _mm(y_b, wpb[...])
            sga = _sigmoid(ga)
            sgb = _sigmoid(gb)
            merged = sga * a_pr + sgb * b_pr
            out = _mm(merged, wout[...])
            rs_o = lax.rsqrt(jnp.mean(out * out, axis=-1, keepdims=True) + EPS)
            n_o = out * rs_o
            rr = n_o * gpost
            diff = x_ref[...] + gate * rr - t_ref[...]
            vec_ref[5:6, :] += jnp.sum(diff * diff, axis=0, keepdims=True)
            dy = diff * (1.0 / D)
            dy_ref[...] = dy
            vec_ref[0:1, :] += jnp.sum(dy * rr, axis=0, keepdims=True)
            dr = dy * gate
            vec_ref[1:2, :] += jnp.sum(dr * n_o, axis=0, keepdims=True)
            dn_o = dr * gpost
            dout = rs_o * (dn_o - n_o * jnp.mean(dn_o * n_o, axis=-1, keepdims=True))
            dmerged = _mm(dout, wout[...], NT)
            d_a = dmerged * sga
            d_b = dmerged * sgb
            dga = dmerged * a_pr * sga * (1.0 - sga)
            dgb = dmerged * b_pr * sgb * (1.0 - sgb)
            dy_a = _mm(d_a, wpa[...], NT)
            dy_b = _mm(d_b, wpb[...], NT)
            ya_ref[...] = y_a.astype(ya_ref.dtype)
            yb_ref[...] = y_b.astype(yb_ref.dtype)
            mg_ref[...] = merged.astype(mg_ref.dtype)
            da_ref[...] = d_a.astype(da_ref.dtype)
            db_ref[...] = d_b.astype(db_ref.dtype)
            dout_ref[...] = dout.astype(dout_ref.dtype)
            dua = dy_a * s_sp * sza * du_dua
            ds_sp = dy_a * u * sza
            dza = dy_a * u * s_sp * dsza
            dv_l = []
            for g in range(NH):
                ds_g = ds_sp[:, g * DH:(g + 1) * DH]
                dv_l.append(_mm(wspt[g], ds_g))
                dwsp_ref[g] += _mm(ds_g, v[:, g * DH:(g + 1) * DH], NT)
                dbsp_ref[g] += ds_g
            dv = jnp.concatenate(dv_l, axis=-1)
            vec_ref[2:3, :] += jnp.sum(dv * vhat, axis=0, keepdims=True)
            vec_ref[3:4, :] += jnp.sum(dv, axis=0, keepdims=True)
            dvh = dv * lng
            dgv = rs_ln * (dvh - jnp.mean(dvh, axis=-1, keepdims=True) - vhat * jnp.mean(dvh * vhat, axis=-1, keepdims=True))
            dva = dgv * dgv_dva
            dzb = dy_b * nrm_b * gon_t * dszb
            dgon_full = jnp.sum(dy_b * nrm_b * szb, axis=0, keepdims=True)
            dgon = dgon_full[:, 0:DH]
            for h in range(1, NH):
                dgon = dgon + dgon_full[:, h * DH:(h + 1) * DH]
            vec_ref[4:5, 0:DH] += dgon
            dnb = dy_b * gon_t * szb
            do_l = []
            for h in range(NH):
                sl = slice(h * DH, (h + 1) * DH)
                dn_h = dnb[:, sl]
                do_l.append(r_l[h] * (dn_h - nh_l[h] * jnp.mean(dn_h * nh_l[h], axis=-1, keepdims=True)))
            do_ref[...] = jnp.concatenate(do_l, axis=-1)
            for j, val in enumerate((dzb, dua, dva, dza, dga, dgb)):
                dp_ref[:, j * D:(j + 1) * D] = val.astype(dp_ref.dtype)

    xrow = lambda i: (jnp.maximum(i - nct, 0), 0)
    wspec = _full((D, D))
    gspec = _full((NH, GC, GC))
    vspec = _full((1, D))
    bf_out = _sds((l, D), _BF)
    return pl.pallas_call(
        body, name="post", grid=(lt // tm,),
        out_shape=(_sds((lt, NREST), _BF), _sds((lt, D)), _sds((l, D)), bf_out, bf_out, bf_out, bf_out, bf_out, bf_out,
                   _sds((NH, GC, GC)), _sds((NH, GC, GC)), _sds((8, D))),
        in_specs=[pl.BlockSpec((tm, NREST), lambda i: (i, 0)), pl.BlockSpec((tm, D), lambda i: (i, 0)),
                  pl.BlockSpec((tm, D), lambda i: (i, 0)), pl.BlockSpec((tm, D), xrow), pl.BlockSpec((tm, D), xrow),
                  wspec, wspec, wspec, gspec, gspec, gspec, vspec, vspec, _full((1, DH)), vspec, vspec],
        out_specs=(pl.BlockSpec((tm, NREST), lambda i: (i, 0)), pl.BlockSpec((tm, D), lambda i: (i, 0)),
                   pl.BlockSpec((tm, D), xrow), pl.BlockSpec((tm, D), xrow), pl.BlockSpec((tm, D), xrow),
                   pl.BlockSpec((tm, D), xrow), pl.BlockSpec((tm, D), xrow), pl.BlockSpec((tm, D), xrow),
                   pl.BlockSpec((tm, D), xrow), gspec, gspec, _full((8, D))),
        compiler_params=_params(("arbitrary",)),
    )(p, o_f, o_b, x, tgt, w_pa, w_pb, w_out, w_sp, w_spt, b_spb, ln_g, ln_b, g_on, g_post, gate_x)


def _sum_parts(parts, name):
    r = parts.shape[1]
    tr = _tile(r, (512, 256, 128, 64, 32, 16, 8))

    def body(p_ref, o_ref):
        acc = p_ref[0]
        for s in range(1, NDEV):
            acc = acc + p_ref[s]
        o_ref[...] = acc

    return pl.pallas_call(
        body, name=name, out_shape=_sds((r, LANE)), grid=(r // tr,),
        in_specs=[pl.BlockSpec((NDEV, tr, LANE), lambda i: (0, i, 0))],
        out_specs=pl.BlockSpec((tr, LANE), lambda i: (i, 0)),
        compiler_params=_params(("parallel",)),
    )(parts)


def _mod_bwd(c_all, c_ctx, dmx, dmc, w_mod_g):
    ws = w_mod_g.shape[2]

    def body(ca_ref, cc_ref, dsh_ref, dmx_ref, dmc_ref, dmc_sh_ref, w_ref, gw_ref, gc_ref, gb_ref):
        sc, _ = _silu_g(ca_ref[...])
        scc, dscc = _silu_g(cc_ref[...])
        dmc_tot = jnp.sum(dmc_ref[...], axis=0, keepdims=True)
        gb_ref[...] = jnp.sum(dmx_ref[...], axis=0, keepdims=True) + dmc_tot
        lhs = jnp.concatenate([sc, jnp.broadcast_to(scc, (8, D))], axis=0)
        rhs = jnp.concatenate([dsh_ref[...], dmc_sh_ref[...]], axis=0)
        gw_ref[...] = _mmh(lhs, rhs, TN)
        acc = jnp.zeros((8, D), F32)
        tot8 = jnp.broadcast_to(dmc_tot, (8, 3 * D))
        for j in range(NDEV):
            acc = acc + _mm(tot8[:, j * ws:(j + 1) * ws], w_ref[j], NT)
        gc_ref[...] = acc[0:1, :] * dscc

    return pl.pallas_call(
        body, name="mod_bwd", out_shape=(_sds((D, ws)), _sds((1, D)), _sds((1, 3 * D))),
        compiler_params=_params(),
    )(c_all, c_ctx, _my_cols(dmx, ws), dmx, dmc, _my_cols(dmc, ws), w_mod_g)


def _my_cols(a, ws):
    me = 4 * lax.axis_index("x") + 2 * lax.axis_index("y") + lax.axis_index("c")
    return lax.dynamic_slice_in_dim(a, me * ws, ws, axis=1)


def _adamw(parts, w, m, v, name):
    s_, r, c = parts.shape
    tr = _tile(r, (128, 64, 32, 16, 8)) if r * c * 4 > (1 << 20) else r
    c1 = 1.0 / (1.0 - ADAM_B1 ** ADAM_STEP)
    c2 = 1.0 / (1.0 - ADAM_B2 ** ADAM_STEP)

    def body(p_ref, w_ref, m_ref, v_ref, g_ref, d_ref, nm_ref, nv_ref):
        g = p_ref[0].astype(F32)
        for s in range(1, s_):
            g = g + p_ref[s].astype(F32)
        m_new = ADAM_B1 * m_ref[...] + (1.0 - ADAM_B1) * g
        v_new = ADAM_B2 * v_ref[...] + (1.0 - ADAM_B2) * (g * g)
        g_ref[...] = g
        nm_ref[...] = m_new
        nv_ref[...] = v_new
        d_ref[...] = -ADAM_LR * ((m_new * c1) / (jnp.sqrt(v_new * c2) + ADAM_EPS) + ADAM_WD * w_ref[...])

    blk = pl.BlockSpec((tr, c), lambda i: (i, 0))
    o = _sds((r, c))
    return pl.pallas_call(
        body, name=name, out_shape=(o, o, o, o), grid=(r // tr,),
        in_specs=[pl.BlockSpec((s_, tr, c), lambda i: (0, i, 0)), blk, blk, blk],
        out_specs=(blk, blk, blk, blk),
        compiler_params=_params(("parallel",)),
    )(parts, w, m, v)


def _rows(a):
    flat = a.reshape(-1)
    n = flat.shape[0]
    r = -(-n // (8 * LANE)) * 8
    return jnp.pad(flat, (0, r * LANE - n)).reshape(r, LANE)


def _pack(items):
    parts, layout, at = [], [], 0
    for name, a in items:
        rws = _rows(a.astype(F32))
        layout.append((name, at, rws.shape[0], a.shape))
        parts.append(rws)
        at += rws.shape[0]
    return jnp.concatenate(parts, axis=0), layout


def _unpack(packed, layout):
    out = {}
    for name, at, r, shape in layout:
        n = 1
        for s in shape:
            n *= s
        out[name] = packed[at:at + r].reshape(-1)[:n].reshape(shape)
    return out


def kernel(x, c, ctx, c_ctx, w_mod, b_mod, g_pre, g_post, w_in, w_conv, a_log, dt_bias, g_onorm, gm_ln_g, gm_ln_b, w_sp, b_sp, w_pa, w_pb, w_out, loss_target, m_c_ctx, m_w_mod, m_b_mod, m_g_pre, m_g_post, m_w_in, m_w_conv, m_a_log, m_dt_bias, m_g_onorm, m_gm_ln_g, m_gm_ln_b, m_w_sp, m_b_sp, m_w_pa, m_w_pb, m_w_out, v_c_ctx, v_w_mod, v_b_mod, v_g_pre, v_g_post, v_w_in, v_w_conv, v_a_log, v_dt_bias, v_g_onorm, v_gm_ln_g, v_gm_ln_b, v_w_sp, v_b_sp, v_w_pa, v_w_pb, v_w_out):
    l = x.shape[1]
    lc = ctx.shape[1]
    lt = l + lc
    nch = lt // CH
    me = 4 * lax.axis_index("x") + 2 * lax.axis_index("y") + lax.axis_index("c")
    wsh = w_in.shape[2]
    off_a = 3 * D
    n_ab = 4 * NH
    jb = off_a // wsh
    o1 = off_a - jb * wsh
    o2 = o1 + n_ab
    assert o2 <= wsh and NREST == (NDEV - jb) * wsh - o2
    split = jb + 1

    w_in_bf = w_in[0].astype(_BF)
    wg_lo, wg_mod, wg_conv, c_all = _exchange(
        [w_in_bf, w_mod[0].astype(_BF), w_conv[0], c], ["gather_lo", "gather", "gather", "gather"],
        "gather_first", split)
    w_qkv = jnp.concatenate([wg_lo[j] for j in range(jb)] + [wg_lo[jb][:, :o1]], axis=1)
    w_ab = jnp.pad(wg_lo[jb][:, o1:o2], ((0, 0), (0, LANE - n_ab)))
    wconv_full = jnp.moveaxis(wg_conv, 0, 1).reshape(3, 3 * D)
    c_all = c_all.reshape(NDEV, D)

    cc = jnp.concatenate([c, c_ctx.reshape(1, D), jnp.zeros((6, D), F32)], axis=0)
    mods = _modulation(cc, wg_mod, b_mod)
    xa = jnp.concatenate([ctx[0], x[0]], axis=0)
    h = _prenorm(xa, mods, g_pre, lc)
    p_qkv = _matmul_nn(h, w_qkv, "in_proj_qkv")
    pab = _matmul_nn(h, w_ab, "in_proj_ab")
    abt = jnp.swapaxes(pab[:, :n_ab].reshape(nch, CH, n_ab), 1, 2)
    alog16, dtb16 = a_log.reshape(1, 2 * NH), dt_bias.reshape(1, 2 * NH)
    alog_r = jnp.pad(alog16, ((0, 0), (0, LANE - 2 * NH)))
    dtb_r = jnp.pad(dtb16, ((0, 0), (0, LANE - 2 * NH)))
    alog_c = jnp.pad(alog16.reshape(2 * NH, 1), ((0, 2 * NH), (0, 0)))
    dtb_c = jnp.pad(dtb16.reshape(2 * NH, 1), ((0, 2 * NH), (0, 0)))
    qkv = _qkv_fwd(p_qkv, wconv_full, lc)
    late = [w_in_bf, w_pa[0].astype(_BF), w_pb[0].astype(_BF), w_out[0].astype(_BF)]
    xc_late = _Exchange(zip(late, ["gather_hi", "gather", "gather", "gather"]), split)
    o_f, o_b, s_f, s_b, t_f, t_b, wg_hi, wg_pa, wg_pb, wg_out = _gdn_fwd(
        qkv, pab, abt, alog_r, dtb_r, alog_c, dtb_c, lc, xc_late, late)
    w_rest = jnp.concatenate([wg_lo[jb][:, o2:]] + [wg_hi[j] for j in range(split, NDEV)], axis=1)
    wf_pa, wf_pb, wf_out = wg_pa.reshape(D, D), wg_pb.reshape(D, D), wg_out.reshape(D, D)
    p_rest = _matmul_nn(h, w_rest, "in_proj_rest")

    w_spt = jnp.swapaxes(w_sp[0], 1, 2)
    b_spb = jnp.broadcast_to(b_sp[0][:, :, None], (NH, GC, GC))
    gate_x = mods[0:1, 2 * D:]
    dp_rest, do, dy, ya, yb, mg, d_a, d_b, dout, dwsp, dbsp_l, pvec = _post(
        p_rest, o_f, o_b, x[0], loss_target[0], wf_pa, wf_pb, wf_out, w_sp[0], w_spt, b_spb, gm_ln_g, gm_ln_b,
        g_onorm, g_post, gate_x, lc)

    dw_rest = _matmul_tn(h, dp_rest, "dw_in_rest")
    o3 = wsh - o2
    chunks_hi = jnp.moveaxis(dw_rest[:, o3:].reshape(D, NDEV - split, wsh), 1, 0)
    dw_pa = _matmul_tn(ya, d_a, "dw_pa").reshape(NDEV, D // NDEV, D)
    dw_pb = _matmul_tn(yb, d_b, "dw_pb").reshape(NDEV, D // NDEV, D)
    dw_out = _matmul_tn(mg, dout, "dw_out").reshape(NDEV, D // NDEV, D)
    small_a, lay_a = _pack([
        ("g_post", pvec[1]), ("g_onorm", pvec[4, :DH]), ("gm_ln_g", pvec[2]), ("gm_ln_b", pvec[3]), ("w_sp", dwsp),
        ("b_sp", jnp.sum(dbsp_l, axis=-1)), ("loss", pvec[5]), ("dgate", pvec[0])])
    early = [chunks_hi, dw_pa, dw_pb, dw_out, small_a]
    xc_early = _Exchange(zip(early, ["scatter_hi", "scatter", "scatter", "scatter", "gather"]), split)

    dqkv_f, dqkv_b, dcol_f, dcol_b, drow_f, drow_b, gvec_c, gvec_r, r_in, r_pa, r_pb, r_out, small_a_all = _gdn_bwd(
        qkv, pab, abt, alog_r, dtb_r, alog_c, dtb_c, s_f, s_b, t_f, t_b, do, lc, xc_early, early)
    dp_qkv, dwconv = _qkv_bwd(p_qkv, wconv_full, dqkv_f, dqkv_b, lc)
    drow = jnp.swapaxes(drow_f + drow_b, 1, 2).reshape(lt, n_ab)
    dpab = (dcol_f + dcol_b + jnp.pad(drow, ((0, 0), (0, LANE - n_ab)))).astype(_BF)

    dw_qkv = _matmul_tn(h, dp_qkv, "dw_in_qkv")
    dw_ab = _matmul_tn(h, dpab, "dw_in_ab")
    dw_lo = jnp.concatenate([dw_qkv, dw_ab[:, :n_ab], dw_rest[:, :o3]], axis=1)
    chunks_lo = jnp.moveaxis(dw_lo.reshape(D, split, wsh), 1, 0)
    xc_last = _Exchange([(chunks_lo, "scatter_lo")], split)
    dh, r_in = _dh_matmul(dp_rest, dp_qkv, dpab, w_rest, w_qkv, w_ab, xc_last, [chunks_lo], {0: r_in})
    grad_x, nvec = _prenorm_bwd(xa, dh, dy, mods, g_pre, lc)

    dalog = gvec_c[0, :2 * NH] + gvec_r[:2 * NH, 0]
    ddtb = gvec_c[1, :2 * NH] + gvec_r[:2 * NH, 1]
    small_b, lay_b = _pack([
        ("g_pre", nvec[4]), ("a_log", dalog), ("dt_bias", ddtb), ("w_conv", dwconv),
        ("dshift", nvec[0]), ("dscale", nvec[1]), ("dshift_c", nvec[2]), ("dscale_c", nvec[3])])
    (small_b_all,) = _exchange([small_b], ["gather"], "gather_small")
    tot = _unpack(_sum_parts(small_a_all, "sum_small_a"), lay_a)
    tot.update(_unpack(_sum_parts(small_b_all, "sum_small_b"), lay_b))

    def per_device(packed_all, layout, name):
        at, r = [(a_, r_) for nm, a_, r_, _ in layout if nm == name][0]
        return packed_all[:, at:at + r].reshape(NDEV, -1)

    dmx_all = jnp.concatenate([per_device(small_b_all, lay_b, "dshift"), per_device(small_b_all, lay_b, "dscale"),
                               per_device(small_a_all, lay_a, "dgate")], axis=1)
    dmc_all = jnp.concatenate([per_device(small_b_all, lay_b, "dshift_c"), per_device(small_b_all, lay_b, "dscale_c"),
                               jnp.zeros((NDEV, D), F32)], axis=1)
    g_wmod, g_cctx, g_bmod = _mod_bwd(c_all, c_ctx.reshape(1, D), dmx_all, dmc_all, wg_mod)
    loss = 0.5 / D * jnp.sum(tot["loss"])
    ws_conv = w_conv.shape[2]
    g_wconv = lax.dynamic_slice_in_dim(tot["w_conv"], me * ws_conv, ws_conv, axis=1)

    small_names = ["c_ctx", "b_mod", "g_pre", "g_post", "a_log", "dt_bias", "g_onorm", "gm_ln_g", "gm_ln_b",
                   "w_sp", "b_sp", "w_conv"]
    wts = dict(c_ctx=c_ctx, b_mod=b_mod, g_pre=g_pre, g_post=g_post, a_log=a_log, dt_bias=dt_bias, g_onorm=g_onorm,
               gm_ln_g=gm_ln_g, gm_ln_b=gm_ln_b, w_sp=w_sp, b_sp=b_sp, w_conv=w_conv)
    ms = dict(c_ctx=m_c_ctx, b_mod=m_b_mod, g_pre=m_g_pre, g_post=m_g_post, a_log=m_a_log, dt_bias=m_dt_bias,
              g_onorm=m_g_onorm, gm_ln_g=m_gm_ln_g, gm_ln_b=m_gm_ln_b, w_sp=m_w_sp, b_sp=m_b_sp, w_conv=m_w_conv)
    vs = dict(c_ctx=v_c_ctx, b_mod=v_b_mod, g_pre=v_g_pre, g_post=v_g_post, a_log=v_a_log, dt_bias=v_dt_bias,
              g_onorm=v_g_onorm, gm_ln_g=v_gm_ln_g, gm_ln_b=v_gm_ln_b, w_sp=v_w_sp, b_sp=v_b_sp, w_conv=v_w_conv)
    gs = dict(tot)
    gs.update(c_ctx=g_cctx, b_mod=g_bmod, w_conv=g_wconv)
    gpk, play = _pack([(nm, gs[nm].reshape(wts[nm].shape)) for nm in small_names])
    wpk, _ = _pack([(nm, wts[nm]) for nm in small_names])
    mpk, _ = _pack([(nm, ms[nm]) for nm in small_names])
    vpk, _ = _pack([(nm, vs[nm]) for nm in small_names])
    res_small = [_unpack(a, play) for a in _adamw(gpk[None], wpk, mpk, vpk, "adamw_small")]
    res_big = {
        "w_mod": _adamw(g_wmod[None], w_mod[0], m_w_mod[0], v_w_mod[0], "adamw_w_mod"),
        "w_in": _adamw(r_in, w_in[0], m_w_in[0], v_w_in[0], "adamw_w_in"),
        "w_pa": _adamw(r_pa, w_pa[0], m_w_pa[0], v_w_pa[0], "adamw_w_pa"),
        "w_pb": _adamw(r_pb, w_pb[0], m_w_pb[0], v_w_pb[0], "adamw_w_pb"),
        "w_out": _adamw(r_out, w_out[0], m_w_out[0], v_w_out[0], "adamw_w_out"),
    }
    order = ["c_ctx", "w_mod", "b_mod", "g_pre", "g_post", "w_in", "w_conv", "a_log", "dt_bias", "g_onorm",
             "gm_ln_g", "gm_ln_b", "w_sp", "b_sp", "w_pa", "w_pb", "w_out"]
    outs = [loss, grad_x[None]]
    for k in range(4):
        for nm in order:
            if nm in res_big:
                outs.append(res_big[nm][k][None])
            else:
                outs.append(res_small[k][nm])
    return tuple(outs)
```

```python
import functools

import jax
import jax.numpy as jnp
from jax import lax
from jax.experimental import pallas as pl
from jax.experimental.pallas import tpu as pltpu

F32 = jnp.float32
_BF = jnp.bfloat16
_HI = lax.Precision.HIGHEST
D = 1024
NH = 8
DH = 128
CH = 64
GC = 128
NREST = 6 * D
NMAIN = NREST + 3 * D
EPS = 1e-6
LANE = 128
NDEV = 8
VMEM_LIMIT = 56 * 1024 * 1024
MESH = pl.DeviceIdType.MESH

ADAM_LR, ADAM_B1, ADAM_B2, ADAM_EPS, ADAM_WD, ADAM_STEP = 0.001, 0.9, 0.999, 1e-08, 0.01, 10

NN = ((1,), (0,))
NT = ((1,), (1,))
TN = ((0,), (0,))


def _dot(a, b, dims=NN, prec=None):
    return lax.dot_general(a, b, (dims, ((), ())), precision=prec, preferred_element_type=F32)


def _mm(a, b, dims=NN):
    return _dot(a.astype(_BF), b.astype(_BF), dims)


def _mmh(a, b, dims=NN):
    return _dot(a.astype(F32), b.astype(F32), dims, _HI)


def _split(a):
    hi = a.astype(_BF)
    return hi, (a - hi.astype(F32)).astype(_BF)


def _mm3(a, b, dims=NN):
    ah, al = _split(a)
    bh, bl = _split(b)
    return _dot(ah, bh, dims) + (_dot(ah, bl, dims) + _dot(al, bh, dims))


def _sigmoid(x):
    return 1.0 / (1.0 + jnp.exp(-x))


def _silu_g(x):
    s = _sigmoid(x)
    return x * s, s * (1.0 + x * (1.0 - s))


def _gelu_g(x):
    c = 0.7978845608028654
    t = jnp.tanh(c * (x + 0.044715 * (x * x * x)))
    cdf = 0.5 * (1.0 + t)
    return x * cdf, cdf + 0.5 * x * (1.0 - t * t) * c * (1.0 + 3 * 0.044715 * x * x)


def _softplus(x):
    return jnp.maximum(x, 0.0) + jnp.log(1.0 + jnp.exp(-jnp.abs(x)))


def _params(sem=None):
    return pltpu.CompilerParams(dimension_semantics=sem, vmem_limit_bytes=VMEM_LIMIT)


def _tile(n, pref):
    for t in pref:
        if n % t == 0:
            return t
    return n


def _full(shape):
    nd = len(shape)
    return pl.BlockSpec(shape, lambda *_: (0,) * nd)


def _sds(shape, dt=F32):
    return jax.ShapeDtypeStruct(shape, dt)


class _Exchange:
    def __init__(self, specs, split):
        self.specs = list(specs)
        self.split = split
        self.n = len(self.specs)
        def out(a, k):
            if k == "sibling":
                return (2,) + tuple(a.shape)
            return (NDEV,) + (tuple(a.shape) if k.startswith("gather") else tuple(a.shape[1:]))

        self.out_shape = tuple(_sds(out(a, k), a.dtype) for a, k in self.specs)
        self.scratch = [pltpu.SemaphoreType.DMA((self.n, NDEV - 1)), pltpu.SemaphoreType.DMA((self.n, NDEV - 1)),
                        pltpu.SemaphoreType.DMA((self.n,))]

    def _ok(self, kind, idx):
        if kind.endswith("_lo"):
            return idx < self.split
        if kind.endswith("_hi"):
            return idx >= self.split
        return True

    def _phases(self, ins, outs, sems):
        send_sems, recv_sems, loc_sems = sems
        x, y, c = lax.axis_index("x"), lax.axis_index("y"), lax.axis_index("c")
        me = 4 * x + 2 * y + c
        sib = (x, y, 1 - c)
        sib_idx = 4 * x + 2 * y + (1 - c)
        chips = [(1 - x, y), (x, 1 - y), (1 - x, 1 - y)]
        starts, forwards, waits = [], [], []
        for a, (_, kind) in enumerate(self.specs):
            ok = functools.partial(self._ok, kind)
            if kind.startswith("gather"):
                def copy(k, block, to, src=None, a=a):
                    rows = outs[a].at[block]
                    return pltpu.make_async_remote_copy(
                        src_ref=rows if src is None else src, dst_ref=rows, send_sem=send_sems.at[a, k],
                        recv_sem=recv_sems.at[a, k], device_id=to, device_id_type=MESH)

                loc = pltpu.make_async_copy(ins[a], outs[a].at[me], loc_sems.at[a])
                first = [copy(0, me, sib, ins[a])] + [copy(1 + j, me, (*chip, c), ins[a]) for j, chip in enumerate(chips)]
                starts += [(ok(me), loc.start)] + [(ok(me), cp.start) for cp in first]
                waits += [(ok(me), loc.wait)] + [(ok(me), cp.wait_send) for cp in first]
                for j, chip in enumerate(chips):
                    origin = 4 * chip[0] + 2 * chip[1] + c
                    passed = copy(4 + j, origin, sib)
                    forwards += [(ok(origin), copy(1 + j, origin, sib).wait_recv), (ok(origin), passed.start)]
                    waits.append((ok(origin), passed.wait_send))
                    other = 4 * chip[0] + 2 * chip[1] + (1 - c)
                    waits.append((ok(other), copy(4 + j, other, sib).wait_recv))
                waits.append((ok(sib_idx), copy(0, sib_idx, sib).wait_recv))
            elif kind == "sibling":
                loc = pltpu.make_async_copy(ins[a], outs[a].at[c], loc_sems.at[a])
                swap = pltpu.make_async_remote_copy(
                    src_ref=ins[a], dst_ref=outs[a].at[c], send_sem=send_sems.at[a, 0], recv_sem=recv_sems.at[a, 0],
                    device_id=sib, device_id_type=MESH)
                arrive = pltpu.make_async_remote_copy(
                    src_ref=ins[a], dst_ref=outs[a].at[1 - c], send_sem=send_sems.at[a, 0], recv_sem=recv_sems.at[a, 0],
                    device_id=sib, device_id_type=MESH)
                starts += [(True, loc.start), (True, swap.start)]
                waits += [(True, loc.wait), (True, swap.wait_send), (True, arrive.wait_recv)]
            else:
                base = self.split if kind.endswith("_hi") else 0
                same_core_only = kind.endswith("_par_lo")

                def src(idx, a=a, base=base):
                    return ins[a].at[jnp.clip(idx - base, 0, ins[a].shape[0] - 1)]

                loc = pltpu.make_async_copy(src(me), outs[a].at[me], loc_sems.at[a])
                starts.append((ok(me), loc.start))
                waits.append((ok(me), loc.wait))
                for k in range(1, NDEV):
                    if same_core_only and k & 1:
                        continue
                    px = 1 - x if (k >> 2) & 1 else x
                    py = 1 - y if (k >> 1) & 1 else y
                    pc = 1 - c if k & 1 else c
                    pidx = 4 * px + 2 * py + pc
                    sems_k = dict(send_sem=send_sems.at[a, k - 1], recv_sem=recv_sems.at[a, k - 1],
                                  device_id=(px, py, pc), device_id_type=MESH)
                    send = pltpu.make_async_remote_copy(src_ref=src(pidx), dst_ref=outs[a].at[me], **sems_k)
                    arrive = pltpu.make_async_remote_copy(src_ref=src(pidx), dst_ref=outs[a].at[pidx], **sems_k)
                    starts.append((ok(pidx), send.start))
                    waits += [(ok(pidx), send.wait_send), (ok(me), arrive.wait_recv)]
        return starts, forwards, waits

    @staticmethod
    def _run(actions):
        for cond, fn in actions:
            if cond is True:
                fn()
            else:
                pl.when(cond)(fn)

    def start(self, ins, outs, sems):
        self._run(self._phases(ins, outs, sems)[0])

    def forward(self, ins, outs, sems):
        self._run(self._phases(ins, outs, sems)[1])

    def wait(self, ins, outs, sems):
        self._run(self._phases(ins, outs, sems)[2])


_ANY = pl.BlockSpec(memory_space=pl.ANY)


def _exchange(arrays, kinds, name, split=0, into=None):
    xc = _Exchange(zip(arrays, kinds), split)
    n = xc.n
    into = into or {}
    ni = len(into)

    def body(*refs):
        ins, outs, sems = refs[:n], refs[n + ni:2 * n + ni], refs[2 * n + ni:]
        xc.start(ins, outs, sems)
        xc.forward(ins, outs, sems)
        xc.wait(ins, outs, sems)

    return pl.pallas_call(
        body, name=name, out_shape=xc.out_shape, in_specs=[_ANY] * (n + ni), out_specs=tuple([_ANY] * n),
        scratch_shapes=xc.scratch, input_output_aliases={n + t: a for t, a in enumerate(into)},
    )(*arrays, *into.values())


def _matmul_nn(a, b, name):
    m, kk = a.shape
    n = b.shape[1]
    tm = _tile(m, (1088, 1024, 640, 512, 256, 128))
    tn = _tile(n, (512, 256, 128))

    def body(a_ref, b_ref, o_ref):
        o_ref[...] = _mm(a_ref[...], b_ref[...])

    return pl.pallas_call(
        body, name=name, out_shape=_sds((m, n)), grid=(n // tn, m // tm),
        in_specs=[pl.BlockSpec((tm, kk), lambda j, i: (i, 0)), pl.BlockSpec((kk, tn), lambda j, i: (0, j))],
        out_specs=pl.BlockSpec((tm, tn), lambda j, i: (i, j)),
        compiler_params=_params(("parallel", "parallel")),
    )(a, b)


def _matmul_tn(a, b, name):
    kk, m = a.shape
    n = b.shape[1]
    tk = _tile(kk, (1088, 1024, 640, 512, 256, 128))
    tn = _tile(n, (1024, 512, 256, 128))
    nk = kk // tk

    def body(a_ref, b_ref, o_ref, acc_ref):
        k = pl.program_id(1)

        @pl.when(k == 0)
        def _():
            acc_ref[...] = jnp.zeros_like(acc_ref)

        acc_ref[...] += _mm(a_ref[...], b_ref[...], TN)

        @pl.when(k == nk - 1)
        def _():
            o_ref[...] = acc_ref[...].astype(o_ref.dtype)

    return pl.pallas_call(
        body, name=name, out_shape=_sds((m, n), _BF), grid=(n // tn, nk),
        in_specs=[pl.BlockSpec((tk, m), lambda j, k: (k, 0)), pl.BlockSpec((tk, tn), lambda j, k: (k, j))],
        out_specs=pl.BlockSpec((m, tn), lambda j, k: (0, j)),
        scratch_shapes=[pltpu.VMEM((m, tn), F32)],
        compiler_params=_params(("parallel", "arbitrary")),
    )(a, b)


def _dh_matmul(dp_rest, dp_qkv, dpab, w_rest, w_qkv, w_ab, xc, xc_arrays, xc_into):
    lt = dp_rest.shape[0]
    tm = _tile(lt, (1088, 1024, 640, 512, 256, 128))
    nr, nq = dp_rest.shape[1] // D, dp_qkv.shape[1] // D
    nx, ni = xc.n, len(xc_into)
    ni_steps = lt // tm

    def body(*refs):
        dr_ref, dq_ref, ab_ref, wr_ref, wq_ref, wab_ref = refs[:6]
        x_in = refs[6:6 + nx]
        o_ref = refs[6 + nx + ni]
        x_out = refs[7 + nx + ni:7 + 2 * nx + ni]
        sems = refs[7 + 2 * nx + ni:]
        i = pl.program_id(0)
        k = pl.program_id(1)

        @pl.when((i == 0) & (k == 0))
        def _():
            xc.start(x_in, x_out, sems)

        @pl.when(k == 0)
        def _():
            o_ref[...] = _mm(ab_ref[...], wab_ref[...], NT)

        @pl.when(k < nr)
        def _():
            o_ref[...] += _mm(dr_ref[...], wr_ref[...], NT)

        @pl.when(k >= nr)
        def _():
            o_ref[...] += _mm(dq_ref[...], wq_ref[...], NT)

        @pl.when((i == ni_steps - 1) & (k == nr + nq - 1))
        def _():
            xc.wait(x_in, x_out, sems)

    rk = lambda k: jnp.minimum(k, nr - 1)
    qk = lambda k: jnp.maximum(k - nr, 0)
    return pl.pallas_call(
        body, name="dh_matmul", out_shape=(_sds((lt, D)),) + xc.out_shape, grid=(lt // tm, nr + nq),
        in_specs=[pl.BlockSpec((tm, D), lambda i, k: (i, rk(k))), pl.BlockSpec((tm, D), lambda i, k: (i, qk(k))),
                  pl.BlockSpec((tm, LANE), lambda i, k: (i, 0)),
                  pl.BlockSpec((D, D), lambda i, k: (0, rk(k))), pl.BlockSpec((D, D), lambda i, k: (0, qk(k))),
                  _full((D, LANE))] + [_ANY] * (nx + ni),
        out_specs=(pl.BlockSpec((tm, D), lambda i, k: (i, 0)),) + tuple([_ANY] * nx),
        scratch_shapes=xc.scratch, input_output_aliases={6 + nx + t: 1 + a for t, a in enumerate(xc_into)},
        compiler_params=_params(("arbitrary", "arbitrary")),
    )(dp_rest, dp_qkv, dpab, w_rest, w_qkv, w_ab, *xc_arrays, *xc_into.values())


def _modulation(cc, w_mod_g, b_mod):
    ws = w_mod_g.shape[2]

    def body(c_ref, w_ref, b_ref, o_ref):
        s, _ = _silu_g(c_ref[...])
        o_ref[...] = _mm(s, w_ref[0]) + b_ref[...]

    return pl.pallas_call(
        body, name="modulation", out_shape=_sds((8, 3 * D)), grid=(NDEV,),
        in_specs=[_full((8, D)), pl.BlockSpec((1, D, ws), lambda j: (j, 0, 0)), pl.BlockSpec((1, ws), lambda j: (0, j))],
        out_specs=pl.BlockSpec((8, ws), lambda j: (0, j)),
        compiler_params=_params(("parallel",)),
    )(cc, w_mod_g, b_mod)


def _prenorm(xa, mods, g_pre, lc):
    lt = xa.shape[0]
    tm = _tile(lc, (256, 128))
    nct = lc // tm

    def body(x_ref, m_ref, g_ref, o_ref):
        x = x_ref[...]
        is_ctx = pl.program_id(0) < nct
        shift = jnp.where(is_ctx, m_ref[1:2, 0:D], m_ref[0:1, 0:D])
        scale = jnp.where(is_ctx, m_ref[1:2, D:2 * D], m_ref[0:1, D:2 * D])
        r = lax.rsqrt(jnp.mean(x * x, axis=-1, keepdims=True) + EPS)
        o_ref[...] = ((x * r * g_ref[...]) * (1.0 + scale) + shift).astype(o_ref.dtype)

    return pl.pallas_call(
        body, name="prenorm", out_shape=_sds((lt, D), _BF), grid=(lt // tm,),
        in_specs=[pl.BlockSpec((tm, D), lambda i: (i, 0)), _full((8, 3 * D)), _full((1, D))],
        out_specs=pl.BlockSpec((tm, D), lambda i: (i, 0)),
        compiler_params=_params(("parallel",)),
    )(xa, mods, g_pre)


def _prenorm_bwd(xa, dh, dy, mods, g_pre, lc):
    lt = xa.shape[0]
    tm = _tile(lc, (256, 128))
    nct = lc // tm
    nl = (lt - lc) // tm

    def body(x_ref, dh_ref, dy_ref, m_ref, g_ref, gx_ref, vec_ref):
        i = pl.program_id(0)

        @pl.when(i == 0)
        def _():
            vec_ref[...] = jnp.zeros_like(vec_ref)

        x = x_ref[...]
        dh = dh_ref[...]
        g = g_ref[...]
        is_ctx = i < nct
        scale = jnp.where(is_ctx, m_ref[1:2, D:2 * D], m_ref[0:1, D:2 * D])
        r = lax.rsqrt(jnp.mean(x * x, axis=-1, keepdims=True) + EPS)
        n = x * r
        hn = n * g
        dsh = jnp.sum(dh, axis=0, keepdims=True)
        dsc = jnp.sum(dh * hn, axis=0, keepdims=True)
        dhn = dh * (1.0 + scale)
        vec_ref[4:5, :] += jnp.sum(dhn * n, axis=0, keepdims=True)
        dn = dhn * g
        dx = r * (dn - n * jnp.mean(dn * n, axis=-1, keepdims=True))

        @pl.when(is_ctx)
        def _():
            vec_ref[2:3, :] += dsh
            vec_ref[3:4, :] += dsc

        @pl.when(jnp.logical_not(is_ctx))
        def _():
            vec_ref[0:1, :] += dsh
            vec_ref[1:2, :] += dsc
            gx_ref[...] = dy_ref[...] + dx

    xrow = lambda i: (jnp.maximum(i - nct, 0), 0)
    return pl.pallas_call(
        body, name="prenorm_bwd", out_shape=(_sds((nl * tm, D)), _sds((8, D))), grid=(lt // tm,),
        in_specs=[pl.BlockSpec((tm, D), lambda i: (i, 0)), pl.BlockSpec((tm, D), lambda i: (i, 0)),
                  pl.BlockSpec((tm, D), xrow), _full((8, 3 * D)), _full((1, D))],
        out_specs=(pl.BlockSpec((tm, D), xrow), _full((8, D))),
        compiler_params=_params(("arbitrary",)),
    )(xa, dh, dy, mods, g_pre)


def _conv_parts(x, w, lc):
    lt = x.shape[0]
    row = lax.broadcasted_iota(jnp.int32, x.shape, 0)
    first = (row == 0) | (row == lc)
    last = (row == lc - 1) | (row == lt - 1)
    xp = jnp.where(first, 0.0, pltpu.roll(x, 1, 0))
    xn = jnp.where(last, 0.0, pltpu.roll(x, lt - 1, 0))
    y = w[0:1, :] * xp + w[1:2, :] * x + w[2:3, :] * xn
    return xp, xn, y, first, last


def _qkv_fwd(p, w_conv, lc):
    lt = p.shape[0]

    def body(p_ref, w_ref, o_ref):
        _, _, y, _, _ = _conv_parts(p_ref[...], w_ref[...], lc)
        s, _ = _silu_g(y)
        rs = lax.rsqrt(jnp.sum(s * s, axis=-1, keepdims=True) + EPS)
        o_ref[...] = s * jnp.where(pl.program_id(0) < 2 * NH, rs, 1.0)

    return pl.pallas_call(
        body, name="qkv_fwd", out_shape=_sds((lt, 3 * D)), grid=(3 * NH,),
        in_specs=[pl.BlockSpec((lt, DH), lambda j: (0, j)), pl.BlockSpec((3, DH), lambda j: (0, j))],
        out_specs=pl.BlockSpec((lt, DH), lambda j: (0, j)),
        compiler_params=_params(("parallel",)),
    )(p, w_conv)


def _qkv_bwd(p, w_conv, dqkv_f, dqkv_b, lc):
    lt = p.shape[0]

    def body(p_ref, w_ref, df_ref, db_ref, dp_ref, dw_ref):
        w = w_ref[...]
        xp, xn, y, first, last = _conv_parts(p_ref[...], w, lc)
        s, ds_dy = _silu_g(y)
        dn = df_ref[...] + db_ref[...]
        rs = lax.rsqrt(jnp.sum(s * s, axis=-1, keepdims=True) + EPS)
        nrm = s * rs
        ds_n = rs * (dn - nrm * jnp.sum(dn * nrm, axis=-1, keepdims=True))
        ds = jnp.where(pl.program_id(0) < 2 * NH, ds_n, dn)
        dy = ds * ds_dy
        dw_ref[0:1, :] = jnp.sum(dy * xp, axis=0, keepdims=True)
        dw_ref[1:2, :] = jnp.sum(dy * p_ref[...], axis=0, keepdims=True)
        dw_ref[2:3, :] = jnp.sum(dy * xn, axis=0, keepdims=True)
        dyn = jnp.where(last, 0.0, pltpu.roll(dy, lt - 1, 0))
        dyp = jnp.where(first, 0.0, pltpu.roll(dy, 1, 0))
        dp_ref[...] = (w[1:2, :] * dy + w[0:1, :] * dyn + w[2:3, :] * dyp).astype(dp_ref.dtype)

    return pl.pallas_call(
        body, name="qkv_bwd", out_shape=(_sds((lt, 3 * D), _BF), _sds((3, 3 * D))), grid=(3 * NH,),
        in_specs=[pl.BlockSpec((lt, DH), lambda j: (0, j)), pl.BlockSpec((3, DH), lambda j: (0, j)),
                  pl.BlockSpec((lt, DH), lambda j: (0, j)), pl.BlockSpec((lt, DH), lambda j: (0, j))],
        out_specs=(pl.BlockSpec((lt, DH), lambda j: (0, j)), pl.BlockSpec((3, DH), lambda j: (0, j))),
        compiler_params=_params(("parallel",)),
    )(p, w_conv, dqkv_f, dqkv_b)


def _masks(d):
    ri = lax.broadcasted_iota(jnp.int32, (CH, CH), 0)
    ci = lax.broadcasted_iota(jnp.int32, (CH, CH), 1)
    incl = (ri >= ci) if d == 0 else (ri <= ci)
    strict = (ri > ci) if d == 0 else (ri < ci)
    incl_t = (ri <= ci) if d == 0 else (ri >= ci)
    return incl, strict, incl_t, ri == ci


def _decays(d, ab, abt, alog_r, dtb_r, alog_c, dtb_c, incl, incl_t):
    g_full = -jnp.exp(alog_r) * _softplus(ab + dtb_r)
    beta_full = _sigmoid(ab)
    gc_full = _mmh(incl.astype(F32), g_full)
    gl_full = jnp.sum(g_full, axis=0, keepdims=True)
    gt_full = -jnp.exp(alog_c) * _softplus(abt + dtb_c)
    gct = _mmh(gt_full, incl_t.astype(F32))
    return g_full, beta_full, gc_full, gl_full, gt_full, gct


def _lane_onehot(idx, n=LANE):
    return (lax.broadcasted_iota(jnp.int32, (1, n), 1) == idx).astype(F32)


def _head_scalars(d, h, beta_full, gc_full, gl_full, gct):
    idx = d * NH + h
    oh = _lane_onehot(idx)
    gcol = jnp.sum(gc_full * oh, axis=-1, keepdims=True)
    bcol = jnp.sum(beta_full * _lane_onehot(2 * NH + idx), axis=-1, keepdims=True)
    gl = jnp.sum(gl_full * oh, axis=-1, keepdims=True)
    grow = gct[idx:idx + 1, :]
    return gcol, grow, bcol, gl


def _lockstep(gens):
    live = list(gens)
    while live:
        nxt = []
        for g in live:
            try:
                next(g)
                nxt.append(g)
            except StopIteration:
                pass
        live = nxt


def _chunk_local(qh, kh, vh, gcol, grow, bcol, gl, incl, strict):
    decay = jnp.where(incl, jnp.exp(gcol - grow), 0.0)
    kb = kh * bcol
    a = jnp.where(strict, _mm(kb, kh, NT) * decay, 0.0)
    egc = jnp.exp(gcol)
    rhs_u = vh * bcol
    rhs_w = kb * egc
    qs = qh * (DH ** -0.5)
    attn = jnp.where(incl, _mm(qs, kh, NT) * decay, 0.0)
    etail = jnp.exp(gl - gcol)
    return decay, kb, a, egc, rhs_u, rhs_w, qs, attn, etail


def _scan_specs(lt, lc, bwd_pass):
    nch = lt // CH
    ncc = lc // CH
    if not bwd_pass:
        cf = lambda s: s
        cb = lambda s: jnp.where(s < ncc, ncc - 1 - s, nch + ncc - 1 - s)
    else:
        cf = lambda s: nch - 1 - s
        cb = lambda s: jnp.where(s < nch - ncc, ncc + s, s - (nch - ncc))
    return nch, cf, cb


def _gdn_fwd(qkv, pab, abt, alog_r, dtb_r, alog_c, dtb_c, lc, xc, xc_arrays):
    lt = qkv.shape[0]
    nch, cf, cb = _scan_specs(lt, lc, False)
    nx = xc.n

    def body(*refs):
        qf, kf, vf, abf, abtf, qb, kb_, vb, abb, abtb, ar, dr, ac, dc = refs[:14]
        x_in = refs[14:14 + nx]
        of_ref, ob_ref, sf_ref, sb_ref, tf_ref, tb_ref = refs[14 + nx:20 + nx]
        x_out = refs[20 + nx:20 + 2 * nx]
        s_scr = refs[20 + 2 * nx]
        sems = refs[21 + 2 * nx:]

        @pl.when(pl.program_id(0) == 0)
        def _():
            s_scr[...] = jnp.zeros_like(s_scr)
            xc.start(x_in, x_out, sems)

        def chain(d, h, q_r, k_r, v_r, o_ref, sh_ref, th_ref, masks, decs):
            incl, strict, _, eye = masks
            sl = slice(h * DH, (h + 1) * DH)
            qh, kh, vh = q_r[:, sl], k_r[:, sl], v_r[:, sl]
            gcol, grow, bcol, gl = _head_scalars(d, h, *decs)
            _, _, a, egc, rhs_u, rhs_w, qs, attn, etail = _chunk_local(qh, kh, vh, gcol, grow, bcol, gl, incl, strict)
            yield
            n = -a
            t = jnp.where(eye, 1.0, 0.0) + n
            p = _mm3(n, n)
            yield
            for _ in range(4):
                r = _mm3(jnp.concatenate([t, p], axis=0), p)
                yield
                t = t + r[:CH]
                p = r[CH:]
            t = t + _mm3(t, p)
            yield
            sol = _mm3(t, jnp.concatenate([rhs_u, rhs_w], axis=1))
            u, w = sol[:, :DH], sol[:, DH:]
            s = s_scr[d, h]
            sh_ref[0, h] = s
            th_ref[0, h] = t
            yield
            ws = _mm(jnp.concatenate([w, qs * egc], axis=0), s)
            yield
            v_new = u - ws[:CH]
            o_ref[:, sl] = ws[CH:] + _mm(attn, v_new)
            s_scr[d, h] = s * jnp.exp(gl) + _mm(kh * etail, v_new, TN)

        chains = []
        for d, (q_r, k_r, v_r, ab_r, abt_r, o_ref, sh_ref, th_ref) in enumerate(
                ((qf, kf, vf, abf, abtf, of_ref, sf_ref, tf_ref), (qb, kb_, vb, abb, abtb, ob_ref, sb_ref, tb_ref))):
            masks = _masks(d)
            _, beta_full, gc_full, gl_full, _, gct = _decays(
                d, ab_r[...], abt_r[0], ar[...], dr[...], ac[...], dc[...], masks[0], masks[2])
            for h in range(NH):
                chains.append(chain(d, h, q_r, k_r, v_r, o_ref, sh_ref, th_ref, masks, (beta_full, gc_full, gl_full, gct)))
        _lockstep(chains)

        @pl.when(pl.program_id(0) == nch // 2)
        def _():
            xc.forward(x_in, x_out, sems)

        @pl.when(pl.program_id(0) == nch - 1)
        def _():
            xc.wait(x_in, x_out, sems)

    def row(c, col):
        return pl.BlockSpec((CH, D), lambda s: (c(s), col))

    def chunk_in(c):
        return [row(c, 0), row(c, 1), row(c, 2), pl.BlockSpec((CH, LANE), lambda s: (c(s), 0)),
                pl.BlockSpec((1, 4 * NH, CH), lambda s: (c(s), 0, 0))]

    def hist(c, n):
        return pl.BlockSpec((1, NH, n, n), lambda s: (c(s), 0, 0, 0))

    small = [_full((1, LANE)), _full((1, LANE)), _full((4 * NH, 1)), _full((4 * NH, 1))]
    return pl.pallas_call(
        body, name="gdn_fwd", grid=(nch,),
        out_shape=(_sds((lt, D)), _sds((lt, D)), _sds((nch, NH, DH, DH)), _sds((nch, NH, DH, DH)),
                   _sds((nch, NH, CH, CH)), _sds((nch, NH, CH, CH))) + xc.out_shape,
        in_specs=chunk_in(cf) + chunk_in(cb) + small + [_ANY] * nx,
        out_specs=(pl.BlockSpec((CH, D), lambda s: (cf(s), 0)), pl.BlockSpec((CH, D), lambda s: (cb(s), 0)),
                   hist(cf, DH), hist(cb, DH), hist(cf, CH), hist(cb, CH)) + tuple([_ANY] * nx),
        scratch_shapes=[pltpu.VMEM((2, NH, DH, DH), F32)] + xc.scratch,
        compiler_params=_params(("arbitrary",)),
    )(qkv, qkv, qkv, pab, abt, qkv, qkv, qkv, pab, abt, alog_r, dtb_r, alog_c, dtb_c, *xc_arrays)


def _gdn_bwd(qkv, pab, abt, alog_r, dtb_r, alog_c, dtb_c, s_f, s_b, t_f, t_b, do, lc, xc, xc_arrays):
    lt = qkv.shape[0]
    nch, cf, cb = _scan_specs(lt, lc, True)
    nx = xc.n

    def body(*refs):
        qf, kf, vf, abf, abtf, sf_ref, tf_ref, dof, qb, kb_, vb, abb, abtb, sb_ref, tb_ref, dob, ar, dr, ac, dc = refs[:20]
        x_in = refs[20:20 + nx]
        dqf_ref, dqb_ref, dcf_ref, dcb_ref, drf_ref, drb_ref, vcol_ref, vrow_ref = refs[20 + nx:28 + nx]
        x_out = refs[28 + nx:28 + 2 * nx]
        ds_scr = refs[28 + 2 * nx]
        sems = refs[29 + 2 * nx:]

        @pl.when(pl.program_id(0) == 0)
        def _():
            ds_scr[...] = jnp.zeros_like(ds_scr)
            vcol_ref[...] = jnp.zeros_like(vcol_ref)
            vrow_ref[...] = jnp.zeros_like(vrow_ref)
            xc.start(x_in, x_out, sems)

        alog_r_, dtb_r_, alog_c_, dtb_c_ = ar[...], dr[...], ac[...], dc[...]
        lane2 = lax.broadcasted_iota(jnp.int32, (1, LANE), 1)
        acc = [[], []]

        def chain(d, h, q_r, k_r, v_r, sh_ref, th_ref, do_r, dq_ref, masks, decs):
            incl, strict, _, _ = masks
            idx = d * NH + h
            sl = slice(h * DH, (h + 1) * DH)
            qh, kh, vh = q_r[:, sl], k_r[:, sl], v_r[:, sl]
            doh = do_r[:, sl]
            gcol, grow, bcol, gl = _head_scalars(d, h, *decs)
            decay, kb, a, egc, rhs_u, rhs_w, qs, attn, etail = _chunk_local(qh, kh, vh, gcol, grow, bcol, gl, incl, strict)
            t = th_ref[0, h]
            s = sh_ref[0, h]
            ds_new = ds_scr[d, h]
            sol = _mm3(t, jnp.concatenate([rhs_u, rhs_w], axis=1))
            u, w = sol[:, :DH], sol[:, DH:]
            q_dec = qs * egc
            k_tail = kh * etail
            egl = jnp.exp(gl)
            dv_new = _mm(attn, doh, TN) + _mm(k_tail, ds_new)
            dq_dec = _mm(doh, s, NT)
            dgl = jnp.sum(jnp.sum(ds_new * s, axis=0, keepdims=True), axis=-1, keepdims=True) * egl
            yield
            v_new = u - _mm(w, s)
            dw = -_mm(dv_new, s, NT)
            ds_scr[d, h] = ds_new * egl + _mm(q_dec, doh, TN) - _mm(w, dv_new, TN)
            yield
            dattn = jnp.where(incl, _mm(doh, v_new, NT), 0.0)
            dk_tail = _mm(v_new, ds_new, NT)
            dr = _mm3(t, jnp.concatenate([dv_new, dw], axis=1), TN)
            dr_u, dr_w = dr[:, :DH], dr[:, DH:]
            yield
            da = -jnp.where(strict, _mm3(dr, sol, NT), 0.0)
            nq = dattn * decay
            dqs = _mm(nq, kh) + dq_dec * egc
            dk = _mm(nq, qs, TN)
            yield
            dv = dr_u * bcol
            dbeta = jnp.sum(dr_u * vh, axis=-1, keepdims=True)
            dgc = jnp.sum(dr_w * rhs_w, axis=-1, keepdims=True)
            m = da * decay
            dkb = dr_w * egc + _mm(m, kh)
            dk = dk + _mm(m, kb, TN)
            pq = da * a + dattn * attn
            dgc = dgc + jnp.sum(pq, axis=-1, keepdims=True) + jnp.sum(dq_dec * q_dec, axis=-1, keepdims=True)
            dgr = -jnp.sum(pq, axis=0, keepdims=True)
            tt = jnp.sum(dk_tail * k_tail, axis=-1, keepdims=True)
            dk = dk + dk_tail * etail + dkb * bcol
            dgc = dgc - tt
            dgl = dgl + jnp.sum(tt, axis=0, keepdims=True)
            dbeta = dbeta + jnp.sum(dkb * kh, axis=-1, keepdims=True)
            dq_ref[:, sl] = dqs * (DH ** -0.5)
            dq_ref[:, D + h * DH:D + (h + 1) * DH] = dk
            dq_ref[:, 2 * D + h * DH:2 * D + (h + 1) * DH] = dv
            acc[d].append((idx, dgc, dgl, dbeta, dgr))

        dirs = ((qf, kf, vf, abf, abtf, sf_ref, tf_ref, dof, dqf_ref, dcf_ref, drf_ref),
                (qb, kb_, vb, abb, abtb, sb_ref, tb_ref, dob, dqb_ref, dcb_ref, drb_ref))
        chains, ctx_d = [], []
        for d, (q_r, k_r, v_r, ab_r, abt_r, sh_ref, th_ref, do_r, dq_ref, _, _) in enumerate(dirs):
            masks = _masks(d)
            ab, abt = ab_r[...], abt_r[0]
            g_full, beta_full, gc_full, gl_full, gt_full, gct = _decays(
                d, ab, abt, alog_r_, dtb_r_, alog_c_, dtb_c_, masks[0], masks[2])
            ctx_d.append((masks, ab, abt, g_full, beta_full, gt_full))
            for h in range(NH):
                chains.append(chain(d, h, q_r, k_r, v_r, sh_ref, th_ref, do_r, dq_ref, masks,
                                    (beta_full, gc_full, gl_full, gct)))
        _lockstep(chains)
        for d in range(2):
            (incl, _, incl_t, _), ab, abt, g_full, beta_full, gt_full = ctx_d[d]
            dcol_ref, drow_ref = dirs[d][9], dirs[d][10]
            dgc_col = jnp.zeros((CH, LANE), F32)
            dgl_row = jnp.zeros((1, LANE), F32)
            dbeta_col = jnp.zeros((CH, LANE), F32)
            dgc_row = jnp.zeros((4 * NH, CH), F32)
            for idx, dgc, dgl, dbeta, dgr in acc[d]:
                oh = _lane_onehot(idx)
                dgc_col = dgc_col + dgc * oh
                dgl_row = dgl_row + dgl * oh
                dbeta_col = dbeta_col + dbeta * _lane_onehot(2 * NH + idx)
                ohc = (lax.broadcasted_iota(jnp.int32, (4 * NH, 1), 0) == idx).astype(F32)
                dgc_row = dgc_row + ohc * dgr
            dg_col = _mmh(incl_t.astype(F32), dgc_col) + dgl_row
            dg_row = _mmh(dgc_row, incl.astype(F32))
            sg_col = _sigmoid(ab + dtb_r_)
            da_col = dg_col * (-jnp.exp(alog_r_)) * sg_col
            dcol_ref[...] = da_col + dbeta_col * beta_full * (1.0 - beta_full)
            da_row = dg_row * (-jnp.exp(alog_c_)) * _sigmoid(abt + dtb_c_)
            drow_ref[0] = da_row
            vcol_ref[0:1, :] += jnp.sum(dg_col * g_full, axis=0, keepdims=True)
            vcol_ref[1:2, :] += jnp.sum(da_col, axis=0, keepdims=True)
            rl = jnp.sum(dg_row * gt_full, axis=-1, keepdims=True)
            rd = jnp.sum(da_row, axis=-1, keepdims=True)
            vrow_ref[...] += jnp.where(lane2 == 0, rl, 0.0) + jnp.where(lane2 == 1, rd, 0.0)

        @pl.when(pl.program_id(0) == nch // 2)
        def _():
            xc.forward(x_in, x_out, sems)

        @pl.when(pl.program_id(0) == nch - 1)
        def _():
            xc.wait(x_in, x_out, sems)

    def row(c, col):
        return pl.BlockSpec((CH, D), lambda s: (c(s), col))

    def hist(c, n):
        return pl.BlockSpec((1, NH, n, n), lambda s: (c(s), 0, 0, 0))

    def chunk_in(c):
        return [row(c, 0), row(c, 1), row(c, 2), pl.BlockSpec((CH, LANE), lambda s: (c(s), 0)),
                pl.BlockSpec((1, 4 * NH, CH), lambda s: (c(s), 0, 0)), hist(c, DH), hist(c, CH), row(c, 0)]

    small = [_full((1, LANE)), _full((1, LANE)), _full((4 * NH, 1)), _full((4 * NH, 1))]
    return pl.pallas_call(
        body, name="gdn_bwd", grid=(nch,),
        out_shape=(_sds((lt, 3 * D)), _sds((lt, 3 * D)), _sds((lt, LANE)), _sds((lt, LANE)),
                   _sds((nch, 4 * NH, CH)), _sds((nch, 4 * NH, CH)), _sds((8, LANE)), _sds((4 * NH, LANE))) + xc.out_shape,
        in_specs=chunk_in(cf) + chunk_in(cb) + small + [_ANY] * nx,
        out_specs=(pl.BlockSpec((CH, 3 * D), lambda s: (cf(s), 0)), pl.BlockSpec((CH, 3 * D), lambda s: (cb(s), 0)),
                   pl.BlockSpec((CH, LANE), lambda s: (cf(s), 0)), pl.BlockSpec((CH, LANE), lambda s: (cb(s), 0)),
                   pl.BlockSpec((1, 4 * NH, CH), lambda s: (cf(s), 0, 0)), pl.BlockSpec((1, 4 * NH, CH), lambda s: (cb(s), 0, 0)),
                   _full((8, LANE)), _full((4 * NH, LANE))) + tuple([_ANY] * nx),
        scratch_shapes=[pltpu.VMEM((2, NH, DH, DH), F32)] + xc.scratch,
        compiler_params=_params(("arbitrary",)),
    )(qkv, qkv, qkv, pab, abt, s_f, t_f, do, qkv, qkv, qkv, pab, abt, s_b, t_b, do, alog_r, dtb_r, alog_c, dtb_c,
      *xc_arrays)


def _post(p, o_f, o_b, x, tgt, w_pa, w_pb, w_out, w_sp, w_spt, b_spb, ln_g, ln_b, g_on, g_post, gate_x, lc):
    lt = p.shape[0]
    l = x.shape[0]
    tm = GC
    nct = lc // tm

    def body(p_ref, of_ref, ob_ref, x_ref, t_ref, wpa, wpb, wout, wsp, wspt, bspb, lng_ref, lnb_ref, gon_ref, gpost_ref, gate_ref,
             dp_ref, do_ref, dy_ref, ya_ref, yb_ref, mg_ref, da_ref, db_ref, dout_ref, dwsp_ref, dbsp_ref, vec_ref):
        i = pl.program_id(0)

        @pl.when(i == 0)
        def _():
            dwsp_ref[...] = jnp.zeros_like(dwsp_ref)
            dbsp_ref[...] = jnp.zeros_like(dbsp_ref)
            vec_ref[...] = jnp.zeros_like(vec_ref)

        @pl.when(i < nct)
        def _():
            dp_ref[...] = jnp.zeros_like(dp_ref)
            do_ref[...] = jnp.zeros_like(do_ref)

        @pl.when(i >= nct)
        def _():
            lng, lnb, gon, gpost, gate = lng_ref[...], lnb_ref[...], gon_ref[...], gpost_ref[...], gate_ref[...]
            zb, ua, va, za, ga, gb = [p_ref[:, j * D:(j + 1) * D] for j in range(6)]
            o = of_ref[...] + ob_ref[...]
            szb, dszb = _silu_g(zb)
            nh_l, r_l = [], []
            for h in range(NH):
                oh = o[:, h * DH:(h + 1) * DH]
                r = lax.rsqrt(jnp.mean(oh * oh, axis=-1, keepdims=True) + EPS)
                nh_l.append(oh * r)
                r_l.append(r)
            nrm_b = jnp.concatenate(nh_l, axis=-1)
            gon_t = jnp.concatenate([gon] * NH, axis=-1)
            y_b = nrm_b * gon_t * szb
            u, du_dua = _gelu_g(ua)
            gv, dgv_dva = _gelu_g(va)
            xc = gv - jnp.mean(gv, axis=-1, keepdims=True)
            rs_ln = lax.rsqrt(jnp.mean(xc * xc, axis=-1, keepdims=True) + EPS)
            vhat = xc * rs_ln
            v = vhat * lng + lnb
            s_sp = jnp.concatenate(
                [_mm(wsp[g], v[:, g * DH:(g + 1) * DH]) + bspb[g] for g in range(NH)], axis=-1)
            sza, dsza = _silu_g(za)
            y_a = u * s_sp * sza
            a_pr = _mm(y_a, wpa[...])
            b_pr = _mm(y_b, wpb[...])
            sga = _sigmoid(ga)
            sgb = _sigmoid(gb)
            merged = sga * a_pr + sgb * b_pr
            out = _mm(merged, wout[...])
            rs_o = lax.rsqrt(jnp.mean(out * out, axis=-1, keepdims=True) + EPS)
            n_o = out * rs_o
            rr = n_o * gpost
            diff = x_ref[...] + gate * rr - t_ref[...]
            vec_ref[5:6, :] += jnp.sum(diff * diff, axis=0, keepdims=True)
            dy = diff * (1.0 / D)
            dy_ref[...] = dy
            vec_ref[0:1, :] += jnp.sum(dy * rr, axis=0, keepdims=True)
            dr = dy * gate
            vec_ref[1:2, :] += jnp.sum(dr * n_o, axis=0, keepdims=True)
            dn_o = dr * gpost
            dout = rs_o * (dn_o - n_o * jnp.mean(dn_o * n_o, axis=-1, keepdims=True))
            dmerged = _mm(dout, wout[...], NT)
            d_a = dmerged * sga
            d_b = dmerged * sgb
            dga = dmerged * a_pr * sga * (1.0 - sga)
            dgb = dmerged * b_pr * sgb * (1.0 - sgb)
            dy_a = _mm(d_a, wpa[...], NT)
            dy_b = _mm(d_b, wpb[...], NT)
            ya_ref[...] = y_a.astype(ya_ref.dtype)
            yb_ref[...] = y_b.astype(yb_ref.dtype)
            mg_ref[...] = merged.astype(mg_ref.dtype)
            da_ref[...] = d_a.astype(da_ref.dtype)
            db_ref[...] = d_b.astype(db_ref.dtype)
            dout_ref[...] = dout.astype(dout_ref.dtype)
            dua = dy_a * s_sp * sza * du_dua
            ds_sp = dy_a * u * sza
            dza = dy_a * u * s_sp * dsza
            dv_l = []
            for g in range(NH):
                ds_g = ds_sp[:, g * DH:(g + 1) * DH]
                dv_l.append(_mm(wspt[g], ds_g))
                dwsp_ref[g] += _mm(ds_g, v[:, g * DH:(g + 1) * DH], NT)
                dbsp_ref[g] += ds_g
            dv = jnp.concatenate(dv_l, axis=-1)
            vec_ref[2:3, :] += jnp.sum(dv * vhat, axis=0, keepdims=True)
            vec_ref[3:4, :] += jnp.sum(dv, axis=0, keepdims=True)
            dvh = dv * lng
            dgv = rs_ln * (dvh - jnp.mean(dvh, axis=-1, keepdims=True) - vhat * jnp.mean(dvh * vhat, axis=-1, keepdims=True))
            dva = dgv * dgv_dva
            dzb = dy_b * nrm_b * gon_t * dszb
            dgon_full = jnp.sum(dy_b * nrm_b * szb, axis=0, keepdims=True)
            dgon = dgon_full[:, 0:DH]
            for h in range(1, NH):
                dgon = dgon + dgon_full[:, h * DH:(h + 1) * DH]
            vec_ref[4:5, 0:DH] += dgon
            dnb = dy_b * gon_t * szb
            do_l = []
            for h in range(NH):
                sl = slice(h * DH, (h + 1) * DH)
                dn_h = dnb[:, sl]
                do_l.append(r_l[h] * (dn_h - nh_l[h] * jnp.mean(dn_h * nh_l[h], axis=-1, keepdims=True)))
            do_ref[...] = jnp.concatenate(do_l, axis=-1)
            for j, val in enumerate((dzb, dua, dva, dza, dga, dgb)):
                dp_ref[:, j * D:(j + 1) * D] = val.astype(dp_ref.dtype)

    xrow = lambda i: (jnp.maximum(i - nct, 0), 0)
    wspec = _full((D, D))
    gspec = _full((NH, GC, GC))
    vspec = _full((1, D))
    bf_out = _sds((l, D), _BF)
    return pl.pallas_call(
        body, name="post", grid=(lt // tm,),
        out_shape=(_sds((lt, NREST), _BF), _sds((lt, D)), _sds((l, D)), bf_out, bf_out, bf_out, bf_out, bf_out, bf_out,
                   _sds((NH, GC, GC)), _sds((NH, GC, GC)), _sds((8, D))),
        in_specs=[pl.BlockSpec((tm, NREST), lambda i: (i, 0)), pl.BlockSpec((tm, D), lambda i: (i, 0)),
                  pl.BlockSpec((tm, D), lambda i: (i, 0)), pl.BlockSpec((tm, D), xrow), pl.BlockSpec((tm, D), xrow),
                  wspec, wspec, wspec, gspec, gspec, gspec, vspec, vspec, _full((1, DH)), vspec, vspec],
        out_specs=(pl.BlockSpec((tm, NREST), lambda i: (i, 0)), pl.BlockSpec((tm, D), lambda i: (i, 0)),
                   pl.BlockSpec((tm, D), xrow), pl.BlockSpec((tm, D), xrow), pl.BlockSpec((tm, D), xrow),
                   pl.BlockSpec((tm, D), xrow), pl.BlockSpec((tm, D), xrow), pl.BlockSpec((tm, D), xrow),
                   pl.BlockSpec((tm, D), xrow), gspec, gspec, _full((8, D))),
        compiler_params=_params(("arbitrary",)),
    )(p, o_f, o_b, x, tgt, w_pa, w_pb, w_out, w_sp, w_spt, b_spb, ln_g, ln_b, g_on, g_post, gate_x)


def _sum_parts(parts, name):
    r = parts.shape[1]
    tr = r if NDEV * r * LANE * 4 <= (8 << 20) else _tile(r, (512, 256, 128, 64, 32, 16, 8))

    def body(p_ref, o_ref):
        acc = p_ref[0]
        for s in range(1, NDEV):
            acc = acc + p_ref[s]
        o_ref[...] = acc

    return pl.pallas_call(
        body, name=name, out_shape=_sds((r, LANE)), grid=(r // tr,),
        in_specs=[pl.BlockSpec((NDEV, tr, LANE), lambda i: (0, i, 0))],
        out_specs=pl.BlockSpec((tr, LANE), lambda i: (i, 0)),
        compiler_params=_params(("parallel",)),
    )(parts)


def _mod_bwd(c_all, c_ctx, dmx, dmc, w_mod_g):
    ws = w_mod_g.shape[2]

    def body(ca_ref, cc_ref, dsh_ref, dmx_ref, dmc_ref, dmc_sh_ref, w_ref, gw_ref, gc_ref, gb_ref):
        sc, _ = _silu_g(ca_ref[...])
        scc, dscc = _silu_g(cc_ref[...])
        dmc_tot = jnp.sum(dmc_ref[...], axis=0, keepdims=True)
        gb_ref[...] = jnp.sum(dmx_ref[...], axis=0, keepdims=True) + dmc_tot
        lhs = jnp.concatenate([sc, jnp.broadcast_to(scc, (8, D))], axis=0)
        rhs = jnp.concatenate([dsh_ref[...], dmc_sh_ref[...]], axis=0)
        gw_ref[...] = _mmh(lhs, rhs, TN)
        acc = jnp.zeros((8, D), F32)
        tot8 = jnp.broadcast_to(dmc_tot, (8, 3 * D))
        for j in range(NDEV):
            acc = acc + _mm(tot8[:, j * ws:(j + 1) * ws], w_ref[j], NT)
        gc_ref[...] = acc[0:1, :] * dscc

    return pl.pallas_call(
        body, name="mod_bwd", out_shape=(_sds((D, ws)), _sds((1, D)), _sds((1, 3 * D))),
        compiler_params=_params(),
    )(c_all, c_ctx, _my_cols(dmx, ws), dmx, dmc, _my_cols(dmc, ws), w_mod_g)


def _my_cols(a, ws):
    me = 4 * lax.axis_index("x") + 2 * lax.axis_index("y") + lax.axis_index("c")
    return lax.dynamic_slice_in_dim(a, me * ws, ws, axis=1)


def _pair_sum(both, name):
    _, n, r, c = both.shape
    tr = _tile(r, (256, 128, 64, 32, 16, 8))

    def body(p_ref, o_ref):
        o_ref[...] = (p_ref[0].astype(F32) + p_ref[1].astype(F32)).astype(o_ref.dtype)

    return pl.pallas_call(
        body, name=name, out_shape=_sds((n, r, c), both.dtype), grid=(n, r // tr),
        in_specs=[pl.BlockSpec((2, 1, tr, c), lambda j, i: (0, j, i, 0))],
        out_specs=pl.BlockSpec((1, tr, c), lambda j, i: (j, i, 0)),
        compiler_params=_params(("parallel", "parallel")),
    )(both)


def _adamw(parts, w, m, v, name, chip_sums_below=None):
    s_, r, c = parts.shape
    tr = _tile(r, (128, 64, 32, 16, 8)) if r * c * 4 > (1 << 20) else r
    c1 = 1.0 / (1.0 - ADAM_B1 ** ADAM_STEP)
    c2 = 1.0 / (1.0 - ADAM_B2 ** ADAM_STEP)

    def body(p_ref, w_ref, m_ref, v_ref, g_ref, d_ref, nm_ref, nv_ref):
        if chip_sums_below is None:
            part = lambda s: p_ref[s].astype(F32)
        else:
            core = lax.axis_index("c")
            me = 4 * lax.axis_index("x") + 2 * lax.axis_index("y") + core
            every = me >= chip_sums_below
            part = lambda s: jnp.where(every | (core == s % 2), p_ref[s].astype(F32), 0.0)
        g = part(0)
        for s in range(1, s_):
            g = g + part(s)
        m_new = ADAM_B1 * m_ref[...] + (1.0 - ADAM_B1) * g
        v_new = ADAM_B2 * v_ref[...] + (1.0 - ADAM_B2) * (g * g)
        g_ref[...] = g
        nm_ref[...] = m_new
        nv_ref[...] = v_new
        d_ref[...] = -ADAM_LR * ((m_new * c1) / (jnp.sqrt(v_new * c2) + ADAM_EPS) + ADAM_WD * w_ref[...])

    blk = pl.BlockSpec((tr, c), lambda i: (i, 0))
    o = _sds((r, c))
    return pl.pallas_call(
        body, name=name, out_shape=(o, o, o, o), grid=(r // tr,),
        in_specs=[pl.BlockSpec((s_, tr, c), lambda i: (0, i, 0)), blk, blk, blk],
        out_specs=(blk, blk, blk, blk),
        compiler_params=_params(("parallel",)),
    )(parts, w, m, v)


def _rows(a):
    flat = a.reshape(-1)
    n = flat.shape[0]
    r = -(-n // (8 * LANE)) * 8
    return jnp.pad(flat, (0, r * LANE - n)).reshape(r, LANE)


def _pack(items):
    parts, layout, at = [], [], 0
    for name, a in items:
        rws = _rows(a.astype(F32))
        layout.append((name, at, rws.shape[0], a.shape))
        parts.append(rws)
        at += rws.shape[0]
    return jnp.concatenate(parts, axis=0), layout


def _unpack(packed, layout):
    out = {}
    for name, at, r, shape in layout:
        n = 1
        for s in shape:
            n *= s
        out[name] = packed[at:at + r].reshape(-1)[:n].reshape(shape)
    return out


def kernel(x, c, ctx, c_ctx, w_mod, b_mod, g_pre, g_post, w_in, w_conv, a_log, dt_bias, g_onorm, gm_ln_g, gm_ln_b, w_sp, b_sp, w_pa, w_pb, w_out, loss_target, m_c_ctx, m_w_mod, m_b_mod, m_g_pre, m_g_post, m_w_in, m_w_conv, m_a_log, m_dt_bias, m_g_onorm, m_gm_ln_g, m_gm_ln_b, m_w_sp, m_b_sp, m_w_pa, m_w_pb, m_w_out, v_c_ctx, v_w_mod, v_b_mod, v_g_pre, v_g_post, v_w_in, v_w_conv, v_a_log, v_dt_bias, v_g_onorm, v_gm_ln_g, v_gm_ln_b, v_w_sp, v_b_sp, v_w_pa, v_w_pb, v_w_out):
    l = x.shape[1]
    lc = ctx.shape[1]
    lt = l + lc
    nch = lt // CH
    me = 4 * lax.axis_index("x") + 2 * lax.axis_index("y") + lax.axis_index("c")
    wsh = w_in.shape[2]
    off_a = 3 * D
    n_ab = 4 * NH
    jb = off_a // wsh
    o1 = off_a - jb * wsh
    o2 = o1 + n_ab
    assert o2 <= wsh and NREST == (NDEV - jb) * wsh - o2
    split = jb + 1

    w_in_bf = w_in[0].astype(_BF)
    wg_lo, wg_mod, wg_conv, c_all = _exchange(
        [w_in_bf, w_mod[0].astype(_BF), w_conv[0], c], ["gather_lo", "gather", "gather", "gather"],
        "gather_first", split)
    w_qkv = jnp.concatenate([wg_lo[j] for j in range(jb)] + [wg_lo[jb][:, :o1]], axis=1)
    w_ab = jnp.pad(wg_lo[jb][:, o1:o2], ((0, 0), (0, LANE - n_ab)))
    wconv_full = jnp.moveaxis(wg_conv, 0, 1).reshape(3, 3 * D)
    c_all = c_all.reshape(NDEV, D)

    cc = jnp.concatenate([c, c_ctx.reshape(1, D), jnp.zeros((6, D), F32)], axis=0)
    mods = _modulation(cc, wg_mod, b_mod)
    xa = jnp.concatenate([ctx[0], x[0]], axis=0)
    h = _prenorm(xa, mods, g_pre, lc)
    p_qkv = _matmul_nn(h, w_qkv, "in_proj_qkv")
    pab = _matmul_nn(h, w_ab, "in_proj_ab")
    abt = jnp.swapaxes(pab[:, :n_ab].reshape(nch, CH, n_ab), 1, 2)
    alog16, dtb16 = a_log.reshape(1, 2 * NH), dt_bias.reshape(1, 2 * NH)
    alog_r = jnp.pad(alog16, ((0, 0), (0, LANE - 2 * NH)))
    dtb_r = jnp.pad(dtb16, ((0, 0), (0, LANE - 2 * NH)))
    alog_c = jnp.pad(alog16.reshape(2 * NH, 1), ((0, 2 * NH), (0, 0)))
    dtb_c = jnp.pad(dtb16.reshape(2 * NH, 1), ((0, 2 * NH), (0, 0)))
    qkv = _qkv_fwd(p_qkv, wconv_full, lc)
    late = [w_in_bf, w_pa[0].astype(_BF), w_pb[0].astype(_BF), w_out[0].astype(_BF)]
    xc_late = _Exchange(zip(late, ["gather_hi", "gather", "gather", "gather"]), split)
    o_f, o_b, s_f, s_b, t_f, t_b, wg_hi, wg_pa, wg_pb, wg_out = _gdn_fwd(
        qkv, pab, abt, alog_r, dtb_r, alog_c, dtb_c, lc, xc_late, late)
    w_rest = jnp.concatenate([wg_lo[jb][:, o2:]] + [wg_hi[j] for j in range(split, NDEV)], axis=1)
    wf_pa, wf_pb, wf_out = wg_pa.reshape(D, D), wg_pb.reshape(D, D), wg_out.reshape(D, D)
    p_rest = _matmul_nn(h, w_rest, "in_proj_rest")

    w_spt = jnp.swapaxes(w_sp[0], 1, 2)
    b_spb = jnp.broadcast_to(b_sp[0][:, :, None], (NH, GC, GC))
    gate_x = mods[0:1, 2 * D:]
    dp_rest, do, dy, ya, yb, mg, d_a, d_b, dout, dwsp, dbsp_l, pvec = _post(
        p_rest, o_f, o_b, x[0], loss_target[0], wf_pa, wf_pb, wf_out, w_sp[0], w_spt, b_spb, gm_ln_g, gm_ln_b,
        g_onorm, g_post, gate_x, lc)

    dw_rest = _matmul_tn(h, dp_rest, "dw_in_rest")
    o3 = wsh - o2
    chunks_hi = jnp.moveaxis(dw_rest[:, o3:].reshape(D, NDEV - split, wsh), 1, 0)
    dw_pa = _matmul_tn(ya, d_a, "dw_pa").reshape(NDEV, D // NDEV, D)
    dw_pb = _matmul_tn(yb, d_b, "dw_pb").reshape(NDEV, D // NDEV, D)
    dw_out = _matmul_tn(mg, dout, "dw_out").reshape(NDEV, D // NDEV, D)
    small_a, lay_a = _pack([
        ("g_post", pvec[1]), ("g_onorm", pvec[4, :DH]), ("gm_ln_g", pvec[2]), ("gm_ln_b", pvec[3]), ("w_sp", dwsp),
        ("b_sp", jnp.sum(dbsp_l, axis=-1)), ("loss", pvec[5]), ("dgate", pvec[0])])
    early = [chunks_hi, dw_pa, dw_pb, dw_out, small_a]
    xc_early = _Exchange(zip(early, ["scatter_hi", "scatter", "scatter", "scatter", "gather"]), split)

    dqkv_f, dqkv_b, dcol_f, dcol_b, drow_f, drow_b, gvec_c, gvec_r, r_in, r_pa, r_pb, r_out, small_a_all = _gdn_bwd(
        qkv, pab, abt, alog_r, dtb_r, alog_c, dtb_c, s_f, s_b, t_f, t_b, do, lc, xc_early, early)
    dp_qkv, dwconv = _qkv_bwd(p_qkv, wconv_full, dqkv_f, dqkv_b, lc)
    drow = jnp.swapaxes(drow_f + drow_b, 1, 2).reshape(lt, n_ab)
    dpab = (dcol_f + dcol_b + jnp.pad(drow, ((0, 0), (0, LANE - n_ab)))).astype(_BF)

    dw_qkv = _matmul_tn(h, dp_qkv, "dw_in_qkv")
    dw_ab = _matmul_tn(h, dpab, "dw_in_ab")
    dw_lo = jnp.concatenate([dw_qkv, dw_ab[:, :n_ab], dw_rest[:, :o3]], axis=1)
    chunks_lo = jnp.moveaxis(dw_lo.reshape(D, split, wsh), 1, 0)
    (both,) = _exchange([chunks_lo], ["sibling"], "pair_swap")
    chip_lo = _pair_sum(both, "pair_sum")
    xc_last = _Exchange([(chip_lo, "scatter_par_lo")], split)
    dh, r_in = _dh_matmul(dp_rest, dp_qkv, dpab, w_rest, w_qkv, w_ab, xc_last, [chip_lo], {0: r_in})
    grad_x, nvec = _prenorm_bwd(xa, dh, dy, mods, g_pre, lc)

    dalog = gvec_c[0, :2 * NH] + gvec_r[:2 * NH, 0]
    ddtb = gvec_c[1, :2 * NH] + gvec_r[:2 * NH, 1]
    small_b, lay_b = _pack([
        ("g_pre", nvec[4]), ("a_log", dalog), ("dt_bias", ddtb), ("w_conv", dwconv),
        ("dshift", nvec[0]), ("dscale", nvec[1]), ("dshift_c", nvec[2]), ("dscale_c", nvec[3])])
    (small_b_all,) = _exchange([small_b], ["gather"], "gather_small")
    tot = _unpack(_sum_parts(small_a_all, "sum_small_a"), lay_a)
    tot.update(_unpack(_sum_parts(small_b_all, "sum_small_b"), lay_b))

    def per_device(packed_all, layout, name):
        at, r = [(a_, r_) for nm, a_, r_, _ in layout if nm == name][0]
        return packed_all[:, at:at + r].reshape(NDEV, -1)

    dmx_all = jnp.concatenate([per_device(small_b_all, lay_b, "dshift"), per_device(small_b_all, lay_b, "dscale"),
                               per_device(small_a_all, lay_a, "dgate")], axis=1)
    dmc_all = jnp.concatenate([per_device(small_b_all, lay_b, "dshift_c"), per_device(small_b_all, lay_b, "dscale_c"),
                               jnp.zeros((NDEV, D), F32)], axis=1)
    g_wmod, g_cctx, g_bmod = _mod_bwd(c_all, c_ctx.reshape(1, D), dmx_all, dmc_all, wg_mod)
    loss = 0.5 / D * jnp.sum(tot["loss"])
    ws_conv = w_conv.shape[2]
    g_wconv = lax.dynamic_slice_in_dim(tot["w_conv"], me * ws_conv, ws_conv, axis=1)

    small_names = ["c_ctx", "b_mod", "g_pre", "g_post", "a_log", "dt_bias", "g_onorm", "gm_ln_g", "gm_ln_b",
                   "w_sp", "b_sp", "w_conv"]
    wts = dict(c_ctx=c_ctx, b_mod=b_mod, g_pre=g_pre, g_post=g_post, a_log=a_log, dt_bias=dt_bias, g_onorm=g_onorm,
               gm_ln_g=gm_ln_g, gm_ln_b=gm_ln_b, w_sp=w_sp, b_sp=b_sp, w_conv=w_conv)
    ms = dict(c_ctx=m_c_ctx, b_mod=m_b_mod, g_pre=m_g_pre, g_post=m_g_post, a_log=m_a_log, dt_bias=m_dt_bias,
              g_onorm=m_g_onorm, gm_ln_g=m_gm_ln_g, gm_ln_b=m_gm_ln_b, w_sp=m_w_sp, b_sp=m_b_sp, w_conv=m_w_conv)
    vs = dict(c_ctx=v_c_ctx, b_mod=v_b_mod, g_pre=v_g_pre, g_post=v_g_post, a_log=v_a_log, dt_bias=v_dt_bias,
              g_onorm=v_g_onorm, gm_ln_g=v_gm_ln_g, gm_ln_b=v_gm_ln_b, w_sp=v_w_sp, b_sp=v_b_sp, w_conv=v_w_conv)
    gs = dict(tot)
    gs.update(c_ctx=g_cctx, b_mod=g_bmod, w_conv=g_wconv)
    gpk, play = _pack([(nm, gs[nm].reshape(wts[nm].shape)) for nm in small_names])
    wpk, _ = _pack([(nm, wts[nm]) for nm in small_names])
    mpk, _ = _pack([(nm, ms[nm]) for nm in small_names])
    vpk, _ = _pack([(nm, vs[nm]) for nm in small_names])
    res_small = [_unpack(a, play) for a in _adamw(gpk[None], wpk, mpk, vpk, "adamw_small")]
    res_big = {
        "w_mod": _adamw(g_wmod[None], w_mod[0], m_w_mod[0], v_w_mod[0], "adamw_w_mod"),
        "w_in": _adamw(r_in, w_in[0], m_w_in[0], v_w_in[0], "adamw_w_in", chip_sums_below=split),
        "w_pa": _adamw(r_pa, w_pa[0], m_w_pa[0], v_w_pa[0], "adamw_w_pa"),
        "w_pb": _adamw(r_pb, w_pb[0], m_w_pb[0], v_w_pb[0], "adamw_w_pb"),
        "w_out": _adamw(r_out, w_out[0], m_w_out[0], v_w_out[0], "adamw_w_out"),
    }
    order = ["c_ctx", "w_mod", "b_mod", "g_pre", "g_post", "w_in", "w_conv", "a_log", "dt_bias", "g_onorm",
             "gm_ln_g", "gm_ln_b", "w_sp", "b_sp", "w_pa", "w_pb", "w_out"]
    outs = [loss, grad_x[None]]
    for k in range(4):
        for nm in order:
            if nm in res_big:
                outs.append(res_big[nm][k][None])
            else:
                outs.append(res_small[k][nm])
    return tuple(outs)
```

```python
import functools

import jax
import jax.numpy as jnp
from jax import lax
from jax.experimental import pallas as pl
from jax.experimental.pallas import tpu as pltpu

F32 = jnp.float32
_BF = jnp.bfloat16
_HI = lax.Precision.HIGHEST
D = 1024
NH = 8
DH = 128
CH = 64
GC = 128
NREST = 6 * D
NMAIN = NREST + 3 * D
EPS = 1e-6
LANE = 128
NDEV = 8
VMEM_LIMIT = 56 * 1024 * 1024
MESH = pl.DeviceIdType.MESH

ADAM_LR, ADAM_B1, ADAM_B2, ADAM_EPS, ADAM_WD, ADAM_STEP = 0.001, 0.9, 0.999, 1e-08, 0.01, 10

NN = ((1,), (0,))
NT = ((1,), (1,))
TN = ((0,), (0,))


def _dot(a, b, dims=NN, prec=None):
    return lax.dot_general(a, b, (dims, ((), ())), precision=prec, preferred_element_type=F32)


def _mm(a, b, dims=NN):
    return _dot(a.astype(_BF), b.astype(_BF), dims)


def _mmh(a, b, dims=NN):
    return _dot(a.astype(F32), b.astype(F32), dims, _HI)


def _split(a):
    hi = a.astype(_BF)
    return hi, (a - hi.astype(F32)).astype(_BF)


def _mm3(a, b, dims=NN):
    ah, al = _split(a)
    bh, bl = _split(b)
    return _dot(ah, bh, dims) + (_dot(ah, bl, dims) + _dot(al, bh, dims))


def _sigmoid(x):
    return 1.0 / (1.0 + jnp.exp(-x))


def _silu_g(x):
    s = _sigmoid(x)
    return x * s, s * (1.0 + x * (1.0 - s))


def _gelu_g(x):
    c = 0.7978845608028654
    t = jnp.tanh(c * (x + 0.044715 * (x * x * x)))
    cdf = 0.5 * (1.0 + t)
    return x * cdf, cdf + 0.5 * x * (1.0 - t * t) * c * (1.0 + 3 * 0.044715 * x * x)


def _softplus(x):
    return jnp.maximum(x, 0.0) + jnp.log(1.0 + jnp.exp(-jnp.abs(x)))


def _params(sem=None):
    return pltpu.CompilerParams(dimension_semantics=sem, vmem_limit_bytes=VMEM_LIMIT)


def _tile(n, pref):
    for t in pref:
        if n % t == 0:
            return t
    return n


def _full(shape):
    nd = len(shape)
    return pl.BlockSpec(shape, lambda *_: (0,) * nd)


def _sds(shape, dt=F32):
    return jax.ShapeDtypeStruct(shape, dt)


class _Exchange:
    def __init__(self, specs, split):
        self.specs = list(specs)
        self.split = split
        self.n = len(self.specs)
        def out(a, k):
            if k == "sibling":
                return (2,) + tuple(a.shape)
            return (NDEV,) + (tuple(a.shape) if k.startswith("gather") else tuple(a.shape[1:]))

        self.out_shape = tuple(_sds(out(a, k), a.dtype) for a, k in self.specs)
        self.scratch = [pltpu.SemaphoreType.DMA((self.n, NDEV - 1)), pltpu.SemaphoreType.DMA((self.n, NDEV - 1)),
                        pltpu.SemaphoreType.DMA((self.n,))]

    def _ok(self, kind, idx):
        if kind.endswith("_lo"):
            return idx < self.split
        if kind.endswith("_hi"):
            return idx >= self.split
        return True

    def _phases(self, ins, outs, sems):
        send_sems, recv_sems, loc_sems = sems
        x, y, c = lax.axis_index("x"), lax.axis_index("y"), lax.axis_index("c")
        me = 4 * x + 2 * y + c
        sib = (x, y, 1 - c)
        sib_idx = 4 * x + 2 * y + (1 - c)
        chips = [(1 - x, y), (x, 1 - y), (1 - x, 1 - y)]
        starts, forwards, waits = [], [], []
        for a, (_, kind) in enumerate(self.specs):
            ok = functools.partial(self._ok, kind)
            if kind.startswith("gather"):
                def copy(k, block, to, src=None, a=a):
                    rows = outs[a].at[block]
                    return pltpu.make_async_remote_copy(
                        src_ref=rows if src is None else src, dst_ref=rows, send_sem=send_sems.at[a, k],
                        recv_sem=recv_sems.at[a, k], device_id=to, device_id_type=MESH)

                loc = pltpu.make_async_copy(ins[a], outs[a].at[me], loc_sems.at[a])
                first = [copy(0, me, sib, ins[a])] + [copy(1 + j, me, (*chip, c), ins[a]) for j, chip in enumerate(chips)]
                starts += [(ok(me), loc.start)] + [(ok(me), cp.start) for cp in first]
                waits += [(ok(me), loc.wait)] + [(ok(me), cp.wait_send) for cp in first]
                for j, chip in enumerate(chips):
                    origin = 4 * chip[0] + 2 * chip[1] + c
                    passed = copy(4 + j, origin, sib)
                    forwards += [(ok(origin), copy(1 + j, origin, sib).wait_recv), (ok(origin), passed.start)]
                    waits.append((ok(origin), passed.wait_send))
                    other = 4 * chip[0] + 2 * chip[1] + (1 - c)
                    waits.append((ok(other), copy(4 + j, other, sib).wait_recv))
                waits.append((ok(sib_idx), copy(0, sib_idx, sib).wait_recv))
            elif kind == "sibling":
                loc = pltpu.make_async_copy(ins[a], outs[a].at[c], loc_sems.at[a])
                swap = pltpu.make_async_remote_copy(
                    src_ref=ins[a], dst_ref=outs[a].at[c], send_sem=send_sems.at[a, 0], recv_sem=recv_sems.at[a, 0],
                    device_id=sib, device_id_type=MESH)
                arrive = pltpu.make_async_remote_copy(
                    src_ref=ins[a], dst_ref=outs[a].at[1 - c], send_sem=send_sems.at[a, 0], recv_sem=recv_sems.at[a, 0],
                    device_id=sib, device_id_type=MESH)
                starts += [(True, loc.start), (True, swap.start)]
                waits += [(True, loc.wait), (True, swap.wait_send), (True, arrive.wait_recv)]
            else:
                base = self.split if kind.endswith("_hi") else 0
                same_core_only = kind.endswith("_par_lo")

                def src(idx, a=a, base=base):
                    return ins[a].at[jnp.clip(idx - base, 0, ins[a].shape[0] - 1)]

                loc = pltpu.make_async_copy(src(me), outs[a].at[me], loc_sems.at[a])
                starts.append((ok(me), loc.start))
                waits.append((ok(me), loc.wait))
                for k in range(1, NDEV):
                    if same_core_only and k & 1:
                        continue
                    px = 1 - x if (k >> 2) & 1 else x
                    py = 1 - y if (k >> 1) & 1 else y
                    pc = 1 - c if k & 1 else c
                    pidx = 4 * px + 2 * py + pc
                    sems_k = dict(send_sem=send_sems.at[a, k - 1], recv_sem=recv_sems.at[a, k - 1],
                                  device_id=(px, py, pc), device_id_type=MESH)
                    send = pltpu.make_async_remote_copy(src_ref=src(pidx), dst_ref=outs[a].at[me], **sems_k)
                    arrive = pltpu.make_async_remote_copy(src_ref=src(pidx), dst_ref=outs[a].at[pidx], **sems_k)
                    starts.append((ok(pidx), send.start))
                    waits += [(ok(pidx), send.wait_send), (ok(me), arrive.wait_recv)]
        return starts, forwards, waits

    @staticmethod
    def _run(actions):
        for cond, fn in actions:
            if cond is True:
                fn()
            else:
                pl.when(cond)(fn)

    def start(self, ins, outs, sems):
        self._run(self._phases(ins, outs, sems)[0])

    def forward(self, ins, outs, sems):
        self._run(self._phases(ins, outs, sems)[1])

    def wait(self, ins, outs, sems):
        self._run(self._phases(ins, outs, sems)[2])


_ANY = pl.BlockSpec(memory_space=pl.ANY)


def _exchange(arrays, kinds, name, split=0, into=None):
    xc = _Exchange(zip(arrays, kinds), split)
    n = xc.n
    into = into or {}
    ni = len(into)

    def body(*refs):
        ins, outs, sems = refs[:n], refs[n + ni:2 * n + ni], refs[2 * n + ni:]
        xc.start(ins, outs, sems)
        xc.forward(ins, outs, sems)
        xc.wait(ins, outs, sems)

    return pl.pallas_call(
        body, name=name, out_shape=xc.out_shape, in_specs=[_ANY] * (n + ni), out_specs=tuple([_ANY] * n),
        scratch_shapes=xc.scratch, input_output_aliases={n + t: a for t, a in enumerate(into)},
    )(*arrays, *into.values())


def _matmul_nn(a, b, name):
    m, kk = a.shape
    n = b.shape[1]
    tm = _tile(m, (1088, 1024, 640, 512, 256, 128))
    tn = _tile(n, (512, 256, 128))

    def body(a_ref, b_ref, o_ref):
        o_ref[...] = _mm(a_ref[...], b_ref[...])

    return pl.pallas_call(
        body, name=name, out_shape=_sds((m, n)), grid=(n // tn, m // tm),
        in_specs=[pl.BlockSpec((tm, kk), lambda j, i: (i, 0)), pl.BlockSpec((kk, tn), lambda j, i: (0, j))],
        out_specs=pl.BlockSpec((tm, tn), lambda j, i: (i, j)),
        compiler_params=_params(("parallel", "parallel")),
    )(a, b)


def _matmul_tn(a, b, name):
    kk, m = a.shape
    n = b.shape[1]
    tk = _tile(kk, (1088, 1024, 640, 512, 256, 128))
    tn = _tile(n, (1024, 512, 256, 128))
    nk = kk // tk

    def body(a_ref, b_ref, o_ref, acc_ref):
        k = pl.program_id(1)

        @pl.when(k == 0)
        def _():
            acc_ref[...] = jnp.zeros_like(acc_ref)

        acc_ref[...] += _mm(a_ref[...], b_ref[...], TN)

        @pl.when(k == nk - 1)
        def _():
            o_ref[...] = acc_ref[...].astype(o_ref.dtype)

    return pl.pallas_call(
        body, name=name, out_shape=_sds((m, n), _BF), grid=(n // tn, nk),
        in_specs=[pl.BlockSpec((tk, m), lambda j, k: (k, 0)), pl.BlockSpec((tk, tn), lambda j, k: (k, j))],
        out_specs=pl.BlockSpec((m, tn), lambda j, k: (0, j)),
        scratch_shapes=[pltpu.VMEM((m, tn), F32)],
        compiler_params=_params(("parallel", "arbitrary")),
    )(a, b)


def _dh_matmul(dp_rest, dp_qkv, dpab, w_rest, w_qkv, w_ab, xc, xc_arrays, xc_into):
    lt = dp_rest.shape[0]
    tm = _tile(lt, (1088, 1024, 640, 512, 256, 128))
    nr, nq = dp_rest.shape[1] // D, dp_qkv.shape[1] // D
    nx, ni = xc.n, len(xc_into)
    ni_steps = lt // tm

    def body(*refs):
        dr_ref, dq_ref, ab_ref, wr_ref, wq_ref, wab_ref = refs[:6]
        x_in = refs[6:6 + nx]
        o_ref = refs[6 + nx + ni]
        x_out = refs[7 + nx + ni:7 + 2 * nx + ni]
        sems = refs[7 + 2 * nx + ni:]
        i = pl.program_id(0)
        k = pl.program_id(1)

        @pl.when((i == 0) & (k == 0))
        def _():
            xc.start(x_in, x_out, sems)

        @pl.when(k == 0)
        def _():
            o_ref[...] = _mm(ab_ref[...], wab_ref[...], NT)

        @pl.when(k < nr)
        def _():
            o_ref[...] += _mm(dr_ref[...], wr_ref[...], NT)

        @pl.when(k >= nr)
        def _():
            o_ref[...] += _mm(dq_ref[...], wq_ref[...], NT)

        @pl.when((i == ni_steps - 1) & (k == nr + nq - 1))
        def _():
            xc.wait(x_in, x_out, sems)

    rk = lambda k: jnp.minimum(k, nr - 1)
    qk = lambda k: jnp.maximum(k - nr, 0)
    return pl.pallas_call(
        body, name="dh_matmul", out_shape=(_sds((lt, D)),) + xc.out_shape, grid=(lt // tm, nr + nq),
        in_specs=[pl.BlockSpec((tm, D), lambda i, k: (i, rk(k))), pl.BlockSpec((tm, D), lambda i, k: (i, qk(k))),
                  pl.BlockSpec((tm, LANE), lambda i, k: (i, 0)),
                  pl.BlockSpec((D, D), lambda i, k: (0, rk(k))), pl.BlockSpec((D, D), lambda i, k: (0, qk(k))),
                  _full((D, LANE))] + [_ANY] * (nx + ni),
        out_specs=(pl.BlockSpec((tm, D), lambda i, k: (i, 0)),) + tuple([_ANY] * nx),
        scratch_shapes=xc.scratch, input_output_aliases={6 + nx + t: 1 + a for t, a in enumerate(xc_into)},
        compiler_params=_params(("arbitrary", "arbitrary")),
    )(dp_rest, dp_qkv, dpab, w_rest, w_qkv, w_ab, *xc_arrays, *xc_into.values())


def _modulation(cc, w_mod_g, b_mod):
    ws = w_mod_g.shape[2]

    def body(c_ref, w_ref, b_ref, o_ref):
        s, _ = _silu_g(c_ref[...])
        o_ref[...] = _mm(s, w_ref[0]) + b_ref[...]

    return pl.pallas_call(
        body, name="modulation", out_shape=_sds((8, 3 * D)), grid=(NDEV,),
        in_specs=[_full((8, D)), pl.BlockSpec((1, D, ws), lambda j: (j, 0, 0)), pl.BlockSpec((1, ws), lambda j: (0, j))],
        out_specs=pl.BlockSpec((8, ws), lambda j: (0, j)),
        compiler_params=_params(("parallel",)),
    )(cc, w_mod_g, b_mod)


def _prenorm(xa, mods, g_pre, lc):
    lt = xa.shape[0]
    tm = _tile(lc, (256, 128))
    nct = lc // tm

    def body(x_ref, m_ref, g_ref, o_ref):
        x = x_ref[...]
        is_ctx = pl.program_id(0) < nct
        shift = jnp.where(is_ctx, m_ref[1:2, 0:D], m_ref[0:1, 0:D])
        scale = jnp.where(is_ctx, m_ref[1:2, D:2 * D], m_ref[0:1, D:2 * D])
        r = lax.rsqrt(jnp.mean(x * x, axis=-1, keepdims=True) + EPS)
        o_ref[...] = ((x * r * g_ref[...]) * (1.0 + scale) + shift).astype(o_ref.dtype)

    return pl.pallas_call(
        body, name="prenorm", out_shape=_sds((lt, D), _BF), grid=(lt // tm,),
        in_specs=[pl.BlockSpec((tm, D), lambda i: (i, 0)), _full((8, 3 * D)), _full((1, D))],
        out_specs=pl.BlockSpec((tm, D), lambda i: (i, 0)),
        compiler_params=_params(("parallel",)),
    )(xa, mods, g_pre)


def _prenorm_bwd(xa, dh, dy, mods, g_pre, lc):
    lt = xa.shape[0]
    tm = _tile(lc, (256, 128))
    nct = lc // tm
    nl = (lt - lc) // tm

    def body(x_ref, dh_ref, dy_ref, m_ref, g_ref, gx_ref, vec_ref):
        i = pl.program_id(0)

        @pl.when(i == 0)
        def _():
            vec_ref[...] = jnp.zeros_like(vec_ref)

        x = x_ref[...]
        dh = dh_ref[...]
        g = g_ref[...]
        is_ctx = i < nct
        scale = jnp.where(is_ctx, m_ref[1:2, D:2 * D], m_ref[0:1, D:2 * D])
        r = lax.rsqrt(jnp.mean(x * x, axis=-1, keepdims=True) + EPS)
        n = x * r
        hn = n * g
        dsh = jnp.sum(dh, axis=0, keepdims=True)
        dsc = jnp.sum(dh * hn, axis=0, keepdims=True)
        dhn = dh * (1.0 + scale)
        vec_ref[4:5, :] += jnp.sum(dhn * n, axis=0, keepdims=True)
        dn = dhn * g
        dx = r * (dn - n * jnp.mean(dn * n, axis=-1, keepdims=True))

        @pl.when(is_ctx)
        def _():
            vec_ref[2:3, :] += dsh
            vec_ref[3:4, :] += dsc

        @pl.when(jnp.logical_not(is_ctx))
        def _():
            vec_ref[0:1, :] += dsh
            vec_ref[1:2, :] += dsc
            gx_ref[...] = dy_ref[...] + dx

    xrow = lambda i: (jnp.maximum(i - nct, 0), 0)
    return pl.pallas_call(
        body, name="prenorm_bwd", out_shape=(_sds((nl * tm, D)), _sds((8, D))), grid=(lt // tm,),
        in_specs=[pl.BlockSpec((tm, D), lambda i: (i, 0)), pl.BlockSpec((tm, D), lambda i: (i, 0)),
                  pl.BlockSpec((tm, D), xrow), _full((8, 3 * D)), _full((1, D))],
        out_specs=(pl.BlockSpec((tm, D), xrow), _full((8, D))),
        compiler_params=_params(("arbitrary",)),
    )(xa, dh, dy, mods, g_pre)


def _conv_parts(x, w, lc):
    lt = x.shape[0]
    row = lax.broadcasted_iota(jnp.int32, x.shape, 0)
    first = (row == 0) | (row == lc)
    last = (row == lc - 1) | (row == lt - 1)
    xp = jnp.where(first, 0.0, pltpu.roll(x, 1, 0))
    xn = jnp.where(last, 0.0, pltpu.roll(x, lt - 1, 0))
    y = w[0:1, :] * xp + w[1:2, :] * x + w[2:3, :] * xn
    return xp, xn, y, first, last


def _qkv_fwd(p, w_conv, lc):
    lt = p.shape[0]

    def body(p_ref, w_ref, o_ref):
        _, _, y, _, _ = _conv_parts(p_ref[...], w_ref[...], lc)
        s, _ = _silu_g(y)
        rs = lax.rsqrt(jnp.sum(s * s, axis=-1, keepdims=True) + EPS)
        o_ref[...] = s * jnp.where(pl.program_id(0) < 2 * NH, rs, 1.0)

    return pl.pallas_call(
        body, name="qkv_fwd", out_shape=_sds((lt, 3 * D)), grid=(3 * NH,),
        in_specs=[pl.BlockSpec((lt, DH), lambda j: (0, j)), pl.BlockSpec((3, DH), lambda j: (0, j))],
        out_specs=pl.BlockSpec((lt, DH), lambda j: (0, j)),
        compiler_params=_params(("parallel",)),
    )(p, w_conv)


def _qkv_bwd(p, w_conv, dqkv_f, dqkv_b, lc):
    lt = p.shape[0]

    def body(p_ref, w_ref, df_ref, db_ref, dp_ref, dw_ref):
        w = w_ref[...]
        xp, xn, y, first, last = _conv_parts(p_ref[...], w, lc)
        s, ds_dy = _silu_g(y)
        dn = df_ref[...] + db_ref[...]
        rs = lax.rsqrt(jnp.sum(s * s, axis=-1, keepdims=True) + EPS)
        nrm = s * rs
        ds_n = rs * (dn - nrm * jnp.sum(dn * nrm, axis=-1, keepdims=True))
        ds = jnp.where(pl.program_id(0) < 2 * NH, ds_n, dn)
        dy = ds * ds_dy
        dw_ref[0:1, :] = jnp.sum(dy * xp, axis=0, keepdims=True)
        dw_ref[1:2, :] = jnp.sum(dy * p_ref[...], axis=0, keepdims=True)
        dw_ref[2:3, :] = jnp.sum(dy * xn, axis=0, keepdims=True)
        dyn = jnp.where(last, 0.0, pltpu.roll(dy, lt - 1, 0))
        dyp = jnp.where(first, 0.0, pltpu.roll(dy, 1, 0))
        dp_ref[...] = (w[1:2, :] * dy + w[0:1, :] * dyn + w[2:3, :] * dyp).astype(dp_ref.dtype)

    return pl.pallas_call(
        body, name="qkv_bwd", out_shape=(_sds((lt, 3 * D), _BF), _sds((3, 3 * D))), grid=(3 * NH,),
        in_specs=[pl.BlockSpec((lt, DH), lambda j: (0, j)), pl.BlockSpec((3, DH), lambda j: (0, j)),
                  pl.BlockSpec((lt, DH), lambda j: (0, j)), pl.BlockSpec((lt, DH), lambda j: (0, j))],
        out_specs=(pl.BlockSpec((lt, DH), lambda j: (0, j)), pl.BlockSpec((3, DH), lambda j: (0, j))),
        compiler_params=_params(("parallel",)),
    )(p, w_conv, dqkv_f, dqkv_b)


def _masks(d):
    ri = lax.broadcasted_iota(jnp.int32, (CH, CH), 0)
    ci = lax.broadcasted_iota(jnp.int32, (CH, CH), 1)
    incl = (ri >= ci) if d == 0 else (ri <= ci)
    strict = (ri > ci) if d == 0 else (ri < ci)
    incl_t = (ri <= ci) if d == 0 else (ri >= ci)
    return incl, strict, incl_t, ri == ci


def _decays(d, ab, abt, alog_r, dtb_r, alog_c, dtb_c, incl, incl_t):
    g_full = -jnp.exp(alog_r) * _softplus(ab + dtb_r)
    beta_full = _sigmoid(ab)
    gc_full = _mmh(incl.astype(F32), g_full)
    gl_full = jnp.sum(g_full, axis=0, keepdims=True)
    gt_full = -jnp.exp(alog_c) * _softplus(abt + dtb_c)
    gct = _mmh(gt_full, incl_t.astype(F32))
    return g_full, beta_full, gc_full, gl_full, gt_full, gct


def _lane_onehot(idx, n=LANE):
    return (lax.broadcasted_iota(jnp.int32, (1, n), 1) == idx).astype(F32)


def _head_scalars(d, h, beta_full, gc_full, gl_full, gct):
    idx = d * NH + h
    oh = _lane_onehot(idx)
    gcol = jnp.sum(gc_full * oh, axis=-1, keepdims=True)
    bcol = jnp.sum(beta_full * _lane_onehot(2 * NH + idx), axis=-1, keepdims=True)
    gl = jnp.sum(gl_full * oh, axis=-1, keepdims=True)
    grow = gct[idx:idx + 1, :]
    return gcol, grow, bcol, gl


def _lockstep(gens):
    live = list(gens)
    while live:
        nxt = []
        for g in live:
            try:
                next(g)
                nxt.append(g)
            except StopIteration:
                pass
        live = nxt


def _chunk_local(qh, kh, vh, gcol, grow, bcol, gl, incl, strict):
    decay = jnp.where(incl, jnp.exp(gcol - grow), 0.0)
    kb = kh * bcol
    a = jnp.where(strict, _mm(kb, kh, NT) * decay, 0.0)
    egc = jnp.exp(gcol)
    rhs_u = vh * bcol
    rhs_w = kb * egc
    qs = qh * (DH ** -0.5)
    attn = jnp.where(incl, _mm(qs, kh, NT) * decay, 0.0)
    etail = jnp.exp(gl - gcol)
    return decay, kb, a, egc, rhs_u, rhs_w, qs, attn, etail


def _scan_specs(lt, lc, bwd_pass):
    nch = lt // CH
    ncc = lc // CH
    if not bwd_pass:
        cf = lambda s: s
        cb = lambda s: jnp.where(s < ncc, ncc - 1 - s, nch + ncc - 1 - s)
    else:
        cf = lambda s: nch - 1 - s
        cb = lambda s: jnp.where(s < nch - ncc, ncc + s, s - (nch - ncc))
    return nch, cf, cb


def _gdn_fwd(qkv, pab, abt, alog_r, dtb_r, alog_c, dtb_c, lc, xc, xc_arrays):
    lt = qkv.shape[0]
    nch, cf, cb = _scan_specs(lt, lc, False)
    nx = xc.n

    def body(*refs):
        qf, kf, vf, abf, abtf, qb, kb_, vb, abb, abtb, ar, dr, ac, dc = refs[:14]
        x_in = refs[14:14 + nx]
        of_ref, ob_ref, sf_ref, sb_ref, tf_ref, tb_ref = refs[14 + nx:20 + nx]
        x_out = refs[20 + nx:20 + 2 * nx]
        s_scr = refs[20 + 2 * nx]
        sems = refs[21 + 2 * nx:]

        @pl.when(pl.program_id(0) == 0)
        def _():
            s_scr[...] = jnp.zeros_like(s_scr)
            xc.start(x_in, x_out, sems)

        def chain(d, h, q_r, k_r, v_r, o_ref, sh_ref, th_ref, masks, decs):
            incl, strict, _, eye = masks
            sl = slice(h * DH, (h + 1) * DH)
            qh, kh, vh = q_r[:, sl], k_r[:, sl], v_r[:, sl]
            gcol, grow, bcol, gl = _head_scalars(d, h, *decs)
            _, _, a, egc, rhs_u, rhs_w, qs, attn, etail = _chunk_local(qh, kh, vh, gcol, grow, bcol, gl, incl, strict)
            yield
            n = -a
            t = jnp.where(eye, 1.0, 0.0) + n
            p = _mm3(n, n)
            yield
            for _ in range(4):
                r = _mm3(jnp.concatenate([t, p], axis=0), p)
                yield
                t = t + r[:CH]
                p = r[CH:]
            t = t + _mm3(t, p)
            yield
            sol = _mm3(t, jnp.concatenate([rhs_u, rhs_w], axis=1))
            u, w = sol[:, :DH], sol[:, DH:]
            s = s_scr[d, h]
            sh_ref[0, h] = s
            th_ref[0, h] = t
            yield
            ws = _mm(jnp.concatenate([w, qs * egc], axis=0), s)
            yield
            v_new = u - ws[:CH]
            o_ref[:, sl] = ws[CH:] + _mm(attn, v_new)
            s_scr[d, h] = s * jnp.exp(gl) + _mm(kh * etail, v_new, TN)

        chains = []
        for d, (q_r, k_r, v_r, ab_r, abt_r, o_ref, sh_ref, th_ref) in enumerate(
                ((qf, kf, vf, abf, abtf, of_ref, sf_ref, tf_ref), (qb, kb_, vb, abb, abtb, ob_ref, sb_ref, tb_ref))):
            masks = _masks(d)
            _, beta_full, gc_full, gl_full, _, gct = _decays(
                d, ab_r[...], abt_r[0], ar[...], dr[...], ac[...], dc[...], masks[0], masks[2])
            for h in range(NH):
                chains.append(chain(d, h, q_r, k_r, v_r, o_ref, sh_ref, th_ref, masks, (beta_full, gc_full, gl_full, gct)))
        _lockstep(chains)

        @pl.when(pl.program_id(0) == nch // 2)
        def _():
            xc.forward(x_in, x_out, sems)

        @pl.when(pl.program_id(0) == nch - 1)
        def _():
            xc.wait(x_in, x_out, sems)

    def row(c, col):
        return pl.BlockSpec((CH, D), lambda s: (c(s), col))

    def chunk_in(c):
        return [row(c, 0), row(c, 1), row(c, 2), pl.BlockSpec((CH, LANE), lambda s: (c(s), 0)),
                pl.BlockSpec((1, 4 * NH, CH), lambda s: (c(s), 0, 0))]

    def hist(c, n):
        return pl.BlockSpec((1, NH, n, n), lambda s: (c(s), 0, 0, 0))

    small = [_full((1, LANE)), _full((1, LANE)), _full((4 * NH, 1)), _full((4 * NH, 1))]
    return pl.pallas_call(
        body, name="gdn_fwd", grid=(nch,),
        out_shape=(_sds((lt, D)), _sds((lt, D)), _sds((nch, NH, DH, DH)), _sds((nch, NH, DH, DH)),
                   _sds((nch, NH, CH, CH)), _sds((nch, NH, CH, CH))) + xc.out_shape,
        in_specs=chunk_in(cf) + chunk_in(cb) + small + [_ANY] * nx,
        out_specs=(pl.BlockSpec((CH, D), lambda s: (cf(s), 0)), pl.BlockSpec((CH, D), lambda s: (cb(s), 0)),
                   hist(cf, DH), hist(cb, DH), hist(cf, CH), hist(cb, CH)) + tuple([_ANY] * nx),
        scratch_shapes=[pltpu.VMEM((2, NH, DH, DH), F32)] + xc.scratch,
        compiler_params=_params(("arbitrary",)),
    )(qkv, qkv, qkv, pab, abt, qkv, qkv, qkv, pab, abt, alog_r, dtb_r, alog_c, dtb_c, *xc_arrays)


def _gdn_bwd(qkv, pab, abt, alog_r, dtb_r, alog_c, dtb_c, s_f, s_b, t_f, t_b, do, lc, xc, xc_arrays):
    lt = qkv.shape[0]
    nch, cf, cb = _scan_specs(lt, lc, True)
    nx = xc.n

    def body(*refs):
        qf, kf, vf, abf, abtf, sf_ref, tf_ref, dof, qb, kb_, vb, abb, abtb, sb_ref, tb_ref, dob, ar, dr, ac, dc = refs[:20]
        x_in = refs[20:20 + nx]
        dqf_ref, dqb_ref, dcf_ref, dcb_ref, drf_ref, drb_ref, vcol_ref, vrow_ref = refs[20 + nx:28 + nx]
        x_out = refs[28 + nx:28 + 2 * nx]
        ds_scr = refs[28 + 2 * nx]
        sems = refs[29 + 2 * nx:]

        @pl.when(pl.program_id(0) == 0)
        def _():
            ds_scr[...] = jnp.zeros_like(ds_scr)
            vcol_ref[...] = jnp.zeros_like(vcol_ref)
            vrow_ref[...] = jnp.zeros_like(vrow_ref)
            xc.start(x_in, x_out, sems)

        alog_r_, dtb_r_, alog_c_, dtb_c_ = ar[...], dr[...], ac[...], dc[...]
        lane2 = lax.broadcasted_iota(jnp.int32, (1, LANE), 1)
        acc = [[], []]

        def chain(d, h, q_r, k_r, v_r, sh_ref, th_ref, do_r, dq_ref, masks, decs):
            incl, strict, _, _ = masks
            idx = d * NH + h
            sl = slice(h * DH, (h + 1) * DH)
            qh, kh, vh = q_r[:, sl], k_r[:, sl], v_r[:, sl]
            doh = do_r[:, sl]
            gcol, grow, bcol, gl = _head_scalars(d, h, *decs)
            decay, kb, a, egc, rhs_u, rhs_w, qs, attn, etail = _chunk_local(qh, kh, vh, gcol, grow, bcol, gl, incl, strict)
            t = th_ref[0, h]
            s = sh_ref[0, h]
            ds_new = ds_scr[d, h]
            sol = _mm3(t, jnp.concatenate([rhs_u, rhs_w], axis=1))
            u, w = sol[:, :DH], sol[:, DH:]
            q_dec = qs * egc
            k_tail = kh * etail
            egl = jnp.exp(gl)
            dv_new = _mm(attn, doh, TN) + _mm(k_tail, ds_new)
            dq_dec = _mm(doh, s, NT)
            dgl = jnp.sum(jnp.sum(ds_new * s, axis=0, keepdims=True), axis=-1, keepdims=True) * egl
            yield
            v_new = u - _mm(w, s)
            dw = -_mm(dv_new, s, NT)
            ds_scr[d, h] = ds_new * egl + _mm(q_dec, doh, TN) - _mm(w, dv_new, TN)
            yield
            dattn = jnp.where(incl, _mm(doh, v_new, NT), 0.0)
            dk_tail = _mm(v_new, ds_new, NT)
            dr = _mm3(t, jnp.concatenate([dv_new, dw], axis=1), TN)
            dr_u, dr_w = dr[:, :DH], dr[:, DH:]
            yield
            da = -jnp.where(strict, _mm3(dr, sol, NT), 0.0)
            nq = dattn * decay
            dqs = _mm(nq, kh) + dq_dec * egc
            dk = _mm(nq, qs, TN)
            yield
            dv = dr_u * bcol
            dbeta = jnp.sum(dr_u * vh, axis=-1, keepdims=True)
            dgc = jnp.sum(dr_w * rhs_w, axis=-1, keepdims=True)
            m = da * decay
            dkb = dr_w * egc + _mm(m, kh)
            dk = dk + _mm(m, kb, TN)
            pq = da * a + dattn * attn
            dgc = dgc + jnp.sum(pq, axis=-1, keepdims=True) + jnp.sum(dq_dec * q_dec, axis=-1, keepdims=True)
            dgr = -jnp.sum(pq, axis=0, keepdims=True)
            tt = jnp.sum(dk_tail * k_tail, axis=-1, keepdims=True)
            dk = dk + dk_tail * etail + dkb * bcol
            dgc = dgc - tt
            dgl = dgl + jnp.sum(tt, axis=0, keepdims=True)
            dbeta = dbeta + jnp.sum(dkb * kh, axis=-1, keepdims=True)
            dq_ref[:, sl] = dqs * (DH ** -0.5)
            dq_ref[:, D + h * DH:D + (h + 1) * DH] = dk
            dq_ref[:, 2 * D + h * DH:2 * D + (h + 1) * DH] = dv
            acc[d].append((idx, dgc, dgl, dbeta, dgr))

        dirs = ((qf, kf, vf, abf, abtf, sf_ref, tf_ref, dof, dqf_ref, dcf_ref, drf_ref),
                (qb, kb_, vb, abb, abtb, sb_ref, tb_ref, dob, dqb_ref, dcb_ref, drb_ref))
        chains, ctx_d = [], []
        for d, (q_r, k_r, v_r, ab_r, abt_r, sh_ref, th_ref, do_r, dq_ref, _, _) in enumerate(dirs):
            masks = _masks(d)
            ab, abt = ab_r[...], abt_r[0]
            g_full, beta_full, gc_full, gl_full, gt_full, gct = _decays(
                d, ab, abt, alog_r_, dtb_r_, alog_c_, dtb_c_, masks[0], masks[2])
            ctx_d.append((masks, ab, abt, g_full, beta_full, gt_full))
            for h in range(NH):
                chains.append(chain(d, h, q_r, k_r, v_r, sh_ref, th_ref, do_r, dq_ref, masks,
                                    (beta_full, gc_full, gl_full, gct)))
        _lockstep(chains)
        for d in range(2):
            (incl, _, incl_t, _), ab, abt, g_full, beta_full, gt_full = ctx_d[d]
            dcol_ref, drow_ref = dirs[d][9], dirs[d][10]
            dgc_col = jnp.zeros((CH, LANE), F32)
            dgl_row = jnp.zeros((1, LANE), F32)
            dbeta_col = jnp.zeros((CH, LANE), F32)
            dgc_row = jnp.zeros((4 * NH, CH), F32)
            for idx, dgc, dgl, dbeta, dgr in acc[d]:
                oh = _lane_onehot(idx)
                dgc_col = dgc_col + dgc * oh
                dgl_row = dgl_row + dgl * oh
                dbeta_col = dbeta_col + dbeta * _lane_onehot(2 * NH + idx)
                ohc = (lax.broadcasted_iota(jnp.int32, (4 * NH, 1), 0) == idx).astype(F32)
                dgc_row = dgc_row + ohc * dgr
            dg_col = _mmh(incl_t.astype(F32), dgc_col) + dgl_row
            dg_row = _mmh(dgc_row, incl.astype(F32))
            sg_col = _sigmoid(ab + dtb_r_)
            da_col = dg_col * (-jnp.exp(alog_r_)) * sg_col
            dcol_ref[...] = da_col + dbeta_col * beta_full * (1.0 - beta_full)
            da_row = dg_row * (-jnp.exp(alog_c_)) * _sigmoid(abt + dtb_c_)
            drow_ref[0] = da_row
            vcol_ref[0:1, :] += jnp.sum(dg_col * g_full, axis=0, keepdims=True)
            vcol_ref[1:2, :] += jnp.sum(da_col, axis=0, keepdims=True)
            rl = jnp.sum(dg_row * gt_full, axis=-1, keepdims=True)
            rd = jnp.sum(da_row, axis=-1, keepdims=True)
            vrow_ref[...] += jnp.where(lane2 == 0, rl, 0.0) + jnp.where(lane2 == 1, rd, 0.0)

        @pl.when(pl.program_id(0) == nch // 2)
        def _():
            xc.forward(x_in, x_out, sems)

        @pl.when(pl.program_id(0) == nch - 1)
        def _():
            xc.wait(x_in, x_out, sems)

    def row(c, col):
        return pl.BlockSpec((CH, D), lambda s: (c(s), col))

    def hist(c, n):
        return pl.BlockSpec((1, NH, n, n), lambda s: (c(s), 0, 0, 0))

    def chunk_in(c):
        return [row(c, 0), row(c, 1), row(c, 2), pl.BlockSpec((CH, LANE), lambda s: (c(s), 0)),
                pl.BlockSpec((1, 4 * NH, CH), lambda s: (c(s), 0, 0)), hist(c, DH), hist(c, CH), row(c, 0)]

    small = [_full((1, LANE)), _full((1, LANE)), _full((4 * NH, 1)), _full((4 * NH, 1))]
    return pl.pallas_call(
        body, name="gdn_bwd", grid=(nch,),
        out_shape=(_sds((lt, 3 * D)), _sds((lt, 3 * D)), _sds((lt, LANE)), _sds((lt, LANE)),
                   _sds((nch, 4 * NH, CH)), _sds((nch, 4 * NH, CH)), _sds((8, LANE)), _sds((4 * NH, LANE))) + xc.out_shape,
        in_specs=chunk_in(cf) + chunk_in(cb) + small + [_ANY] * nx,
        out_specs=(pl.BlockSpec((CH, 3 * D), lambda s: (cf(s), 0)), pl.BlockSpec((CH, 3 * D), lambda s: (cb(s), 0)),
                   pl.BlockSpec((CH, LANE), lambda s: (cf(s), 0)), pl.BlockSpec((CH, LANE), lambda s: (cb(s), 0)),
                   pl.BlockSpec((1, 4 * NH, CH), lambda s: (cf(s), 0, 0)), pl.BlockSpec((1, 4 * NH, CH), lambda s: (cb(s), 0, 0)),
                   _full((8, LANE)), _full((4 * NH, LANE))) + tuple([_ANY] * nx),
        scratch_shapes=[pltpu.VMEM((2, NH, DH, DH), F32)] + xc.scratch,
        compiler_params=_params(("arbitrary",)),
    )(qkv, qkv, qkv, pab, abt, s_f, t_f, do, qkv, qkv, qkv, pab, abt, s_b, t_b, do, alog_r, dtb_r, alog_c, dtb_c,
      *xc_arrays)


def _post(p, o_f, o_b, x, tgt, w_pa, w_pb, w_out, w_sp, w_spt, b_spb, ln_g, ln_b, g_on, g_post, gate_x, lc):
    lt = p.shape[0]
    l = x.shape[0]
    tm = GC
    nct = lc // tm

    def body(p_ref, of_ref, ob_ref, x_ref, t_ref, wpa, wpb, wout, wsp, wspt, bspb, lng_ref, lnb_ref, gon_ref, gpost_ref, gate_ref,
             dp_ref, do_ref, dy_ref, ya_ref, yb_ref, mg_ref, da_ref, db_ref, dout_ref, dwsp_ref, dbsp_ref, vec_ref):
        i = pl.program_id(0)

        @pl.when(i == 0)
        def _():
            dwsp_ref[...] = jnp.zeros_like(dwsp_ref)
            dbsp_ref[...] = jnp.zeros_like(dbsp_ref)
            vec_ref[...] = jnp.zeros_like(vec_ref)

        @pl.when(i < nct)
        def _():
            dp_ref[...] = jnp.zeros_like(dp_ref)
            do_ref[...] = jnp.zeros_like(do_ref)

        @pl.when(i >= nct)
        def _():
            lng, lnb, gon, gpost, gate = lng_ref[...], lnb_ref[...], gon_ref[...], gpost_ref[...], gate_ref[...]
            zb, ua, va, za, ga, gb = [p_ref[:, j * D:(j + 1) * D] for j in range(6)]
            o = of_ref[...] + ob_ref[...]
            szb, dszb = _silu_g(zb)
            nh_l, r_l = [], []
            for h in range(NH):
                oh = o[:, h * DH:(h + 1) * DH]
                r = lax.rsqrt(jnp.mean(oh * oh, axis=-1, keepdims=True) + EPS)
                nh_l.append(oh * r)
                r_l.append(r)
            nrm_b = jnp.concatenate(nh_l, axis=-1)
            gon_t = jnp.concatenate([gon] * NH, axis=-1)
            y_b = nrm_b * gon_t * szb
            u, du_dua = _gelu_g(ua)
            gv, dgv_dva = _gelu_g(va)
            xc = gv - jnp.mean(gv, axis=-1, keepdims=True)
            rs_ln = lax.rsqrt(jnp.mean(xc * xc, axis=-1, keepdims=True) + EPS)
            vhat = xc * rs_ln
            v = vhat * lng + lnb
            s_sp = jnp.concatenate(
                [_mm(wsp[g], v[:, g * DH:(g + 1) * DH]) + bspb[g] for g in range(NH)], axis=-1)
            sza, dsza = _silu_g(za)
            y_a = u * s_sp * sza
            a_pr = _mm(y_a, wpa[...])
            b_pr = _mm(y_b, wpb[...])
            sga = _sigmoid(ga)
            sgb = _sigmoid(gb)
            merged = sga * a_pr + sgb * b_pr
            out = _mm(merged, wout[...])
            rs_o = lax.rsqrt(jnp.mean(out * out, axis=-1, keepdims=True) + EPS)
            n_o = out * rs_o
            rr = n_o * gpost
            diff = x_ref[...] + gate * rr - t_ref[...]
            vec_ref[5:6, :] += jnp.sum(diff * diff, axis=0, keepdims=True)
            dy = diff * (1.0 / D)
            dy_ref[...] = dy
            vec_ref[0:1, :] += jnp.sum(dy * rr, axis=0, keepdims=True)
            dr = dy * gate
            vec_ref[1:2, :] += jnp.sum(dr * n_o, axis=0, keepdims=True)
            dn_o = dr * gpost
            dout = rs_o * (dn_o - n_o * jnp.mean(dn_o * n_o, axis=-1, keepdims=True))
            dmerged = _mm(dout, wout[...], NT)
            d_a = dmerged * sga
            d_b = dmerged * sgb
            dga = dmerged * a_pr * sga * (1.0 - sga)
            dgb = dmerged * b_pr * sgb * (1.0 - sgb)
            dy_a = _mm(d_a, wpa[...], NT)
            dy_b = _mm(d_b, wpb[...], NT)
            ya_ref[...] = y_a.astype(ya_ref.dtype)
            yb_ref[...] = y_b.astype(yb_ref.dtype)
            mg_ref[...] = merged.astype(mg_ref.dtype)
            da_ref[...] = d_a.astype(da_ref.dtype)
            db_ref[...] = d_b.astype(db_ref.dtype)
            dout_ref[...] = dout.astype(dout_ref.dtype)
            dua = dy_a * s_sp * sza * du_dua
            ds_sp = dy_a * u * sza
            dza = dy_a * u * s_sp * dsza
            dv_l = []
            for g in range(NH):
                ds_g = ds_sp[:, g * DH:(g + 1) * DH]
                dv_l.append(_mm(wspt[g], ds_g))
                dwsp_ref[g] += _mm(ds_g, v[:, g * DH:(g + 1) * DH], NT)
                dbsp_ref[g] += ds_g
            dv = jnp.concatenate(dv_l, axis=-1)
            vec_ref[2:3, :] += jnp.sum(dv * vhat, axis=0, keepdims=True)
            vec_ref[3:4, :] += jnp.sum(dv, axis=0, keepdims=True)
            dvh = dv * lng
            dgv = rs_ln * (dvh - jnp.mean(dvh, axis=-1, keepdims=True) - vhat * jnp.mean(dvh * vhat, axis=-1, keepdims=True))
            dva = dgv * dgv_dva
            dzb = dy_b * nrm_b * gon_t * dszb
            dgon_full = jnp.sum(dy_b * nrm_b * szb, axis=0, keepdims=True)
            dgon = dgon_full[:, 0:DH]
            for h in range(1, NH):
                dgon = dgon + dgon_full[:, h * DH:(h + 1) * DH]
            vec_ref[4:5, 0:DH] += dgon
            dnb = dy_b * gon_t * szb
            do_l = []
            for h in range(NH):
                sl = slice(h * DH, (h + 1) * DH)
                dn_h = dnb[:, sl]
                do_l.append(r_l[h] * (dn_h - nh_l[h] * jnp.mean(dn_h * nh_l[h], axis=-1, keepdims=True)))
            do_ref[...] = jnp.concatenate(do_l, axis=-1)
            for j, val in enumerate((dzb, dua, dva, dza, dga, dgb)):
                dp_ref[:, j * D:(j + 1) * D] = val.astype(dp_ref.dtype)

    xrow = lambda i: (jnp.maximum(i - nct, 0), 0)
    wspec = _full((D, D))
    gspec = _full((NH, GC, GC))
    vspec = _full((1, D))
    bf_out = _sds((l, D), _BF)
    return pl.pallas_call(
        body, name="post", grid=(lt // tm,),
        out_shape=(_sds((lt, NREST), _BF), _sds((lt, D)), _sds((l, D)), bf_out, bf_out, bf_out, bf_out, bf_out, bf_out,
                   _sds((NH, GC, GC)), _sds((NH, GC, GC)), _sds((8, D))),
        in_specs=[pl.BlockSpec((tm, NREST), lambda i: (i, 0)), pl.BlockSpec((tm, D), lambda i: (i, 0)),
                  pl.BlockSpec((tm, D), lambda i: (i, 0)), pl.BlockSpec((tm, D), xrow), pl.BlockSpec((tm, D), xrow),
                  wspec, wspec, wspec, gspec, gspec, gspec, vspec, vspec, _full((1, DH)), vspec, vspec],
        out_specs=(pl.BlockSpec((tm, NREST), lambda i: (i, 0)), pl.BlockSpec((tm, D), lambda i: (i, 0)),
                   pl.BlockSpec((tm, D), xrow), pl.BlockSpec((tm, D), xrow), pl.BlockSpec((tm, D), xrow),
                   pl.BlockSpec((tm, D), xrow), pl.BlockSpec((tm, D), xrow), pl.BlockSpec((tm, D), xrow),
                   pl.BlockSpec((tm, D), xrow), gspec, gspec, _full((8, D))),
        compiler_params=_params(("arbitrary",)),
    )(p, o_f, o_b, x, tgt, w_pa, w_pb, w_out, w_sp, w_spt, b_spb, ln_g, ln_b, g_on, g_post, gate_x)


def _sum_parts(parts, name):
    r = parts.shape[1]
    tr = r if NDEV * r * LANE * 4 <= (8 << 20) else _tile(r, (512, 256, 128, 64, 32, 16, 8))

    def body(p_ref, o_ref):
        acc = p_ref[0]
        for s in range(1, NDEV):
            acc = acc + p_ref[s]
        o_ref[...] = acc

    return pl.pallas_call(
        body, name=name, out_shape=_sds((r, LANE)), grid=(r // tr,),
        in_specs=[pl.BlockSpec((NDEV, tr, LANE), lambda i: (0, i, 0))],
        out_specs=pl.BlockSpec((tr, LANE), lambda i: (i, 0)),
        compiler_params=_params(("parallel",)),
    )(parts)


def _mod_bwd(c_all, c_ctx, dmx, dmc, w_mod_g):
    ws = w_mod_g.shape[2]

    def body(ca_ref, cc_ref, dsh_ref, dmx_ref, dmc_ref, dmc_sh_ref, w_ref, gw_ref, gc_ref, gb_ref):
        sc, _ = _silu_g(ca_ref[...])
        scc, dscc = _silu_g(cc_ref[...])
        dmc_tot = jnp.sum(dmc_ref[...], axis=0, keepdims=True)
        gb_ref[...] = jnp.sum(dmx_ref[...], axis=0, keepdims=True) + dmc_tot
        lhs = jnp.concatenate([sc, jnp.broadcast_to(scc, (8, D))], axis=0)
        rhs = jnp.concatenate([dsh_ref[...], dmc_sh_ref[...]], axis=0)
        gw_ref[...] = _mmh(lhs, rhs, TN)
        acc = jnp.zeros((8, D), F32)
        tot8 = jnp.broadcast_to(dmc_tot, (8, 3 * D))
        for j in range(NDEV):
            acc = acc + _mm(tot8[:, j * ws:(j + 1) * ws], w_ref[j], NT)
        gc_ref[...] = acc[0:1, :] * dscc

    return pl.pallas_call(
        body, name="mod_bwd", out_shape=(_sds((D, ws)), _sds((1, D)), _sds((1, 3 * D))),
        compiler_params=_params(),
    )(c_all, c_ctx, _my_cols(dmx, ws), dmx, dmc, _my_cols(dmc, ws), w_mod_g)


def _my_cols(a, ws):
    me = 4 * lax.axis_index("x") + 2 * lax.axis_index("y") + lax.axis_index("c")
    return lax.dynamic_slice_in_dim(a, me * ws, ws, axis=1)


def _pair_sum(both, name):
    _, n, r, c = both.shape
    tr = _tile(r, (256, 128, 64, 32, 16, 8))

    def body(p_ref, o_ref):
        o_ref[...] = (p_ref[0].astype(F32) + p_ref[1].astype(F32)).astype(o_ref.dtype)

    return pl.pallas_call(
        body, name=name, out_shape=_sds((n, r, c), both.dtype), grid=(n, r // tr),
        in_specs=[pl.BlockSpec((2, 1, tr, c), lambda j, i: (0, j, i, 0))],
        out_specs=pl.BlockSpec((1, tr, c), lambda j, i: (j, i, 0)),
        compiler_params=_params(("parallel", "parallel")),
    )(both)


def _adamw(parts, w, m, v, name, chip_sums_below=None):
    s_, r, c = parts.shape
    tr = _tile(r, (128, 64, 32, 16, 8)) if r * c * 4 > (1 << 20) else r
    c1 = 1.0 / (1.0 - ADAM_B1 ** ADAM_STEP)
    c2 = 1.0 / (1.0 - ADAM_B2 ** ADAM_STEP)

    def body(p_ref, w_ref, m_ref, v_ref, g_ref, d_ref, nm_ref, nv_ref):
        if chip_sums_below is None:
            part = lambda s: p_ref[s].astype(F32)
        else:
            core = lax.axis_index("c")
            me = 4 * lax.axis_index("x") + 2 * lax.axis_index("y") + core
            every = me >= chip_sums_below
            part = lambda s: jnp.where(every | (core == s % 2), p_ref[s].astype(F32), 0.0)
        g = part(0)
        for s in range(1, s_):
            g = g + part(s)
        m_new = ADAM_B1 * m_ref[...] + (1.0 - ADAM_B1) * g
        v_new = ADAM_B2 * v_ref[...] + (1.0 - ADAM_B2) * (g * g)
        g_ref[...] = g
        nm_ref[...] = m_new
        nv_ref[...] = v_new
        d_ref[...] = -ADAM_LR * ((m_new * c1) / (jnp.sqrt(v_new * c2) + ADAM_EPS) + ADAM_WD * w_ref[...])

    blk = pl.BlockSpec((tr, c), lambda i: (i, 0))
    o = _sds((r, c))
    return pl.pallas_call(
        body, name=name, out_shape=(o, o, o, o), grid=(r // tr,),
        in_specs=[pl.BlockSpec((s_, tr, c), lambda i: (0, i, 0)), blk, blk, blk],
        out_specs=(blk, blk, blk, blk),
        compiler_params=_params(("parallel",)),
    )(parts, w, m, v)


def _rows(a):
    flat = a.reshape(-1)
    n = flat.shape[0]
    r = -(-n // (8 * LANE)) * 8
    return jnp.pad(flat, (0, r * LANE - n)).reshape(r, LANE)


def _pack(items):
    parts, layout, at = [], [], 0
    for name, a in items:
        rws = _rows(a.astype(F32))
        layout.append((name, at, rws.shape[0], a.shape))
        parts.append(rws)
        at += rws.shape[0]
    return jnp.concatenate(parts, axis=0), layout


def _unpack(packed, layout):
    out = {}
    for name, at, r, shape in layout:
        n = 1
        for s in shape:
            n *= s
        out[name] = packed[at:at + r].reshape(-1)[:n].reshape(shape)
    return out


def kernel(x, c, ctx, c_ctx, w_mod, b_mod, g_pre, g_post, w_in, w_conv, a_log, dt_bias, g_onorm, gm_ln_g, gm_ln_b, w_sp, b_sp, w_pa, w_pb, w_out, loss_target, m_c_ctx, m_w_mod, m_b_mod, m_g_pre, m_g_post, m_w_in, m_w_conv, m_a_log, m_dt_bias, m_g_onorm, m_gm_ln_g, m_gm_ln_b, m_w_sp, m_b_sp, m_w_pa, m_w_pb, m_w_out, v_c_ctx, v_w_mod, v_b_mod, v_g_pre, v_g_post, v_w_in, v_w_conv, v_a_log, v_dt_bias, v_g_onorm, v_gm_ln_g, v_gm_ln_b, v_w_sp, v_b_sp, v_w_pa, v_w_pb, v_w_out):
    l = x.shape[1]
    lc = ctx.shape[1]
    lt = l + lc
    nch = lt // CH
    me = 4 * lax.axis_index("x") + 2 * lax.axis_index("y") + lax.axis_index("c")
    wsh = w_in.shape[2]
    off_a = 3 * D
    n_ab = 4 * NH
    jb = off_a // wsh
    o1 = off_a - jb * wsh
    o2 = o1 + n_ab
    assert o2 <= wsh and NREST == (NDEV - jb) * wsh - o2
    split = jb + 1

    wp = -(-wsh // LANE) * LANE
    widen = lambda a: jnp.pad(a, [(0, 0)] * (a.ndim - 1) + [(0, wp - wsh)])
    w_in_bf = widen(w_in[0].astype(_BF))
    wg_lo, wg_mod, wg_conv, c_all = _exchange(
        [w_in_bf, w_mod[0].astype(_BF), w_conv[0], c], ["gather_lo", "gather", "gather", "gather"],
        "gather_first", split)
    w_qkv = jnp.concatenate([wg_lo[j][:, :wsh] for j in range(jb)] + [wg_lo[jb][:, :o1]], axis=1)
    w_ab = jnp.pad(wg_lo[jb][:, o1:o2], ((0, 0), (0, LANE - n_ab)))
    wconv_full = jnp.moveaxis(wg_conv, 0, 1).reshape(3, 3 * D)
    c_all = c_all.reshape(NDEV, D)

    cc = jnp.concatenate([c, c_ctx.reshape(1, D), jnp.zeros((6, D), F32)], axis=0)
    mods = _modulation(cc, wg_mod, b_mod)
    xa = jnp.concatenate([ctx[0], x[0]], axis=0)
    h = _prenorm(xa, mods, g_pre, lc)
    p_qkv = _matmul_nn(h, w_qkv, "in_proj_qkv")
    pab = _matmul_nn(h, w_ab, "in_proj_ab")
    abt = jnp.swapaxes(pab[:, :n_ab].reshape(nch, CH, n_ab), 1, 2)
    alog16, dtb16 = a_log.reshape(1, 2 * NH), dt_bias.reshape(1, 2 * NH)
    alog_r = jnp.pad(alog16, ((0, 0), (0, LANE - 2 * NH)))
    dtb_r = jnp.pad(dtb16, ((0, 0), (0, LANE - 2 * NH)))
    alog_c = jnp.pad(alog16.reshape(2 * NH, 1), ((0, 2 * NH), (0, 0)))
    dtb_c = jnp.pad(dtb16.reshape(2 * NH, 1), ((0, 2 * NH), (0, 0)))
    qkv = _qkv_fwd(p_qkv, wconv_full, lc)
    late = [w_in_bf, w_pa[0].astype(_BF), w_pb[0].astype(_BF), w_out[0].astype(_BF)]
    xc_late = _Exchange(zip(late, ["gather_hi", "gather", "gather", "gather"]), split)
    o_f, o_b, s_f, s_b, t_f, t_b, wg_hi, wg_pa, wg_pb, wg_out = _gdn_fwd(
        qkv, pab, abt, alog_r, dtb_r, alog_c, dtb_c, lc, xc_late, late)
    w_rest = jnp.concatenate([wg_lo[jb][:, o2:wsh]] + [wg_hi[j][:, :wsh] for j in range(split, NDEV)], axis=1)
    wf_pa, wf_pb, wf_out = wg_pa.reshape(D, D), wg_pb.reshape(D, D), wg_out.reshape(D, D)
    p_rest = _matmul_nn(h, w_rest, "in_proj_rest")

    w_spt = jnp.swapaxes(w_sp[0], 1, 2)
    b_spb = jnp.broadcast_to(b_sp[0][:, :, None], (NH, GC, GC))
    gate_x = mods[0:1, 2 * D:]
    dp_rest, do, dy, ya, yb, mg, d_a, d_b, dout, dwsp, dbsp_l, pvec = _post(
        p_rest, o_f, o_b, x[0], loss_target[0], wf_pa, wf_pb, wf_out, w_sp[0], w_spt, b_spb, gm_ln_g, gm_ln_b,
        g_onorm, g_post, gate_x, lc)

    dw_rest = _matmul_tn(h, dp_rest, "dw_in_rest")
    o3 = wsh - o2
    chunks_hi = widen(jnp.moveaxis(dw_rest[:, o3:].reshape(D, NDEV - split, wsh), 1, 0))
    dw_pa = _matmul_tn(ya, d_a, "dw_pa").reshape(NDEV, D // NDEV, D)
    dw_pb = _matmul_tn(yb, d_b, "dw_pb").reshape(NDEV, D // NDEV, D)
    dw_out = _matmul_tn(mg, dout, "dw_out").reshape(NDEV, D // NDEV, D)
    small_a, lay_a = _pack([
        ("g_post", pvec[1]), ("g_onorm", pvec[4, :DH]), ("gm_ln_g", pvec[2]), ("gm_ln_b", pvec[3]), ("w_sp", dwsp),
        ("b_sp", jnp.sum(dbsp_l, axis=-1)), ("loss", pvec[5]), ("dgate", pvec[0])])
    early = [chunks_hi, dw_pa, dw_pb, dw_out, small_a]
    xc_early = _Exchange(zip(early, ["scatter_hi", "scatter", "scatter", "scatter", "gather"]), split)

    dqkv_f, dqkv_b, dcol_f, dcol_b, drow_f, drow_b, gvec_c, gvec_r, r_in, r_pa, r_pb, r_out, small_a_all = _gdn_bwd(
        qkv, pab, abt, alog_r, dtb_r, alog_c, dtb_c, s_f, s_b, t_f, t_b, do, lc, xc_early, early)
    dp_qkv, dwconv = _qkv_bwd(p_qkv, wconv_full, dqkv_f, dqkv_b, lc)
    drow = jnp.swapaxes(drow_f + drow_b, 1, 2).reshape(lt, n_ab)
    dpab = (dcol_f + dcol_b + jnp.pad(drow, ((0, 0), (0, LANE - n_ab)))).astype(_BF)

    dw_qkv = _matmul_tn(h, dp_qkv, "dw_in_qkv")
    dw_ab = _matmul_tn(h, dpab, "dw_in_ab")
    dw_lo = jnp.concatenate([dw_qkv, dw_ab[:, :n_ab], dw_rest[:, :o3]], axis=1)
    chunks_lo = widen(jnp.moveaxis(dw_lo.reshape(D, split, wsh), 1, 0))
    (both,) = _exchange([chunks_lo], ["sibling"], "pair_swap")
    chip_lo = _pair_sum(both, "pair_sum")
    xc_last = _Exchange([(chip_lo, "scatter_par_lo")], split)
    dh, r_in = _dh_matmul(dp_rest, dp_qkv, dpab, w_rest, w_qkv, w_ab, xc_last, [chip_lo], {0: r_in})
    grad_x, nvec = _prenorm_bwd(xa, dh, dy, mods, g_pre, lc)

    dalog = gvec_c[0, :2 * NH] + gvec_r[:2 * NH, 0]
    ddtb = gvec_c[1, :2 * NH] + gvec_r[:2 * NH, 1]
    small_b, lay_b = _pack([
        ("g_pre", nvec[4]), ("a_log", dalog), ("dt_bias", ddtb), ("w_conv", dwconv),
        ("dshift", nvec[0]), ("dscale", nvec[1]), ("dshift_c", nvec[2]), ("dscale_c", nvec[3])])
    (small_b_all,) = _exchange([small_b], ["gather"], "gather_small")
    tot = _unpack(_sum_parts(small_a_all, "sum_small_a"), lay_a)
    tot.update(_unpack(_sum_parts(small_b_all, "sum_small_b"), lay_b))

    def per_device(packed_all, layout, name):
        at, r = [(a_, r_) for nm, a_, r_, _ in layout if nm == name][0]
        return packed_all[:, at:at + r].reshape(NDEV, -1)

    dmx_all = jnp.concatenate([per_device(small_b_all, lay_b, "dshift"), per_device(small_b_all, lay_b, "dscale"),
                               per_device(small_a_all, lay_a, "dgate")], axis=1)
    dmc_all = jnp.concatenate([per_device(small_b_all, lay_b, "dshift_c"), per_device(small_b_all, lay_b, "dscale_c"),
                               jnp.zeros((NDEV, D), F32)], axis=1)
    g_wmod, g_cctx, g_bmod = _mod_bwd(c_all, c_ctx.reshape(1, D), dmx_all, dmc_all, wg_mod)
    loss = 0.5 / D * jnp.sum(tot["loss"])
    ws_conv = w_conv.shape[2]
    g_wconv = lax.dynamic_slice_in_dim(tot["w_conv"], me * ws_conv, ws_conv, axis=1)

    small_names = ["c_ctx", "b_mod", "g_pre", "g_post", "a_log", "dt_bias", "g_onorm", "gm_ln_g", "gm_ln_b",
                   "w_sp", "b_sp", "w_conv"]
    wts = dict(c_ctx=c_ctx, b_mod=b_mod, g_pre=g_pre, g_post=g_post, a_log=a_log, dt_bias=dt_bias, g_onorm=g_onorm,
               gm_ln_g=gm_ln_g, gm_ln_b=gm_ln_b, w_sp=w_sp, b_sp=b_sp, w_conv=w_conv)
    ms = dict(c_ctx=m_c_ctx, b_mod=m_b_mod, g_pre=m_g_pre, g_post=m_g_post, a_log=m_a_log, dt_bias=m_dt_bias,
              g_onorm=m_g_onorm, gm_ln_g=m_gm_ln_g, gm_ln_b=m_gm_ln_b, w_sp=m_w_sp, b_sp=m_b_sp, w_conv=m_w_conv)
    vs = dict(c_ctx=v_c_ctx, b_mod=v_b_mod, g_pre=v_g_pre, g_post=v_g_post, a_log=v_a_log, dt_bias=v_dt_bias,
              g_onorm=v_g_onorm, gm_ln_g=v_gm_ln_g, gm_ln_b=v_gm_ln_b, w_sp=v_w_sp, b_sp=v_b_sp, w_conv=v_w_conv)
    gs = dict(tot)
    gs.update(c_ctx=g_cctx, b_mod=g_bmod, w_conv=g_wconv)
    gpk, play = _pack([(nm, gs[nm].reshape(wts[nm].shape)) for nm in small_names])
    wpk, _ = _pack([(nm, wts[nm]) for nm in small_names])
    mpk, _ = _pack([(nm, ms[nm]) for nm in small_names])
    vpk, _ = _pack([(nm, vs[nm]) for nm in small_names])
    res_small = [_unpack(a, play) for a in _adamw(gpk[None], wpk, mpk, vpk, "adamw_small")]
    res_big = {
        "w_mod": _adamw(g_wmod[None], w_mod[0], m_w_mod[0], v_w_mod[0], "adamw_w_mod"),
        "w_in": [a[:, :wsh] for a in _adamw(r_in, widen(w_in[0]), widen(m_w_in[0]), widen(v_w_in[0]), "adamw_w_in",
                                            chip_sums_below=split)],
        "w_pa": _adamw(r_pa, w_pa[0], m_w_pa[0], v_w_pa[0], "adamw_w_pa"),
        "w_pb": _adamw(r_pb, w_pb[0], m_w_pb[0], v_w_pb[0], "adamw_w_pb"),
        "w_out": _adamw(r_out, w_out[0], m_w_out[0], v_w_out[0], "adamw_w_out"),
    }
    order = ["c_ctx", "w_mod", "b_mod", "g_pre", "g_post", "w_in", "w_conv", "a_log", "dt_bias", "g_onorm",
             "gm_ln_g", "gm_ln_b", "w_sp", "b_sp", "w_pa", "w_pb", "w_out"]
    outs = [loss, grad_x[None]]
    for k in range(4):
        for nm in order:
            if nm in res_big:
                outs.append(res_big[nm][k][None])
            else:
                outs.append(res_small[k][nm])
    return tuple(outs)
```

```python
import functools

import jax
import jax.numpy as jnp
from jax import lax
from jax.experimental import pallas as pl
from jax.experimental.pallas import tpu as pltpu

F32 = jnp.float32
_BF = jnp.bfloat16
_HI = lax.Precision.HIGHEST
D = 1024
NH = 8
DH = 128
CH = 64
GC = 128
NREST = 6 * D
NMAIN = NREST + 3 * D
EPS = 1e-6
LANE = 128
NDEV = 8
VMEM_LIMIT = 56 * 1024 * 1024
MESH = pl.DeviceIdType.MESH

ADAM_LR, ADAM_B1, ADAM_B2, ADAM_EPS, ADAM_WD, ADAM_STEP = 0.001, 0.9, 0.999, 1e-08, 0.01, 10

NN = ((1,), (0,))
NT = ((1,), (1,))
TN = ((0,), (0,))


def _dot(a, b, dims=NN, prec=None):
    return lax.dot_general(a, b, (dims, ((), ())), precision=prec, preferred_element_type=F32)


def _mm(a, b, dims=NN):
    return _dot(a.astype(_BF), b.astype(_BF), dims)


def _mmh(a, b, dims=NN):
    return _dot(a.astype(F32), b.astype(F32), dims, _HI)


def _split(a):
    hi = a.astype(_BF)
    return hi, (a - hi.astype(F32)).astype(_BF)


def _mm3(a, b, dims=NN):
    ah, al = _split(a)
    bh, bl = _split(b)
    return _dot(ah, bh, dims) + (_dot(ah, bl, dims) + _dot(al, bh, dims))


def _sigmoid(x):
    return 1.0 / (1.0 + jnp.exp(-x))


def _silu_g(x):
    s = _sigmoid(x)
    return x * s, s * (1.0 + x * (1.0 - s))


def _gelu_g(x):
    c = 0.7978845608028654
    t = jnp.tanh(c * (x + 0.044715 * (x * x * x)))
    cdf = 0.5 * (1.0 + t)
    return x * cdf, cdf + 0.5 * x * (1.0 - t * t) * c * (1.0 + 3 * 0.044715 * x * x)


def _softplus(x):
    return jnp.maximum(x, 0.0) + jnp.log(1.0 + jnp.exp(-jnp.abs(x)))


def _params(sem=None):
    return pltpu.CompilerParams(dimension_semantics=sem, vmem_limit_bytes=VMEM_LIMIT)


def _tile(n, pref):
    for t in pref:
        if n % t == 0:
            return t
    return n


def _full(shape):
    nd = len(shape)
    return pl.BlockSpec(shape, lambda *_: (0,) * nd)


def _sds(shape, dt=F32):
    return jax.ShapeDtypeStruct(shape, dt)


MAX_PIECES = 12
PIECE_BYTES = 256 * 1024


def _piece_slices(shape, itemsize):
    total = itemsize
    for d in shape:
        total *= d
    want = min(MAX_PIECES, total // PIECE_BYTES)
    lead = shape[0] if len(shape) >= 3 else 1
    rows = shape[-2] if len(shape) >= 2 else 1
    if want < 2 or lead > want:
        return [()]
    m = max([n for n in (8, 4, 2, 1) if n * lead <= want and rows % (16 * n) == 0], default=1)
    if m * lead < 2:
        return [()]
    rs = rows // m
    mid = (slice(None),) * max(len(shape) - 3, 0)
    if len(shape) >= 3:
        return [(i,) + mid + (pl.ds(j * rs, rs),) for i in range(lead) for j in range(m)]
    return [(pl.ds(j * rs, rs),) for j in range(m)]


class _Pieces:
    def __init__(self, copies):
        self.copies = copies

    def start(self):
        for cp in self.copies:
            cp.start()

    def wait_send(self):
        for cp in self.copies:
            cp.wait_send()

    def wait_recv(self):
        for cp in self.copies:
            cp.wait_recv()


class _Exchange:
    def __init__(self, specs, split):
        self.specs = list(specs)
        self.split = split
        self.n = len(self.specs)
        def out(a, k):
            if k == "sibling":
                return (2,) + tuple(a.shape)
            return (NDEV,) + (tuple(a.shape) if k.startswith("gather") else tuple(a.shape[1:]))

        self.out_shape = tuple(_sds(out(a, k), a.dtype) for a, k in self.specs)
        self.pieces = [_piece_slices(o.shape[1:], jnp.dtype(o.dtype).itemsize) for o in self.out_shape]
        self.sem_base = [(NDEV - 1) * sum(len(p) for p in self.pieces[:a]) for a in range(self.n + 1)]
        self.scratch = [pltpu.SemaphoreType.DMA((self.sem_base[-1],)), pltpu.SemaphoreType.DMA((self.sem_base[-1],)),
                        pltpu.SemaphoreType.DMA((self.n,))]

    def _remote(self, sems, a, k, src, dst, to):
        send_sems, recv_sems, _ = sems
        base = self.sem_base[a] + k * len(self.pieces[a])
        return _Pieces([
            pltpu.make_async_remote_copy(
                src_ref=src.at[sl] if sl else src, dst_ref=dst.at[sl] if sl else dst, send_sem=send_sems.at[base + p],
                recv_sem=recv_sems.at[base + p], device_id=to, device_id_type=MESH)
            for p, sl in enumerate(self.pieces[a])])

    def _ok(self, kind, idx):
        if kind.endswith("_lo"):
            return idx < self.split
        if kind.endswith("_hi"):
            return idx >= self.split
        return True

    def _phases(self, ins, outs, sems):
        loc_sems = sems[2]
        x, y, c = lax.axis_index("x"), lax.axis_index("y"), lax.axis_index("c")
        me = 4 * x + 2 * y + c
        sib = (x, y, 1 - c)
        sib_idx = 4 * x + 2 * y + (1 - c)
        chips = [(1 - x, y), (x, 1 - y), (1 - x, 1 - y)]
        starts, forwards, waits = [], [], []
        for a, (_, kind) in enumerate(self.specs):
            ok = functools.partial(self._ok, kind)
            if kind.startswith("gather"):
                def copy(k, block, to, src=None, a=a):
                    rows = outs[a].at[block]
                    return self._remote(sems, a, k, rows if src is None else src, rows, to)

                loc = pltpu.make_async_copy(ins[a], outs[a].at[me], loc_sems.at[a])
                first = [copy(0, me, sib, ins[a])] + [copy(1 + j, me, (*chip, c), ins[a]) for j, chip in enumerate(chips)]
                starts += [(ok(me), loc.start)] + [(ok(me), cp.start) for cp in first]
                waits += [(ok(me), loc.wait)] + [(ok(me), cp.wait_send) for cp in first]
                for j, chip in enumerate(chips):
                    origin = 4 * chip[0] + 2 * chip[1] + c
                    passed = copy(4 + j, origin, sib)
                    forwards += [(ok(origin), copy(1 + j, origin, sib).wait_recv), (ok(origin), passed.start)]
                    waits.append((ok(origin), passed.wait_send))
                    other = 4 * chip[0] + 2 * chip[1] + (1 - c)
                    waits.append((ok(other), copy(4 + j, other, sib).wait_recv))
                waits.append((ok(sib_idx), copy(0, sib_idx, sib).wait_recv))
            elif kind == "sibling":
                loc = pltpu.make_async_copy(ins[a], outs[a].at[c], loc_sems.at[a])
                swap = self._remote(sems, a, 0, ins[a], outs[a].at[c], sib)
                arrive = self._remote(sems, a, 0, ins[a], outs[a].at[1 - c], sib)
                starts += [(True, loc.start), (True, swap.start)]
                waits += [(True, loc.wait), (True, swap.wait_send), (True, arrive.wait_recv)]
            else:
                base = self.split if kind.endswith("_hi") else 0
                same_core_only = kind.endswith("_par_lo")

                def src(idx, a=a, base=base):
                    return ins[a].at[jnp.clip(idx - base, 0, ins[a].shape[0] - 1)]

                loc = pltpu.make_async_copy(src(me), outs[a].at[me], loc_sems.at[a])
                starts.append((ok(me), loc.start))
                waits.append((ok(me), loc.wait))
                for k in range(1, NDEV):
                    if same_core_only and k & 1:
                        continue
                    px = 1 - x if (k >> 2) & 1 else x
                    py = 1 - y if (k >> 1) & 1 else y
                    pc = 1 - c if k & 1 else c
                    pidx = 4 * px + 2 * py + pc
                    send = self._remote(sems, a, k - 1, src(pidx), outs[a].at[me], (px, py, pc))
                    arrive = self._remote(sems, a, k - 1, src(pidx), outs[a].at[pidx], (px, py, pc))
                    starts.append((ok(pidx), send.start))
                    waits += [(ok(pidx), send.wait_send), (ok(me), arrive.wait_recv)]
        return starts, forwards, waits

    @staticmethod
    def _run(actions):
        for cond, fn in actions:
            if cond is True:
                fn()
            else:
                pl.when(cond)(fn)

    def start(self, ins, outs, sems):
        self._run(self._phases(ins, outs, sems)[0])

    def forward(self, ins, outs, sems):
        self._run(self._phases(ins, outs, sems)[1])

    def wait(self, ins, outs, sems):
        self._run(self._phases(ins, outs, sems)[2])


_ANY = pl.BlockSpec(memory_space=pl.ANY)


def _exchange(arrays, kinds, name, split=0, into=None):
    xc = _Exchange(zip(arrays, kinds), split)
    n = xc.n
    into = into or {}
    ni = len(into)

    def body(*refs):
        ins, outs, sems = refs[:n], refs[n + ni:2 * n + ni], refs[2 * n + ni:]
        xc.start(ins, outs, sems)
        xc.forward(ins, outs, sems)
        xc.wait(ins, outs, sems)

    return pl.pallas_call(
        body, name=name, out_shape=xc.out_shape, in_specs=[_ANY] * (n + ni), out_specs=tuple([_ANY] * n),
        scratch_shapes=xc.scratch, input_output_aliases={n + t: a for t, a in enumerate(into)},
    )(*arrays, *into.values())


def _matmul_nn(a, b, name):
    m, kk = a.shape
    n = b.shape[1]
    tm = _tile(m, (1088, 1024, 640, 512, 256, 128))
    tn = _tile(n, (512, 256, 128))

    def body(a_ref, b_ref, o_ref):
        o_ref[...] = _mm(a_ref[...], b_ref[...])

    return pl.pallas_call(
        body, name=name, out_shape=_sds((m, n)), grid=(n // tn, m // tm),
        in_specs=[pl.BlockSpec((tm, kk), lambda j, i: (i, 0)), pl.BlockSpec((kk, tn), lambda j, i: (0, j))],
        out_specs=pl.BlockSpec((tm, tn), lambda j, i: (i, j)),
        compiler_params=_params(("parallel", "parallel")),
    )(a, b)


def _matmul_tn(a, b, name):
    kk, m = a.shape
    n = b.shape[1]
    tk = _tile(kk, (1088, 1024, 640, 512, 256, 128))
    tn = _tile(n, (1024, 512, 256, 128))
    nk = kk // tk

    def body(a_ref, b_ref, o_ref, acc_ref):
        k = pl.program_id(1)

        @pl.when(k == 0)
        def _():
            acc_ref[...] = jnp.zeros_like(acc_ref)

        acc_ref[...] += _mm(a_ref[...], b_ref[...], TN)

        @pl.when(k == nk - 1)
        def _():
            o_ref[...] = acc_ref[...].astype(o_ref.dtype)

    return pl.pallas_call(
        body, name=name, out_shape=_sds((m, n), _BF), grid=(n // tn, nk),
        in_specs=[pl.BlockSpec((tk, m), lambda j, k: (k, 0)), pl.BlockSpec((tk, tn), lambda j, k: (k, j))],
        out_specs=pl.BlockSpec((m, tn), lambda j, k: (0, j)),
        scratch_shapes=[pltpu.VMEM((m, tn), F32)],
        compiler_params=_params(("parallel", "arbitrary")),
    )(a, b)


def _dh_matmul(dp_rest, dp_qkv, dpab, w_rest, w_qkv, w_ab, xc, xc_arrays, xc_into):
    lt = dp_rest.shape[0]
    tm = _tile(lt, (1088, 1024, 640, 512, 256, 128))
    nr, nq = dp_rest.shape[1] // D, dp_qkv.shape[1] // D
    nx, ni = xc.n, len(xc_into)
    ni_steps = lt // tm

    def body(*refs):
        dr_ref, dq_ref, ab_ref, wr_ref, wq_ref, wab_ref = refs[:6]
        x_in = refs[6:6 + nx]
        o_ref = refs[6 + nx + ni]
        x_out = refs[7 + nx + ni:7 + 2 * nx + ni]
        sems = refs[7 + 2 * nx + ni:]
        i = pl.program_id(0)
        k = pl.program_id(1)

        @pl.when((i == 0) & (k == 0))
        def _():
            xc.start(x_in, x_out, sems)

        @pl.when(k == 0)
        def _():
            o_ref[...] = _mm(ab_ref[...], wab_ref[...], NT)

        @pl.when(k < nr)
        def _():
            o_ref[...] += _mm(dr_ref[...], wr_ref[...], NT)

        @pl.when(k >= nr)
        def _():
            o_ref[...] += _mm(dq_ref[...], wq_ref[...], NT)

        @pl.when((i == ni_steps - 1) & (k == nr + nq - 1))
        def _():
            xc.wait(x_in, x_out, sems)

    rk = lambda k: jnp.minimum(k, nr - 1)
    qk = lambda k: jnp.maximum(k - nr, 0)
    return pl.pallas_call(
        body, name="dh_matmul", out_shape=(_sds((lt, D)),) + xc.out_shape, grid=(lt // tm, nr + nq),
        in_specs=[pl.BlockSpec((tm, D), lambda i, k: (i, rk(k))), pl.BlockSpec((tm, D), lambda i, k: (i, qk(k))),
                  pl.BlockSpec((tm, LANE), lambda i, k: (i, 0)),
                  pl.BlockSpec((D, D), lambda i, k: (0, rk(k))), pl.BlockSpec((D, D), lambda i, k: (0, qk(k))),
                  _full((D, LANE))] + [_ANY] * (nx + ni),
        out_specs=(pl.BlockSpec((tm, D), lambda i, k: (i, 0)),) + tuple([_ANY] * nx),
        scratch_shapes=xc.scratch, input_output_aliases={6 + nx + t: 1 + a for t, a in enumerate(xc_into)},
        compiler_params=_params(("arbitrary", "arbitrary")),
    )(dp_rest, dp_qkv, dpab, w_rest, w_qkv, w_ab, *xc_arrays, *xc_into.values())


def _modulation(cc, w_mod_g, b_mod):
    ws = w_mod_g.shape[2]

    def body(c_ref, w_ref, b_ref, o_ref):
        s, _ = _silu_g(c_ref[...])
        o_ref[...] = _mm(s, w_ref[0]) + b_ref[...]

    return pl.pallas_call(
        body, name="modulation", out_shape=_sds((8, 3 * D)), grid=(NDEV,),
        in_specs=[_full((8, D)), pl.BlockSpec((1, D, ws), lambda j: (j, 0, 0)), pl.BlockSpec((1, ws), lambda j: (0, j))],
        out_specs=pl.BlockSpec((8, ws), lambda j: (0, j)),
        compiler_params=_params(("parallel",)),
    )(cc, w_mod_g, b_mod)


def _prenorm(xa, mods, g_pre, lc):
    lt = xa.shape[0]
    tm = _tile(lc, (256, 128))
    nct = lc // tm

    def body(x_ref, m_ref, g_ref, o_ref):
        x = x_ref[...]
        is_ctx = pl.program_id(0) < nct
        shift = jnp.where(is_ctx, m_ref[1:2, 0:D], m_ref[0:1, 0:D])
        scale = jnp.where(is_ctx, m_ref[1:2, D:2 * D], m_ref[0:1, D:2 * D])
        r = lax.rsqrt(jnp.mean(x * x, axis=-1, keepdims=True) + EPS)
        o_ref[...] = ((x * r * g_ref[...]) * (1.0 + scale) + shift).astype(o_ref.dtype)

    return pl.pallas_call(
        body, name="prenorm", out_shape=_sds((lt, D), _BF), grid=(lt // tm,),
        in_specs=[pl.BlockSpec((tm, D), lambda i: (i, 0)), _full((8, 3 * D)), _full((1, D))],
        out_specs=pl.BlockSpec((tm, D), lambda i: (i, 0)),
        compiler_params=_params(("parallel",)),
    )(xa, mods, g_pre)


def _prenorm_bwd(xa, dh, dy, mods, g_pre, lc):
    lt = xa.shape[0]
    tm = _tile(lc, (256, 128))
    nct = lc // tm
    nl = (lt - lc) // tm

    def body(x_ref, dh_ref, dy_ref, m_ref, g_ref, gx_ref, vec_ref):
        i = pl.program_id(0)

        @pl.when(i == 0)
        def _():
            vec_ref[...] = jnp.zeros_like(vec_ref)

        x = x_ref[...]
        dh = dh_ref[...]
        g = g_ref[...]
        is_ctx = i < nct
        scale = jnp.where(is_ctx, m_ref[1:2, D:2 * D], m_ref[0:1, D:2 * D])
        r = lax.rsqrt(jnp.mean(x * x, axis=-1, keepdims=True) + EPS)
        n = x * r
        hn = n * g
        dsh = jnp.sum(dh, axis=0, keepdims=True)
        dsc = jnp.sum(dh * hn, axis=0, keepdims=True)
        dhn = dh * (1.0 + scale)
        vec_ref[4:5, :] += jnp.sum(dhn * n, axis=0, keepdims=True)
        dn = dhn * g
        dx = r * (dn - n * jnp.mean(dn * n, axis=-1, keepdims=True))

        @pl.when(is_ctx)
        def _():
            vec_ref[2:3, :] += dsh
            vec_ref[3:4, :] += dsc

        @pl.when(jnp.logical_not(is_ctx))
        def _():
            vec_ref[0:1, :] += dsh
            vec_ref[1:2, :] += dsc
            gx_ref[...] = dy_ref[...] + dx

    xrow = lambda i: (jnp.maximum(i - nct, 0), 0)
    return pl.pallas_call(
        body, name="prenorm_bwd", out_shape=(_sds((nl * tm, D)), _sds((8, D))), grid=(lt // tm,),
        in_specs=[pl.BlockSpec((tm, D), lambda i: (i, 0)), pl.BlockSpec((tm, D), lambda i: (i, 0)),
                  pl.BlockSpec((tm, D), xrow), _full((8, 3 * D)), _full((1, D))],
        out_specs=(pl.BlockSpec((tm, D), xrow), _full((8, D))),
        compiler_params=_params(("arbitrary",)),
    )(xa, dh, dy, mods, g_pre)


def _conv_parts(x, w, lc):
    lt = x.shape[0]
    row = lax.broadcasted_iota(jnp.int32, x.shape, 0)
    first = (row == 0) | (row == lc)
    last = (row == lc - 1) | (row == lt - 1)
    xp = jnp.where(first, 0.0, pltpu.roll(x, 1, 0))
    xn = jnp.where(last, 0.0, pltpu.roll(x, lt - 1, 0))
    y = w[0:1, :] * xp + w[1:2, :] * x + w[2:3, :] * xn
    return xp, xn, y, first, last


def _qkv_fwd(p, w_conv, lc):
    lt = p.shape[0]

    def body(p_ref, w_ref, o_ref):
        _, _, y, _, _ = _conv_parts(p_ref[...], w_ref[...], lc)
        s, _ = _silu_g(y)
        rs = lax.rsqrt(jnp.sum(s * s, axis=-1, keepdims=True) + EPS)
        o_ref[...] = s * jnp.where(pl.program_id(0) < 2 * NH, rs, 1.0)

    return pl.pallas_call(
        body, name="qkv_fwd", out_shape=_sds((lt, 3 * D)), grid=(3 * NH,),
        in_specs=[pl.BlockSpec((lt, DH), lambda j: (0, j)), pl.BlockSpec((3, DH), lambda j: (0, j))],
        out_specs=pl.BlockSpec((lt, DH), lambda j: (0, j)),
        compiler_params=_params(("parallel",)),
    )(p, w_conv)


def _qkv_bwd(p, w_conv, dqkv_f, dqkv_b, lc):
    lt = p.shape[0]

    def body(p_ref, w_ref, df_ref, db_ref, dp_ref, dw_ref):
        w = w_ref[...]
        xp, xn, y, first, last = _conv_parts(p_ref[...], w, lc)
        s, ds_dy = _silu_g(y)
        dn = df_ref[...] + db_ref[...]
        rs = lax.rsqrt(jnp.sum(s * s, axis=-1, keepdims=True) + EPS)
        nrm = s * rs
        ds_n = rs * (dn - nrm * jnp.sum(dn * nrm, axis=-1, keepdims=True))
        ds = jnp.where(pl.program_id(0) < 2 * NH, ds_n, dn)
        dy = ds * ds_dy
        dw_ref[0:1, :] = jnp.sum(dy * xp, axis=0, keepdims=True)
        dw_ref[1:2, :] = jnp.sum(dy * p_ref[...], axis=0, keepdims=True)
        dw_ref[2:3, :] = jnp.sum(dy * xn, axis=0, keepdims=True)
        dyn = jnp.where(last, 0.0, pltpu.roll(dy, lt - 1, 0))
        dyp = jnp.where(first, 0.0, pltpu.roll(dy, 1, 0))
        dp_ref[...] = (w[1:2, :] * dy + w[0:1, :] * dyn + w[2:3, :] * dyp).astype(dp_ref.dtype)

    return pl.pallas_call(
        body, name="qkv_bwd", out_shape=(_sds((lt, 3 * D), _BF), _sds((3, 3 * D))), grid=(3 * NH,),
        in_specs=[pl.BlockSpec((lt, DH), lambda j: (0, j)), pl.BlockSpec((3, DH), lambda j: (0, j)),
                  pl.BlockSpec((lt, DH), lambda j: (0, j)), pl.BlockSpec((lt, DH), lambda j: (0, j))],
        out_specs=(pl.BlockSpec((lt, DH), lambda j: (0, j)), pl.BlockSpec((3, DH), lambda j: (0, j))),
        compiler_params=_params(("parallel",)),
    )(p, w_conv, dqkv_f, dqkv_b)


def _masks(d):
    ri = lax.broadcasted_iota(jnp.int32, (CH, CH), 0)
    ci = lax.broadcasted_iota(jnp.int32, (CH, CH), 1)
    incl = (ri >= ci) if d == 0 else (ri <= ci)
    strict = (ri > ci) if d == 0 else (ri < ci)
    incl_t = (ri <= ci) if d == 0 else (ri >= ci)
    return incl, strict, incl_t, ri == ci


def _decays(d, ab, abt, alog_r, dtb_r, alog_c, dtb_c, incl, incl_t):
    g_full = -jnp.exp(alog_r) * _softplus(ab + dtb_r)
    beta_full = _sigmoid(ab)
    gc_full = _mmh(incl.astype(F32), g_full)
    gl_full = jnp.sum(g_full, axis=0, keepdims=True)
    gt_full = -jnp.exp(alog_c) * _softplus(abt + dtb_c)
    gct = _mmh(gt_full, incl_t.astype(F32))
    return g_full, beta_full, gc_full, gl_full, gt_full, gct


def _lane_onehot(idx, n=LANE):
    return (lax.broadcasted_iota(jnp.int32, (1, n), 1) == idx).astype(F32)


def _head_scalars(d, h, beta_full, gc_full, gl_full, gct):
    idx = d * NH + h
    oh = _lane_onehot(idx)
    gcol = jnp.sum(gc_full * oh, axis=-1, keepdims=True)
    bcol = jnp.sum(beta_full * _lane_onehot(2 * NH + idx), axis=-1, keepdims=True)
    gl = jnp.sum(gl_full * oh, axis=-1, keepdims=True)
    grow = gct[idx:idx + 1, :]
    return gcol, grow, bcol, gl


def _lockstep(gens):
    live = list(gens)
    while live:
        nxt = []
        for g in live:
            try:
                next(g)
                nxt.append(g)
            except StopIteration:
                pass
        live = nxt


def _chunk_local(qh, kh, vh, gcol, grow, bcol, gl, incl, strict):
    decay = jnp.where(incl, jnp.exp(gcol - grow), 0.0)
    kb = kh * bcol
    a = jnp.where(strict, _mm(kb, kh, NT) * decay, 0.0)
    egc = jnp.exp(gcol)
    rhs_u = vh * bcol
    rhs_w = kb * egc
    qs = qh * (DH ** -0.5)
    attn = jnp.where(incl, _mm(qs, kh, NT) * decay, 0.0)
    etail = jnp.exp(gl - gcol)
    return decay, kb, a, egc, rhs_u, rhs_w, qs, attn, etail


def _scan_specs(lt, lc, bwd_pass):
    nch = lt // CH
    ncc = lc // CH
    if not bwd_pass:
        cf = lambda s: s
        cb = lambda s: jnp.where(s < ncc, ncc - 1 - s, nch + ncc - 1 - s)
    else:
        cf = lambda s: nch - 1 - s
        cb = lambda s: jnp.where(s < nch - ncc, ncc + s, s - (nch - ncc))
    return nch, cf, cb


def _gdn_fwd(qkv, pab, abt, alog_r, dtb_r, alog_c, dtb_c, lc, xc, xc_arrays):
    lt = qkv.shape[0]
    nch, cf, cb = _scan_specs(lt, lc, False)
    nx = xc.n

    def body(*refs):
        qf, kf, vf, abf, abtf, qb, kb_, vb, abb, abtb, ar, dr, ac, dc = refs[:14]
        x_in = refs[14:14 + nx]
        of_ref, ob_ref, sf_ref, sb_ref, tf_ref, tb_ref = refs[14 + nx:20 + nx]
        x_out = refs[20 + nx:20 + 2 * nx]
        s_scr = refs[20 + 2 * nx]
        sems = refs[21 + 2 * nx:]

        @pl.when(pl.program_id(0) == 0)
        def _():
            s_scr[...] = jnp.zeros_like(s_scr)
            xc.start(x_in, x_out, sems)

        def chain(d, h, q_r, k_r, v_r, o_ref, sh_ref, th_ref, masks, decs):
            incl, strict, _, eye = masks
            sl = slice(h * DH, (h + 1) * DH)
            qh, kh, vh = q_r[:, sl], k_r[:, sl], v_r[:, sl]
            gcol, grow, bcol, gl = _head_scalars(d, h, *decs)
            _, _, a, egc, rhs_u, rhs_w, qs, attn, etail = _chunk_local(qh, kh, vh, gcol, grow, bcol, gl, incl, strict)
            yield
            n = -a
            t = jnp.where(eye, 1.0, 0.0) + n
            p = _mm3(n, n)
            yield
            for _ in range(4):
                r = _mm3(jnp.concatenate([t, p], axis=0), p)
                yield
                t = t + r[:CH]
                p = r[CH:]
            t = t + _mm3(t, p)
            yield
            sol = _mm3(t, jnp.concatenate([rhs_u, rhs_w], axis=1))
            u, w = sol[:, :DH], sol[:, DH:]
            s = s_scr[d, h]
            sh_ref[0, h] = s
            th_ref[0, h] = t
            yield
            ws = _mm(jnp.concatenate([w, qs * egc], axis=0), s)
            yield
            v_new = u - ws[:CH]
            o_ref[:, sl] = ws[CH:] + _mm(attn, v_new)
            s_scr[d, h] = s * jnp.exp(gl) + _mm(kh * etail, v_new, TN)

        chains = []
        for d, (q_r, k_r, v_r, ab_r, abt_r, o_ref, sh_ref, th_ref) in enumerate(
                ((qf, kf, vf, abf, abtf, of_ref, sf_ref, tf_ref), (qb, kb_, vb, abb, abtb, ob_ref, sb_ref, tb_ref))):
            masks = _masks(d)
            _, beta_full, gc_full, gl_full, _, gct = _decays(
                d, ab_r[...], abt_r[0], ar[...], dr[...], ac[...], dc[...], masks[0], masks[2])
            for h in range(NH):
                chains.append(chain(d, h, q_r, k_r, v_r, o_ref, sh_ref, th_ref, masks, (beta_full, gc_full, gl_full, gct)))
        _lockstep(chains)

        @pl.when(pl.program_id(0) == nch // 2)
        def _():
            xc.forward(x_in, x_out, sems)

        @pl.when(pl.program_id(0) == nch - 1)
        def _():
            xc.wait(x_in, x_out, sems)

    def row(c, col):
        return pl.BlockSpec((CH, D), lambda s: (c(s), col))

    def chunk_in(c):
        return [row(c, 0), row(c, 1), row(c, 2), pl.BlockSpec((CH, LANE), lambda s: (c(s), 0)),
                pl.BlockSpec((1, 4 * NH, CH), lambda s: (c(s), 0, 0))]

    def hist(c, n):
        return pl.BlockSpec((1, NH, n, n), lambda s: (c(s), 0, 0, 0))

    small = [_full((1, LANE)), _full((1, LANE)), _full((4 * NH, 1)), _full((4 * NH, 1))]
    return pl.pallas_call(
        body, name="gdn_fwd", grid=(nch,),
        out_shape=(_sds((lt, D)), _sds((lt, D)), _sds((nch, NH, DH, DH)), _sds((nch, NH, DH, DH)),
                   _sds((nch, NH, CH, CH)), _sds((nch, NH, CH, CH))) + xc.out_shape,
        in_specs=chunk_in(cf) + chunk_in(cb) + small + [_ANY] * nx,
        out_specs=(pl.BlockSpec((CH, D), lambda s: (cf(s), 0)), pl.BlockSpec((CH, D), lambda s: (cb(s), 0)),
                   hist(cf, DH), hist(cb, DH), hist(cf, CH), hist(cb, CH)) + tuple([_ANY] * nx),
        scratch_shapes=[pltpu.VMEM((2, NH, DH, DH), F32)] + xc.scratch,
        compiler_params=_params(("arbitrary",)),
    )(qkv, qkv, qkv, pab, abt, qkv, qkv, qkv, pab, abt, alog_r, dtb_r, alog_c, dtb_c, *xc_arrays)


def _gdn_bwd(qkv, pab, abt, alog_r, dtb_r, alog_c, dtb_c, s_f, s_b, t_f, t_b, do, lc, xc, xc_arrays):
    lt = qkv.shape[0]
    nch, cf, cb = _scan_specs(lt, lc, True)
    nx = xc.n

    def body(*refs):
        qf, kf, vf, abf, abtf, sf_ref, tf_ref, dof, qb, kb_, vb, abb, abtb, sb_ref, tb_ref, dob, ar, dr, ac, dc = refs[:20]
        x_in = refs[20:20 + nx]
        dqf_ref, dqb_ref, dcf_ref, dcb_ref, drf_ref, drb_ref, vcol_ref, vrow_ref = refs[20 + nx:28 + nx]
        x_out = refs[28 + nx:28 + 2 * nx]
        ds_scr = refs[28 + 2 * nx]
        sems = refs[29 + 2 * nx:]

        @pl.when(pl.program_id(0) == 0)
        def _():
            ds_scr[...] = jnp.zeros_like(ds_scr)
            vcol_ref[...] = jnp.zeros_like(vcol_ref)
            vrow_ref[...] = jnp.zeros_like(vrow_ref)
            xc.start(x_in, x_out, sems)

        alog_r_, dtb_r_, alog_c_, dtb_c_ = ar[...], dr[...], ac[...], dc[...]
        lane2 = lax.broadcasted_iota(jnp.int32, (1, LANE), 1)
        acc = [[], []]

        def chain(d, h, q_r, k_r, v_r, sh_ref, th_ref, do_r, dq_ref, masks, decs):
            incl, strict, _, _ = masks
            idx = d * NH + h
            sl = slice(h * DH, (h + 1) * DH)
            qh, kh, vh = q_r[:, sl], k_r[:, sl], v_r[:, sl]
            doh = do_r[:, sl]
            gcol, grow, bcol, gl = _head_scalars(d, h, *decs)
            decay, kb, a, egc, rhs_u, rhs_w, qs, attn, etail = _chunk_local(qh, kh, vh, gcol, grow, bcol, gl, incl, strict)
            t = th_ref[0, h]
            s = sh_ref[0, h]
            ds_new = ds_scr[d, h]
            sol = _mm3(t, jnp.concatenate([rhs_u, rhs_w], axis=1))
            u, w = sol[:, :DH], sol[:, DH:]
            q_dec = qs * egc
            k_tail = kh * etail
            egl = jnp.exp(gl)
            dv_new = _mm(attn, doh, TN) + _mm(k_tail, ds_new)
            dq_dec = _mm(doh, s, NT)
            dgl = jnp.sum(jnp.sum(ds_new * s, axis=0, keepdims=True), axis=-1, keepdims=True) * egl
            yield
            v_new = u - _mm(w, s)
            dw = -_mm(dv_new, s, NT)
            ds_scr[d, h] = ds_new * egl + _mm(q_dec, doh, TN) - _mm(w, dv_new, TN)
            yield
            dattn = jnp.where(incl, _mm(doh, v_new, NT), 0.0)
            dk_tail = _mm(v_new, ds_new, NT)
            dr = _mm3(t, jnp.concatenate([dv_new, dw], axis=1), TN)
            dr_u, dr_w = dr[:, :DH], dr[:, DH:]
            yield
            da = -jnp.where(strict, _mm3(dr, sol, NT), 0.0)
            nq = dattn * decay
            dqs = _mm(nq, kh) + dq_dec * egc
            dk = _mm(nq, qs, TN)
            yield
            dv = dr_u * bcol
            dbeta = jnp.sum(dr_u * vh, axis=-1, keepdims=True)
            dgc = jnp.sum(dr_w * rhs_w, axis=-1, keepdims=True)
            m = da * decay
            dkb = dr_w * egc + _mm(m, kh)
            dk = dk + _mm(m, kb, TN)
            pq = da * a + dattn * attn
            dgc = dgc + jnp.sum(pq, axis=-1, keepdims=True) + jnp.sum(dq_dec * q_dec, axis=-1, keepdims=True)
            dgr = -jnp.sum(pq, axis=0, keepdims=True)
            tt = jnp.sum(dk_tail * k_tail, axis=-1, keepdims=True)
            dk = dk + dk_tail * etail + dkb * bcol
            dgc = dgc - tt
            dgl = dgl + jnp.sum(tt, axis=0, keepdims=True)
            dbeta = dbeta + jnp.sum(dkb * kh, axis=-1, keepdims=True)
            dq_ref[:, sl] = dqs * (DH ** -0.5)
            dq_ref[:, D + h * DH:D + (h + 1) * DH] = dk
            dq_ref[:, 2 * D + h * DH:2 * D + (h + 1) * DH] = dv
            acc[d].append((idx, dgc, dgl, dbeta, dgr))

        dirs = ((qf, kf, vf, abf, abtf, sf_ref, tf_ref, dof, dqf_ref, dcf_ref, drf_ref),
                (qb, kb_, vb, abb, abtb, sb_ref, tb_ref, dob, dqb_ref, dcb_ref, drb_ref))
        chains, ctx_d = [], []
        for d, (q_r, k_r, v_r, ab_r, abt_r, sh_ref, th_ref, do_r, dq_ref, _, _) in enumerate(dirs):
            masks = _masks(d)
            ab, abt = ab_r[...], abt_r[0]
            g_full, beta_full, gc_full, gl_full, gt_full, gct = _decays(
                d, ab, abt, alog_r_, dtb_r_, alog_c_, dtb_c_, masks[0], masks[2])
            ctx_d.append((masks, ab, abt, g_full, beta_full, gt_full))
            for h in range(NH):
                chains.append(chain(d, h, q_r, k_r, v_r, sh_ref, th_ref, do_r, dq_ref, masks,
                                    (beta_full, gc_full, gl_full, gct)))
        _lockstep(chains)
        for d in range(2):
            (incl, _, incl_t, _), ab, abt, g_full, beta_full, gt_full = ctx_d[d]
            dcol_ref, drow_ref = dirs[d][9], dirs[d][10]
            dgc_col = jnp.zeros((CH, LANE), F32)
            dgl_row = jnp.zeros((1, LANE), F32)
            dbeta_col = jnp.zeros((CH, LANE), F32)
            dgc_row = jnp.zeros((4 * NH, CH), F32)
            for idx, dgc, dgl, dbeta, dgr in acc[d]:
                oh = _lane_onehot(idx)
                dgc_col = dgc_col + dgc * oh
                dgl_row = dgl_row + dgl * oh
                dbeta_col = dbeta_col + dbeta * _lane_onehot(2 * NH + idx)
                ohc = (lax.broadcasted_iota(jnp.int32, (4 * NH, 1), 0) == idx).astype(F32)
                dgc_row = dgc_row + ohc * dgr
            dg_col = _mmh(incl_t.astype(F32), dgc_col) + dgl_row
            dg_row = _mmh(dgc_row, incl.astype(F32))
            sg_col = _sigmoid(ab + dtb_r_)
            da_col = dg_col * (-jnp.exp(alog_r_)) * sg_col
            dcol_ref[...] = da_col + dbeta_col * beta_full * (1.0 - beta_full)
            da_row = dg_row * (-jnp.exp(alog_c_)) * _sigmoid(abt + dtb_c_)
            drow_ref[0] = da_row
            vcol_ref[0:1, :] += jnp.sum(dg_col * g_full, axis=0, keepdims=True)
            vcol_ref[1:2, :] += jnp.sum(da_col, axis=0, keepdims=True)
            rl = jnp.sum(dg_row * gt_full, axis=-1, keepdims=True)
            rd = jnp.sum(da_row, axis=-1, keepdims=True)
            vrow_ref[...] += jnp.where(lane2 == 0, rl, 0.0) + jnp.where(lane2 == 1, rd, 0.0)

        @pl.when(pl.program_id(0) == nch // 2)
        def _():
            xc.forward(x_in, x_out, sems)

        @pl.when(pl.program_id(0) == nch - 1)
        def _():
            xc.wait(x_in, x_out, sems)

    def row(c, col):
        return pl.BlockSpec((CH, D), lambda s: (c(s), col))

    def hist(c, n):
        return pl.BlockSpec((1, NH, n, n), lambda s: (c(s), 0, 0, 0))

    def chunk_in(c):
        return [row(c, 0), row(c, 1), row(c, 2), pl.BlockSpec((CH, LANE), lambda s: (c(s), 0)),
                pl.BlockSpec((1, 4 * NH, CH), lambda s: (c(s), 0, 0)), hist(c, DH), hist(c, CH), row(c, 0)]

    small = [_full((1, LANE)), _full((1, LANE)), _full((4 * NH, 1)), _full((4 * NH, 1))]
    return pl.pallas_call(
        body, name="gdn_bwd", grid=(nch,),
        out_shape=(_sds((lt, 3 * D)), _sds((lt, 3 * D)), _sds((lt, LANE)), _sds((lt, LANE)),
                   _sds((nch, 4 * NH, CH)), _sds((nch, 4 * NH, CH)), _sds((8, LANE)), _sds((4 * NH, LANE))) + xc.out_shape,
        in_specs=chunk_in(cf) + chunk_in(cb) + small + [_ANY] * nx,
        out_specs=(pl.BlockSpec((CH, 3 * D), lambda s: (cf(s), 0)), pl.BlockSpec((CH, 3 * D), lambda s: (cb(s), 0)),
                   pl.BlockSpec((CH, LANE), lambda s: (cf(s), 0)), pl.BlockSpec((CH, LANE), lambda s: (cb(s), 0)),
                   pl.BlockSpec((1, 4 * NH, CH), lambda s: (cf(s), 0, 0)), pl.BlockSpec((1, 4 * NH, CH), lambda s: (cb(s), 0, 0)),
                   _full((8, LANE)), _full((4 * NH, LANE))) + tuple([_ANY] * nx),
        scratch_shapes=[pltpu.VMEM((2, NH, DH, DH), F32)] + xc.scratch,
        compiler_params=_params(("arbitrary",)),
    )(qkv, qkv, qkv, pab, abt, s_f, t_f, do, qkv, qkv, qkv, pab, abt, s_b, t_b, do, alog_r, dtb_r, alog_c, dtb_c,
      *xc_arrays)


def _post(p, o_f, o_b, x, tgt, w_pa, w_pb, w_out, w_sp, w_spt, b_spb, ln_g, ln_b, g_on, g_post, gate_x, lc):
    lt = p.shape[0]
    l = x.shape[0]
    tm = GC
    nct = lc // tm

    def body(p_ref, of_ref, ob_ref, x_ref, t_ref, wpa, wpb, wout, wsp, wspt, bspb, lng_ref, lnb_ref, gon_ref, gpost_ref, gate_ref,
             dp_ref, do_ref, dy_ref, ya_ref, yb_ref, mg_ref, da_ref, db_ref, dout_ref, dwsp_ref, dbsp_ref, vec_ref):
        i = pl.program_id(0)

        @pl.when(i == 0)
        def _():
            dwsp_ref[...] = jnp.zeros_like(dwsp_ref)
            dbsp_ref[...] = jnp.zeros_like(dbsp_ref)
            vec_ref[...] = jnp.zeros_like(vec_ref)

        @pl.when(i < nct)
        def _():
            dp_ref[...] = jnp.zeros_like(dp_ref)
            do_ref[...] = jnp.zeros_like(do_ref)

        @pl.when(i >= nct)
        def _():
            lng, lnb, gon, gpost, gate = lng_ref[...], lnb_ref[...], gon_ref[...], gpost_ref[...], gate_ref[...]
            zb, ua, va, za, ga, gb = [p_ref[:, j * D:(j + 1) * D] for j in range(6)]
            o = of_ref[...] + ob_ref[...]
            szb, dszb = _silu_g(zb)
            nh_l, r_l = [], []
            for h in range(NH):
                oh = o[:, h * DH:(h + 1) * DH]
                r = lax.rsqrt(jnp.mean(oh * oh, axis=-1, keepdims=True) + EPS)
                nh_l.append(oh * r)
                r_l.append(r)
            nrm_b = jnp.concatenate(nh_l, axis=-1)
            gon_t = jnp.concatenate([gon] * NH, axis=-1)
            y_b = nrm_b * gon_t * szb
            u, du_dua = _gelu_g(ua)
            gv, dgv_dva = _gelu_g(va)
            xc = gv - jnp.mean(gv, axis=-1, keepdims=True)
            rs_ln = lax.rsqrt(jnp.mean(xc * xc, axis=-1, keepdims=True) + EPS)
            vhat = xc * rs_ln
            v = vhat * lng + lnb
            s_sp = jnp.concatenate(
                [_mm(wsp[g], v[:, g * DH:(g + 1) * DH]) + bspb[g] for g in range(NH)], axis=-1)
            sza, dsza = _silu_g(za)
            y_a = u * s_sp * sza
            a_pr = _mm(y_a, wpa[...])
            b_pr = _mm(y_b, wpb[...])
            sga = _sigmoid(ga)
            sgb = _sigmoid(gb)
            merged = sga * a_pr + sgb * b_pr
            out = _mm(merged, wout[...])
            rs_o = lax.rsqrt(jnp.mean(out * out, axis=-1, keepdims=True) + EPS)
            n_o = out * rs_o
            rr = n_o * gpost
            diff = x_ref[...] + gate * rr - t_ref[...]
            vec_ref[5:6, :] += jnp.sum(diff * diff, axis=0, keepdims=True)
            dy = diff * (1.0 / D)
            dy_ref[...] = dy
            vec_ref[0:1, :] += jnp.sum(dy * rr, axis=0, keepdims=True)
            dr = dy * gate
            vec_ref[1:2, :] += jnp.sum(dr * n_o, axis=0, keepdims=True)
            dn_o = dr * gpost
            dout = rs_o * (dn_o - n_o * jnp.mean(dn_o * n_o, axis=-1, keepdims=True))
            dmerged = _mm(dout, wout[...], NT)
            d_a = dmerged * sga
            d_b = dmerged * sgb
            dga = dmerged * a_pr * sga * (1.0 - sga)
            dgb = dmerged * b_pr * sgb * (1.0 - sgb)
            dy_a = _mm(d_a, wpa[...], NT)
            dy_b = _mm(d_b, wpb[...], NT)
            ya_ref[...] = y_a.astype(ya_ref.dtype)
            yb_ref[...] = y_b.astype(yb_ref.dtype)
            mg_ref[...] = merged.astype(mg_ref.dtype)
            da_ref[...] = d_a.astype(da_ref.dtype)
            db_ref[...] = d_b.astype(db_ref.dtype)
            dout_ref[...] = dout.astype(dout_ref.dtype)
            dua = dy_a * s_sp * sza * du_dua
            ds_sp = dy_a * u * sza
            dza = dy_a * u * s_sp * dsza
            dv_l = []
            for g in range(NH):
                ds_g = ds_sp[:, g * DH:(g + 1) * DH]
                dv_l.append(_mm(wspt[g], ds_g))
                dwsp_ref[g] += _mm(ds_g, v[:, g * DH:(g + 1) * DH], NT)
                dbsp_ref[g] += ds_g
            dv = jnp.concatenate(dv_l, axis=-1)
            vec_ref[2:3, :] += jnp.sum(dv * vhat, axis=0, keepdims=True)
            vec_ref[3:4, :] += jnp.sum(dv, axis=0, keepdims=True)
            dvh = dv * lng
            dgv = rs_ln * (dvh - jnp.mean(dvh, axis=-1, keepdims=True) - vhat * jnp.mean(dvh * vhat, axis=-1, keepdims=True))
            dva = dgv * dgv_dva
            dzb = dy_b * nrm_b * gon_t * dszb
            dgon_full = jnp.sum(dy_b * nrm_b * szb, axis=0, keepdims=True)
            dgon = dgon_full[:, 0:DH]
            for h in range(1, NH):
                dgon = dgon + dgon_full[:, h * DH:(h + 1) * DH]
            vec_ref[4:5, 0:DH] += dgon
            dnb = dy_b * gon_t * szb
            do_l = []
            for h in range(NH):
                sl = slice(h * DH, (h + 1) * DH)
                dn_h = dnb[:, sl]
                do_l.append(r_l[h] * (dn_h - nh_l[h] * jnp.mean(dn_h * nh_l[h], axis=-1, keepdims=True)))
            do_ref[...] = jnp.concatenate(do_l, axis=-1)
            for j, val in enumerate((dzb, dua, dva, dza, dga, dgb)):
                dp_ref[:, j * D:(j + 1) * D] = val.astype(dp_ref.dtype)

    xrow = lambda i: (jnp.maximum(i - nct, 0), 0)
    wspec = _full((D, D))
    gspec = _full((NH, GC, GC))
    vspec = _full((1, D))
    bf_out = _sds((l, D), _BF)
    return pl.pallas_call(
        body, name="post", grid=(lt // tm,),
        out_shape=(_sds((lt, NREST), _BF), _sds((lt, D)), _sds((l, D)), bf_out, bf_out, bf_out, bf_out, bf_out, bf_out,
                   _sds((NH, GC, GC)), _sds((NH, GC, GC)), _sds((8, D))),
        in_specs=[pl.BlockSpec((tm, NREST), lambda i: (i, 0)), pl.BlockSpec((tm, D), lambda i: (i, 0)),
                  pl.BlockSpec((tm, D), lambda i: (i, 0)), pl.BlockSpec((tm, D), xrow), pl.BlockSpec((tm, D), xrow),
                  wspec, wspec, wspec, gspec, gspec, gspec, vspec, vspec, _full((1, DH)), vspec, vspec],
        out_specs=(pl.BlockSpec((tm, NREST), lambda i: (i, 0)), pl.BlockSpec((tm, D), lambda i: (i, 0)),
                   pl.BlockSpec((tm, D), xrow), pl.BlockSpec((tm, D), xrow), pl.BlockSpec((tm, D), xrow),
                   pl.BlockSpec((tm, D), xrow), pl.BlockSpec((tm, D), xrow), pl.BlockSpec((tm, D), xrow),
                   pl.BlockSpec((tm, D), xrow), gspec, gspec, _full((8, D))),
        compiler_params=_params(("arbitrary",)),
    )(p, o_f, o_b, x, tgt, w_pa, w_pb, w_out, w_sp, w_spt, b_spb, ln_g, ln_b, g_on, g_post, gate_x)


def _sum_parts(parts, name):
    r = parts.shape[1]
    tr = r if NDEV * r * LANE * 4 <= (8 << 20) else _tile(r, (512, 256, 128, 64, 32, 16, 8))

    def body(p_ref, o_ref):
        acc = p_ref[0]
        for s in range(1, NDEV):
            acc = acc + p_ref[s]
        o_ref[...] = acc

    return pl.pallas_call(
        body, name=name, out_shape=_sds((r, LANE)), grid=(r // tr,),
        in_specs=[pl.BlockSpec((NDEV, tr, LANE), lambda i: (0, i, 0))],
        out_specs=pl.BlockSpec((tr, LANE), lambda i: (i, 0)),
        compiler_params=_params(("parallel",)),
    )(parts)


def _mod_bwd(c_all, c_ctx, dmx, dmc, w_mod_g):
    ws = w_mod_g.shape[2]

    def body(ca_ref, cc_ref, dsh_ref, dmx_ref, dmc_ref, dmc_sh_ref, w_ref, gw_ref, gc_ref, gb_ref):
        sc, _ = _silu_g(ca_ref[...])
        scc, dscc = _silu_g(cc_ref[...])
        dmc_tot = jnp.sum(dmc_ref[...], axis=0, keepdims=True)
        gb_ref[...] = jnp.sum(dmx_ref[...], axis=0, keepdims=True) + dmc_tot
        lhs = jnp.concatenate([sc, jnp.broadcast_to(scc, (8, D))], axis=0)
        rhs = jnp.concatenate([dsh_ref[...], dmc_sh_ref[...]], axis=0)
        gw_ref[...] = _mmh(lhs, rhs, TN)
        acc = jnp.zeros((8, D), F32)
        tot8 = jnp.broadcast_to(dmc_tot, (8, 3 * D))
        for j in range(NDEV):
            acc = acc + _mm(tot8[:, j * ws:(j + 1) * ws], w_ref[j], NT)
        gc_ref[...] = acc[0:1, :] * dscc

    return pl.pallas_call(
        body, name="mod_bwd", out_shape=(_sds((D, ws)), _sds((1, D)), _sds((1, 3 * D))),
        compiler_params=_params(),
    )(c_all, c_ctx, _my_cols(dmx, ws), dmx, dmc, _my_cols(dmc, ws), w_mod_g)


def _my_cols(a, ws):
    me = 4 * lax.axis_index("x") + 2 * lax.axis_index("y") + lax.axis_index("c")
    return lax.dynamic_slice_in_dim(a, me * ws, ws, axis=1)


def _pair_sum(both, name):
    _, n, r, c = both.shape
    tr = _tile(r, (256, 128, 64, 32, 16, 8))

    def body(p_ref, o_ref):
        o_ref[...] = (p_ref[0].astype(F32) + p_ref[1].astype(F32)).astype(o_ref.dtype)

    return pl.pallas_call(
        body, name=name, out_shape=_sds((n, r, c), both.dtype), grid=(n, r // tr),
        in_specs=[pl.BlockSpec((2, 1, tr, c), lambda j, i: (0, j, i, 0))],
        out_specs=pl.BlockSpec((1, tr, c), lambda j, i: (j, i, 0)),
        compiler_params=_params(("parallel", "parallel")),
    )(both)


def _adamw(parts, w, m, v, name, chip_sums_below=None):
    s_, r, c = parts.shape
    tr = _tile(r, (128, 64, 32, 16, 8)) if r * c * 4 > (1 << 20) else r
    c1 = 1.0 / (1.0 - ADAM_B1 ** ADAM_STEP)
    c2 = 1.0 / (1.0 - ADAM_B2 ** ADAM_STEP)

    def body(p_ref, w_ref, m_ref, v_ref, g_ref, d_ref, nm_ref, nv_ref):
        if chip_sums_below is None:
            part = lambda s: p_ref[s].astype(F32)
        else:
            core = lax.axis_index("c")
            me = 4 * lax.axis_index("x") + 2 * lax.axis_index("y") + core
            every = me >= chip_sums_below
            part = lambda s: jnp.where(every | (core == s % 2), p_ref[s].astype(F32), 0.0)
        g = part(0)
        for s in range(1, s_):
            g = g + part(s)
        m_new = ADAM_B1 * m_ref[...] + (1.0 - ADAM_B1) * g
        v_new = ADAM_B2 * v_ref[...] + (1.0 - ADAM_B2) * (g * g)
        g_ref[...] = g
        nm_ref[...] = m_new
        nv_ref[...] = v_new
        d_ref[...] = -ADAM_LR * ((m_new * c1) / (jnp.sqrt(v_new * c2) + ADAM_EPS) + ADAM_WD * w_ref[...])

    blk = pl.BlockSpec((tr, c), lambda i: (i, 0))
    o = _sds((r, c))
    return pl.pallas_call(
        body, name=name, out_shape=(o, o, o, o), grid=(r // tr,),
        in_specs=[pl.BlockSpec((s_, tr, c), lambda i: (0, i, 0)), blk, blk, blk],
        out_specs=(blk, blk, blk, blk),
        compiler_params=_params(("parallel",)),
    )(parts, w, m, v)


def _rows(a):
    flat = a.reshape(-1)
    n = flat.shape[0]
    r = -(-n // (8 * LANE)) * 8
    return jnp.pad(flat, (0, r * LANE - n)).reshape(r, LANE)


def _pack(items):
    parts, layout, at = [], [], 0
    for name, a in items:
        rws = _rows(a.astype(F32))
        layout.append((name, at, rws.shape[0], a.shape))
        parts.append(rws)
        at += rws.shape[0]
    return jnp.concatenate(parts, axis=0), layout


def _unpack(packed, layout):
    out = {}
    for name, at, r, shape in layout:
        n = 1
        for s in shape:
            n *= s
        out[name] = packed[at:at + r].reshape(-1)[:n].reshape(shape)
    return out


def kernel(x, c, ctx, c_ctx, w_mod, b_mod, g_pre, g_post, w_in, w_conv, a_log, dt_bias, g_onorm, gm_ln_g, gm_ln_b, w_sp, b_sp, w_pa, w_pb, w_out, loss_target, m_c_ctx, m_w_mod, m_b_mod, m_g_pre, m_g_post, m_w_in, m_w_conv, m_a_log, m_dt_bias, m_g_onorm, m_gm_ln_g, m_gm_ln_b, m_w_sp, m_b_sp, m_w_pa, m_w_pb, m_w_out, v_c_ctx, v_w_mod, v_b_mod, v_g_pre, v_g_post, v_w_in, v_w_conv, v_a_log, v_dt_bias, v_g_onorm, v_gm_ln_g, v_gm_ln_b, v_w_sp, v_b_sp, v_w_pa, v_w_pb, v_w_out):
    l = x.shape[1]
    lc = ctx.shape[1]
    lt = l + lc
    nch = lt // CH
    me = 4 * lax.axis_index("x") + 2 * lax.axis_index("y") + lax.axis_index("c")
    wsh = w_in.shape[2]
    off_a = 3 * D
    n_ab = 4 * NH
    jb = off_a // wsh
    o1 = off_a - jb * wsh
    o2 = o1 + n_ab
    assert o2 <= wsh and NREST == (NDEV - jb) * wsh - o2
    split = jb + 1

    wp = -(-wsh // LANE) * LANE
    widen = lambda a: jnp.pad(a, [(0, 0)] * (a.ndim - 1) + [(0, wp - wsh)])
    w_in_bf = widen(w_in[0].astype(_BF))
    wg_lo, wg_mod, wg_conv, c_all = _exchange(
        [w_in_bf, w_mod[0].astype(_BF), w_conv[0], c], ["gather_lo", "gather", "gather", "gather"],
        "gather_first", split)
    w_qkv = jnp.concatenate([wg_lo[j][:, :wsh] for j in range(jb)] + [wg_lo[jb][:, :o1]], axis=1)
    w_ab = jnp.pad(wg_lo[jb][:, o1:o2], ((0, 0), (0, LANE - n_ab)))
    wconv_full = jnp.moveaxis(wg_conv, 0, 1).reshape(3, 3 * D)
    c_all = c_all.reshape(NDEV, D)

    cc = jnp.concatenate([c, c_ctx.reshape(1, D), jnp.zeros((6, D), F32)], axis=0)
    mods = _modulation(cc, wg_mod, b_mod)
    xa = jnp.concatenate([ctx[0], x[0]], axis=0)
    h = _prenorm(xa, mods, g_pre, lc)
    p_qkv = _matmul_nn(h, w_qkv, "in_proj_qkv")
    pab = _matmul_nn(h, w_ab, "in_proj_ab")
    abt = jnp.swapaxes(pab[:, :n_ab].reshape(nch, CH, n_ab), 1, 2)
    alog16, dtb16 = a_log.reshape(1, 2 * NH), dt_bias.reshape(1, 2 * NH)
    alog_r = jnp.pad(alog16, ((0, 0), (0, LANE - 2 * NH)))
    dtb_r = jnp.pad(dtb16, ((0, 0), (0, LANE - 2 * NH)))
    alog_c = jnp.pad(alog16.reshape(2 * NH, 1), ((0, 2 * NH), (0, 0)))
    dtb_c = jnp.pad(dtb16.reshape(2 * NH, 1), ((0, 2 * NH), (0, 0)))
    qkv = _qkv_fwd(p_qkv, wconv_full, lc)
    late = [w_in_bf, w_pa[0].astype(_BF), w_pb[0].astype(_BF), w_out[0].astype(_BF)]
    xc_late = _Exchange(zip(late, ["gather_hi", "gather", "gather", "gather"]), split)
    o_f, o_b, s_f, s_b, t_f, t_b, wg_hi, wg_pa, wg_pb, wg_out = _gdn_fwd(
        qkv, pab, abt, alog_r, dtb_r, alog_c, dtb_c, lc, xc_late, late)
    w_rest = jnp.concatenate([wg_lo[jb][:, o2:wsh]] + [wg_hi[j][:, :wsh] for j in range(split, NDEV)], axis=1)
    wf_pa, wf_pb, wf_out = wg_pa.reshape(D, D), wg_pb.reshape(D, D), wg_out.reshape(D, D)
    p_rest = _matmul_nn(h, w_rest, "in_proj_rest")

    w_spt = jnp.swapaxes(w_sp[0], 1, 2)
    b_spb = jnp.broadcast_to(b_sp[0][:, :, None], (NH, GC, GC))
    gate_x = mods[0:1, 2 * D:]
    dp_rest, do, dy, ya, yb, mg, d_a, d_b, dout, dwsp, dbsp_l, pvec = _post(
        p_rest, o_f, o_b, x[0], loss_target[0], wf_pa, wf_pb, wf_out, w_sp[0], w_spt, b_spb, gm_ln_g, gm_ln_b,
        g_onorm, g_post, gate_x, lc)

    dw_rest = _matmul_tn(h, dp_rest, "dw_in_rest")
    o3 = wsh - o2
    chunks_hi = widen(jnp.moveaxis(dw_rest[:, o3:].reshape(D, NDEV - split, wsh), 1, 0))
    dw_pa = _matmul_tn(ya, d_a, "dw_pa").reshape(NDEV, D // NDEV, D)
    dw_pb = _matmul_tn(yb, d_b, "dw_pb").reshape(NDEV, D // NDEV, D)
    dw_out = _matmul_tn(mg, dout, "dw_out").reshape(NDEV, D // NDEV, D)
    small_a, lay_a = _pack([
        ("g_post", pvec[1]), ("g_onorm", pvec[4, :DH]), ("gm_ln_g", pvec[2]), ("gm_ln_b", pvec[3]), ("w_sp", dwsp),
        ("b_sp", jnp.sum(dbsp_l, axis=-1)), ("loss", pvec[5]), ("dgate", pvec[0])])
    early = [chunks_hi, dw_pa, dw_pb, dw_out, small_a]
    xc_early = _Exchange(zip(early, ["scatter_hi", "scatter", "scatter", "scatter", "gather"]), split)

    dqkv_f, dqkv_b, dcol_f, dcol_b, drow_f, drow_b, gvec_c, gvec_r, r_in, r_pa, r_pb, r_out, small_a_all = _gdn_bwd(
        qkv, pab, abt, alog_r, dtb_r, alog_c, dtb_c, s_f, s_b, t_f, t_b, do, lc, xc_early, early)
    dp_qkv, dwconv = _qkv_bwd(p_qkv, wconv_full, dqkv_f, dqkv_b, lc)
    drow = jnp.swapaxes(drow_f + drow_b, 1, 2).reshape(lt, n_ab)
    dpab = (dcol_f + dcol_b + jnp.pad(drow, ((0, 0), (0, LANE - n_ab)))).astype(_BF)

    dw_qkv = _matmul_tn(h, dp_qkv, "dw_in_qkv")
    dw_ab = _matmul_tn(h, dpab, "dw_in_ab")
    dw_lo = jnp.concatenate([dw_qkv, dw_ab[:, :n_ab], dw_rest[:, :o3]], axis=1)
    chunks_lo = widen(jnp.moveaxis(dw_lo.reshape(D, split, wsh), 1, 0))
    (both,) = _exchange([chunks_lo], ["sibling"], "pair_swap")
    chip_lo = _pair_sum(both, "pair_sum")
    xc_last = _Exchange([(chip_lo, "scatter_par_lo")], split)
    dh, r_in = _dh_matmul(dp_rest, dp_qkv, dpab, w_rest, w_qkv, w_ab, xc_last, [chip_lo], {0: r_in})
    grad_x, nvec = _prenorm_bwd(xa, dh, dy, mods, g_pre, lc)

    dalog = gvec_c[0, :2 * NH] + gvec_r[:2 * NH, 0]
    ddtb = gvec_c[1, :2 * NH] + gvec_r[:2 * NH, 1]
    small_b, lay_b = _pack([
        ("g_pre", nvec[4]), ("a_log", dalog), ("dt_bias", ddtb), ("w_conv", dwconv),
        ("dshift", nvec[0]), ("dscale", nvec[1]), ("dshift_c", nvec[2]), ("dscale_c", nvec[3])])
    (small_b_all,) = _exchange([small_b], ["gather"], "gather_small")
    tot = _unpack(_sum_parts(small_a_all, "sum_small_a"), lay_a)
    tot.update(_unpack(_sum_parts(small_b_all, "sum_small_b"), lay_b))

    def per_device(packed_all, layout, name):
        at, r = [(a_, r_) for nm, a_, r_, _ in layout if nm == name][0]
        return packed_all[:, at:at + r].reshape(NDEV, -1)

    dmx_all = jnp.concatenate([per_device(small_b_all, lay_b, "dshift"), per_device(small_b_all, lay_b, "dscale"),
                               per_device(small_a_all, lay_a, "dgate")], axis=1)
    dmc_all = jnp.concatenate([per_device(small_b_all, lay_b, "dshift_c"), per_device(small_b_all, lay_b, "dscale_c"),
                               jnp.zeros((NDEV, D), F32)], axis=1)
    g_wmod, g_cctx, g_bmod = _mod_bwd(c_all, c_ctx.reshape(1, D), dmx_all, dmc_all, wg_mod)
    loss = 0.5 / D * jnp.sum(tot["loss"])
    ws_conv = w_conv.shape[2]
    g_wconv = lax.dynamic_slice_in_dim(tot["w_conv"], me * ws_conv, ws_conv, axis=1)

    small_names = ["c_ctx", "b_mod", "g_pre", "g_post", "a_log", "dt_bias", "g_onorm", "gm_ln_g", "gm_ln_b",
                   "w_sp", "b_sp", "w_conv"]
    wts = dict(c_ctx=c_ctx, b_mod=b_mod, g_pre=g_pre, g_post=g_post, a_log=a_log, dt_bias=dt_bias, g_onorm=g_onorm,
               gm_ln_g=gm_ln_g, gm_ln_b=gm_ln_b, w_sp=w_sp, b_sp=b_sp, w_conv=w_conv)
    ms = dict(c_ctx=m_c_ctx, b_mod=m_b_mod, g_pre=m_g_pre, g_post=m_g_post, a_log=m_a_log, dt_bias=m_dt_bias,
              g_onorm=m_g_onorm, gm_ln_g=m_gm_ln_g, gm_ln_b=m_gm_ln_b, w_sp=m_w_sp, b_sp=m_b_sp, w_conv=m_w_conv)
    vs = dict(c_ctx=v_c_ctx, b_mod=v_b_mod, g_pre=v_g_pre, g_post=v_g_post, a_log=v_a_log, dt_bias=v_dt_bias,
              g_onorm=v_g_onorm, gm_ln_g=v_gm_ln_g, gm_ln_b=v_gm_ln_b, w_sp=v_w_sp, b_sp=v_b_sp, w_conv=v_w_conv)
    gs = dict(tot)
    gs.update(c_ctx=g_cctx, b_mod=g_bmod, w_conv=g_wconv)
    gpk, play = _pack([(nm, gs[nm].reshape(wts[nm].shape)) for nm in small_names])
    wpk, _ = _pack([(nm, wts[nm]) for nm in small_names])
    mpk, _ = _pack([(nm, ms[nm]) for nm in small_names])
    vpk, _ = _pack([(nm, vs[nm]) for nm in small_names])
    res_small = [_unpack(a, play) for a in _adamw(gpk[None], wpk, mpk, vpk, "adamw_small")]
    res_big = {
        "w_mod": _adamw(g_wmod[None], w_mod[0], m_w_mod[0], v_w_mod[0], "adamw_w_mod"),
        "w_in": [a[:, :wsh] for a in _adamw(r_in, widen(w_in[0]), widen(m_w_in[0]), widen(v_w_in[0]), "adamw_w_in",
                                            chip_sums_below=split)],
        "w_pa": _adamw(r_pa, w_pa[0], m_w_pa[0], v_w_pa[0], "adamw_w_pa"),
        "w_pb": _adamw(r_pb, w_pb[0], m_w_pb[0], v_w_pb[0], "adamw_w_pb"),
        "w_out": _adamw(r_out, w_out[0], m_w_out[0], v_w_out[0], "adamw_w_out"),
    }
    order = ["c_ctx", "w_mod", "b_mod", "g_pre", "g_post", "w_in", "w_conv", "a_log", "dt_bias", "g_onorm",
             "gm_ln_g", "gm_ln_b", "w_sp", "b_sp", "w_pa", "w_pb", "w_out"]
    outs = [loss, grad_x[None]]
    for k in range(4):
        for nm in order:
            if nm in res_big:
                outs.append(res_big[nm][k][None])
            else:
                outs.append(res_small[k][nm])
    return tuple(outs)
```

```python
import functools

import jax
import jax.numpy as jnp
from jax import lax
from jax.experimental import pallas as pl
from jax.experimental.pallas import tpu as pltpu

F32 = jnp.float32
_BF = jnp.bfloat16
_HI = lax.Precision.HIGHEST
D = 1024
NH = 8
DH = 128
CH = 64
GC = 128
NREST = 6 * D
NMAIN = NREST + 3 * D
EPS = 1e-6
LANE = 128
NDEV = 8
VMEM_LIMIT = 56 * 1024 * 1024
MESH = pl.DeviceIdType.MESH

ADAM_LR, ADAM_B1, ADAM_B2, ADAM_EPS, ADAM_WD, ADAM_STEP = 0.001, 0.9, 0.999, 1e-08, 0.01, 10

NN = ((1,), (0,))
NT = ((1,), (1,))
TN = ((0,), (0,))


def _dot(a, b, dims=NN, prec=None):
    return lax.dot_general(a, b, (dims, ((), ())), precision=prec, preferred_element_type=F32)


def _mm(a, b, dims=NN):
    return _dot(a.astype(_BF), b.astype(_BF), dims)


def _mmh(a, b, dims=NN):
    return _dot(a.astype(F32), b.astype(F32), dims, _HI)


def _split(a):
    hi = a.astype(_BF)
    return hi, (a - hi.astype(F32)).astype(_BF)


def _mm3(a, b, dims=NN):
    ah, al = _split(a)
    bh, bl = _split(b)
    return _dot(ah, bh, dims) + (_dot(ah, bl, dims) + _dot(al, bh, dims))


def _sigmoid(x):
    return 1.0 / (1.0 + jnp.exp(-x))


def _silu_g(x):
    s = _sigmoid(x)
    return x * s, s * (1.0 + x * (1.0 - s))


def _gelu_g(x):
    c = 0.7978845608028654
    t = jnp.tanh(c * (x + 0.044715 * (x * x * x)))
    cdf = 0.5 * (1.0 + t)
    return x * cdf, cdf + 0.5 * x * (1.0 - t * t) * c * (1.0 + 3 * 0.044715 * x * x)


def _softplus(x):
    return jnp.maximum(x, 0.0) + jnp.log(1.0 + jnp.exp(-jnp.abs(x)))


def _params(sem=None):
    return pltpu.CompilerParams(dimension_semantics=sem, vmem_limit_bytes=VMEM_LIMIT)


def _tile(n, pref):
    for t in pref:
        if n % t == 0:
            return t
    return n


def _full(shape):
    nd = len(shape)
    return pl.BlockSpec(shape, lambda *_: (0,) * nd)


def _sds(shape, dt=F32):
    return jax.ShapeDtypeStruct(shape, dt)


MAX_PIECES = 12
PIECE_BYTES = 256 * 1024


def _piece_slices(shape, itemsize):
    total = itemsize
    for d in shape:
        total *= d
    want = min(MAX_PIECES, total // PIECE_BYTES)
    lead = shape[0] if len(shape) >= 3 else 1
    rows = shape[-2] if len(shape) >= 2 else 1
    if want < 2 or lead > want:
        return [()]
    m = max([n for n in (8, 4, 2, 1) if n * lead <= want and rows % (16 * n) == 0], default=1)
    if m * lead < 2:
        return [()]
    rs = rows // m
    mid = (slice(None),) * max(len(shape) - 3, 0)
    if len(shape) >= 3:
        return [(i,) + mid + (pl.ds(j * rs, rs),) for i in range(lead) for j in range(m)]
    return [(pl.ds(j * rs, rs),) for j in range(m)]


class _Pieces:
    def __init__(self, copies):
        self.copies = copies

    def start(self):
        for cp in self.copies:
            cp.start()

    def wait_send(self):
        for cp in self.copies:
            cp.wait_send()

    def wait_recv(self):
        for cp in self.copies:
            cp.wait_recv()

    def wait(self):
        for cp in self.copies:
            cp.wait()


class _Exchange:
    def __init__(self, specs, split):
        self.specs = list(specs)
        self.split = split
        self.n = len(self.specs)
        def out(a, k):
            if k == "sibling":
                return (1,) + tuple(a.shape)
            return (NDEV,) + (tuple(a.shape) if k.startswith("gather") else tuple(a.shape[1:]))

        self.out_shape = tuple(_sds(out(a, k), a.dtype) for a, k in self.specs)
        self.pieces = [_piece_slices(o.shape[1:], jnp.dtype(o.dtype).itemsize) for o in self.out_shape]
        self.sem_base = [(NDEV - 1) * sum(len(p) for p in self.pieces[:a]) for a in range(self.n + 1)]
        self.scratch = [pltpu.SemaphoreType.DMA((self.sem_base[-1],)), pltpu.SemaphoreType.DMA((self.sem_base[-1],)),
                        pltpu.SemaphoreType.DMA((self.sem_base[-1] // (NDEV - 1),))]

    def _local(self, sems, a, src, dst):
        base = self.sem_base[a] // (NDEV - 1)
        return _Pieces([pltpu.make_async_copy(src.at[sl] if sl else src, dst.at[sl] if sl else dst, sems[2].at[base + p])
                        for p, sl in enumerate(self.pieces[a])])

    def _remote(self, sems, a, k, src, dst, to):
        send_sems, recv_sems, _ = sems
        base = self.sem_base[a] + k * len(self.pieces[a])
        return _Pieces([
            pltpu.make_async_remote_copy(
                src_ref=src.at[sl] if sl else src, dst_ref=dst.at[sl] if sl else dst, send_sem=send_sems.at[base + p],
                recv_sem=recv_sems.at[base + p], device_id=to, device_id_type=MESH)
            for p, sl in enumerate(self.pieces[a])])

    def _ok(self, kind, idx):
        if kind.endswith("_lo"):
            return idx < self.split
        if kind.endswith("_hi"):
            return idx >= self.split
        return True

    def _phases(self, ins, outs, sems):
        x, y, c = lax.axis_index("x"), lax.axis_index("y"), lax.axis_index("c")
        me = 4 * x + 2 * y + c
        sib = (x, y, 1 - c)
        sib_idx = 4 * x + 2 * y + (1 - c)
        chips = [(1 - x, y), (x, 1 - y), (1 - x, 1 - y)]
        starts, forwards, waits = [], [], []
        for a, (_, kind) in enumerate(self.specs):
            ok = functools.partial(self._ok, kind)
            if kind.startswith("gather"):
                def copy(k, block, to, src=None, a=a):
                    rows = outs[a].at[block]
                    return self._remote(sems, a, k, rows if src is None else src, rows, to)

                loc = self._local(sems, a, ins[a], outs[a].at[me])
                first = [copy(0, me, sib, ins[a])] + [copy(1 + j, me, (*chip, c), ins[a]) for j, chip in enumerate(chips)]
                starts += [(ok(me), loc.start)] + [(ok(me), cp.start) for cp in first]
                waits += [(ok(me), loc.wait)] + [(ok(me), cp.wait_send) for cp in first]
                for j, chip in enumerate(chips):
                    origin = 4 * chip[0] + 2 * chip[1] + c
                    passed = copy(4 + j, origin, sib)
                    forwards += [(ok(origin), copy(1 + j, origin, sib).wait_recv), (ok(origin), passed.start)]
                    waits.append((ok(origin), passed.wait_send))
                    other = 4 * chip[0] + 2 * chip[1] + (1 - c)
                    waits.append((ok(other), copy(4 + j, other, sib).wait_recv))
                waits.append((ok(sib_idx), copy(0, sib_idx, sib).wait_recv))
            elif kind == "sibling":
                swap = self._remote(sems, a, 0, ins[a], outs[a].at[0], sib)
                starts.append((True, swap.start))
                waits += [(True, swap.wait_send), (True, swap.wait_recv)]
            else:
                base = self.split if kind.endswith("_hi") else 0
                same_core_only = kind.endswith("_par_lo")

                def src(idx, a=a, base=base):
                    return ins[a].at[jnp.clip(idx - base, 0, ins[a].shape[0] - 1)]

                loc = self._local(sems, a, src(me), outs[a].at[me])
                starts.append((ok(me), loc.start))
                waits.append((ok(me), loc.wait))
                for k in range(1, NDEV):
                    if same_core_only and k & 1:
                        continue
                    px = 1 - x if (k >> 2) & 1 else x
                    py = 1 - y if (k >> 1) & 1 else y
                    pc = 1 - c if k & 1 else c
                    pidx = 4 * px + 2 * py + pc
                    send = self._remote(sems, a, k - 1, src(pidx), outs[a].at[me], (px, py, pc))
                    arrive = self._remote(sems, a, k - 1, src(pidx), outs[a].at[pidx], (px, py, pc))
                    starts.append((ok(pidx), send.start))
                    waits += [(ok(pidx), send.wait_send), (ok(me), arrive.wait_recv)]
        return starts, forwards, waits

    @staticmethod
    def _run(actions):
        for cond, fn in actions:
            if cond is True:
                fn()
            else:
                pl.when(cond)(fn)

    def start(self, ins, outs, sems):
        self._run(self._phases(ins, outs, sems)[0])

    def forward(self, ins, outs, sems):
        self._run(self._phases(ins, outs, sems)[1])

    def wait(self, ins, outs, sems):
        self._run(self._phases(ins, outs, sems)[2])


_ANY = pl.BlockSpec(memory_space=pl.ANY)


def _exchange(arrays, kinds, name, split=0, into=None):
    xc = _Exchange(zip(arrays, kinds), split)
    n = xc.n
    into = into or {}
    ni = len(into)

    def body(*refs):
        ins, outs, sems = refs[:n], refs[n + ni:2 * n + ni], refs[2 * n + ni:]
        xc.start(ins, outs, sems)
        xc.forward(ins, outs, sems)
        xc.wait(ins, outs, sems)

    return pl.pallas_call(
        body, name=name, out_shape=xc.out_shape, in_specs=[_ANY] * (n + ni), out_specs=tuple([_ANY] * n),
        scratch_shapes=xc.scratch, input_output_aliases={n + t: a for t, a in enumerate(into)},
    )(*arrays, *into.values())


def _matmul_nn(a, b, name):
    m, kk = a.shape
    n = b.shape[1]
    tm = _tile(m, (1088, 1024, 640, 512, 256, 128))
    tn = _tile(n, (512, 256, 128))

    def body(a_ref, b_ref, o_ref):
        o_ref[...] = _mm(a_ref[...], b_ref[...])

    return pl.pallas_call(
        body, name=name, out_shape=_sds((m, n)), grid=(n // tn, m // tm),
        in_specs=[pl.BlockSpec((tm, kk), lambda j, i: (i, 0)), pl.BlockSpec((kk, tn), lambda j, i: (0, j))],
        out_specs=pl.BlockSpec((tm, tn), lambda j, i: (i, j)),
        compiler_params=_params(("parallel", "parallel")),
    )(a, b)


def _matmul_tn(a, b, name):
    kk, m = a.shape
    n = b.shape[1]
    tk = _tile(kk, (1088, 1024, 640, 512, 256, 128))
    tn = _tile(n, (1024, 512, 256, 128))
    nk = kk // tk

    def body(a_ref, b_ref, o_ref, acc_ref):
        k = pl.program_id(1)

        @pl.when(k == 0)
        def _():
            acc_ref[...] = jnp.zeros_like(acc_ref)

        acc_ref[...] += _mm(a_ref[...], b_ref[...], TN)

        @pl.when(k == nk - 1)
        def _():
            o_ref[...] = acc_ref[...].astype(o_ref.dtype)

    return pl.pallas_call(
        body, name=name, out_shape=_sds((m, n), _BF), grid=(n // tn, nk),
        in_specs=[pl.BlockSpec((tk, m), lambda j, k: (k, 0)), pl.BlockSpec((tk, tn), lambda j, k: (k, j))],
        out_specs=pl.BlockSpec((m, tn), lambda j, k: (0, j)),
        scratch_shapes=[pltpu.VMEM((m, tn), F32)],
        compiler_params=_params(("parallel", "arbitrary")),
    )(a, b)


def _dh_matmul(dp_rest, dp_qkv, dpab, w_rest, w_qkv, w_ab, xc, xc_arrays, xc_into):
    lt = dp_rest.shape[0]
    tm = _tile(lt, (1088, 1024, 640, 512, 256, 128))
    nr, nq = dp_rest.shape[1] // D, dp_qkv.shape[1] // D
    nx, ni = xc.n, len(xc_into)
    ni_steps = lt // tm

    def body(*refs):
        dr_ref, dq_ref, ab_ref, wr_ref, wq_ref, wab_ref = refs[:6]
        x_in = refs[6:6 + nx]
        o_ref = refs[6 + nx + ni]
        x_out = refs[7 + nx + ni:7 + 2 * nx + ni]
        sems = refs[7 + 2 * nx + ni:]
        i = pl.program_id(0)
        k = pl.program_id(1)

        @pl.when((i == 0) & (k == 0))
        def _():
            xc.start(x_in, x_out, sems)

        @pl.when(k == 0)
        def _():
            o_ref[...] = _mm(ab_ref[...], wab_ref[...], NT)

        @pl.when(k < nr)
        def _():
            o_ref[...] += _mm(dr_ref[...], wr_ref[...], NT)

        @pl.when(k >= nr)
        def _():
            o_ref[...] += _mm(dq_ref[...], wq_ref[...], NT)

        @pl.when((i == ni_steps - 1) & (k == nr + nq - 1))
        def _():
            xc.wait(x_in, x_out, sems)

    rk = lambda k: jnp.minimum(k, nr - 1)
    qk = lambda k: jnp.maximum(k - nr, 0)
    return pl.pallas_call(
        body, name="dh_matmul", out_shape=(_sds((lt, D)),) + xc.out_shape, grid=(lt // tm, nr + nq),
        in_specs=[pl.BlockSpec((tm, D), lambda i, k: (i, rk(k))), pl.BlockSpec((tm, D), lambda i, k: (i, qk(k))),
                  pl.BlockSpec((tm, LANE), lambda i, k: (i, 0)),
                  pl.BlockSpec((D, D), lambda i, k: (0, rk(k))), pl.BlockSpec((D, D), lambda i, k: (0, qk(k))),
                  _full((D, LANE))] + [_ANY] * (nx + ni),
        out_specs=(pl.BlockSpec((tm, D), lambda i, k: (i, 0)),) + tuple([_ANY] * nx),
        scratch_shapes=xc.scratch, input_output_aliases={6 + nx + t: 1 + a for t, a in enumerate(xc_into)},
        compiler_params=_params(("arbitrary", "arbitrary")),
    )(dp_rest, dp_qkv, dpab, w_rest, w_qkv, w_ab, *xc_arrays, *xc_into.values())


def _modulation(cc, w_mod_g, b_mod):
    ws = w_mod_g.shape[2]

    def body(c_ref, w_ref, b_ref, o_ref):
        s, _ = _silu_g(c_ref[...])
        o_ref[...] = _mm(s, w_ref[0]) + b_ref[...]

    return pl.pallas_call(
        body, name="modulation", out_shape=_sds((8, 3 * D)), grid=(NDEV,),
        in_specs=[_full((8, D)), pl.BlockSpec((1, D, ws), lambda j: (j, 0, 0)), pl.BlockSpec((1, ws), lambda j: (0, j))],
        out_specs=pl.BlockSpec((8, ws), lambda j: (0, j)),
        compiler_params=_params(("parallel",)),
    )(cc, w_mod_g, b_mod)


def _prenorm(xa, mods, g_pre, lc):
    lt = xa.shape[0]
    tm = _tile(lc, (256, 128))
    nct = lc // tm

    def body(x_ref, m_ref, g_ref, o_ref):
        x = x_ref[...]
        is_ctx = pl.program_id(0) < nct
        shift = jnp.where(is_ctx, m_ref[1:2, 0:D], m_ref[0:1, 0:D])
        scale = jnp.where(is_ctx, m_ref[1:2, D:2 * D], m_ref[0:1, D:2 * D])
        r = lax.rsqrt(jnp.mean(x * x, axis=-1, keepdims=True) + EPS)
        o_ref[...] = ((x * r * g_ref[...]) * (1.0 + scale) + shift).astype(o_ref.dtype)

    return pl.pallas_call(
        body, name="prenorm", out_shape=_sds((lt, D), _BF), grid=(lt // tm,),
        in_specs=[pl.BlockSpec((tm, D), lambda i: (i, 0)), _full((8, 3 * D)), _full((1, D))],
        out_specs=pl.BlockSpec((tm, D), lambda i: (i, 0)),
        compiler_params=_params(("parallel",)),
    )(xa, mods, g_pre)


def _prenorm_bwd(xa, dh, dy, mods, g_pre, lc):
    lt = xa.shape[0]
    tm = _tile(lc, (256, 128))
    nct = lc // tm
    nl = (lt - lc) // tm

    def body(x_ref, dh_ref, dy_ref, m_ref, g_ref, gx_ref, vec_ref):
        i = pl.program_id(0)

        @pl.when(i == 0)
        def _():
            vec_ref[...] = jnp.zeros_like(vec_ref)

        x = x_ref[...]
        dh = dh_ref[...]
        g = g_ref[...]
        is_ctx = i < nct
        scale = jnp.where(is_ctx, m_ref[1:2, D:2 * D], m_ref[0:1, D:2 * D])
        r = lax.rsqrt(jnp.mean(x * x, axis=-1, keepdims=True) + EPS)
        n = x * r
        hn = n * g
        dsh = jnp.sum(dh, axis=0, keepdims=True)
        dsc = jnp.sum(dh * hn, axis=0, keepdims=True)
        dhn = dh * (1.0 + scale)
        vec_ref[4:5, :] += jnp.sum(dhn * n, axis=0, keepdims=True)
        dn = dhn * g
        dx = r * (dn - n * jnp.mean(dn * n, axis=-1, keepdims=True))

        @pl.when(is_ctx)
        def _():
            vec_ref[2:3, :] += dsh
            vec_ref[3:4, :] += dsc

        @pl.when(jnp.logical_not(is_ctx))
        def _():
            vec_ref[0:1, :] += dsh
            vec_ref[1:2, :] += dsc
            gx_ref[...] = dy_ref[...] + dx

    xrow = lambda i: (jnp.maximum(i - nct, 0), 0)
    return pl.pallas_call(
        body, name="prenorm_bwd", out_shape=(_sds((nl * tm, D)), _sds((8, D))), grid=(lt // tm,),
        in_specs=[pl.BlockSpec((tm, D), lambda i: (i, 0)), pl.BlockSpec((tm, D), lambda i: (i, 0)),
                  pl.BlockSpec((tm, D), xrow), _full((8, 3 * D)), _full((1, D))],
        out_specs=(pl.BlockSpec((tm, D), xrow), _full((8, D))),
        compiler_params=_params(("arbitrary",)),
    )(xa, dh, dy, mods, g_pre)


def _conv_parts(x, w, lc):
    lt = x.shape[0]
    row = lax.broadcasted_iota(jnp.int32, x.shape, 0)
    first = (row == 0) | (row == lc)
    last = (row == lc - 1) | (row == lt - 1)
    xp = jnp.where(first, 0.0, pltpu.roll(x, 1, 0))
    xn = jnp.where(last, 0.0, pltpu.roll(x, lt - 1, 0))
    y = w[0:1, :] * xp + w[1:2, :] * x + w[2:3, :] * xn
    return xp, xn, y, first, last


def _qkv_fwd(p, w_conv, lc):
    lt = p.shape[0]

    def body(p_ref, w_ref, o_ref):
        _, _, y, _, _ = _conv_parts(p_ref[...], w_ref[...], lc)
        s, _ = _silu_g(y)
        rs = lax.rsqrt(jnp.sum(s * s, axis=-1, keepdims=True) + EPS)
        o_ref[...] = s * jnp.where(pl.program_id(0) < 2 * NH, rs, 1.0)

    return pl.pallas_call(
        body, name="qkv_fwd", out_shape=_sds((lt, 3 * D)), grid=(3 * NH,),
        in_specs=[pl.BlockSpec((lt, DH), lambda j: (0, j)), pl.BlockSpec((3, DH), lambda j: (0, j))],
        out_specs=pl.BlockSpec((lt, DH), lambda j: (0, j)),
        compiler_params=_params(("parallel",)),
    )(p, w_conv)


def _qkv_bwd(p, w_conv, dqkv_f, dqkv_b, lc):
    lt = p.shape[0]

    def body(p_ref, w_ref, df_ref, db_ref, dp_ref, dw_ref):
        w = w_ref[...]
        xp, xn, y, first, last = _conv_parts(p_ref[...], w, lc)
        s, ds_dy = _silu_g(y)
        dn = df_ref[...] + db_ref[...]
        rs = lax.rsqrt(jnp.sum(s * s, axis=-1, keepdims=True) + EPS)
        nrm = s * rs
        ds_n = rs * (dn - nrm * jnp.sum(dn * nrm, axis=-1, keepdims=True))
        ds = jnp.where(pl.program_id(0) < 2 * NH, ds_n, dn)
        dy = ds * ds_dy
        dw_ref[0:1, :] = jnp.sum(dy * xp, axis=0, keepdims=True)
        dw_ref[1:2, :] = jnp.sum(dy * p_ref[...], axis=0, keepdims=True)
        dw_ref[2:3, :] = jnp.sum(dy * xn, axis=0, keepdims=True)
        dyn = jnp.where(last, 0.0, pltpu.roll(dy, lt - 1, 0))
        dyp = jnp.where(first, 0.0, pltpu.roll(dy, 1, 0))
        dp_ref[...] = (w[1:2, :] * dy + w[0:1, :] * dyn + w[2:3, :] * dyp).astype(dp_ref.dtype)

    return pl.pallas_call(
        body, name="qkv_bwd", out_shape=(_sds((lt, 3 * D), _BF), _sds((3, 3 * D))), grid=(3 * NH,),
        in_specs=[pl.BlockSpec((lt, DH), lambda j: (0, j)), pl.BlockSpec((3, DH), lambda j: (0, j)),
                  pl.BlockSpec((lt, DH), lambda j: (0, j)), pl.BlockSpec((lt, DH), lambda j: (0, j))],
        out_specs=(pl.BlockSpec((lt, DH), lambda j: (0, j)), pl.BlockSpec((3, DH), lambda j: (0, j))),
        compiler_params=_params(("parallel",)),
    )(p, w_conv, dqkv_f, dqkv_b)


def _masks(d):
    ri = lax.broadcasted_iota(jnp.int32, (CH, CH), 0)
    ci = lax.broadcasted_iota(jnp.int32, (CH, CH), 1)
    incl = (ri >= ci) if d == 0 else (ri <= ci)
    strict = (ri > ci) if d == 0 else (ri < ci)
    incl_t = (ri <= ci) if d == 0 else (ri >= ci)
    return incl, strict, incl_t, ri == ci


def _decays(d, ab, abt, alog_r, dtb_r, alog_c, dtb_c, incl, incl_t):
    g_full = -jnp.exp(alog_r) * _softplus(ab + dtb_r)
    beta_full = _sigmoid(ab)
    gc_full = _mmh(incl.astype(F32), g_full)
    gl_full = jnp.sum(g_full, axis=0, keepdims=True)
    gt_full = -jnp.exp(alog_c) * _softplus(abt + dtb_c)
    gct = _mmh(gt_full, incl_t.astype(F32))
    return g_full, beta_full, gc_full, gl_full, gt_full, gct


def _lane_onehot(idx, n=LANE):
    return (lax.broadcasted_iota(jnp.int32, (1, n), 1) == idx).astype(F32)


def _head_scalars(d, h, beta_full, gc_full, gl_full, gct):
    idx = d * NH + h
    oh = _lane_onehot(idx)
    gcol = jnp.sum(gc_full * oh, axis=-1, keepdims=True)
    bcol = jnp.sum(beta_full * _lane_onehot(2 * NH + idx), axis=-1, keepdims=True)
    gl = jnp.sum(gl_full * oh, axis=-1, keepdims=True)
    grow = gct[idx:idx + 1, :]
    return gcol, grow, bcol, gl


def _lockstep(gens):
    live = list(gens)
    while live:
        nxt = []
        for g in live:
            try:
                next(g)
                nxt.append(g)
            except StopIteration:
                pass
        live = nxt


def _chunk_local(qh, kh, vh, gcol, grow, bcol, gl, incl, strict):
    decay = jnp.where(incl, jnp.exp(gcol - grow), 0.0)
    kb = kh * bcol
    a = jnp.where(strict, _mm(kb, kh, NT) * decay, 0.0)
    egc = jnp.exp(gcol)
    rhs_u = vh * bcol
    rhs_w = kb * egc
    qs = qh * (DH ** -0.5)
    attn = jnp.where(incl, _mm(qs, kh, NT) * decay, 0.0)
    etail = jnp.exp(gl - gcol)
    return decay, kb, a, egc, rhs_u, rhs_w, qs, attn, etail


def _scan_specs(lt, lc, bwd_pass):
    nch = lt // CH
    ncc = lc // CH
    if not bwd_pass:
        cf = lambda s: s
        cb = lambda s: jnp.where(s < ncc, ncc - 1 - s, nch + ncc - 1 - s)
    else:
        cf = lambda s: nch - 1 - s
        cb = lambda s: jnp.where(s < nch - ncc, ncc + s, s - (nch - ncc))
    return nch, cf, cb


def _gdn_fwd(qkv, pab, abt, alog_r, dtb_r, alog_c, dtb_c, lc, xc, xc_arrays):
    lt = qkv.shape[0]
    nch, cf, cb = _scan_specs(lt, lc, False)
    nx = xc.n

    def body(*refs):
        qf, kf, vf, abf, abtf, qb, kb_, vb, abb, abtb, ar, dr, ac, dc = refs[:14]
        x_in = refs[14:14 + nx]
        of_ref, ob_ref, sf_ref, sb_ref, tf_ref, tb_ref = refs[14 + nx:20 + nx]
        x_out = refs[20 + nx:20 + 2 * nx]
        s_scr = refs[20 + 2 * nx]
        sems = refs[21 + 2 * nx:]

        @pl.when(pl.program_id(0) == 0)
        def _():
            s_scr[...] = jnp.zeros_like(s_scr)
            xc.start(x_in, x_out, sems)

        def chain(d, h, q_r, k_r, v_r, o_ref, sh_ref, th_ref, masks, decs):
            incl, strict, _, eye = masks
            sl = slice(h * DH, (h + 1) * DH)
            qh, kh, vh = q_r[:, sl], k_r[:, sl], v_r[:, sl]
            gcol, grow, bcol, gl = _head_scalars(d, h, *decs)
            _, _, a, egc, rhs_u, rhs_w, qs, attn, etail = _chunk_local(qh, kh, vh, gcol, grow, bcol, gl, incl, strict)
            yield
            n = -a
            t = jnp.where(eye, 1.0, 0.0) + n
            p = _mm3(n, n)
            yield
            for _ in range(4):
                r = _mm3(jnp.concatenate([t, p], axis=0), p)
                yield
                t = t + r[:CH]
                p = r[CH:]
            t = t + _mm3(t, p)
            yield
            sol = _mm3(t, jnp.concatenate([rhs_u, rhs_w], axis=1))
            u, w = sol[:, :DH], sol[:, DH:]
            s = s_scr[d, h]
            sh_ref[0, h] = s
            th_ref[0, h] = t
            yield
            ws = _mm(jnp.concatenate([w, qs * egc], axis=0), s)
            yield
            v_new = u - ws[:CH]
            o_ref[:, sl] = ws[CH:] + _mm(attn, v_new)
            s_scr[d, h] = s * jnp.exp(gl) + _mm(kh * etail, v_new, TN)

        chains = []
        for d, (q_r, k_r, v_r, ab_r, abt_r, o_ref, sh_ref, th_ref) in enumerate(
                ((qf, kf, vf, abf, abtf, of_ref, sf_ref, tf_ref), (qb, kb_, vb, abb, abtb, ob_ref, sb_ref, tb_ref))):
            masks = _masks(d)
            _, beta_full, gc_full, gl_full, _, gct = _decays(
                d, ab_r[...], abt_r[0], ar[...], dr[...], ac[...], dc[...], masks[0], masks[2])
            for h in range(NH):
                chains.append(chain(d, h, q_r, k_r, v_r, o_ref, sh_ref, th_ref, masks, (beta_full, gc_full, gl_full, gct)))
        _lockstep(chains)

        @pl.when(pl.program_id(0) == nch // 2)
        def _():
            xc.forward(x_in, x_out, sems)

        @pl.when(pl.program_id(0) == nch - 1)
        def _():
            xc.wait(x_in, x_out, sems)

    def row(c, col):
        return pl.BlockSpec((CH, D), lambda s: (c(s), col))

    def chunk_in(c):
        return [row(c, 0), row(c, 1), row(c, 2), pl.BlockSpec((CH, LANE), lambda s: (c(s), 0)),
                pl.BlockSpec((1, 4 * NH, CH), lambda s: (c(s), 0, 0))]

    def hist(c, n):
        return pl.BlockSpec((1, NH, n, n), lambda s: (c(s), 0, 0, 0))

    small = [_full((1, LANE)), _full((1, LANE)), _full((4 * NH, 1)), _full((4 * NH, 1))]
    return pl.pallas_call(
        body, name="gdn_fwd", grid=(nch,),
        out_shape=(_sds((lt, D)), _sds((lt, D)), _sds((nch, NH, DH, DH)), _sds((nch, NH, DH, DH)),
                   _sds((nch, NH, CH, CH)), _sds((nch, NH, CH, CH))) + xc.out_shape,
        in_specs=chunk_in(cf) + chunk_in(cb) + small + [_ANY] * nx,
        out_specs=(pl.BlockSpec((CH, D), lambda s: (cf(s), 0)), pl.BlockSpec((CH, D), lambda s: (cb(s), 0)),
                   hist(cf, DH), hist(cb, DH), hist(cf, CH), hist(cb, CH)) + tuple([_ANY] * nx),
        scratch_shapes=[pltpu.VMEM((2, NH, DH, DH), F32)] + xc.scratch,
        compiler_params=_params(("arbitrary",)),
    )(qkv, qkv, qkv, pab, abt, qkv, qkv, qkv, pab, abt, alog_r, dtb_r, alog_c, dtb_c, *xc_arrays)


def _gdn_bwd(qkv, pab, abt, alog_r, dtb_r, alog_c, dtb_c, s_f, s_b, t_f, t_b, do, lc, xc, xc_arrays):
    lt = qkv.shape[0]
    nch, cf, cb = _scan_specs(lt, lc, True)
    nx = xc.n

    def body(*refs):
        qf, kf, vf, abf, abtf, sf_ref, tf_ref, dof, qb, kb_, vb, abb, abtb, sb_ref, tb_ref, dob, ar, dr, ac, dc = refs[:20]
        x_in = refs[20:20 + nx]
        dqf_ref, dqb_ref, dcf_ref, dcb_ref, drf_ref, drb_ref, vcol_ref, vrow_ref = refs[20 + nx:28 + nx]
        x_out = refs[28 + nx:28 + 2 * nx]
        ds_scr = refs[28 + 2 * nx]
        sems = refs[29 + 2 * nx:]

        @pl.when(pl.program_id(0) == 0)
        def _():
            ds_scr[...] = jnp.zeros_like(ds_scr)
            vcol_ref[...] = jnp.zeros_like(vcol_ref)
            vrow_ref[...] = jnp.zeros_like(vrow_ref)
            xc.start(x_in, x_out, sems)

        alog_r_, dtb_r_, alog_c_, dtb_c_ = ar[...], dr[...], ac[...], dc[...]
        lane2 = lax.broadcasted_iota(jnp.int32, (1, LANE), 1)
        acc = [[], []]

        def chain(d, h, q_r, k_r, v_r, sh_ref, th_ref, do_r, dq_ref, masks, decs):
            incl, strict, _, _ = masks
            idx = d * NH + h
            sl = slice(h * DH, (h + 1) * DH)
            qh, kh, vh = q_r[:, sl], k_r[:, sl], v_r[:, sl]
            doh = do_r[:, sl]
            gcol, grow, bcol, gl = _head_scalars(d, h, *decs)
            decay, kb, a, egc, rhs_u, rhs_w, qs, attn, etail = _chunk_local(qh, kh, vh, gcol, grow, bcol, gl, incl, strict)
            t = th_ref[0, h]
            s = sh_ref[0, h]
            ds_new = ds_scr[d, h]
            sol = _mm3(t, jnp.concatenate([rhs_u, rhs_w], axis=1))
            u, w = sol[:, :DH], sol[:, DH:]
            q_dec = qs * egc
            k_tail = kh * etail
            egl = jnp.exp(gl)
            dv_new = _mm(attn, doh, TN) + _mm(k_tail, ds_new)
            dq_dec = _mm(doh, s, NT)
            dgl = jnp.sum(jnp.sum(ds_new * s, axis=0, keepdims=True), axis=-1, keepdims=True) * egl
            yield
            v_new = u - _mm(w, s)
            dw = -_mm(dv_new, s, NT)
            ds_scr[d, h] = ds_new * egl + _mm(q_dec, doh, TN) - _mm(w, dv_new, TN)
            yield
            dattn = jnp.where(incl, _mm(doh, v_new, NT), 0.0)
            dk_tail = _mm(v_new, ds_new, NT)
            dr = _mm3(t, jnp.concatenate([dv_new, dw], axis=1), TN)
            dr_u, dr_w = dr[:, :DH], dr[:, DH:]
            yield
            da = -jnp.where(strict, _mm3(dr, sol, NT), 0.0)
            nq = dattn * decay
            dqs = _mm(nq, kh) + dq_dec * egc
            dk = _mm(nq, qs, TN)
            yield
            dv = dr_u * bcol
            dbeta = jnp.sum(dr_u * vh, axis=-1, keepdims=True)
            dgc = jnp.sum(dr_w * rhs_w, axis=-1, keepdims=True)
            m = da * decay
            dkb = dr_w * egc + _mm(m, kh)
            dk = dk + _mm(m, kb, TN)
            pq = da * a + dattn * attn
            dgc = dgc + jnp.sum(pq, axis=-1, keepdims=True) + jnp.sum(dq_dec * q_dec, axis=-1, keepdims=True)
            dgr = -jnp.sum(pq, axis=0, keepdims=True)
            tt = jnp.sum(dk_tail * k_tail, axis=-1, keepdims=True)
            dk = dk + dk_tail * etail + dkb * bcol
            dgc = dgc - tt
            dgl = dgl + jnp.sum(tt, axis=0, keepdims=True)
            dbeta = dbeta + jnp.sum(dkb * kh, axis=-1, keepdims=True)
            dq_ref[:, sl] = dqs * (DH ** -0.5)
            dq_ref[:, D + h * DH:D + (h + 1) * DH] = dk
            dq_ref[:, 2 * D + h * DH:2 * D + (h + 1) * DH] = dv
            acc[d].append((idx, dgc, dgl, dbeta, dgr))

        dirs = ((qf, kf, vf, abf, abtf, sf_ref, tf_ref, dof, dqf_ref, dcf_ref, drf_ref),
                (qb, kb_, vb, abb, abtb, sb_ref, tb_ref, dob, dqb_ref, dcb_ref, drb_ref))
        chains, ctx_d = [], []
        for d, (q_r, k_r, v_r, ab_r, abt_r, sh_ref, th_ref, do_r, dq_ref, _, _) in enumerate(dirs):
            masks = _masks(d)
            ab, abt = ab_r[...], abt_r[0]
            g_full, beta_full, gc_full, gl_full, gt_full, gct = _decays(
                d, ab, abt, alog_r_, dtb_r_, alog_c_, dtb_c_, masks[0], masks[2])
            ctx_d.append((masks, ab, abt, g_full, beta_full, gt_full))
            for h in range(NH):
                chains.append(chain(d, h, q_r, k_r, v_r, sh_ref, th_ref, do_r, dq_ref, masks,
                                    (beta_full, gc_full, gl_full, gct)))
        _lockstep(chains)
        for d in range(2):
            (incl, _, incl_t, _), ab, abt, g_full, beta_full, gt_full = ctx_d[d]
            dcol_ref, drow_ref = dirs[d][9], dirs[d][10]
            dgc_col = jnp.zeros((CH, LANE), F32)
            dgl_row = jnp.zeros((1, LANE), F32)
            dbeta_col = jnp.zeros((CH, LANE), F32)
            dgc_row = jnp.zeros((4 * NH, CH), F32)
            for idx, dgc, dgl, dbeta, dgr in acc[d]:
                oh = _lane_onehot(idx)
                dgc_col = dgc_col + dgc * oh
                dgl_row = dgl_row + dgl * oh
                dbeta_col = dbeta_col + dbeta * _lane_onehot(2 * NH + idx)
                ohc = (lax.broadcasted_iota(jnp.int32, (4 * NH, 1), 0) == idx).astype(F32)
                dgc_row = dgc_row + ohc * dgr
            dg_col = _mmh(incl_t.astype(F32), dgc_col) + dgl_row
            dg_row = _mmh(dgc_row, incl.astype(F32))
            sg_col = _sigmoid(ab + dtb_r_)
            da_col = dg_col * (-jnp.exp(alog_r_)) * sg_col
            dcol_ref[...] = da_col + dbeta_col * beta_full * (1.0 - beta_full)
            da_row = dg_row * (-jnp.exp(alog_c_)) * _sigmoid(abt + dtb_c_)
            drow_ref[0] = da_row
            vcol_ref[0:1, :] += jnp.sum(dg_col * g_full, axis=0, keepdims=True)
            vcol_ref[1:2, :] += jnp.sum(da_col, axis=0, keepdims=True)
            rl = jnp.sum(dg_row * gt_full, axis=-1, keepdims=True)
            rd = jnp.sum(da_row, axis=-1, keepdims=True)
            vrow_ref[...] += jnp.where(lane2 == 0, rl, 0.0) + jnp.where(lane2 == 1, rd, 0.0)

        @pl.when(pl.program_id(0) == nch // 2)
        def _():
            xc.forward(x_in, x_out, sems)

        @pl.when(pl.program_id(0) == nch - 1)
        def _():
            xc.wait(x_in, x_out, sems)

    def row(c, col):
        return pl.BlockSpec((CH, D), lambda s: (c(s), col))

    def hist(c, n):
        return pl.BlockSpec((1, NH, n, n), lambda s: (c(s), 0, 0, 0))

    def chunk_in(c):
        return [row(c, 0), row(c, 1), row(c, 2), pl.BlockSpec((CH, LANE), lambda s: (c(s), 0)),
                pl.BlockSpec((1, 4 * NH, CH), lambda s: (c(s), 0, 0)), hist(c, DH), hist(c, CH), row(c, 0)]

    small = [_full((1, LANE)), _full((1, LANE)), _full((4 * NH, 1)), _full((4 * NH, 1))]
    return pl.pallas_call(
        body, name="gdn_bwd", grid=(nch,),
        out_shape=(_sds((lt, 3 * D)), _sds((lt, 3 * D)), _sds((lt, LANE)), _sds((lt, LANE)),
                   _sds((nch, 4 * NH, CH)), _sds((nch, 4 * NH, CH)), _sds((8, LANE)), _sds((4 * NH, LANE))) + xc.out_shape,
        in_specs=chunk_in(cf) + chunk_in(cb) + small + [_ANY] * nx,
        out_specs=(pl.BlockSpec((CH, 3 * D), lambda s: (cf(s), 0)), pl.BlockSpec((CH, 3 * D), lambda s: (cb(s), 0)),
                   pl.BlockSpec((CH, LANE), lambda s: (cf(s), 0)), pl.BlockSpec((CH, LANE), lambda s: (cb(s), 0)),
                   pl.BlockSpec((1, 4 * NH, CH), lambda s: (cf(s), 0, 0)), pl.BlockSpec((1, 4 * NH, CH), lambda s: (cb(s), 0, 0)),
                   _full((8, LANE)), _full((4 * NH, LANE))) + tuple([_ANY] * nx),
        scratch_shapes=[pltpu.VMEM((2, NH, DH, DH), F32)] + xc.scratch,
        compiler_params=_params(("arbitrary",)),
    )(qkv, qkv, qkv, pab, abt, s_f, t_f, do, qkv, qkv, qkv, pab, abt, s_b, t_b, do, alog_r, dtb_r, alog_c, dtb_c,
      *xc_arrays)


def _post(p, o_f, o_b, x, tgt, w_pa, w_pb, w_out, w_sp, w_spt, b_spb, ln_g, ln_b, g_on, g_post, gate_x, lc):
    lt = p.shape[0]
    l = x.shape[0]
    tm = GC
    nct = lc // tm

    def body(p_ref, of_ref, ob_ref, x_ref, t_ref, wpa, wpb, wout, wsp, wspt, bspb, lng_ref, lnb_ref, gon_ref, gpost_ref, gate_ref,
             dp_ref, do_ref, dy_ref, ya_ref, yb_ref, mg_ref, da_ref, db_ref, dout_ref, dwsp_ref, dbsp_ref, vec_ref):
        i = pl.program_id(0)

        @pl.when(i == 0)
        def _():
            dwsp_ref[...] = jnp.zeros_like(dwsp_ref)
            dbsp_ref[...] = jnp.zeros_like(dbsp_ref)
            vec_ref[...] = jnp.zeros_like(vec_ref)

        @pl.when(i < nct)
        def _():
            dp_ref[...] = jnp.zeros_like(dp_ref)
            do_ref[...] = jnp.zeros_like(do_ref)

        @pl.when(i >= nct)
        def _():
            lng, lnb, gon, gpost, gate = lng_ref[...], lnb_ref[...], gon_ref[...], gpost_ref[...], gate_ref[...]
            zb, ua, va, za, ga, gb = [p_ref[:, j * D:(j + 1) * D] for j in range(6)]
            o = of_ref[...] + ob_ref[...]
            szb, dszb = _silu_g(zb)
            nh_l, r_l = [], []
            for h in range(NH):
                oh = o[:, h * DH:(h + 1) * DH]
                r = lax.rsqrt(jnp.mean(oh * oh, axis=-1, keepdims=True) + EPS)
                nh_l.append(oh * r)
                r_l.append(r)
            nrm_b = jnp.concatenate(nh_l, axis=-1)
            gon_t = jnp.concatenate([gon] * NH, axis=-1)
            y_b = nrm_b * gon_t * szb
            u, du_dua = _gelu_g(ua)
            gv, dgv_dva = _gelu_g(va)
            xc = gv - jnp.mean(gv, axis=-1, keepdims=True)
            rs_ln = lax.rsqrt(jnp.mean(xc * xc, axis=-1, keepdims=True) + EPS)
            vhat = xc * rs_ln
            v = vhat * lng + lnb
            s_sp = jnp.concatenate(
                [_mm(wsp[g], v[:, g * DH:(g + 1) * DH]) + bspb[g] for g in range(NH)], axis=-1)
            sza, dsza = _silu_g(za)
            y_a = u * s_sp * sza
            a_pr = _mm(y_a, wpa[...])
            b_pr = _mm(y_b, wpb[...])
            sga = _sigmoid(ga)
            sgb = _sigmoid(gb)
            merged = sga * a_pr + sgb * b_pr
            out = _mm(merged, wout[...])
            rs_o = lax.rsqrt(jnp.mean(out * out, axis=-1, keepdims=True) + EPS)
            n_o = out * rs_o
            rr = n_o * gpost
            diff = x_ref[...] + gate * rr - t_ref[...]
            vec_ref[5:6, :] += jnp.sum(diff * diff, axis=0, keepdims=True)
            dy = diff * (1.0 / D)
            dy_ref[...] = dy
            vec_ref[0:1, :] += jnp.sum(dy * rr, axis=0, keepdims=True)
            dr = dy * gate
            vec_ref[1:2, :] += jnp.sum(dr * n_o, axis=0, keepdims=True)
            dn_o = dr * gpost
            dout = rs_o * (dn_o - n_o * jnp.mean(dn_o * n_o, axis=-1, keepdims=True))
            dmerged = _mm(dout, wout[...], NT)
            d_a = dmerged * sga
            d_b = dmerged * sgb
            dga = dmerged * a_pr * sga * (1.0 - sga)
            dgb = dmerged * b_pr * sgb * (1.0 - sgb)
            dy_a = _mm(d_a, wpa[...], NT)
            dy_b = _mm(d_b, wpb[...], NT)
            ya_ref[...] = y_a.astype(ya_ref.dtype)
            yb_ref[...] = y_b.astype(yb_ref.dtype)
            mg_ref[...] = merged.astype(mg_ref.dtype)
            da_ref[...] = d_a.astype(da_ref.dtype)
            db_ref[...] = d_b.astype(db_ref.dtype)
            dout_ref[...] = dout.astype(dout_ref.dtype)
            dua = dy_a * s_sp * sza * du_dua
            ds_sp = dy_a * u * sza
            dza = dy_a * u * s_sp * dsza
            dv_l = []
            for g in range(NH):
                ds_g = ds_sp[:, g * DH:(g + 1) * DH]
                dv_l.append(_mm(wspt[g], ds_g))
                dwsp_ref[g] += _mm(ds_g, v[:, g * DH:(g + 1) * DH], NT)
                dbsp_ref[g] += ds_g
            dv = jnp.concatenate(dv_l, axis=-1)
            vec_ref[2:3, :] += jnp.sum(dv * vhat, axis=0, keepdims=True)
            vec_ref[3:4, :] += jnp.sum(dv, axis=0, keepdims=True)
            dvh = dv * lng
            dgv = rs_ln * (dvh - jnp.mean(dvh, axis=-1, keepdims=True) - vhat * jnp.mean(dvh * vhat, axis=-1, keepdims=True))
            dva = dgv * dgv_dva
            dzb = dy_b * nrm_b * gon_t * dszb
            dgon_full = jnp.sum(dy_b * nrm_b * szb, axis=0, keepdims=True)
            dgon = dgon_full[:, 0:DH]
            for h in range(1, NH):
                dgon = dgon + dgon_full[:, h * DH:(h + 1) * DH]
            vec_ref[4:5, 0:DH] += dgon
            dnb = dy_b * gon_t * szb
            do_l = []
            for h in range(NH):
                sl = slice(h * DH, (h + 1) * DH)
                dn_h = dnb[:, sl]
                do_l.append(r_l[h] * (dn_h - nh_l[h] * jnp.mean(dn_h * nh_l[h], axis=-1, keepdims=True)))
            do_ref[...] = jnp.concatenate(do_l, axis=-1)
            for j, val in enumerate((dzb, dua, dva, dza, dga, dgb)):
                dp_ref[:, j * D:(j + 1) * D] = val.astype(dp_ref.dtype)

    xrow = lambda i: (jnp.maximum(i - nct, 0), 0)
    wspec = _full((D, D))
    gspec = _full((NH, GC, GC))
    vspec = _full((1, D))
    bf_out = _sds((l, D), _BF)
    return pl.pallas_call(
        body, name="post", grid=(lt // tm,),
        out_shape=(_sds((lt, NREST), _BF), _sds((lt, D)), _sds((l, D)), bf_out, bf_out, bf_out, bf_out, bf_out, bf_out,
                   _sds((NH, GC, GC)), _sds((NH, GC, GC)), _sds((8, D))),
        in_specs=[pl.BlockSpec((tm, NREST), lambda i: (i, 0)), pl.BlockSpec((tm, D), lambda i: (i, 0)),
                  pl.BlockSpec((tm, D), lambda i: (i, 0)), pl.BlockSpec((tm, D), xrow), pl.BlockSpec((tm, D), xrow),
                  wspec, wspec, wspec, gspec, gspec, gspec, vspec, vspec, _full((1, DH)), vspec, vspec],
        out_specs=(pl.BlockSpec((tm, NREST), lambda i: (i, 0)), pl.BlockSpec((tm, D), lambda i: (i, 0)),
                   pl.BlockSpec((tm, D), xrow), pl.BlockSpec((tm, D), xrow), pl.BlockSpec((tm, D), xrow),
                   pl.BlockSpec((tm, D), xrow), pl.BlockSpec((tm, D), xrow), pl.BlockSpec((tm, D), xrow),
                   pl.BlockSpec((tm, D), xrow), gspec, gspec, _full((8, D))),
        compiler_params=_params(("arbitrary",)),
    )(p, o_f, o_b, x, tgt, w_pa, w_pb, w_out, w_sp, w_spt, b_spb, ln_g, ln_b, g_on, g_post, gate_x)


def _sum_parts(parts, name):
    r = parts.shape[1]
    tr = r if NDEV * r * LANE * 4 <= (8 << 20) else _tile(r, (512, 256, 128, 64, 32, 16, 8))

    def body(p_ref, o_ref):
        acc = p_ref[0]
        for s in range(1, NDEV):
            acc = acc + p_ref[s]
        o_ref[...] = acc

    return pl.pallas_call(
        body, name=name, out_shape=_sds((r, LANE)), grid=(r // tr,),
        in_specs=[pl.BlockSpec((NDEV, tr, LANE), lambda i: (0, i, 0))],
        out_specs=pl.BlockSpec((tr, LANE), lambda i: (i, 0)),
        compiler_params=_params(("parallel",)),
    )(parts)


def _mod_bwd(c_all, c_ctx, dmx, dmc, w_mod_g):
    ws = w_mod_g.shape[2]

    def body(ca_ref, cc_ref, dsh_ref, dmx_ref, dmc_ref, dmc_sh_ref, w_ref, gw_ref, gc_ref, gb_ref):
        sc, _ = _silu_g(ca_ref[...])
        scc, dscc = _silu_g(cc_ref[...])
        dmc_tot = jnp.sum(dmc_ref[...], axis=0, keepdims=True)
        gb_ref[...] = jnp.sum(dmx_ref[...], axis=0, keepdims=True) + dmc_tot
        lhs = jnp.concatenate([sc, jnp.broadcast_to(scc, (8, D))], axis=0)
        rhs = jnp.concatenate([dsh_ref[...], dmc_sh_ref[...]], axis=0)
        gw_ref[...] = _mmh(lhs, rhs, TN)
        acc = jnp.zeros((8, D), F32)
        tot8 = jnp.broadcast_to(dmc_tot, (8, 3 * D))
        for j in range(NDEV):
            acc = acc + _mm(tot8[:, j * ws:(j + 1) * ws], w_ref[j], NT)
        gc_ref[...] = acc[0:1, :] * dscc

    return pl.pallas_call(
        body, name="mod_bwd", out_shape=(_sds((D, ws)), _sds((1, D)), _sds((1, 3 * D))),
        compiler_params=_params(),
    )(c_all, c_ctx, _my_cols(dmx, ws), dmx, dmc, _my_cols(dmc, ws), w_mod_g)


def _my_cols(a, ws):
    me = 4 * lax.axis_index("x") + 2 * lax.axis_index("y") + lax.axis_index("c")
    return lax.dynamic_slice_in_dim(a, me * ws, ws, axis=1)


def _pair_sum(mine, other, name):
    n, r, c = mine.shape
    tr = _tile(r, (256, 128, 64, 32, 16, 8))

    def body(a_ref, b_ref, o_ref):
        o_ref[...] = (a_ref[...].astype(F32) + b_ref[...].astype(F32)).astype(o_ref.dtype)

    blk = pl.BlockSpec((1, tr, c), lambda j, i: (j, i, 0))
    return pl.pallas_call(
        body, name=name, out_shape=_sds((n, r, c), mine.dtype), grid=(n, r // tr),
        in_specs=[blk, blk], out_specs=blk, compiler_params=_params(("parallel", "parallel")),
    )(mine, other)


def _adamw(parts, w, m, v, name, chip_sums_below=None):
    s_, r, c = parts.shape
    tr = _tile(r, (128, 64, 32, 16, 8)) if r * c * 4 > (1 << 20) else r
    c1 = 1.0 / (1.0 - ADAM_B1 ** ADAM_STEP)
    c2 = 1.0 / (1.0 - ADAM_B2 ** ADAM_STEP)

    def body(p_ref, w_ref, m_ref, v_ref, g_ref, d_ref, nm_ref, nv_ref):
        if chip_sums_below is None:
            part = lambda s: p_ref[s].astype(F32)
        else:
            core = lax.axis_index("c")
            me = 4 * lax.axis_index("x") + 2 * lax.axis_index("y") + core
            every = me >= chip_sums_below
            part = lambda s: jnp.where(every | (core == s % 2), p_ref[s].astype(F32), 0.0)
        g = part(0)
        for s in range(1, s_):
            g = g + part(s)
        m_new = ADAM_B1 * m_ref[...] + (1.0 - ADAM_B1) * g
        v_new = ADAM_B2 * v_ref[...] + (1.0 - ADAM_B2) * (g * g)
        g_ref[...] = g
        nm_ref[...] = m_new
        nv_ref[...] = v_new
        d_ref[...] = -ADAM_LR * ((m_new * c1) / (jnp.sqrt(v_new * c2) + ADAM_EPS) + ADAM_WD * w_ref[...])

    blk = pl.BlockSpec((tr, c), lambda i: (i, 0))
    o = _sds((r, c))
    return pl.pallas_call(
        body, name=name, out_shape=(o, o, o, o), grid=(r // tr,),
        in_specs=[pl.BlockSpec((s_, tr, c), lambda i: (0, i, 0)), blk, blk, blk],
        out_specs=(blk, blk, blk, blk),
        compiler_params=_params(("parallel",)),
    )(parts, w, m, v)


def _rows(a):
    flat = a.reshape(-1)
    n = flat.shape[0]
    r = -(-n // (8 * LANE)) * 8
    return jnp.pad(flat, (0, r * LANE - n)).reshape(r, LANE)


def _pack(items):
    parts, layout, at = [], [], 0
    for name, a in items:
        rws = _rows(a.astype(F32))
        layout.append((name, at, rws.shape[0], a.shape))
        parts.append(rws)
        at += rws.shape[0]
    return jnp.concatenate(parts, axis=0), layout


def _unpack(packed, layout):
    out = {}
    for name, at, r, shape in layout:
        n = 1
        for s in shape:
            n *= s
        out[name] = packed[at:at + r].reshape(-1)[:n].reshape(shape)
    return out


def kernel(x, c, ctx, c_ctx, w_mod, b_mod, g_pre, g_post, w_in, w_conv, a_log, dt_bias, g_onorm, gm_ln_g, gm_ln_b, w_sp, b_sp, w_pa, w_pb, w_out, loss_target, m_c_ctx, m_w_mod, m_b_mod, m_g_pre, m_g_post, m_w_in, m_w_conv, m_a_log, m_dt_bias, m_g_onorm, m_gm_ln_g, m_gm_ln_b, m_w_sp, m_b_sp, m_w_pa, m_w_pb, m_w_out, v_c_ctx, v_w_mod, v_b_mod, v_g_pre, v_g_post, v_w_in, v_w_conv, v_a_log, v_dt_bias, v_g_onorm, v_gm_ln_g, v_gm_ln_b, v_w_sp, v_b_sp, v_w_pa, v_w_pb, v_w_out):
    l = x.shape[1]
    lc = ctx.shape[1]
    lt = l + lc
    nch = lt // CH
    me = 4 * lax.axis_index("x") + 2 * lax.axis_index("y") + lax.axis_index("c")
    wsh = w_in.shape[2]
    off_a = 3 * D
    n_ab = 4 * NH
    jb = off_a // wsh
    o1 = off_a - jb * wsh
    o2 = o1 + n_ab
    assert o2 <= wsh and NREST == (NDEV - jb) * wsh - o2
    split = jb + 1

    wp = -(-wsh // LANE) * LANE
    widen = lambda a: jnp.pad(a, [(0, 0)] * (a.ndim - 1) + [(0, wp - wsh)])
    w_in_bf = widen(w_in[0].astype(_BF))
    wg_lo, wg_mod, wg_conv, c_all = _exchange(
        [w_in_bf, w_mod[0].astype(_BF), w_conv[0], c], ["gather_lo", "gather", "gather", "gather"],
        "gather_first", split)
    w_qkv = jnp.concatenate([wg_lo[j][:, :wsh] for j in range(jb)] + [wg_lo[jb][:, :o1]], axis=1)
    w_ab = jnp.pad(wg_lo[jb][:, o1:o2], ((0, 0), (0, LANE - n_ab)))
    wconv_full = jnp.moveaxis(wg_conv, 0, 1).reshape(3, 3 * D)
    c_all = c_all.reshape(NDEV, D)

    cc = jnp.concatenate([c, c_ctx.reshape(1, D), jnp.zeros((6, D), F32)], axis=0)
    mods = _modulation(cc, wg_mod, b_mod)
    xa = jnp.concatenate([ctx[0], x[0]], axis=0)
    h = _prenorm(xa, mods, g_pre, lc)
    p_qkv = _matmul_nn(h, w_qkv, "in_proj_qkv")
    pab = _matmul_nn(h, w_ab, "in_proj_ab")
    abt = jnp.swapaxes(pab[:, :n_ab].reshape(nch, CH, n_ab), 1, 2)
    alog16, dtb16 = a_log.reshape(1, 2 * NH), dt_bias.reshape(1, 2 * NH)
    alog_r = jnp.pad(alog16, ((0, 0), (0, LANE - 2 * NH)))
    dtb_r = jnp.pad(dtb16, ((0, 0), (0, LANE - 2 * NH)))
    alog_c = jnp.pad(alog16.reshape(2 * NH, 1), ((0, 2 * NH), (0, 0)))
    dtb_c = jnp.pad(dtb16.reshape(2 * NH, 1), ((0, 2 * NH), (0, 0)))
    qkv = _qkv_fwd(p_qkv, wconv_full, lc)
    late = [w_in_bf, w_pa[0].astype(_BF), w_pb[0].astype(_BF), w_out[0].astype(_BF)]
    xc_late = _Exchange(zip(late, ["gather_hi", "gather", "gather", "gather"]), split)
    o_f, o_b, s_f, s_b, t_f, t_b, wg_hi, wg_pa, wg_pb, wg_out = _gdn_fwd(
        qkv, pab, abt, alog_r, dtb_r, alog_c, dtb_c, lc, xc_late, late)
    w_rest = jnp.concatenate([wg_lo[jb][:, o2:wsh]] + [wg_hi[j][:, :wsh] for j in range(split, NDEV)], axis=1)
    wf_pa, wf_pb, wf_out = wg_pa.reshape(D, D), wg_pb.reshape(D, D), wg_out.reshape(D, D)
    p_rest = _matmul_nn(h, w_rest, "in_proj_rest")

    w_spt = jnp.swapaxes(w_sp[0], 1, 2)
    b_spb = jnp.broadcast_to(b_sp[0][:, :, None], (NH, GC, GC))
    gate_x = mods[0:1, 2 * D:]
    dp_rest, do, dy, ya, yb, mg, d_a, d_b, dout, dwsp, dbsp_l, pvec = _post(
        p_rest, o_f, o_b, x[0], loss_target[0], wf_pa, wf_pb, wf_out, w_sp[0], w_spt, b_spb, gm_ln_g, gm_ln_b,
        g_onorm, g_post, gate_x, lc)

    dw_rest = _matmul_tn(h, dp_rest, "dw_in_rest")
    o3 = wsh - o2
    chunks_hi = widen(jnp.moveaxis(dw_rest[:, o3:].reshape(D, NDEV - split, wsh), 1, 0))
    dw_pa = _matmul_tn(ya, d_a, "dw_pa").reshape(NDEV, D // NDEV, D)
    dw_pb = _matmul_tn(yb, d_b, "dw_pb").reshape(NDEV, D // NDEV, D)
    dw_out = _matmul_tn(mg, dout, "dw_out").reshape(NDEV, D // NDEV, D)
    small_a, lay_a = _pack([
        ("g_post", pvec[1]), ("g_onorm", pvec[4, :DH]), ("gm_ln_g", pvec[2]), ("gm_ln_b", pvec[3]), ("w_sp", dwsp),
        ("b_sp", jnp.sum(dbsp_l, axis=-1)), ("loss", pvec[5]), ("dgate", pvec[0])])
    early = [chunks_hi, dw_pa, dw_pb, dw_out, small_a]
    xc_early = _Exchange(zip(early, ["scatter_hi", "scatter", "scatter", "scatter", "gather"]), split)

    dqkv_f, dqkv_b, dcol_f, dcol_b, drow_f, drow_b, gvec_c, gvec_r, r_in, r_pa, r_pb, r_out, small_a_all = _gdn_bwd(
        qkv, pab, abt, alog_r, dtb_r, alog_c, dtb_c, s_f, s_b, t_f, t_b, do, lc, xc_early, early)
    dp_qkv, dwconv = _qkv_bwd(p_qkv, wconv_full, dqkv_f, dqkv_b, lc)
    drow = jnp.swapaxes(drow_f + drow_b, 1, 2).reshape(lt, n_ab)
    dpab = (dcol_f + dcol_b + jnp.pad(drow, ((0, 0), (0, LANE - n_ab)))).astype(_BF)

    dw_qkv = _matmul_tn(h, dp_qkv, "dw_in_qkv")
    dw_ab = _matmul_tn(h, dpab, "dw_in_ab")
    dw_lo = jnp.concatenate([dw_qkv, dw_ab[:, :n_ab], dw_rest[:, :o3]], axis=1)
    chunks_lo = widen(jnp.moveaxis(dw_lo.reshape(D, split, wsh), 1, 0))
    (theirs,) = _exchange([chunks_lo], ["sibling"], "pair_swap")
    chip_lo = _pair_sum(chunks_lo, theirs[0], "pair_sum")
    xc_last = _Exchange([(chip_lo, "scatter_par_lo")], split)
    dh, r_in = _dh_matmul(dp_rest, dp_qkv, dpab, w_rest, w_qkv, w_ab, xc_last, [chip_lo], {0: r_in})
    grad_x, nvec = _prenorm_bwd(xa, dh, dy, mods, g_pre, lc)

    dalog = gvec_c[0, :2 * NH] + gvec_r[:2 * NH, 0]
    ddtb = gvec_c[1, :2 * NH] + gvec_r[:2 * NH, 1]
    small_b, lay_b = _pack([
        ("g_pre", nvec[4]), ("a_log", dalog), ("dt_bias", ddtb), ("w_conv", dwconv),
        ("dshift", nvec[0]), ("dscale", nvec[1]), ("dshift_c", nvec[2]), ("dscale_c", nvec[3])])
    (small_b_all,) = _exchange([small_b], ["gather"], "gather_small")
    tot = _unpack(_sum_parts(small_a_all, "sum_small_a"), lay_a)
    tot.update(_unpack(_sum_parts(small_b_all, "sum_small_b"), lay_b))

    def per_device(packed_all, layout, name):
        at, r = [(a_, r_) for nm, a_, r_, _ in layout if nm == name][0]
        return packed_all[:, at:at + r].reshape(NDEV, -1)

    dmx_all = jnp.concatenate([per_device(small_b_all, lay_b, "dshift"), per_device(small_b_all, lay_b, "dscale"),
                               per_device(small_a_all, lay_a, "dgate")], axis=1)
    dmc_all = jnp.concatenate([per_device(small_b_all, lay_b, "dshift_c"), per_device(small_b_all, lay_b, "dscale_c"),
                               jnp.zeros((NDEV, D), F32)], axis=1)
    g_wmod, g_cctx, g_bmod = _mod_bwd(c_all, c_ctx.reshape(1, D), dmx_all, dmc_all, wg_mod)
    loss = 0.5 / D * jnp.sum(tot["loss"])
    ws_conv = w_conv.shape[2]
    g_wconv = lax.dynamic_slice_in_dim(tot["w_conv"], me * ws_conv, ws_conv, axis=1)

    small_names = ["c_ctx", "b_mod", "g_pre", "g_post", "a_log", "dt_bias", "g_onorm", "gm_ln_g", "gm_ln_b",
                   "w_sp", "b_sp", "w_conv"]
    wts = dict(c_ctx=c_ctx, b_mod=b_mod, g_pre=g_pre, g_post=g_post, a_log=a_log, dt_bias=dt_bias, g_onorm=g_onorm,
               gm_ln_g=gm_ln_g, gm_ln_b=gm_ln_b, w_sp=w_sp, b_sp=b_sp, w_conv=w_conv)
    ms = dict(c_ctx=m_c_ctx, b_mod=m_b_mod, g_pre=m_g_pre, g_post=m_g_post, a_log=m_a_log, dt_bias=m_dt_bias,
              g_onorm=m_g_onorm, gm_ln_g=m_gm_ln_g, gm_ln_b=m_gm_ln_b, w_sp=m_w_sp, b_sp=m_b_sp, w_conv=m_w_conv)
    vs = dict(c_ctx=v_c_ctx, b_mod=v_b_mod, g_pre=v_g_pre, g_post=v_g_post, a_log=v_a_log, dt_bias=v_dt_bias,
              g_onorm=v_g_onorm, gm_ln_g=v_gm_ln_g, gm_ln_b=v_gm_ln_b, w_sp=v_w_sp, b_sp=v_b_sp, w_conv=v_w_conv)
    gs = dict(tot)
    gs.update(c_ctx=g_cctx, b_mod=g_bmod, w_conv=g_wconv)
    gpk, play = _pack([(nm, gs[nm].reshape(wts[nm].shape)) for nm in small_names])
    wpk, _ = _pack([(nm, wts[nm]) for nm in small_names])
    mpk, _ = _pack([(nm, ms[nm]) for nm in small_names])
    vpk, _ = _pack([(nm, vs[nm]) for nm in small_names])
    res_small = [_unpack(a, play) for a in _adamw(gpk[None], wpk, mpk, vpk, "adamw_small")]
    res_big = {
        "w_mod": _adamw(g_wmod[None], w_mod[0], m_w_mod[0], v_w_mod[0], "adamw_w_mod"),
        "w_in": [a[:, :wsh] for a in _adamw(r_in, widen(w_in[0]), widen(m_w_in[0]), widen(v_w_in[0]), "adamw_w_in",
                                            chip_sums_below=split)],
        "w_pa": _adamw(r_pa, w_pa[0], m_w_pa[0], v_w_pa[0], "adamw_w_pa"),
        "w_pb": _adamw(r_pb, w_pb[0], m_w_pb[0], v_w_pb[0], "adamw_w_pb"),
        "w_out": _adamw(r_out, w_out[0], m_w_out[0], v_w_out[0], "adamw_w_out"),
    }
    order = ["c_ctx", "w_mod", "b_mod", "g_pre", "g_post", "w_in", "w_conv", "a_log", "dt_bias", "g_onorm",
             "gm_ln_g", "gm_ln_b", "w_sp", "b_sp", "w_pa", "w_pb", "w_out"]
    outs = [loss, grad_x[None]]
    for k in range(4):
        for nm in order:
            if nm in res_big:
                outs.append(res_big[nm][k][None])
            else:
                outs.append(res_small[k][nm])
    return tuple(outs)
```

```python
import functools

import jax
import jax.numpy as jnp
from jax import lax
from jax.experimental import pallas as pl
from jax.experimental.pallas import tpu as pltpu

F32 = jnp.float32
_BF = jnp.bfloat16
_HI = lax.Precision.HIGHEST
D = 1024
NH = 8
DH = 128
CH = 64
GC = 128
NREST = 6 * D
NMAIN = NREST + 3 * D
EPS = 1e-6
LANE = 128
NDEV = 8
VMEM_LIMIT = 56 * 1024 * 1024
MESH = pl.DeviceIdType.MESH

ADAM_LR, ADAM_B1, ADAM_B2, ADAM_EPS, ADAM_WD, ADAM_STEP = 0.001, 0.9, 0.999, 1e-08, 0.01, 10

NN = ((1,), (0,))
NT = ((1,), (1,))
TN = ((0,), (0,))


def _dot(a, b, dims=NN, prec=None):
    return lax.dot_general(a, b, (dims, ((), ())), precision=prec, preferred_element_type=F32)


def _mm(a, b, dims=NN):
    return _dot(a.astype(_BF), b.astype(_BF), dims)


def _mmh(a, b, dims=NN):
    return _dot(a.astype(F32), b.astype(F32), dims, _HI)


def _split(a):
    hi = a.astype(_BF)
    return hi, (a - hi.astype(F32)).astype(_BF)


def _mm3(a, b, dims=NN):
    ah, al = _split(a)
    bh, bl = _split(b)
    return _dot(ah, bh, dims) + (_dot(ah, bl, dims) + _dot(al, bh, dims))


def _sigmoid(x):
    return 1.0 / (1.0 + jnp.exp(-x))


def _silu_g(x):
    s = _sigmoid(x)
    return x * s, s * (1.0 + x * (1.0 - s))


def _gelu_g(x):
    c = 0.7978845608028654
    t = jnp.tanh(c * (x + 0.044715 * (x * x * x)))
    cdf = 0.5 * (1.0 + t)
    return x * cdf, cdf + 0.5 * x * (1.0 - t * t) * c * (1.0 + 3 * 0.044715 * x * x)


def _softplus(x):
    return jnp.maximum(x, 0.0) + jnp.log(1.0 + jnp.exp(-jnp.abs(x)))


def _params(sem=None):
    return pltpu.CompilerParams(dimension_semantics=sem, vmem_limit_bytes=VMEM_LIMIT)


def _tile(n, pref):
    for t in pref:
        if n % t == 0:
            return t
    return n


def _full(shape):
    nd = len(shape)
    return pl.BlockSpec(shape, lambda *_: (0,) * nd)


def _sds(shape, dt=F32):
    return jax.ShapeDtypeStruct(shape, dt)


MAX_PIECES = 12
PIECE_BYTES = 256 * 1024


def _piece_slices(shape, itemsize):
    total = itemsize
    for d in shape:
        total *= d
    want = min(MAX_PIECES, total // PIECE_BYTES)
    lead = shape[0] if len(shape) >= 3 else 1
    rows = shape[-2] if len(shape) >= 2 else 1
    if want < 2 or lead > want:
        return [()]
    m = max([n for n in (8, 4, 2, 1) if n * lead <= want and rows % (16 * n) == 0], default=1)
    if m * lead < 2:
        return [()]
    rs = rows // m
    mid = (slice(None),) * max(len(shape) - 3, 0)
    if len(shape) >= 3:
        return [(i,) + mid + (pl.ds(j * rs, rs),) for i in range(lead) for j in range(m)]
    return [(pl.ds(j * rs, rs),) for j in range(m)]


class _Pieces:
    def __init__(self, copies):
        self.copies = copies

    def start(self):
        for cp in self.copies:
            cp.start()

    def wait_send(self):
        for cp in self.copies:
            cp.wait_send()

    def wait_recv(self):
        for cp in self.copies:
            cp.wait_recv()

    def wait(self):
        for cp in self.copies:
            cp.wait()


class _Exchange:
    def __init__(self, specs, split):
        self.specs = list(specs)
        self.split = split
        self.n = len(self.specs)
        def out(a, k):
            if k == "sibling":
                return (1,) + tuple(a.shape)
            return (NDEV,) + (tuple(a.shape) if k.startswith("gather") else tuple(a.shape[1:]))

        self.out_shape = tuple(_sds(out(a, k), a.dtype) for a, k in self.specs)
        self.pieces = [_piece_slices(o.shape[1:], jnp.dtype(o.dtype).itemsize) for o in self.out_shape]
        self.sem_base = [(NDEV - 1) * sum(len(p) for p in self.pieces[:a]) for a in range(self.n + 1)]
        self.scratch = [pltpu.SemaphoreType.DMA((self.sem_base[-1],)), pltpu.SemaphoreType.DMA((self.sem_base[-1],)),
                        pltpu.SemaphoreType.DMA((self.sem_base[-1] // (NDEV - 1),))]

    def _local(self, sems, a, src, dst):
        base = self.sem_base[a] // (NDEV - 1)
        return _Pieces([pltpu.make_async_copy(src.at[sl] if sl else src, dst.at[sl] if sl else dst, sems[2].at[base + p])
                        for p, sl in enumerate(self.pieces[a])])

    def _remote(self, sems, a, k, src, dst, to):
        send_sems, recv_sems, _ = sems
        base = self.sem_base[a] + k * len(self.pieces[a])
        return _Pieces([
            pltpu.make_async_remote_copy(
                src_ref=src.at[sl] if sl else src, dst_ref=dst.at[sl] if sl else dst, send_sem=send_sems.at[base + p],
                recv_sem=recv_sems.at[base + p], device_id=to, device_id_type=MESH)
            for p, sl in enumerate(self.pieces[a])])

    def _ok(self, kind, idx):
        if kind.endswith("_lo"):
            return idx < self.split
        if kind.endswith("_hi"):
            return idx >= self.split
        return True

    def _phases(self, ins, outs, sems):
        x, y, c = lax.axis_index("x"), lax.axis_index("y"), lax.axis_index("c")
        me = 4 * x + 2 * y + c
        sib = (x, y, 1 - c)
        sib_idx = 4 * x + 2 * y + (1 - c)
        chips = [(1 - x, y), (x, 1 - y), (1 - x, 1 - y)]
        starts, forwards, waits = [], [], []
        for a, (_, kind) in enumerate(self.specs):
            ok = functools.partial(self._ok, kind)
            if kind.startswith("gather"):
                def copy(k, block, to, src=None, a=a):
                    rows = outs[a].at[block]
                    return self._remote(sems, a, k, rows if src is None else src, rows, to)

                loc = self._local(sems, a, ins[a], outs[a].at[me])
                first = [copy(0, me, sib, ins[a])] + [copy(1 + j, me, (*chip, c), ins[a]) for j, chip in enumerate(chips)]
                starts += [(ok(me), loc.start)] + [(ok(me), cp.start) for cp in first]
                waits += [(ok(me), loc.wait)] + [(ok(me), cp.wait_send) for cp in first]
                for j, chip in enumerate(chips):
                    origin = 4 * chip[0] + 2 * chip[1] + c
                    passed = copy(4 + j, origin, sib)
                    forwards += [(ok(origin), copy(1 + j, origin, sib).wait_recv), (ok(origin), passed.start)]
                    waits.append((ok(origin), passed.wait_send))
                    other = 4 * chip[0] + 2 * chip[1] + (1 - c)
                    waits.append((ok(other), copy(4 + j, other, sib).wait_recv))
                waits.append((ok(sib_idx), copy(0, sib_idx, sib).wait_recv))
            elif kind == "sibling":
                swap = self._remote(sems, a, 0, ins[a], outs[a].at[0], sib)
                starts.append((True, swap.start))
                waits += [(True, swap.wait_send), (True, swap.wait_recv)]
            else:
                base = self.split if kind.endswith("_hi") else 0
                same_core_only = "_par" in kind

                def src(idx, a=a, base=base):
                    return ins[a].at[jnp.clip(idx - base, 0, ins[a].shape[0] - 1)]

                loc = self._local(sems, a, src(me), outs[a].at[me])
                starts.append((ok(me), loc.start))
                waits.append((ok(me), loc.wait))
                for k in range(1, NDEV):
                    if same_core_only and k & 1:
                        continue
                    px = 1 - x if (k >> 2) & 1 else x
                    py = 1 - y if (k >> 1) & 1 else y
                    pc = 1 - c if k & 1 else c
                    pidx = 4 * px + 2 * py + pc
                    send = self._remote(sems, a, k - 1, src(pidx), outs[a].at[me], (px, py, pc))
                    arrive = self._remote(sems, a, k - 1, src(pidx), outs[a].at[pidx], (px, py, pc))
                    starts.append((ok(pidx), send.start))
                    waits += [(ok(pidx), send.wait_send), (ok(me), arrive.wait_recv)]
        return starts, forwards, waits

    @staticmethod
    def _run(actions):
        for cond, fn in actions:
            if cond is True:
                fn()
            else:
                pl.when(cond)(fn)

    def start(self, ins, outs, sems):
        self._run(self._phases(ins, outs, sems)[0])

    def forward(self, ins, outs, sems):
        self._run(self._phases(ins, outs, sems)[1])

    def wait(self, ins, outs, sems):
        self._run(self._phases(ins, outs, sems)[2])


_ANY = pl.BlockSpec(memory_space=pl.ANY)


def _exchange(arrays, kinds, name, split=0, into=None):
    xc = _Exchange(zip(arrays, kinds), split)
    n = xc.n
    into = into or {}
    ni = len(into)

    def body(*refs):
        ins, outs, sems = refs[:n], refs[n + ni:2 * n + ni], refs[2 * n + ni:]
        xc.start(ins, outs, sems)
        xc.forward(ins, outs, sems)
        xc.wait(ins, outs, sems)

    return pl.pallas_call(
        body, name=name, out_shape=xc.out_shape, in_specs=[_ANY] * (n + ni), out_specs=tuple([_ANY] * n),
        scratch_shapes=xc.scratch, input_output_aliases={n + t: a for t, a in enumerate(into)},
    )(*arrays, *into.values())


def _matmul_nn(a, b, name):
    m, kk = a.shape
    n = b.shape[1]
    tm = _tile(m, (1088, 1024, 640, 512, 256, 128))
    tn = _tile(n, (512, 256, 128))

    def body(a_ref, b_ref, o_ref):
        o_ref[...] = _mm(a_ref[...], b_ref[...])

    return pl.pallas_call(
        body, name=name, out_shape=_sds((m, n)), grid=(n // tn, m // tm),
        in_specs=[pl.BlockSpec((tm, kk), lambda j, i: (i, 0)), pl.BlockSpec((kk, tn), lambda j, i: (0, j))],
        out_specs=pl.BlockSpec((tm, tn), lambda j, i: (i, j)),
        compiler_params=_params(("parallel", "parallel")),
    )(a, b)


def _matmul_tn(a, b, name):
    kk, m = a.shape
    n = b.shape[1]
    tk = _tile(kk, (1088, 1024, 640, 512, 256, 128))
    tn = _tile(n, (1024, 512, 256, 128))
    nk = kk // tk

    def body(a_ref, b_ref, o_ref, acc_ref):
        k = pl.program_id(1)

        @pl.when(k == 0)
        def _():
            acc_ref[...] = jnp.zeros_like(acc_ref)

        acc_ref[...] += _mm(a_ref[...], b_ref[...], TN)

        @pl.when(k == nk - 1)
        def _():
            o_ref[...] = acc_ref[...].astype(o_ref.dtype)

    return pl.pallas_call(
        body, name=name, out_shape=_sds((m, n), _BF), grid=(n // tn, nk),
        in_specs=[pl.BlockSpec((tk, m), lambda j, k: (k, 0)), pl.BlockSpec((tk, tn), lambda j, k: (k, j))],
        out_specs=pl.BlockSpec((m, tn), lambda j, k: (0, j)),
        scratch_shapes=[pltpu.VMEM((m, tn), F32)],
        compiler_params=_params(("parallel", "arbitrary")),
    )(a, b)


def _dh_matmul(dp_rest, dp_qkv, dpab, w_rest, w_qkv, w_ab, xc, xc_arrays, xc_into):
    lt = dp_rest.shape[0]
    tm = _tile(lt, (1088, 1024, 640, 512, 256, 128))
    nr, nq = dp_rest.shape[1] // D, dp_qkv.shape[1] // D
    nx, ni = xc.n, len(xc_into)
    ni_steps = lt // tm

    def body(*refs):
        dr_ref, dq_ref, ab_ref, wr_ref, wq_ref, wab_ref = refs[:6]
        x_in = refs[6:6 + nx]
        o_ref = refs[6 + nx + ni]
        x_out = refs[7 + nx + ni:7 + 2 * nx + ni]
        sems = refs[7 + 2 * nx + ni:]
        i = pl.program_id(0)
        k = pl.program_id(1)

        @pl.when((i == 0) & (k == 0))
        def _():
            xc.start(x_in, x_out, sems)

        @pl.when(k == 0)
        def _():
            o_ref[...] = _mm(ab_ref[...], wab_ref[...], NT)

        @pl.when(k < nr)
        def _():
            o_ref[...] += _mm(dr_ref[...], wr_ref[...], NT)

        @pl.when(k >= nr)
        def _():
            o_ref[...] += _mm(dq_ref[...], wq_ref[...], NT)

        @pl.when((i == ni_steps - 1) & (k == nr + nq - 1))
        def _():
            xc.wait(x_in, x_out, sems)

    rk = lambda k: jnp.minimum(k, nr - 1)
    qk = lambda k: jnp.maximum(k - nr, 0)
    return pl.pallas_call(
        body, name="dh_matmul", out_shape=(_sds((lt, D)),) + xc.out_shape, grid=(lt // tm, nr + nq),
        in_specs=[pl.BlockSpec((tm, D), lambda i, k: (i, rk(k))), pl.BlockSpec((tm, D), lambda i, k: (i, qk(k))),
                  pl.BlockSpec((tm, LANE), lambda i, k: (i, 0)),
                  pl.BlockSpec((D, D), lambda i, k: (0, rk(k))), pl.BlockSpec((D, D), lambda i, k: (0, qk(k))),
                  _full((D, LANE))] + [_ANY] * (nx + ni),
        out_specs=(pl.BlockSpec((tm, D), lambda i, k: (i, 0)),) + tuple([_ANY] * nx),
        scratch_shapes=xc.scratch, input_output_aliases={6 + nx + t: 1 + a for t, a in enumerate(xc_into)},
        compiler_params=_params(("arbitrary", "arbitrary")),
    )(dp_rest, dp_qkv, dpab, w_rest, w_qkv, w_ab, *xc_arrays, *xc_into.values())


def _modulation(cc, w_mod_g, b_mod):
    ws = w_mod_g.shape[2]

    def body(c_ref, w_ref, b_ref, o_ref):
        s, _ = _silu_g(c_ref[...])
        o_ref[...] = _mm(s, w_ref[0]) + b_ref[...]

    return pl.pallas_call(
        body, name="modulation", out_shape=_sds((8, 3 * D)), grid=(NDEV,),
        in_specs=[_full((8, D)), pl.BlockSpec((1, D, ws), lambda j: (j, 0, 0)), pl.BlockSpec((1, ws), lambda j: (0, j))],
        out_specs=pl.BlockSpec((8, ws), lambda j: (0, j)),
        compiler_params=_params(("parallel",)),
    )(cc, w_mod_g, b_mod)


def _prenorm(xa, mods, g_pre, lc):
    lt = xa.shape[0]
    tm = _tile(lc, (256, 128))
    nct = lc // tm

    def body(x_ref, m_ref, g_ref, o_ref):
        x = x_ref[...]
        is_ctx = pl.program_id(0) < nct
        shift = jnp.where(is_ctx, m_ref[1:2, 0:D], m_ref[0:1, 0:D])
        scale = jnp.where(is_ctx, m_ref[1:2, D:2 * D], m_ref[0:1, D:2 * D])
        r = lax.rsqrt(jnp.mean(x * x, axis=-1, keepdims=True) + EPS)
        o_ref[...] = ((x * r * g_ref[...]) * (1.0 + scale) + shift).astype(o_ref.dtype)

    return pl.pallas_call(
        body, name="prenorm", out_shape=_sds((lt, D), _BF), grid=(lt // tm,),
        in_specs=[pl.BlockSpec((tm, D), lambda i: (i, 0)), _full((8, 3 * D)), _full((1, D))],
        out_specs=pl.BlockSpec((tm, D), lambda i: (i, 0)),
        compiler_params=_params(("parallel",)),
    )(xa, mods, g_pre)


def _prenorm_bwd(xa, dh, dy, mods, g_pre, lc):
    lt = xa.shape[0]
    tm = _tile(lc, (256, 128))
    nct = lc // tm
    nl = (lt - lc) // tm

    def body(x_ref, dh_ref, dy_ref, m_ref, g_ref, gx_ref, vec_ref):
        i = pl.program_id(0)

        @pl.when(i == 0)
        def _():
            vec_ref[...] = jnp.zeros_like(vec_ref)

        x = x_ref[...]
        dh = dh_ref[...]
        g = g_ref[...]
        is_ctx = i < nct
        scale = jnp.where(is_ctx, m_ref[1:2, D:2 * D], m_ref[0:1, D:2 * D])
        r = lax.rsqrt(jnp.mean(x * x, axis=-1, keepdims=True) + EPS)
        n = x * r
        hn = n * g
        dsh = jnp.sum(dh, axis=0, keepdims=True)
        dsc = jnp.sum(dh * hn, axis=0, keepdims=True)
        dhn = dh * (1.0 + scale)
        vec_ref[4:5, :] += jnp.sum(dhn * n, axis=0, keepdims=True)
        dn = dhn * g
        dx = r * (dn - n * jnp.mean(dn * n, axis=-1, keepdims=True))

        @pl.when(is_ctx)
        def _():
            vec_ref[2:3, :] += dsh
            vec_ref[3:4, :] += dsc

        @pl.when(jnp.logical_not(is_ctx))
        def _():
            vec_ref[0:1, :] += dsh
            vec_ref[1:2, :] += dsc
            gx_ref[...] = dy_ref[...] + dx

    xrow = lambda i: (jnp.maximum(i - nct, 0), 0)
    return pl.pallas_call(
        body, name="prenorm_bwd", out_shape=(_sds((nl * tm, D)), _sds((8, D))), grid=(lt // tm,),
        in_specs=[pl.BlockSpec((tm, D), lambda i: (i, 0)), pl.BlockSpec((tm, D), lambda i: (i, 0)),
                  pl.BlockSpec((tm, D), xrow), _full((8, 3 * D)), _full((1, D))],
        out_specs=(pl.BlockSpec((tm, D), xrow), _full((8, D))),
        compiler_params=_params(("arbitrary",)),
    )(xa, dh, dy, mods, g_pre)


def _conv_parts(x, w, lc):
    lt = x.shape[0]
    row = lax.broadcasted_iota(jnp.int32, x.shape, 0)
    first = (row == 0) | (row == lc)
    last = (row == lc - 1) | (row == lt - 1)
    xp = jnp.where(first, 0.0, pltpu.roll(x, 1, 0))
    xn = jnp.where(last, 0.0, pltpu.roll(x, lt - 1, 0))
    y = w[0:1, :] * xp + w[1:2, :] * x + w[2:3, :] * xn
    return xp, xn, y, first, last


def _qkv_fwd(p, w_conv, lc):
    lt = p.shape[0]

    def body(p_ref, w_ref, o_ref):
        _, _, y, _, _ = _conv_parts(p_ref[...], w_ref[...], lc)
        s, _ = _silu_g(y)
        rs = lax.rsqrt(jnp.sum(s * s, axis=-1, keepdims=True) + EPS)
        o_ref[...] = s * jnp.where(pl.program_id(0) < 2 * NH, rs, 1.0)

    return pl.pallas_call(
        body, name="qkv_fwd", out_shape=_sds((lt, 3 * D)), grid=(3 * NH,),
        in_specs=[pl.BlockSpec((lt, DH), lambda j: (0, j)), pl.BlockSpec((3, DH), lambda j: (0, j))],
        out_specs=pl.BlockSpec((lt, DH), lambda j: (0, j)),
        compiler_params=_params(("parallel",)),
    )(p, w_conv)


def _qkv_bwd(p, w_conv, dqkv_f, dqkv_b, lc):
    lt = p.shape[0]

    def body(p_ref, w_ref, df_ref, db_ref, dp_ref, dw_ref):
        w = w_ref[...]
        xp, xn, y, first, last = _conv_parts(p_ref[...], w, lc)
        s, ds_dy = _silu_g(y)
        dn = df_ref[...] + db_ref[...]
        rs = lax.rsqrt(jnp.sum(s * s, axis=-1, keepdims=True) + EPS)
        nrm = s * rs
        ds_n = rs * (dn - nrm * jnp.sum(dn * nrm, axis=-1, keepdims=True))
        ds = jnp.where(pl.program_id(0) < 2 * NH, ds_n, dn)
        dy = ds * ds_dy
        dw_ref[0:1, :] = jnp.sum(dy * xp, axis=0, keepdims=True)
        dw_ref[1:2, :] = jnp.sum(dy * p_ref[...], axis=0, keepdims=True)
        dw_ref[2:3, :] = jnp.sum(dy * xn, axis=0, keepdims=True)
        dyn = jnp.where(last, 0.0, pltpu.roll(dy, lt - 1, 0))
        dyp = jnp.where(first, 0.0, pltpu.roll(dy, 1, 0))
        dp_ref[...] = (w[1:2, :] * dy + w[0:1, :] * dyn + w[2:3, :] * dyp).astype(dp_ref.dtype)

    return pl.pallas_call(
        body, name="qkv_bwd", out_shape=(_sds((lt, 3 * D), _BF), _sds((3, 3 * D))), grid=(3 * NH,),
        in_specs=[pl.BlockSpec((lt, DH), lambda j: (0, j)), pl.BlockSpec((3, DH), lambda j: (0, j)),
                  pl.BlockSpec((lt, DH), lambda j: (0, j)), pl.BlockSpec((lt, DH), lambda j: (0, j))],
        out_specs=(pl.BlockSpec((lt, DH), lambda j: (0, j)), pl.BlockSpec((3, DH), lambda j: (0, j))),
        compiler_params=_params(("parallel",)),
    )(p, w_conv, dqkv_f, dqkv_b)


def _masks(d):
    ri = lax.broadcasted_iota(jnp.int32, (CH, CH), 0)
    ci = lax.broadcasted_iota(jnp.int32, (CH, CH), 1)
    incl = (ri >= ci) if d == 0 else (ri <= ci)
    strict = (ri > ci) if d == 0 else (ri < ci)
    incl_t = (ri <= ci) if d == 0 else (ri >= ci)
    return incl, strict, incl_t, ri == ci


def _decays(d, ab, abt, alog_r, dtb_r, alog_c, dtb_c, incl, incl_t):
    g_full = -jnp.exp(alog_r) * _softplus(ab + dtb_r)
    beta_full = _sigmoid(ab)
    gc_full = _mmh(incl.astype(F32), g_full)
    gl_full = jnp.sum(g_full, axis=0, keepdims=True)
    gt_full = -jnp.exp(alog_c) * _softplus(abt + dtb_c)
    gct = _mmh(gt_full, incl_t.astype(F32))
    return g_full, beta_full, gc_full, gl_full, gt_full, gct


def _lane_onehot(idx, n=LANE):
    return (lax.broadcasted_iota(jnp.int32, (1, n), 1) == idx).astype(F32)


def _head_scalars(d, h, beta_full, gc_full, gl_full, gct):
    idx = d * NH + h
    oh = _lane_onehot(idx)
    gcol = jnp.sum(gc_full * oh, axis=-1, keepdims=True)
    bcol = jnp.sum(beta_full * _lane_onehot(2 * NH + idx), axis=-1, keepdims=True)
    gl = jnp.sum(gl_full * oh, axis=-1, keepdims=True)
    grow = gct[idx:idx + 1, :]
    return gcol, grow, bcol, gl


def _lockstep(gens):
    live = list(gens)
    while live:
        nxt = []
        for g in live:
            try:
                next(g)
                nxt.append(g)
            except StopIteration:
                pass
        live = nxt


def _chunk_local(qh, kh, vh, gcol, grow, bcol, gl, incl, strict):
    decay = jnp.where(incl, jnp.exp(gcol - grow), 0.0)
    kb = kh * bcol
    a = jnp.where(strict, _mm(kb, kh, NT) * decay, 0.0)
    egc = jnp.exp(gcol)
    rhs_u = vh * bcol
    rhs_w = kb * egc
    qs = qh * (DH ** -0.5)
    attn = jnp.where(incl, _mm(qs, kh, NT) * decay, 0.0)
    etail = jnp.exp(gl - gcol)
    return decay, kb, a, egc, rhs_u, rhs_w, qs, attn, etail


def _scan_specs(lt, lc, bwd_pass):
    nch = lt // CH
    ncc = lc // CH
    if not bwd_pass:
        cf = lambda s: s
        cb = lambda s: jnp.where(s < ncc, ncc - 1 - s, nch + ncc - 1 - s)
    else:
        cf = lambda s: nch - 1 - s
        cb = lambda s: jnp.where(s < nch - ncc, ncc + s, s - (nch - ncc))
    return nch, cf, cb


def _gdn_fwd(qkv, pab, abt, alog_r, dtb_r, alog_c, dtb_c, lc, xc, xc_arrays):
    lt = qkv.shape[0]
    nch, cf, cb = _scan_specs(lt, lc, False)
    nx = xc.n

    def body(*refs):
        qf, kf, vf, abf, abtf, qb, kb_, vb, abb, abtb, ar, dr, ac, dc = refs[:14]
        x_in = refs[14:14 + nx]
        of_ref, ob_ref, sf_ref, sb_ref, tf_ref, tb_ref = refs[14 + nx:20 + nx]
        x_out = refs[20 + nx:20 + 2 * nx]
        s_scr = refs[20 + 2 * nx]
        sems = refs[21 + 2 * nx:]

        @pl.when(pl.program_id(0) == 0)
        def _():
            s_scr[...] = jnp.zeros_like(s_scr)
            xc.start(x_in, x_out, sems)

        def chain(d, h, q_r, k_r, v_r, o_ref, sh_ref, th_ref, masks, decs):
            incl, strict, _, eye = masks
            sl = slice(h * DH, (h + 1) * DH)
            qh, kh, vh = q_r[:, sl], k_r[:, sl], v_r[:, sl]
            gcol, grow, bcol, gl = _head_scalars(d, h, *decs)
            _, _, a, egc, rhs_u, rhs_w, qs, attn, etail = _chunk_local(qh, kh, vh, gcol, grow, bcol, gl, incl, strict)
            yield
            n = -a
            t = jnp.where(eye, 1.0, 0.0) + n
            p = _mm3(n, n)
            yield
            for _ in range(4):
                r = _mm3(jnp.concatenate([t, p], axis=0), p)
                yield
                t = t + r[:CH]
                p = r[CH:]
            t = t + _mm3(t, p)
            yield
            sol = _mm3(t, jnp.concatenate([rhs_u, rhs_w], axis=1))
            u, w = sol[:, :DH], sol[:, DH:]
            s = s_scr[d, h]
            sh_ref[0, h] = s
            th_ref[0, h] = t
            yield
            ws = _mm(jnp.concatenate([w, qs * egc], axis=0), s)
            yield
            v_new = u - ws[:CH]
            o_ref[:, sl] = ws[CH:] + _mm(attn, v_new)
            s_scr[d, h] = s * jnp.exp(gl) + _mm(kh * etail, v_new, TN)

        chains = []
        for d, (q_r, k_r, v_r, ab_r, abt_r, o_ref, sh_ref, th_ref) in enumerate(
                ((qf, kf, vf, abf, abtf, of_ref, sf_ref, tf_ref), (qb, kb_, vb, abb, abtb, ob_ref, sb_ref, tb_ref))):
            masks = _masks(d)
            _, beta_full, gc_full, gl_full, _, gct = _decays(
                d, ab_r[...], abt_r[0], ar[...], dr[...], ac[...], dc[...], masks[0], masks[2])
            for h in range(NH):
                chains.append(chain(d, h, q_r, k_r, v_r, o_ref, sh_ref, th_ref, masks, (beta_full, gc_full, gl_full, gct)))
        _lockstep(chains)

        @pl.when(pl.program_id(0) == nch // 2)
        def _():
            xc.forward(x_in, x_out, sems)

        @pl.when(pl.program_id(0) == nch - 1)
        def _():
            xc.wait(x_in, x_out, sems)

    def row(c, col):
        return pl.BlockSpec((CH, D), lambda s: (c(s), col))

    def chunk_in(c):
        return [row(c, 0), row(c, 1), row(c, 2), pl.BlockSpec((CH, LANE), lambda s: (c(s), 0)),
                pl.BlockSpec((1, 4 * NH, CH), lambda s: (c(s), 0, 0))]

    def hist(c, n):
        return pl.BlockSpec((1, NH, n, n), lambda s: (c(s), 0, 0, 0))

    small = [_full((1, LANE)), _full((1, LANE)), _full((4 * NH, 1)), _full((4 * NH, 1))]
    return pl.pallas_call(
        body, name="gdn_fwd", grid=(nch,),
        out_shape=(_sds((lt, D)), _sds((lt, D)), _sds((nch, NH, DH, DH)), _sds((nch, NH, DH, DH)),
                   _sds((nch, NH, CH, CH)), _sds((nch, NH, CH, CH))) + xc.out_shape,
        in_specs=chunk_in(cf) + chunk_in(cb) + small + [_ANY] * nx,
        out_specs=(pl.BlockSpec((CH, D), lambda s: (cf(s), 0)), pl.BlockSpec((CH, D), lambda s: (cb(s), 0)),
                   hist(cf, DH), hist(cb, DH), hist(cf, CH), hist(cb, CH)) + tuple([_ANY] * nx),
        scratch_shapes=[pltpu.VMEM((2, NH, DH, DH), F32)] + xc.scratch,
        compiler_params=_params(("arbitrary",)),
    )(qkv, qkv, qkv, pab, abt, qkv, qkv, qkv, pab, abt, alog_r, dtb_r, alog_c, dtb_c, *xc_arrays)


def _gdn_bwd(qkv, pab, abt, alog_r, dtb_r, alog_c, dtb_c, s_f, s_b, t_f, t_b, do, lc, xc, xc_arrays):
    lt = qkv.shape[0]
    nch, cf, cb = _scan_specs(lt, lc, True)
    nx = xc.n

    def body(*refs):
        qf, kf, vf, abf, abtf, sf_ref, tf_ref, dof, qb, kb_, vb, abb, abtb, sb_ref, tb_ref, dob, ar, dr, ac, dc = refs[:20]
        x_in = refs[20:20 + nx]
        dqf_ref, dqb_ref, dcf_ref, dcb_ref, drf_ref, drb_ref, vcol_ref, vrow_ref = refs[20 + nx:28 + nx]
        x_out = refs[28 + nx:28 + 2 * nx]
        ds_scr = refs[28 + 2 * nx]
        sems = refs[29 + 2 * nx:]

        @pl.when(pl.program_id(0) == 0)
        def _():
            ds_scr[...] = jnp.zeros_like(ds_scr)
            vcol_ref[...] = jnp.zeros_like(vcol_ref)
            vrow_ref[...] = jnp.zeros_like(vrow_ref)
            xc.start(x_in, x_out, sems)

        alog_r_, dtb_r_, alog_c_, dtb_c_ = ar[...], dr[...], ac[...], dc[...]
        lane2 = lax.broadcasted_iota(jnp.int32, (1, LANE), 1)
        acc = [[], []]

        def chain(d, h, q_r, k_r, v_r, sh_ref, th_ref, do_r, dq_ref, masks, decs):
            incl, strict, _, _ = masks
            idx = d * NH + h
            sl = slice(h * DH, (h + 1) * DH)
            qh, kh, vh = q_r[:, sl], k_r[:, sl], v_r[:, sl]
            doh = do_r[:, sl]
            gcol, grow, bcol, gl = _head_scalars(d, h, *decs)
            decay, kb, a, egc, rhs_u, rhs_w, qs, attn, etail = _chunk_local(qh, kh, vh, gcol, grow, bcol, gl, incl, strict)
            t = th_ref[0, h]
            s = sh_ref[0, h]
            ds_new = ds_scr[d, h]
            sol = _mm3(t, jnp.concatenate([rhs_u, rhs_w], axis=1))
            u, w = sol[:, :DH], sol[:, DH:]
            q_dec = qs * egc
            k_tail = kh * etail
            egl = jnp.exp(gl)
            dv_new = _mm(attn, doh, TN) + _mm(k_tail, ds_new)
            dq_dec = _mm(doh, s, NT)
            dgl = jnp.sum(jnp.sum(ds_new * s, axis=0, keepdims=True), axis=-1, keepdims=True) * egl
            yield
            v_new = u - _mm(w, s)
            dw = -_mm(dv_new, s, NT)
            ds_scr[d, h] = ds_new * egl + _mm(q_dec, doh, TN) - _mm(w, dv_new, TN)
            yield
            dattn = jnp.where(incl, _mm(doh, v_new, NT), 0.0)
            dk_tail = _mm(v_new, ds_new, NT)
            dr = _mm3(t, jnp.concatenate([dv_new, dw], axis=1), TN)
            dr_u, dr_w = dr[:, :DH], dr[:, DH:]
            yield
            da = -jnp.where(strict, _mm3(dr, sol, NT), 0.0)
            nq = dattn * decay
            dqs = _mm(nq, kh) + dq_dec * egc
            dk = _mm(nq, qs, TN)
            yield
            dv = dr_u * bcol
            dbeta = jnp.sum(dr_u * vh, axis=-1, keepdims=True)
            dgc = jnp.sum(dr_w * rhs_w, axis=-1, keepdims=True)
            m = da * decay
            dkb = dr_w * egc + _mm(m, kh)
            dk = dk + _mm(m, kb, TN)
            pq = da * a + dattn * attn
            dgc = dgc + jnp.sum(pq, axis=-1, keepdims=True) + jnp.sum(dq_dec * q_dec, axis=-1, keepdims=True)
            dgr = -jnp.sum(pq, axis=0, keepdims=True)
            tt = jnp.sum(dk_tail * k_tail, axis=-1, keepdims=True)
            dk = dk + dk_tail * etail + dkb * bcol
            dgc = dgc - tt
            dgl = dgl + jnp.sum(tt, axis=0, keepdims=True)
            dbeta = dbeta + jnp.sum(dkb * kh, axis=-1, keepdims=True)
            dq_ref[:, sl] = dqs * (DH ** -0.5)
            dq_ref[:, D + h * DH:D + (h + 1) * DH] = dk
            dq_ref[:, 2 * D + h * DH:2 * D + (h + 1) * DH] = dv
            acc[d].append((idx, dgc, dgl, dbeta, dgr))

        dirs = ((qf, kf, vf, abf, abtf, sf_ref, tf_ref, dof, dqf_ref, dcf_ref, drf_ref),
                (qb, kb_, vb, abb, abtb, sb_ref, tb_ref, dob, dqb_ref, dcb_ref, drb_ref))
        chains, ctx_d = [], []
        for d, (q_r, k_r, v_r, ab_r, abt_r, sh_ref, th_ref, do_r, dq_ref, _, _) in enumerate(dirs):
            masks = _masks(d)
            ab, abt = ab_r[...], abt_r[0]
            g_full, beta_full, gc_full, gl_full, gt_full, gct = _decays(
                d, ab, abt, alog_r_, dtb_r_, alog_c_, dtb_c_, masks[0], masks[2])
            ctx_d.append((masks, ab, abt, g_full, beta_full, gt_full))
            for h in range(NH):
                chains.append(chain(d, h, q_r, k_r, v_r, sh_ref, th_ref, do_r, dq_ref, masks,
                                    (beta_full, gc_full, gl_full, gct)))
        _lockstep(chains)
        for d in range(2):
            (incl, _, incl_t, _), ab, abt, g_full, beta_full, gt_full = ctx_d[d]
            dcol_ref, drow_ref = dirs[d][9], dirs[d][10]
            dgc_col = jnp.zeros((CH, LANE), F32)
            dgl_row = jnp.zeros((1, LANE), F32)
            dbeta_col = jnp.zeros((CH, LANE), F32)
            dgc_row = jnp.zeros((4 * NH, CH), F32)
            for idx, dgc, dgl, dbeta, dgr in acc[d]:
                oh = _lane_onehot(idx)
                dgc_col = dgc_col + dgc * oh
                dgl_row = dgl_row + dgl * oh
                dbeta_col = dbeta_col + dbeta * _lane_onehot(2 * NH + idx)
                ohc = (lax.broadcasted_iota(jnp.int32, (4 * NH, 1), 0) == idx).astype(F32)
                dgc_row = dgc_row + ohc * dgr
            dg_col = _mmh(incl_t.astype(F32), dgc_col) + dgl_row
            dg_row = _mmh(dgc_row, incl.astype(F32))
            sg_col = _sigmoid(ab + dtb_r_)
            da_col = dg_col * (-jnp.exp(alog_r_)) * sg_col
            dcol_ref[...] = da_col + dbeta_col * beta_full * (1.0 - beta_full)
            da_row = dg_row * (-jnp.exp(alog_c_)) * _sigmoid(abt + dtb_c_)
            drow_ref[0] = da_row
            vcol_ref[0:1, :] += jnp.sum(dg_col * g_full, axis=0, keepdims=True)
            vcol_ref[1:2, :] += jnp.sum(da_col, axis=0, keepdims=True)
            rl = jnp.sum(dg_row * gt_full, axis=-1, keepdims=True)
            rd = jnp.sum(da_row, axis=-1, keepdims=True)
            vrow_ref[...] += jnp.where(lane2 == 0, rl, 0.0) + jnp.where(lane2 == 1, rd, 0.0)

        @pl.when(pl.program_id(0) == nch // 2)
        def _():
            xc.forward(x_in, x_out, sems)

        @pl.when(pl.program_id(0) == nch - 1)
        def _():
            xc.wait(x_in, x_out, sems)

    def row(c, col):
        return pl.BlockSpec((CH, D), lambda s: (c(s), col))

    def hist(c, n):
        return pl.BlockSpec((1, NH, n, n), lambda s: (c(s), 0, 0, 0))

    def chunk_in(c):
        return [row(c, 0), row(c, 1), row(c, 2), pl.BlockSpec((CH, LANE), lambda s: (c(s), 0)),
                pl.BlockSpec((1, 4 * NH, CH), lambda s: (c(s), 0, 0)), hist(c, DH), hist(c, CH), row(c, 0)]

    small = [_full((1, LANE)), _full((1, LANE)), _full((4 * NH, 1)), _full((4 * NH, 1))]
    return pl.pallas_call(
        body, name="gdn_bwd", grid=(nch,),
        out_shape=(_sds((lt, 3 * D)), _sds((lt, 3 * D)), _sds((lt, LANE)), _sds((lt, LANE)),
                   _sds((nch, 4 * NH, CH)), _sds((nch, 4 * NH, CH)), _sds((8, LANE)), _sds((4 * NH, LANE))) + xc.out_shape,
        in_specs=chunk_in(cf) + chunk_in(cb) + small + [_ANY] * nx,
        out_specs=(pl.BlockSpec((CH, 3 * D), lambda s: (cf(s), 0)), pl.BlockSpec((CH, 3 * D), lambda s: (cb(s), 0)),
                   pl.BlockSpec((CH, LANE), lambda s: (cf(s), 0)), pl.BlockSpec((CH, LANE), lambda s: (cb(s), 0)),
                   pl.BlockSpec((1, 4 * NH, CH), lambda s: (cf(s), 0, 0)), pl.BlockSpec((1, 4 * NH, CH), lambda s: (cb(s), 0, 0)),
                   _full((8, LANE)), _full((4 * NH, LANE))) + tuple([_ANY] * nx),
        scratch_shapes=[pltpu.VMEM((2, NH, DH, DH), F32)] + xc.scratch,
        compiler_params=_params(("arbitrary",)),
    )(qkv, qkv, qkv, pab, abt, s_f, t_f, do, qkv, qkv, qkv, pab, abt, s_b, t_b, do, alog_r, dtb_r, alog_c, dtb_c,
      *xc_arrays)


def _post(p, o_f, o_b, x, tgt, w_pa, w_pb, w_out, w_sp, w_spt, b_spb, ln_g, ln_b, g_on, g_post, gate_x, lc):
    lt = p.shape[0]
    l = x.shape[0]
    tm = GC
    nct = lc // tm

    def body(p_ref, of_ref, ob_ref, x_ref, t_ref, wpa, wpb, wout, wsp, wspt, bspb, lng_ref, lnb_ref, gon_ref, gpost_ref, gate_ref,
             dp_ref, do_ref, dy_ref, ya_ref, yb_ref, mg_ref, da_ref, db_ref, dout_ref, dwsp_ref, dbsp_ref, vec_ref):
        i = pl.program_id(0)

        @pl.when(i == 0)
        def _():
            dwsp_ref[...] = jnp.zeros_like(dwsp_ref)
            dbsp_ref[...] = jnp.zeros_like(dbsp_ref)
            vec_ref[...] = jnp.zeros_like(vec_ref)

        @pl.when(i < nct)
        def _():
            dp_ref[...] = jnp.zeros_like(dp_ref)
            do_ref[...] = jnp.zeros_like(do_ref)

        @pl.when(i >= nct)
        def _():
            lng, lnb, gon, gpost, gate = lng_ref[...], lnb_ref[...], gon_ref[...], gpost_ref[...], gate_ref[...]
            zb, ua, va, za, ga, gb = [p_ref[:, j * D:(j + 1) * D] for j in range(6)]
            o = of_ref[...] + ob_ref[...]
            szb, dszb = _silu_g(zb)
            nh_l, r_l = [], []
            for h in range(NH):
                oh = o[:, h * DH:(h + 1) * DH]
                r = lax.rsqrt(jnp.mean(oh * oh, axis=-1, keepdims=True) + EPS)
                nh_l.append(oh * r)
                r_l.append(r)
            nrm_b = jnp.concatenate(nh_l, axis=-1)
            gon_t = jnp.concatenate([gon] * NH, axis=-1)
            y_b = nrm_b * gon_t * szb
            u, du_dua = _gelu_g(ua)
            gv, dgv_dva = _gelu_g(va)
            xc = gv - jnp.mean(gv, axis=-1, keepdims=True)
            rs_ln = lax.rsqrt(jnp.mean(xc * xc, axis=-1, keepdims=True) + EPS)
            vhat = xc * rs_ln
            v = vhat * lng + lnb
            s_sp = jnp.concatenate(
                [_mm(wsp[g], v[:, g * DH:(g + 1) * DH]) + bspb[g] for g in range(NH)], axis=-1)
            sza, dsza = _silu_g(za)
            y_a = u * s_sp * sza
            a_pr = _mm(y_a, wpa[...])
            b_pr = _mm(y_b, wpb[...])
            sga = _sigmoid(ga)
            sgb = _sigmoid(gb)
            merged = sga * a_pr + sgb * b_pr
            out = _mm(merged, wout[...])
            rs_o = lax.rsqrt(jnp.mean(out * out, axis=-1, keepdims=True) + EPS)
            n_o = out * rs_o
            rr = n_o * gpost
            diff = x_ref[...] + gate * rr - t_ref[...]
            vec_ref[5:6, :] += jnp.sum(diff * diff, axis=0, keepdims=True)
            dy = diff * (1.0 / D)
            dy_ref[...] = dy
            vec_ref[0:1, :] += jnp.sum(dy * rr, axis=0, keepdims=True)
            dr = dy * gate
            vec_ref[1:2, :] += jnp.sum(dr * n_o, axis=0, keepdims=True)
            dn_o = dr * gpost
            dout = rs_o * (dn_o - n_o * jnp.mean(dn_o * n_o, axis=-1, keepdims=True))
            dmerged = _mm(dout, wout[...], NT)
            d_a = dmerged * sga
            d_b = dmerged * sgb
            dga = dmerged * a_pr * sga * (1.0 - sga)
            dgb = dmerged * b_pr * sgb * (1.0 - sgb)
            dy_a = _mm(d_a, wpa[...], NT)
            dy_b = _mm(d_b, wpb[...], NT)
            ya_ref[...] = y_a.astype(ya_ref.dtype)
            yb_ref[...] = y_b.astype(yb_ref.dtype)
            mg_ref[...] = merged.astype(mg_ref.dtype)
            da_ref[...] = d_a.astype(da_ref.dtype)
            db_ref[...] = d_b.astype(db_ref.dtype)
            dout_ref[...] = dout.astype(dout_ref.dtype)
            dua = dy_a * s_sp * sza * du_dua
            ds_sp = dy_a * u * sza
            dza = dy_a * u * s_sp * dsza
            dv_l = []
            for g in range(NH):
                ds_g = ds_sp[:, g * DH:(g + 1) * DH]
                dv_l.append(_mm(wspt[g], ds_g))
                dwsp_ref[g] += _mm(ds_g, v[:, g * DH:(g + 1) * DH], NT)
                dbsp_ref[g] += ds_g
            dv = jnp.concatenate(dv_l, axis=-1)
            vec_ref[2:3, :] += jnp.sum(dv * vhat, axis=0, keepdims=True)
            vec_ref[3:4, :] += jnp.sum(dv, axis=0, keepdims=True)
            dvh = dv * lng
            dgv = rs_ln * (dvh - jnp.mean(dvh, axis=-1, keepdims=True) - vhat * jnp.mean(dvh * vhat, axis=-1, keepdims=True))
            dva = dgv * dgv_dva
            dzb = dy_b * nrm_b * gon_t * dszb
            dgon_full = jnp.sum(dy_b * nrm_b * szb, axis=0, keepdims=True)
            dgon = dgon_full[:, 0:DH]
            for h in range(1, NH):
                dgon = dgon + dgon_full[:, h * DH:(h + 1) * DH]
            vec_ref[4:5, 0:DH] += dgon
            dnb = dy_b * gon_t * szb
            do_l = []
            for h in range(NH):
                sl = slice(h * DH, (h + 1) * DH)
                dn_h = dnb[:, sl]
                do_l.append(r_l[h] * (dn_h - nh_l[h] * jnp.mean(dn_h * nh_l[h], axis=-1, keepdims=True)))
            do_ref[...] = jnp.concatenate(do_l, axis=-1)
            for j, val in enumerate((dzb, dua, dva, dza, dga, dgb)):
                dp_ref[:, j * D:(j + 1) * D] = val.astype(dp_ref.dtype)

    xrow = lambda i: (jnp.maximum(i - nct, 0), 0)
    wspec = _full((D, D))
    gspec = _full((NH, GC, GC))
    vspec = _full((1, D))
    bf_out = _sds((l, D), _BF)
    return pl.pallas_call(
        body, name="post", grid=(lt // tm,),
        out_shape=(_sds((lt, NREST), _BF), _sds((lt, D)), _sds((l, D)), bf_out, bf_out, bf_out, bf_out, bf_out, bf_out,
                   _sds((NH, GC, GC)), _sds((NH, GC, GC)), _sds((8, D))),
        in_specs=[pl.BlockSpec((tm, NREST), lambda i: (i, 0)), pl.BlockSpec((tm, D), lambda i: (i, 0)),
                  pl.BlockSpec((tm, D), lambda i: (i, 0)), pl.BlockSpec((tm, D), xrow), pl.BlockSpec((tm, D), xrow),
                  wspec, wspec, wspec, gspec, gspec, gspec, vspec, vspec, _full((1, DH)), vspec, vspec],
        out_specs=(pl.BlockSpec((tm, NREST), lambda i: (i, 0)), pl.BlockSpec((tm, D), lambda i: (i, 0)),
                   pl.BlockSpec((tm, D), xrow), pl.BlockSpec((tm, D), xrow), pl.BlockSpec((tm, D), xrow),
                   pl.BlockSpec((tm, D), xrow), pl.BlockSpec((tm, D), xrow), pl.BlockSpec((tm, D), xrow),
                   pl.BlockSpec((tm, D), xrow), gspec, gspec, _full((8, D))),
        compiler_params=_params(("arbitrary",)),
    )(p, o_f, o_b, x, tgt, w_pa, w_pb, w_out, w_sp, w_spt, b_spb, ln_g, ln_b, g_on, g_post, gate_x)


def _sum_parts(parts, name):
    r = parts.shape[1]
    tr = r if NDEV * r * LANE * 4 <= (8 << 20) else _tile(r, (512, 256, 128, 64, 32, 16, 8))

    def body(p_ref, o_ref):
        acc = p_ref[0]
        for s in range(1, NDEV):
            acc = acc + p_ref[s]
        o_ref[...] = acc

    return pl.pallas_call(
        body, name=name, out_shape=_sds((r, LANE)), grid=(r // tr,),
        in_specs=[pl.BlockSpec((NDEV, tr, LANE), lambda i: (0, i, 0))],
        out_specs=pl.BlockSpec((tr, LANE), lambda i: (i, 0)),
        compiler_params=_params(("parallel",)),
    )(parts)


def _mod_bwd(c_all, c_ctx, dmx, dmc, w_mod_g):
    ws = w_mod_g.shape[2]

    def body(ca_ref, cc_ref, dsh_ref, dmx_ref, dmc_ref, dmc_sh_ref, w_ref, gw_ref, gc_ref, gb_ref):
        sc, _ = _silu_g(ca_ref[...])
        scc, dscc = _silu_g(cc_ref[...])
        dmc_tot = jnp.sum(dmc_ref[...], axis=0, keepdims=True)
        gb_ref[...] = jnp.sum(dmx_ref[...], axis=0, keepdims=True) + dmc_tot
        lhs = jnp.concatenate([sc, jnp.broadcast_to(scc, (8, D))], axis=0)
        rhs = jnp.concatenate([dsh_ref[...], dmc_sh_ref[...]], axis=0)
        gw_ref[...] = _mmh(lhs, rhs, TN)
        acc = jnp.zeros((8, D), F32)
        tot8 = jnp.broadcast_to(dmc_tot, (8, 3 * D))
        for j in range(NDEV):
            acc = acc + _mm(tot8[:, j * ws:(j + 1) * ws], w_ref[j], NT)
        gc_ref[...] = acc[0:1, :] * dscc

    return pl.pallas_call(
        body, name="mod_bwd", out_shape=(_sds((D, ws)), _sds((1, D)), _sds((1, 3 * D))),
        compiler_params=_params(),
    )(c_all, c_ctx, _my_cols(dmx, ws), dmx, dmc, _my_cols(dmc, ws), w_mod_g)


def _my_cols(a, ws):
    me = 4 * lax.axis_index("x") + 2 * lax.axis_index("y") + lax.axis_index("c")
    return lax.dynamic_slice_in_dim(a, me * ws, ws, axis=1)


def _pair_sum(mine, other, name):
    n, r, c = mine.shape
    tr = _tile(r, (256, 128, 64, 32, 16, 8))

    def body(a_ref, b_ref, o_ref):
        o_ref[...] = (a_ref[...].astype(F32) + b_ref[...].astype(F32)).astype(o_ref.dtype)

    blk = pl.BlockSpec((1, tr, c), lambda j, i: (j, i, 0))
    return pl.pallas_call(
        body, name=name, out_shape=_sds((n, r, c), mine.dtype), grid=(n, r // tr),
        in_specs=[blk, blk], out_specs=blk, compiler_params=_params(("parallel", "parallel")),
    )(mine, other)


def _adamw(parts, w, m, v, name, chip_sums_below=None):
    s_, r, c = parts.shape
    tr = _tile(r, (128, 64, 32, 16, 8)) if r * c * 4 > (1 << 20) else r
    c1 = 1.0 / (1.0 - ADAM_B1 ** ADAM_STEP)
    c2 = 1.0 / (1.0 - ADAM_B2 ** ADAM_STEP)

    def body(p_ref, w_ref, m_ref, v_ref, g_ref, d_ref, nm_ref, nv_ref):
        if chip_sums_below is None:
            part = lambda s: p_ref[s].astype(F32)
        else:
            core = lax.axis_index("c")
            me = 4 * lax.axis_index("x") + 2 * lax.axis_index("y") + core
            every = me >= chip_sums_below
            part = lambda s: jnp.where(every | (core == s % 2), p_ref[s].astype(F32), 0.0)
        g = part(0)
        for s in range(1, s_):
            g = g + part(s)
        m_new = ADAM_B1 * m_ref[...] + (1.0 - ADAM_B1) * g
        v_new = ADAM_B2 * v_ref[...] + (1.0 - ADAM_B2) * (g * g)
        g_ref[...] = g
        nm_ref[...] = m_new
        nv_ref[...] = v_new
        d_ref[...] = -ADAM_LR * ((m_new * c1) / (jnp.sqrt(v_new * c2) + ADAM_EPS) + ADAM_WD * w_ref[...])

    blk = pl.BlockSpec((tr, c), lambda i: (i, 0))
    o = _sds((r, c))
    return pl.pallas_call(
        body, name=name, out_shape=(o, o, o, o), grid=(r // tr,),
        in_specs=[pl.BlockSpec((s_, tr, c), lambda i: (0, i, 0)), blk, blk, blk],
        out_specs=(blk, blk, blk, blk),
        compiler_params=_params(("parallel",)),
    )(parts, w, m, v)


def _rows(a):
    flat = a.reshape(-1)
    n = flat.shape[0]
    r = -(-n // (8 * LANE)) * 8
    return jnp.pad(flat, (0, r * LANE - n)).reshape(r, LANE)


def _pack(items):
    parts, layout, at = [], [], 0
    for name, a in items:
        rws = _rows(a.astype(F32))
        layout.append((name, at, rws.shape[0], a.shape))
        parts.append(rws)
        at += rws.shape[0]
    return jnp.concatenate(parts, axis=0), layout


def _unpack(packed, layout):
    out = {}
    for name, at, r, shape in layout:
        n = 1
        for s in shape:
            n *= s
        out[name] = packed[at:at + r].reshape(-1)[:n].reshape(shape)
    return out


def kernel(x, c, ctx, c_ctx, w_mod, b_mod, g_pre, g_post, w_in, w_conv, a_log, dt_bias, g_onorm, gm_ln_g, gm_ln_b, w_sp, b_sp, w_pa, w_pb, w_out, loss_target, m_c_ctx, m_w_mod, m_b_mod, m_g_pre, m_g_post, m_w_in, m_w_conv, m_a_log, m_dt_bias, m_g_onorm, m_gm_ln_g, m_gm_ln_b, m_w_sp, m_b_sp, m_w_pa, m_w_pb, m_w_out, v_c_ctx, v_w_mod, v_b_mod, v_g_pre, v_g_post, v_w_in, v_w_conv, v_a_log, v_dt_bias, v_g_onorm, v_gm_ln_g, v_gm_ln_b, v_w_sp, v_b_sp, v_w_pa, v_w_pb, v_w_out):
    l = x.shape[1]
    lc = ctx.shape[1]
    lt = l + lc
    nch = lt // CH
    me = 4 * lax.axis_index("x") + 2 * lax.axis_index("y") + lax.axis_index("c")
    wsh = w_in.shape[2]
    off_a = 3 * D
    n_ab = 4 * NH
    jb = off_a // wsh
    o1 = off_a - jb * wsh
    o2 = o1 + n_ab
    assert o2 <= wsh and NREST == (NDEV - jb) * wsh - o2
    split = jb + 1

    wp = -(-wsh // LANE) * LANE
    widen = lambda a: jnp.pad(a, [(0, 0)] * (a.ndim - 1) + [(0, wp - wsh)])
    w_in_bf = widen(w_in[0].astype(_BF))
    wg_lo, wg_mod, wg_conv, c_all = _exchange(
        [w_in_bf, w_mod[0].astype(_BF), w_conv[0], c], ["gather_lo", "gather", "gather", "gather"],
        "gather_first", split)
    w_qkv = jnp.concatenate([wg_lo[j][:, :wsh] for j in range(jb)] + [wg_lo[jb][:, :o1]], axis=1)
    w_ab = jnp.pad(wg_lo[jb][:, o1:o2], ((0, 0), (0, LANE - n_ab)))
    wconv_full = jnp.moveaxis(wg_conv, 0, 1).reshape(3, 3 * D)
    c_all = c_all.reshape(NDEV, D)

    cc = jnp.concatenate([c, c_ctx.reshape(1, D), jnp.zeros((6, D), F32)], axis=0)
    mods = _modulation(cc, wg_mod, b_mod)
    xa = jnp.concatenate([ctx[0], x[0]], axis=0)
    h = _prenorm(xa, mods, g_pre, lc)
    p_qkv = _matmul_nn(h, w_qkv, "in_proj_qkv")
    pab = _matmul_nn(h, w_ab, "in_proj_ab")
    abt = jnp.swapaxes(pab[:, :n_ab].reshape(nch, CH, n_ab), 1, 2)
    alog16, dtb16 = a_log.reshape(1, 2 * NH), dt_bias.reshape(1, 2 * NH)
    alog_r = jnp.pad(alog16, ((0, 0), (0, LANE - 2 * NH)))
    dtb_r = jnp.pad(dtb16, ((0, 0), (0, LANE - 2 * NH)))
    alog_c = jnp.pad(alog16.reshape(2 * NH, 1), ((0, 2 * NH), (0, 0)))
    dtb_c = jnp.pad(dtb16.reshape(2 * NH, 1), ((0, 2 * NH), (0, 0)))
    qkv = _qkv_fwd(p_qkv, wconv_full, lc)
    late = [w_in_bf, w_pa[0].astype(_BF), w_pb[0].astype(_BF), w_out[0].astype(_BF)]
    xc_late = _Exchange(zip(late, ["gather_hi", "gather", "gather", "gather"]), split)
    o_f, o_b, s_f, s_b, t_f, t_b, wg_hi, wg_pa, wg_pb, wg_out = _gdn_fwd(
        qkv, pab, abt, alog_r, dtb_r, alog_c, dtb_c, lc, xc_late, late)
    w_rest = jnp.concatenate([wg_lo[jb][:, o2:wsh]] + [wg_hi[j][:, :wsh] for j in range(split, NDEV)], axis=1)
    wf_pa, wf_pb, wf_out = wg_pa.reshape(D, D), wg_pb.reshape(D, D), wg_out.reshape(D, D)
    p_rest = _matmul_nn(h, w_rest, "in_proj_rest")

    w_spt = jnp.swapaxes(w_sp[0], 1, 2)
    b_spb = jnp.broadcast_to(b_sp[0][:, :, None], (NH, GC, GC))
    gate_x = mods[0:1, 2 * D:]
    dp_rest, do, dy, ya, yb, mg, d_a, d_b, dout, dwsp, dbsp_l, pvec = _post(
        p_rest, o_f, o_b, x[0], loss_target[0], wf_pa, wf_pb, wf_out, w_sp[0], w_spt, b_spb, gm_ln_g, gm_ln_b,
        g_onorm, g_post, gate_x, lc)

    dw_rest = _matmul_tn(h, dp_rest, "dw_in_rest")
    o3 = wsh - o2
    chunks_hi = widen(jnp.moveaxis(dw_rest[:, o3:].reshape(D, NDEV - split, wsh), 1, 0))
    dw_pa = _matmul_tn(ya, d_a, "dw_pa").reshape(NDEV, D // NDEV, D)
    dw_pb = _matmul_tn(yb, d_b, "dw_pb").reshape(NDEV, D // NDEV, D)
    dw_out = _matmul_tn(mg, dout, "dw_out").reshape(NDEV, D // NDEV, D)
    small_a, lay_a = _pack([
        ("g_post", pvec[1]), ("g_onorm", pvec[4, :DH]), ("gm_ln_g", pvec[2]), ("gm_ln_b", pvec[3]), ("w_sp", dwsp),
        ("b_sp", jnp.sum(dbsp_l, axis=-1)), ("loss", pvec[5]), ("dgate", pvec[0])])
    (theirs_hi,) = _exchange([chunks_hi], ["sibling"], "pair_swap_hi")
    chip_hi = _pair_sum(chunks_hi, theirs_hi[0], "pair_sum_hi")
    early = [chip_hi, small_a]
    xc_early = _Exchange(zip(early, ["scatter_par_hi", "gather"]), split)

    dqkv_f, dqkv_b, dcol_f, dcol_b, drow_f, drow_b, gvec_c, gvec_r, r_in, small_a_all = _gdn_bwd(
        qkv, pab, abt, alog_r, dtb_r, alog_c, dtb_c, s_f, s_b, t_f, t_b, do, lc, xc_early, early)
    dp_qkv, dwconv = _qkv_bwd(p_qkv, wconv_full, dqkv_f, dqkv_b, lc)
    drow = jnp.swapaxes(drow_f + drow_b, 1, 2).reshape(lt, n_ab)
    dpab = (dcol_f + dcol_b + jnp.pad(drow, ((0, 0), (0, LANE - n_ab)))).astype(_BF)

    dw_qkv = _matmul_tn(h, dp_qkv, "dw_in_qkv")
    dw_ab = _matmul_tn(h, dpab, "dw_in_ab")
    dw_lo = jnp.concatenate([dw_qkv, dw_ab[:, :n_ab], dw_rest[:, :o3]], axis=1)
    chunks_lo = widen(jnp.moveaxis(dw_lo.reshape(D, split, wsh), 1, 0))
    (theirs,) = _exchange([chunks_lo], ["sibling"], "pair_swap")
    chip_lo = _pair_sum(chunks_lo, theirs[0], "pair_sum")
    last = [chip_lo, dw_pa, dw_pb, dw_out]
    xc_last = _Exchange(zip(last, ["scatter_par_lo", "scatter", "scatter", "scatter"]), split)
    dh, r_in, r_pa, r_pb, r_out = _dh_matmul(dp_rest, dp_qkv, dpab, w_rest, w_qkv, w_ab, xc_last, last, {0: r_in})
    grad_x, nvec = _prenorm_bwd(xa, dh, dy, mods, g_pre, lc)

    dalog = gvec_c[0, :2 * NH] + gvec_r[:2 * NH, 0]
    ddtb = gvec_c[1, :2 * NH] + gvec_r[:2 * NH, 1]
    small_b, lay_b = _pack([
        ("g_pre", nvec[4]), ("a_log", dalog), ("dt_bias", ddtb), ("w_conv", dwconv),
        ("dshift", nvec[0]), ("dscale", nvec[1]), ("dshift_c", nvec[2]), ("dscale_c", nvec[3])])
    (small_b_all,) = _exchange([small_b], ["gather"], "gather_small")
    tot = _unpack(_sum_parts(small_a_all, "sum_small_a"), lay_a)
    tot.update(_unpack(_sum_parts(small_b_all, "sum_small_b"), lay_b))

    def per_device(packed_all, layout, name):
        at, r = [(a_, r_) for nm, a_, r_, _ in layout if nm == name][0]
        return packed_all[:, at:at + r].reshape(NDEV, -1)

    dmx_all = jnp.concatenate([per_device(small_b_all, lay_b, "dshift"), per_device(small_b_all, lay_b, "dscale"),
                               per_device(small_a_all, lay_a, "dgate")], axis=1)
    dmc_all = jnp.concatenate([per_device(small_b_all, lay_b, "dshift_c"), per_device(small_b_all, lay_b, "dscale_c"),
                               jnp.zeros((NDEV, D), F32)], axis=1)
    g_wmod, g_cctx, g_bmod = _mod_bwd(c_all, c_ctx.reshape(1, D), dmx_all, dmc_all, wg_mod)
    loss = 0.5 / D * jnp.sum(tot["loss"])
    ws_conv = w_conv.shape[2]
    g_wconv = lax.dynamic_slice_in_dim(tot["w_conv"], me * ws_conv, ws_conv, axis=1)

    small_names = ["c_ctx", "b_mod", "g_pre", "g_post", "a_log", "dt_bias", "g_onorm", "gm_ln_g", "gm_ln_b",
                   "w_sp", "b_sp", "w_conv"]
    wts = dict(c_ctx=c_ctx, b_mod=b_mod, g_pre=g_pre, g_post=g_post, a_log=a_log, dt_bias=dt_bias, g_onorm=g_onorm,
               gm_ln_g=gm_ln_g, gm_ln_b=gm_ln_b, w_sp=w_sp, b_sp=b_sp, w_conv=w_conv)
    ms = dict(c_ctx=m_c_ctx, b_mod=m_b_mod, g_pre=m_g_pre, g_post=m_g_post, a_log=m_a_log, dt_bias=m_dt_bias,
              g_onorm=m_g_onorm, gm_ln_g=m_gm_ln_g, gm_ln_b=m_gm_ln_b, w_sp=m_w_sp, b_sp=m_b_sp, w_conv=m_w_conv)
    vs = dict(c_ctx=v_c_ctx, b_mod=v_b_mod, g_pre=v_g_pre, g_post=v_g_post, a_log=v_a_log, dt_bias=v_dt_bias,
              g_onorm=v_g_onorm, gm_ln_g=v_gm_ln_g, gm_ln_b=v_gm_ln_b, w_sp=v_w_sp, b_sp=v_b_sp, w_conv=v_w_conv)
    gs = dict(tot)
    gs.update(c_ctx=g_cctx, b_mod=g_bmod, w_conv=g_wconv)
    gpk, play = _pack([(nm, gs[nm].reshape(wts[nm].shape)) for nm in small_names])
    wpk, _ = _pack([(nm, wts[nm]) for nm in small_names])
    mpk, _ = _pack([(nm, ms[nm]) for nm in small_names])
    vpk, _ = _pack([(nm, vs[nm]) for nm in small_names])
    res_small = [_unpack(a, play) for a in _adamw(gpk[None], wpk, mpk, vpk, "adamw_small")]
    res_big = {
        "w_mod": _adamw(g_wmod[None], w_mod[0], m_w_mod[0], v_w_mod[0], "adamw_w_mod"),
        "w_in": [a[:, :wsh] for a in _adamw(r_in, widen(w_in[0]), widen(m_w_in[0]), widen(v_w_in[0]), "adamw_w_in",
                                            chip_sums_below=NDEV)],
        "w_pa": _adamw(r_pa, w_pa[0], m_w_pa[0], v_w_pa[0], "adamw_w_pa"),
        "w_pb": _adamw(r_pb, w_pb[0], m_w_pb[0], v_w_pb[0], "adamw_w_pb"),
        "w_out": _adamw(r_out, w_out[0], m_w_out[0], v_w_out[0], "adamw_w_out"),
    }
    order = ["c_ctx", "w_mod", "b_mod", "g_pre", "g_post", "w_in", "w_conv", "a_log", "dt_bias", "g_onorm",
             "gm_ln_g", "gm_ln_b", "w_sp", "b_sp", "w_pa", "w_pb", "w_out"]
    outs = [loss, grad_x[None]]
    for k in range(4):
        for nm in order:
            if nm in res_big:
                outs.append(res_big[nm][k][None])
            else:
                outs.append(res_small[k][nm])
    return tuple(outs)
```

```python
import functools

import jax
import jax.numpy as jnp
from jax import lax
from jax.experimental import pallas as pl
from jax.experimental.pallas import tpu as pltpu

F32 = jnp.float32
_BF = jnp.bfloat16
_HI = lax.Precision.HIGHEST
D = 1024
NH = 8
DH = 128
CH = 64
GC = 128
NREST = 6 * D
NMAIN = NREST + 3 * D
EPS = 1e-6
LANE = 128
NDEV = 8
VMEM_LIMIT = 56 * 1024 * 1024
MESH = pl.DeviceIdType.MESH

ADAM_LR, ADAM_B1, ADAM_B2, ADAM_EPS, ADAM_WD, ADAM_STEP = 0.001, 0.9, 0.999, 1e-08, 0.01, 10

NN = ((1,), (0,))
NT = ((1,), (1,))
TN = ((0,), (0,))


def _dot(a, b, dims=NN, prec=None):
    return lax.dot_general(a, b, (dims, ((), ())), precision=prec, preferred_element_type=F32)


def _mm(a, b, dims=NN):
    return _dot(a.astype(_BF), b.astype(_BF), dims)


def _mmh(a, b, dims=NN):
    return _dot(a.astype(F32), b.astype(F32), dims, _HI)


def _split(a):
    hi = a.astype(_BF)
    return hi, (a - hi.astype(F32)).astype(_BF)


def _mm3(a, b, dims=NN):
    ah, al = _split(a)
    bh, bl = _split(b)
    return _dot(ah, bh, dims) + (_dot(ah, bl, dims) + _dot(al, bh, dims))


def _sigmoid(x):
    return 1.0 / (1.0 + jnp.exp(-x))


def _silu_g(x):
    s = _sigmoid(x)
    return x * s, s * (1.0 + x * (1.0 - s))


def _gelu_g(x):
    c = 0.7978845608028654
    t = jnp.tanh(c * (x + 0.044715 * (x * x * x)))
    cdf = 0.5 * (1.0 + t)
    return x * cdf, cdf + 0.5 * x * (1.0 - t * t) * c * (1.0 + 3 * 0.044715 * x * x)


def _softplus(x):
    return jnp.maximum(x, 0.0) + jnp.log(1.0 + jnp.exp(-jnp.abs(x)))


def _params(sem=None):
    return pltpu.CompilerParams(dimension_semantics=sem, vmem_limit_bytes=VMEM_LIMIT)


def _tile(n, pref):
    for t in pref:
        if n % t == 0:
            return t
    return n


def _full(shape):
    nd = len(shape)
    return pl.BlockSpec(shape, lambda *_: (0,) * nd)


def _sds(shape, dt=F32):
    return jax.ShapeDtypeStruct(shape, dt)


MAX_PIECES = 12
PIECE_BYTES = 256 * 1024


def _piece_slices(shape, itemsize):
    total = itemsize
    for d in shape:
        total *= d
    want = min(MAX_PIECES, total // PIECE_BYTES)
    lead = shape[0] if len(shape) >= 3 else 1
    rows = shape[-2] if len(shape) >= 2 else 1
    if want < 2 or lead > want:
        return [()]
    m = max([n for n in (8, 4, 2, 1) if n * lead <= want and rows % (16 * n) == 0], default=1)
    if m * lead < 2:
        return [()]
    rs = rows // m
    mid = (slice(None),) * max(len(shape) - 3, 0)
    if len(shape) >= 3:
        return [(i,) + mid + (pl.ds(j * rs, rs),) for i in range(lead) for j in range(m)]
    return [(pl.ds(j * rs, rs),) for j in range(m)]


class _Pieces:
    def __init__(self, copies):
        self.copies = copies

    def start(self):
        for cp in self.copies:
            cp.start()

    def wait_send(self):
        for cp in self.copies:
            cp.wait_send()

    def wait_recv(self):
        for cp in self.copies:
            cp.wait_recv()

    def wait(self):
        for cp in self.copies:
            cp.wait()


class _Exchange:
    def __init__(self, specs, split):
        self.specs = list(specs)
        self.split = split
        self.n = len(self.specs)
        def out(a, k):
            if k == "sibling":
                return (1,) + tuple(a.shape)
            return (NDEV,) + (tuple(a.shape) if k.startswith("gather") else tuple(a.shape[1:]))

        self.out_shape = tuple(_sds(out(a, k), a.dtype) for a, k in self.specs)
        self.pieces = [_piece_slices(o.shape[1:], jnp.dtype(o.dtype).itemsize) for o in self.out_shape]
        self.sem_base = [(NDEV - 1) * sum(len(p) for p in self.pieces[:a]) for a in range(self.n + 1)]
        self.scratch = [pltpu.SemaphoreType.DMA((self.sem_base[-1],)), pltpu.SemaphoreType.DMA((self.sem_base[-1],)),
                        pltpu.SemaphoreType.DMA((self.sem_base[-1] // (NDEV - 1),))]

    def _local(self, sems, a, src, dst):
        base = self.sem_base[a] // (NDEV - 1)
        return _Pieces([pltpu.make_async_copy(src.at[sl] if sl else src, dst.at[sl] if sl else dst, sems[2].at[base + p])
                        for p, sl in enumerate(self.pieces[a])])

    def _remote(self, sems, a, k, src, dst, to):
        send_sems, recv_sems, _ = sems
        base = self.sem_base[a] + k * len(self.pieces[a])
        return _Pieces([
            pltpu.make_async_remote_copy(
                src_ref=src.at[sl] if sl else src, dst_ref=dst.at[sl] if sl else dst, send_sem=send_sems.at[base + p],
                recv_sem=recv_sems.at[base + p], device_id=to, device_id_type=MESH)
            for p, sl in enumerate(self.pieces[a])])

    def _ok(self, kind, idx):
        if kind.endswith("_lo"):
            return idx < self.split
        if kind.endswith("_hi"):
            return idx >= self.split
        return True

    def _phases(self, ins, outs, sems):
        x, y, c = lax.axis_index("x"), lax.axis_index("y"), lax.axis_index("c")
        me = 4 * x + 2 * y + c
        sib = (x, y, 1 - c)
        sib_idx = 4 * x + 2 * y + (1 - c)
        chips = [(1 - x, y), (x, 1 - y), (1 - x, 1 - y)]
        starts, forwards, waits = [], [], []
        for a, (_, kind) in enumerate(self.specs):
            ok = functools.partial(self._ok, kind)
            if kind.startswith("gather"):
                def copy(k, block, to, src=None, a=a):
                    rows = outs[a].at[block]
                    return self._remote(sems, a, k, rows if src is None else src, rows, to)

                loc = self._local(sems, a, ins[a], outs[a].at[me])
                first = [copy(0, me, sib, ins[a])] + [copy(1 + j, me, (*chip, c), ins[a]) for j, chip in enumerate(chips)]
                starts += [(ok(me), loc.start)] + [(ok(me), cp.start) for cp in first]
                waits += [(ok(me), loc.wait)] + [(ok(me), cp.wait_send) for cp in first]
                for j, chip in enumerate(chips):
                    origin = 4 * chip[0] + 2 * chip[1] + c
                    passed = copy(4 + j, origin, sib)
                    forwards += [(ok(origin), copy(1 + j, origin, sib).wait_recv), (ok(origin), passed.start)]
                    waits.append((ok(origin), passed.wait_send))
                    other = 4 * chip[0] + 2 * chip[1] + (1 - c)
                    waits.append((ok(other), copy(4 + j, other, sib).wait_recv))
                waits.append((ok(sib_idx), copy(0, sib_idx, sib).wait_recv))
            elif kind == "sibling":
                swap = self._remote(sems, a, 0, ins[a], outs[a].at[0], sib)
                starts.append((True, swap.start))
                waits += [(True, swap.wait_send), (True, swap.wait_recv)]
            else:
                base = self.split if kind.endswith("_hi") else 0
                same_core_only = "_par" in kind

                def src(idx, a=a, base=base):
                    return ins[a].at[jnp.clip(idx - base, 0, ins[a].shape[0] - 1)]

                loc = self._local(sems, a, src(me), outs[a].at[me])
                starts.append((ok(me), loc.start))
                waits.append((ok(me), loc.wait))
                for k in range(1, NDEV):
                    if same_core_only and k & 1:
                        continue
                    px = 1 - x if (k >> 2) & 1 else x
                    py = 1 - y if (k >> 1) & 1 else y
                    pc = 1 - c if k & 1 else c
                    pidx = 4 * px + 2 * py + pc
                    send = self._remote(sems, a, k - 1, src(pidx), outs[a].at[me], (px, py, pc))
                    arrive = self._remote(sems, a, k - 1, src(pidx), outs[a].at[pidx], (px, py, pc))
                    starts.append((ok(pidx), send.start))
                    waits += [(ok(pidx), send.wait_send), (ok(me), arrive.wait_recv)]
        return starts, forwards, waits

    @staticmethod
    def _run(actions):
        for cond, fn in actions:
            if cond is True:
                fn()
            else:
                pl.when(cond)(fn)

    def start(self, ins, outs, sems):
        self._run(self._phases(ins, outs, sems)[0])

    def forward(self, ins, outs, sems):
        self._run(self._phases(ins, outs, sems)[1])

    def wait(self, ins, outs, sems):
        self._run(self._phases(ins, outs, sems)[2])


_ANY = pl.BlockSpec(memory_space=pl.ANY)


def _exchange(arrays, kinds, name, split=0, into=None):
    xc = _Exchange(zip(arrays, kinds), split)
    n = xc.n
    into = into or {}
    ni = len(into)

    def body(*refs):
        ins, outs, sems = refs[:n], refs[n + ni:2 * n + ni], refs[2 * n + ni:]
        xc.start(ins, outs, sems)
        xc.forward(ins, outs, sems)
        xc.wait(ins, outs, sems)

    return pl.pallas_call(
        body, name=name, out_shape=xc.out_shape, in_specs=[_ANY] * (n + ni), out_specs=tuple([_ANY] * n),
        scratch_shapes=xc.scratch, input_output_aliases={n + t: a for t, a in enumerate(into)},
    )(*arrays, *into.values())


def _matmul_nn(a, b, name):
    m, kk = a.shape
    n = b.shape[1]
    tm = m if m * kk * a.dtype.itemsize <= (12 << 20) else _tile(m, (1088, 1024, 640, 512, 256, 128))
    tn = _tile(n, (512, 256, 128))

    def body(a_ref, b_ref, o_ref):
        o_ref[...] = _mm(a_ref[...], b_ref[...])

    return pl.pallas_call(
        body, name=name, out_shape=_sds((m, n)), grid=(n // tn, m // tm),
        in_specs=[pl.BlockSpec((tm, kk), lambda j, i: (i, 0)), pl.BlockSpec((kk, tn), lambda j, i: (0, j))],
        out_specs=pl.BlockSpec((tm, tn), lambda j, i: (i, j)),
        compiler_params=_params(("parallel", "parallel")),
    )(a, b)


def _matmul_tn(a, b, name):
    kk, m = a.shape
    n = b.shape[1]
    tk = _tile(kk, (1088, 1024, 640, 512, 256, 128))
    tn = _tile(n, (1024, 512, 256, 128))
    nk = kk // tk

    def body(a_ref, b_ref, o_ref, acc_ref):
        k = pl.program_id(1)

        @pl.when(k == 0)
        def _():
            acc_ref[...] = jnp.zeros_like(acc_ref)

        acc_ref[...] += _mm(a_ref[...], b_ref[...], TN)

        @pl.when(k == nk - 1)
        def _():
            o_ref[...] = acc_ref[...].astype(o_ref.dtype)

    return pl.pallas_call(
        body, name=name, out_shape=_sds((m, n), _BF), grid=(n // tn, nk),
        in_specs=[pl.BlockSpec((tk, m), lambda j, k: (k, 0)), pl.BlockSpec((tk, tn), lambda j, k: (k, j))],
        out_specs=pl.BlockSpec((m, tn), lambda j, k: (0, j)),
        scratch_shapes=[pltpu.VMEM((m, tn), F32)],
        compiler_params=_params(("parallel", "arbitrary")),
    )(a, b)


def _dh_matmul(dp_rest, dp_qkv, dpab, w_rest, w_qkv, w_ab, xc, xc_arrays, xc_into):
    lt = dp_rest.shape[0]
    tm = _tile(lt, (1088, 1024, 640, 512, 256, 128))
    nr, nq = dp_rest.shape[1] // D, dp_qkv.shape[1] // D
    nx, ni = xc.n, len(xc_into)
    ni_steps = lt // tm

    def body(*refs):
        dr_ref, dq_ref, ab_ref, wr_ref, wq_ref, wab_ref = refs[:6]
        x_in = refs[6:6 + nx]
        o_ref = refs[6 + nx + ni]
        x_out = refs[7 + nx + ni:7 + 2 * nx + ni]
        sems = refs[7 + 2 * nx + ni:]
        i = pl.program_id(0)
        k = pl.program_id(1)

        @pl.when((i == 0) & (k == 0))
        def _():
            xc.start(x_in, x_out, sems)

        @pl.when(k == 0)
        def _():
            o_ref[...] = _mm(ab_ref[...], wab_ref[...], NT)

        @pl.when(k < nr)
        def _():
            o_ref[...] += _mm(dr_ref[...], wr_ref[...], NT)

        @pl.when(k >= nr)
        def _():
            o_ref[...] += _mm(dq_ref[...], wq_ref[...], NT)

        @pl.when((i == ni_steps - 1) & (k == nr + nq - 1))
        def _():
            xc.wait(x_in, x_out, sems)

    rk = lambda k: jnp.minimum(k, nr - 1)
    qk = lambda k: jnp.maximum(k - nr, 0)
    return pl.pallas_call(
        body, name="dh_matmul", out_shape=(_sds((lt, D)),) + xc.out_shape, grid=(lt // tm, nr + nq),
        in_specs=[pl.BlockSpec((tm, D), lambda i, k: (i, rk(k))), pl.BlockSpec((tm, D), lambda i, k: (i, qk(k))),
                  pl.BlockSpec((tm, LANE), lambda i, k: (i, 0)),
                  pl.BlockSpec((D, D), lambda i, k: (0, rk(k))), pl.BlockSpec((D, D), lambda i, k: (0, qk(k))),
                  _full((D, LANE))] + [_ANY] * (nx + ni),
        out_specs=(pl.BlockSpec((tm, D), lambda i, k: (i, 0)),) + tuple([_ANY] * nx),
        scratch_shapes=xc.scratch, input_output_aliases={6 + nx + t: 1 + a for t, a in enumerate(xc_into)},
        compiler_params=_params(("arbitrary", "arbitrary")),
    )(dp_rest, dp_qkv, dpab, w_rest, w_qkv, w_ab, *xc_arrays, *xc_into.values())


def _modulation(cc, w_mod_g, b_mod):
    ws = w_mod_g.shape[2]

    def body(c_ref, w_ref, b_ref, o_ref):
        s, _ = _silu_g(c_ref[...])
        o_ref[...] = _mm(s, w_ref[0]) + b_ref[...]

    return pl.pallas_call(
        body, name="modulation", out_shape=_sds((8, 3 * D)), grid=(NDEV,),
        in_specs=[_full((8, D)), pl.BlockSpec((1, D, ws), lambda j: (j, 0, 0)), pl.BlockSpec((1, ws), lambda j: (0, j))],
        out_specs=pl.BlockSpec((8, ws), lambda j: (0, j)),
        compiler_params=_params(("parallel",)),
    )(cc, w_mod_g, b_mod)


def _prenorm(xa, mods, g_pre, lc):
    lt = xa.shape[0]
    tm = _tile(lc, (256, 128))
    nct = lc // tm

    def body(x_ref, m_ref, g_ref, o_ref):
        x = x_ref[...]
        is_ctx = pl.program_id(0) < nct
        shift = jnp.where(is_ctx, m_ref[1:2, 0:D], m_ref[0:1, 0:D])
        scale = jnp.where(is_ctx, m_ref[1:2, D:2 * D], m_ref[0:1, D:2 * D])
        r = lax.rsqrt(jnp.mean(x * x, axis=-1, keepdims=True) + EPS)
        o_ref[...] = ((x * r * g_ref[...]) * (1.0 + scale) + shift).astype(o_ref.dtype)

    return pl.pallas_call(
        body, name="prenorm", out_shape=_sds((lt, D), _BF), grid=(lt // tm,),
        in_specs=[pl.BlockSpec((tm, D), lambda i: (i, 0)), _full((8, 3 * D)), _full((1, D))],
        out_specs=pl.BlockSpec((tm, D), lambda i: (i, 0)),
        compiler_params=_params(("parallel",)),
    )(xa, mods, g_pre)


def _prenorm_bwd(xa, dh, dy, mods, g_pre, lc):
    lt = xa.shape[0]
    tm = _tile(lc, (256, 128))
    nct = lc // tm
    nl = (lt - lc) // tm

    def body(x_ref, dh_ref, dy_ref, m_ref, g_ref, gx_ref, vec_ref):
        i = pl.program_id(0)

        @pl.when(i == 0)
        def _():
            vec_ref[...] = jnp.zeros_like(vec_ref)

        x = x_ref[...]
        dh = dh_ref[...]
        g = g_ref[...]
        is_ctx = i < nct
        scale = jnp.where(is_ctx, m_ref[1:2, D:2 * D], m_ref[0:1, D:2 * D])
        r = lax.rsqrt(jnp.mean(x * x, axis=-1, keepdims=True) + EPS)
        n = x * r
        hn = n * g
        dsh = jnp.sum(dh, axis=0, keepdims=True)
        dsc = jnp.sum(dh * hn, axis=0, keepdims=True)
        dhn = dh * (1.0 + scale)
        vec_ref[4:5, :] += jnp.sum(dhn * n, axis=0, keepdims=True)
        dn = dhn * g
        dx = r * (dn - n * jnp.mean(dn * n, axis=-1, keepdims=True))

        @pl.when(is_ctx)
        def _():
            vec_ref[2:3, :] += dsh
            vec_ref[3:4, :] += dsc

        @pl.when(jnp.logical_not(is_ctx))
        def _():
            vec_ref[0:1, :] += dsh
            vec_ref[1:2, :] += dsc
            gx_ref[...] = dy_ref[...] + dx

    xrow = lambda i: (jnp.maximum(i - nct, 0), 0)
    return pl.pallas_call(
        body, name="prenorm_bwd", out_shape=(_sds((nl * tm, D)), _sds((8, D))), grid=(lt // tm,),
        in_specs=[pl.BlockSpec((tm, D), lambda i: (i, 0)), pl.BlockSpec((tm, D), lambda i: (i, 0)),
                  pl.BlockSpec((tm, D), xrow), _full((8, 3 * D)), _full((1, D))],
        out_specs=(pl.BlockSpec((tm, D), xrow), _full((8, D))),
        compiler_params=_params(("arbitrary",)),
    )(xa, dh, dy, mods, g_pre)


def _conv_parts(x, w, lc):
    lt = x.shape[0]
    row = lax.broadcasted_iota(jnp.int32, x.shape, 0)
    first = (row == 0) | (row == lc)
    last = (row == lc - 1) | (row == lt - 1)
    xp = jnp.where(first, 0.0, pltpu.roll(x, 1, 0))
    xn = jnp.where(last, 0.0, pltpu.roll(x, lt - 1, 0))
    y = w[0:1, :] * xp + w[1:2, :] * x + w[2:3, :] * xn
    return xp, xn, y, first, last


def _qkv_fwd(p, w_conv, lc):
    lt = p.shape[0]

    def body(p_ref, w_ref, o_ref):
        _, _, y, _, _ = _conv_parts(p_ref[...], w_ref[...], lc)
        s, _ = _silu_g(y)
        rs = lax.rsqrt(jnp.sum(s * s, axis=-1, keepdims=True) + EPS)
        o_ref[...] = s * jnp.where(pl.program_id(0) < 2 * NH, rs, 1.0)

    return pl.pallas_call(
        body, name="qkv_fwd", out_shape=_sds((lt, 3 * D)), grid=(3 * NH,),
        in_specs=[pl.BlockSpec((lt, DH), lambda j: (0, j)), pl.BlockSpec((3, DH), lambda j: (0, j))],
        out_specs=pl.BlockSpec((lt, DH), lambda j: (0, j)),
        compiler_params=_params(("parallel",)),
    )(p, w_conv)


def _qkv_bwd(p, w_conv, dqkv_f, dqkv_b, lc):
    lt = p.shape[0]

    def body(p_ref, w_ref, df_ref, db_ref, dp_ref, dw_ref):
        w = w_ref[...]
        xp, xn, y, first, last = _conv_parts(p_ref[...], w, lc)
        s, ds_dy = _silu_g(y)
        dn = df_ref[...] + db_ref[...]
        rs = lax.rsqrt(jnp.sum(s * s, axis=-1, keepdims=True) + EPS)
        nrm = s * rs
        ds_n = rs * (dn - nrm * jnp.sum(dn * nrm, axis=-1, keepdims=True))
        ds = jnp.where(pl.program_id(0) < 2 * NH, ds_n, dn)
        dy = ds * ds_dy
        dw_ref[0:1, :] = jnp.sum(dy * xp, axis=0, keepdims=True)
        dw_ref[1:2, :] = jnp.sum(dy * p_ref[...], axis=0, keepdims=True)
        dw_ref[2:3, :] = jnp.sum(dy * xn, axis=0, keepdims=True)
        dyn = jnp.where(last, 0.0, pltpu.roll(dy, lt - 1, 0))
        dyp = jnp.where(first, 0.0, pltpu.roll(dy, 1, 0))
        dp_ref[...] = (w[1:2, :] * dy + w[0:1, :] * dyn + w[2:3, :] * dyp).astype(dp_ref.dtype)

    return pl.pallas_call(
        body, name="qkv_bwd", out_shape=(_sds((lt, 3 * D), _BF), _sds((3, 3 * D))), grid=(3 * NH,),
        in_specs=[pl.BlockSpec((lt, DH), lambda j: (0, j)), pl.BlockSpec((3, DH), lambda j: (0, j)),
                  pl.BlockSpec((lt, DH), lambda j: (0, j)), pl.BlockSpec((lt, DH), lambda j: (0, j))],
        out_specs=(pl.BlockSpec((lt, DH), lambda j: (0, j)), pl.BlockSpec((3, DH), lambda j: (0, j))),
        compiler_params=_params(("parallel",)),
    )(p, w_conv, dqkv_f, dqkv_b)


def _masks(d):
    ri = lax.broadcasted_iota(jnp.int32, (CH, CH), 0)
    ci = lax.broadcasted_iota(jnp.int32, (CH, CH), 1)
    incl = (ri >= ci) if d == 0 else (ri <= ci)
    strict = (ri > ci) if d == 0 else (ri < ci)
    incl_t = (ri <= ci) if d == 0 else (ri >= ci)
    return incl, strict, incl_t, ri == ci


def _decays(d, ab, abt, alog_r, dtb_r, alog_c, dtb_c, incl, incl_t):
    g_full = -jnp.exp(alog_r) * _softplus(ab + dtb_r)
    beta_full = _sigmoid(ab)
    gc_full = _mmh(incl.astype(F32), g_full)
    gl_full = jnp.sum(g_full, axis=0, keepdims=True)
    gt_full = -jnp.exp(alog_c) * _softplus(abt + dtb_c)
    gct = _mmh(gt_full, incl_t.astype(F32))
    return g_full, beta_full, gc_full, gl_full, gt_full, gct


def _lane_onehot(idx, n=LANE):
    return (lax.broadcasted_iota(jnp.int32, (1, n), 1) == idx).astype(F32)


def _head_scalars(d, h, beta_full, gc_full, gl_full, gct):
    idx = d * NH + h
    oh = _lane_onehot(idx)
    gcol = jnp.sum(gc_full * oh, axis=-1, keepdims=True)
    bcol = jnp.sum(beta_full * _lane_onehot(2 * NH + idx), axis=-1, keepdims=True)
    gl = jnp.sum(gl_full * oh, axis=-1, keepdims=True)
    grow = gct[idx:idx + 1, :]
    return gcol, grow, bcol, gl


def _lockstep(gens):
    live = list(gens)
    while live:
        nxt = []
        for g in live:
            try:
                next(g)
                nxt.append(g)
            except StopIteration:
                pass
        live = nxt


def _chunk_local(qh, kh, vh, gcol, grow, bcol, gl, incl, strict):
    decay = jnp.where(incl, jnp.exp(gcol - grow), 0.0)
    kb = kh * bcol
    a = jnp.where(strict, _mm(kb, kh, NT) * decay, 0.0)
    egc = jnp.exp(gcol)
    rhs_u = vh * bcol
    rhs_w = kb * egc
    qs = qh * (DH ** -0.5)
    attn = jnp.where(incl, _mm(qs, kh, NT) * decay, 0.0)
    etail = jnp.exp(gl - gcol)
    return decay, kb, a, egc, rhs_u, rhs_w, qs, attn, etail


def _scan_specs(lt, lc, bwd_pass):
    nch = lt // CH
    ncc = lc // CH
    if not bwd_pass:
        cf = lambda s: s
        cb = lambda s: jnp.where(s < ncc, ncc - 1 - s, nch + ncc - 1 - s)
    else:
        cf = lambda s: nch - 1 - s
        cb = lambda s: jnp.where(s < nch - ncc, ncc + s, s - (nch - ncc))
    return nch, cf, cb


def _gdn_fwd(qkv, pab, abt, alog_r, dtb_r, alog_c, dtb_c, lc, xc, xc_arrays):
    lt = qkv.shape[0]
    nch, cf, cb = _scan_specs(lt, lc, False)
    nx = xc.n

    def body(*refs):
        qf, kf, vf, abf, abtf, qb, kb_, vb, abb, abtb, ar, dr, ac, dc = refs[:14]
        x_in = refs[14:14 + nx]
        of_ref, ob_ref, sf_ref, sb_ref, tf_ref, tb_ref = refs[14 + nx:20 + nx]
        x_out = refs[20 + nx:20 + 2 * nx]
        s_scr = refs[20 + 2 * nx]
        sems = refs[21 + 2 * nx:]

        @pl.when(pl.program_id(0) == 0)
        def _():
            s_scr[...] = jnp.zeros_like(s_scr)
            xc.start(x_in, x_out, sems)

        def chain(d, h, q_r, k_r, v_r, o_ref, sh_ref, th_ref, masks, decs):
            incl, strict, _, eye = masks
            sl = slice(h * DH, (h + 1) * DH)
            qh, kh, vh = q_r[:, sl], k_r[:, sl], v_r[:, sl]
            gcol, grow, bcol, gl = _head_scalars(d, h, *decs)
            _, _, a, egc, rhs_u, rhs_w, qs, attn, etail = _chunk_local(qh, kh, vh, gcol, grow, bcol, gl, incl, strict)
            yield
            n = -a
            t = jnp.where(eye, 1.0, 0.0) + n
            p = _mm3(n, n)
            yield
            for _ in range(4):
                r = _mm3(jnp.concatenate([t, p], axis=0), p)
                yield
                t = t + r[:CH]
                p = r[CH:]
            t = t + _mm3(t, p)
            yield
            sol = _mm3(t, jnp.concatenate([rhs_u, rhs_w], axis=1))
            u, w = sol[:, :DH], sol[:, DH:]
            s = s_scr[d, h]
            sh_ref[0, h] = s
            th_ref[0, h] = t
            yield
            ws = _mm(jnp.concatenate([w, qs * egc], axis=0), s)
            yield
            v_new = u - ws[:CH]
            o_ref[:, sl] = ws[CH:] + _mm(attn, v_new)
            s_scr[d, h] = s * jnp.exp(gl) + _mm(kh * etail, v_new, TN)

        chains = []
        for d, (q_r, k_r, v_r, ab_r, abt_r, o_ref, sh_ref, th_ref) in enumerate(
                ((qf, kf, vf, abf, abtf, of_ref, sf_ref, tf_ref), (qb, kb_, vb, abb, abtb, ob_ref, sb_ref, tb_ref))):
            masks = _masks(d)
            _, beta_full, gc_full, gl_full, _, gct = _decays(
                d, ab_r[...], abt_r[0], ar[...], dr[...], ac[...], dc[...], masks[0], masks[2])
            for h in range(NH):
                chains.append(chain(d, h, q_r, k_r, v_r, o_ref, sh_ref, th_ref, masks, (beta_full, gc_full, gl_full, gct)))
        _lockstep(chains)

        @pl.when(pl.program_id(0) == nch // 2)
        def _():
            xc.forward(x_in, x_out, sems)

        @pl.when(pl.program_id(0) == nch - 1)
        def _():
            xc.wait(x_in, x_out, sems)

    def row(c, col):
        return pl.BlockSpec((CH, D), lambda s: (c(s), col))

    def chunk_in(c):
        return [row(c, 0), row(c, 1), row(c, 2), pl.BlockSpec((CH, LANE), lambda s: (c(s), 0)),
                pl.BlockSpec((1, 4 * NH, CH), lambda s: (c(s), 0, 0))]

    def hist(c, n):
        return pl.BlockSpec((1, NH, n, n), lambda s: (c(s), 0, 0, 0))

    small = [_full((1, LANE)), _full((1, LANE)), _full((4 * NH, 1)), _full((4 * NH, 1))]
    return pl.pallas_call(
        body, name="gdn_fwd", grid=(nch,),
        out_shape=(_sds((lt, D)), _sds((lt, D)), _sds((nch, NH, DH, DH)), _sds((nch, NH, DH, DH)),
                   _sds((nch, NH, CH, CH)), _sds((nch, NH, CH, CH))) + xc.out_shape,
        in_specs=chunk_in(cf) + chunk_in(cb) + small + [_ANY] * nx,
        out_specs=(pl.BlockSpec((CH, D), lambda s: (cf(s), 0)), pl.BlockSpec((CH, D), lambda s: (cb(s), 0)),
                   hist(cf, DH), hist(cb, DH), hist(cf, CH), hist(cb, CH)) + tuple([_ANY] * nx),
        scratch_shapes=[pltpu.VMEM((2, NH, DH, DH), F32)] + xc.scratch,
        compiler_params=_params(("arbitrary",)),
    )(qkv, qkv, qkv, pab, abt, qkv, qkv, qkv, pab, abt, alog_r, dtb_r, alog_c, dtb_c, *xc_arrays)


def _gdn_bwd(qkv, pab, abt, alog_r, dtb_r, alog_c, dtb_c, s_f, s_b, t_f, t_b, do, lc, xc, xc_arrays):
    lt = qkv.shape[0]
    nch, cf, cb = _scan_specs(lt, lc, True)
    nx = xc.n

    def body(*refs):
        qf, kf, vf, abf, abtf, sf_ref, tf_ref, dof, qb, kb_, vb, abb, abtb, sb_ref, tb_ref, dob, ar, dr, ac, dc = refs[:20]
        x_in = refs[20:20 + nx]
        dqf_ref, dqb_ref, dcf_ref, dcb_ref, drf_ref, drb_ref, vcol_ref, vrow_ref = refs[20 + nx:28 + nx]
        x_out = refs[28 + nx:28 + 2 * nx]
        ds_scr = refs[28 + 2 * nx]
        sems = refs[29 + 2 * nx:]

        @pl.when(pl.program_id(0) == 0)
        def _():
            ds_scr[...] = jnp.zeros_like(ds_scr)
            vcol_ref[...] = jnp.zeros_like(vcol_ref)
            vrow_ref[...] = jnp.zeros_like(vrow_ref)
            xc.start(x_in, x_out, sems)

        alog_r_, dtb_r_, alog_c_, dtb_c_ = ar[...], dr[...], ac[...], dc[...]
        lane2 = lax.broadcasted_iota(jnp.int32, (1, LANE), 1)
        acc = [[], []]

        def chain(d, h, q_r, k_r, v_r, sh_ref, th_ref, do_r, dq_ref, masks, decs):
            incl, strict, _, _ = masks
            idx = d * NH + h
            sl = slice(h * DH, (h + 1) * DH)
            qh, kh, vh = q_r[:, sl], k_r[:, sl], v_r[:, sl]
            doh = do_r[:, sl]
            gcol, grow, bcol, gl = _head_scalars(d, h, *decs)
            decay, kb, a, egc, rhs_u, rhs_w, qs, attn, etail = _chunk_local(qh, kh, vh, gcol, grow, bcol, gl, incl, strict)
            t = th_ref[0, h]
            s = sh_ref[0, h]
            ds_new = ds_scr[d, h]
            sol = _mm3(t, jnp.concatenate([rhs_u, rhs_w], axis=1))
            u, w = sol[:, :DH], sol[:, DH:]
            q_dec = qs * egc
            k_tail = kh * etail
            egl = jnp.exp(gl)
            dv_new = _mm(attn, doh, TN) + _mm(k_tail, ds_new)
            dq_dec = _mm(doh, s, NT)
            dgl = jnp.sum(jnp.sum(ds_new * s, axis=0, keepdims=True), axis=-1, keepdims=True) * egl
            yield
            v_new = u - _mm(w, s)
            dw = -_mm(dv_new, s, NT)
            ds_scr[d, h] = ds_new * egl + _mm(q_dec, doh, TN) - _mm(w, dv_new, TN)
            yield
            dattn = jnp.where(incl, _mm(doh, v_new, NT), 0.0)
            dk_tail = _mm(v_new, ds_new, NT)
            dr = _mm3(t, jnp.concatenate([dv_new, dw], axis=1), TN)
            dr_u, dr_w = dr[:, :DH], dr[:, DH:]
            yield
            da = -jnp.where(strict, _mm3(dr, sol, NT), 0.0)
            nq = dattn * decay
            dqs = _mm(nq, kh) + dq_dec * egc
            dk = _mm(nq, qs, TN)
            yield
            dv = dr_u * bcol
            dbeta = jnp.sum(dr_u * vh, axis=-1, keepdims=True)
            dgc = jnp.sum(dr_w * rhs_w, axis=-1, keepdims=True)
            m = da * decay
            dkb = dr_w * egc + _mm(m, kh)
            dk = dk + _mm(m, kb, TN)
            pq = da * a + dattn * attn
            dgc = dgc + jnp.sum(pq, axis=-1, keepdims=True) + jnp.sum(dq_dec * q_dec, axis=-1, keepdims=True)
            dgr = -jnp.sum(pq, axis=0, keepdims=True)
            tt = jnp.sum(dk_tail * k_tail, axis=-1, keepdims=True)
            dk = dk + dk_tail * etail + dkb * bcol
            dgc = dgc - tt
            dgl = dgl + jnp.sum(tt, axis=0, keepdims=True)
            dbeta = dbeta + jnp.sum(dkb * kh, axis=-1, keepdims=True)
            dq_ref[:, sl] = dqs * (DH ** -0.5)
            dq_ref[:, D + h * DH:D + (h + 1) * DH] = dk
            dq_ref[:, 2 * D + h * DH:2 * D + (h + 1) * DH] = dv
            acc[d].append((idx, dgc, dgl, dbeta, dgr))

        dirs = ((qf, kf, vf, abf, abtf, sf_ref, tf_ref, dof, dqf_ref, dcf_ref, drf_ref),
                (qb, kb_, vb, abb, abtb, sb_ref, tb_ref, dob, dqb_ref, dcb_ref, drb_ref))
        chains, ctx_d = [], []
        for d, (q_r, k_r, v_r, ab_r, abt_r, sh_ref, th_ref, do_r, dq_ref, _, _) in enumerate(dirs):
            masks = _masks(d)
            ab, abt = ab_r[...], abt_r[0]
            g_full, beta_full, gc_full, gl_full, gt_full, gct = _decays(
                d, ab, abt, alog_r_, dtb_r_, alog_c_, dtb_c_, masks[0], masks[2])
            ctx_d.append((masks, ab, abt, g_full, beta_full, gt_full))
            for h in range(NH):
                chains.append(chain(d, h, q_r, k_r, v_r, sh_ref, th_ref, do_r, dq_ref, masks,
                                    (beta_full, gc_full, gl_full, gct)))
        _lockstep(chains)
        for d in range(2):
            (incl, _, incl_t, _), ab, abt, g_full, beta_full, gt_full = ctx_d[d]
            dcol_ref, drow_ref = dirs[d][9], dirs[d][10]
            dgc_col = jnp.zeros((CH, LANE), F32)
            dgl_row = jnp.zeros((1, LANE), F32)
            dbeta_col = jnp.zeros((CH, LANE), F32)
            dgc_row = jnp.zeros((4 * NH, CH), F32)
            for idx, dgc, dgl, dbeta, dgr in acc[d]:
                oh = _lane_onehot(idx)
                dgc_col = dgc_col + dgc * oh
                dgl_row = dgl_row + dgl * oh
                dbeta_col = dbeta_col + dbeta * _lane_onehot(2 * NH + idx)
                ohc = (lax.broadcasted_iota(jnp.int32, (4 * NH, 1), 0) == idx).astype(F32)
                dgc_row = dgc_row + ohc * dgr
            dg_col = _mmh(incl_t.astype(F32), dgc_col) + dgl_row
            dg_row = _mmh(dgc_row, incl.astype(F32))
            sg_col = _sigmoid(ab + dtb_r_)
            da_col = dg_col * (-jnp.exp(alog_r_)) * sg_col
            dcol_ref[...] = da_col + dbeta_col * beta_full * (1.0 - beta_full)
            da_row = dg_row * (-jnp.exp(alog_c_)) * _sigmoid(abt + dtb_c_)
            drow_ref[0] = da_row
            vcol_ref[0:1, :] += jnp.sum(dg_col * g_full, axis=0, keepdims=True)
            vcol_ref[1:2, :] += jnp.sum(da_col, axis=0, keepdims=True)
            rl = jnp.sum(dg_row * gt_full, axis=-1, keepdims=True)
            rd = jnp.sum(da_row, axis=-1, keepdims=True)
            vrow_ref[...] += jnp.where(lane2 == 0, rl, 0.0) + jnp.where(lane2 == 1, rd, 0.0)

        @pl.when(pl.program_id(0) == nch // 2)
        def _():
            xc.forward(x_in, x_out, sems)

        @pl.when(pl.program_id(0) == nch - 1)
        def _():
            xc.wait(x_in, x_out, sems)

    def row(c, col):
        return pl.BlockSpec((CH, D), lambda s: (c(s), col))

    def hist(c, n):
        return pl.BlockSpec((1, NH, n, n), lambda s: (c(s), 0, 0, 0))

    def chunk_in(c):
        return [row(c, 0), row(c, 1), row(c, 2), pl.BlockSpec((CH, LANE), lambda s: (c(s), 0)),
                pl.BlockSpec((1, 4 * NH, CH), lambda s: (c(s), 0, 0)), hist(c, DH), hist(c, CH), row(c, 0)]

    small = [_full((1, LANE)), _full((1, LANE)), _full((4 * NH, 1)), _full((4 * NH, 1))]
    return pl.pallas_call(
        body, name="gdn_bwd", grid=(nch,),
        out_shape=(_sds((lt, 3 * D)), _sds((lt, 3 * D)), _sds((lt, LANE)), _sds((lt, LANE)),
                   _sds((nch, 4 * NH, CH)), _sds((nch, 4 * NH, CH)), _sds((8, LANE)), _sds((4 * NH, LANE))) + xc.out_shape,
        in_specs=chunk_in(cf) + chunk_in(cb) + small + [_ANY] * nx,
        out_specs=(pl.BlockSpec((CH, 3 * D), lambda s: (cf(s), 0)), pl.BlockSpec((CH, 3 * D), lambda s: (cb(s), 0)),
                   pl.BlockSpec((CH, LANE), lambda s: (cf(s), 0)), pl.BlockSpec((CH, LANE), lambda s: (cb(s), 0)),
                   pl.BlockSpec((1, 4 * NH, CH), lambda s: (cf(s), 0, 0)), pl.BlockSpec((1, 4 * NH, CH), lambda s: (cb(s), 0, 0)),
                   _full((8, LANE)), _full((4 * NH, LANE))) + tuple([_ANY] * nx),
        scratch_shapes=[pltpu.VMEM((2, NH, DH, DH), F32)] + xc.scratch,
        compiler_params=_params(("arbitrary",)),
    )(qkv, qkv, qkv, pab, abt, s_f, t_f, do, qkv, qkv, qkv, pab, abt, s_b, t_b, do, alog_r, dtb_r, alog_c, dtb_c,
      *xc_arrays)


def _post(p, o_f, o_b, x, tgt, w_pa, w_pb, w_out, w_sp, w_spt, b_spb, ln_g, ln_b, g_on, g_post, gate_x, lc):
    lt = p.shape[0]
    l = x.shape[0]
    tm = GC
    nct = lc // tm

    def body(p_ref, of_ref, ob_ref, x_ref, t_ref, wpa, wpb, wout, wsp, wspt, bspb, lng_ref, lnb_ref, gon_ref, gpost_ref, gate_ref,
             dp_ref, do_ref, dy_ref, ya_ref, yb_ref, mg_ref, da_ref, db_ref, dout_ref, dwsp_ref, dbsp_ref, vec_ref):
        i = pl.program_id(0)

        @pl.when(i == 0)
        def _():
            dwsp_ref[...] = jnp.zeros_like(dwsp_ref)
            dbsp_ref[...] = jnp.zeros_like(dbsp_ref)
            vec_ref[...] = jnp.zeros_like(vec_ref)

        @pl.when(i < nct)
        def _():
            dp_ref[...] = jnp.zeros_like(dp_ref)
            do_ref[...] = jnp.zeros_like(do_ref)

        @pl.when(i >= nct)
        def _():
            lng, lnb, gon, gpost, gate = lng_ref[...], lnb_ref[...], gon_ref[...], gpost_ref[...], gate_ref[...]
            zb, ua, va, za, ga, gb = [p_ref[:, j * D:(j + 1) * D] for j in range(6)]
            o = of_ref[...] + ob_ref[...]
            szb, dszb = _silu_g(zb)
            nh_l, r_l = [], []
            for h in range(NH):
                oh = o[:, h * DH:(h + 1) * DH]
                r = lax.rsqrt(jnp.mean(oh * oh, axis=-1, keepdims=True) + EPS)
                nh_l.append(oh * r)
                r_l.append(r)
            nrm_b = jnp.concatenate(nh_l, axis=-1)
            gon_t = jnp.concatenate([gon] * NH, axis=-1)
            y_b = nrm_b * gon_t * szb
            u, du_dua = _gelu_g(ua)
            gv, dgv_dva = _gelu_g(va)
            xc = gv - jnp.mean(gv, axis=-1, keepdims=True)
            rs_ln = lax.rsqrt(jnp.mean(xc * xc, axis=-1, keepdims=True) + EPS)
            vhat = xc * rs_ln
            v = vhat * lng + lnb
            s_sp = jnp.concatenate(
                [_mm(wsp[g], v[:, g * DH:(g + 1) * DH]) + bspb[g] for g in range(NH)], axis=-1)
            sza, dsza = _silu_g(za)
            y_a = u * s_sp * sza
            a_pr = _mm(y_a, wpa[...])
            b_pr = _mm(y_b, wpb[...])
            sga = _sigmoid(ga)
            sgb = _sigmoid(gb)
            merged = sga * a_pr + sgb * b_pr
            out = _mm(merged, wout[...])
            rs_o = lax.rsqrt(jnp.mean(out * out, axis=-1, keepdims=True) + EPS)
            n_o = out * rs_o
            rr = n_o * gpost
            diff = x_ref[...] + gate * rr - t_ref[...]
            vec_ref[5:6, :] += jnp.sum(diff * diff, axis=0, keepdims=True)
            dy = diff * (1.0 / D)
            dy_ref[...] = dy
            vec_ref[0:1, :] += jnp.sum(dy * rr, axis=0, keepdims=True)
            dr = dy * gate
            vec_ref[1:2, :] += jnp.sum(dr * n_o, axis=0, keepdims=True)
            dn_o = dr * gpost
            dout = rs_o * (dn_o - n_o * jnp.mean(dn_o * n_o, axis=-1, keepdims=True))
            dmerged = _mm(dout, wout[...], NT)
            d_a = dmerged * sga
            d_b = dmerged * sgb
            dga = dmerged * a_pr * sga * (1.0 - sga)
            dgb = dmerged * b_pr * sgb * (1.0 - sgb)
            dy_a = _mm(d_a, wpa[...], NT)
            dy_b = _mm(d_b, wpb[...], NT)
            ya_ref[...] = y_a.astype(ya_ref.dtype)
            yb_ref[...] = y_b.astype(yb_ref.dtype)
            mg_ref[...] = merged.astype(mg_ref.dtype)
            da_ref[...] = d_a.astype(da_ref.dtype)
            db_ref[...] = d_b.astype(db_ref.dtype)
            dout_ref[...] = dout.astype(dout_ref.dtype)
            dua = dy_a * s_sp * sza * du_dua
            ds_sp = dy_a * u * sza
            dza = dy_a * u * s_sp * dsza
            dv_l = []
            for g in range(NH):
                ds_g = ds_sp[:, g * DH:(g + 1) * DH]
                dv_l.append(_mm(wspt[g], ds_g))
                dwsp_ref[g] += _mm(ds_g, v[:, g * DH:(g + 1) * DH], NT)
                dbsp_ref[g] += ds_g
            dv = jnp.concatenate(dv_l, axis=-1)
            vec_ref[2:3, :] += jnp.sum(dv * vhat, axis=0, keepdims=True)
            vec_ref[3:4, :] += jnp.sum(dv, axis=0, keepdims=True)
            dvh = dv * lng
            dgv = rs_ln * (dvh - jnp.mean(dvh, axis=-1, keepdims=True) - vhat * jnp.mean(dvh * vhat, axis=-1, keepdims=True))
            dva = dgv * dgv_dva
            dzb = dy_b * nrm_b * gon_t * dszb
            dgon_full = jnp.sum(dy_b * nrm_b * szb, axis=0, keepdims=True)
            dgon = dgon_full[:, 0:DH]
            for h in range(1, NH):
                dgon = dgon + dgon_full[:, h * DH:(h + 1) * DH]
            vec_ref[4:5, 0:DH] += dgon
            dnb = dy_b * gon_t * szb
            do_l = []
            for h in range(NH):
                sl = slice(h * DH, (h + 1) * DH)
                dn_h = dnb[:, sl]
                do_l.append(r_l[h] * (dn_h - nh_l[h] * jnp.mean(dn_h * nh_l[h], axis=-1, keepdims=True)))
            do_ref[...] = jnp.concatenate(do_l, axis=-1)
            for j, val in enumerate((dzb, dua, dva, dza, dga, dgb)):
                dp_ref[:, j * D:(j + 1) * D] = val.astype(dp_ref.dtype)

    xrow = lambda i: (jnp.maximum(i - nct, 0), 0)
    wspec = _full((D, D))
    gspec = _full((NH, GC, GC))
    vspec = _full((1, D))
    bf_out = _sds((l, D), _BF)
    return pl.pallas_call(
        body, name="post", grid=(lt // tm,),
        out_shape=(_sds((lt, NREST), _BF), _sds((lt, D)), _sds((l, D)), bf_out, bf_out, bf_out, bf_out, bf_out, bf_out,
                   _sds((NH, GC, GC)), _sds((NH, GC, GC)), _sds((8, D))),
        in_specs=[pl.BlockSpec((tm, NREST), lambda i: (i, 0)), pl.BlockSpec((tm, D), lambda i: (i, 0)),
                  pl.BlockSpec((tm, D), lambda i: (i, 0)), pl.BlockSpec((tm, D), xrow), pl.BlockSpec((tm, D), xrow),
                  wspec, wspec, wspec, gspec, gspec, gspec, vspec, vspec, _full((1, DH)), vspec, vspec],
        out_specs=(pl.BlockSpec((tm, NREST), lambda i: (i, 0)), pl.BlockSpec((tm, D), lambda i: (i, 0)),
                   pl.BlockSpec((tm, D), xrow), pl.BlockSpec((tm, D), xrow), pl.BlockSpec((tm, D), xrow),
                   pl.BlockSpec((tm, D), xrow), pl.BlockSpec((tm, D), xrow), pl.BlockSpec((tm, D), xrow),
                   pl.BlockSpec((tm, D), xrow), gspec, gspec, _full((8, D))),
        compiler_params=_params(("arbitrary",)),
    )(p, o_f, o_b, x, tgt, w_pa, w_pb, w_out, w_sp, w_spt, b_spb, ln_g, ln_b, g_on, g_post, gate_x)


def _sum_parts(parts, name):
    r = parts.shape[1]
    tr = r if NDEV * r * LANE * 4 <= (8 << 20) else _tile(r, (512, 256, 128, 64, 32, 16, 8))

    def body(p_ref, o_ref):
        acc = p_ref[0]
        for s in range(1, NDEV):
            acc = acc + p_ref[s]
        o_ref[...] = acc

    return pl.pallas_call(
        body, name=name, out_shape=_sds((r, LANE)), grid=(r // tr,),
        in_specs=[pl.BlockSpec((NDEV, tr, LANE), lambda i: (0, i, 0))],
        out_specs=pl.BlockSpec((tr, LANE), lambda i: (i, 0)),
        compiler_params=_params(("parallel",)),
    )(parts)


def _mod_bwd(c_all, c_ctx, dmx, dmc, w_mod_g):
    ws = w_mod_g.shape[2]

    def body(ca_ref, cc_ref, dsh_ref, dmx_ref, dmc_ref, dmc_sh_ref, w_ref, gw_ref, gc_ref, gb_ref):
        sc, _ = _silu_g(ca_ref[...])
        scc, dscc = _silu_g(cc_ref[...])
        dmc_tot = jnp.sum(dmc_ref[...], axis=0, keepdims=True)
        gb_ref[...] = jnp.sum(dmx_ref[...], axis=0, keepdims=True) + dmc_tot
        lhs = jnp.concatenate([sc, jnp.broadcast_to(scc, (8, D))], axis=0)
        rhs = jnp.concatenate([dsh_ref[...], dmc_sh_ref[...]], axis=0)
        gw_ref[...] = _mmh(lhs, rhs, TN)
        acc = jnp.zeros((8, D), F32)
        tot8 = jnp.broadcast_to(dmc_tot, (8, 3 * D))
        for j in range(NDEV):
            acc = acc + _mm(tot8[:, j * ws:(j + 1) * ws], w_ref[j], NT)
        gc_ref[...] = acc[0:1, :] * dscc

    return pl.pallas_call(
        body, name="mod_bwd", out_shape=(_sds((D, ws)), _sds((1, D)), _sds((1, 3 * D))),
        compiler_params=_params(),
    )(c_all, c_ctx, _my_cols(dmx, ws), dmx, dmc, _my_cols(dmc, ws), w_mod_g)


def _my_cols(a, ws):
    me = 4 * lax.axis_index("x") + 2 * lax.axis_index("y") + lax.axis_index("c")
    return lax.dynamic_slice_in_dim(a, me * ws, ws, axis=1)


def _pair_sum(mine, other, name):
    n, r, c = mine.shape
    tr = _tile(r, (256, 128, 64, 32, 16, 8))

    def body(a_ref, b_ref, o_ref):
        o_ref[...] = (a_ref[...].astype(F32) + b_ref[...].astype(F32)).astype(o_ref.dtype)

    blk = pl.BlockSpec((1, tr, c), lambda j, i: (j, i, 0))
    return pl.pallas_call(
        body, name=name, out_shape=_sds((n, r, c), mine.dtype), grid=(n, r // tr),
        in_specs=[blk, blk], out_specs=blk, compiler_params=_params(("parallel", "parallel")),
    )(mine, other)


def _adamw(parts, w, m, v, name, chip_sums_below=None):
    s_, r, c = parts.shape
    tr = _tile(r, (128, 64, 32, 16, 8)) if r * c * 4 > (1 << 20) else r
    c1 = 1.0 / (1.0 - ADAM_B1 ** ADAM_STEP)
    c2 = 1.0 / (1.0 - ADAM_B2 ** ADAM_STEP)

    def body(p_ref, w_ref, m_ref, v_ref, g_ref, d_ref, nm_ref, nv_ref):
        if chip_sums_below is None:
            part = lambda s: p_ref[s].astype(F32)
        else:
            core = lax.axis_index("c")
            me = 4 * lax.axis_index("x") + 2 * lax.axis_index("y") + core
            every = me >= chip_sums_below
            part = lambda s: jnp.where(every | (core == s % 2), p_ref[s].astype(F32), 0.0)
        g = part(0)
        for s in range(1, s_):
            g = g + part(s)
        m_new = ADAM_B1 * m_ref[...] + (1.0 - ADAM_B1) * g
        v_new = ADAM_B2 * v_ref[...] + (1.0 - ADAM_B2) * (g * g)
        g_ref[...] = g
        nm_ref[...] = m_new
        nv_ref[...] = v_new
        d_ref[...] = -ADAM_LR * ((m_new * c1) / (jnp.sqrt(v_new * c2) + ADAM_EPS) + ADAM_WD * w_ref[...])

    blk = pl.BlockSpec((tr, c), lambda i: (i, 0))
    o = _sds((r, c))
    return pl.pallas_call(
        body, name=name, out_shape=(o, o, o, o), grid=(r // tr,),
        in_specs=[pl.BlockSpec((s_, tr, c), lambda i: (0, i, 0)), blk, blk, blk],
        out_specs=(blk, blk, blk, blk),
        compiler_params=_params(("parallel",)),
    )(parts, w, m, v)


def _rows(a):
    flat = a.reshape(-1)
    n = flat.shape[0]
    r = -(-n // (8 * LANE)) * 8
    return jnp.pad(flat, (0, r * LANE - n)).reshape(r, LANE)


def _pack(items):
    parts, layout, at = [], [], 0
    for name, a in items:
        rws = _rows(a.astype(F32))
        layout.append((name, at, rws.shape[0], a.shape))
        parts.append(rws)
        at += rws.shape[0]
    return jnp.concatenate(parts, axis=0), layout


def _unpack(packed, layout):
    out = {}
    for name, at, r, shape in layout:
        n = 1
        for s in shape:
            n *= s
        out[name] = packed[at:at + r].reshape(-1)[:n].reshape(shape)
    return out


def kernel(x, c, ctx, c_ctx, w_mod, b_mod, g_pre, g_post, w_in, w_conv, a_log, dt_bias, g_onorm, gm_ln_g, gm_ln_b, w_sp, b_sp, w_pa, w_pb, w_out, loss_target, m_c_ctx, m_w_mod, m_b_mod, m_g_pre, m_g_post, m_w_in, m_w_conv, m_a_log, m_dt_bias, m_g_onorm, m_gm_ln_g, m_gm_ln_b, m_w_sp, m_b_sp, m_w_pa, m_w_pb, m_w_out, v_c_ctx, v_w_mod, v_b_mod, v_g_pre, v_g_post, v_w_in, v_w_conv, v_a_log, v_dt_bias, v_g_onorm, v_gm_ln_g, v_gm_ln_b, v_w_sp, v_b_sp, v_w_pa, v_w_pb, v_w_out):
    l = x.shape[1]
    lc = ctx.shape[1]
    lt = l + lc
    nch = lt // CH
    me = 4 * lax.axis_index("x") + 2 * lax.axis_index("y") + lax.axis_index("c")
    wsh = w_in.shape[2]
    off_a = 3 * D
    n_ab = 4 * NH
    jb = off_a // wsh
    o1 = off_a - jb * wsh
    o2 = o1 + n_ab
    assert o2 <= wsh and NREST == (NDEV - jb) * wsh - o2
    split = jb + 1

    wp = -(-wsh // LANE) * LANE
    widen = lambda a: jnp.pad(a, [(0, 0)] * (a.ndim - 1) + [(0, wp - wsh)])
    w_in_bf = widen(w_in[0].astype(_BF))
    wg_lo, wg_mod, wg_conv, c_all = _exchange(
        [w_in_bf, w_mod[0].astype(_BF), w_conv[0], c], ["gather_lo", "gather", "gather", "gather"],
        "gather_first", split)
    w_qkv = jnp.concatenate([wg_lo[j][:, :wsh] for j in range(jb)] + [wg_lo[jb][:, :o1]], axis=1)
    w_ab = jnp.pad(wg_lo[jb][:, o1:o2], ((0, 0), (0, LANE - n_ab)))
    wconv_full = jnp.moveaxis(wg_conv, 0, 1).reshape(3, 3 * D)
    c_all = c_all.reshape(NDEV, D)

    cc = jnp.concatenate([c, c_ctx.reshape(1, D), jnp.zeros((6, D), F32)], axis=0)
    mods = _modulation(cc, wg_mod, b_mod)
    xa = jnp.concatenate([ctx[0], x[0]], axis=0)
    h = _prenorm(xa, mods, g_pre, lc)
    p_qkv = _matmul_nn(h, w_qkv, "in_proj_qkv")
    pab = _matmul_nn(h, w_ab, "in_proj_ab")
    abt = jnp.swapaxes(pab[:, :n_ab].reshape(nch, CH, n_ab), 1, 2)
    alog16, dtb16 = a_log.reshape(1, 2 * NH), dt_bias.reshape(1, 2 * NH)
    alog_r = jnp.pad(alog16, ((0, 0), (0, LANE - 2 * NH)))
    dtb_r = jnp.pad(dtb16, ((0, 0), (0, LANE - 2 * NH)))
    alog_c = jnp.pad(alog16.reshape(2 * NH, 1), ((0, 2 * NH), (0, 0)))
    dtb_c = jnp.pad(dtb16.reshape(2 * NH, 1), ((0, 2 * NH), (0, 0)))
    qkv = _qkv_fwd(p_qkv, wconv_full, lc)
    late = [w_in_bf, w_pa[0].astype(_BF), w_pb[0].astype(_BF), w_out[0].astype(_BF)]
    xc_late = _Exchange(zip(late, ["gather_hi", "gather", "gather", "gather"]), split)
    o_f, o_b, s_f, s_b, t_f, t_b, wg_hi, wg_pa, wg_pb, wg_out = _gdn_fwd(
        qkv, pab, abt, alog_r, dtb_r, alog_c, dtb_c, lc, xc_late, late)
    w_rest = jnp.concatenate([wg_lo[jb][:, o2:wsh]] + [wg_hi[j][:, :wsh] for j in range(split, NDEV)], axis=1)
    wf_pa, wf_pb, wf_out = wg_pa.reshape(D, D), wg_pb.reshape(D, D), wg_out.reshape(D, D)
    p_rest = _matmul_nn(h, w_rest, "in_proj_rest")

    w_spt = jnp.swapaxes(w_sp[0], 1, 2)
    b_spb = jnp.broadcast_to(b_sp[0][:, :, None], (NH, GC, GC))
    gate_x = mods[0:1, 2 * D:]
    dp_rest, do, dy, ya, yb, mg, d_a, d_b, dout, dwsp, dbsp_l, pvec = _post(
        p_rest, o_f, o_b, x[0], loss_target[0], wf_pa, wf_pb, wf_out, w_sp[0], w_spt, b_spb, gm_ln_g, gm_ln_b,
        g_onorm, g_post, gate_x, lc)

    dw_rest = _matmul_tn(h, dp_rest, "dw_in_rest")
    o3 = wsh - o2
    chunks_hi = widen(jnp.moveaxis(dw_rest[:, o3:].reshape(D, NDEV - split, wsh), 1, 0))
    dw_pa = _matmul_tn(ya, d_a, "dw_pa").reshape(NDEV, D // NDEV, D)
    dw_pb = _matmul_tn(yb, d_b, "dw_pb").reshape(NDEV, D // NDEV, D)
    dw_out = _matmul_tn(mg, dout, "dw_out").reshape(NDEV, D // NDEV, D)
    small_a, lay_a = _pack([
        ("g_post", pvec[1]), ("g_onorm", pvec[4, :DH]), ("gm_ln_g", pvec[2]), ("gm_ln_b", pvec[3]), ("w_sp", dwsp),
        ("b_sp", jnp.sum(dbsp_l, axis=-1)), ("loss", pvec[5]), ("dgate", pvec[0])])
    (theirs_hi,) = _exchange([chunks_hi], ["sibling"], "pair_swap_hi")
    chip_hi = _pair_sum(chunks_hi, theirs_hi[0], "pair_sum_hi")
    early = [chip_hi, dw_pa, dw_pb, dw_out, small_a]
    xc_early = _Exchange(zip(early, ["scatter_par_hi", "scatter", "scatter", "scatter", "gather"]), split)

    dqkv_f, dqkv_b, dcol_f, dcol_b, drow_f, drow_b, gvec_c, gvec_r, r_in, r_pa, r_pb, r_out, small_a_all = _gdn_bwd(
        qkv, pab, abt, alog_r, dtb_r, alog_c, dtb_c, s_f, s_b, t_f, t_b, do, lc, xc_early, early)
    dp_qkv, dwconv = _qkv_bwd(p_qkv, wconv_full, dqkv_f, dqkv_b, lc)
    drow = jnp.swapaxes(drow_f + drow_b, 1, 2).reshape(lt, n_ab)
    dpab = (dcol_f + dcol_b + jnp.pad(drow, ((0, 0), (0, LANE - n_ab)))).astype(_BF)

    dw_qkv = _matmul_tn(h, dp_qkv, "dw_in_qkv")
    dw_ab = _matmul_tn(h, dpab, "dw_in_ab")
    dw_lo = jnp.concatenate([dw_qkv, dw_ab[:, :n_ab], dw_rest[:, :o3]], axis=1)
    chunks_lo = widen(jnp.moveaxis(dw_lo.reshape(D, split, wsh), 1, 0))
    (theirs,) = _exchange([chunks_lo], ["sibling"], "pair_swap")
    chip_lo = _pair_sum(chunks_lo, theirs[0], "pair_sum")
    xc_last = _Exchange([(chip_lo, "scatter_par_lo")], split)
    dh, r_in = _dh_matmul(dp_rest, dp_qkv, dpab, w_rest, w_qkv, w_ab, xc_last, [chip_lo], {0: r_in})
    grad_x, nvec = _prenorm_bwd(xa, dh, dy, mods, g_pre, lc)

    dalog = gvec_c[0, :2 * NH] + gvec_r[:2 * NH, 0]
    ddtb = gvec_c[1, :2 * NH] + gvec_r[:2 * NH, 1]
    small_b, lay_b = _pack([
        ("g_pre", nvec[4]), ("a_log", dalog), ("dt_bias", ddtb), ("w_conv", dwconv),
        ("dshift", nvec[0]), ("dscale", nvec[1]), ("dshift_c", nvec[2]), ("dscale_c", nvec[3])])
    (small_b_all,) = _exchange([small_b], ["gather"], "gather_small")
    tot = _unpack(_sum_parts(small_a_all, "sum_small_a"), lay_a)
    tot.update(_unpack(_sum_parts(small_b_all, "sum_small_b"), lay_b))

    def per_device(packed_all, layout, name):
        at, r = [(a_, r_) for nm, a_, r_, _ in layout if nm == name][0]
        return packed_all[:, at:at + r].reshape(NDEV, -1)

    dmx_all = jnp.concatenate([per_device(small_b_all, lay_b, "dshift"), per_device(small_b_all, lay_b, "dscale"),
                               per_device(small_a_all, lay_a, "dgate")], axis=1)
    dmc_all = jnp.concatenate([per_device(small_b_all, lay_b, "dshift_c"), per_device(small_b_all, lay_b, "dscale_c"),
                               jnp.zeros((NDEV, D), F32)], axis=1)
    g_wmod, g_cctx, g_bmod = _mod_bwd(c_all, c_ctx.reshape(1, D), dmx_all, dmc_all, wg_mod)
    loss = 0.5 / D * jnp.sum(tot["loss"])
    ws_conv = w_conv.shape[2]
    g_wconv = lax.dynamic_slice_in_dim(tot["w_conv"], me * ws_conv, ws_conv, axis=1)

    small_names = ["c_ctx", "b_mod", "g_pre", "g_post", "a_log", "dt_bias", "g_onorm", "gm_ln_g", "gm_ln_b",
                   "w_sp", "b_sp", "w_conv"]
    wts = dict(c_ctx=c_ctx, b_mod=b_mod, g_pre=g_pre, g_post=g_post, a_log=a_log, dt_bias=dt_bias, g_onorm=g_onorm,
               gm_ln_g=gm_ln_g, gm_ln_b=gm_ln_b, w_sp=w_sp, b_sp=b_sp, w_conv=w_conv)
    ms = dict(c_ctx=m_c_ctx, b_mod=m_b_mod, g_pre=m_g_pre, g_post=m_g_post, a_log=m_a_log, dt_bias=m_dt_bias,
              g_onorm=m_g_onorm, gm_ln_g=m_gm_ln_g, gm_ln_b=m_gm_ln_b, w_sp=m_w_sp, b_sp=m_b_sp, w_conv=m_w_conv)
    vs = dict(c_ctx=v_c_ctx, b_mod=v_b_mod, g_pre=v_g_pre, g_post=v_g_post, a_log=v_a_log, dt_bias=v_dt_bias,
              g_onorm=v_g_onorm, gm_ln_g=v_gm_ln_g, gm_ln_b=v_gm_ln_b, w_sp=v_w_sp, b_sp=v_b_sp, w_conv=v_w_conv)
    gs = dict(tot)
    gs.update(c_ctx=g_cctx, b_mod=g_bmod, w_conv=g_wconv)
    gpk, play = _pack([(nm, gs[nm].reshape(wts[nm].shape)) for nm in small_names])
    wpk, _ = _pack([(nm, wts[nm]) for nm in small_names])
    mpk, _ = _pack([(nm, ms[nm]) for nm in small_names])
    vpk, _ = _pack([(nm, vs[nm]) for nm in small_names])
    res_small = [_unpack(a, play) for a in _adamw(gpk[None], wpk, mpk, vpk, "adamw_small")]
    res_big = {
        "w_mod": _adamw(g_wmod[None], w_mod[0], m_w_mod[0], v_w_mod[0], "adamw_w_mod"),
        "w_in": [a[:, :wsh] for a in _adamw(r_in, widen(w_in[0]), widen(m_w_in[0]), widen(v_w_in[0]), "adamw_w_in",
                                            chip_sums_below=NDEV)],
        "w_pa": _adamw(r_pa, w_pa[0], m_w_pa[0], v_w_pa[0], "adamw_w_pa"),
        "w_pb": _adamw(r_pb, w_pb[0], m_w_pb[0], v_w_pb[0], "adamw_w_pb"),
        "w_out": _adamw(r_out, w_out[0], m_w_out[0], v_w_out[0], "adamw_w_out"),
    }
    order = ["c_ctx", "w_mod", "b_mod", "g_pre", "g_post", "w_in", "w_conv", "a_log", "dt_bias", "g_onorm",
             "gm_ln_g", "gm_ln_b", "w_sp", "b_sp", "w_pa", "w_pb", "w_out"]
    outs = [loss, grad_x[None]]
    for k in range(4):
        for nm in order:
            if nm in res_big:
                outs.append(res_big[nm][k][None])
            else:
                outs.append(res_small[k][nm])
    return tuple(outs)
```

```python
import functools

import jax
import jax.numpy as jnp
from jax import lax
from jax.experimental import pallas as pl
from jax.experimental.pallas import tpu as pltpu

F32 = jnp.float32
_BF = jnp.bfloat16
_HI = lax.Precision.HIGHEST
D = 1024
NH = 8
DH = 128
CH = 64
GC = 128
NREST = 6 * D
NMAIN = NREST + 3 * D
EPS = 1e-6
LANE = 128
NDEV = 8
VMEM_LIMIT = 56 * 1024 * 1024
MESH = pl.DeviceIdType.MESH

ADAM_LR, ADAM_B1, ADAM_B2, ADAM_EPS, ADAM_WD, ADAM_STEP = 0.001, 0.9, 0.999, 1e-08, 0.01, 10

NN = ((1,), (0,))
NT = ((1,), (1,))
TN = ((0,), (0,))


def _dot(a, b, dims=NN, prec=None):
    return lax.dot_general(a, b, (dims, ((), ())), precision=prec, preferred_element_type=F32)


def _mm(a, b, dims=NN):
    return _dot(a.astype(_BF), b.astype(_BF), dims)


def _mmh(a, b, dims=NN):
    return _dot(a.astype(F32), b.astype(F32), dims, _HI)


def _split(a):
    hi = a.astype(_BF)
    return hi, (a - hi.astype(F32)).astype(_BF)


def _mm3(a, b, dims=NN):
    ah, al = _split(a)
    bh, bl = _split(b)
    return _dot(ah, bh, dims) + (_dot(ah, bl, dims) + _dot(al, bh, dims))


def _sigmoid(x):
    return 1.0 / (1.0 + jnp.exp(-x))


def _silu_g(x):
    s = _sigmoid(x)
    return x * s, s * (1.0 + x * (1.0 - s))


def _gelu_g(x):
    c = 0.7978845608028654
    t = jnp.tanh(c * (x + 0.044715 * (x * x * x)))
    cdf = 0.5 * (1.0 + t)
    return x * cdf, cdf + 0.5 * x * (1.0 - t * t) * c * (1.0 + 3 * 0.044715 * x * x)


def _softplus(x):
    return jnp.maximum(x, 0.0) + jnp.log(1.0 + jnp.exp(-jnp.abs(x)))


def _params(sem=None):
    return pltpu.CompilerParams(dimension_semantics=sem, vmem_limit_bytes=VMEM_LIMIT)


def _tile(n, pref):
    for t in pref:
        if n % t == 0:
            return t
    return n


def _full(shape):
    nd = len(shape)
    return pl.BlockSpec(shape, lambda *_: (0,) * nd)


def _sds(shape, dt=F32):
    return jax.ShapeDtypeStruct(shape, dt)


MAX_PIECES = 12
PIECE_BYTES = 256 * 1024


def _piece_slices(shape, itemsize):
    total = itemsize
    for d in shape:
        total *= d
    want = min(MAX_PIECES, total // PIECE_BYTES)
    lead = shape[0] if len(shape) >= 3 else 1
    rows = shape[-2] if len(shape) >= 2 else 1
    if want < 2 or lead > want:
        return [()]
    m = max([n for n in (8, 4, 2, 1) if n * lead <= want and rows % (16 * n) == 0], default=1)
    if m * lead < 2:
        return [()]
    rs = rows // m
    mid = (slice(None),) * max(len(shape) - 3, 0)
    if len(shape) >= 3:
        return [(i,) + mid + (pl.ds(j * rs, rs),) for i in range(lead) for j in range(m)]
    return [(pl.ds(j * rs, rs),) for j in range(m)]


class _Pieces:
    def __init__(self, copies):
        self.copies = copies

    def start(self):
        for cp in self.copies:
            cp.start()

    def wait_send(self):
        for cp in self.copies:
            cp.wait_send()

    def wait_recv(self):
        for cp in self.copies:
            cp.wait_recv()

    def wait(self):
        for cp in self.copies:
            cp.wait()


class _Exchange:
    def __init__(self, specs, split):
        self.specs = list(specs)
        self.split = split
        self.n = len(self.specs)
        def out(a, k):
            if k == "sibling":
                return (1,) + tuple(a.shape)
            return (NDEV,) + (tuple(a.shape) if k.startswith("gather") else tuple(a.shape[1:]))

        self.out_shape = tuple(_sds(out(a, k), a.dtype) for a, k in self.specs)
        self.pieces = [_piece_slices(o.shape[1:], jnp.dtype(o.dtype).itemsize) for o in self.out_shape]
        self.sem_base = [(NDEV - 1) * sum(len(p) for p in self.pieces[:a]) for a in range(self.n + 1)]
        self.scratch = [pltpu.SemaphoreType.DMA((self.sem_base[-1],)), pltpu.SemaphoreType.DMA((self.sem_base[-1],)),
                        pltpu.SemaphoreType.DMA((self.sem_base[-1] // (NDEV - 1),))]

    def _local(self, sems, a, src, dst):
        base = self.sem_base[a] // (NDEV - 1)
        return _Pieces([pltpu.make_async_copy(src.at[sl] if sl else src, dst.at[sl] if sl else dst, sems[2].at[base + p])
                        for p, sl in enumerate(self.pieces[a])])

    def _remote(self, sems, a, k, src, dst, to):
        send_sems, recv_sems, _ = sems
        base = self.sem_base[a] + k * len(self.pieces[a])
        return _Pieces([
            pltpu.make_async_remote_copy(
                src_ref=src.at[sl] if sl else src, dst_ref=dst.at[sl] if sl else dst, send_sem=send_sems.at[base + p],
                recv_sem=recv_sems.at[base + p], device_id=to, device_id_type=MESH)
            for p, sl in enumerate(self.pieces[a])])

    def _ok(self, kind, idx):
        if kind.endswith("_lo"):
            return idx < self.split
        if kind.endswith("_hi"):
            return idx >= self.split
        return True

    def _phases(self, ins, outs, sems):
        x, y, c = lax.axis_index("x"), lax.axis_index("y"), lax.axis_index("c")
        me = 4 * x + 2 * y + c
        sib = (x, y, 1 - c)
        sib_idx = 4 * x + 2 * y + (1 - c)
        chips = [(1 - x, y), (x, 1 - y), (1 - x, 1 - y)]
        starts, forwards, waits = [], [], []
        for a, (_, kind) in enumerate(self.specs):
            ok = functools.partial(self._ok, kind)
            if kind.startswith("gather"):
                def copy(k, block, to, src=None, a=a):
                    rows = outs[a].at[block]
                    return self._remote(sems, a, k, rows if src is None else src, rows, to)

                loc = self._local(sems, a, ins[a], outs[a].at[me])
                first = [copy(0, me, sib, ins[a])] + [copy(1 + j, me, (*chip, c), ins[a]) for j, chip in enumerate(chips)]
                starts += [(ok(me), loc.start)] + [(ok(me), cp.start) for cp in first]
                waits += [(ok(me), loc.wait)] + [(ok(me), cp.wait_send) for cp in first]
                for j, chip in enumerate(chips):
                    origin = 4 * chip[0] + 2 * chip[1] + c
                    passed = copy(4 + j, origin, sib)
                    forwards += [(ok(origin), copy(1 + j, origin, sib).wait_recv), (ok(origin), passed.start)]
                    waits.append((ok(origin), passed.wait_send))
                    other = 4 * chip[0] + 2 * chip[1] + (1 - c)
                    waits.append((ok(other), copy(4 + j, other, sib).wait_recv))
                waits.append((ok(sib_idx), copy(0, sib_idx, sib).wait_recv))
            elif kind == "sibling":
                swap = self._remote(sems, a, 0, ins[a], outs[a].at[0], sib)
                starts.append((True, swap.start))
                waits += [(True, swap.wait_send), (True, swap.wait_recv)]
            else:
                base = self.split if kind.endswith("_hi") else 0
                same_core_only = "_par" in kind

                def src(idx, a=a, base=base):
                    return ins[a].at[jnp.clip(idx - base, 0, ins[a].shape[0] - 1)]

                loc = self._local(sems, a, src(me), outs[a].at[me])
                starts.append((ok(me), loc.start))
                waits.append((ok(me), loc.wait))
                for k in range(1, NDEV):
                    if same_core_only and k & 1:
                        continue
                    px = 1 - x if (k >> 2) & 1 else x
                    py = 1 - y if (k >> 1) & 1 else y
                    pc = 1 - c if k & 1 else c
                    pidx = 4 * px + 2 * py + pc
                    send = self._remote(sems, a, k - 1, src(pidx), outs[a].at[me], (px, py, pc))
                    arrive = self._remote(sems, a, k - 1, src(pidx), outs[a].at[pidx], (px, py, pc))
                    starts.append((ok(pidx), send.start))
                    waits += [(ok(pidx), send.wait_send), (ok(me), arrive.wait_recv)]
        return starts, forwards, waits

    @staticmethod
    def _run(actions):
        for cond, fn in actions:
            if cond is True:
                fn()
            else:
                pl.when(cond)(fn)

    def start(self, ins, outs, sems):
        self._run(self._phases(ins, outs, sems)[0])

    def forward(self, ins, outs, sems):
        self._run(self._phases(ins, outs, sems)[1])

    def wait(self, ins, outs, sems):
        self._run(self._phases(ins, outs, sems)[2])


_ANY = pl.BlockSpec(memory_space=pl.ANY)


def _exchange(arrays, kinds, name, split=0, into=None):
    xc = _Exchange(zip(arrays, kinds), split)
    n = xc.n
    into = into or {}
    ni = len(into)

    def body(*refs):
        ins, outs, sems = refs[:n], refs[n + ni:2 * n + ni], refs[2 * n + ni:]
        xc.start(ins, outs, sems)
        xc.forward(ins, outs, sems)
        xc.wait(ins, outs, sems)

    return pl.pallas_call(
        body, name=name, out_shape=xc.out_shape, in_specs=[_ANY] * (n + ni), out_specs=tuple([_ANY] * n),
        scratch_shapes=xc.scratch, input_output_aliases={n + t: a for t, a in enumerate(into)},
    )(*arrays, *into.values())


def _matmul_nn(a, b, name):
    m, kk = a.shape
    n = b.shape[1]
    tm = m if m * kk * a.dtype.itemsize <= (12 << 20) else _tile(m, (1088, 1024, 640, 512, 256, 128))
    tn = _tile(n, (512, 256, 128))

    def body(a_ref, b_ref, o_ref):
        o_ref[...] = _mm(a_ref[...], b_ref[...])

    return pl.pallas_call(
        body, name=name, out_shape=_sds((m, n)), grid=(n // tn, m // tm),
        in_specs=[pl.BlockSpec((tm, kk), lambda j, i: (i, 0)), pl.BlockSpec((kk, tn), lambda j, i: (0, j))],
        out_specs=pl.BlockSpec((tm, tn), lambda j, i: (i, j)),
        compiler_params=_params(("parallel", "parallel")),
    )(a, b)


def _matmul_tn(a, b, name):
    kk, m = a.shape
    n = b.shape[1]
    tk = _tile(kk, (1088, 1024, 640, 512, 256, 128))
    tn = _tile(n, (1024, 512, 256, 128))
    nk = kk // tk

    def body(a_ref, b_ref, o_ref, acc_ref):
        k = pl.program_id(1)

        @pl.when(k == 0)
        def _():
            acc_ref[...] = jnp.zeros_like(acc_ref)

        acc_ref[...] += _mm(a_ref[...], b_ref[...], TN)

        @pl.when(k == nk - 1)
        def _():
            o_ref[...] = acc_ref[...].astype(o_ref.dtype)

    return pl.pallas_call(
        body, name=name, out_shape=_sds((m, n), _BF), grid=(n // tn, nk),
        in_specs=[pl.BlockSpec((tk, m), lambda j, k: (k, 0)), pl.BlockSpec((tk, tn), lambda j, k: (k, j))],
        out_specs=pl.BlockSpec((m, tn), lambda j, k: (0, j)),
        scratch_shapes=[pltpu.VMEM((m, tn), F32)],
        compiler_params=_params(("parallel", "arbitrary")),
    )(a, b)


def _dh_matmul(dp_rest, dp_qkv, dpab, w_rest, w_qkv, w_ab, xc, xc_arrays, xc_into):
    lt = dp_rest.shape[0]
    tm = _tile(lt, (1088, 1024, 640, 512, 256, 128))
    nr, nq = dp_rest.shape[1] // D, dp_qkv.shape[1] // D
    nx, ni = xc.n, len(xc_into)
    ni_steps = lt // tm

    def body(*refs):
        dr_ref, dq_ref, ab_ref, wr_ref, wq_ref, wab_ref = refs[:6]
        x_in = refs[6:6 + nx]
        o_ref = refs[6 + nx + ni]
        x_out = refs[7 + nx + ni:7 + 2 * nx + ni]
        sems = refs[7 + 2 * nx + ni:]
        i = pl.program_id(0)
        k = pl.program_id(1)

        @pl.when((i == 0) & (k == 0))
        def _():
            xc.start(x_in, x_out, sems)

        @pl.when(k == 0)
        def _():
            o_ref[...] = _mm(ab_ref[...], wab_ref[...], NT)

        @pl.when(k < nr)
        def _():
            o_ref[...] += _mm(dr_ref[...], wr_ref[...], NT)

        @pl.when(k >= nr)
        def _():
            o_ref[...] += _mm(dq_ref[...], wq_ref[...], NT)

        @pl.when((i == ni_steps - 1) & (k == nr + nq - 1))
        def _():
            xc.wait(x_in, x_out, sems)

    rk = lambda k: jnp.minimum(k, nr - 1)
    qk = lambda k: jnp.maximum(k - nr, 0)
    return pl.pallas_call(
        body, name="dh_matmul", out_shape=(_sds((lt, D)),) + xc.out_shape, grid=(lt // tm, nr + nq),
        in_specs=[pl.BlockSpec((tm, D), lambda i, k: (i, rk(k))), pl.BlockSpec((tm, D), lambda i, k: (i, qk(k))),
                  pl.BlockSpec((tm, LANE), lambda i, k: (i, 0)),
                  pl.BlockSpec((D, D), lambda i, k: (0, rk(k))), pl.BlockSpec((D, D), lambda i, k: (0, qk(k))),
                  _full((D, LANE))] + [_ANY] * (nx + ni),
        out_specs=(pl.BlockSpec((tm, D), lambda i, k: (i, 0)),) + tuple([_ANY] * nx),
        scratch_shapes=xc.scratch, input_output_aliases={6 + nx + t: 1 + a for t, a in enumerate(xc_into)},
        compiler_params=_params(("arbitrary", "arbitrary")),
    )(dp_rest, dp_qkv, dpab, w_rest, w_qkv, w_ab, *xc_arrays, *xc_into.values())


def _modulation(cc, w_mod_g, b_mod):
    ws = w_mod_g.shape[2]

    def body(c_ref, w_ref, b_ref, o_ref):
        s, _ = _silu_g(c_ref[...])
        o_ref[...] = _mm(s, w_ref[0]) + b_ref[...]

    return pl.pallas_call(
        body, name="modulation", out_shape=_sds((8, 3 * D)), grid=(NDEV,),
        in_specs=[_full((8, D)), pl.BlockSpec((1, D, ws), lambda j: (j, 0, 0)), pl.BlockSpec((1, ws), lambda j: (0, j))],
        out_specs=pl.BlockSpec((8, ws), lambda j: (0, j)),
        compiler_params=_params(("parallel",)),
    )(cc, w_mod_g, b_mod)


def _prenorm(ctx, x, mods, g_pre):
    lc = ctx.shape[0]
    lt = lc + x.shape[0]
    tm = _tile(lc, (256, 128))
    nct = lc // tm

    def body(c_ref, x_ref, m_ref, g_ref, o_ref):
        is_ctx = pl.program_id(0) < nct
        x = jnp.where(is_ctx, c_ref[...], x_ref[...])
        shift = jnp.where(is_ctx, m_ref[1:2, 0:D], m_ref[0:1, 0:D])
        scale = jnp.where(is_ctx, m_ref[1:2, D:2 * D], m_ref[0:1, D:2 * D])
        r = lax.rsqrt(jnp.mean(x * x, axis=-1, keepdims=True) + EPS)
        o_ref[...] = ((x * r * g_ref[...]) * (1.0 + scale) + shift).astype(o_ref.dtype)

    return pl.pallas_call(
        body, name="prenorm", out_shape=_sds((lt, D), _BF), grid=(lt // tm,),
        in_specs=[pl.BlockSpec((tm, D), lambda i: (jnp.minimum(i, nct - 1), 0)),
                  pl.BlockSpec((tm, D), lambda i: (jnp.maximum(i - nct, 0), 0)), _full((8, 3 * D)), _full((1, D))],
        out_specs=pl.BlockSpec((tm, D), lambda i: (i, 0)),
        compiler_params=_params(("parallel",)),
    )(ctx, x, mods, g_pre)


def _prenorm_bwd(ctx, x, dh, dy, mods, g_pre):
    lc = ctx.shape[0]
    lt = lc + x.shape[0]
    tm = _tile(lc, (256, 128))
    nct = lc // tm
    nl = (lt - lc) // tm

    def body(c_ref, x_ref, dh_ref, dy_ref, m_ref, g_ref, gx_ref, vec_ref):
        i = pl.program_id(0)

        @pl.when(i == 0)
        def _():
            vec_ref[...] = jnp.zeros_like(vec_ref)

        is_ctx = i < nct
        x = jnp.where(is_ctx, c_ref[...], x_ref[...])
        dh = dh_ref[...]
        g = g_ref[...]
        scale = jnp.where(is_ctx, m_ref[1:2, D:2 * D], m_ref[0:1, D:2 * D])
        r = lax.rsqrt(jnp.mean(x * x, axis=-1, keepdims=True) + EPS)
        n = x * r
        hn = n * g
        dsh = jnp.sum(dh, axis=0, keepdims=True)
        dsc = jnp.sum(dh * hn, axis=0, keepdims=True)
        dhn = dh * (1.0 + scale)
        vec_ref[4:5, :] += jnp.sum(dhn * n, axis=0, keepdims=True)
        dn = dhn * g
        dx = r * (dn - n * jnp.mean(dn * n, axis=-1, keepdims=True))

        @pl.when(is_ctx)
        def _():
            vec_ref[2:3, :] += dsh
            vec_ref[3:4, :] += dsc

        @pl.when(jnp.logical_not(is_ctx))
        def _():
            vec_ref[0:1, :] += dsh
            vec_ref[1:2, :] += dsc
            gx_ref[...] = dy_ref[...] + dx

    xrow = lambda i: (jnp.maximum(i - nct, 0), 0)
    return pl.pallas_call(
        body, name="prenorm_bwd", out_shape=(_sds((nl * tm, D)), _sds((8, D))), grid=(lt // tm,),
        in_specs=[pl.BlockSpec((tm, D), lambda i: (jnp.minimum(i, nct - 1), 0)), pl.BlockSpec((tm, D), xrow),
                  pl.BlockSpec((tm, D), lambda i: (i, 0)), pl.BlockSpec((tm, D), xrow), _full((8, 3 * D)), _full((1, D))],
        out_specs=(pl.BlockSpec((tm, D), xrow), _full((8, D))),
        compiler_params=_params(("arbitrary",)),
    )(ctx, x, dh, dy, mods, g_pre)


def _conv_parts(x, w, lc):
    lt = x.shape[0]
    row = lax.broadcasted_iota(jnp.int32, x.shape, 0)
    first = (row == 0) | (row == lc)
    last = (row == lc - 1) | (row == lt - 1)
    xp = jnp.where(first, 0.0, pltpu.roll(x, 1, 0))
    xn = jnp.where(last, 0.0, pltpu.roll(x, lt - 1, 0))
    y = w[0:1, :] * xp + w[1:2, :] * x + w[2:3, :] * xn
    return xp, xn, y, first, last


def _qkv_fwd(p, w_conv, lc):
    lt = p.shape[0]

    def body(p_ref, w_ref, o_ref):
        _, _, y, _, _ = _conv_parts(p_ref[...], w_ref[...], lc)
        s, _ = _silu_g(y)
        rs = lax.rsqrt(jnp.sum(s * s, axis=-1, keepdims=True) + EPS)
        o_ref[...] = s * jnp.where(pl.program_id(0) < 2 * NH, rs, 1.0)

    return pl.pallas_call(
        body, name="qkv_fwd", out_shape=_sds((lt, 3 * D)), grid=(3 * NH,),
        in_specs=[pl.BlockSpec((lt, DH), lambda j: (0, j)), pl.BlockSpec((3, DH), lambda j: (0, j))],
        out_specs=pl.BlockSpec((lt, DH), lambda j: (0, j)),
        compiler_params=_params(("parallel",)),
    )(p, w_conv)


def _qkv_bwd(p, w_conv, dqkv_f, dqkv_b, lc):
    lt = p.shape[0]

    def body(p_ref, w_ref, df_ref, db_ref, dp_ref, dw_ref):
        w = w_ref[...]
        xp, xn, y, first, last = _conv_parts(p_ref[...], w, lc)
        s, ds_dy = _silu_g(y)
        dn = df_ref[...] + db_ref[...]
        rs = lax.rsqrt(jnp.sum(s * s, axis=-1, keepdims=True) + EPS)
        nrm = s * rs
        ds_n = rs * (dn - nrm * jnp.sum(dn * nrm, axis=-1, keepdims=True))
        ds = jnp.where(pl.program_id(0) < 2 * NH, ds_n, dn)
        dy = ds * ds_dy
        dw_ref[0:1, :] = jnp.sum(dy * xp, axis=0, keepdims=True)
        dw_ref[1:2, :] = jnp.sum(dy * p_ref[...], axis=0, keepdims=True)
        dw_ref[2:3, :] = jnp.sum(dy * xn, axis=0, keepdims=True)
        dyn = jnp.where(last, 0.0, pltpu.roll(dy, lt - 1, 0))
        dyp = jnp.where(first, 0.0, pltpu.roll(dy, 1, 0))
        dp_ref[...] = (w[1:2, :] * dy + w[0:1, :] * dyn + w[2:3, :] * dyp).astype(dp_ref.dtype)

    return pl.pallas_call(
        body, name="qkv_bwd", out_shape=(_sds((lt, 3 * D), _BF), _sds((3, 3 * D))), grid=(3 * NH,),
        in_specs=[pl.BlockSpec((lt, DH), lambda j: (0, j)), pl.BlockSpec((3, DH), lambda j: (0, j)),
                  pl.BlockSpec((lt, DH), lambda j: (0, j)), pl.BlockSpec((lt, DH), lambda j: (0, j))],
        out_specs=(pl.BlockSpec((lt, DH), lambda j: (0, j)), pl.BlockSpec((3, DH), lambda j: (0, j))),
        compiler_params=_params(("parallel",)),
    )(p, w_conv, dqkv_f, dqkv_b)


def _masks(d):
    ri = lax.broadcasted_iota(jnp.int32, (CH, CH), 0)
    ci = lax.broadcasted_iota(jnp.int32, (CH, CH), 1)
    incl = (ri >= ci) if d == 0 else (ri <= ci)
    strict = (ri > ci) if d == 0 else (ri < ci)
    incl_t = (ri <= ci) if d == 0 else (ri >= ci)
    return incl, strict, incl_t, ri == ci


def _decays(d, ab, abt, alog_r, dtb_r, alog_c, dtb_c, incl, incl_t):
    g_full = -jnp.exp(alog_r) * _softplus(ab + dtb_r)
    beta_full = _sigmoid(ab)
    gc_full = _mmh(incl.astype(F32), g_full)
    gl_full = jnp.sum(g_full, axis=0, keepdims=True)
    gt_full = -jnp.exp(alog_c) * _softplus(abt + dtb_c)
    gct = _mmh(gt_full, incl_t.astype(F32))
    return g_full, beta_full, gc_full, gl_full, gt_full, gct


def _lane_onehot(idx, n=LANE):
    return (lax.broadcasted_iota(jnp.int32, (1, n), 1) == idx).astype(F32)


def _head_scalars(d, h, beta_full, gc_full, gl_full, gct):
    idx = d * NH + h
    oh = _lane_onehot(idx)
    gcol = jnp.sum(gc_full * oh, axis=-1, keepdims=True)
    bcol = jnp.sum(beta_full * _lane_onehot(2 * NH + idx), axis=-1, keepdims=True)
    gl = jnp.sum(gl_full * oh, axis=-1, keepdims=True)
    grow = gct[idx:idx + 1, :]
    return gcol, grow, bcol, gl


def _lockstep(gens):
    live = list(gens)
    while live:
        nxt = []
        for g in live:
            try:
                next(g)
                nxt.append(g)
            except StopIteration:
                pass
        live = nxt


def _chunk_local(qh, kh, vh, gcol, grow, bcol, gl, incl, strict):
    decay = jnp.where(incl, jnp.exp(gcol - grow), 0.0)
    kb = kh * bcol
    a = jnp.where(strict, _mm(kb, kh, NT) * decay, 0.0)
    egc = jnp.exp(gcol)
    rhs_u = vh * bcol
    rhs_w = kb * egc
    qs = qh * (DH ** -0.5)
    attn = jnp.where(incl, _mm(qs, kh, NT) * decay, 0.0)
    etail = jnp.exp(gl - gcol)
    return decay, kb, a, egc, rhs_u, rhs_w, qs, attn, etail


def _scan_specs(lt, lc, bwd_pass):
    nch = lt // CH
    ncc = lc // CH
    if not bwd_pass:
        cf = lambda s: s
        cb = lambda s: jnp.where(s < ncc, ncc - 1 - s, nch + ncc - 1 - s)
    else:
        cf = lambda s: nch - 1 - s
        cb = lambda s: jnp.where(s < nch - ncc, ncc + s, s - (nch - ncc))
    return nch, cf, cb


def _gdn_fwd(qkv, pab, abt, alog_r, dtb_r, alog_c, dtb_c, lc, xc, xc_arrays):
    lt = qkv.shape[0]
    nch, cf, cb = _scan_specs(lt, lc, False)
    nx = xc.n

    def body(*refs):
        qf, kf, vf, abf, abtf, qb, kb_, vb, abb, abtb, ar, dr, ac, dc = refs[:14]
        x_in = refs[14:14 + nx]
        of_ref, ob_ref, sf_ref, sb_ref, tf_ref, tb_ref = refs[14 + nx:20 + nx]
        x_out = refs[20 + nx:20 + 2 * nx]
        s_scr = refs[20 + 2 * nx]
        sems = refs[21 + 2 * nx:]

        @pl.when(pl.program_id(0) == 0)
        def _():
            s_scr[...] = jnp.zeros_like(s_scr)
            xc.start(x_in, x_out, sems)

        def chain(d, h, q_r, k_r, v_r, o_ref, sh_ref, th_ref, masks, decs):
            incl, strict, _, eye = masks
            sl = slice(h * DH, (h + 1) * DH)
            qh, kh, vh = q_r[:, sl], k_r[:, sl], v_r[:, sl]
            gcol, grow, bcol, gl = _head_scalars(d, h, *decs)
            _, _, a, egc, rhs_u, rhs_w, qs, attn, etail = _chunk_local(qh, kh, vh, gcol, grow, bcol, gl, incl, strict)
            yield
            n = -a
            t = jnp.where(eye, 1.0, 0.0) + n
            p = _mm3(n, n)
            yield
            for _ in range(4):
                r = _mm3(jnp.concatenate([t, p], axis=0), p)
                yield
                t = t + r[:CH]
                p = r[CH:]
            t = t + _mm3(t, p)
            yield
            sol = _mm3(t, jnp.concatenate([rhs_u, rhs_w], axis=1))
            u, w = sol[:, :DH], sol[:, DH:]
            s = s_scr[d, h]
            sh_ref[0, h] = s
            th_ref[0, h] = t
            yield
            ws = _mm(jnp.concatenate([w, qs * egc], axis=0), s)
            yield
            v_new = u - ws[:CH]
            o_ref[:, sl] = ws[CH:] + _mm(attn, v_new)
            s_scr[d, h] = s * jnp.exp(gl) + _mm(kh * etail, v_new, TN)

        chains = []
        for d, (q_r, k_r, v_r, ab_r, abt_r, o_ref, sh_ref, th_ref) in enumerate(
                ((qf, kf, vf, abf, abtf, of_ref, sf_ref, tf_ref), (qb, kb_, vb, abb, abtb, ob_ref, sb_ref, tb_ref))):
            masks = _masks(d)
            _, beta_full, gc_full, gl_full, _, gct = _decays(
                d, ab_r[...], abt_r[0], ar[...], dr[...], ac[...], dc[...], masks[0], masks[2])
            for h in range(NH):
                chains.append(chain(d, h, q_r, k_r, v_r, o_ref, sh_ref, th_ref, masks, (beta_full, gc_full, gl_full, gct)))
        _lockstep(chains)

        @pl.when(pl.program_id(0) == nch // 2)
        def _():
            xc.forward(x_in, x_out, sems)

        @pl.when(pl.program_id(0) == nch - 1)
        def _():
            xc.wait(x_in, x_out, sems)

    def row(c, col):
        return pl.BlockSpec((CH, D), lambda s: (c(s), col))

    def chunk_in(c):
        return [row(c, 0), row(c, 1), row(c, 2), pl.BlockSpec((CH, LANE), lambda s: (c(s), 0)),
                pl.BlockSpec((1, 4 * NH, CH), lambda s: (c(s), 0, 0))]

    def hist(c, n):
        return pl.BlockSpec((1, NH, n, n), lambda s: (c(s), 0, 0, 0))

    small = [_full((1, LANE)), _full((1, LANE)), _full((4 * NH, 1)), _full((4 * NH, 1))]
    return pl.pallas_call(
        body, name="gdn_fwd", grid=(nch,),
        out_shape=(_sds((lt, D)), _sds((lt, D)), _sds((nch, NH, DH, DH)), _sds((nch, NH, DH, DH)),
                   _sds((nch, NH, CH, CH)), _sds((nch, NH, CH, CH))) + xc.out_shape,
        in_specs=chunk_in(cf) + chunk_in(cb) + small + [_ANY] * nx,
        out_specs=(pl.BlockSpec((CH, D), lambda s: (cf(s), 0)), pl.BlockSpec((CH, D), lambda s: (cb(s), 0)),
                   hist(cf, DH), hist(cb, DH), hist(cf, CH), hist(cb, CH)) + tuple([_ANY] * nx),
        scratch_shapes=[pltpu.VMEM((2, NH, DH, DH), F32)] + xc.scratch,
        compiler_params=_params(("arbitrary",)),
    )(qkv, qkv, qkv, pab, abt, qkv, qkv, qkv, pab, abt, alog_r, dtb_r, alog_c, dtb_c, *xc_arrays)


def _gdn_bwd(qkv, pab, abt, alog_r, dtb_r, alog_c, dtb_c, s_f, s_b, t_f, t_b, do, lc, xc, xc_arrays):
    lt = qkv.shape[0]
    nch, cf, cb = _scan_specs(lt, lc, True)
    nx = xc.n

    def body(*refs):
        qf, kf, vf, abf, abtf, sf_ref, tf_ref, dof, qb, kb_, vb, abb, abtb, sb_ref, tb_ref, dob, ar, dr, ac, dc = refs[:20]
        x_in = refs[20:20 + nx]
        dqf_ref, dqb_ref, dcf_ref, dcb_ref, drf_ref, drb_ref, vcol_ref, vrow_ref = refs[20 + nx:28 + nx]
        x_out = refs[28 + nx:28 + 2 * nx]
        ds_scr = refs[28 + 2 * nx]
        sems = refs[29 + 2 * nx:]

        @pl.when(pl.program_id(0) == 0)
        def _():
            ds_scr[...] = jnp.zeros_like(ds_scr)
            vcol_ref[...] = jnp.zeros_like(vcol_ref)
            vrow_ref[...] = jnp.zeros_like(vrow_ref)
            xc.start(x_in, x_out, sems)

        alog_r_, dtb_r_, alog_c_, dtb_c_ = ar[...], dr[...], ac[...], dc[...]
        lane2 = lax.broadcasted_iota(jnp.int32, (1, LANE), 1)
        acc = [[], []]

        def chain(d, h, q_r, k_r, v_r, sh_ref, th_ref, do_r, dq_ref, masks, decs):
            incl, strict, _, _ = masks
            idx = d * NH + h
            sl = slice(h * DH, (h + 1) * DH)
            qh, kh, vh = q_r[:, sl], k_r[:, sl], v_r[:, sl]
            doh = do_r[:, sl]
            gcol, grow, bcol, gl = _head_scalars(d, h, *decs)
            decay, kb, a, egc, rhs_u, rhs_w, qs, attn, etail = _chunk_local(qh, kh, vh, gcol, grow, bcol, gl, incl, strict)
            t = th_ref[0, h]
            s = sh_ref[0, h]
            ds_new = ds_scr[d, h]
            sol = _mm3(t, jnp.concatenate([rhs_u, rhs_w], axis=1))
            u, w = sol[:, :DH], sol[:, DH:]
            q_dec = qs * egc
            k_tail = kh * etail
            egl = jnp.exp(gl)
            dv_new = _mm(attn, doh, TN) + _mm(k_tail, ds_new)
            dq_dec = _mm(doh, s, NT)
            dgl = jnp.sum(jnp.sum(ds_new * s, axis=0, keepdims=True), axis=-1, keepdims=True) * egl
            yield
            v_new = u - _mm(w, s)
            dw = -_mm(dv_new, s, NT)
            ds_scr[d, h] = ds_new * egl + _mm(q_dec, doh, TN) - _mm(w, dv_new, TN)
            yield
            dattn = jnp.where(incl, _mm(doh, v_new, NT), 0.0)
            dk_tail = _mm(v_new, ds_new, NT)
            dr = _mm3(t, jnp.concatenate([dv_new, dw], axis=1), TN)
            dr_u, dr_w = dr[:, :DH], dr[:, DH:]
            yield
            da = -jnp.where(strict, _mm3(dr, sol, NT), 0.0)
            nq = dattn * decay
            dqs = _mm(nq, kh) + dq_dec * egc
            dk = _mm(nq, qs, TN)
            yield
            dv = dr_u * bcol
            dbeta = jnp.sum(dr_u * vh, axis=-1, keepdims=True)
            dgc = jnp.sum(dr_w * rhs_w, axis=-1, keepdims=True)
            m = da * decay
            dkb = dr_w * egc + _mm(m, kh)
            dk = dk + _mm(m, kb, TN)
            pq = da * a + dattn * attn
            dgc = dgc + jnp.sum(pq, axis=-1, keepdims=True) + jnp.sum(dq_dec * q_dec, axis=-1, keepdims=True)
            dgr = -jnp.sum(pq, axis=0, keepdims=True)
            tt = jnp.sum(dk_tail * k_tail, axis=-1, keepdims=True)
            dk = dk + dk_tail * etail + dkb * bcol
            dgc = dgc - tt
            dgl = dgl + jnp.sum(tt, axis=0, keepdims=True)
            dbeta = dbeta + jnp.sum(dkb * kh, axis=-1, keepdims=True)
            dq_ref[:, sl] = dqs * (DH ** -0.5)
            dq_ref[:, D + h * DH:D + (h + 1) * DH] = dk
            dq_ref[:, 2 * D + h * DH:2 * D + (h + 1) * DH] = dv
            acc[d].append((idx, dgc, dgl, dbeta, dgr))

        dirs = ((qf, kf, vf, abf, abtf, sf_ref, tf_ref, dof, dqf_ref, dcf_ref, drf_ref),
                (qb, kb_, vb, abb, abtb, sb_ref, tb_ref, dob, dqb_ref, dcb_ref, drb_ref))
        chains, ctx_d = [], []
        for d, (q_r, k_r, v_r, ab_r, abt_r, sh_ref, th_ref, do_r, dq_ref, _, _) in enumerate(dirs):
            masks = _masks(d)
            ab, abt = ab_r[...], abt_r[0]
            g_full, beta_full, gc_full, gl_full, gt_full, gct = _decays(
                d, ab, abt, alog_r_, dtb_r_, alog_c_, dtb_c_, masks[0], masks[2])
            ctx_d.append((masks, ab, abt, g_full, beta_full, gt_full))
            for h in range(NH):
                chains.append(chain(d, h, q_r, k_r, v_r, sh_ref, th_ref, do_r, dq_ref, masks,
                                    (beta_full, gc_full, gl_full, gct)))
        _lockstep(chains)
        for d in range(2):
            (incl, _, incl_t, _), ab, abt, g_full, beta_full, gt_full = ctx_d[d]
            dcol_ref, drow_ref = dirs[d][9], dirs[d][10]
            dgc_col = jnp.zeros((CH, LANE), F32)
            dgl_row = jnp.zeros((1, LANE), F32)
            dbeta_col = jnp.zeros((CH, LANE), F32)
            dgc_row = jnp.zeros((4 * NH, CH), F32)
            for idx, dgc, dgl, dbeta, dgr in acc[d]:
                oh = _lane_onehot(idx)
                dgc_col = dgc_col + dgc * oh
                dgl_row = dgl_row + dgl * oh
                dbeta_col = dbeta_col + dbeta * _lane_onehot(2 * NH + idx)
                ohc = (lax.broadcasted_iota(jnp.int32, (4 * NH, 1), 0) == idx).astype(F32)
                dgc_row = dgc_row + ohc * dgr
            dg_col = _mmh(incl_t.astype(F32), dgc_col) + dgl_row
            dg_row = _mmh(dgc_row, incl.astype(F32))
            sg_col = _sigmoid(ab + dtb_r_)
            da_col = dg_col * (-jnp.exp(alog_r_)) * sg_col
            dcol_ref[...] = da_col + dbeta_col * beta_full * (1.0 - beta_full)
            da_row = dg_row * (-jnp.exp(alog_c_)) * _sigmoid(abt + dtb_c_)
            drow_ref[0] = da_row
            vcol_ref[0:1, :] += jnp.sum(dg_col * g_full, axis=0, keepdims=True)
            vcol_ref[1:2, :] += jnp.sum(da_col, axis=0, keepdims=True)
            rl = jnp.sum(dg_row * gt_full, axis=-1, keepdims=True)
            rd = jnp.sum(da_row, axis=-1, keepdims=True)
            vrow_ref[...] += jnp.where(lane2 == 0, rl, 0.0) + jnp.where(lane2 == 1, rd, 0.0)

        @pl.when(pl.program_id(0) == nch // 2)
        def _():
            xc.forward(x_in, x_out, sems)

        @pl.when(pl.program_id(0) == nch - 1)
        def _():
            xc.wait(x_in, x_out, sems)

    def row(c, col):
        return pl.BlockSpec((CH, D), lambda s: (c(s), col))

    def hist(c, n):
        return pl.BlockSpec((1, NH, n, n), lambda s: (c(s), 0, 0, 0))

    def chunk_in(c):
        return [row(c, 0), row(c, 1), row(c, 2), pl.BlockSpec((CH, LANE), lambda s: (c(s), 0)),
                pl.BlockSpec((1, 4 * NH, CH), lambda s: (c(s), 0, 0)), hist(c, DH), hist(c, CH), row(c, 0)]

    small = [_full((1, LANE)), _full((1, LANE)), _full((4 * NH, 1)), _full((4 * NH, 1))]
    return pl.pallas_call(
        body, name="gdn_bwd", grid=(nch,),
        out_shape=(_sds((lt, 3 * D)), _sds((lt, 3 * D)), _sds((lt, LANE)), _sds((lt, LANE)),
                   _sds((nch, 4 * NH, CH)), _sds((nch, 4 * NH, CH)), _sds((8, LANE)), _sds((4 * NH, LANE))) + xc.out_shape,
        in_specs=chunk_in(cf) + chunk_in(cb) + small + [_ANY] * nx,
        out_specs=(pl.BlockSpec((CH, 3 * D), lambda s: (cf(s), 0)), pl.BlockSpec((CH, 3 * D), lambda s: (cb(s), 0)),
                   pl.BlockSpec((CH, LANE), lambda s: (cf(s), 0)), pl.BlockSpec((CH, LANE), lambda s: (cb(s), 0)),
                   pl.BlockSpec((1, 4 * NH, CH), lambda s: (cf(s), 0, 0)), pl.BlockSpec((1, 4 * NH, CH), lambda s: (cb(s), 0, 0)),
                   _full((8, LANE)), _full((4 * NH, LANE))) + tuple([_ANY] * nx),
        scratch_shapes=[pltpu.VMEM((2, NH, DH, DH), F32)] + xc.scratch,
        compiler_params=_params(("arbitrary",)),
    )(qkv, qkv, qkv, pab, abt, s_f, t_f, do, qkv, qkv, qkv, pab, abt, s_b, t_b, do, alog_r, dtb_r, alog_c, dtb_c,
      *xc_arrays)


def _post(p, o_f, o_b, x, tgt, w_pa, w_pb, w_out, w_sp, w_spt, b_spb, ln_g, ln_b, g_on, g_post, gate_x, lc):
    lt = p.shape[0]
    l = x.shape[0]
    tm = GC
    nct = lc // tm

    def body(p_ref, of_ref, ob_ref, x_ref, t_ref, wpa, wpb, wout, wsp, wspt, bspb, lng_ref, lnb_ref, gon_ref, gpost_ref, gate_ref,
             dp_ref, do_ref, dy_ref, ya_ref, yb_ref, mg_ref, da_ref, db_ref, dout_ref, dwsp_ref, dbsp_ref, vec_ref):
        i = pl.program_id(0)

        @pl.when(i == 0)
        def _():
            dwsp_ref[...] = jnp.zeros_like(dwsp_ref)
            dbsp_ref[...] = jnp.zeros_like(dbsp_ref)
            vec_ref[...] = jnp.zeros_like(vec_ref)

        @pl.when(i < nct)
        def _():
            dp_ref[...] = jnp.zeros_like(dp_ref)
            do_ref[...] = jnp.zeros_like(do_ref)

        @pl.when(i >= nct)
        def _():
            lng, lnb, gon, gpost, gate = lng_ref[...], lnb_ref[...], gon_ref[...], gpost_ref[...], gate_ref[...]
            zb, ua, va, za, ga, gb = [p_ref[:, j * D:(j + 1) * D] for j in range(6)]
            o = of_ref[...] + ob_ref[...]
            szb, dszb = _silu_g(zb)
            nh_l, r_l = [], []
            for h in range(NH):
                oh = o[:, h * DH:(h + 1) * DH]
                r = lax.rsqrt(jnp.mean(oh * oh, axis=-1, keepdims=True) + EPS)
                nh_l.append(oh * r)
                r_l.append(r)
            nrm_b = jnp.concatenate(nh_l, axis=-1)
            gon_t = jnp.concatenate([gon] * NH, axis=-1)
            y_b = nrm_b * gon_t * szb
            u, du_dua = _gelu_g(ua)
            gv, dgv_dva = _gelu_g(va)
            xc = gv - jnp.mean(gv, axis=-1, keepdims=True)
            rs_ln = lax.rsqrt(jnp.mean(xc * xc, axis=-1, keepdims=True) + EPS)
            vhat = xc * rs_ln
            v = vhat * lng + lnb
            s_sp = jnp.concatenate(
                [_mm(wsp[g], v[:, g * DH:(g + 1) * DH]) + bspb[g] for g in range(NH)], axis=-1)
            sza, dsza = _silu_g(za)
            y_a = u * s_sp * sza
            a_pr = _mm(y_a, wpa[...])
            b_pr = _mm(y_b, wpb[...])
            sga = _sigmoid(ga)
            sgb = _sigmoid(gb)
            merged = sga * a_pr + sgb * b_pr
            out = _mm(merged, wout[...])
            rs_o = lax.rsqrt(jnp.mean(out * out, axis=-1, keepdims=True) + EPS)
            n_o = out * rs_o
            rr = n_o * gpost
            diff = x_ref[...] + gate * rr - t_ref[...]
            vec_ref[5:6, :] += jnp.sum(diff * diff, axis=0, keepdims=True)
            dy = diff * (1.0 / D)
            dy_ref[...] = dy
            vec_ref[0:1, :] += jnp.sum(dy * rr, axis=0, keepdims=True)
            dr = dy * gate
            vec_ref[1:2, :] += jnp.sum(dr * n_o, axis=0, keepdims=True)
            dn_o = dr * gpost
            dout = rs_o * (dn_o - n_o * jnp.mean(dn_o * n_o, axis=-1, keepdims=True))
            dmerged = _mm(dout, wout[...], NT)
            d_a = dmerged * sga
            d_b = dmerged * sgb
            dga = dmerged * a_pr * sga * (1.0 - sga)
            dgb = dmerged * b_pr * sgb * (1.0 - sgb)
            dy_a = _mm(d_a, wpa[...], NT)
            dy_b = _mm(d_b, wpb[...], NT)
            ya_ref[...] = y_a.astype(ya_ref.dtype)
            yb_ref[...] = y_b.astype(yb_ref.dtype)
            mg_ref[...] = merged.astype(mg_ref.dtype)
            da_ref[...] = d_a.astype(da_ref.dtype)
            db_ref[...] = d_b.astype(db_ref.dtype)
            dout_ref[...] = dout.astype(dout_ref.dtype)
            dua = dy_a * s_sp * sza * du_dua
            ds_sp = dy_a * u * sza
            dza = dy_a * u * s_sp * dsza
            dv_l = []
            for g in range(NH):
                ds_g = ds_sp[:, g * DH:(g + 1) * DH]
                dv_l.append(_mm(wspt[g], ds_g))
                dwsp_ref[g] += _mm(ds_g, v[:, g * DH:(g + 1) * DH], NT)
                dbsp_ref[g] += ds_g
            dv = jnp.concatenate(dv_l, axis=-1)
            vec_ref[2:3, :] += jnp.sum(dv * vhat, axis=0, keepdims=True)
            vec_ref[3:4, :] += jnp.sum(dv, axis=0, keepdims=True)
            dvh = dv * lng
            dgv = rs_ln * (dvh - jnp.mean(dvh, axis=-1, keepdims=True) - vhat * jnp.mean(dvh * vhat, axis=-1, keepdims=True))
            dva = dgv * dgv_dva
            dzb = dy_b * nrm_b * gon_t * dszb
            dgon_full = jnp.sum(dy_b * nrm_b * szb, axis=0, keepdims=True)
            dgon = dgon_full[:, 0:DH]
            for h in range(1, NH):
                dgon = dgon + dgon_full[:, h * DH:(h + 1) * DH]
            vec_ref[4:5, 0:DH] += dgon
            dnb = dy_b * gon_t * szb
            do_l = []
            for h in range(NH):
                sl = slice(h * DH, (h + 1) * DH)
                dn_h = dnb[:, sl]
                do_l.append(r_l[h] * (dn_h - nh_l[h] * jnp.mean(dn_h * nh_l[h], axis=-1, keepdims=True)))
            do_ref[...] = jnp.concatenate(do_l, axis=-1)
            for j, val in enumerate((dzb, dua, dva, dza, dga, dgb)):
                dp_ref[:, j * D:(j + 1) * D] = val.astype(dp_ref.dtype)

    xrow = lambda i: (jnp.maximum(i - nct, 0), 0)
    wspec = _full((D, D))
    gspec = _full((NH, GC, GC))
    vspec = _full((1, D))
    bf_out = _sds((l, D), _BF)
    return pl.pallas_call(
        body, name="post", grid=(lt // tm,),
        out_shape=(_sds((lt, NREST), _BF), _sds((lt, D)), _sds((l, D)), bf_out, bf_out, bf_out, bf_out, bf_out, bf_out,
                   _sds((NH, GC, GC)), _sds((NH, GC, GC)), _sds((8, D))),
        in_specs=[pl.BlockSpec((tm, NREST), lambda i: (i, 0)), pl.BlockSpec((tm, D), lambda i: (i, 0)),
                  pl.BlockSpec((tm, D), lambda i: (i, 0)), pl.BlockSpec((tm, D), xrow), pl.BlockSpec((tm, D), xrow),
                  wspec, wspec, wspec, gspec, gspec, gspec, vspec, vspec, _full((1, DH)), vspec, vspec],
        out_specs=(pl.BlockSpec((tm, NREST), lambda i: (i, 0)), pl.BlockSpec((tm, D), lambda i: (i, 0)),
                   pl.BlockSpec((tm, D), xrow), pl.BlockSpec((tm, D), xrow), pl.BlockSpec((tm, D), xrow),
                   pl.BlockSpec((tm, D), xrow), pl.BlockSpec((tm, D), xrow), pl.BlockSpec((tm, D), xrow),
                   pl.BlockSpec((tm, D), xrow), gspec, gspec, _full((8, D))),
        compiler_params=_params(("arbitrary",)),
    )(p, o_f, o_b, x, tgt, w_pa, w_pb, w_out, w_sp, w_spt, b_spb, ln_g, ln_b, g_on, g_post, gate_x)


def _sum_parts(parts, name):
    r = parts.shape[1]
    tr = r if NDEV * r * LANE * 4 <= (8 << 20) else _tile(r, (512, 256, 128, 64, 32, 16, 8))

    def body(p_ref, o_ref):
        acc = p_ref[0]
        for s in range(1, NDEV):
            acc = acc + p_ref[s]
        o_ref[...] = acc

    return pl.pallas_call(
        body, name=name, out_shape=_sds((r, LANE)), grid=(r // tr,),
        in_specs=[pl.BlockSpec((NDEV, tr, LANE), lambda i: (0, i, 0))],
        out_specs=pl.BlockSpec((tr, LANE), lambda i: (i, 0)),
        compiler_params=_params(("parallel",)),
    )(parts)


def _mod_bwd(c_all, c_ctx, dmx, dmc, w_mod_g):
    ws = w_mod_g.shape[2]

    def body(ca_ref, cc_ref, dsh_ref, dmx_ref, dmc_ref, dmc_sh_ref, w_ref, gw_ref, gc_ref, gb_ref):
        sc, _ = _silu_g(ca_ref[...])
        scc, dscc = _silu_g(cc_ref[...])
        dmc_tot = jnp.sum(dmc_ref[...], axis=0, keepdims=True)
        gb_ref[...] = jnp.sum(dmx_ref[...], axis=0, keepdims=True) + dmc_tot
        lhs = jnp.concatenate([sc, jnp.broadcast_to(scc, (8, D))], axis=0)
        rhs = jnp.concatenate([dsh_ref[...], dmc_sh_ref[...]], axis=0)
        gw_ref[...] = _mmh(lhs, rhs, TN)
        acc = jnp.zeros((8, D), F32)
        tot8 = jnp.broadcast_to(dmc_tot, (8, 3 * D))
        for j in range(NDEV):
            acc = acc + _mm(tot8[:, j * ws:(j + 1) * ws], w_ref[j], NT)
        gc_ref[...] = acc[0:1, :] * dscc

    return pl.pallas_call(
        body, name="mod_bwd", out_shape=(_sds((D, ws)), _sds((1, D)), _sds((1, 3 * D))),
        compiler_params=_params(),
    )(c_all, c_ctx, _my_cols(dmx, ws), dmx, dmc, _my_cols(dmc, ws), w_mod_g)


def _my_cols(a, ws):
    me = 4 * lax.axis_index("x") + 2 * lax.axis_index("y") + lax.axis_index("c")
    return lax.dynamic_slice_in_dim(a, me * ws, ws, axis=1)


def _pair_sum(mine, other, name):
    n, r, c = mine.shape
    tr = _tile(r, (256, 128, 64, 32, 16, 8))

    def body(a_ref, b_ref, o_ref):
        o_ref[...] = (a_ref[...].astype(F32) + b_ref[...].astype(F32)).astype(o_ref.dtype)

    blk = pl.BlockSpec((1, tr, c), lambda j, i: (j, i, 0))
    return pl.pallas_call(
        body, name=name, out_shape=_sds((n, r, c), mine.dtype), grid=(n, r // tr),
        in_specs=[blk, blk], out_specs=blk, compiler_params=_params(("parallel", "parallel")),
    )(mine, other)


def _adamw(parts, w, m, v, name, chip_sums_below=None):
    s_, r, c = parts.shape
    tr = _tile(r, (128, 64, 32, 16, 8)) if r * c * 4 > (1 << 20) else r
    c1 = 1.0 / (1.0 - ADAM_B1 ** ADAM_STEP)
    c2 = 1.0 / (1.0 - ADAM_B2 ** ADAM_STEP)

    def body(p_ref, w_ref, m_ref, v_ref, g_ref, d_ref, nm_ref, nv_ref):
        if chip_sums_below is None:
            part = lambda s: p_ref[s].astype(F32)
        else:
            core = lax.axis_index("c")
            me = 4 * lax.axis_index("x") + 2 * lax.axis_index("y") + core
            every = me >= chip_sums_below
            part = lambda s: jnp.where(every | (core == s % 2), p_ref[s].astype(F32), 0.0)
        g = part(0)
        for s in range(1, s_):
            g = g + part(s)
        m_new = ADAM_B1 * m_ref[...] + (1.0 - ADAM_B1) * g
        v_new = ADAM_B2 * v_ref[...] + (1.0 - ADAM_B2) * (g * g)
        g_ref[...] = g
        nm_ref[...] = m_new
        nv_ref[...] = v_new
        d_ref[...] = -ADAM_LR * ((m_new * c1) / (jnp.sqrt(v_new * c2) + ADAM_EPS) + ADAM_WD * w_ref[...])

    blk = pl.BlockSpec((tr, c), lambda i: (i, 0))
    o = _sds((r, c))
    return pl.pallas_call(
        body, name=name, out_shape=(o, o, o, o), grid=(r // tr,),
        in_specs=[pl.BlockSpec((s_, tr, c), lambda i: (0, i, 0)), blk, blk, blk],
        out_specs=(blk, blk, blk, blk),
        compiler_params=_params(("parallel",)),
    )(parts, w, m, v)


def _rows(a):
    flat = a.reshape(-1)
    n = flat.shape[0]
    r = -(-n // (8 * LANE)) * 8
    return jnp.pad(flat, (0, r * LANE - n)).reshape(r, LANE)


def _pack(items):
    parts, layout, at = [], [], 0
    for name, a in items:
        rws = _rows(a.astype(F32))
        layout.append((name, at, rws.shape[0], a.shape))
        parts.append(rws)
        at += rws.shape[0]
    return jnp.concatenate(parts, axis=0), layout


def _unpack(packed, layout):
    out = {}
    for name, at, r, shape in layout:
        n = 1
        for s in shape:
            n *= s
        out[name] = packed[at:at + r].reshape(-1)[:n].reshape(shape)
    return out


def kernel(x, c, ctx, c_ctx, w_mod, b_mod, g_pre, g_post, w_in, w_conv, a_log, dt_bias, g_onorm, gm_ln_g, gm_ln_b, w_sp, b_sp, w_pa, w_pb, w_out, loss_target, m_c_ctx, m_w_mod, m_b_mod, m_g_pre, m_g_post, m_w_in, m_w_conv, m_a_log, m_dt_bias, m_g_onorm, m_gm_ln_g, m_gm_ln_b, m_w_sp, m_b_sp, m_w_pa, m_w_pb, m_w_out, v_c_ctx, v_w_mod, v_b_mod, v_g_pre, v_g_post, v_w_in, v_w_conv, v_a_log, v_dt_bias, v_g_onorm, v_gm_ln_g, v_gm_ln_b, v_w_sp, v_b_sp, v_w_pa, v_w_pb, v_w_out):
    l = x.shape[1]
    lc = ctx.shape[1]
    lt = l + lc
    nch = lt // CH
    me = 4 * lax.axis_index("x") + 2 * lax.axis_index("y") + lax.axis_index("c")
    wsh = w_in.shape[2]
    off_a = 3 * D
    n_ab = 4 * NH
    jb = off_a // wsh
    o1 = off_a - jb * wsh
    o2 = o1 + n_ab
    assert o2 <= wsh and NREST == (NDEV - jb) * wsh - o2
    split = jb + 1

    w_in_bf = w_in[0].astype(_BF)
    wg_lo, wg_mod, wg_conv, c_all = _exchange(
        [w_in_bf, w_mod[0].astype(_BF), w_conv[0], c], ["gather_lo", "gather", "gather", "gather"],
        "gather_first", split)
    w_qkv = jnp.concatenate([wg_lo[j][:, :wsh] for j in range(jb)] + [wg_lo[jb][:, :o1]], axis=1)
    w_ab = jnp.pad(wg_lo[jb][:, o1:o2], ((0, 0), (0, LANE - n_ab)))
    wconv_full = jnp.moveaxis(wg_conv, 0, 1).reshape(3, 3 * D)
    c_all = c_all.reshape(NDEV, D)

    cc = jnp.concatenate([c, c_ctx.reshape(1, D), jnp.zeros((6, D), F32)], axis=0)
    mods = _modulation(cc, wg_mod, b_mod)
    h = _prenorm(ctx[0], x[0], mods, g_pre)
    p_qkv = _matmul_nn(h, w_qkv, "in_proj_qkv")
    pab = _matmul_nn(h, w_ab, "in_proj_ab")
    abt = jnp.swapaxes(pab[:, :n_ab].reshape(nch, CH, n_ab), 1, 2)
    alog16, dtb16 = a_log.reshape(1, 2 * NH), dt_bias.reshape(1, 2 * NH)
    alog_r = jnp.pad(alog16, ((0, 0), (0, LANE - 2 * NH)))
    dtb_r = jnp.pad(dtb16, ((0, 0), (0, LANE - 2 * NH)))
    alog_c = jnp.pad(alog16.reshape(2 * NH, 1), ((0, 2 * NH), (0, 0)))
    dtb_c = jnp.pad(dtb16.reshape(2 * NH, 1), ((0, 2 * NH), (0, 0)))
    qkv = _qkv_fwd(p_qkv, wconv_full, lc)
    late = [w_in_bf, w_pa[0].astype(_BF), w_pb[0].astype(_BF), w_out[0].astype(_BF)]
    xc_late = _Exchange(zip(late, ["gather_hi", "gather", "gather", "gather"]), split)
    o_f, o_b, s_f, s_b, t_f, t_b, wg_hi, wg_pa, wg_pb, wg_out = _gdn_fwd(
        qkv, pab, abt, alog_r, dtb_r, alog_c, dtb_c, lc, xc_late, late)
    w_rest = jnp.concatenate([wg_lo[jb][:, o2:wsh]] + [wg_hi[j][:, :wsh] for j in range(split, NDEV)], axis=1)
    wf_pa, wf_pb, wf_out = wg_pa.reshape(D, D), wg_pb.reshape(D, D), wg_out.reshape(D, D)
    p_rest = _matmul_nn(h, w_rest, "in_proj_rest")

    w_spt = jnp.swapaxes(w_sp[0], 1, 2)
    b_spb = jnp.broadcast_to(b_sp[0][:, :, None], (NH, GC, GC))
    gate_x = mods[0:1, 2 * D:]
    dp_rest, do, dy, ya, yb, mg, d_a, d_b, dout, dwsp, dbsp_l, pvec = _post(
        p_rest, o_f, o_b, x[0], loss_target[0], wf_pa, wf_pb, wf_out, w_sp[0], w_spt, b_spb, gm_ln_g, gm_ln_b,
        g_onorm, g_post, gate_x, lc)

    dw_rest = _matmul_tn(h, dp_rest, "dw_in_rest")
    o3 = wsh - o2
    chunks_hi = jnp.moveaxis(dw_rest[:, o3:].reshape(D, NDEV - split, wsh), 1, 0)
    dw_pa = _matmul_tn(ya, d_a, "dw_pa").reshape(NDEV, D // NDEV, D)
    dw_pb = _matmul_tn(yb, d_b, "dw_pb").reshape(NDEV, D // NDEV, D)
    dw_out = _matmul_tn(mg, dout, "dw_out").reshape(NDEV, D // NDEV, D)
    small_a, lay_a = _pack([
        ("g_post", pvec[1]), ("g_onorm", pvec[4, :DH]), ("gm_ln_g", pvec[2]), ("gm_ln_b", pvec[3]), ("w_sp", dwsp),
        ("b_sp", jnp.sum(dbsp_l, axis=-1)), ("loss", pvec[5]), ("dgate", pvec[0])])
    (theirs_hi,) = _exchange([chunks_hi], ["sibling"], "pair_swap_hi")
    chip_hi = _pair_sum(chunks_hi, theirs_hi[0], "pair_sum_hi")
    early = [chip_hi, dw_pa, dw_pb, dw_out, small_a]
    xc_early = _Exchange(zip(early, ["scatter_par_hi", "scatter", "scatter", "scatter", "gather"]), split)

    dqkv_f, dqkv_b, dcol_f, dcol_b, drow_f, drow_b, gvec_c, gvec_r, r_in, r_pa, r_pb, r_out, small_a_all = _gdn_bwd(
        qkv, pab, abt, alog_r, dtb_r, alog_c, dtb_c, s_f, s_b, t_f, t_b, do, lc, xc_early, early)
    dp_qkv, dwconv = _qkv_bwd(p_qkv, wconv_full, dqkv_f, dqkv_b, lc)
    drow = jnp.swapaxes(drow_f + drow_b, 1, 2).reshape(lt, n_ab)
    dpab = (dcol_f + dcol_b + jnp.pad(drow, ((0, 0), (0, LANE - n_ab)))).astype(_BF)

    dw_qkv = _matmul_tn(h, dp_qkv, "dw_in_qkv")
    dw_ab = _matmul_tn(h, dpab, "dw_in_ab")
    dw_lo = jnp.concatenate([dw_qkv, dw_ab[:, :n_ab], dw_rest[:, :o3]], axis=1)
    chunks_lo = jnp.moveaxis(dw_lo.reshape(D, split, wsh), 1, 0)
    (theirs,) = _exchange([chunks_lo], ["sibling"], "pair_swap")
    chip_lo = _pair_sum(chunks_lo, theirs[0], "pair_sum")
    xc_last = _Exchange([(chip_lo, "scatter_par_lo")], split)
    dh, r_in = _dh_matmul(dp_rest, dp_qkv, dpab, w_rest, w_qkv, w_ab, xc_last, [chip_lo], {0: r_in})
    grad_x, nvec = _prenorm_bwd(ctx[0], x[0], dh, dy, mods, g_pre)

    dalog = gvec_c[0, :2 * NH] + gvec_r[:2 * NH, 0]
    ddtb = gvec_c[1, :2 * NH] + gvec_r[:2 * NH, 1]
    small_b, lay_b = _pack([
        ("g_pre", nvec[4]), ("a_log", dalog), ("dt_bias", ddtb), ("w_conv", dwconv),
        ("dshift", nvec[0]), ("dscale", nvec[1]), ("dshift_c", nvec[2]), ("dscale_c", nvec[3])])
    (small_b_all,) = _exchange([small_b], ["gather"], "gather_small")
    tot = _unpack(_sum_parts(small_a_all, "sum_small_a"), lay_a)
    tot.update(_unpack(_sum_parts(small_b_all, "sum_small_b"), lay_b))

    def per_device(packed_all, layout, name):
        at, r = [(a_, r_) for nm, a_, r_, _ in layout if nm == name][0]
        return packed_all[:, at:at + r].reshape(NDEV, -1)

    dmx_all = jnp.concatenate([per_device(small_b_all, lay_b, "dshift"), per_device(small_b_all, lay_b, "dscale"),
                               per_device(small_a_all, lay_a, "dgate")], axis=1)
    dmc_all = jnp.concatenate([per_device(small_b_all, lay_b, "dshift_c"), per_device(small_b_all, lay_b, "dscale_c"),
                               jnp.zeros((NDEV, D), F32)], axis=1)
    g_wmod, g_cctx, g_bmod = _mod_bwd(c_all, c_ctx.reshape(1, D), dmx_all, dmc_all, wg_mod)
    loss = 0.5 / D * jnp.sum(tot["loss"])
    ws_conv = w_conv.shape[2]
    g_wconv = lax.dynamic_slice_in_dim(tot["w_conv"], me * ws_conv, ws_conv, axis=1)

    small_names = ["c_ctx", "b_mod", "g_pre", "g_post", "a_log", "dt_bias", "g_onorm", "gm_ln_g", "gm_ln_b",
                   "w_sp", "b_sp", "w_conv"]
    wts = dict(c_ctx=c_ctx, b_mod=b_mod, g_pre=g_pre, g_post=g_post, a_log=a_log, dt_bias=dt_bias, g_onorm=g_onorm,
               gm_ln_g=gm_ln_g, gm_ln_b=gm_ln_b, w_sp=w_sp, b_sp=b_sp, w_conv=w_conv)
    ms = dict(c_ctx=m_c_ctx, b_mod=m_b_mod, g_pre=m_g_pre, g_post=m_g_post, a_log=m_a_log, dt_bias=m_dt_bias,
              g_onorm=m_g_onorm, gm_ln_g=m_gm_ln_g, gm_ln_b=m_gm_ln_b, w_sp=m_w_sp, b_sp=m_b_sp, w_conv=m_w_conv)
    vs = dict(c_ctx=v_c_ctx, b_mod=v_b_mod, g_pre=v_g_pre, g_post=v_g_post, a_log=v_a_log, dt_bias=v_dt_bias,
              g_onorm=v_g_onorm, gm_ln_g=v_gm_ln_g, gm_ln_b=v_gm_ln_b, w_sp=v_w_sp, b_sp=v_b_sp, w_conv=v_w_conv)
    gs = dict(tot)
    gs.update(c_ctx=g_cctx, b_mod=g_bmod, w_conv=g_wconv)
    gpk, play = _pack([(nm, gs[nm].reshape(wts[nm].shape)) for nm in small_names])
    wpk, _ = _pack([(nm, wts[nm]) for nm in small_names])
    mpk, _ = _pack([(nm, ms[nm]) for nm in small_names])
    vpk, _ = _pack([(nm, vs[nm]) for nm in small_names])
    res_small = [_unpack(a, play) for a in _adamw(gpk[None], wpk, mpk, vpk, "adamw_small")]
    res_big = {
        "w_mod": _adamw(g_wmod[None], w_mod[0], m_w_mod[0], v_w_mod[0], "adamw_w_mod"),
        "w_in": _adamw(r_in, w_in[0], m_w_in[0], v_w_in[0], "adamw_w_in", chip_sums_below=NDEV),
        "w_pa": _adamw(r_pa, w_pa[0], m_w_pa[0], v_w_pa[0], "adamw_w_pa"),
        "w_pb": _adamw(r_pb, w_pb[0], m_w_pb[0], v_w_pb[0], "adamw_w_pb"),
        "w_out": _adamw(r_out, w_out[0], m_w_out[0], v_w_out[0], "adamw_w_out"),
    }
    order = ["c_ctx", "w_mod", "b_mod", "g_pre", "g_post", "w_in", "w_conv", "a_log", "dt_bias", "g_onorm",
             "gm_ln_g", "gm_ln_b", "w_sp", "b_sp", "w_pa", "w_pb", "w_out"]
    outs = [loss, grad_x[None]]
    for k in range(4):
        for nm in order:
            if nm in res_big:
                outs.append(res_big[nm][k][None])
            else:
                outs.append(res_small[k][nm])
    return tuple(outs)
```

```python
import functools

import jax
import jax.numpy as jnp
from jax import lax
from jax.experimental import pallas as pl
from jax.experimental.pallas import tpu as pltpu

F32 = jnp.float32
_BF = jnp.bfloat16
_HI = lax.Precision.HIGHEST
D = 1024
NH = 8
DH = 128
CH = 64
GC = 128
NREST = 6 * D
NMAIN = NREST + 3 * D
EPS = 1e-6
LANE = 128
NDEV = 8
VMEM_LIMIT = 56 * 1024 * 1024
MESH = pl.DeviceIdType.MESH

ADAM_LR, ADAM_B1, ADAM_B2, ADAM_EPS, ADAM_WD, ADAM_STEP = 0.001, 0.9, 0.999, 1e-08, 0.01, 10

NN = ((1,), (0,))
NT = ((1,), (1,))
TN = ((0,), (0,))


def _dot(a, b, dims=NN, prec=None):
    return lax.dot_general(a, b, (dims, ((), ())), precision=prec, preferred_element_type=F32)


def _mm(a, b, dims=NN):
    return _dot(a.astype(_BF), b.astype(_BF), dims)


def _mmh(a, b, dims=NN):
    return _dot(a.astype(F32), b.astype(F32), dims, _HI)


def _split(a):
    hi = a.astype(_BF)
    return hi, (a - hi.astype(F32)).astype(_BF)


def _mm3(a, b, dims=NN):
    ah, al = _split(a)
    bh, bl = _split(b)
    return _dot(ah, bh, dims) + (_dot(ah, bl, dims) + _dot(al, bh, dims))


def _sigmoid(x):
    return 1.0 / (1.0 + jnp.exp(-x))


def _silu_g(x):
    s = _sigmoid(x)
    return x * s, s * (1.0 + x * (1.0 - s))


def _gelu_g(x):
    c = 0.7978845608028654
    t = jnp.tanh(c * (x + 0.044715 * (x * x * x)))
    cdf = 0.5 * (1.0 + t)
    return x * cdf, cdf + 0.5 * x * (1.0 - t * t) * c * (1.0 + 3 * 0.044715 * x * x)


def _softplus(x):
    return jnp.maximum(x, 0.0) + jnp.log(1.0 + jnp.exp(-jnp.abs(x)))


def _params(sem=None):
    return pltpu.CompilerParams(dimension_semantics=sem, vmem_limit_bytes=VMEM_LIMIT)


def _tile(n, pref):
    for t in pref:
        if n % t == 0:
            return t
    return n


def _full(shape):
    nd = len(shape)
    return pl.BlockSpec(shape, lambda *_: (0,) * nd)


def _sds(shape, dt=F32):
    return jax.ShapeDtypeStruct(shape, dt)


MAX_PIECES = 12
PIECE_BYTES = 256 * 1024


def _piece_slices(shape, itemsize):
    total = itemsize
    for d in shape:
        total *= d
    want = min(MAX_PIECES, total // PIECE_BYTES)
    lead = shape[0] if len(shape) >= 3 else 1
    rows = shape[-2] if len(shape) >= 2 else 1
    if want < 2 or lead > want:
        return [()]
    m = max([n for n in (8, 4, 2, 1) if n * lead <= want and rows % (16 * n) == 0], default=1)
    if m * lead < 2:
        return [()]
    rs = rows // m
    mid = (slice(None),) * max(len(shape) - 3, 0)
    if len(shape) >= 3:
        return [(i,) + mid + (pl.ds(j * rs, rs),) for i in range(lead) for j in range(m)]
    return [(pl.ds(j * rs, rs),) for j in range(m)]


class _Pieces:
    def __init__(self, copies):
        self.copies = copies

    def start(self):
        for cp in self.copies:
            cp.start()

    def wait_send(self):
        for cp in self.copies:
            cp.wait_send()

    def wait_recv(self):
        for cp in self.copies:
            cp.wait_recv()

    def wait(self):
        for cp in self.copies:
            cp.wait()


class _Exchange:
    def __init__(self, specs, split):
        self.specs = list(specs)
        self.split = split
        self.n = len(self.specs)
        def out(a, k):
            if k == "sibling":
                return (1,) + tuple(a.shape)
            return (NDEV,) + (tuple(a.shape) if k.startswith("gather") else tuple(a.shape[1:]))

        self.out_shape = tuple(_sds(out(a, k), a.dtype) for a, k in self.specs)
        self.pieces = [_piece_slices(o.shape[1:], jnp.dtype(o.dtype).itemsize) for o in self.out_shape]
        self.sem_base = [(NDEV - 1) * sum(len(p) for p in self.pieces[:a]) for a in range(self.n + 1)]
        self.scratch = [pltpu.SemaphoreType.DMA((self.sem_base[-1],)), pltpu.SemaphoreType.DMA((self.sem_base[-1],)),
                        pltpu.SemaphoreType.DMA((self.sem_base[-1] // (NDEV - 1),))]

    def _local(self, sems, a, src, dst):
        base = self.sem_base[a] // (NDEV - 1)
        return _Pieces([pltpu.make_async_copy(src.at[sl] if sl else src, dst.at[sl] if sl else dst, sems[2].at[base + p])
                        for p, sl in enumerate(self.pieces[a])])

    def _remote(self, sems, a, k, src, dst, to):
        send_sems, recv_sems, _ = sems
        base = self.sem_base[a] + k * len(self.pieces[a])
        return _Pieces([
            pltpu.make_async_remote_copy(
                src_ref=src.at[sl] if sl else src, dst_ref=dst.at[sl] if sl else dst, send_sem=send_sems.at[base + p],
                recv_sem=recv_sems.at[base + p], device_id=to, device_id_type=MESH)
            for p, sl in enumerate(self.pieces[a])])

    def _ok(self, kind, idx):
        if kind.endswith("_lo"):
            return idx < self.split
        if kind.endswith("_hi"):
            return idx >= self.split
        return True

    def _phases(self, ins, outs, sems):
        x, y, c = lax.axis_index("x"), lax.axis_index("y"), lax.axis_index("c")
        me = 4 * x + 2 * y + c
        sib = (x, y, 1 - c)
        sib_idx = 4 * x + 2 * y + (1 - c)
        chips = [(1 - x, y), (x, 1 - y), (1 - x, 1 - y)]
        starts, forwards, waits = [], [], []
        for a, (_, kind) in enumerate(self.specs):
            ok = functools.partial(self._ok, kind)
            if kind.startswith("gather"):
                def copy(k, block, to, src=None, a=a):
                    rows = outs[a].at[block]
                    return self._remote(sems, a, k, rows if src is None else src, rows, to)

                loc = self._local(sems, a, ins[a], outs[a].at[me])
                first = [copy(0, me, sib, ins[a])] + [copy(1 + j, me, (*chip, c), ins[a]) for j, chip in enumerate(chips)]
                starts += [(ok(me), loc.start)] + [(ok(me), cp.start) for cp in first]
                waits += [(ok(me), loc.wait)] + [(ok(me), cp.wait_send) for cp in first]
                for j, chip in enumerate(chips):
                    origin = 4 * chip[0] + 2 * chip[1] + c
                    passed = copy(4 + j, origin, sib)
                    forwards += [(ok(origin), copy(1 + j, origin, sib).wait_recv), (ok(origin), passed.start)]
                    waits.append((ok(origin), passed.wait_send))
                    other = 4 * chip[0] + 2 * chip[1] + (1 - c)
                    waits.append((ok(other), copy(4 + j, other, sib).wait_recv))
                waits.append((ok(sib_idx), copy(0, sib_idx, sib).wait_recv))
            elif kind == "sibling":
                swap = self._remote(sems, a, 0, ins[a], outs[a].at[0], sib)
                starts.append((True, swap.start))
                waits += [(True, swap.wait_send), (True, swap.wait_recv)]
            else:
                base = self.split if kind.endswith("_hi") else 0
                same_core_only = "_par" in kind

                def src(idx, a=a, base=base):
                    return ins[a].at[jnp.clip(idx - base, 0, ins[a].shape[0] - 1)]

                loc = self._local(sems, a, src(me), outs[a].at[me])
                starts.append((ok(me), loc.start))
                waits.append((ok(me), loc.wait))
                for k in range(1, NDEV):
                    if same_core_only and k & 1:
                        continue
                    px = 1 - x if (k >> 2) & 1 else x
                    py = 1 - y if (k >> 1) & 1 else y
                    pc = 1 - c if k & 1 else c
                    pidx = 4 * px + 2 * py + pc
                    send = self._remote(sems, a, k - 1, src(pidx), outs[a].at[me], (px, py, pc))
                    arrive = self._remote(sems, a, k - 1, src(pidx), outs[a].at[pidx], (px, py, pc))
                    starts.append((ok(pidx), send.start))
                    waits += [(ok(pidx), send.wait_send), (ok(me), arrive.wait_recv)]
        return starts, forwards, waits

    @staticmethod
    def _run(actions):
        for cond, fn in actions:
            if cond is True:
                fn()
            else:
                pl.when(cond)(fn)

    def start(self, ins, outs, sems):
        self._run(self._phases(ins, outs, sems)[0])

    def forward(self, ins, outs, sems):
        self._run(self._phases(ins, outs, sems)[1])

    def wait(self, ins, outs, sems):
        self._run(self._phases(ins, outs, sems)[2])


_ANY = pl.BlockSpec(memory_space=pl.ANY)


def _exchange(arrays, kinds, name, split=0, into=None):
    xc = _Exchange(zip(arrays, kinds), split)
    n = xc.n
    into = into or {}
    ni = len(into)

    def body(*refs):
        ins, outs, sems = refs[:n], refs[n + ni:2 * n + ni], refs[2 * n + ni:]
        xc.start(ins, outs, sems)
        xc.forward(ins, outs, sems)
        xc.wait(ins, outs, sems)

    return pl.pallas_call(
        body, name=name, out_shape=xc.out_shape, in_specs=[_ANY] * (n + ni), out_specs=tuple([_ANY] * n),
        scratch_shapes=xc.scratch, input_output_aliases={n + t: a for t, a in enumerate(into)},
    )(*arrays, *into.values())


def _matmul_nn(a, b, name, out_dtype=F32):
    m, kk = a.shape
    n = b.shape[1]
    tm = m if m * kk * a.dtype.itemsize <= (12 << 20) else _tile(m, (1088, 1024, 640, 512, 256, 128))
    tn = _tile(n, (512, 256, 128))

    def body(a_ref, b_ref, o_ref):
        o_ref[...] = _mm(a_ref[...], b_ref[...]).astype(o_ref.dtype)

    return pl.pallas_call(
        body, name=name, out_shape=_sds((m, n), out_dtype), grid=(n // tn, m // tm),
        in_specs=[pl.BlockSpec((tm, kk), lambda j, i: (i, 0)), pl.BlockSpec((kk, tn), lambda j, i: (0, j))],
        out_specs=pl.BlockSpec((tm, tn), lambda j, i: (i, j)),
        compiler_params=_params(("parallel", "parallel")),
    )(a, b)


def _matmul_tn(a, b, name):
    kk, m = a.shape
    n = b.shape[1]
    tk = _tile(kk, (1088, 1024, 640, 512, 256, 128))
    tn = _tile(n, (1024, 512, 256, 128))
    nk = kk // tk

    def body(a_ref, b_ref, o_ref, acc_ref):
        k = pl.program_id(1)

        @pl.when(k == 0)
        def _():
            acc_ref[...] = jnp.zeros_like(acc_ref)

        acc_ref[...] += _mm(a_ref[...], b_ref[...], TN)

        @pl.when(k == nk - 1)
        def _():
            o_ref[...] = acc_ref[...].astype(o_ref.dtype)

    return pl.pallas_call(
        body, name=name, out_shape=_sds((m, n), _BF), grid=(n // tn, nk),
        in_specs=[pl.BlockSpec((tk, m), lambda j, k: (k, 0)), pl.BlockSpec((tk, tn), lambda j, k: (k, j))],
        out_specs=pl.BlockSpec((m, tn), lambda j, k: (0, j)),
        scratch_shapes=[pltpu.VMEM((m, tn), F32)],
        compiler_params=_params(("parallel", "arbitrary")),
    )(a, b)


def _dh_matmul(dp_rest, dp_qkv, dpab, w_rest, w_qkv, w_ab, xc, xc_arrays, xc_into):
    lt = dp_rest.shape[0]
    tm = _tile(lt, (1088, 1024, 640, 512, 256, 128))
    nr, nq = dp_rest.shape[1] // D, dp_qkv.shape[1] // D
    nx, ni = xc.n, len(xc_into)
    ni_steps = lt // tm

    def body(*refs):
        dr_ref, dq_ref, ab_ref, wr_ref, wq_ref, wab_ref = refs[:6]
        x_in = refs[6:6 + nx]
        o_ref = refs[6 + nx + ni]
        x_out = refs[7 + nx + ni:7 + 2 * nx + ni]
        sems = refs[7 + 2 * nx + ni:]
        i = pl.program_id(0)
        k = pl.program_id(1)

        @pl.when((i == 0) & (k == 0))
        def _():
            xc.start(x_in, x_out, sems)

        @pl.when(k == 0)
        def _():
            o_ref[...] = _mm(ab_ref[...], wab_ref[...], NT)

        @pl.when(k < nr)
        def _():
            o_ref[...] += _mm(dr_ref[...], wr_ref[...], NT)

        @pl.when(k >= nr)
        def _():
            o_ref[...] += _mm(dq_ref[...], wq_ref[...], NT)

        @pl.when((i == ni_steps - 1) & (k == nr + nq - 1))
        def _():
            xc.wait(x_in, x_out, sems)

    rk = lambda k: jnp.minimum(k, nr - 1)
    qk = lambda k: jnp.maximum(k - nr, 0)
    return pl.pallas_call(
        body, name="dh_matmul", out_shape=(_sds((lt, D)),) + xc.out_shape, grid=(lt // tm, nr + nq),
        in_specs=[pl.BlockSpec((tm, D), lambda i, k: (i, rk(k))), pl.BlockSpec((tm, D), lambda i, k: (i, qk(k))),
                  pl.BlockSpec((tm, LANE), lambda i, k: (i, 0)),
                  pl.BlockSpec((D, D), lambda i, k: (0, rk(k))), pl.BlockSpec((D, D), lambda i, k: (0, qk(k))),
                  _full((D, LANE))] + [_ANY] * (nx + ni),
        out_specs=(pl.BlockSpec((tm, D), lambda i, k: (i, 0)),) + tuple([_ANY] * nx),
        scratch_shapes=xc.scratch, input_output_aliases={6 + nx + t: 1 + a for t, a in enumerate(xc_into)},
        compiler_params=_params(("arbitrary", "arbitrary")),
    )(dp_rest, dp_qkv, dpab, w_rest, w_qkv, w_ab, *xc_arrays, *xc_into.values())


def _modulation(cc, w_mod_g, b_mod):
    ws = w_mod_g.shape[2]

    def body(c_ref, w_ref, b_ref, o_ref):
        s, _ = _silu_g(c_ref[...])
        o_ref[...] = _mm(s, w_ref[0]) + b_ref[...]

    return pl.pallas_call(
        body, name="modulation", out_shape=_sds((8, 3 * D)), grid=(NDEV,),
        in_specs=[_full((8, D)), pl.BlockSpec((1, D, ws), lambda j: (j, 0, 0)), pl.BlockSpec((1, ws), lambda j: (0, j))],
        out_specs=pl.BlockSpec((8, ws), lambda j: (0, j)),
        compiler_params=_params(("parallel",)),
    )(cc, w_mod_g, b_mod)


def _prenorm(ctx, x, mods, g_pre):
    lc = ctx.shape[0]
    lt = lc + x.shape[0]
    tm = _tile(lc, (256, 128))
    nct = lc // tm

    def body(c_ref, x_ref, m_ref, g_ref, o_ref, ot_ref):
        is_ctx = pl.program_id(0) < nct
        x = jnp.where(is_ctx, c_ref[...], x_ref[...])
        shift = jnp.where(is_ctx, m_ref[1:2, 0:D], m_ref[0:1, 0:D])
        scale = jnp.where(is_ctx, m_ref[1:2, D:2 * D], m_ref[0:1, D:2 * D])
        r = lax.rsqrt(jnp.mean(x * x, axis=-1, keepdims=True) + EPS)
        h = ((x * r * g_ref[...]) * (1.0 + scale) + shift).astype(o_ref.dtype)
        o_ref[...] = h
        ot_ref[...] = h.T

    return pl.pallas_call(
        body, name="prenorm", out_shape=(_sds((lt, D), _BF), _sds((D, lt), _BF)), grid=(lt // tm,),
        in_specs=[pl.BlockSpec((tm, D), lambda i: (jnp.minimum(i, nct - 1), 0)),
                  pl.BlockSpec((tm, D), lambda i: (jnp.maximum(i - nct, 0), 0)), _full((8, 3 * D)), _full((1, D))],
        out_specs=(pl.BlockSpec((tm, D), lambda i: (i, 0)), pl.BlockSpec((D, tm), lambda i: (0, i))),
        compiler_params=_params(("parallel",)),
    )(ctx, x, mods, g_pre)


def _prenorm_bwd(ctx, x, dh, dy, mods, g_pre):
    lc = ctx.shape[0]
    lt = lc + x.shape[0]
    tm = _tile(lc, (256, 128))
    nct = lc // tm
    nl = (lt - lc) // tm

    def body(c_ref, x_ref, dh_ref, dy_ref, m_ref, g_ref, gx_ref, vec_ref):
        i = pl.program_id(0)

        @pl.when(i == 0)
        def _():
            vec_ref[...] = jnp.zeros_like(vec_ref)

        is_ctx = i < nct
        x = jnp.where(is_ctx, c_ref[...], x_ref[...])
        dh = dh_ref[...]
        g = g_ref[...]
        scale = jnp.where(is_ctx, m_ref[1:2, D:2 * D], m_ref[0:1, D:2 * D])
        r = lax.rsqrt(jnp.mean(x * x, axis=-1, keepdims=True) + EPS)
        n = x * r
        hn = n * g
        dsh = jnp.sum(dh, axis=0, keepdims=True)
        dsc = jnp.sum(dh * hn, axis=0, keepdims=True)
        dhn = dh * (1.0 + scale)
        vec_ref[4:5, :] += jnp.sum(dhn * n, axis=0, keepdims=True)
        dn = dhn * g
        dx = r * (dn - n * jnp.mean(dn * n, axis=-1, keepdims=True))

        @pl.when(is_ctx)
        def _():
            vec_ref[2:3, :] += dsh
            vec_ref[3:4, :] += dsc

        @pl.when(jnp.logical_not(is_ctx))
        def _():
            vec_ref[0:1, :] += dsh
            vec_ref[1:2, :] += dsc
            gx_ref[...] = dy_ref[...] + dx

    xrow = lambda i: (jnp.maximum(i - nct, 0), 0)
    return pl.pallas_call(
        body, name="prenorm_bwd", out_shape=(_sds((nl * tm, D)), _sds((8, D))), grid=(lt // tm,),
        in_specs=[pl.BlockSpec((tm, D), lambda i: (jnp.minimum(i, nct - 1), 0)), pl.BlockSpec((tm, D), xrow),
                  pl.BlockSpec((tm, D), lambda i: (i, 0)), pl.BlockSpec((tm, D), xrow), _full((8, 3 * D)), _full((1, D))],
        out_specs=(pl.BlockSpec((tm, D), xrow), _full((8, D))),
        compiler_params=_params(("arbitrary",)),
    )(ctx, x, dh, dy, mods, g_pre)


def _conv_parts(x, w, lc):
    lt = x.shape[0]
    row = lax.broadcasted_iota(jnp.int32, x.shape, 0)
    first = (row == 0) | (row == lc)
    last = (row == lc - 1) | (row == lt - 1)
    xp = jnp.where(first, 0.0, pltpu.roll(x, 1, 0))
    xn = jnp.where(last, 0.0, pltpu.roll(x, lt - 1, 0))
    y = w[0:1, :] * xp + w[1:2, :] * x + w[2:3, :] * xn
    return xp, xn, y, first, last


def _qkv_fwd(p, w_conv, lc):
    lt = p.shape[0]

    def body(p_ref, w_ref, o_ref):
        _, _, y, _, _ = _conv_parts(p_ref[...], w_ref[...], lc)
        s, _ = _silu_g(y)
        rs = lax.rsqrt(jnp.sum(s * s, axis=-1, keepdims=True) + EPS)
        o_ref[...] = s * jnp.where(pl.program_id(0) < 2 * NH, rs, 1.0)

    return pl.pallas_call(
        body, name="qkv_fwd", out_shape=_sds((lt, 3 * D)), grid=(3 * NH,),
        in_specs=[pl.BlockSpec((lt, DH), lambda j: (0, j)), pl.BlockSpec((3, DH), lambda j: (0, j))],
        out_specs=pl.BlockSpec((lt, DH), lambda j: (0, j)),
        compiler_params=_params(("parallel",)),
    )(p, w_conv)


def _qkv_bwd(p, w_conv, dqkv_f, dqkv_b, lc):
    lt = p.shape[0]

    def body(p_ref, w_ref, df_ref, db_ref, dp_ref, dw_ref):
        w = w_ref[...]
        xp, xn, y, first, last = _conv_parts(p_ref[...], w, lc)
        s, ds_dy = _silu_g(y)
        dn = df_ref[...] + db_ref[...]
        rs = lax.rsqrt(jnp.sum(s * s, axis=-1, keepdims=True) + EPS)
        nrm = s * rs
        ds_n = rs * (dn - nrm * jnp.sum(dn * nrm, axis=-1, keepdims=True))
        ds = jnp.where(pl.program_id(0) < 2 * NH, ds_n, dn)
        dy = ds * ds_dy
        dw_ref[0:1, :] = jnp.sum(dy * xp, axis=0, keepdims=True)
        dw_ref[1:2, :] = jnp.sum(dy * p_ref[...], axis=0, keepdims=True)
        dw_ref[2:3, :] = jnp.sum(dy * xn, axis=0, keepdims=True)
        dyn = jnp.where(last, 0.0, pltpu.roll(dy, lt - 1, 0))
        dyp = jnp.where(first, 0.0, pltpu.roll(dy, 1, 0))
        dp_ref[...] = (w[1:2, :] * dy + w[0:1, :] * dyn + w[2:3, :] * dyp).astype(dp_ref.dtype)

    return pl.pallas_call(
        body, name="qkv_bwd", out_shape=(_sds((lt, 3 * D), _BF), _sds((3, 3 * D))), grid=(3 * NH,),
        in_specs=[pl.BlockSpec((lt, DH), lambda j: (0, j)), pl.BlockSpec((3, DH), lambda j: (0, j)),
                  pl.BlockSpec((lt, DH), lambda j: (0, j)), pl.BlockSpec((lt, DH), lambda j: (0, j))],
        out_specs=(pl.BlockSpec((lt, DH), lambda j: (0, j)), pl.BlockSpec((3, DH), lambda j: (0, j))),
        compiler_params=_params(("parallel",)),
    )(p, w_conv, dqkv_f, dqkv_b)


def _masks(d):
    ri = lax.broadcasted_iota(jnp.int32, (CH, CH), 0)
    ci = lax.broadcasted_iota(jnp.int32, (CH, CH), 1)
    incl = (ri >= ci) if d == 0 else (ri <= ci)
    strict = (ri > ci) if d == 0 else (ri < ci)
    incl_t = (ri <= ci) if d == 0 else (ri >= ci)
    return incl, strict, incl_t, ri == ci


def _decays(d, ab, abt, alog_r, dtb_r, alog_c, dtb_c, incl, incl_t):
    g_full = -jnp.exp(alog_r) * _softplus(ab + dtb_r)
    beta_full = _sigmoid(ab)
    gc_full = _mmh(incl.astype(F32), g_full)
    gl_full = jnp.sum(g_full, axis=0, keepdims=True)
    gt_full = -jnp.exp(alog_c) * _softplus(abt + dtb_c)
    gct = _mmh(gt_full, incl_t.astype(F32))
    return g_full, beta_full, gc_full, gl_full, gt_full, gct


def _lane_onehot(idx, n=LANE):
    return (lax.broadcasted_iota(jnp.int32, (1, n), 1) == idx).astype(F32)


def _head_scalars(d, h, beta_full, gc_full, gl_full, gct):
    idx = d * NH + h
    oh = _lane_onehot(idx)
    gcol = jnp.sum(gc_full * oh, axis=-1, keepdims=True)
    bcol = jnp.sum(beta_full * _lane_onehot(2 * NH + idx), axis=-1, keepdims=True)
    gl = jnp.sum(gl_full * oh, axis=-1, keepdims=True)
    grow = gct[idx:idx + 1, :]
    return gcol, grow, bcol, gl


def _lockstep(gens):
    live = list(gens)
    while live:
        nxt = []
        for g in live:
            try:
                next(g)
                nxt.append(g)
            except StopIteration:
                pass
        live = nxt


def _chunk_local(qh, kh, vh, gcol, grow, bcol, gl, incl, strict):
    decay = jnp.where(incl, jnp.exp(gcol - grow), 0.0)
    kb = kh * bcol
    a = jnp.where(strict, _mm(kb, kh, NT) * decay, 0.0)
    egc = jnp.exp(gcol)
    rhs_u = vh * bcol
    rhs_w = kb * egc
    qs = qh * (DH ** -0.5)
    attn = jnp.where(incl, _mm(qs, kh, NT) * decay, 0.0)
    etail = jnp.exp(gl - gcol)
    return decay, kb, a, egc, rhs_u, rhs_w, qs, attn, etail


def _scan_specs(lt, lc, bwd_pass):
    nch = lt // CH
    ncc = lc // CH
    if not bwd_pass:
        cf = lambda s: s
        cb = lambda s: jnp.where(s < ncc, ncc - 1 - s, nch + ncc - 1 - s)
    else:
        cf = lambda s: nch - 1 - s
        cb = lambda s: jnp.where(s < nch - ncc, ncc + s, s - (nch - ncc))
    return nch, cf, cb


def _gdn_fwd(qkv, pab, abt, alog_r, dtb_r, alog_c, dtb_c, lc, xc, xc_arrays):
    lt = qkv.shape[0]
    nch, cf, cb = _scan_specs(lt, lc, False)
    nx = xc.n

    def body(*refs):
        qf, kf, vf, abf, abtf, qb, kb_, vb, abb, abtb, ar, dr, ac, dc = refs[:14]
        x_in = refs[14:14 + nx]
        of_ref, ob_ref, sf_ref, sb_ref, tf_ref, tb_ref = refs[14 + nx:20 + nx]
        x_out = refs[20 + nx:20 + 2 * nx]
        s_scr = refs[20 + 2 * nx]
        sems = refs[21 + 2 * nx:]

        @pl.when(pl.program_id(0) == 0)
        def _():
            s_scr[...] = jnp.zeros_like(s_scr)
            xc.start(x_in, x_out, sems)

        def chain(d, h, q_r, k_r, v_r, o_ref, sh_ref, th_ref, masks, decs):
            incl, strict, _, eye = masks
            sl = slice(h * DH, (h + 1) * DH)
            qh, kh, vh = q_r[:, sl], k_r[:, sl], v_r[:, sl]
            gcol, grow, bcol, gl = _head_scalars(d, h, *decs)
            _, _, a, egc, rhs_u, rhs_w, qs, attn, etail = _chunk_local(qh, kh, vh, gcol, grow, bcol, gl, incl, strict)
            yield
            n = -a
            t = jnp.where(eye, 1.0, 0.0) + n
            p = _mm3(n, n)
            yield
            for _ in range(4):
                r = _mm3(jnp.concatenate([t, p], axis=0), p)
                yield
                t = t + r[:CH]
                p = r[CH:]
            t = t + _mm3(t, p)
            yield
            sol = _mm3(t, jnp.concatenate([rhs_u, rhs_w], axis=1))
            u, w = sol[:, :DH], sol[:, DH:]
            s = s_scr[d, h]
            sh_ref[0, h] = s
            th_ref[0, h] = t
            yield
            ws = _mm(jnp.concatenate([w, qs * egc], axis=0), s)
            yield
            v_new = u - ws[:CH]
            o_ref[:, sl] = ws[CH:] + _mm(attn, v_new)
            s_scr[d, h] = s * jnp.exp(gl) + _mm(kh * etail, v_new, TN)

        chains = []
        for d, (q_r, k_r, v_r, ab_r, abt_r, o_ref, sh_ref, th_ref) in enumerate(
                ((qf, kf, vf, abf, abtf, of_ref, sf_ref, tf_ref), (qb, kb_, vb, abb, abtb, ob_ref, sb_ref, tb_ref))):
            masks = _masks(d)
            _, beta_full, gc_full, gl_full, _, gct = _decays(
                d, ab_r[...], abt_r[0], ar[...], dr[...], ac[...], dc[...], masks[0], masks[2])
            for h in range(NH):
                chains.append(chain(d, h, q_r, k_r, v_r, o_ref, sh_ref, th_ref, masks, (beta_full, gc_full, gl_full, gct)))
        _lockstep(chains)

        @pl.when(pl.program_id(0) == nch // 2)
        def _():
            xc.forward(x_in, x_out, sems)

        @pl.when(pl.program_id(0) == nch - 1)
        def _():
            xc.wait(x_in, x_out, sems)

    def row(c, col):
        return pl.BlockSpec((CH, D), lambda s: (c(s), col))

    def chunk_in(c):
        return [row(c, 0), row(c, 1), row(c, 2), pl.BlockSpec((CH, LANE), lambda s: (c(s), 0)),
                pl.BlockSpec((1, 4 * NH, CH), lambda s: (c(s), 0, 0))]

    def hist(c, n):
        return pl.BlockSpec((1, NH, n, n), lambda s: (c(s), 0, 0, 0))

    small = [_full((1, LANE)), _full((1, LANE)), _full((4 * NH, 1)), _full((4 * NH, 1))]
    return pl.pallas_call(
        body, name="gdn_fwd", grid=(nch,),
        out_shape=(_sds((lt, D)), _sds((lt, D)), _sds((nch, NH, DH, DH)), _sds((nch, NH, DH, DH)),
                   _sds((nch, NH, CH, CH)), _sds((nch, NH, CH, CH))) + xc.out_shape,
        in_specs=chunk_in(cf) + chunk_in(cb) + small + [_ANY] * nx,
        out_specs=(pl.BlockSpec((CH, D), lambda s: (cf(s), 0)), pl.BlockSpec((CH, D), lambda s: (cb(s), 0)),
                   hist(cf, DH), hist(cb, DH), hist(cf, CH), hist(cb, CH)) + tuple([_ANY] * nx),
        scratch_shapes=[pltpu.VMEM((2, NH, DH, DH), F32)] + xc.scratch,
        compiler_params=_params(("arbitrary",)),
    )(qkv, qkv, qkv, pab, abt, qkv, qkv, qkv, pab, abt, alog_r, dtb_r, alog_c, dtb_c, *xc_arrays)


def _gdn_bwd(qkv, pab, abt, alog_r, dtb_r, alog_c, dtb_c, s_f, s_b, t_f, t_b, do, lc, xc, xc_arrays):
    lt = qkv.shape[0]
    nch, cf, cb = _scan_specs(lt, lc, True)
    nx = xc.n

    def body(*refs):
        qf, kf, vf, abf, abtf, sf_ref, tf_ref, dof, qb, kb_, vb, abb, abtb, sb_ref, tb_ref, dob, ar, dr, ac, dc = refs[:20]
        x_in = refs[20:20 + nx]
        dqf_ref, dqb_ref, dcf_ref, dcb_ref, drf_ref, drb_ref, vcol_ref, vrow_ref = refs[20 + nx:28 + nx]
        x_out = refs[28 + nx:28 + 2 * nx]
        ds_scr = refs[28 + 2 * nx]
        sems = refs[29 + 2 * nx:]

        @pl.when(pl.program_id(0) == 0)
        def _():
            ds_scr[...] = jnp.zeros_like(ds_scr)
            vcol_ref[...] = jnp.zeros_like(vcol_ref)
            vrow_ref[...] = jnp.zeros_like(vrow_ref)
            xc.start(x_in, x_out, sems)

        alog_r_, dtb_r_, alog_c_, dtb_c_ = ar[...], dr[...], ac[...], dc[...]
        lane2 = lax.broadcasted_iota(jnp.int32, (1, LANE), 1)
        acc = [[], []]

        def chain(d, h, q_r, k_r, v_r, sh_ref, th_ref, do_r, dq_ref, masks, decs):
            incl, strict, _, _ = masks
            idx = d * NH + h
            sl = slice(h * DH, (h + 1) * DH)
            qh, kh, vh = q_r[:, sl], k_r[:, sl], v_r[:, sl]
            doh = do_r[:, sl]
            gcol, grow, bcol, gl = _head_scalars(d, h, *decs)
            decay, kb, a, egc, rhs_u, rhs_w, qs, attn, etail = _chunk_local(qh, kh, vh, gcol, grow, bcol, gl, incl, strict)
            t = th_ref[0, h]
            s = sh_ref[0, h]
            ds_new = ds_scr[d, h]
            sol = _mm3(t, jnp.concatenate([rhs_u, rhs_w], axis=1))
            u, w = sol[:, :DH], sol[:, DH:]
            q_dec = qs * egc
            k_tail = kh * etail
            egl = jnp.exp(gl)
            dv_new = _mm(attn, doh, TN) + _mm(k_tail, ds_new)
            dq_dec = _mm(doh, s, NT)
            dgl = jnp.sum(jnp.sum(ds_new * s, axis=0, keepdims=True), axis=-1, keepdims=True) * egl
            yield
            v_new = u - _mm(w, s)
            dw = -_mm(dv_new, s, NT)
            ds_scr[d, h] = ds_new * egl + _mm(q_dec, doh, TN) - _mm(w, dv_new, TN)
            yield
            dattn = jnp.where(incl, _mm(doh, v_new, NT), 0.0)
            dk_tail = _mm(v_new, ds_new, NT)
            dr = _mm3(t, jnp.concatenate([dv_new, dw], axis=1), TN)
            dr_u, dr_w = dr[:, :DH], dr[:, DH:]
            yield
            da = -jnp.where(strict, _mm3(dr, sol, NT), 0.0)
            nq = dattn * decay
            dqs = _mm(nq, kh) + dq_dec * egc
            dk = _mm(nq, qs, TN)
            yield
            dv = dr_u * bcol
            dbeta = jnp.sum(dr_u * vh, axis=-1, keepdims=True)
            dgc = jnp.sum(dr_w * rhs_w, axis=-1, keepdims=True)
            m = da * decay
            dkb = dr_w * egc + _mm(m, kh)
            dk = dk + _mm(m, kb, TN)
            pq = da * a + dattn * attn
            dgc = dgc + jnp.sum(pq, axis=-1, keepdims=True) + jnp.sum(dq_dec * q_dec, axis=-1, keepdims=True)
            dgr = -jnp.sum(pq, axis=0, keepdims=True)
            tt = jnp.sum(dk_tail * k_tail, axis=-1, keepdims=True)
            dk = dk + dk_tail * etail + dkb * bcol
            dgc = dgc - tt
            dgl = dgl + jnp.sum(tt, axis=0, keepdims=True)
            dbeta = dbeta + jnp.sum(dkb * kh, axis=-1, keepdims=True)
            dq_ref[:, sl] = dqs * (DH ** -0.5)
            dq_ref[:, D + h * DH:D + (h + 1) * DH] = dk
            dq_ref[:, 2 * D + h * DH:2 * D + (h + 1) * DH] = dv
            acc[d].append((idx, dgc, dgl, dbeta, dgr))

        dirs = ((qf, kf, vf, abf, abtf, sf_ref, tf_ref, dof, dqf_ref, dcf_ref, drf_ref),
                (qb, kb_, vb, abb, abtb, sb_ref, tb_ref, dob, dqb_ref, dcb_ref, drb_ref))
        chains, ctx_d = [], []
        for d, (q_r, k_r, v_r, ab_r, abt_r, sh_ref, th_ref, do_r, dq_ref, _, _) in enumerate(dirs):
            masks = _masks(d)
            ab, abt = ab_r[...], abt_r[0]
            g_full, beta_full, gc_full, gl_full, gt_full, gct = _decays(
                d, ab, abt, alog_r_, dtb_r_, alog_c_, dtb_c_, masks[0], masks[2])
            ctx_d.append((masks, ab, abt, g_full, beta_full, gt_full))
            for h in range(NH):
                chains.append(chain(d, h, q_r, k_r, v_r, sh_ref, th_ref, do_r, dq_ref, masks,
                                    (beta_full, gc_full, gl_full, gct)))
        _lockstep(chains)
        for d in range(2):
            (incl, _, incl_t, _), ab, abt, g_full, beta_full, gt_full = ctx_d[d]
            dcol_ref, drow_ref = dirs[d][9], dirs[d][10]
            dgc_col = jnp.zeros((CH, LANE), F32)
            dgl_row = jnp.zeros((1, LANE), F32)
            dbeta_col = jnp.zeros((CH, LANE), F32)
            dgc_row = jnp.zeros((4 * NH, CH), F32)
            for idx, dgc, dgl, dbeta, dgr in acc[d]:
                oh = _lane_onehot(idx)
                dgc_col = dgc_col + dgc * oh
                dgl_row = dgl_row + dgl * oh
                dbeta_col = dbeta_col + dbeta * _lane_onehot(2 * NH + idx)
                ohc = (lax.broadcasted_iota(jnp.int32, (4 * NH, 1), 0) == idx).astype(F32)
                dgc_row = dgc_row + ohc * dgr
            dg_col = _mmh(incl_t.astype(F32), dgc_col) + dgl_row
            dg_row = _mmh(dgc_row, incl.astype(F32))
            sg_col = _sigmoid(ab + dtb_r_)
            da_col = dg_col * (-jnp.exp(alog_r_)) * sg_col
            dcol_ref[...] = da_col + dbeta_col * beta_full * (1.0 - beta_full)
            da_row = dg_row * (-jnp.exp(alog_c_)) * _sigmoid(abt + dtb_c_)
            drow_ref[0] = da_row
            vcol_ref[0:1, :] += jnp.sum(dg_col * g_full, axis=0, keepdims=True)
            vcol_ref[1:2, :] += jnp.sum(da_col, axis=0, keepdims=True)
            rl = jnp.sum(dg_row * gt_full, axis=-1, keepdims=True)
            rd = jnp.sum(da_row, axis=-1, keepdims=True)
            vrow_ref[...] += jnp.where(lane2 == 0, rl, 0.0) + jnp.where(lane2 == 1, rd, 0.0)

        @pl.when(pl.program_id(0) == nch // 2)
        def _():
            xc.forward(x_in, x_out, sems)

        @pl.when(pl.program_id(0) == nch - 1)
        def _():
            xc.wait(x_in, x_out, sems)

    def row(c, col):
        return pl.BlockSpec((CH, D), lambda s: (c(s), col))

    def hist(c, n):
        return pl.BlockSpec((1, NH, n, n), lambda s: (c(s), 0, 0, 0))

    def chunk_in(c):
        return [row(c, 0), row(c, 1), row(c, 2), pl.BlockSpec((CH, LANE), lambda s: (c(s), 0)),
                pl.BlockSpec((1, 4 * NH, CH), lambda s: (c(s), 0, 0)), hist(c, DH), hist(c, CH), row(c, 0)]

    small = [_full((1, LANE)), _full((1, LANE)), _full((4 * NH, 1)), _full((4 * NH, 1))]
    return pl.pallas_call(
        body, name="gdn_bwd", grid=(nch,),
        out_shape=(_sds((lt, 3 * D)), _sds((lt, 3 * D)), _sds((lt, LANE)), _sds((lt, LANE)),
                   _sds((nch, 4 * NH, CH)), _sds((nch, 4 * NH, CH)), _sds((8, LANE)), _sds((4 * NH, LANE))) + xc.out_shape,
        in_specs=chunk_in(cf) + chunk_in(cb) + small + [_ANY] * nx,
        out_specs=(pl.BlockSpec((CH, 3 * D), lambda s: (cf(s), 0)), pl.BlockSpec((CH, 3 * D), lambda s: (cb(s), 0)),
                   pl.BlockSpec((CH, LANE), lambda s: (cf(s), 0)), pl.BlockSpec((CH, LANE), lambda s: (cb(s), 0)),
                   pl.BlockSpec((1, 4 * NH, CH), lambda s: (cf(s), 0, 0)), pl.BlockSpec((1, 4 * NH, CH), lambda s: (cb(s), 0, 0)),
                   _full((8, LANE)), _full((4 * NH, LANE))) + tuple([_ANY] * nx),
        scratch_shapes=[pltpu.VMEM((2, NH, DH, DH), F32)] + xc.scratch,
        compiler_params=_params(("arbitrary",)),
    )(qkv, qkv, qkv, pab, abt, s_f, t_f, do, qkv, qkv, qkv, pab, abt, s_b, t_b, do, alog_r, dtb_r, alog_c, dtb_c,
      *xc_arrays)


def _post(p, o_f, o_b, x, tgt, w_pa, w_pb, w_out, w_sp, w_spt, b_spb, ln_g, ln_b, g_on, g_post, gate_x, lc):
    lt = p.shape[0]
    l = x.shape[0]
    tm = GC
    nct = lc // tm

    def body(p_ref, of_ref, ob_ref, x_ref, t_ref, wpa, wpb, wout, wsp, wspt, bspb, lng_ref, lnb_ref, gon_ref, gpost_ref, gate_ref,
             dp_ref, do_ref, dy_ref, ya_ref, yb_ref, mg_ref, da_ref, db_ref, dout_ref, dwsp_ref, dbsp_ref, vec_ref):
        i = pl.program_id(0)

        @pl.when(i == 0)
        def _():
            dwsp_ref[...] = jnp.zeros_like(dwsp_ref)
            dbsp_ref[...] = jnp.zeros_like(dbsp_ref)
            vec_ref[...] = jnp.zeros_like(vec_ref)

        @pl.when(i < nct)
        def _():
            dp_ref[...] = jnp.zeros_like(dp_ref)
            do_ref[...] = jnp.zeros_like(do_ref)

        @pl.when(i >= nct)
        def _():
            lng, lnb, gon, gpost, gate = lng_ref[...], lnb_ref[...], gon_ref[...], gpost_ref[...], gate_ref[...]
            zb, ua, va, za, ga, gb = [p_ref[:, j * D:(j + 1) * D] for j in range(6)]
            o = of_ref[...] + ob_ref[...]
            szb, dszb = _silu_g(zb)
            nh_l, r_l = [], []
            for h in range(NH):
                oh = o[:, h * DH:(h + 1) * DH]
                r = lax.rsqrt(jnp.mean(oh * oh, axis=-1, keepdims=True) + EPS)
                nh_l.append(oh * r)
                r_l.append(r)
            nrm_b = jnp.concatenate(nh_l, axis=-1)
            gon_t = jnp.concatenate([gon] * NH, axis=-1)
            y_b = nrm_b * gon_t * szb
            u, du_dua = _gelu_g(ua)
            gv, dgv_dva = _gelu_g(va)
            xc = gv - jnp.mean(gv, axis=-1, keepdims=True)
            rs_ln = lax.rsqrt(jnp.mean(xc * xc, axis=-1, keepdims=True) + EPS)
            vhat = xc * rs_ln
            v = vhat * lng + lnb
            s_sp = jnp.concatenate(
                [_mm(wsp[g], v[:, g * DH:(g + 1) * DH]) + bspb[g] for g in range(NH)], axis=-1)
            sza, dsza = _silu_g(za)
            y_a = u * s_sp * sza
            a_pr = _mm(y_a, wpa[...])
            b_pr = _mm(y_b, wpb[...])
            sga = _sigmoid(ga)
            sgb = _sigmoid(gb)
            merged = sga * a_pr + sgb * b_pr
            out = _mm(merged, wout[...])
            rs_o = lax.rsqrt(jnp.mean(out * out, axis=-1, keepdims=True) + EPS)
            n_o = out * rs_o
            rr = n_o * gpost
            diff = x_ref[...] + gate * rr - t_ref[...]
            vec_ref[5:6, :] += jnp.sum(diff * diff, axis=0, keepdims=True)
            dy = diff * (1.0 / D)
            dy_ref[...] = dy
            vec_ref[0:1, :] += jnp.sum(dy * rr, axis=0, keepdims=True)
            dr = dy * gate
            vec_ref[1:2, :] += jnp.sum(dr * n_o, axis=0, keepdims=True)
            dn_o = dr * gpost
            dout = rs_o * (dn_o - n_o * jnp.mean(dn_o * n_o, axis=-1, keepdims=True))
            dmerged = _mm(dout, wout[...], NT)
            d_a = dmerged * sga
            d_b = dmerged * sgb
            dga = dmerged * a_pr * sga * (1.0 - sga)
            dgb = dmerged * b_pr * sgb * (1.0 - sgb)
            dy_a = _mm(d_a, wpa[...], NT)
            dy_b = _mm(d_b, wpb[...], NT)
            ya_ref[...] = y_a.astype(ya_ref.dtype).T
            yb_ref[...] = y_b.astype(yb_ref.dtype).T
            mg_ref[...] = merged.astype(mg_ref.dtype).T
            da_ref[...] = d_a.astype(da_ref.dtype)
            db_ref[...] = d_b.astype(db_ref.dtype)
            dout_ref[...] = dout.astype(dout_ref.dtype)
            dua = dy_a * s_sp * sza * du_dua
            ds_sp = dy_a * u * sza
            dza = dy_a * u * s_sp * dsza
            dv_l = []
            for g in range(NH):
                ds_g = ds_sp[:, g * DH:(g + 1) * DH]
                dv_l.append(_mm(wspt[g], ds_g))
                dwsp_ref[g] += _mm(ds_g, v[:, g * DH:(g + 1) * DH], NT)
                dbsp_ref[g] += ds_g
            dv = jnp.concatenate(dv_l, axis=-1)
            vec_ref[2:3, :] += jnp.sum(dv * vhat, axis=0, keepdims=True)
            vec_ref[3:4, :] += jnp.sum(dv, axis=0, keepdims=True)
            dvh = dv * lng
            dgv = rs_ln * (dvh - jnp.mean(dvh, axis=-1, keepdims=True) - vhat * jnp.mean(dvh * vhat, axis=-1, keepdims=True))
            dva = dgv * dgv_dva
            dzb = dy_b * nrm_b * gon_t * dszb
            dgon_full = jnp.sum(dy_b * nrm_b * szb, axis=0, keepdims=True)
            dgon = dgon_full[:, 0:DH]
            for h in range(1, NH):
                dgon = dgon + dgon_full[:, h * DH:(h + 1) * DH]
            vec_ref[4:5, 0:DH] += dgon
            dnb = dy_b * gon_t * szb
            do_l = []
            for h in range(NH):
                sl = slice(h * DH, (h + 1) * DH)
                dn_h = dnb[:, sl]
                do_l.append(r_l[h] * (dn_h - nh_l[h] * jnp.mean(dn_h * nh_l[h], axis=-1, keepdims=True)))
            do_ref[...] = jnp.concatenate(do_l, axis=-1)
            for j, val in enumerate((dzb, dua, dva, dza, dga, dgb)):
                dp_ref[:, j * D:(j + 1) * D] = val.astype(dp_ref.dtype)

    xrow = lambda i: (jnp.maximum(i - nct, 0), 0)
    wspec = _full((D, D))
    gspec = _full((NH, GC, GC))
    vspec = _full((1, D))
    bf_out = _sds((l, D), _BF)
    bf_out_t = _sds((D, l), _BF)
    xcol = lambda i: (0, jnp.maximum(i - nct, 0))
    return pl.pallas_call(
        body, name="post", grid=(lt // tm,),
        out_shape=(_sds((lt, NREST), _BF), _sds((lt, D)), _sds((l, D)), bf_out_t, bf_out_t, bf_out_t, bf_out, bf_out, bf_out,
                   _sds((NH, GC, GC)), _sds((NH, GC, GC)), _sds((8, D))),
        in_specs=[pl.BlockSpec((tm, NREST), lambda i: (i, 0)), pl.BlockSpec((tm, D), lambda i: (i, 0)),
                  pl.BlockSpec((tm, D), lambda i: (i, 0)), pl.BlockSpec((tm, D), xrow), pl.BlockSpec((tm, D), xrow),
                  wspec, wspec, wspec, gspec, gspec, gspec, vspec, vspec, _full((1, DH)), vspec, vspec],
        out_specs=(pl.BlockSpec((tm, NREST), lambda i: (i, 0)), pl.BlockSpec((tm, D), lambda i: (i, 0)),
                   pl.BlockSpec((tm, D), xrow), pl.BlockSpec((D, tm), xcol), pl.BlockSpec((D, tm), xcol),
                   pl.BlockSpec((D, tm), xcol), pl.BlockSpec((tm, D), xrow), pl.BlockSpec((tm, D), xrow),
                   pl.BlockSpec((tm, D), xrow), gspec, gspec, _full((8, D))),
        compiler_params=_params(("arbitrary",)),
    )(p, o_f, o_b, x, tgt, w_pa, w_pb, w_out, w_sp, w_spt, b_spb, ln_g, ln_b, g_on, g_post, gate_x)


def _sum_parts(parts, name):
    r = parts.shape[1]
    tr = r if NDEV * r * LANE * 4 <= (8 << 20) else _tile(r, (512, 256, 128, 64, 32, 16, 8))

    def body(p_ref, o_ref):
        acc = p_ref[0]
        for s in range(1, NDEV):
            acc = acc + p_ref[s]
        o_ref[...] = acc

    return pl.pallas_call(
        body, name=name, out_shape=_sds((r, LANE)), grid=(r // tr,),
        in_specs=[pl.BlockSpec((NDEV, tr, LANE), lambda i: (0, i, 0))],
        out_specs=pl.BlockSpec((tr, LANE), lambda i: (i, 0)),
        compiler_params=_params(("parallel",)),
    )(parts)


def _mod_bwd(c_all, c_ctx, dmx, dmc, w_mod_g):
    ws = w_mod_g.shape[2]

    def body(ca_ref, cc_ref, dsh_ref, dmx_ref, dmc_ref, dmc_sh_ref, w_ref, gw_ref, gc_ref, gb_ref):
        sc, _ = _silu_g(ca_ref[...])
        scc, dscc = _silu_g(cc_ref[...])
        dmc_tot = jnp.sum(dmc_ref[...], axis=0, keepdims=True)
        gb_ref[...] = jnp.sum(dmx_ref[...], axis=0, keepdims=True) + dmc_tot
        lhs = jnp.concatenate([sc, jnp.broadcast_to(scc, (8, D))], axis=0)
        rhs = jnp.concatenate([dsh_ref[...], dmc_sh_ref[...]], axis=0)
        gw_ref[...] = _mmh(lhs, rhs, TN)
        acc = jnp.zeros((8, D), F32)
        tot8 = jnp.broadcast_to(dmc_tot, (8, 3 * D))
        for j in range(NDEV):
            acc = acc + _mm(tot8[:, j * ws:(j + 1) * ws], w_ref[j], NT)
        gc_ref[...] = acc[0:1, :] * dscc

    return pl.pallas_call(
        body, name="mod_bwd", out_shape=(_sds((D, ws)), _sds((1, D)), _sds((1, 3 * D))),
        compiler_params=_params(),
    )(c_all, c_ctx, _my_cols(dmx, ws), dmx, dmc, _my_cols(dmc, ws), w_mod_g)


def _my_cols(a, ws):
    me = 4 * lax.axis_index("x") + 2 * lax.axis_index("y") + lax.axis_index("c")
    return lax.dynamic_slice_in_dim(a, me * ws, ws, axis=1)


def _pair_sum(mine, other, name):
    n, r, c = mine.shape
    tr = _tile(r, (256, 128, 64, 32, 16, 8))

    def body(a_ref, b_ref, o_ref):
        o_ref[...] = (a_ref[...].astype(F32) + b_ref[...].astype(F32)).astype(o_ref.dtype)

    blk = pl.BlockSpec((1, tr, c), lambda j, i: (j, i, 0))
    return pl.pallas_call(
        body, name=name, out_shape=_sds((n, r, c), mine.dtype), grid=(n, r // tr),
        in_specs=[blk, blk], out_specs=blk, compiler_params=_params(("parallel", "parallel")),
    )(mine, other)


def _adamw(parts, w, m, v, name, chip_sums_below=None):
    s_, r, c = parts.shape
    tr = _tile(r, (128, 64, 32, 16, 8)) if r * c * 4 > (1 << 20) else r
    c1 = 1.0 / (1.0 - ADAM_B1 ** ADAM_STEP)
    c2 = 1.0 / (1.0 - ADAM_B2 ** ADAM_STEP)

    def body(p_ref, w_ref, m_ref, v_ref, g_ref, d_ref, nm_ref, nv_ref):
        if chip_sums_below is None:
            part = lambda s: p_ref[s].astype(F32)
        else:
            core = lax.axis_index("c")
            me = 4 * lax.axis_index("x") + 2 * lax.axis_index("y") + core
            every = me >= chip_sums_below
            part = lambda s: jnp.where(every | (core == s % 2), p_ref[s].astype(F32), 0.0)
        g = part(0)
        for s in range(1, s_):
            g = g + part(s)
        m_new = ADAM_B1 * m_ref[...] + (1.0 - ADAM_B1) * g
        v_new = ADAM_B2 * v_ref[...] + (1.0 - ADAM_B2) * (g * g)
        g_ref[...] = g
        nm_ref[...] = m_new
        nv_ref[...] = v_new
        d_ref[...] = -ADAM_LR * ((m_new * c1) / (jnp.sqrt(v_new * c2) + ADAM_EPS) + ADAM_WD * w_ref[...])

    blk = pl.BlockSpec((tr, c), lambda i: (i, 0))
    o = _sds((r, c))
    return pl.pallas_call(
        body, name=name, out_shape=(o, o, o, o), grid=(r // tr,),
        in_specs=[pl.BlockSpec((s_, tr, c), lambda i: (0, i, 0)), blk, blk, blk],
        out_specs=(blk, blk, blk, blk),
        compiler_params=_params(("parallel",)),
    )(parts, w, m, v)


def _rows(a):
    flat = a.reshape(-1)
    n = flat.shape[0]
    r = -(-n // (8 * LANE)) * 8
    return jnp.pad(flat, (0, r * LANE - n)).reshape(r, LANE)


def _pack(items):
    parts, layout, at = [], [], 0
    for name, a in items:
        rws = _rows(a.astype(F32))
        layout.append((name, at, rws.shape[0], a.shape))
        parts.append(rws)
        at += rws.shape[0]
    return jnp.concatenate(parts, axis=0), layout


def _unpack(packed, layout):
    out = {}
    for name, at, r, shape in layout:
        n = 1
        for s in shape:
            n *= s
        out[name] = packed[at:at + r].reshape(-1)[:n].reshape(shape)
    return out


def kernel(x, c, ctx, c_ctx, w_mod, b_mod, g_pre, g_post, w_in, w_conv, a_log, dt_bias, g_onorm, gm_ln_g, gm_ln_b, w_sp, b_sp, w_pa, w_pb, w_out, loss_target, m_c_ctx, m_w_mod, m_b_mod, m_g_pre, m_g_post, m_w_in, m_w_conv, m_a_log, m_dt_bias, m_g_onorm, m_gm_ln_g, m_gm_ln_b, m_w_sp, m_b_sp, m_w_pa, m_w_pb, m_w_out, v_c_ctx, v_w_mod, v_b_mod, v_g_pre, v_g_post, v_w_in, v_w_conv, v_a_log, v_dt_bias, v_g_onorm, v_gm_ln_g, v_gm_ln_b, v_w_sp, v_b_sp, v_w_pa, v_w_pb, v_w_out):
    l = x.shape[1]
    lc = ctx.shape[1]
    lt = l + lc
    nch = lt // CH
    me = 4 * lax.axis_index("x") + 2 * lax.axis_index("y") + lax.axis_index("c")
    wsh = w_in.shape[2]
    off_a = 3 * D
    n_ab = 4 * NH
    jb = off_a // wsh
    o1 = off_a - jb * wsh
    o2 = o1 + n_ab
    assert o2 <= wsh and NREST == (NDEV - jb) * wsh - o2
    split = jb + 1

    w_in_bf = w_in[0].astype(_BF)
    wg_lo, wg_mod, wg_conv, c_all = _exchange(
        [w_in_bf, w_mod[0].astype(_BF), w_conv[0], c], ["gather_lo", "gather", "gather", "gather"],
        "gather_first", split)
    w_qkv = jnp.concatenate([wg_lo[j][:, :wsh] for j in range(jb)] + [wg_lo[jb][:, :o1]], axis=1)
    w_ab = jnp.pad(wg_lo[jb][:, o1:o2], ((0, 0), (0, LANE - n_ab)))
    wconv_full = jnp.moveaxis(wg_conv, 0, 1).reshape(3, 3 * D)
    c_all = c_all.reshape(NDEV, D)

    cc = jnp.concatenate([c, c_ctx.reshape(1, D), jnp.zeros((6, D), F32)], axis=0)
    mods = _modulation(cc, wg_mod, b_mod)
    h, h_t = _prenorm(ctx[0], x[0], mods, g_pre)
    p_qkv = _matmul_nn(h, w_qkv, "in_proj_qkv")
    pab = _matmul_nn(h, w_ab, "in_proj_ab")
    abt = jnp.swapaxes(pab[:, :n_ab].reshape(nch, CH, n_ab), 1, 2)
    alog16, dtb16 = a_log.reshape(1, 2 * NH), dt_bias.reshape(1, 2 * NH)
    alog_r = jnp.pad(alog16, ((0, 0), (0, LANE - 2 * NH)))
    dtb_r = jnp.pad(dtb16, ((0, 0), (0, LANE - 2 * NH)))
    alog_c = jnp.pad(alog16.reshape(2 * NH, 1), ((0, 2 * NH), (0, 0)))
    dtb_c = jnp.pad(dtb16.reshape(2 * NH, 1), ((0, 2 * NH), (0, 0)))
    qkv = _qkv_fwd(p_qkv, wconv_full, lc)
    late = [w_in_bf, w_pa[0].astype(_BF), w_pb[0].astype(_BF), w_out[0].astype(_BF)]
    xc_late = _Exchange(zip(late, ["gather_hi", "gather", "gather", "gather"]), split)
    o_f, o_b, s_f, s_b, t_f, t_b, wg_hi, wg_pa, wg_pb, wg_out = _gdn_fwd(
        qkv, pab, abt, alog_r, dtb_r, alog_c, dtb_c, lc, xc_late, late)
    w_rest = jnp.concatenate([wg_lo[jb][:, o2:wsh]] + [wg_hi[j][:, :wsh] for j in range(split, NDEV)], axis=1)
    wf_pa, wf_pb, wf_out = wg_pa.reshape(D, D), wg_pb.reshape(D, D), wg_out.reshape(D, D)
    p_rest = _matmul_nn(h, w_rest, "in_proj_rest")

    w_spt = jnp.swapaxes(w_sp[0], 1, 2)
    b_spb = jnp.broadcast_to(b_sp[0][:, :, None], (NH, GC, GC))
    gate_x = mods[0:1, 2 * D:]
    dp_rest, do, dy, ya, yb, mg, d_a, d_b, dout, dwsp, dbsp_l, pvec = _post(
        p_rest, o_f, o_b, x[0], loss_target[0], wf_pa, wf_pb, wf_out, w_sp[0], w_spt, b_spb, gm_ln_g, gm_ln_b,
        g_onorm, g_post, gate_x, lc)

    dw_rest = _matmul_nn(h_t, dp_rest, "dw_in_rest", _BF)
    o3 = wsh - o2
    chunks_hi = jnp.moveaxis(dw_rest[:, o3:].reshape(D, NDEV - split, wsh), 1, 0)
    dw_pa = _matmul_nn(ya, d_a, "dw_pa", _BF).reshape(NDEV, D // NDEV, D)
    dw_pb = _matmul_nn(yb, d_b, "dw_pb", _BF).reshape(NDEV, D // NDEV, D)
    dw_out = _matmul_nn(mg, dout, "dw_out", _BF).reshape(NDEV, D // NDEV, D)
    small_a, lay_a = _pack([
        ("g_post", pvec[1]), ("g_onorm", pvec[4, :DH]), ("gm_ln_g", pvec[2]), ("gm_ln_b", pvec[3]), ("w_sp", dwsp),
        ("b_sp", jnp.sum(dbsp_l, axis=-1)), ("loss", pvec[5]), ("dgate", pvec[0])])
    (theirs_hi,) = _exchange([chunks_hi], ["sibling"], "pair_swap_hi")
    chip_hi = _pair_sum(chunks_hi, theirs_hi[0], "pair_sum_hi")
    early = [chip_hi, dw_pa, dw_pb, dw_out, small_a]
    xc_early = _Exchange(zip(early, ["scatter_par_hi", "scatter", "scatter", "scatter", "gather"]), split)

    dqkv_f, dqkv_b, dcol_f, dcol_b, drow_f, drow_b, gvec_c, gvec_r, r_in, r_pa, r_pb, r_out, small_a_all = _gdn_bwd(
        qkv, pab, abt, alog_r, dtb_r, alog_c, dtb_c, s_f, s_b, t_f, t_b, do, lc, xc_early, early)
    dp_qkv, dwconv = _qkv_bwd(p_qkv, wconv_full, dqkv_f, dqkv_b, lc)
    drow = jnp.swapaxes(drow_f + drow_b, 1, 2).reshape(lt, n_ab)
    dpab = (dcol_f + dcol_b + jnp.pad(drow, ((0, 0), (0, LANE - n_ab)))).astype(_BF)

    dw_qkv = _matmul_nn(h_t, dp_qkv, "dw_in_qkv", _BF)
    dw_ab = _matmul_nn(h_t, dpab, "dw_in_ab", _BF)
    dw_lo = jnp.concatenate([dw_qkv, dw_ab[:, :n_ab], dw_rest[:, :o3]], axis=1)
    chunks_lo = jnp.moveaxis(dw_lo.reshape(D, split, wsh), 1, 0)
    (theirs,) = _exchange([chunks_lo], ["sibling"], "pair_swap")
    chip_lo = _pair_sum(chunks_lo, theirs[0], "pair_sum")
    xc_last = _Exchange([(chip_lo, "scatter_par_lo")], split)
    dh, r_in = _dh_matmul(dp_rest, dp_qkv, dpab, w_rest, w_qkv, w_ab, xc_last, [chip_lo], {0: r_in})
    grad_x, nvec = _prenorm_bwd(ctx[0], x[0], dh, dy, mods, g_pre)

    dalog = gvec_c[0, :2 * NH] + gvec_r[:2 * NH, 0]
    ddtb = gvec_c[1, :2 * NH] + gvec_r[:2 * NH, 1]
    small_b, lay_b = _pack([
        ("g_pre", nvec[4]), ("a_log", dalog), ("dt_bias", ddtb), ("w_conv", dwconv),
        ("dshift", nvec[0]), ("dscale", nvec[1]), ("dshift_c", nvec[2]), ("dscale_c", nvec[3])])
    (small_b_all,) = _exchange([small_b], ["gather"], "gather_small")
    tot = _unpack(_sum_parts(small_a_all, "sum_small_a"), lay_a)
    tot.update(_unpack(_sum_parts(small_b_all, "sum_small_b"), lay_b))

    def per_device(packed_all, layout, name):
        at, r = [(a_, r_) for nm, a_, r_, _ in layout if nm == name][0]
        return packed_all[:, at:at + r].reshape(NDEV, -1)

    dmx_all = jnp.concatenate([per_device(small_b_all, lay_b, "dshift"), per_device(small_b_all, lay_b, "dscale"),
                               per_device(small_a_all, lay_a, "dgate")], axis=1)
    dmc_all = jnp.concatenate([per_device(small_b_all, lay_b, "dshift_c"), per_device(small_b_all, lay_b, "dscale_c"),
                               jnp.zeros((NDEV, D), F32)], axis=1)
    g_wmod, g_cctx, g_bmod = _mod_bwd(c_all, c_ctx.reshape(1, D), dmx_all, dmc_all, wg_mod)
    loss = 0.5 / D * jnp.sum(tot["loss"])
    ws_conv = w_conv.shape[2]
    g_wconv = lax.dynamic_slice_in_dim(tot["w_conv"], me * ws_conv, ws_conv, axis=1)

    small_names = ["c_ctx", "b_mod", "g_pre", "g_post", "a_log", "dt_bias", "g_onorm", "gm_ln_g", "gm_ln_b",
                   "w_sp", "b_sp", "w_conv"]
    wts = dict(c_ctx=c_ctx, b_mod=b_mod, g_pre=g_pre, g_post=g_post, a_log=a_log, dt_bias=dt_bias, g_onorm=g_onorm,
               gm_ln_g=gm_ln_g, gm_ln_b=gm_ln_b, w_sp=w_sp, b_sp=b_sp, w_conv=w_conv)
    ms = dict(c_ctx=m_c_ctx, b_mod=m_b_mod, g_pre=m_g_pre, g_post=m_g_post, a_log=m_a_log, dt_bias=m_dt_bias,
              g_onorm=m_g_onorm, gm_ln_g=m_gm_ln_g, gm_ln_b=m_gm_ln_b, w_sp=m_w_sp, b_sp=m_b_sp, w_conv=m_w_conv)
    vs = dict(c_ctx=v_c_ctx, b_mod=v_b_mod, g_pre=v_g_pre, g_post=v_g_post, a_log=v_a_log, dt_bias=v_dt_bias,
              g_onorm=v_g_onorm, gm_ln_g=v_gm_ln_g, gm_ln_b=v_gm_ln_b, w_sp=v_w_sp, b_sp=v_b_sp, w_conv=v_w_conv)
    gs = dict(tot)
    gs.update(c_ctx=g_cctx, b_mod=g_bmod, w_conv=g_wconv)
    gpk, play = _pack([(nm, gs[nm].reshape(wts[nm].shape)) for nm in small_names])
    wpk, _ = _pack([(nm, wts[nm]) for nm in small_names])
    mpk, _ = _pack([(nm, ms[nm]) for nm in small_names])
    vpk, _ = _pack([(nm, vs[nm]) for nm in small_names])
    res_small = [_unpack(a, play) for a in _adamw(gpk[None], wpk, mpk, vpk, "adamw_small")]
    res_big = {
        "w_mod": _adamw(g_wmod[None], w_mod[0], m_w_mod[0], v_w_mod[0], "adamw_w_mod"),
        "w_in": _adamw(r_in, w_in[0], m_w_in[0], v_w_in[0], "adamw_w_in", chip_sums_below=NDEV),
        "w_pa": _adamw(r_pa, w_pa[0], m_w_pa[0], v_w_pa[0], "adamw_w_pa"),
        "w_pb": _adamw(r_pb, w_pb[0], m_w_pb[0], v_w_pb[0], "adamw_w_pb"),
        "w_out": _adamw(r_out, w_out[0], m_w_out[0], v_w_out[0], "adamw_w_out"),
    }
    order = ["c_ctx", "w_mod", "b_mod", "g_pre", "g_post", "w_in", "w_conv", "a_log", "dt_bias", "g_onorm",
             "gm_ln_g", "gm_ln_b", "w_sp", "b_sp", "w_pa", "w_pb", "w_out"]
    outs = [loss, grad_x[None]]
    for k in range(4):
        for nm in order:
            if nm in res_big:
                outs.append(res_big[nm][k][None])
            else:
                outs.append(res_small[k][nm])
    return tuple(outs)
```

```python
import functools

import jax
import jax.numpy as jnp
from jax import lax
from jax.experimental import pallas as pl
from jax.experimental.pallas import tpu as pltpu

F32 = jnp.float32
_BF = jnp.bfloat16
_HI = lax.Precision.HIGHEST
D = 1024
NH = 8
DH = 128
CH = 64
SUB = 2
GC = 128
NREST = 6 * D
NMAIN = NREST + 3 * D
EPS = 1e-6
LANE = 128
NDEV = 8
VMEM_LIMIT = 56 * 1024 * 1024
MESH = pl.DeviceIdType.MESH

ADAM_LR, ADAM_B1, ADAM_B2, ADAM_EPS, ADAM_WD, ADAM_STEP = 0.001, 0.9, 0.999, 1e-08, 0.01, 10

NN = ((1,), (0,))
NT = ((1,), (1,))
TN = ((0,), (0,))


def _dot(a, b, dims=NN, prec=None):
    return lax.dot_general(a, b, (dims, ((), ())), precision=prec, preferred_element_type=F32)


def _mm(a, b, dims=NN):
    return _dot(a.astype(_BF), b.astype(_BF), dims)


def _mmh(a, b, dims=NN):
    return _dot(a.astype(F32), b.astype(F32), dims, _HI)


def _split(a):
    hi = a.astype(_BF)
    return hi, (a - hi.astype(F32)).astype(_BF)


def _mm3(a, b, dims=NN):
    ah, al = _split(a)
    bh, bl = _split(b)
    return _dot(ah, bh, dims) + (_dot(ah, bl, dims) + _dot(al, bh, dims))


def _sigmoid(x):
    return 1.0 / (1.0 + jnp.exp(-x))


def _silu_g(x):
    s = _sigmoid(x)
    return x * s, s * (1.0 + x * (1.0 - s))


def _gelu_g(x):
    c = 0.7978845608028654
    t = jnp.tanh(c * (x + 0.044715 * (x * x * x)))
    cdf = 0.5 * (1.0 + t)
    return x * cdf, cdf + 0.5 * x * (1.0 - t * t) * c * (1.0 + 3 * 0.044715 * x * x)


def _softplus(x):
    return jnp.maximum(x, 0.0) + jnp.log(1.0 + jnp.exp(-jnp.abs(x)))


def _params(sem=None):
    return pltpu.CompilerParams(dimension_semantics=sem, vmem_limit_bytes=VMEM_LIMIT)


def _tile(n, pref):
    for t in pref:
        if n % t == 0:
            return t
    return n


def _full(shape):
    nd = len(shape)
    return pl.BlockSpec(shape, lambda *_: (0,) * nd)


def _sds(shape, dt=F32):
    return jax.ShapeDtypeStruct(shape, dt)


MAX_PIECES = 12
PIECE_BYTES = 256 * 1024


def _piece_slices(shape, itemsize):
    total = itemsize
    for d in shape:
        total *= d
    want = min(MAX_PIECES, total // PIECE_BYTES)
    lead = shape[0] if len(shape) >= 3 else 1
    rows = shape[-2] if len(shape) >= 2 else 1
    if want < 2 or lead > want:
        return [()]
    m = max([n for n in (8, 4, 2, 1) if n * lead <= want and rows % (16 * n) == 0], default=1)
    if m * lead < 2:
        return [()]
    rs = rows // m
    mid = (slice(None),) * max(len(shape) - 3, 0)
    if len(shape) >= 3:
        return [(i,) + mid + (pl.ds(j * rs, rs),) for i in range(lead) for j in range(m)]
    return [(pl.ds(j * rs, rs),) for j in range(m)]


class _Pieces:
    def __init__(self, copies):
        self.copies = copies

    def start(self):
        for cp in self.copies:
            cp.start()

    def wait_send(self):
        for cp in self.copies:
            cp.wait_send()

    def wait_recv(self):
        for cp in self.copies:
            cp.wait_recv()

    def wait(self):
        for cp in self.copies:
            cp.wait()


class _Exchange:
    def __init__(self, specs, split):
        self.specs = list(specs)
        self.split = split
        self.n = len(self.specs)
        def out(a, k):
            if k == "sibling":
                return (1,) + tuple(a.shape)
            return (NDEV,) + (tuple(a.shape) if k.startswith("gather") else tuple(a.shape[1:]))

        self.out_shape = tuple(_sds(out(a, k), a.dtype) for a, k in self.specs)
        self.pieces = [_piece_slices(o.shape[1:], jnp.dtype(o.dtype).itemsize) for o in self.out_shape]
        self.sem_base = [(NDEV - 1) * sum(len(p) for p in self.pieces[:a]) for a in range(self.n + 1)]
        self.scratch = [pltpu.SemaphoreType.DMA((self.sem_base[-1],)), pltpu.SemaphoreType.DMA((self.sem_base[-1],)),
                        pltpu.SemaphoreType.DMA((self.sem_base[-1] // (NDEV - 1),))]

    def _local(self, sems, a, src, dst):
        base = self.sem_base[a] // (NDEV - 1)
        return _Pieces([pltpu.make_async_copy(src.at[sl] if sl else src, dst.at[sl] if sl else dst, sems[2].at[base + p])
                        for p, sl in enumerate(self.pieces[a])])

    def _remote(self, sems, a, k, src, dst, to):
        send_sems, recv_sems, _ = sems
        base = self.sem_base[a] + k * len(self.pieces[a])
        return _Pieces([
            pltpu.make_async_remote_copy(
                src_ref=src.at[sl] if sl else src, dst_ref=dst.at[sl] if sl else dst, send_sem=send_sems.at[base + p],
                recv_sem=recv_sems.at[base + p], device_id=to, device_id_type=MESH)
            for p, sl in enumerate(self.pieces[a])])

    def _ok(self, kind, idx):
        if kind.endswith("_lo"):
            return idx < self.split
        if kind.endswith("_hi"):
            return idx >= self.split
        return True

    def _phases(self, ins, outs, sems):
        x, y, c = lax.axis_index("x"), lax.axis_index("y"), lax.axis_index("c")
        me = 4 * x + 2 * y + c
        sib = (x, y, 1 - c)
        sib_idx = 4 * x + 2 * y + (1 - c)
        chips = [(1 - x, y), (x, 1 - y), (1 - x, 1 - y)]
        starts, forwards, waits = [], [], []
        for a, (_, kind) in enumerate(self.specs):
            ok = functools.partial(self._ok, kind)
            if kind.startswith("gather"):
                def copy(k, block, to, src=None, a=a):
                    rows = outs[a].at[block]
                    return self._remote(sems, a, k, rows if src is None else src, rows, to)

                loc = self._local(sems, a, ins[a], outs[a].at[me])
                first = [copy(0, me, sib, ins[a])] + [copy(1 + j, me, (*chip, c), ins[a]) for j, chip in enumerate(chips)]
                starts += [(ok(me), loc.start)] + [(ok(me), cp.start) for cp in first]
                waits += [(ok(me), loc.wait)] + [(ok(me), cp.wait_send) for cp in first]
                for j, chip in enumerate(chips):
                    origin = 4 * chip[0] + 2 * chip[1] + c
                    passed = copy(4 + j, origin, sib)
                    forwards += [(ok(origin), copy(1 + j, origin, sib).wait_recv), (ok(origin), passed.start)]
                    waits.append((ok(origin), passed.wait_send))
                    other = 4 * chip[0] + 2 * chip[1] + (1 - c)
                    waits.append((ok(other), copy(4 + j, other, sib).wait_recv))
                waits.append((ok(sib_idx), copy(0, sib_idx, sib).wait_recv))
            elif kind == "sibling":
                swap = self._remote(sems, a, 0, ins[a], outs[a].at[0], sib)
                starts.append((True, swap.start))
                waits += [(True, swap.wait_send), (True, swap.wait_recv)]
            else:
                base = self.split if kind.endswith("_hi") else 0
                same_core_only = "_par" in kind

                def src(idx, a=a, base=base):
                    return ins[a].at[jnp.clip(idx - base, 0, ins[a].shape[0] - 1)]

                loc = self._local(sems, a, src(me), outs[a].at[me])
                starts.append((ok(me), loc.start))
                waits.append((ok(me), loc.wait))
                for k in range(1, NDEV):
                    if same_core_only and k & 1:
                        continue
                    px = 1 - x if (k >> 2) & 1 else x
                    py = 1 - y if (k >> 1) & 1 else y
                    pc = 1 - c if k & 1 else c
                    pidx = 4 * px + 2 * py + pc
                    send = self._remote(sems, a, k - 1, src(pidx), outs[a].at[me], (px, py, pc))
                    arrive = self._remote(sems, a, k - 1, src(pidx), outs[a].at[pidx], (px, py, pc))
                    starts.append((ok(pidx), send.start))
                    waits += [(ok(pidx), send.wait_send), (ok(me), arrive.wait_recv)]
        return starts, forwards, waits

    @staticmethod
    def _run(actions):
        for cond, fn in actions:
            if cond is True:
                fn()
            else:
                pl.when(cond)(fn)

    def start(self, ins, outs, sems):
        self._run(self._phases(ins, outs, sems)[0])

    def forward(self, ins, outs, sems):
        self._run(self._phases(ins, outs, sems)[1])

    def wait(self, ins, outs, sems):
        self._run(self._phases(ins, outs, sems)[2])


_ANY = pl.BlockSpec(memory_space=pl.ANY)


def _exchange(arrays, kinds, name, split=0, into=None):
    xc = _Exchange(zip(arrays, kinds), split)
    n = xc.n
    into = into or {}
    ni = len(into)

    def body(*refs):
        ins, outs, sems = refs[:n], refs[n + ni:2 * n + ni], refs[2 * n + ni:]
        xc.start(ins, outs, sems)
        xc.forward(ins, outs, sems)
        xc.wait(ins, outs, sems)

    return pl.pallas_call(
        body, name=name, out_shape=xc.out_shape, in_specs=[_ANY] * (n + ni), out_specs=tuple([_ANY] * n),
        scratch_shapes=xc.scratch, input_output_aliases={n + t: a for t, a in enumerate(into)},
    )(*arrays, *into.values())


def _matmul_nn(a, b, name, out_dtype=F32):
    m, kk = a.shape
    n = b.shape[1]
    tm = m if m * kk * a.dtype.itemsize <= (12 << 20) else _tile(m, (1088, 1024, 640, 512, 256, 128))
    tn = _tile(n, (512, 256, 128))

    def body(a_ref, b_ref, o_ref):
        o_ref[...] = _mm(a_ref[...], b_ref[...]).astype(o_ref.dtype)

    return pl.pallas_call(
        body, name=name, out_shape=_sds((m, n), out_dtype), grid=(n // tn, m // tm),
        in_specs=[pl.BlockSpec((tm, kk), lambda j, i: (i, 0)), pl.BlockSpec((kk, tn), lambda j, i: (0, j))],
        out_specs=pl.BlockSpec((tm, tn), lambda j, i: (i, j)),
        compiler_params=_params(("parallel", "parallel")),
    )(a, b)


def _matmul_tn(a, b, name):
    kk, m = a.shape
    n = b.shape[1]
    tk = _tile(kk, (1088, 1024, 640, 512, 256, 128))
    tn = _tile(n, (1024, 512, 256, 128))
    nk = kk // tk

    def body(a_ref, b_ref, o_ref, acc_ref):
        k = pl.program_id(1)

        @pl.when(k == 0)
        def _():
            acc_ref[...] = jnp.zeros_like(acc_ref)

        acc_ref[...] += _mm(a_ref[...], b_ref[...], TN)

        @pl.when(k == nk - 1)
        def _():
            o_ref[...] = acc_ref[...].astype(o_ref.dtype)

    return pl.pallas_call(
        body, name=name, out_shape=_sds((m, n), _BF), grid=(n // tn, nk),
        in_specs=[pl.BlockSpec((tk, m), lambda j, k: (k, 0)), pl.BlockSpec((tk, tn), lambda j, k: (k, j))],
        out_specs=pl.BlockSpec((m, tn), lambda j, k: (0, j)),
        scratch_shapes=[pltpu.VMEM((m, tn), F32)],
        compiler_params=_params(("parallel", "arbitrary")),
    )(a, b)


def _dh_matmul(dp_rest, dp_qkv, dpab, w_rest, w_qkv, w_ab, xc, xc_arrays, xc_into):
    lt = dp_rest.shape[0]
    tm = _tile(lt, (1088, 1024, 640, 512, 256, 128))
    nr, nq = dp_rest.shape[1] // D, dp_qkv.shape[1] // D
    nx, ni = xc.n, len(xc_into)
    ni_steps = lt // tm

    def body(*refs):
        dr_ref, dq_ref, ab_ref, wr_ref, wq_ref, wab_ref = refs[:6]
        x_in = refs[6:6 + nx]
        o_ref = refs[6 + nx + ni]
        x_out = refs[7 + nx + ni:7 + 2 * nx + ni]
        sems = refs[7 + 2 * nx + ni:]
        i = pl.program_id(0)
        k = pl.program_id(1)

        @pl.when((i == 0) & (k == 0))
        def _():
            xc.start(x_in, x_out, sems)

        @pl.when(k == 0)
        def _():
            o_ref[...] = _mm(ab_ref[...], wab_ref[...], NT)

        @pl.when(k < nr)
        def _():
            o_ref[...] += _mm(dr_ref[...], wr_ref[...], NT)

        @pl.when(k >= nr)
        def _():
            o_ref[...] += _mm(dq_ref[...], wq_ref[...], NT)

        @pl.when((i == ni_steps - 1) & (k == nr + nq - 1))
        def _():
            xc.wait(x_in, x_out, sems)

    rk = lambda k: jnp.minimum(k, nr - 1)
    qk = lambda k: jnp.maximum(k - nr, 0)
    return pl.pallas_call(
        body, name="dh_matmul", out_shape=(_sds((lt, D)),) + xc.out_shape, grid=(lt // tm, nr + nq),
        in_specs=[pl.BlockSpec((tm, D), lambda i, k: (i, rk(k))), pl.BlockSpec((tm, D), lambda i, k: (i, qk(k))),
                  pl.BlockSpec((tm, LANE), lambda i, k: (i, 0)),
                  pl.BlockSpec((D, D), lambda i, k: (0, rk(k))), pl.BlockSpec((D, D), lambda i, k: (0, qk(k))),
                  _full((D, LANE))] + [_ANY] * (nx + ni),
        out_specs=(pl.BlockSpec((tm, D), lambda i, k: (i, 0)),) + tuple([_ANY] * nx),
        scratch_shapes=xc.scratch, input_output_aliases={6 + nx + t: 1 + a for t, a in enumerate(xc_into)},
        compiler_params=_params(("arbitrary", "arbitrary")),
    )(dp_rest, dp_qkv, dpab, w_rest, w_qkv, w_ab, *xc_arrays, *xc_into.values())


def _modulation(cc, w_mod_g, b_mod):
    ws = w_mod_g.shape[2]

    def body(c_ref, w_ref, b_ref, o_ref):
        s, _ = _silu_g(c_ref[...])
        o_ref[...] = _mm(s, w_ref[0]) + b_ref[...]

    return pl.pallas_call(
        body, name="modulation", out_shape=_sds((8, 3 * D)), grid=(NDEV,),
        in_specs=[_full((8, D)), pl.BlockSpec((1, D, ws), lambda j: (j, 0, 0)), pl.BlockSpec((1, ws), lambda j: (0, j))],
        out_specs=pl.BlockSpec((8, ws), lambda j: (0, j)),
        compiler_params=_params(("parallel",)),
    )(cc, w_mod_g, b_mod)


def _prenorm(ctx, x, mods, g_pre):
    lc = ctx.shape[0]
    lt = lc + x.shape[0]
    tm = _tile(lc, (256, 128))
    nct = lc // tm

    def body(c_ref, x_ref, m_ref, g_ref, o_ref, ot_ref):
        is_ctx = pl.program_id(0) < nct
        x = jnp.where(is_ctx, c_ref[...], x_ref[...])
        shift = jnp.where(is_ctx, m_ref[1:2, 0:D], m_ref[0:1, 0:D])
        scale = jnp.where(is_ctx, m_ref[1:2, D:2 * D], m_ref[0:1, D:2 * D])
        r = lax.rsqrt(jnp.mean(x * x, axis=-1, keepdims=True) + EPS)
        h = ((x * r * g_ref[...]) * (1.0 + scale) + shift).astype(o_ref.dtype)
        o_ref[...] = h
        ot_ref[...] = h.T

    return pl.pallas_call(
        body, name="prenorm", out_shape=(_sds((lt, D), _BF), _sds((D, lt), _BF)), grid=(lt // tm,),
        in_specs=[pl.BlockSpec((tm, D), lambda i: (jnp.minimum(i, nct - 1), 0)),
                  pl.BlockSpec((tm, D), lambda i: (jnp.maximum(i - nct, 0), 0)), _full((8, 3 * D)), _full((1, D))],
        out_specs=(pl.BlockSpec((tm, D), lambda i: (i, 0)), pl.BlockSpec((D, tm), lambda i: (0, i))),
        compiler_params=_params(("parallel",)),
    )(ctx, x, mods, g_pre)


def _prenorm_bwd(ctx, x, dh, dy, mods, g_pre):
    lc = ctx.shape[0]
    lt = lc + x.shape[0]
    tm = _tile(lc, (256, 128))
    nct = lc // tm
    nl = (lt - lc) // tm

    def body(c_ref, x_ref, dh_ref, dy_ref, m_ref, g_ref, gx_ref, vec_ref):
        i = pl.program_id(0)

        @pl.when(i == 0)
        def _():
            vec_ref[...] = jnp.zeros_like(vec_ref)

        is_ctx = i < nct
        x = jnp.where(is_ctx, c_ref[...], x_ref[...])
        dh = dh_ref[...]
        g = g_ref[...]
        scale = jnp.where(is_ctx, m_ref[1:2, D:2 * D], m_ref[0:1, D:2 * D])
        r = lax.rsqrt(jnp.mean(x * x, axis=-1, keepdims=True) + EPS)
        n = x * r
        hn = n * g
        dsh = jnp.sum(dh, axis=0, keepdims=True)
        dsc = jnp.sum(dh * hn, axis=0, keepdims=True)
        dhn = dh * (1.0 + scale)
        vec_ref[4:5, :] += jnp.sum(dhn * n, axis=0, keepdims=True)
        dn = dhn * g
        dx = r * (dn - n * jnp.mean(dn * n, axis=-1, keepdims=True))

        @pl.when(is_ctx)
        def _():
            vec_ref[2:3, :] += dsh
            vec_ref[3:4, :] += dsc

        @pl.when(jnp.logical_not(is_ctx))
        def _():
            vec_ref[0:1, :] += dsh
            vec_ref[1:2, :] += dsc
            gx_ref[...] = dy_ref[...] + dx

    xrow = lambda i: (jnp.maximum(i - nct, 0), 0)
    return pl.pallas_call(
        body, name="prenorm_bwd", out_shape=(_sds((nl * tm, D)), _sds((8, D))), grid=(lt // tm,),
        in_specs=[pl.BlockSpec((tm, D), lambda i: (jnp.minimum(i, nct - 1), 0)), pl.BlockSpec((tm, D), xrow),
                  pl.BlockSpec((tm, D), lambda i: (i, 0)), pl.BlockSpec((tm, D), xrow), _full((8, 3 * D)), _full((1, D))],
        out_specs=(pl.BlockSpec((tm, D), xrow), _full((8, D))),
        compiler_params=_params(("arbitrary",)),
    )(ctx, x, dh, dy, mods, g_pre)


def _conv_parts(x, w, lc):
    lt = x.shape[0]
    row = lax.broadcasted_iota(jnp.int32, x.shape, 0)
    first = (row == 0) | (row == lc)
    last = (row == lc - 1) | (row == lt - 1)
    xp = jnp.where(first, 0.0, pltpu.roll(x, 1, 0))
    xn = jnp.where(last, 0.0, pltpu.roll(x, lt - 1, 0))
    y = w[0:1, :] * xp + w[1:2, :] * x + w[2:3, :] * xn
    return xp, xn, y, first, last


def _qkv_fwd(p, w_conv, lc):
    lt = p.shape[0]

    def body(p_ref, w_ref, o_ref):
        _, _, y, _, _ = _conv_parts(p_ref[...], w_ref[...], lc)
        s, _ = _silu_g(y)
        rs = lax.rsqrt(jnp.sum(s * s, axis=-1, keepdims=True) + EPS)
        o_ref[...] = s * jnp.where(pl.program_id(0) < 2 * NH, rs, 1.0)

    return pl.pallas_call(
        body, name="qkv_fwd", out_shape=_sds((lt, 3 * D)), grid=(3 * NH,),
        in_specs=[pl.BlockSpec((lt, DH), lambda j: (0, j)), pl.BlockSpec((3, DH), lambda j: (0, j))],
        out_specs=pl.BlockSpec((lt, DH), lambda j: (0, j)),
        compiler_params=_params(("parallel",)),
    )(p, w_conv)


def _qkv_bwd(p, w_conv, dqkv_f, dqkv_b, lc):
    lt = p.shape[0]

    def body(p_ref, w_ref, df_ref, db_ref, dp_ref, dw_ref):
        w = w_ref[...]
        xp, xn, y, first, last = _conv_parts(p_ref[...], w, lc)
        s, ds_dy = _silu_g(y)
        dn = df_ref[...] + db_ref[...]
        rs = lax.rsqrt(jnp.sum(s * s, axis=-1, keepdims=True) + EPS)
        nrm = s * rs
        ds_n = rs * (dn - nrm * jnp.sum(dn * nrm, axis=-1, keepdims=True))
        ds = jnp.where(pl.program_id(0) < 2 * NH, ds_n, dn)
        dy = ds * ds_dy
        dw_ref[0:1, :] = jnp.sum(dy * xp, axis=0, keepdims=True)
        dw_ref[1:2, :] = jnp.sum(dy * p_ref[...], axis=0, keepdims=True)
        dw_ref[2:3, :] = jnp.sum(dy * xn, axis=0, keepdims=True)
        dyn = jnp.where(last, 0.0, pltpu.roll(dy, lt - 1, 0))
        dyp = jnp.where(first, 0.0, pltpu.roll(dy, 1, 0))
        dp_ref[...] = (w[1:2, :] * dy + w[0:1, :] * dyn + w[2:3, :] * dyp).astype(dp_ref.dtype)

    return pl.pallas_call(
        body, name="qkv_bwd", out_shape=(_sds((lt, 3 * D), _BF), _sds((3, 3 * D))), grid=(3 * NH,),
        in_specs=[pl.BlockSpec((lt, DH), lambda j: (0, j)), pl.BlockSpec((3, DH), lambda j: (0, j)),
                  pl.BlockSpec((lt, DH), lambda j: (0, j)), pl.BlockSpec((lt, DH), lambda j: (0, j))],
        out_specs=(pl.BlockSpec((lt, DH), lambda j: (0, j)), pl.BlockSpec((3, DH), lambda j: (0, j))),
        compiler_params=_params(("parallel",)),
    )(p, w_conv, dqkv_f, dqkv_b)


def _masks(d):
    ri = lax.broadcasted_iota(jnp.int32, (CH, CH), 0)
    ci = lax.broadcasted_iota(jnp.int32, (CH, CH), 1)
    incl = (ri >= ci) if d == 0 else (ri <= ci)
    strict = (ri > ci) if d == 0 else (ri < ci)
    incl_t = (ri <= ci) if d == 0 else (ri >= ci)
    return incl, strict, incl_t, ri == ci


def _decays(d, ab, abt, alog_r, dtb_r, alog_c, dtb_c, incl, incl_t):
    g_full = -jnp.exp(alog_r) * _softplus(ab + dtb_r)
    beta_full = _sigmoid(ab)
    gc_full = _mmh(incl.astype(F32), g_full)
    gl_full = jnp.sum(g_full, axis=0, keepdims=True)
    gt_full = -jnp.exp(alog_c) * _softplus(abt + dtb_c)
    gct = _mmh(gt_full, incl_t.astype(F32))
    return g_full, beta_full, gc_full, gl_full, gt_full, gct


def _lane_onehot(idx, n=LANE):
    return (lax.broadcasted_iota(jnp.int32, (1, n), 1) == idx).astype(F32)


def _head_scalars(d, h, beta_full, gc_full, gl_full, gct):
    idx = d * NH + h
    oh = _lane_onehot(idx)
    gcol = jnp.sum(gc_full * oh, axis=-1, keepdims=True)
    bcol = jnp.sum(beta_full * _lane_onehot(2 * NH + idx), axis=-1, keepdims=True)
    gl = jnp.sum(gl_full * oh, axis=-1, keepdims=True)
    grow = gct[idx:idx + 1, :]
    return gcol, grow, bcol, gl


def _lockstep(gens):
    live = list(gens)
    while live:
        nxt = []
        for g in live:
            try:
                next(g)
                nxt.append(g)
            except StopIteration:
                pass
        live = nxt


def _chunk_local(qh, kh, vh, gcol, grow, bcol, gl, incl, strict):
    decay = jnp.where(incl, jnp.exp(gcol - grow), 0.0)
    kb = kh * bcol
    a = jnp.where(strict, _mm(kb, kh, NT) * decay, 0.0)
    egc = jnp.exp(gcol)
    rhs_u = vh * bcol
    rhs_w = kb * egc
    qs = qh * (DH ** -0.5)
    attn = jnp.where(incl, _mm(qs, kh, NT) * decay, 0.0)
    etail = jnp.exp(gl - gcol)
    return decay, kb, a, egc, rhs_u, rhs_w, qs, attn, etail


def _scan_specs(lt, lc, bwd_pass):
    assert lt % (SUB * CH) == 0 and lc % (SUB * CH) == 0
    nch = lt // (SUB * CH)
    ncc = lc // (SUB * CH)
    if not bwd_pass:
        cf = lambda s: s
        cb = lambda s: jnp.where(s < ncc, ncc - 1 - s, nch + ncc - 1 - s)
    else:
        cf = lambda s: nch - 1 - s
        cb = lambda s: jnp.where(s < nch - ncc, ncc + s, s - (nch - ncc))
    return nch, cf, cb


def _gdn_fwd(qkv, pab, abt, alog_r, dtb_r, alog_c, dtb_c, lc, xc, xc_arrays):
    lt = qkv.shape[0]
    nch, cf, cb = _scan_specs(lt, lc, False)
    nx = xc.n

    def body(*refs):
        qf, kf, vf, abf, abtf, qb, kb_, vb, abb, abtb, ar, dr, ac, dc = refs[:14]
        x_in = refs[14:14 + nx]
        of_ref, ob_ref, sf_ref, sb_ref, tf_ref, tb_ref = refs[14 + nx:20 + nx]
        x_out = refs[20 + nx:20 + 2 * nx]
        s_scr = refs[20 + 2 * nx]
        sems = refs[21 + 2 * nx:]

        @pl.when(pl.program_id(0) == 0)
        def _():
            s_scr[...] = jnp.zeros_like(s_scr)
            xc.start(x_in, x_out, sems)

        def chain(d, h, c, late, q_r, k_r, v_r, o_ref, sh_ref, th_ref, masks, decs):
            incl, strict, _, eye = masks
            sl = slice(h * DH, (h + 1) * DH)
            rows = slice(c * CH, (c + 1) * CH)
            qh, kh, vh = q_r[rows, sl], k_r[rows, sl], v_r[rows, sl]
            gcol, grow, bcol, gl = _head_scalars(d, h, *decs)
            _, _, a, egc, rhs_u, rhs_w, qs, attn, etail = _chunk_local(qh, kh, vh, gcol, grow, bcol, gl, incl, strict)
            yield
            n = -a
            t = jnp.where(eye, 1.0, 0.0) + n
            p = _mm3(n, n)
            yield
            for _ in range(4):
                r = _mm3(jnp.concatenate([t, p], axis=0), p)
                yield
                t = t + r[:CH]
                p = r[CH:]
            t = t + _mm3(t, p)
            yield
            for _ in range(3 if late else 0):
                yield
            sol = _mm3(t, jnp.concatenate([rhs_u, rhs_w], axis=1))
            u, w = sol[:, :DH], sol[:, DH:]
            s = s_scr[d, h]
            sh_ref[c, h] = s
            th_ref[c, h] = t
            yield
            ws = _mm(jnp.concatenate([w, qs * egc], axis=0), s)
            yield
            v_new = u - ws[:CH]
            o_ref[rows, sl] = ws[CH:] + _mm(attn, v_new)
            s_scr[d, h] = s * jnp.exp(gl) + _mm(kh * etail, v_new, TN)

        chains = []
        for d, (q_r, k_r, v_r, ab_r, abt_r, o_ref, sh_ref, th_ref) in enumerate(
                ((qf, kf, vf, abf, abtf, of_ref, sf_ref, tf_ref), (qb, kb_, vb, abb, abtb, ob_ref, sb_ref, tb_ref))):
            masks = _masks(d)
            for pos, c in enumerate(range(SUB) if d == 0 else reversed(range(SUB))):
                _, beta_full, gc_full, gl_full, _, gct = _decays(
                    d, ab_r[c * CH:(c + 1) * CH, :], abt_r[c], ar[...], dr[...], ac[...], dc[...], masks[0], masks[2])
                for h in range(NH):
                    chains.append(chain(d, h, c, pos > 0, q_r, k_r, v_r, o_ref, sh_ref, th_ref, masks,
                                        (beta_full, gc_full, gl_full, gct)))
        _lockstep(chains)

        @pl.when(pl.program_id(0) == nch // 2)
        def _():
            xc.forward(x_in, x_out, sems)

        @pl.when(pl.program_id(0) == nch - 1)
        def _():
            xc.wait(x_in, x_out, sems)

    def row(c, col):
        return pl.BlockSpec((SUB * CH, D), lambda s: (c(s), col))

    def chunk_in(c):
        return [row(c, 0), row(c, 1), row(c, 2), pl.BlockSpec((SUB * CH, LANE), lambda s: (c(s), 0)),
                pl.BlockSpec((SUB, 4 * NH, CH), lambda s: (c(s), 0, 0))]

    def hist(c, n):
        return pl.BlockSpec((SUB, NH, n, n), lambda s: (c(s), 0, 0, 0))

    small = [_full((1, LANE)), _full((1, LANE)), _full((4 * NH, 1)), _full((4 * NH, 1))]
    return pl.pallas_call(
        body, name="gdn_fwd", grid=(nch,),
        out_shape=(_sds((lt, D)), _sds((lt, D)), _sds((lt // CH, NH, DH, DH)), _sds((lt // CH, NH, DH, DH)),
                   _sds((lt // CH, NH, CH, CH)), _sds((lt // CH, NH, CH, CH))) + xc.out_shape,
        in_specs=chunk_in(cf) + chunk_in(cb) + small + [_ANY] * nx,
        out_specs=(pl.BlockSpec((SUB * CH, D), lambda s: (cf(s), 0)), pl.BlockSpec((SUB * CH, D), lambda s: (cb(s), 0)),
                   hist(cf, DH), hist(cb, DH), hist(cf, CH), hist(cb, CH)) + tuple([_ANY] * nx),
        scratch_shapes=[pltpu.VMEM((2, NH, DH, DH), F32)] + xc.scratch,
        compiler_params=_params(("arbitrary",)),
    )(qkv, qkv, qkv, pab, abt, qkv, qkv, qkv, pab, abt, alog_r, dtb_r, alog_c, dtb_c, *xc_arrays)


def _gdn_bwd(qkv, pab, abt, alog_r, dtb_r, alog_c, dtb_c, s_f, s_b, t_f, t_b, do, lc, xc, xc_arrays):
    lt = qkv.shape[0]
    nch, cf, cb = _scan_specs(lt, lc, True)
    nx = xc.n

    def body(*refs):
        qf, kf, vf, abf, abtf, sf_ref, tf_ref, dof, qb, kb_, vb, abb, abtb, sb_ref, tb_ref, dob, ar, dr, ac, dc = refs[:20]
        x_in = refs[20:20 + nx]
        dqf_ref, dqb_ref, dcf_ref, dcb_ref, drf_ref, drb_ref, vcol_ref, vrow_ref = refs[20 + nx:28 + nx]
        x_out = refs[28 + nx:28 + 2 * nx]
        ds_scr = refs[28 + 2 * nx]
        sems = refs[29 + 2 * nx:]

        @pl.when(pl.program_id(0) == 0)
        def _():
            ds_scr[...] = jnp.zeros_like(ds_scr)
            vcol_ref[...] = jnp.zeros_like(vcol_ref)
            vrow_ref[...] = jnp.zeros_like(vrow_ref)
            xc.start(x_in, x_out, sems)

        alog_r_, dtb_r_, alog_c_, dtb_c_ = ar[...], dr[...], ac[...], dc[...]
        lane2 = lax.broadcasted_iota(jnp.int32, (1, LANE), 1)
        acc = {}

        def chain(d, h, c, late, q_r, k_r, v_r, sh_ref, th_ref, do_r, dq_ref, masks, decs):
            incl, strict, _, _ = masks
            idx = d * NH + h
            sl = slice(h * DH, (h + 1) * DH)
            rows = slice(c * CH, (c + 1) * CH)
            qh, kh, vh = q_r[rows, sl], k_r[rows, sl], v_r[rows, sl]
            doh = do_r[rows, sl]
            gcol, grow, bcol, gl = _head_scalars(d, h, *decs)
            decay, kb, a, egc, rhs_u, rhs_w, qs, attn, etail = _chunk_local(qh, kh, vh, gcol, grow, bcol, gl, incl, strict)
            t = th_ref[c, h]
            s = sh_ref[c, h]
            sol = _mm3(t, jnp.concatenate([rhs_u, rhs_w], axis=1))
            u, w = sol[:, :DH], sol[:, DH:]
            q_dec = qs * egc
            k_tail = kh * etail
            egl = jnp.exp(gl)
            dq_dec = _mm(doh, s, NT)
            yield
            for _ in range(2 if late else 0):
                yield
            ds_new = ds_scr[d, h]
            dv_new = _mm(attn, doh, TN) + _mm(k_tail, ds_new)
            dgl = jnp.sum(jnp.sum(ds_new * s, axis=0, keepdims=True), axis=-1, keepdims=True) * egl
            yield
            v_new = u - _mm(w, s)
            dw = -_mm(dv_new, s, NT)
            ds_scr[d, h] = ds_new * egl + _mm(q_dec, doh, TN) - _mm(w, dv_new, TN)
            yield
            dattn = jnp.where(incl, _mm(doh, v_new, NT), 0.0)
            dk_tail = _mm(v_new, ds_new, NT)
            dr = _mm3(t, jnp.concatenate([dv_new, dw], axis=1), TN)
            dr_u, dr_w = dr[:, :DH], dr[:, DH:]
            yield
            da = -jnp.where(strict, _mm3(dr, sol, NT), 0.0)
            nq = dattn * decay
            dqs = _mm(nq, kh) + dq_dec * egc
            dk = _mm(nq, qs, TN)
            yield
            dv = dr_u * bcol
            dbeta = jnp.sum(dr_u * vh, axis=-1, keepdims=True)
            dgc = jnp.sum(dr_w * rhs_w, axis=-1, keepdims=True)
            m = da * decay
            dkb = dr_w * egc + _mm(m, kh)
            dk = dk + _mm(m, kb, TN)
            pq = da * a + dattn * attn
            dgc = dgc + jnp.sum(pq, axis=-1, keepdims=True) + jnp.sum(dq_dec * q_dec, axis=-1, keepdims=True)
            dgr = -jnp.sum(pq, axis=0, keepdims=True)
            tt = jnp.sum(dk_tail * k_tail, axis=-1, keepdims=True)
            dk = dk + dk_tail * etail + dkb * bcol
            dgc = dgc - tt
            dgl = dgl + jnp.sum(tt, axis=0, keepdims=True)
            dbeta = dbeta + jnp.sum(dkb * kh, axis=-1, keepdims=True)
            dq_ref[rows, sl] = dqs * (DH ** -0.5)
            dq_ref[rows, D + h * DH:D + (h + 1) * DH] = dk
            dq_ref[rows, 2 * D + h * DH:2 * D + (h + 1) * DH] = dv
            acc.setdefault((d, c), []).append((idx, dgc, dgl, dbeta, dgr))

        dirs = ((qf, kf, vf, abf, abtf, sf_ref, tf_ref, dof, dqf_ref, dcf_ref, drf_ref),
                (qb, kb_, vb, abb, abtb, sb_ref, tb_ref, dob, dqb_ref, dcb_ref, drb_ref))
        chains, ctx_d = [], {}
        for d, (q_r, k_r, v_r, ab_r, abt_r, sh_ref, th_ref, do_r, dq_ref, _, _) in enumerate(dirs):
            masks = _masks(d)
            for pos, c in enumerate(reversed(range(SUB)) if d == 0 else range(SUB)):
                ab, abt = ab_r[c * CH:(c + 1) * CH, :], abt_r[c]
                g_full, beta_full, gc_full, gl_full, gt_full, gct = _decays(
                    d, ab, abt, alog_r_, dtb_r_, alog_c_, dtb_c_, masks[0], masks[2])
                ctx_d[(d, c)] = (masks, ab, abt, g_full, beta_full, gt_full)
                for h in range(NH):
                    chains.append(chain(d, h, c, pos > 0, q_r, k_r, v_r, sh_ref, th_ref, do_r, dq_ref, masks,
                                        (beta_full, gc_full, gl_full, gct)))
        _lockstep(chains)
        for d, c in sorted(ctx_d):
            (incl, _, incl_t, _), ab, abt, g_full, beta_full, gt_full = ctx_d[(d, c)]
            dcol_ref, drow_ref = dirs[d][9], dirs[d][10]
            dgc_col = jnp.zeros((CH, LANE), F32)
            dgl_row = jnp.zeros((1, LANE), F32)
            dbeta_col = jnp.zeros((CH, LANE), F32)
            dgc_row = jnp.zeros((4 * NH, CH), F32)
            for idx, dgc, dgl, dbeta, dgr in acc[(d, c)]:
                oh = _lane_onehot(idx)
                dgc_col = dgc_col + dgc * oh
                dgl_row = dgl_row + dgl * oh
                dbeta_col = dbeta_col + dbeta * _lane_onehot(2 * NH + idx)
                ohc = (lax.broadcasted_iota(jnp.int32, (4 * NH, 1), 0) == idx).astype(F32)
                dgc_row = dgc_row + ohc * dgr
            dg_col = _mmh(incl_t.astype(F32), dgc_col) + dgl_row
            dg_row = _mmh(dgc_row, incl.astype(F32))
            sg_col = _sigmoid(ab + dtb_r_)
            da_col = dg_col * (-jnp.exp(alog_r_)) * sg_col
            dcol_ref[c * CH:(c + 1) * CH, :] = da_col + dbeta_col * beta_full * (1.0 - beta_full)
            da_row = dg_row * (-jnp.exp(alog_c_)) * _sigmoid(abt + dtb_c_)
            drow_ref[c] = da_row
            vcol_ref[0:1, :] += jnp.sum(dg_col * g_full, axis=0, keepdims=True)
            vcol_ref[1:2, :] += jnp.sum(da_col, axis=0, keepdims=True)
            rl = jnp.sum(dg_row * gt_full, axis=-1, keepdims=True)
            rd = jnp.sum(da_row, axis=-1, keepdims=True)
            vrow_ref[...] += jnp.where(lane2 == 0, rl, 0.0) + jnp.where(lane2 == 1, rd, 0.0)

        @pl.when(pl.program_id(0) == nch // 2)
        def _():
            xc.forward(x_in, x_out, sems)

        @pl.when(pl.program_id(0) == nch - 1)
        def _():
            xc.wait(x_in, x_out, sems)

    def row(c, col):
        return pl.BlockSpec((SUB * CH, D), lambda s: (c(s), col))

    def hist(c, n):
        return pl.BlockSpec((SUB, NH, n, n), lambda s: (c(s), 0, 0, 0))

    def chunk_in(c):
        return [row(c, 0), row(c, 1), row(c, 2), pl.BlockSpec((SUB * CH, LANE), lambda s: (c(s), 0)),
                pl.BlockSpec((SUB, 4 * NH, CH), lambda s: (c(s), 0, 0)), hist(c, DH), hist(c, CH), row(c, 0)]

    small = [_full((1, LANE)), _full((1, LANE)), _full((4 * NH, 1)), _full((4 * NH, 1))]
    return pl.pallas_call(
        body, name="gdn_bwd", grid=(nch,),
        out_shape=(_sds((lt, 3 * D)), _sds((lt, 3 * D)), _sds((lt, LANE)), _sds((lt, LANE)),
                   _sds((lt // CH, 4 * NH, CH)), _sds((lt // CH, 4 * NH, CH)), _sds((8, LANE)), _sds((4 * NH, LANE))) + xc.out_shape,
        in_specs=chunk_in(cf) + chunk_in(cb) + small + [_ANY] * nx,
        out_specs=(pl.BlockSpec((SUB * CH, 3 * D), lambda s: (cf(s), 0)), pl.BlockSpec((SUB * CH, 3 * D), lambda s: (cb(s), 0)),
                   pl.BlockSpec((SUB * CH, LANE), lambda s: (cf(s), 0)), pl.BlockSpec((SUB * CH, LANE), lambda s: (cb(s), 0)),
                   pl.BlockSpec((SUB, 4 * NH, CH), lambda s: (cf(s), 0, 0)), pl.BlockSpec((SUB, 4 * NH, CH), lambda s: (cb(s), 0, 0)),
                   _full((8, LANE)), _full((4 * NH, LANE))) + tuple([_ANY] * nx),
        scratch_shapes=[pltpu.VMEM((2, NH, DH, DH), F32)] + xc.scratch,
        compiler_params=_params(("arbitrary",)),
    )(qkv, qkv, qkv, pab, abt, s_f, t_f, do, qkv, qkv, qkv, pab, abt, s_b, t_b, do, alog_r, dtb_r, alog_c, dtb_c,
      *xc_arrays)


def _post(p, o_f, o_b, x, tgt, w_pa, w_pb, w_out, w_sp, w_spt, b_spb, ln_g, ln_b, g_on, g_post, gate_x, lc):
    lt = p.shape[0]
    l = x.shape[0]
    tm = GC
    nct = lc // tm

    def body(p_ref, of_ref, ob_ref, x_ref, t_ref, wpa, wpb, wout, wsp, wspt, bspb, lng_ref, lnb_ref, gon_ref, gpost_ref, gate_ref,
             dp_ref, do_ref, dy_ref, ya_ref, yb_ref, mg_ref, da_ref, db_ref, dout_ref, dwsp_ref, dbsp_ref, vec_ref):
        i = pl.program_id(0)

        @pl.when(i == 0)
        def _():
            dwsp_ref[...] = jnp.zeros_like(dwsp_ref)
            dbsp_ref[...] = jnp.zeros_like(dbsp_ref)
            vec_ref[...] = jnp.zeros_like(vec_ref)

        @pl.when(i < nct)
        def _():
            dp_ref[...] = jnp.zeros_like(dp_ref)
            do_ref[...] = jnp.zeros_like(do_ref)

        @pl.when(i >= nct)
        def _():
            lng, lnb, gon, gpost, gate = lng_ref[...], lnb_ref[...], gon_ref[...], gpost_ref[...], gate_ref[...]
            zb, ua, va, za, ga, gb = [p_ref[:, j * D:(j + 1) * D] for j in range(6)]
            o = of_ref[...] + ob_ref[...]
            szb, dszb = _silu_g(zb)
            nh_l, r_l = [], []
            for h in range(NH):
                oh = o[:, h * DH:(h + 1) * DH]
                r = lax.rsqrt(jnp.mean(oh * oh, axis=-1, keepdims=True) + EPS)
                nh_l.append(oh * r)
                r_l.append(r)
            nrm_b = jnp.concatenate(nh_l, axis=-1)
            gon_t = jnp.concatenate([gon] * NH, axis=-1)
            y_b = nrm_b * gon_t * szb
            u, du_dua = _gelu_g(ua)
            gv, dgv_dva = _gelu_g(va)
            xc = gv - jnp.mean(gv, axis=-1, keepdims=True)
            rs_ln = lax.rsqrt(jnp.mean(xc * xc, axis=-1, keepdims=True) + EPS)
            vhat = xc * rs_ln
            v = vhat * lng + lnb
            s_sp = jnp.concatenate(
                [_mm(wsp[g], v[:, g * DH:(g + 1) * DH]) + bspb[g] for g in range(NH)], axis=-1)
            sza, dsza = _silu_g(za)
            y_a = u * s_sp * sza
            a_pr = _mm(y_a, wpa[...])
            b_pr = _mm(y_b, wpb[...])
            sga = _sigmoid(ga)
            sgb = _sigmoid(gb)
            merged = sga * a_pr + sgb * b_pr
            out = _mm(merged, wout[...])
            rs_o = lax.rsqrt(jnp.mean(out * out, axis=-1, keepdims=True) + EPS)
            n_o = out * rs_o
            rr = n_o * gpost
            diff = x_ref[...] + gate * rr - t_ref[...]
            vec_ref[5:6, :] += jnp.sum(diff * diff, axis=0, keepdims=True)
            dy = diff * (1.0 / D)
            dy_ref[...] = dy
            vec_ref[0:1, :] += jnp.sum(dy * rr, axis=0, keepdims=True)
            dr = dy * gate
            vec_ref[1:2, :] += jnp.sum(dr * n_o, axis=0, keepdims=True)
            dn_o = dr * gpost
            dout = rs_o * (dn_o - n_o * jnp.mean(dn_o * n_o, axis=-1, keepdims=True))
            dmerged = _mm(dout, wout[...], NT)
            d_a = dmerged * sga
            d_b = dmerged * sgb
            dga = dmerged * a_pr * sga * (1.0 - sga)
            dgb = dmerged * b_pr * sgb * (1.0 - sgb)
            dy_a = _mm(d_a, wpa[...], NT)
            dy_b = _mm(d_b, wpb[...], NT)
            ya_ref[...] = y_a.astype(ya_ref.dtype).T
            yb_ref[...] = y_b.astype(yb_ref.dtype).T
            mg_ref[...] = merged.astype(mg_ref.dtype).T
            da_ref[...] = d_a.astype(da_ref.dtype)
            db_ref[...] = d_b.astype(db_ref.dtype)
            dout_ref[...] = dout.astype(dout_ref.dtype)
            dua = dy_a * s_sp * sza * du_dua
            ds_sp = dy_a * u * sza
            dza = dy_a * u * s_sp * dsza
            dv_l = []
            for g in range(NH):
                ds_g = ds_sp[:, g * DH:(g + 1) * DH]
                dv_l.append(_mm(wspt[g], ds_g))
                dwsp_ref[g] += _mm(ds_g, v[:, g * DH:(g + 1) * DH], NT)
                dbsp_ref[g] += ds_g
            dv = jnp.concatenate(dv_l, axis=-1)
            vec_ref[2:3, :] += jnp.sum(dv * vhat, axis=0, keepdims=True)
            vec_ref[3:4, :] += jnp.sum(dv, axis=0, keepdims=True)
            dvh = dv * lng
            dgv = rs_ln * (dvh - jnp.mean(dvh, axis=-1, keepdims=True) - vhat * jnp.mean(dvh * vhat, axis=-1, keepdims=True))
            dva = dgv * dgv_dva
            dzb = dy_b * nrm_b * gon_t * dszb
            dgon_full = jnp.sum(dy_b * nrm_b * szb, axis=0, keepdims=True)
            dgon = dgon_full[:, 0:DH]
            for h in range(1, NH):
                dgon = dgon + dgon_full[:, h * DH:(h + 1) * DH]
            vec_ref[4:5, 0:DH] += dgon
            dnb = dy_b * gon_t * szb
            do_l = []
            for h in range(NH):
                sl = slice(h * DH, (h + 1) * DH)
                dn_h = dnb[:, sl]
                do_l.append(r_l[h] * (dn_h - nh_l[h] * jnp.mean(dn_h * nh_l[h], axis=-1, keepdims=True)))
            do_ref[...] = jnp.concatenate(do_l, axis=-1)
            for j, val in enumerate((dzb, dua, dva, dza, dga, dgb)):
                dp_ref[:, j * D:(j + 1) * D] = val.astype(dp_ref.dtype)

    xrow = lambda i: (jnp.maximum(i - nct, 0), 0)
    wspec = _full((D, D))
    gspec = _full((NH, GC, GC))
    vspec = _full((1, D))
    bf_out = _sds((l, D), _BF)
    bf_out_t = _sds((D, l), _BF)
    xcol = lambda i: (0, jnp.maximum(i - nct, 0))
    return pl.pallas_call(
        body, name="post", grid=(lt // tm,),
        out_shape=(_sds((lt, NREST), _BF), _sds((lt, D)), _sds((l, D)), bf_out_t, bf_out_t, bf_out_t, bf_out, bf_out, bf_out,
                   _sds((NH, GC, GC)), _sds((NH, GC, GC)), _sds((8, D))),
        in_specs=[pl.BlockSpec((tm, NREST), lambda i: (i, 0)), pl.BlockSpec((tm, D), lambda i: (i, 0)),
                  pl.BlockSpec((tm, D), lambda i: (i, 0)), pl.BlockSpec((tm, D), xrow), pl.BlockSpec((tm, D), xrow),
                  wspec, wspec, wspec, gspec, gspec, gspec, vspec, vspec, _full((1, DH)), vspec, vspec],
        out_specs=(pl.BlockSpec((tm, NREST), lambda i: (i, 0)), pl.BlockSpec((tm, D), lambda i: (i, 0)),
                   pl.BlockSpec((tm, D), xrow), pl.BlockSpec((D, tm), xcol), pl.BlockSpec((D, tm), xcol),
                   pl.BlockSpec((D, tm), xcol), pl.BlockSpec((tm, D), xrow), pl.BlockSpec((tm, D), xrow),
                   pl.BlockSpec((tm, D), xrow), gspec, gspec, _full((8, D))),
        compiler_params=_params(("arbitrary",)),
    )(p, o_f, o_b, x, tgt, w_pa, w_pb, w_out, w_sp, w_spt, b_spb, ln_g, ln_b, g_on, g_post, gate_x)


def _sum_parts(parts, name):
    r = parts.shape[1]
    tr = r if NDEV * r * LANE * 4 <= (8 << 20) else _tile(r, (512, 256, 128, 64, 32, 16, 8))

    def body(p_ref, o_ref):
        acc = p_ref[0]
        for s in range(1, NDEV):
            acc = acc + p_ref[s]
        o_ref[...] = acc

    return pl.pallas_call(
        body, name=name, out_shape=_sds((r, LANE)), grid=(r // tr,),
        in_specs=[pl.BlockSpec((NDEV, tr, LANE), lambda i: (0, i, 0))],
        out_specs=pl.BlockSpec((tr, LANE), lambda i: (i, 0)),
        compiler_params=_params(("parallel",)),
    )(parts)


def _mod_bwd(c_all, c_ctx, dmx, dmc, w_mod_g):
    ws = w_mod_g.shape[2]

    def body(ca_ref, cc_ref, dsh_ref, dmx_ref, dmc_ref, dmc_sh_ref, w_ref, gw_ref, gc_ref, gb_ref):
        sc, _ = _silu_g(ca_ref[...])
        scc, dscc = _silu_g(cc_ref[...])
        dmc_tot = jnp.sum(dmc_ref[...], axis=0, keepdims=True)
        gb_ref[...] = jnp.sum(dmx_ref[...], axis=0, keepdims=True) + dmc_tot
        lhs = jnp.concatenate([sc, jnp.broadcast_to(scc, (8, D))], axis=0)
        rhs = jnp.concatenate([dsh_ref[...], dmc_sh_ref[...]], axis=0)
        gw_ref[...] = _mmh(lhs, rhs, TN)
        acc = jnp.zeros((8, D), F32)
        tot8 = jnp.broadcast_to(dmc_tot, (8, 3 * D))
        for j in range(NDEV):
            acc = acc + _mm(tot8[:, j * ws:(j + 1) * ws], w_ref[j], NT)
        gc_ref[...] = acc[0:1, :] * dscc

    return pl.pallas_call(
        body, name="mod_bwd", out_shape=(_sds((D, ws)), _sds((1, D)), _sds((1, 3 * D))),
        compiler_params=_params(),
    )(c_all, c_ctx, _my_cols(dmx, ws), dmx, dmc, _my_cols(dmc, ws), w_mod_g)


def _my_cols(a, ws):
    me = 4 * lax.axis_index("x") + 2 * lax.axis_index("y") + lax.axis_index("c")
    return lax.dynamic_slice_in_dim(a, me * ws, ws, axis=1)


def _pair_sum(mine, other, name):
    n, r, c = mine.shape
    tr = _tile(r, (256, 128, 64, 32, 16, 8))

    def body(a_ref, b_ref, o_ref):
        o_ref[...] = (a_ref[...].astype(F32) + b_ref[...].astype(F32)).astype(o_ref.dtype)

    blk = pl.BlockSpec((1, tr, c), lambda j, i: (j, i, 0))
    return pl.pallas_call(
        body, name=name, out_shape=_sds((n, r, c), mine.dtype), grid=(n, r // tr),
        in_specs=[blk, blk], out_specs=blk, compiler_params=_params(("parallel", "parallel")),
    )(mine, other)


def _adamw(parts, w, m, v, name, chip_sums_below=None):
    s_, r, c = parts.shape
    tr = _tile(r, (128, 64, 32, 16, 8)) if r * c * 4 > (1 << 20) else r
    c1 = 1.0 / (1.0 - ADAM_B1 ** ADAM_STEP)
    c2 = 1.0 / (1.0 - ADAM_B2 ** ADAM_STEP)

    def body(p_ref, w_ref, m_ref, v_ref, g_ref, d_ref, nm_ref, nv_ref):
        if chip_sums_below is None:
            part = lambda s: p_ref[s].astype(F32)
        else:
            core = lax.axis_index("c")
            me = 4 * lax.axis_index("x") + 2 * lax.axis_index("y") + core
            every = me >= chip_sums_below
            part = lambda s: jnp.where(every | (core == s % 2), p_ref[s].astype(F32), 0.0)
        g = part(0)
        for s in range(1, s_):
            g = g + part(s)
        m_new = ADAM_B1 * m_ref[...] + (1.0 - ADAM_B1) * g
        v_new = ADAM_B2 * v_ref[...] + (1.0 - ADAM_B2) * (g * g)
        g_ref[...] = g
        nm_ref[...] = m_new
        nv_ref[...] = v_new
        d_ref[...] = -ADAM_LR * ((m_new * c1) / (jnp.sqrt(v_new * c2) + ADAM_EPS) + ADAM_WD * w_ref[...])

    blk = pl.BlockSpec((tr, c), lambda i: (i, 0))
    o = _sds((r, c))
    return pl.pallas_call(
        body, name=name, out_shape=(o, o, o, o), grid=(r // tr,),
        in_specs=[pl.BlockSpec((s_, tr, c), lambda i: (0, i, 0)), blk, blk, blk],
        out_specs=(blk, blk, blk, blk),
        compiler_params=_params(("parallel",)),
    )(parts, w, m, v)


def _rows(a):
    flat = a.reshape(-1)
    n = flat.shape[0]
    r = -(-n // (8 * LANE)) * 8
    return jnp.pad(flat, (0, r * LANE - n)).reshape(r, LANE)


def _pack(items):
    parts, layout, at = [], [], 0
    for name, a in items:
        rws = _rows(a.astype(F32))
        layout.append((name, at, rws.shape[0], a.shape))
        parts.append(rws)
        at += rws.shape[0]
    return jnp.concatenate(parts, axis=0), layout


def _unpack(packed, layout):
    out = {}
    for name, at, r, shape in layout:
        n = 1
        for s in shape:
            n *= s
        out[name] = packed[at:at + r].reshape(-1)[:n].reshape(shape)
    return out


def kernel(x, c, ctx, c_ctx, w_mod, b_mod, g_pre, g_post, w_in, w_conv, a_log, dt_bias, g_onorm, gm_ln_g, gm_ln_b, w_sp, b_sp, w_pa, w_pb, w_out, loss_target, m_c_ctx, m_w_mod, m_b_mod, m_g_pre, m_g_post, m_w_in, m_w_conv, m_a_log, m_dt_bias, m_g_onorm, m_gm_ln_g, m_gm_ln_b, m_w_sp, m_b_sp, m_w_pa, m_w_pb, m_w_out, v_c_ctx, v_w_mod, v_b_mod, v_g_pre, v_g_post, v_w_in, v_w_conv, v_a_log, v_dt_bias, v_g_onorm, v_gm_ln_g, v_gm_ln_b, v_w_sp, v_b_sp, v_w_pa, v_w_pb, v_w_out):
    l = x.shape[1]
    lc = ctx.shape[1]
    lt = l + lc
    nch = lt // CH
    me = 4 * lax.axis_index("x") + 2 * lax.axis_index("y") + lax.axis_index("c")
    wsh = w_in.shape[2]
    off_a = 3 * D
    n_ab = 4 * NH
    jb = off_a // wsh
    o1 = off_a - jb * wsh
    o2 = o1 + n_ab
    assert o2 <= wsh and NREST == (NDEV - jb) * wsh - o2
    split = jb + 1

    w_in_bf = w_in[0].astype(_BF)
    wg_lo, wg_mod, wg_conv, c_all = _exchange(
        [w_in_bf, w_mod[0].astype(_BF), w_conv[0], c], ["gather_lo", "gather", "gather", "gather"],
        "gather_first", split)
    w_qkv = jnp.concatenate([wg_lo[j][:, :wsh] for j in range(jb)] + [wg_lo[jb][:, :o1]], axis=1)
    w_ab = jnp.pad(wg_lo[jb][:, o1:o2], ((0, 0), (0, LANE - n_ab)))
    wconv_full = jnp.moveaxis(wg_conv, 0, 1).reshape(3, 3 * D)
    c_all = c_all.reshape(NDEV, D)

    cc = jnp.concatenate([c, c_ctx.reshape(1, D), jnp.zeros((6, D), F32)], axis=0)
    mods = _modulation(cc, wg_mod, b_mod)
    h, h_t = _prenorm(ctx[0], x[0], mods, g_pre)
    p_qkv = _matmul_nn(h, w_qkv, "in_proj_qkv")
    pab = _matmul_nn(h, w_ab, "in_proj_ab")
    abt = jnp.swapaxes(pab[:, :n_ab].reshape(nch, CH, n_ab), 1, 2)
    alog16, dtb16 = a_log.reshape(1, 2 * NH), dt_bias.reshape(1, 2 * NH)
    alog_r = jnp.pad(alog16, ((0, 0), (0, LANE - 2 * NH)))
    dtb_r = jnp.pad(dtb16, ((0, 0), (0, LANE - 2 * NH)))
    alog_c = jnp.pad(alog16.reshape(2 * NH, 1), ((0, 2 * NH), (0, 0)))
    dtb_c = jnp.pad(dtb16.reshape(2 * NH, 1), ((0, 2 * NH), (0, 0)))
    qkv = _qkv_fwd(p_qkv, wconv_full, lc)
    late = [w_in_bf, w_pa[0].astype(_BF), w_pb[0].astype(_BF), w_out[0].astype(_BF)]
    xc_late = _Exchange(zip(late, ["gather_hi", "gather", "gather", "gather"]), split)
    o_f, o_b, s_f, s_b, t_f, t_b, wg_hi, wg_pa, wg_pb, wg_out = _gdn_fwd(
        qkv, pab, abt, alog_r, dtb_r, alog_c, dtb_c, lc, xc_late, late)
    w_rest = jnp.concatenate([wg_lo[jb][:, o2:wsh]] + [wg_hi[j][:, :wsh] for j in range(split, NDEV)], axis=1)
    wf_pa, wf_pb, wf_out = wg_pa.reshape(D, D), wg_pb.reshape(D, D), wg_out.reshape(D, D)
    p_rest = _matmul_nn(h, w_rest, "in_proj_rest")

    w_spt = jnp.swapaxes(w_sp[0], 1, 2)
    b_spb = jnp.broadcast_to(b_sp[0][:, :, None], (NH, GC, GC))
    gate_x = mods[0:1, 2 * D:]
    dp_rest, do, dy, ya, yb, mg, d_a, d_b, dout, dwsp, dbsp_l, pvec = _post(
        p_rest, o_f, o_b, x[0], loss_target[0], wf_pa, wf_pb, wf_out, w_sp[0], w_spt, b_spb, gm_ln_g, gm_ln_b,
        g_onorm, g_post, gate_x, lc)

    dw_rest = _matmul_nn(h_t, dp_rest, "dw_in_rest", _BF)
    o3 = wsh - o2
    chunks_hi = jnp.moveaxis(dw_rest[:, o3:].reshape(D, NDEV - split, wsh), 1, 0)
    dw_pa = _matmul_nn(ya, d_a, "dw_pa", _BF).reshape(NDEV, D // NDEV, D)
    dw_pb = _matmul_nn(yb, d_b, "dw_pb", _BF).reshape(NDEV, D // NDEV, D)
    dw_out = _matmul_nn(mg, dout, "dw_out", _BF).reshape(NDEV, D // NDEV, D)
    small_a, lay_a = _pack([
        ("g_post", pvec[1]), ("g_onorm", pvec[4, :DH]), ("gm_ln_g", pvec[2]), ("gm_ln_b", pvec[3]), ("w_sp", dwsp),
        ("b_sp", jnp.sum(dbsp_l, axis=-1)), ("loss", pvec[5]), ("dgate", pvec[0])])
    (theirs_hi,) = _exchange([chunks_hi], ["sibling"], "pair_swap_hi")
    chip_hi = _pair_sum(chunks_hi, theirs_hi[0], "pair_sum_hi")
    early = [chip_hi, dw_pa, dw_pb, dw_out, small_a]
    xc_early = _Exchange(zip(early, ["scatter_par_hi", "scatter", "scatter", "scatter", "gather"]), split)

    dqkv_f, dqkv_b, dcol_f, dcol_b, drow_f, drow_b, gvec_c, gvec_r, r_in, r_pa, r_pb, r_out, small_a_all = _gdn_bwd(
        qkv, pab, abt, alog_r, dtb_r, alog_c, dtb_c, s_f, s_b, t_f, t_b, do, lc, xc_early, early)
    dp_qkv, dwconv = _qkv_bwd(p_qkv, wconv_full, dqkv_f, dqkv_b, lc)
    drow = jnp.swapaxes(drow_f + drow_b, 1, 2).reshape(lt, n_ab)
    dpab = (dcol_f + dcol_b + jnp.pad(drow, ((0, 0), (0, LANE - n_ab)))).astype(_BF)

    dw_qkv = _matmul_nn(h_t, dp_qkv, "dw_in_qkv", _BF)
    dw_ab = _matmul_nn(h_t, dpab, "dw_in_ab", _BF)
    dw_lo = jnp.concatenate([dw_qkv, dw_ab[:, :n_ab], dw_rest[:, :o3]], axis=1)
    chunks_lo = jnp.moveaxis(dw_lo.reshape(D, split, wsh), 1, 0)
    (theirs,) = _exchange([chunks_lo], ["sibling"], "pair_swap")
    chip_lo = _pair_sum(chunks_lo, theirs[0], "pair_sum")
    xc_last = _Exchange([(chip_lo, "scatter_par_lo")], split)
    dh, r_in = _dh_matmul(dp_rest, dp_qkv, dpab, w_rest, w_qkv, w_ab, xc_last, [chip_lo], {0: r_in})
    grad_x, nvec = _prenorm_bwd(ctx[0], x[0], dh, dy, mods, g_pre)

    dalog = gvec_c[0, :2 * NH] + gvec_r[:2 * NH, 0]
    ddtb = gvec_c[1, :2 * NH] + gvec_r[:2 * NH, 1]
    small_b, lay_b = _pack([
        ("g_pre", nvec[4]), ("a_log", dalog), ("dt_bias", ddtb), ("w_conv", dwconv),
        ("dshift", nvec[0]), ("dscale", nvec[1]), ("dshift_c", nvec[2]), ("dscale_c", nvec[3])])
    (small_b_all,) = _exchange([small_b], ["gather"], "gather_small")
    tot = _unpack(_sum_parts(small_a_all, "sum_small_a"), lay_a)
    tot.update(_unpack(_sum_parts(small_b_all, "sum_small_b"), lay_b))

    def per_device(packed_all, layout, name):
        at, r = [(a_, r_) for nm, a_, r_, _ in layout if nm == name][0]
        return packed_all[:, at:at + r].reshape(NDEV, -1)

    dmx_all = jnp.concatenate([per_device(small_b_all, lay_b, "dshift"), per_device(small_b_all, lay_b, "dscale"),
                               per_device(small_a_all, lay_a, "dgate")], axis=1)
    dmc_all = jnp.concatenate([per_device(small_b_all, lay_b, "dshift_c"), per_device(small_b_all, lay_b, "dscale_c"),
                               jnp.zeros((NDEV, D), F32)], axis=1)
    g_wmod, g_cctx, g_bmod = _mod_bwd(c_all, c_ctx.reshape(1, D), dmx_all, dmc_all, wg_mod)
    loss = 0.5 / D * jnp.sum(tot["loss"])
    ws_conv = w_conv.shape[2]
    g_wconv = lax.dynamic_slice_in_dim(tot["w_conv"], me * ws_conv, ws_conv, axis=1)

    small_names = ["c_ctx", "b_mod", "g_pre", "g_post", "a_log", "dt_bias", "g_onorm", "gm_ln_g", "gm_ln_b",
                   "w_sp", "b_sp", "w_conv"]
    wts = dict(c_ctx=c_ctx, b_mod=b_mod, g_pre=g_pre, g_post=g_post, a_log=a_log, dt_bias=dt_bias, g_onorm=g_onorm,
               gm_ln_g=gm_ln_g, gm_ln_b=gm_ln_b, w_sp=w_sp, b_sp=b_sp, w_conv=w_conv)
    ms = dict(c_ctx=m_c_ctx, b_mod=m_b_mod, g_pre=m_g_pre, g_post=m_g_post, a_log=m_a_log, dt_bias=m_dt_bias,
              g_onorm=m_g_onorm, gm_ln_g=m_gm_ln_g, gm_ln_b=m_gm_ln_b, w_sp=m_w_sp, b_sp=m_b_sp, w_conv=m_w_conv)
    vs = dict(c_ctx=v_c_ctx, b_mod=v_b_mod, g_pre=v_g_pre, g_post=v_g_post, a_log=v_a_log, dt_bias=v_dt_bias,
              g_onorm=v_g_onorm, gm_ln_g=v_gm_ln_g, gm_ln_b=v_gm_ln_b, w_sp=v_w_sp, b_sp=v_b_sp, w_conv=v_w_conv)
    gs = dict(tot)
    gs.update(c_ctx=g_cctx, b_mod=g_bmod, w_conv=g_wconv)
    gpk, play = _pack([(nm, gs[nm].reshape(wts[nm].shape)) for nm in small_names])
    wpk, _ = _pack([(nm, wts[nm]) for nm in small_names])
    mpk, _ = _pack([(nm, ms[nm]) for nm in small_names])
    vpk, _ = _pack([(nm, vs[nm]) for nm in small_names])
    res_small = [_unpack(a, play) for a in _adamw(gpk[None], wpk, mpk, vpk, "adamw_small")]
    res_big = {
        "w_mod": _adamw(g_wmod[None], w_mod[0], m_w_mod[0], v_w_mod[0], "adamw_w_mod"),
        "w_in": _adamw(r_in, w_in[0], m_w_in[0], v_w_in[0], "adamw_w_in", chip_sums_below=NDEV),
        "w_pa": _adamw(r_pa, w_pa[0], m_w_pa[0], v_w_pa[0], "adamw_w_pa"),
        "w_pb": _adamw(r_pb, w_pb[0], m_w_pb[0], v_w_pb[0], "adamw_w_pb"),
        "w_out": _adamw(r_out, w_out[0], m_w_out[0], v_w_out[0], "adamw_w_out"),
    }
    order = ["c_ctx", "w_mod", "b_mod", "g_pre", "g_post", "w_in", "w_conv", "a_log", "dt_bias", "g_onorm",
             "gm_ln_g", "gm_ln_b", "w_sp", "b_sp", "w_pa", "w_pb", "w_out"]
    outs = [loss, grad_x[None]]
    for k in range(4):
        for nm in order:
            if nm in res_big:
                outs.append(res_big[nm][k][None])
            else:
                outs.append(res_small[k][nm])
    return tuple(outs)
```

```python
import functools

import jax
import jax.numpy as jnp
from jax import lax
from jax.experimental import pallas as pl
from jax.experimental.pallas import tpu as pltpu

F32 = jnp.float32
_BF = jnp.bfloat16
_HI = lax.Precision.HIGHEST
D = 1024
NH = 8
DH = 128
CH = 64
SUB = 2
GC = 128
NREST = 6 * D
NMAIN = NREST + 3 * D
EPS = 1e-6
LANE = 128
NDEV = 8
VMEM_LIMIT = 56 * 1024 * 1024
MESH = pl.DeviceIdType.MESH

ADAM_LR, ADAM_B1, ADAM_B2, ADAM_EPS, ADAM_WD, ADAM_STEP = 0.001, 0.9, 0.999, 1e-08, 0.01, 10

NN = ((1,), (0,))
NT = ((1,), (1,))
TN = ((0,), (0,))


def _dot(a, b, dims=NN, prec=None):
    return lax.dot_general(a, b, (dims, ((), ())), precision=prec, preferred_element_type=F32)


def _mm(a, b, dims=NN):
    return _dot(a.astype(_BF), b.astype(_BF), dims)


def _mmh(a, b, dims=NN):
    return _dot(a.astype(F32), b.astype(F32), dims, _HI)


def _split(a):
    hi = a.astype(_BF)
    return hi, (a - hi.astype(F32)).astype(_BF)


def _mm3(a, b, dims=NN):
    ah, al = _split(a)
    bh, bl = _split(b)
    return _dot(ah, bh, dims) + (_dot(ah, bl, dims) + _dot(al, bh, dims))


def _sigmoid(x):
    return 1.0 / (1.0 + jnp.exp(-x))


def _silu_g(x):
    s = _sigmoid(x)
    return x * s, s * (1.0 + x * (1.0 - s))


def _gelu_g(x):
    c = 0.7978845608028654
    t = jnp.tanh(c * (x + 0.044715 * (x * x * x)))
    cdf = 0.5 * (1.0 + t)
    return x * cdf, cdf + 0.5 * x * (1.0 - t * t) * c * (1.0 + 3 * 0.044715 * x * x)


def _softplus(x):
    return jnp.maximum(x, 0.0) + jnp.log(1.0 + jnp.exp(-jnp.abs(x)))


def _params(sem=None):
    return pltpu.CompilerParams(dimension_semantics=sem, vmem_limit_bytes=VMEM_LIMIT)


def _tile(n, pref):
    for t in pref:
        if n % t == 0:
            return t
    return n


def _full(shape):
    nd = len(shape)
    return pl.BlockSpec(shape, lambda *_: (0,) * nd)


def _sds(shape, dt=F32):
    return jax.ShapeDtypeStruct(shape, dt)


MAX_PIECES = 12
PIECE_BYTES = 256 * 1024


def _piece_slices(shape, itemsize):
    total = itemsize
    for d in shape:
        total *= d
    want = min(MAX_PIECES, total // PIECE_BYTES)
    lead = shape[0] if len(shape) >= 3 else 1
    rows = shape[-2] if len(shape) >= 2 else 1
    if want < 2 or lead > want:
        return [()]
    m = max([n for n in (8, 4, 2, 1) if n * lead <= want and rows % (16 * n) == 0], default=1)
    if m * lead < 2:
        return [()]
    rs = rows // m
    mid = (slice(None),) * max(len(shape) - 3, 0)
    if len(shape) >= 3:
        return [(i,) + mid + (pl.ds(j * rs, rs),) for i in range(lead) for j in range(m)]
    return [(pl.ds(j * rs, rs),) for j in range(m)]


class _Pieces:
    def __init__(self, copies):
        self.copies = copies

    def start(self):
        for cp in self.copies:
            cp.start()

    def wait_send(self):
        for cp in self.copies:
            cp.wait_send()

    def wait_recv(self):
        for cp in self.copies:
            cp.wait_recv()

    def wait(self):
        for cp in self.copies:
            cp.wait()


class _Exchange:
    def __init__(self, specs, split):
        self.specs = list(specs)
        self.split = split
        self.n = len(self.specs)
        def out(a, k):
            if k == "sibling":
                return (1,) + tuple(a.shape)
            return (NDEV,) + (tuple(a.shape) if k.startswith("gather") else tuple(a.shape[1:]))

        self.out_shape = tuple(_sds(out(a, k), a.dtype) for a, k in self.specs)
        self.pieces = [_piece_slices(o.shape[1:], jnp.dtype(o.dtype).itemsize) for o in self.out_shape]
        self.sem_base = [(NDEV - 1) * sum(len(p) for p in self.pieces[:a]) for a in range(self.n + 1)]
        self.scratch = [pltpu.SemaphoreType.DMA((self.sem_base[-1],)), pltpu.SemaphoreType.DMA((self.sem_base[-1],)),
                        pltpu.SemaphoreType.DMA((self.sem_base[-1] // (NDEV - 1),))]

    def _local(self, sems, a, src, dst):
        base = self.sem_base[a] // (NDEV - 1)
        return _Pieces([pltpu.make_async_copy(src.at[sl] if sl else src, dst.at[sl] if sl else dst, sems[2].at[base + p])
                        for p, sl in enumerate(self.pieces[a])])

    def _remote(self, sems, a, k, src, dst, to):
        send_sems, recv_sems, _ = sems
        base = self.sem_base[a] + k * len(self.pieces[a])
        return _Pieces([
            pltpu.make_async_remote_copy(
                src_ref=src.at[sl] if sl else src, dst_ref=dst.at[sl] if sl else dst, send_sem=send_sems.at[base + p],
                recv_sem=recv_sems.at[base + p], device_id=to, device_id_type=MESH)
            for p, sl in enumerate(self.pieces[a])])

    def _ok(self, kind, idx):
        if kind.endswith("_lo"):
            return idx < self.split
        if kind.endswith("_hi"):
            return idx >= self.split
        return True

    def _phases(self, ins, outs, sems):
        x, y, c = lax.axis_index("x"), lax.axis_index("y"), lax.axis_index("c")
        me = 4 * x + 2 * y + c
        sib = (x, y, 1 - c)
        sib_idx = 4 * x + 2 * y + (1 - c)
        chips = [(1 - x, y), (x, 1 - y), (1 - x, 1 - y)]
        starts, forwards, waits = [], [], []
        for a, (_, kind) in enumerate(self.specs):
            ok = functools.partial(self._ok, kind)
            if kind.startswith("gather"):
                def copy(k, block, to, src=None, a=a):
                    rows = outs[a].at[block]
                    return self._remote(sems, a, k, rows if src is None else src, rows, to)

                loc = self._local(sems, a, ins[a], outs[a].at[me])
                first = [copy(0, me, sib, ins[a])] + [copy(1 + j, me, (*chip, c), ins[a]) for j, chip in enumerate(chips)]
                starts += [(ok(me), loc.start)] + [(ok(me), cp.start) for cp in first]
                waits += [(ok(me), loc.wait)] + [(ok(me), cp.wait_send) for cp in first]
                for j, chip in enumerate(chips):
                    origin = 4 * chip[0] + 2 * chip[1] + c
                    passed = copy(4 + j, origin, sib)
                    forwards += [(ok(origin), copy(1 + j, origin, sib).wait_recv), (ok(origin), passed.start)]
                    waits.append((ok(origin), passed.wait_send))
                    other = 4 * chip[0] + 2 * chip[1] + (1 - c)
                    waits.append((ok(other), copy(4 + j, other, sib).wait_recv))
                waits.append((ok(sib_idx), copy(0, sib_idx, sib).wait_recv))
            elif kind == "sibling":
                swap = self._remote(sems, a, 0, ins[a], outs[a].at[0], sib)
                starts.append((True, swap.start))
                waits += [(True, swap.wait_send), (True, swap.wait_recv)]
            else:
                base = self.split if kind.endswith("_hi") else 0
                same_core_only = "_par" in kind

                def src(idx, a=a, base=base):
                    return ins[a].at[jnp.clip(idx - base, 0, ins[a].shape[0] - 1)]

                loc = self._local(sems, a, src(me), outs[a].at[me])
                starts.append((ok(me), loc.start))
                waits.append((ok(me), loc.wait))
                for k in range(1, NDEV):
                    if same_core_only and k & 1:
                        continue
                    px = 1 - x if (k >> 2) & 1 else x
                    py = 1 - y if (k >> 1) & 1 else y
                    pc = 1 - c if k & 1 else c
                    pidx = 4 * px + 2 * py + pc
                    send = self._remote(sems, a, k - 1, src(pidx), outs[a].at[me], (px, py, pc))
                    arrive = self._remote(sems, a, k - 1, src(pidx), outs[a].at[pidx], (px, py, pc))
                    starts.append((ok(pidx), send.start))
                    waits += [(ok(pidx), send.wait_send), (ok(me), arrive.wait_recv)]
        return starts, forwards, waits

    @staticmethod
    def _run(actions):
        for cond, fn in actions:
            if cond is True:
                fn()
            else:
                pl.when(cond)(fn)

    def start(self, ins, outs, sems):
        self._run(self._phases(ins, outs, sems)[0])

    def forward(self, ins, outs, sems):
        self._run(self._phases(ins, outs, sems)[1])

    def wait(self, ins, outs, sems):
        self._run(self._phases(ins, outs, sems)[2])


_ANY = pl.BlockSpec(memory_space=pl.ANY)


def _exchange(arrays, kinds, name, split=0, into=None):
    xc = _Exchange(zip(arrays, kinds), split)
    n = xc.n
    into = into or {}
    ni = len(into)

    def body(*refs):
        ins, outs, sems = refs[:n], refs[n + ni:2 * n + ni], refs[2 * n + ni:]
        xc.start(ins, outs, sems)
        xc.forward(ins, outs, sems)
        xc.wait(ins, outs, sems)

    return pl.pallas_call(
        body, name=name, out_shape=xc.out_shape, in_specs=[_ANY] * (n + ni), out_specs=tuple([_ANY] * n),
        scratch_shapes=xc.scratch, input_output_aliases={n + t: a for t, a in enumerate(into)},
    )(*arrays, *into.values())


def _matmul_nn(a, b, name, out_dtype=F32):
    m, kk = a.shape
    n = b.shape[1]
    tm = m if m * kk * a.dtype.itemsize <= (12 << 20) else _tile(m, (1088, 1024, 640, 512, 256, 128))
    tn = _tile(n, (512, 256, 128))

    def body(a_ref, b_ref, o_ref):
        o_ref[...] = _mm(a_ref[...], b_ref[...]).astype(o_ref.dtype)

    return pl.pallas_call(
        body, name=name, out_shape=_sds((m, n), out_dtype), grid=(n // tn, m // tm),
        in_specs=[pl.BlockSpec((tm, kk), lambda j, i: (i, 0)), pl.BlockSpec((kk, tn), lambda j, i: (0, j))],
        out_specs=pl.BlockSpec((tm, tn), lambda j, i: (i, j)),
        compiler_params=_params(("parallel", "parallel")),
    )(a, b)


def _dh_matmul(dp_rest, dp_qkv, dpab, w_rest, w_qkv, w_ab, xc, xc_arrays, xc_into):
    lt = dp_rest.shape[0]
    tm = _tile(lt, (1088, 1024, 640, 512, 256, 128))
    nr, nq = dp_rest.shape[1] // D, dp_qkv.shape[1] // D
    nx, ni = xc.n, len(xc_into)
    ni_steps = lt // tm

    def body(*refs):
        dr_ref, dq_ref, ab_ref, wr_ref, wq_ref, wab_ref = refs[:6]
        x_in = refs[6:6 + nx]
        o_ref = refs[6 + nx + ni]
        x_out = refs[7 + nx + ni:7 + 2 * nx + ni]
        sems = refs[7 + 2 * nx + ni:]
        i = pl.program_id(0)
        k = pl.program_id(1)

        @pl.when((i == 0) & (k == 0))
        def _():
            xc.start(x_in, x_out, sems)

        @pl.when(k == 0)
        def _():
            o_ref[...] = _mm(ab_ref[...], wab_ref[...], NT)

        @pl.when(k < nr)
        def _():
            o_ref[...] += _mm(dr_ref[...], wr_ref[...], NT)

        @pl.when(k >= nr)
        def _():
            o_ref[...] += _mm(dq_ref[...], wq_ref[...], NT)

        @pl.when((i == ni_steps - 1) & (k == nr + nq - 1))
        def _():
            xc.wait(x_in, x_out, sems)

    rk = lambda k: jnp.minimum(k, nr - 1)
    qk = lambda k: jnp.maximum(k - nr, 0)
    return pl.pallas_call(
        body, name="dh_matmul", out_shape=(_sds((lt, D)),) + xc.out_shape, grid=(lt // tm, nr + nq),
        in_specs=[pl.BlockSpec((tm, D), lambda i, k: (i, rk(k))), pl.BlockSpec((tm, D), lambda i, k: (i, qk(k))),
                  pl.BlockSpec((tm, LANE), lambda i, k: (i, 0)),
                  pl.BlockSpec((D, D), lambda i, k: (0, rk(k))), pl.BlockSpec((D, D), lambda i, k: (0, qk(k))),
                  _full((D, LANE))] + [_ANY] * (nx + ni),
        out_specs=(pl.BlockSpec((tm, D), lambda i, k: (i, 0)),) + tuple([_ANY] * nx),
        scratch_shapes=xc.scratch, input_output_aliases={6 + nx + t: 1 + a for t, a in enumerate(xc_into)},
        compiler_params=_params(("arbitrary", "arbitrary")),
    )(dp_rest, dp_qkv, dpab, w_rest, w_qkv, w_ab, *xc_arrays, *xc_into.values())


def _modulation(cc, w_mod_g, b_mod):
    ws = w_mod_g.shape[2]

    def body(c_ref, w_ref, b_ref, o_ref):
        s, _ = _silu_g(c_ref[...])
        o_ref[...] = _mm(s, w_ref[0]) + b_ref[...]

    return pl.pallas_call(
        body, name="modulation", out_shape=_sds((8, 3 * D)), grid=(NDEV,),
        in_specs=[_full((8, D)), pl.BlockSpec((1, D, ws), lambda j: (j, 0, 0)), pl.BlockSpec((1, ws), lambda j: (0, j))],
        out_specs=pl.BlockSpec((8, ws), lambda j: (0, j)),
        compiler_params=_params(("parallel",)),
    )(cc, w_mod_g, b_mod)


def _prenorm(ctx, x, mods, g_pre):
    lc = ctx.shape[0]
    lt = lc + x.shape[0]
    tm = _tile(lc, (256, 128))
    nct = lc // tm

    def body(c_ref, x_ref, m_ref, g_ref, o_ref, ot_ref):
        is_ctx = pl.program_id(0) < nct
        x = jnp.where(is_ctx, c_ref[...], x_ref[...])
        shift = jnp.where(is_ctx, m_ref[1:2, 0:D], m_ref[0:1, 0:D])
        scale = jnp.where(is_ctx, m_ref[1:2, D:2 * D], m_ref[0:1, D:2 * D])
        r = lax.rsqrt(jnp.mean(x * x, axis=-1, keepdims=True) + EPS)
        h = ((x * r * g_ref[...]) * (1.0 + scale) + shift).astype(o_ref.dtype)
        o_ref[...] = h
        ot_ref[...] = h.T

    return pl.pallas_call(
        body, name="prenorm", out_shape=(_sds((lt, D), _BF), _sds((D, lt), _BF)), grid=(lt // tm,),
        in_specs=[pl.BlockSpec((tm, D), lambda i: (jnp.minimum(i, nct - 1), 0)),
                  pl.BlockSpec((tm, D), lambda i: (jnp.maximum(i - nct, 0), 0)), _full((8, 3 * D)), _full((1, D))],
        out_specs=(pl.BlockSpec((tm, D), lambda i: (i, 0)), pl.BlockSpec((D, tm), lambda i: (0, i))),
        compiler_params=_params(("parallel",)),
    )(ctx, x, mods, g_pre)


def _prenorm_bwd(ctx, x, dh, dy, mods, g_pre):
    lc = ctx.shape[0]
    lt = lc + x.shape[0]
    tm = _tile(lc, (256, 128))
    nct = lc // tm
    nl = (lt - lc) // tm

    def body(c_ref, x_ref, dh_ref, dy_ref, m_ref, g_ref, gx_ref, vec_ref):
        i = pl.program_id(0)

        @pl.when(i == 0)
        def _():
            vec_ref[...] = jnp.zeros_like(vec_ref)

        is_ctx = i < nct
        x = jnp.where(is_ctx, c_ref[...], x_ref[...])
        dh = dh_ref[...]
        g = g_ref[...]
        scale = jnp.where(is_ctx, m_ref[1:2, D:2 * D], m_ref[0:1, D:2 * D])
        r = lax.rsqrt(jnp.mean(x * x, axis=-1, keepdims=True) + EPS)
        n = x * r
        hn = n * g
        dsh = jnp.sum(dh, axis=0, keepdims=True)
        dsc = jnp.sum(dh * hn, axis=0, keepdims=True)
        dhn = dh * (1.0 + scale)
        vec_ref[4:5, :] += jnp.sum(dhn * n, axis=0, keepdims=True)
        dn = dhn * g
        dx = r * (dn - n * jnp.mean(dn * n, axis=-1, keepdims=True))

        @pl.when(is_ctx)
        def _():
            vec_ref[2:3, :] += dsh
            vec_ref[3:4, :] += dsc

        @pl.when(jnp.logical_not(is_ctx))
        def _():
            vec_ref[0:1, :] += dsh
            vec_ref[1:2, :] += dsc
            gx_ref[...] = dy_ref[...] + dx

    xrow = lambda i: (jnp.maximum(i - nct, 0), 0)
    return pl.pallas_call(
        body, name="prenorm_bwd", out_shape=(_sds((nl * tm, D)), _sds((8, D))), grid=(lt // tm,),
        in_specs=[pl.BlockSpec((tm, D), lambda i: (jnp.minimum(i, nct - 1), 0)), pl.BlockSpec((tm, D), xrow),
                  pl.BlockSpec((tm, D), lambda i: (i, 0)), pl.BlockSpec((tm, D), xrow), _full((8, 3 * D)), _full((1, D))],
        out_specs=(pl.BlockSpec((tm, D), xrow), _full((8, D))),
        compiler_params=_params(("arbitrary",)),
    )(ctx, x, dh, dy, mods, g_pre)


def _conv_parts(x, w, lc):
    lt = x.shape[0]
    row = lax.broadcasted_iota(jnp.int32, x.shape, 0)
    first = (row == 0) | (row == lc)
    last = (row == lc - 1) | (row == lt - 1)
    xp = jnp.where(first, 0.0, pltpu.roll(x, 1, 0))
    xn = jnp.where(last, 0.0, pltpu.roll(x, lt - 1, 0))
    y = w[0:1, :] * xp + w[1:2, :] * x + w[2:3, :] * xn
    return xp, xn, y, first, last


def _qkv_fwd(p, w_conv, lc):
    lt = p.shape[0]

    def body(p_ref, w_ref, o_ref):
        _, _, y, _, _ = _conv_parts(p_ref[...], w_ref[...], lc)
        s, _ = _silu_g(y)
        rs = lax.rsqrt(jnp.sum(s * s, axis=-1, keepdims=True) + EPS)
        o_ref[...] = s * jnp.where(pl.program_id(0) < 2 * NH, rs, 1.0)

    return pl.pallas_call(
        body, name="qkv_fwd", out_shape=_sds((lt, 3 * D)), grid=(3 * NH,),
        in_specs=[pl.BlockSpec((lt, DH), lambda j: (0, j)), pl.BlockSpec((3, DH), lambda j: (0, j))],
        out_specs=pl.BlockSpec((lt, DH), lambda j: (0, j)),
        compiler_params=_params(("parallel",)),
    )(p, w_conv)


def _qkv_bwd(p, w_conv, dqkv_f, dqkv_b, lc):
    lt = p.shape[0]

    def body(p_ref, w_ref, df_ref, db_ref, dp_ref, dw_ref):
        w = w_ref[...]
        xp, xn, y, first, last = _conv_parts(p_ref[...], w, lc)
        s, ds_dy = _silu_g(y)
        dn = df_ref[...] + db_ref[...]
        rs = lax.rsqrt(jnp.sum(s * s, axis=-1, keepdims=True) + EPS)
        nrm = s * rs
        ds_n = rs * (dn - nrm * jnp.sum(dn * nrm, axis=-1, keepdims=True))
        ds = jnp.where(pl.program_id(0) < 2 * NH, ds_n, dn)
        dy = ds * ds_dy
        dw_ref[0:1, :] = jnp.sum(dy * xp, axis=0, keepdims=True)
        dw_ref[1:2, :] = jnp.sum(dy * p_ref[...], axis=0, keepdims=True)
        dw_ref[2:3, :] = jnp.sum(dy * xn, axis=0, keepdims=True)
        dyn = jnp.where(last, 0.0, pltpu.roll(dy, lt - 1, 0))
        dyp = jnp.where(first, 0.0, pltpu.roll(dy, 1, 0))
        dp_ref[...] = (w[1:2, :] * dy + w[0:1, :] * dyn + w[2:3, :] * dyp).astype(dp_ref.dtype)

    return pl.pallas_call(
        body, name="qkv_bwd", out_shape=(_sds((lt, 3 * D), _BF), _sds((3, 3 * D))), grid=(3 * NH,),
        in_specs=[pl.BlockSpec((lt, DH), lambda j: (0, j)), pl.BlockSpec((3, DH), lambda j: (0, j)),
                  pl.BlockSpec((lt, DH), lambda j: (0, j)), pl.BlockSpec((lt, DH), lambda j: (0, j))],
        out_specs=(pl.BlockSpec((lt, DH), lambda j: (0, j)), pl.BlockSpec((3, DH), lambda j: (0, j))),
        compiler_params=_params(("parallel",)),
    )(p, w_conv, dqkv_f, dqkv_b)


def _masks(d):
    ri = lax.broadcasted_iota(jnp.int32, (CH, CH), 0)
    ci = lax.broadcasted_iota(jnp.int32, (CH, CH), 1)
    incl = (ri >= ci) if d == 0 else (ri <= ci)
    strict = (ri > ci) if d == 0 else (ri < ci)
    incl_t = (ri <= ci) if d == 0 else (ri >= ci)
    return incl, strict, incl_t, ri == ci


def _decays(d, ab, abt, alog_r, dtb_r, alog_c, dtb_c, incl, incl_t):
    g_full = -jnp.exp(alog_r) * _softplus(ab + dtb_r)
    beta_full = _sigmoid(ab)
    gc_full = _mmh(incl.astype(F32), g_full)
    gl_full = jnp.sum(g_full, axis=0, keepdims=True)
    gt_full = -jnp.exp(alog_c) * _softplus(abt + dtb_c)
    gct = _mmh(gt_full, incl_t.astype(F32))
    return g_full, beta_full, gc_full, gl_full, gt_full, gct


def _lane_onehot(idx, n=LANE):
    return (lax.broadcasted_iota(jnp.int32, (1, n), 1) == idx).astype(F32)


def _head_scalars(d, h, beta_full, gc_full, gl_full, gct):
    idx = d * NH + h
    oh = _lane_onehot(idx)
    gcol = jnp.sum(gc_full * oh, axis=-1, keepdims=True)
    bcol = jnp.sum(beta_full * _lane_onehot(2 * NH + idx), axis=-1, keepdims=True)
    gl = jnp.sum(gl_full * oh, axis=-1, keepdims=True)
    grow = gct[idx:idx + 1, :]
    return gcol, grow, bcol, gl


def _lockstep(gens):
    live = list(gens)
    while live:
        nxt = []
        for g in live:
            try:
                next(g)
                nxt.append(g)
            except StopIteration:
                pass
        live = nxt


def _chunk_local(qh, kh, vh, gcol, grow, bcol, gl, incl, strict):
    decay = jnp.where(incl, jnp.exp(gcol - grow), 0.0)
    kb = kh * bcol
    a = jnp.where(strict, _mm(kb, kh, NT) * decay, 0.0)
    egc = jnp.exp(gcol)
    rhs_u = vh * bcol
    rhs_w = kb * egc
    qs = qh * (DH ** -0.5)
    attn = jnp.where(incl, _mm(qs, kh, NT) * decay, 0.0)
    etail = jnp.exp(gl - gcol)
    return decay, kb, a, egc, rhs_u, rhs_w, qs, attn, etail


def _scan_specs(lt, lc, bwd_pass):
    assert lt % (SUB * CH) == 0 and lc % (SUB * CH) == 0
    nch = lt // (SUB * CH)
    ncc = lc // (SUB * CH)
    if not bwd_pass:
        cf = lambda s: s
        cb = lambda s: jnp.where(s < ncc, ncc - 1 - s, nch + ncc - 1 - s)
    else:
        cf = lambda s: nch - 1 - s
        cb = lambda s: jnp.where(s < nch - ncc, ncc + s, s - (nch - ncc))
    return nch, cf, cb


def _gdn_fwd(qkv, pab, abt, alog_r, dtb_r, alog_c, dtb_c, lc, xc, xc_arrays):
    lt = qkv.shape[0]
    nch, cf, cb = _scan_specs(lt, lc, False)
    nx = xc.n

    def body(*refs):
        qf, kf, vf, abf, abtf, qb, kb_, vb, abb, abtb, ar, dr, ac, dc = refs[:14]
        x_in = refs[14:14 + nx]
        of_ref, ob_ref, sf_ref, sb_ref, tf_ref, tb_ref = refs[14 + nx:20 + nx]
        x_out = refs[20 + nx:20 + 2 * nx]
        s_scr = refs[20 + 2 * nx]
        sems = refs[21 + 2 * nx:]

        @pl.when(pl.program_id(0) == 0)
        def _():
            s_scr[...] = jnp.zeros_like(s_scr)
            xc.start(x_in, x_out, sems)

        def chain(d, h, c, late, q_r, k_r, v_r, o_ref, sh_ref, th_ref, masks, decs):
            incl, strict, _, eye = masks
            sl = slice(h * DH, (h + 1) * DH)
            rows = slice(c * CH, (c + 1) * CH)
            qh, kh, vh = q_r[rows, sl], k_r[rows, sl], v_r[rows, sl]
            gcol, grow, bcol, gl = _head_scalars(d, h, *decs)
            _, _, a, egc, rhs_u, rhs_w, qs, attn, etail = _chunk_local(qh, kh, vh, gcol, grow, bcol, gl, incl, strict)
            yield
            n = -a
            t = jnp.where(eye, 1.0, 0.0) + n
            p = _mm3(n, n)
            yield
            for _ in range(4):
                r = _mm3(jnp.concatenate([t, p], axis=0), p)
                yield
                t = t + r[:CH]
                p = r[CH:]
            t = t + _mm3(t, p)
            yield
            for _ in range(3 if late else 0):
                yield
            sol = _mm3(t, jnp.concatenate([rhs_u, rhs_w], axis=1))
            u, w = sol[:, :DH], sol[:, DH:]
            s = s_scr[d, h]
            sh_ref[c, h] = s
            th_ref[c, h] = t
            yield
            ws = _mm(jnp.concatenate([w, qs * egc], axis=0), s)
            yield
            v_new = u - ws[:CH]
            o_ref[rows, sl] = ws[CH:] + _mm(attn, v_new)
            s_scr[d, h] = s * jnp.exp(gl) + _mm(kh * etail, v_new, TN)

        chains = []
        for d, (q_r, k_r, v_r, ab_r, abt_r, o_ref, sh_ref, th_ref) in enumerate(
                ((qf, kf, vf, abf, abtf, of_ref, sf_ref, tf_ref), (qb, kb_, vb, abb, abtb, ob_ref, sb_ref, tb_ref))):
            masks = _masks(d)
            for pos, c in enumerate(range(SUB) if d == 0 else reversed(range(SUB))):
                _, beta_full, gc_full, gl_full, _, gct = _decays(
                    d, ab_r[c * CH:(c + 1) * CH, :], abt_r[c], ar[...], dr[...], ac[...], dc[...], masks[0], masks[2])
                for h in range(NH):
                    chains.append(chain(d, h, c, pos > 0, q_r, k_r, v_r, o_ref, sh_ref, th_ref, masks,
                                        (beta_full, gc_full, gl_full, gct)))
        _lockstep(chains)

        @pl.when(pl.program_id(0) == nch // 2)
        def _():
            xc.forward(x_in, x_out, sems)

        @pl.when(pl.program_id(0) == nch - 1)
        def _():
            xc.wait(x_in, x_out, sems)

    def row(c, col):
        return pl.BlockSpec((SUB * CH, D), lambda s: (c(s), col))

    def chunk_in(c):
        return [row(c, 0), row(c, 1), row(c, 2), pl.BlockSpec((SUB * CH, LANE), lambda s: (c(s), 0)),
                pl.BlockSpec((SUB, 4 * NH, CH), lambda s: (c(s), 0, 0))]

    def hist(c, n):
        return pl.BlockSpec((SUB, NH, n, n), lambda s: (c(s), 0, 0, 0))

    small = [_full((1, LANE)), _full((1, LANE)), _full((4 * NH, 1)), _full((4 * NH, 1))]
    return pl.pallas_call(
        body, name="gdn_fwd", grid=(nch,),
        out_shape=(_sds((lt, D)), _sds((lt, D)), _sds((lt // CH, NH, DH, DH)), _sds((lt // CH, NH, DH, DH)),
                   _sds((lt // CH, NH, CH, CH)), _sds((lt // CH, NH, CH, CH))) + xc.out_shape,
        in_specs=chunk_in(cf) + chunk_in(cb) + small + [_ANY] * nx,
        out_specs=(pl.BlockSpec((SUB * CH, D), lambda s: (cf(s), 0)), pl.BlockSpec((SUB * CH, D), lambda s: (cb(s), 0)),
                   hist(cf, DH), hist(cb, DH), hist(cf, CH), hist(cb, CH)) + tuple([_ANY] * nx),
        scratch_shapes=[pltpu.VMEM((2, NH, DH, DH), F32)] + xc.scratch,
        compiler_params=_params(("arbitrary",)),
    )(qkv, qkv, qkv, pab, abt, qkv, qkv, qkv, pab, abt, alog_r, dtb_r, alog_c, dtb_c, *xc_arrays)


def _gdn_bwd(qkv, pab, abt, alog_r, dtb_r, alog_c, dtb_c, s_f, s_b, t_f, t_b, do, lc, xc, xc_arrays):
    lt = qkv.shape[0]
    nch, cf, cb = _scan_specs(lt, lc, True)
    nx = xc.n

    def body(*refs):
        qf, kf, vf, abf, abtf, sf_ref, tf_ref, dof, qb, kb_, vb, abb, abtb, sb_ref, tb_ref, dob, ar, dr, ac, dc = refs[:20]
        x_in = refs[20:20 + nx]
        dqf_ref, dqb_ref, dcf_ref, dcb_ref, drf_ref, drb_ref, vcol_ref, vrow_ref = refs[20 + nx:28 + nx]
        x_out = refs[28 + nx:28 + 2 * nx]
        ds_scr = refs[28 + 2 * nx]
        sems = refs[29 + 2 * nx:]

        @pl.when(pl.program_id(0) == 0)
        def _():
            ds_scr[...] = jnp.zeros_like(ds_scr)
            vcol_ref[...] = jnp.zeros_like(vcol_ref)
            vrow_ref[...] = jnp.zeros_like(vrow_ref)
            xc.start(x_in, x_out, sems)

        alog_r_, dtb_r_, alog_c_, dtb_c_ = ar[...], dr[...], ac[...], dc[...]
        lane2 = lax.broadcasted_iota(jnp.int32, (1, LANE), 1)
        acc = {}

        def chain(d, h, c, late, q_r, k_r, v_r, sh_ref, th_ref, do_r, dq_ref, masks, decs):
            incl, strict, _, _ = masks
            idx = d * NH + h
            sl = slice(h * DH, (h + 1) * DH)
            rows = slice(c * CH, (c + 1) * CH)
            qh, kh, vh = q_r[rows, sl], k_r[rows, sl], v_r[rows, sl]
            doh = do_r[rows, sl]
            gcol, grow, bcol, gl = _head_scalars(d, h, *decs)
            decay, kb, a, egc, rhs_u, rhs_w, qs, attn, etail = _chunk_local(qh, kh, vh, gcol, grow, bcol, gl, incl, strict)
            t = th_ref[c, h]
            s = sh_ref[c, h]
            sol = _mm3(t, jnp.concatenate([rhs_u, rhs_w], axis=1))
            u, w = sol[:, :DH], sol[:, DH:]
            q_dec = qs * egc
            k_tail = kh * etail
            egl = jnp.exp(gl)
            dq_dec = _mm(doh, s, NT)
            yield
            for _ in range(2 if late else 0):
                yield
            ds_new = ds_scr[d, h]
            dv_new = _mm(attn, doh, TN) + _mm(k_tail, ds_new)
            dgl = jnp.sum(jnp.sum(ds_new * s, axis=0, keepdims=True), axis=-1, keepdims=True) * egl
            yield
            v_new = u - _mm(w, s)
            dw = -_mm(dv_new, s, NT)
            ds_scr[d, h] = ds_new * egl + _mm(q_dec, doh, TN) - _mm(w, dv_new, TN)
            yield
            dattn = jnp.where(incl, _mm(doh, v_new, NT), 0.0)
            dk_tail = _mm(v_new, ds_new, NT)
            dr = _mm3(t, jnp.concatenate([dv_new, dw], axis=1), TN)
            dr_u, dr_w = dr[:, :DH], dr[:, DH:]
            yield
            da = -jnp.where(strict, _mm3(dr, sol, NT), 0.0)
            nq = dattn * decay
            dqs = _mm(nq, kh) + dq_dec * egc
            dk = _mm(nq, qs, TN)
            yield
            dv = dr_u * bcol
            dbeta = jnp.sum(dr_u * vh, axis=-1, keepdims=True)
            dgc = jnp.sum(dr_w * rhs_w, axis=-1, keepdims=True)
            m = da * decay
            dkb = dr_w * egc + _mm(m, kh)
            dk = dk + _mm(m, kb, TN)
            pq = da * a + dattn * attn
            dgc = dgc + jnp.sum(pq, axis=-1, keepdims=True) + jnp.sum(dq_dec * q_dec, axis=-1, keepdims=True)
            dgr = -jnp.sum(pq, axis=0, keepdims=True)
            tt = jnp.sum(dk_tail * k_tail, axis=-1, keepdims=True)
            dk = dk + dk_tail * etail + dkb * bcol
            dgc = dgc - tt
            dgl = dgl + jnp.sum(tt, axis=0, keepdims=True)
            dbeta = dbeta + jnp.sum(dkb * kh, axis=-1, keepdims=True)
            dq_ref[rows, sl] = dqs * (DH ** -0.5)
            dq_ref[rows, D + h * DH:D + (h + 1) * DH] = dk
            dq_ref[rows, 2 * D + h * DH:2 * D + (h + 1) * DH] = dv
            acc.setdefault((d, c), []).append((idx, dgc, dgl, dbeta, dgr))

        dirs = ((qf, kf, vf, abf, abtf, sf_ref, tf_ref, dof, dqf_ref, dcf_ref, drf_ref),
                (qb, kb_, vb, abb, abtb, sb_ref, tb_ref, dob, dqb_ref, dcb_ref, drb_ref))
        chains, ctx_d = [], {}
        for d, (q_r, k_r, v_r, ab_r, abt_r, sh_ref, th_ref, do_r, dq_ref, _, _) in enumerate(dirs):
            masks = _masks(d)
            for pos, c in enumerate(reversed(range(SUB)) if d == 0 else range(SUB)):
                ab, abt = ab_r[c * CH:(c + 1) * CH, :], abt_r[c]
                g_full, beta_full, gc_full, gl_full, gt_full, gct = _decays(
                    d, ab, abt, alog_r_, dtb_r_, alog_c_, dtb_c_, masks[0], masks[2])
                ctx_d[(d, c)] = (masks, ab, abt, g_full, beta_full, gt_full)
                for h in range(NH):
                    chains.append(chain(d, h, c, pos > 0, q_r, k_r, v_r, sh_ref, th_ref, do_r, dq_ref, masks,
                                        (beta_full, gc_full, gl_full, gct)))
        _lockstep(chains)
        for d, c in sorted(ctx_d):
            (incl, _, incl_t, _), ab, abt, g_full, beta_full, gt_full = ctx_d[(d, c)]
            dcol_ref, drow_ref = dirs[d][9], dirs[d][10]
            dgc_col = jnp.zeros((CH, LANE), F32)
            dgl_row = jnp.zeros((1, LANE), F32)
            dbeta_col = jnp.zeros((CH, LANE), F32)
            dgc_row = jnp.zeros((4 * NH, CH), F32)
            for idx, dgc, dgl, dbeta, dgr in acc[(d, c)]:
                oh = _lane_onehot(idx)
                dgc_col = dgc_col + dgc * oh
                dgl_row = dgl_row + dgl * oh
                dbeta_col = dbeta_col + dbeta * _lane_onehot(2 * NH + idx)
                ohc = (lax.broadcasted_iota(jnp.int32, (4 * NH, 1), 0) == idx).astype(F32)
                dgc_row = dgc_row + ohc * dgr
            dg_col = _mmh(incl_t.astype(F32), dgc_col) + dgl_row
            dg_row = _mmh(dgc_row, incl.astype(F32))
            sg_col = _sigmoid(ab + dtb_r_)
            da_col = dg_col * (-jnp.exp(alog_r_)) * sg_col
            dcol_ref[c * CH:(c + 1) * CH, :] = da_col + dbeta_col * beta_full * (1.0 - beta_full)
            da_row = dg_row * (-jnp.exp(alog_c_)) * _sigmoid(abt + dtb_c_)
            drow_ref[c] = da_row
            vcol_ref[0:1, :] += jnp.sum(dg_col * g_full, axis=0, keepdims=True)
            vcol_ref[1:2, :] += jnp.sum(da_col, axis=0, keepdims=True)
            rl = jnp.sum(dg_row * gt_full, axis=-1, keepdims=True)
            rd = jnp.sum(da_row, axis=-1, keepdims=True)
            vrow_ref[...] += jnp.where(lane2 == 0, rl, 0.0) + jnp.where(lane2 == 1, rd, 0.0)

        @pl.when(pl.program_id(0) == nch // 2)
        def _():
            xc.forward(x_in, x_out, sems)

        @pl.when(pl.program_id(0) == nch - 1)
        def _():
            xc.wait(x_in, x_out, sems)

    def row(c, col):
        return pl.BlockSpec((SUB * CH, D), lambda s: (c(s), col))

    def hist(c, n):
        return pl.BlockSpec((SUB, NH, n, n), lambda s: (c(s), 0, 0, 0))

    def chunk_in(c):
        return [row(c, 0), row(c, 1), row(c, 2), pl.BlockSpec((SUB * CH, LANE), lambda s: (c(s), 0)),
                pl.BlockSpec((SUB, 4 * NH, CH), lambda s: (c(s), 0, 0)), hist(c, DH), hist(c, CH), row(c, 0)]

    small = [_full((1, LANE)), _full((1, LANE)), _full((4 * NH, 1)), _full((4 * NH, 1))]
    return pl.pallas_call(
        body, name="gdn_bwd", grid=(nch,),
        out_shape=(_sds((lt, 3 * D)), _sds((lt, 3 * D)), _sds((lt, LANE)), _sds((lt, LANE)),
                   _sds((lt // CH, 4 * NH, CH)), _sds((lt // CH, 4 * NH, CH)), _sds((8, LANE)), _sds((4 * NH, LANE))) + xc.out_shape,
        in_specs=chunk_in(cf) + chunk_in(cb) + small + [_ANY] * nx,
        out_specs=(pl.BlockSpec((SUB * CH, 3 * D), lambda s: (cf(s), 0)), pl.BlockSpec((SUB * CH, 3 * D), lambda s: (cb(s), 0)),
                   pl.BlockSpec((SUB * CH, LANE), lambda s: (cf(s), 0)), pl.BlockSpec((SUB * CH, LANE), lambda s: (cb(s), 0)),
                   pl.BlockSpec((SUB, 4 * NH, CH), lambda s: (cf(s), 0, 0)), pl.BlockSpec((SUB, 4 * NH, CH), lambda s: (cb(s), 0, 0)),
                   _full((8, LANE)), _full((4 * NH, LANE))) + tuple([_ANY] * nx),
        scratch_shapes=[pltpu.VMEM((2, NH, DH, DH), F32)] + xc.scratch,
        compiler_params=_params(("arbitrary",)),
    )(qkv, qkv, qkv, pab, abt, s_f, t_f, do, qkv, qkv, qkv, pab, abt, s_b, t_b, do, alog_r, dtb_r, alog_c, dtb_c,
      *xc_arrays)


def _post(p, o_f, o_b, x, tgt, w_pa, w_pb, w_out, w_sp, w_spt, b_spb, ln_g, ln_b, g_on, g_post, gate_x, lc):
    lt = p.shape[0]
    l = x.shape[0]
    tm = GC
    nct = lc // tm

    def body(p_ref, of_ref, ob_ref, x_ref, t_ref, wpa, wpb, wout, wsp, wspt, bspb, lng_ref, lnb_ref, gon_ref, gpost_ref, gate_ref,
             dp_ref, do_ref, dy_ref, ya_ref, yb_ref, mg_ref, da_ref, db_ref, dout_ref, dwsp_ref, dbsp_ref, vec_ref):
        i = pl.program_id(0)

        @pl.when(i == 0)
        def _():
            dwsp_ref[...] = jnp.zeros_like(dwsp_ref)
            dbsp_ref[...] = jnp.zeros_like(dbsp_ref)
            vec_ref[...] = jnp.zeros_like(vec_ref)

        @pl.when(i < nct)
        def _():
            dp_ref[...] = jnp.zeros_like(dp_ref)
            do_ref[...] = jnp.zeros_like(do_ref)

        @pl.when(i >= nct)
        def _():
            lng, lnb, gon, gpost, gate = lng_ref[...], lnb_ref[...], gon_ref[...], gpost_ref[...], gate_ref[...]
            zb, ua, va, za, ga, gb = [p_ref[:, j * D:(j + 1) * D] for j in range(6)]
            o = of_ref[...] + ob_ref[...]
            szb, dszb = _silu_g(zb)
            nh_l, r_l = [], []
            for h in range(NH):
                oh = o[:, h * DH:(h + 1) * DH]
                r = lax.rsqrt(jnp.mean(oh * oh, axis=-1, keepdims=True) + EPS)
                nh_l.append(oh * r)
                r_l.append(r)
            nrm_b = jnp.concatenate(nh_l, axis=-1)
            gon_t = jnp.concatenate([gon] * NH, axis=-1)
            y_b = nrm_b * gon_t * szb
            u, du_dua = _gelu_g(ua)
            gv, dgv_dva = _gelu_g(va)
            xc = gv - jnp.mean(gv, axis=-1, keepdims=True)
            rs_ln = lax.rsqrt(jnp.mean(xc * xc, axis=-1, keepdims=True) + EPS)
            vhat = xc * rs_ln
            v = vhat * lng + lnb
            s_sp = jnp.concatenate(
                [_mm(wsp[g], v[:, g * DH:(g + 1) * DH]) + bspb[g] for g in range(NH)], axis=-1)
            sza, dsza = _silu_g(za)
            y_a = u * s_sp * sza
            a_pr = _mm(y_a, wpa[...])
            b_pr = _mm(y_b, wpb[...])
            sga = _sigmoid(ga)
            sgb = _sigmoid(gb)
            merged = sga * a_pr + sgb * b_pr
            out = _mm(merged, wout[...])
            rs_o = lax.rsqrt(jnp.mean(out * out, axis=-1, keepdims=True) + EPS)
            n_o = out * rs_o
            rr = n_o * gpost
            diff = x_ref[...] + gate * rr - t_ref[...]
            vec_ref[5:6, :] += jnp.sum(diff * diff, axis=0, keepdims=True)
            dy = diff * (1.0 / D)
            dy_ref[...] = dy
            vec_ref[0:1, :] += jnp.sum(dy * rr, axis=0, keepdims=True)
            dr = dy * gate
            vec_ref[1:2, :] += jnp.sum(dr * n_o, axis=0, keepdims=True)
            dn_o = dr * gpost
            dout = rs_o * (dn_o - n_o * jnp.mean(dn_o * n_o, axis=-1, keepdims=True))
            dmerged = _mm(dout, wout[...], NT)
            d_a = dmerged * sga
            d_b = dmerged * sgb
            dga = dmerged * a_pr * sga * (1.0 - sga)
            dgb = dmerged * b_pr * sgb * (1.0 - sgb)
            dy_a = _mm(d_a, wpa[...], NT)
            dy_b = _mm(d_b, wpb[...], NT)
            ya_ref[...] = y_a.astype(ya_ref.dtype).T
            yb_ref[...] = y_b.astype(yb_ref.dtype).T
            mg_ref[...] = merged.astype(mg_ref.dtype).T
            da_ref[...] = d_a.astype(da_ref.dtype)
            db_ref[...] = d_b.astype(db_ref.dtype)
            dout_ref[...] = dout.astype(dout_ref.dtype)
            dua = dy_a * s_sp * sza * du_dua
            ds_sp = dy_a * u * sza
            dza = dy_a * u * s_sp * dsza
            dv_l = []
            for g in range(NH):
                ds_g = ds_sp[:, g * DH:(g + 1) * DH]
                dv_l.append(_mm(wspt[g], ds_g))
                dwsp_ref[g] += _mm(ds_g, v[:, g * DH:(g + 1) * DH], NT)
                dbsp_ref[g] += ds_g
            dv = jnp.concatenate(dv_l, axis=-1)
            vec_ref[2:3, :] += jnp.sum(dv * vhat, axis=0, keepdims=True)
            vec_ref[3:4, :] += jnp.sum(dv, axis=0, keepdims=True)
            dvh = dv * lng
            dgv = rs_ln * (dvh - jnp.mean(dvh, axis=-1, keepdims=True) - vhat * jnp.mean(dvh * vhat, axis=-1, keepdims=True))
            dva = dgv * dgv_dva
            dzb = dy_b * nrm_b * gon_t * dszb
            dgon_full = jnp.sum(dy_b * nrm_b * szb, axis=0, keepdims=True)
            dgon = dgon_full[:, 0:DH]
            for h in range(1, NH):
                dgon = dgon + dgon_full[:, h * DH:(h + 1) * DH]
            vec_ref[4:5, 0:DH] += dgon
            dnb = dy_b * gon_t * szb
            do_l = []
            for h in range(NH):
                sl = slice(h * DH, (h + 1) * DH)
                dn_h = dnb[:, sl]
                do_l.append(r_l[h] * (dn_h - nh_l[h] * jnp.mean(dn_h * nh_l[h], axis=-1, keepdims=True)))
            do_ref[...] = jnp.concatenate(do_l, axis=-1)
            for j, val in enumerate((dzb, dua, dva, dza, dga, dgb)):
                dp_ref[:, j * D:(j + 1) * D] = val.astype(dp_ref.dtype)

    xrow = lambda i: (jnp.maximum(i - nct, 0), 0)
    wspec = _full((D, D))
    gspec = _full((NH, GC, GC))
    vspec = _full((1, D))
    bf_out = _sds((l, D), _BF)
    bf_out_t = _sds((D, l), _BF)
    xcol = lambda i: (0, jnp.maximum(i - nct, 0))
    return pl.pallas_call(
        body, name="post", grid=(lt // tm,),
        out_shape=(_sds((lt, NREST), _BF), _sds((lt, D)), _sds((l, D)), bf_out_t, bf_out_t, bf_out_t, bf_out, bf_out, bf_out,
                   _sds((NH, GC, GC)), _sds((NH, GC, GC)), _sds((8, D))),
        in_specs=[pl.BlockSpec((tm, NREST), lambda i: (i, 0)), pl.BlockSpec((tm, D), lambda i: (i, 0)),
                  pl.BlockSpec((tm, D), lambda i: (i, 0)), pl.BlockSpec((tm, D), xrow), pl.BlockSpec((tm, D), xrow),
                  wspec, wspec, wspec, gspec, gspec, gspec, vspec, vspec, _full((1, DH)), vspec, vspec],
        out_specs=(pl.BlockSpec((tm, NREST), lambda i: (i, 0)), pl.BlockSpec((tm, D), lambda i: (i, 0)),
                   pl.BlockSpec((tm, D), xrow), pl.BlockSpec((D, tm), xcol), pl.BlockSpec((D, tm), xcol),
                   pl.BlockSpec((D, tm), xcol), pl.BlockSpec((tm, D), xrow), pl.BlockSpec((tm, D), xrow),
                   pl.BlockSpec((tm, D), xrow), gspec, gspec, _full((8, D))),
        compiler_params=_params(("arbitrary",)),
    )(p, o_f, o_b, x, tgt, w_pa, w_pb, w_out, w_sp, w_spt, b_spb, ln_g, ln_b, g_on, g_post, gate_x)


def _sum_parts(parts, name):
    r = parts.shape[1]
    tr = r if NDEV * r * LANE * 4 <= (8 << 20) else _tile(r, (512, 256, 128, 64, 32, 16, 8))

    def body(p_ref, o_ref):
        acc = p_ref[0]
        for s in range(1, NDEV):
            acc = acc + p_ref[s]
        o_ref[...] = acc

    return pl.pallas_call(
        body, name=name, out_shape=_sds((r, LANE)), grid=(r // tr,),
        in_specs=[pl.BlockSpec((NDEV, tr, LANE), lambda i: (0, i, 0))],
        out_specs=pl.BlockSpec((tr, LANE), lambda i: (i, 0)),
        compiler_params=_params(("parallel",)),
    )(parts)


def _mod_bwd(c_all, c_ctx, dmx, dmc, w_mod_g):
    ws = w_mod_g.shape[2]

    def body(ca_ref, cc_ref, dsh_ref, dmx_ref, dmc_ref, dmc_sh_ref, w_ref, gw_ref, gc_ref, gb_ref):
        sc, _ = _silu_g(ca_ref[...])
        scc, dscc = _silu_g(cc_ref[...])
        dmc_tot = jnp.sum(dmc_ref[...], axis=0, keepdims=True)
        gb_ref[...] = jnp.sum(dmx_ref[...], axis=0, keepdims=True) + dmc_tot
        lhs = jnp.concatenate([sc, jnp.broadcast_to(scc, (8, D))], axis=0)
        rhs = jnp.concatenate([dsh_ref[...], dmc_sh_ref[...]], axis=0)
        gw_ref[...] = _mmh(lhs, rhs, TN)
        acc = jnp.zeros((8, D), F32)
        tot8 = jnp.broadcast_to(dmc_tot, (8, 3 * D))
        for j in range(NDEV):
            acc = acc + _mm(tot8[:, j * ws:(j + 1) * ws], w_ref[j], NT)
        gc_ref[...] = acc[0:1, :] * dscc

    return pl.pallas_call(
        body, name="mod_bwd", out_shape=(_sds((D, ws)), _sds((1, D)), _sds((1, 3 * D))),
        compiler_params=_params(),
    )(c_all, c_ctx, _my_cols(dmx, ws), dmx, dmc, _my_cols(dmc, ws), w_mod_g)


def _my_cols(a, ws):
    me = 4 * lax.axis_index("x") + 2 * lax.axis_index("y") + lax.axis_index("c")
    return lax.dynamic_slice_in_dim(a, me * ws, ws, axis=1)


def _pair_sum(mine, other, name):
    n, r, c = mine.shape
    tr = _tile(r, (256, 128, 64, 32, 16, 8))

    def body(a_ref, b_ref, o_ref):
        o_ref[...] = (a_ref[...].astype(F32) + b_ref[...].astype(F32)).astype(o_ref.dtype)

    blk = pl.BlockSpec((1, tr, c), lambda j, i: (j, i, 0))
    return pl.pallas_call(
        body, name=name, out_shape=_sds((n, r, c), mine.dtype), grid=(n, r // tr),
        in_specs=[blk, blk], out_specs=blk, compiler_params=_params(("parallel", "parallel")),
    )(mine, other)


def _adamw(parts, w, m, v, name, chip_sums_below=None):
    s_, r, c = parts.shape
    tr = _tile(r, (128, 64, 32, 16, 8)) if r * c * 4 > (1 << 20) else r
    c1 = 1.0 / (1.0 - ADAM_B1 ** ADAM_STEP)
    c2 = 1.0 / (1.0 - ADAM_B2 ** ADAM_STEP)

    def body(p_ref, w_ref, m_ref, v_ref, g_ref, d_ref, nm_ref, nv_ref):
        if chip_sums_below is None:
            part = lambda s: p_ref[s].astype(F32)
        else:
            core = lax.axis_index("c")
            me = 4 * lax.axis_index("x") + 2 * lax.axis_index("y") + core
            every = me >= chip_sums_below
            part = lambda s: jnp.where(every | (core == s % 2), p_ref[s].astype(F32), 0.0)
        g = part(0)
        for s in range(1, s_):
            g = g + part(s)
        m_new = ADAM_B1 * m_ref[...] + (1.0 - ADAM_B1) * g
        v_new = ADAM_B2 * v_ref[...] + (1.0 - ADAM_B2) * (g * g)
        g_ref[...] = g
        nm_ref[...] = m_new
        nv_ref[...] = v_new
        d_ref[...] = -ADAM_LR * ((m_new * c1) / (jnp.sqrt(v_new * c2) + ADAM_EPS) + ADAM_WD * w_ref[...])

    blk = pl.BlockSpec((tr, c), lambda i: (i, 0))
    o = _sds((r, c))
    return pl.pallas_call(
        body, name=name, out_shape=(o, o, o, o), grid=(r // tr,),
        in_specs=[pl.BlockSpec((s_, tr, c), lambda i: (0, i, 0)), blk, blk, blk],
        out_specs=(blk, blk, blk, blk),
        compiler_params=_params(("parallel",)),
    )(parts, w, m, v)


def _adamw_many(gs, ws, ms, vs, name):
    n = len(gs)
    c1 = 1.0 / (1.0 - ADAM_B1 ** ADAM_STEP)
    c2 = 1.0 / (1.0 - ADAM_B2 ** ADAM_STEP)

    def body(*refs):
        g_in, w_in_, m_in, v_in = (refs[k * n:(k + 1) * n] for k in range(4))
        g_out, d_out, m_out, v_out = (refs[(4 + k) * n:(5 + k) * n] for k in range(4))
        for p in range(n):
            g = g_in[p][...]
            m_new = ADAM_B1 * m_in[p][...] + (1.0 - ADAM_B1) * g
            v_new = ADAM_B2 * v_in[p][...] + (1.0 - ADAM_B2) * (g * g)
            g_out[p][...] = g
            m_out[p][...] = m_new
            v_out[p][...] = v_new
            d_out[p][...] = -ADAM_LR * ((m_new * c1) / (jnp.sqrt(v_new * c2) + ADAM_EPS) + ADAM_WD * w_in_[p][...])

    shapes = tuple(_sds(g.shape) for g in gs)
    res = pl.pallas_call(body, name=name, out_shape=shapes * 4, compiler_params=_params())(*gs, *ws, *ms, *vs)
    return [res[k * n:(k + 1) * n] for k in range(4)]


def _rows(a):
    flat = a.reshape(-1)
    n = flat.shape[0]
    r = -(-n // (8 * LANE)) * 8
    return jnp.pad(flat, (0, r * LANE - n)).reshape(r, LANE)


def _pack(items):
    parts, layout, at = [], [], 0
    for name, a in items:
        rws = _rows(a.astype(F32))
        layout.append((name, at, rws.shape[0], a.shape))
        parts.append(rws)
        at += rws.shape[0]
    return jnp.concatenate(parts, axis=0), layout


def _unpack(packed, layout):
    out = {}
    for name, at, r, shape in layout:
        n = 1
        for s in shape:
            n *= s
        out[name] = packed[at:at + r].reshape(-1)[:n].reshape(shape)
    return out


def kernel(x, c, ctx, c_ctx, w_mod, b_mod, g_pre, g_post, w_in, w_conv, a_log, dt_bias, g_onorm, gm_ln_g, gm_ln_b, w_sp, b_sp, w_pa, w_pb, w_out, loss_target, m_c_ctx, m_w_mod, m_b_mod, m_g_pre, m_g_post, m_w_in, m_w_conv, m_a_log, m_dt_bias, m_g_onorm, m_gm_ln_g, m_gm_ln_b, m_w_sp, m_b_sp, m_w_pa, m_w_pb, m_w_out, v_c_ctx, v_w_mod, v_b_mod, v_g_pre, v_g_post, v_w_in, v_w_conv, v_a_log, v_dt_bias, v_g_onorm, v_gm_ln_g, v_gm_ln_b, v_w_sp, v_b_sp, v_w_pa, v_w_pb, v_w_out):
    l = x.shape[1]
    lc = ctx.shape[1]
    lt = l + lc
    nch = lt // CH
    me = 4 * lax.axis_index("x") + 2 * lax.axis_index("y") + lax.axis_index("c")
    wsh = w_in.shape[2]
    off_a = 3 * D
    n_ab = 4 * NH
    jb = off_a // wsh
    o1 = off_a - jb * wsh
    o2 = o1 + n_ab
    assert o2 <= wsh and NREST == (NDEV - jb) * wsh - o2
    split = jb + 1

    w_in_bf = w_in[0].astype(_BF)
    wg_lo, wg_mod, wg_conv, c_all = _exchange(
        [w_in_bf, w_mod[0].astype(_BF), w_conv[0], c], ["gather_lo", "gather", "gather", "gather"],
        "gather_first", split)
    w_qkv = jnp.concatenate([wg_lo[j][:, :wsh] for j in range(jb)] + [wg_lo[jb][:, :o1]], axis=1)
    w_ab = jnp.pad(wg_lo[jb][:, o1:o2], ((0, 0), (0, LANE - n_ab)))
    wconv_full = jnp.moveaxis(wg_conv, 0, 1).reshape(3, 3 * D)
    c_all = c_all.reshape(NDEV, D)

    cc = jnp.concatenate([c, c_ctx.reshape(1, D), jnp.zeros((6, D), F32)], axis=0)
    mods = _modulation(cc, wg_mod, b_mod)
    h, h_t = _prenorm(ctx[0], x[0], mods, g_pre)
    p_qkv = _matmul_nn(h, w_qkv, "in_proj_qkv")
    pab = _matmul_nn(h, w_ab, "in_proj_ab")
    abt = jnp.swapaxes(pab[:, :n_ab].reshape(nch, CH, n_ab), 1, 2)
    alog16, dtb16 = a_log.reshape(1, 2 * NH), dt_bias.reshape(1, 2 * NH)
    alog_r = jnp.pad(alog16, ((0, 0), (0, LANE - 2 * NH)))
    dtb_r = jnp.pad(dtb16, ((0, 0), (0, LANE - 2 * NH)))
    alog_c = jnp.pad(alog16.reshape(2 * NH, 1), ((0, 2 * NH), (0, 0)))
    dtb_c = jnp.pad(dtb16.reshape(2 * NH, 1), ((0, 2 * NH), (0, 0)))
    qkv = _qkv_fwd(p_qkv, wconv_full, lc)
    late = [w_in_bf, w_pa[0].astype(_BF), w_pb[0].astype(_BF), w_out[0].astype(_BF)]
    xc_late = _Exchange(zip(late, ["gather_hi", "gather", "gather", "gather"]), split)
    o_f, o_b, s_f, s_b, t_f, t_b, wg_hi, wg_pa, wg_pb, wg_out = _gdn_fwd(
        qkv, pab, abt, alog_r, dtb_r, alog_c, dtb_c, lc, xc_late, late)
    w_rest = jnp.concatenate([wg_lo[jb][:, o2:wsh]] + [wg_hi[j][:, :wsh] for j in range(split, NDEV)], axis=1)
    wf_pa, wf_pb, wf_out = wg_pa.reshape(D, D), wg_pb.reshape(D, D), wg_out.reshape(D, D)
    p_rest = _matmul_nn(h, w_rest, "in_proj_rest")

    w_spt = jnp.swapaxes(w_sp[0], 1, 2)
    b_spb = jnp.broadcast_to(b_sp[0][:, :, None], (NH, GC, GC))
    gate_x = mods[0:1, 2 * D:]
    dp_rest, do, dy, ya, yb, mg, d_a, d_b, dout, dwsp, dbsp_l, pvec = _post(
        p_rest, o_f, o_b, x[0], loss_target[0], wf_pa, wf_pb, wf_out, w_sp[0], w_spt, b_spb, gm_ln_g, gm_ln_b,
        g_onorm, g_post, gate_x, lc)

    dw_rest = _matmul_nn(h_t, dp_rest, "dw_in_rest", _BF)
    o3 = wsh - o2
    chunks_hi = jnp.moveaxis(dw_rest[:, o3:].reshape(D, NDEV - split, wsh), 1, 0)
    dw_pa = _matmul_nn(ya, d_a, "dw_pa", _BF).reshape(NDEV, D // NDEV, D)
    dw_pb = _matmul_nn(yb, d_b, "dw_pb", _BF).reshape(NDEV, D // NDEV, D)
    dw_out = _matmul_nn(mg, dout, "dw_out", _BF).reshape(NDEV, D // NDEV, D)
    small_a, lay_a = _pack([
        ("g_post", pvec[1]), ("g_onorm", pvec[4, :DH]), ("gm_ln_g", pvec[2]), ("gm_ln_b", pvec[3]), ("w_sp", dwsp),
        ("b_sp", jnp.sum(dbsp_l, axis=-1)), ("loss", pvec[5]), ("dgate", pvec[0])])
    (theirs_hi,) = _exchange([chunks_hi], ["sibling"], "pair_swap_hi")
    chip_hi = _pair_sum(chunks_hi, theirs_hi[0], "pair_sum_hi")
    early = [chip_hi, dw_pa, dw_pb, dw_out, small_a]
    xc_early = _Exchange(zip(early, ["scatter_par_hi", "scatter", "scatter", "scatter", "gather"]), split)

    dqkv_f, dqkv_b, dcol_f, dcol_b, drow_f, drow_b, gvec_c, gvec_r, r_in, r_pa, r_pb, r_out, small_a_all = _gdn_bwd(
        qkv, pab, abt, alog_r, dtb_r, alog_c, dtb_c, s_f, s_b, t_f, t_b, do, lc, xc_early, early)
    dp_qkv, dwconv = _qkv_bwd(p_qkv, wconv_full, dqkv_f, dqkv_b, lc)
    drow = jnp.swapaxes(drow_f + drow_b, 1, 2).reshape(lt, n_ab)
    dpab = (dcol_f + dcol_b + jnp.pad(drow, ((0, 0), (0, LANE - n_ab)))).astype(_BF)

    dw_qkv = _matmul_nn(h_t, dp_qkv, "dw_in_qkv", _BF)
    dw_ab = _matmul_nn(h_t, dpab, "dw_in_ab", _BF)
    dw_lo = jnp.concatenate([dw_qkv, dw_ab[:, :n_ab], dw_rest[:, :o3]], axis=1)
    chunks_lo = jnp.moveaxis(dw_lo.reshape(D, split, wsh), 1, 0)
    (theirs,) = _exchange([chunks_lo], ["sibling"], "pair_swap")
    chip_lo = _pair_sum(chunks_lo, theirs[0], "pair_sum")
    xc_last = _Exchange([(chip_lo, "scatter_par_lo")], split)
    dh, r_in = _dh_matmul(dp_rest, dp_qkv, dpab, w_rest, w_qkv, w_ab, xc_last, [chip_lo], {0: r_in})
    grad_x, nvec = _prenorm_bwd(ctx[0], x[0], dh, dy, mods, g_pre)

    dalog = gvec_c[0, :2 * NH] + gvec_r[:2 * NH, 0]
    ddtb = gvec_c[1, :2 * NH] + gvec_r[:2 * NH, 1]
    small_b, lay_b = _pack([
        ("g_pre", nvec[4]), ("a_log", dalog), ("dt_bias", ddtb), ("w_conv", dwconv),
        ("dshift", nvec[0]), ("dscale", nvec[1]), ("dshift_c", nvec[2]), ("dscale_c", nvec[3])])
    (small_b_all,) = _exchange([small_b], ["gather"], "gather_small")
    tot = _unpack(_sum_parts(small_a_all, "sum_small_a"), lay_a)
    tot.update(_unpack(_sum_parts(small_b_all, "sum_small_b"), lay_b))

    def per_device(packed_all, layout, name):
        at, r = [(a_, r_) for nm, a_, r_, _ in layout if nm == name][0]
        return packed_all[:, at:at + r].reshape(NDEV, -1)

    dmx_all = jnp.concatenate([per_device(small_b_all, lay_b, "dshift"), per_device(small_b_all, lay_b, "dscale"),
                               per_device(small_a_all, lay_a, "dgate")], axis=1)
    dmc_all = jnp.concatenate([per_device(small_b_all, lay_b, "dshift_c"), per_device(small_b_all, lay_b, "dscale_c"),
                               jnp.zeros((NDEV, D), F32)], axis=1)
    g_wmod, g_cctx, g_bmod = _mod_bwd(c_all, c_ctx.reshape(1, D), dmx_all, dmc_all, wg_mod)
    loss = 0.5 / D * jnp.sum(tot["loss"])
    ws_conv = w_conv.shape[2]
    g_wconv = lax.dynamic_slice_in_dim(tot["w_conv"], me * ws_conv, ws_conv, axis=1)

    small_names = ["c_ctx", "b_mod", "g_pre", "g_post", "a_log", "dt_bias", "g_onorm", "gm_ln_g", "gm_ln_b",
                   "w_sp", "b_sp", "w_conv"]
    wts = dict(c_ctx=c_ctx, b_mod=b_mod, g_pre=g_pre, g_post=g_post, a_log=a_log, dt_bias=dt_bias, g_onorm=g_onorm,
               gm_ln_g=gm_ln_g, gm_ln_b=gm_ln_b, w_sp=w_sp, b_sp=b_sp, w_conv=w_conv)
    ms = dict(c_ctx=m_c_ctx, b_mod=m_b_mod, g_pre=m_g_pre, g_post=m_g_post, a_log=m_a_log, dt_bias=m_dt_bias,
              g_onorm=m_g_onorm, gm_ln_g=m_gm_ln_g, gm_ln_b=m_gm_ln_b, w_sp=m_w_sp, b_sp=m_b_sp, w_conv=m_w_conv)
    vs = dict(c_ctx=v_c_ctx, b_mod=v_b_mod, g_pre=v_g_pre, g_post=v_g_post, a_log=v_a_log, dt_bias=v_dt_bias,
              g_onorm=v_g_onorm, gm_ln_g=v_gm_ln_g, gm_ln_b=v_gm_ln_b, w_sp=v_w_sp, b_sp=v_b_sp, w_conv=v_w_conv)
    gs = dict(tot)
    gs.update(c_ctx=g_cctx, b_mod=g_bmod, w_conv=g_wconv)
    flat = lambda a: a.reshape(-1, a.shape[-1])
    res_small = [
        {nm: a.reshape(wts[nm].shape) for nm, a in zip(small_names, arrays)}
        for arrays in _adamw_many([flat(gs[nm].reshape(wts[nm].shape)) for nm in small_names],
                                  [flat(wts[nm]) for nm in small_names], [flat(ms[nm]) for nm in small_names],
                                  [flat(vs[nm]) for nm in small_names], "adamw_small")]
    res_big = {
        "w_mod": _adamw(g_wmod[None], w_mod[0], m_w_mod[0], v_w_mod[0], "adamw_w_mod"),
        "w_in": _adamw(r_in, w_in[0], m_w_in[0], v_w_in[0], "adamw_w_in", chip_sums_below=NDEV),
        "w_pa": _adamw(r_pa, w_pa[0], m_w_pa[0], v_w_pa[0], "adamw_w_pa"),
        "w_pb": _adamw(r_pb, w_pb[0], m_w_pb[0], v_w_pb[0], "adamw_w_pb"),
        "w_out": _adamw(r_out, w_out[0], m_w_out[0], v_w_out[0], "adamw_w_out"),
    }
    order = ["c_ctx", "w_mod", "b_mod", "g_pre", "g_post", "w_in", "w_conv", "a_log", "dt_bias", "g_onorm",
             "gm_ln_g", "gm_ln_b", "w_sp", "b_sp", "w_pa", "w_pb", "w_out"]
    outs = [loss, grad_x[None]]
    for k in range(4):
        for nm in order:
            if nm in res_big:
                outs.append(res_big[nm][k][None])
            else:
                outs.append(res_small[k][nm])
    return tuple(outs)
```

```python
import functools

import jax
import jax.numpy as jnp
from jax import lax
from jax.experimental import pallas as pl
from jax.experimental.pallas import tpu as pltpu

F32 = jnp.float32
_BF = jnp.bfloat16
_HI = lax.Precision.HIGHEST
D = 1024
NH = 8
DH = 128
CH = 64
SUB = 2
GC = 128
NREST = 6 * D
NMAIN = NREST + 3 * D
EPS = 1e-6
LANE = 128
NDEV = 8
VMEM_LIMIT = 56 * 1024 * 1024
MESH = pl.DeviceIdType.MESH

ADAM_LR, ADAM_B1, ADAM_B2, ADAM_EPS, ADAM_WD, ADAM_STEP = 0.001, 0.9, 0.999, 1e-08, 0.01, 10

NN = ((1,), (0,))
NT = ((1,), (1,))
TN = ((0,), (0,))


def _dot(a, b, dims=NN, prec=None):
    return lax.dot_general(a, b, (dims, ((), ())), precision=prec, preferred_element_type=F32)


def _mm(a, b, dims=NN):
    return _dot(a.astype(_BF), b.astype(_BF), dims)


def _mmh(a, b, dims=NN):
    return _dot(a.astype(F32), b.astype(F32), dims, _HI)


def _split(a):
    hi = a.astype(_BF)
    return hi, (a - hi.astype(F32)).astype(_BF)


def _mm3(a, b, dims=NN):
    ah, al = _split(a)
    bh, bl = _split(b)
    return _dot(ah, bh, dims) + (_dot(ah, bl, dims) + _dot(al, bh, dims))


def _sigmoid(x):
    return 1.0 / (1.0 + jnp.exp(-x))


def _silu_g(x):
    s = _sigmoid(x)
    return x * s, s * (1.0 + x * (1.0 - s))


def _gelu_g(x):
    c = 0.7978845608028654
    t = jnp.tanh(c * (x + 0.044715 * (x * x * x)))
    cdf = 0.5 * (1.0 + t)
    return x * cdf, cdf + 0.5 * x * (1.0 - t * t) * c * (1.0 + 3 * 0.044715 * x * x)


def _softplus(x):
    return jnp.maximum(x, 0.0) + jnp.log(1.0 + jnp.exp(-jnp.abs(x)))


def _params(sem=None):
    return pltpu.CompilerParams(dimension_semantics=sem, vmem_limit_bytes=VMEM_LIMIT)


def _tile(n, pref):
    for t in pref:
        if n % t == 0:
            return t
    return n


def _full(shape):
    nd = len(shape)
    return pl.BlockSpec(shape, lambda *_: (0,) * nd)


def _sds(shape, dt=F32):
    return jax.ShapeDtypeStruct(shape, dt)


MAX_PIECES = 12
PIECE_BYTES = 256 * 1024


def _piece_slices(shape, itemsize):
    total = itemsize
    for d in shape:
        total *= d
    want = min(MAX_PIECES, total // PIECE_BYTES)
    lead = shape[0] if len(shape) >= 3 else 1
    rows = shape[-2] if len(shape) >= 2 else 1
    if want < 2 or lead > want:
        return [()]
    m = max([n for n in (8, 4, 2, 1) if n * lead <= want and rows % (16 * n) == 0], default=1)
    if m * lead < 2:
        return [()]
    rs = rows // m
    mid = (slice(None),) * max(len(shape) - 3, 0)
    if len(shape) >= 3:
        return [(i,) + mid + (pl.ds(j * rs, rs),) for i in range(lead) for j in range(m)]
    return [(pl.ds(j * rs, rs),) for j in range(m)]


class _Pieces:
    def __init__(self, copies):
        self.copies = copies

    def start(self):
        for cp in self.copies:
            cp.start()

    def wait_send(self):
        for cp in self.copies:
            cp.wait_send()

    def wait_recv(self):
        for cp in self.copies:
            cp.wait_recv()

    def wait(self):
        for cp in self.copies:
            cp.wait()


class _Exchange:
    def __init__(self, specs, split):
        self.specs = list(specs)
        self.split = split
        self.n = len(self.specs)
        def out(a, k):
            if k == "sibling":
                return (1,) + tuple(a.shape)
            return (NDEV,) + (tuple(a.shape) if k.startswith("gather") else tuple(a.shape[1:]))

        self.out_shape = tuple(_sds(out(a, k), a.dtype) for a, k in self.specs)
        self.pieces = [_piece_slices(o.shape[1:], jnp.dtype(o.dtype).itemsize) for o in self.out_shape]
        self.sem_base = [(NDEV - 1) * sum(len(p) for p in self.pieces[:a]) for a in range(self.n + 1)]
        self.scratch = [pltpu.SemaphoreType.DMA((self.sem_base[-1],)), pltpu.SemaphoreType.DMA((self.sem_base[-1],)),
                        pltpu.SemaphoreType.DMA((self.sem_base[-1] // (NDEV - 1),))]

    def _local(self, sems, a, src, dst):
        base = self.sem_base[a] // (NDEV - 1)
        return _Pieces([pltpu.make_async_copy(src.at[sl] if sl else src, dst.at[sl] if sl else dst, sems[2].at[base + p])
                        for p, sl in enumerate(self.pieces[a])])

    def _remote(self, sems, a, k, src, dst, to):
        send_sems, recv_sems, _ = sems
        base = self.sem_base[a] + k * len(self.pieces[a])
        return _Pieces([
            pltpu.make_async_remote_copy(
                src_ref=src.at[sl] if sl else src, dst_ref=dst.at[sl] if sl else dst, send_sem=send_sems.at[base + p],
                recv_sem=recv_sems.at[base + p], device_id=to, device_id_type=MESH)
            for p, sl in enumerate(self.pieces[a])])

    def _ok(self, kind, idx):
        if kind.endswith("_lo"):
            return idx < self.split
        if kind.endswith("_hi"):
            return idx >= self.split
        return True

    def _phases(self, ins, outs, sems):
        x, y, c = lax.axis_index("x"), lax.axis_index("y"), lax.axis_index("c")
        me = 4 * x + 2 * y + c
        sib = (x, y, 1 - c)
        sib_idx = 4 * x + 2 * y + (1 - c)
        chips = [(1 - x, y), (x, 1 - y), (1 - x, 1 - y)]
        starts, forwards, waits = [], [], []
        for a, (_, kind) in enumerate(self.specs):
            ok = functools.partial(self._ok, kind)
            if kind.startswith("gather"):
                def copy(k, block, to, src=None, a=a):
                    rows = outs[a].at[block]
                    return self._remote(sems, a, k, rows if src is None else src, rows, to)

                loc = self._local(sems, a, ins[a], outs[a].at[me])
                first = [copy(0, me, sib, ins[a])] + [copy(1 + j, me, (*chip, c), ins[a]) for j, chip in enumerate(chips)]
                starts += [(ok(me), loc.start)] + [(ok(me), cp.start) for cp in first]
                waits += [(ok(me), loc.wait)] + [(ok(me), cp.wait_send) for cp in first]
                for j, chip in enumerate(chips):
                    origin = 4 * chip[0] + 2 * chip[1] + c
                    passed = copy(4 + j, origin, sib)
                    forwards += [(ok(origin), copy(1 + j, origin, sib).wait_recv), (ok(origin), passed.start)]
                    waits.append((ok(origin), passed.wait_send))
                    other = 4 * chip[0] + 2 * chip[1] + (1 - c)
                    waits.append((ok(other), copy(4 + j, other, sib).wait_recv))
                waits.append((ok(sib_idx), copy(0, sib_idx, sib).wait_recv))
            elif kind == "sibling":
                swap = self._remote(sems, a, 0, ins[a], outs[a].at[0], sib)
                starts.append((True, swap.start))
                waits += [(True, swap.wait_send), (True, swap.wait_recv)]
            else:
                base = self.split if kind.endswith("_hi") else 0
                same_core_only = "_par" in kind

                def src(idx, a=a, base=base):
                    return ins[a].at[jnp.clip(idx - base, 0, ins[a].shape[0] - 1)]

                loc = self._local(sems, a, src(me), outs[a].at[me])
                starts.append((ok(me), loc.start))
                waits.append((ok(me), loc.wait))
                for k in range(1, NDEV):
                    if same_core_only and k & 1:
                        continue
                    px = 1 - x if (k >> 2) & 1 else x
                    py = 1 - y if (k >> 1) & 1 else y
                    pc = 1 - c if k & 1 else c
                    pidx = 4 * px + 2 * py + pc
                    send = self._remote(sems, a, k - 1, src(pidx), outs[a].at[me], (px, py, pc))
                    arrive = self._remote(sems, a, k - 1, src(pidx), outs[a].at[pidx], (px, py, pc))
                    starts.append((ok(pidx), send.start))
                    waits += [(ok(pidx), send.wait_send), (ok(me), arrive.wait_recv)]
        return starts, forwards, waits

    @staticmethod
    def _run(actions):
        for cond, fn in actions:
            if cond is True:
                fn()
            else:
                pl.when(cond)(fn)

    def start(self, ins, outs, sems):
        self._run(self._phases(ins, outs, sems)[0])

    def forward(self, ins, outs, sems):
        self._run(self._phases(ins, outs, sems)[1])

    def wait(self, ins, outs, sems):
        self._run(self._phases(ins, outs, sems)[2])


_ANY = pl.BlockSpec(memory_space=pl.ANY)


def _exchange(arrays, kinds, name, split=0, into=None):
    xc = _Exchange(zip(arrays, kinds), split)
    n = xc.n
    into = into or {}
    ni = len(into)

    def body(*refs):
        ins, outs, sems = refs[:n], refs[n + ni:2 * n + ni], refs[2 * n + ni:]
        xc.start(ins, outs, sems)
        xc.forward(ins, outs, sems)
        xc.wait(ins, outs, sems)

    return pl.pallas_call(
        body, name=name, out_shape=xc.out_shape, in_specs=[_ANY] * (n + ni), out_specs=tuple([_ANY] * n),
        scratch_shapes=xc.scratch, input_output_aliases={n + t: a for t, a in enumerate(into)},
    )(*arrays, *into.values())


def _matmul_nn(a, b, name, out_dtype=F32):
    m, kk = a.shape
    n = b.shape[1]
    tm = m if m * kk * a.dtype.itemsize <= (12 << 20) else _tile(m, (1088, 1024, 640, 512, 256, 128))
    tn = _tile(n, (512, 256, 128))

    def body(a_ref, b_ref, o_ref):
        o_ref[...] = _mm(a_ref[...], b_ref[...]).astype(o_ref.dtype)

    return pl.pallas_call(
        body, name=name, out_shape=_sds((m, n), out_dtype), grid=(n // tn, m // tm),
        in_specs=[pl.BlockSpec((tm, kk), lambda j, i: (i, 0)), pl.BlockSpec((kk, tn), lambda j, i: (0, j))],
        out_specs=pl.BlockSpec((tm, tn), lambda j, i: (i, j)),
        compiler_params=_params(("parallel", "parallel")),
    )(a, b)


def _dh_matmul(dp_rest, dp_qkv, dpab, w_rest, w_qkv, w_ab, xc, xc_arrays, xc_into):
    lt = dp_rest.shape[0]
    tm = _tile(lt, (1088, 1024, 640, 512, 256, 128))
    nr, nq = dp_rest.shape[1] // D, dp_qkv.shape[1] // D
    nx, ni = xc.n, len(xc_into)
    ni_steps = lt // tm

    def body(*refs):
        dr_ref, dq_ref, ab_ref, wr_ref, wq_ref, wab_ref = refs[:6]
        x_in = refs[6:6 + nx]
        o_ref = refs[6 + nx + ni]
        x_out = refs[7 + nx + ni:7 + 2 * nx + ni]
        sems = refs[7 + 2 * nx + ni:]
        i = pl.program_id(0)
        k = pl.program_id(1)

        @pl.when((i == 0) & (k == 0))
        def _():
            xc.start(x_in, x_out, sems)

        @pl.when(k == 0)
        def _():
            o_ref[...] = _mm(ab_ref[...], wab_ref[...], NT)

        @pl.when(k < nr)
        def _():
            o_ref[...] += _mm(dr_ref[...], wr_ref[...], NT)

        @pl.when(k >= nr)
        def _():
            o_ref[...] += _mm(dq_ref[...], wq_ref[...], NT)

        @pl.when((i == ni_steps - 1) & (k == nr + nq - 1))
        def _():
            xc.wait(x_in, x_out, sems)

    rk = lambda k: jnp.minimum(k, nr - 1)
    qk = lambda k: jnp.maximum(k - nr, 0)
    return pl.pallas_call(
        body, name="dh_matmul", out_shape=(_sds((lt, D)),) + xc.out_shape, grid=(lt // tm, nr + nq),
        in_specs=[pl.BlockSpec((tm, D), lambda i, k: (i, rk(k))), pl.BlockSpec((tm, D), lambda i, k: (i, qk(k))),
                  pl.BlockSpec((tm, LANE), lambda i, k: (i, 0)),
                  pl.BlockSpec((D, D), lambda i, k: (0, rk(k))), pl.BlockSpec((D, D), lambda i, k: (0, qk(k))),
                  _full((D, LANE))] + [_ANY] * (nx + ni),
        out_specs=(pl.BlockSpec((tm, D), lambda i, k: (i, 0)),) + tuple([_ANY] * nx),
        scratch_shapes=xc.scratch, input_output_aliases={6 + nx + t: 1 + a for t, a in enumerate(xc_into)},
        compiler_params=_params(("arbitrary", "arbitrary")),
    )(dp_rest, dp_qkv, dpab, w_rest, w_qkv, w_ab, *xc_arrays, *xc_into.values())


def _modulation(cc, w_mod_g, b_mod):
    ws = w_mod_g.shape[2]

    def body(c_ref, w_ref, b_ref, o_ref):
        s, _ = _silu_g(c_ref[...])
        o_ref[...] = _mm(s, w_ref[0]) + b_ref[...]

    return pl.pallas_call(
        body, name="modulation", out_shape=_sds((8, 3 * D)), grid=(NDEV,),
        in_specs=[_full((8, D)), pl.BlockSpec((1, D, ws), lambda j: (j, 0, 0)), pl.BlockSpec((1, ws), lambda j: (0, j))],
        out_specs=pl.BlockSpec((8, ws), lambda j: (0, j)),
        compiler_params=_params(("parallel",)),
    )(cc, w_mod_g, b_mod)


def _prenorm(ctx, x, mods, g_pre):
    lc = ctx.shape[0]
    lt = lc + x.shape[0]
    tm = _tile(lc, (256, 128))
    nct = lc // tm

    def body(c_ref, x_ref, m_ref, g_ref, o_ref, ot_ref):
        is_ctx = pl.program_id(0) < nct
        x = jnp.where(is_ctx, c_ref[...], x_ref[...])
        shift = jnp.where(is_ctx, m_ref[1:2, 0:D], m_ref[0:1, 0:D])
        scale = jnp.where(is_ctx, m_ref[1:2, D:2 * D], m_ref[0:1, D:2 * D])
        r = lax.rsqrt(jnp.mean(x * x, axis=-1, keepdims=True) + EPS)
        h = ((x * r * g_ref[...]) * (1.0 + scale) + shift).astype(o_ref.dtype)
        o_ref[...] = h
        ot_ref[...] = h.T

    return pl.pallas_call(
        body, name="prenorm", out_shape=(_sds((lt, D), _BF), _sds((D, lt), _BF)), grid=(lt // tm,),
        in_specs=[pl.BlockSpec((tm, D), lambda i: (jnp.minimum(i, nct - 1), 0)),
                  pl.BlockSpec((tm, D), lambda i: (jnp.maximum(i - nct, 0), 0)), _full((8, 3 * D)), _full((1, D))],
        out_specs=(pl.BlockSpec((tm, D), lambda i: (i, 0)), pl.BlockSpec((D, tm), lambda i: (0, i))),
        compiler_params=_params(("parallel",)),
    )(ctx, x, mods, g_pre)


def _prenorm_bwd(ctx, x, dh, dy, mods, g_pre):
    lc = ctx.shape[0]
    lt = lc + x.shape[0]
    tm = _tile(lc, (256, 128))
    nct = lc // tm
    nl = (lt - lc) // tm

    def body(c_ref, x_ref, dh_ref, dy_ref, m_ref, g_ref, gx_ref, vec_ref):
        i = pl.program_id(0)

        @pl.when(i == 0)
        def _():
            vec_ref[...] = jnp.zeros_like(vec_ref)

        is_ctx = i < nct
        x = jnp.where(is_ctx, c_ref[...], x_ref[...])
        dh = dh_ref[...]
        g = g_ref[...]
        scale = jnp.where(is_ctx, m_ref[1:2, D:2 * D], m_ref[0:1, D:2 * D])
        r = lax.rsqrt(jnp.mean(x * x, axis=-1, keepdims=True) + EPS)
        n = x * r
        hn = n * g
        dsh = jnp.sum(dh, axis=0, keepdims=True)
        dsc = jnp.sum(dh * hn, axis=0, keepdims=True)
        dhn = dh * (1.0 + scale)
        vec_ref[4:5, :] += jnp.sum(dhn * n, axis=0, keepdims=True)
        dn = dhn * g
        dx = r * (dn - n * jnp.mean(dn * n, axis=-1, keepdims=True))

        @pl.when(is_ctx)
        def _():
            vec_ref[2:3, :] += dsh
            vec_ref[3:4, :] += dsc

        @pl.when(jnp.logical_not(is_ctx))
        def _():
            vec_ref[0:1, :] += dsh
            vec_ref[1:2, :] += dsc
            gx_ref[...] = dy_ref[...] + dx

    xrow = lambda i: (jnp.maximum(i - nct, 0), 0)
    return pl.pallas_call(
        body, name="prenorm_bwd", out_shape=(_sds((nl * tm, D)), _sds((8, D))), grid=(lt // tm,),
        in_specs=[pl.BlockSpec((tm, D), lambda i: (jnp.minimum(i, nct - 1), 0)), pl.BlockSpec((tm, D), xrow),
                  pl.BlockSpec((tm, D), lambda i: (i, 0)), pl.BlockSpec((tm, D), xrow), _full((8, 3 * D)), _full((1, D))],
        out_specs=(pl.BlockSpec((tm, D), xrow), _full((8, D))),
        compiler_params=_params(("arbitrary",)),
    )(ctx, x, dh, dy, mods, g_pre)


def _conv_parts(x, w, lc):
    lt = x.shape[0]
    row = lax.broadcasted_iota(jnp.int32, x.shape, 0)
    first = (row == 0) | (row == lc)
    last = (row == lc - 1) | (row == lt - 1)
    xp = jnp.where(first, 0.0, pltpu.roll(x, 1, 0))
    xn = jnp.where(last, 0.0, pltpu.roll(x, lt - 1, 0))
    y = w[0:1, :] * xp + w[1:2, :] * x + w[2:3, :] * xn
    return xp, xn, y, first, last


def _qkv_fwd(p, w_conv, lc):
    lt = p.shape[0]

    def body(p_ref, w_ref, o_ref):
        _, _, y, _, _ = _conv_parts(p_ref[...], w_ref[...], lc)
        s, _ = _silu_g(y)
        rs = lax.rsqrt(jnp.sum(s * s, axis=-1, keepdims=True) + EPS)
        o_ref[...] = s * jnp.where(pl.program_id(0) < 2 * NH, rs, 1.0)

    return pl.pallas_call(
        body, name="qkv_fwd", out_shape=_sds((lt, 3 * D)), grid=(3 * NH,),
        in_specs=[pl.BlockSpec((lt, DH), lambda j: (0, j)), pl.BlockSpec((3, DH), lambda j: (0, j))],
        out_specs=pl.BlockSpec((lt, DH), lambda j: (0, j)),
        compiler_params=_params(("parallel",)),
    )(p, w_conv)


def _qkv_bwd(p, w_conv, dqkv_f, dqkv_b, lc):
    lt = p.shape[0]

    def body(p_ref, w_ref, df_ref, db_ref, dp_ref, dw_ref):
        w = w_ref[...]
        xp, xn, y, first, last = _conv_parts(p_ref[...], w, lc)
        s, ds_dy = _silu_g(y)
        dn = df_ref[...] + db_ref[...]
        rs = lax.rsqrt(jnp.sum(s * s, axis=-1, keepdims=True) + EPS)
        nrm = s * rs
        ds_n = rs * (dn - nrm * jnp.sum(dn * nrm, axis=-1, keepdims=True))
        ds = jnp.where(pl.program_id(0) < 2 * NH, ds_n, dn)
        dy = ds * ds_dy
        dw_ref[0:1, :] = jnp.sum(dy * xp, axis=0, keepdims=True)
        dw_ref[1:2, :] = jnp.sum(dy * p_ref[...], axis=0, keepdims=True)
        dw_ref[2:3, :] = jnp.sum(dy * xn, axis=0, keepdims=True)
        dyn = jnp.where(last, 0.0, pltpu.roll(dy, lt - 1, 0))
        dyp = jnp.where(first, 0.0, pltpu.roll(dy, 1, 0))
        dp_ref[...] = (w[1:2, :] * dy + w[0:1, :] * dyn + w[2:3, :] * dyp).astype(dp_ref.dtype)

    return pl.pallas_call(
        body, name="qkv_bwd", out_shape=(_sds((lt, 3 * D), _BF), _sds((3, 3 * D))), grid=(3 * NH,),
        in_specs=[pl.BlockSpec((lt, DH), lambda j: (0, j)), pl.BlockSpec((3, DH), lambda j: (0, j)),
                  pl.BlockSpec((lt, DH), lambda j: (0, j)), pl.BlockSpec((lt, DH), lambda j: (0, j))],
        out_specs=(pl.BlockSpec((lt, DH), lambda j: (0, j)), pl.BlockSpec((3, DH), lambda j: (0, j))),
        compiler_params=_params(("parallel",)),
    )(p, w_conv, dqkv_f, dqkv_b)


def _masks(d):
    ri = lax.broadcasted_iota(jnp.int32, (CH, CH), 0)
    ci = lax.broadcasted_iota(jnp.int32, (CH, CH), 1)
    incl = (ri >= ci) if d == 0 else (ri <= ci)
    strict = (ri > ci) if d == 0 else (ri < ci)
    incl_t = (ri <= ci) if d == 0 else (ri >= ci)
    return incl, strict, incl_t, ri == ci


def _decays(d, ab, abt, alog_r, dtb_r, alog_c, dtb_c, incl, incl_t):
    g_full = -jnp.exp(alog_r) * _softplus(ab + dtb_r)
    beta_full = _sigmoid(ab)
    gc_full = _mmh(incl.astype(F32), g_full)
    gl_full = jnp.sum(g_full, axis=0, keepdims=True)
    gt_full = -jnp.exp(alog_c) * _softplus(abt + dtb_c)
    gct = _mmh(gt_full, incl_t.astype(F32))
    return g_full, beta_full, gc_full, gl_full, gt_full, gct


def _lane_onehot(idx, n=LANE):
    return (lax.broadcasted_iota(jnp.int32, (1, n), 1) == idx).astype(F32)


def _head_scalars(d, h, beta_full, gc_full, gl_full, gct):
    idx = d * NH + h
    oh = _lane_onehot(idx)
    gcol = jnp.sum(gc_full * oh, axis=-1, keepdims=True)
    bcol = jnp.sum(beta_full * _lane_onehot(2 * NH + idx), axis=-1, keepdims=True)
    gl = jnp.sum(gl_full * oh, axis=-1, keepdims=True)
    grow = gct[idx:idx + 1, :]
    return gcol, grow, bcol, gl


def _lockstep(gens):
    live = list(gens)
    while live:
        nxt = []
        for g in live:
            try:
                next(g)
                nxt.append(g)
            except StopIteration:
                pass
        live = nxt


def _chunk_local(qh, kh, vh, gcol, grow, bcol, gl, incl, strict):
    decay = jnp.where(incl, jnp.exp(gcol - grow), 0.0)
    kb = kh * bcol
    qs = qh * (DH ** -0.5)
    both = _mm(jnp.concatenate([kb, qs], axis=0), kh, NT)
    a = jnp.where(strict, both[:CH] * decay, 0.0)
    egc = jnp.exp(gcol)
    rhs_u = vh * bcol
    rhs_w = kb * egc
    attn = jnp.where(incl, both[CH:] * decay, 0.0)
    etail = jnp.exp(gl - gcol)
    return decay, kb, a, egc, rhs_u, rhs_w, qs, attn, etail


def _scan_specs(lt, lc, bwd_pass):
    assert lt % (SUB * CH) == 0 and lc % (SUB * CH) == 0
    nch = lt // (SUB * CH)
    ncc = lc // (SUB * CH)
    if not bwd_pass:
        cf = lambda s: s
        cb = lambda s: jnp.where(s < ncc, ncc - 1 - s, nch + ncc - 1 - s)
    else:
        cf = lambda s: nch - 1 - s
        cb = lambda s: jnp.where(s < nch - ncc, ncc + s, s - (nch - ncc))
    return nch, cf, cb


def _gdn_fwd(qkv, pab, abt, alog_r, dtb_r, alog_c, dtb_c, lc, xc, xc_arrays):
    lt = qkv.shape[0]
    nch, cf, cb = _scan_specs(lt, lc, False)
    nx = xc.n

    def body(*refs):
        qf, kf, vf, abf, abtf, qb, kb_, vb, abb, abtb, ar, dr, ac, dc = refs[:14]
        x_in = refs[14:14 + nx]
        of_ref, ob_ref, sf_ref, sb_ref, tf_ref, tb_ref = refs[14 + nx:20 + nx]
        x_out = refs[20 + nx:20 + 2 * nx]
        s_scr = refs[20 + 2 * nx]
        sems = refs[21 + 2 * nx:]

        @pl.when(pl.program_id(0) == 0)
        def _():
            s_scr[...] = jnp.zeros_like(s_scr)
            xc.start(x_in, x_out, sems)

        def chain(d, h, c, late, q_r, k_r, v_r, o_ref, sh_ref, th_ref, masks, decs):
            incl, strict, _, eye = masks
            sl = slice(h * DH, (h + 1) * DH)
            rows = slice(c * CH, (c + 1) * CH)
            qh, kh, vh = q_r[rows, sl], k_r[rows, sl], v_r[rows, sl]
            gcol, grow, bcol, gl = _head_scalars(d, h, *decs)
            _, _, a, egc, rhs_u, rhs_w, qs, attn, etail = _chunk_local(qh, kh, vh, gcol, grow, bcol, gl, incl, strict)
            yield
            n = -a
            t = jnp.where(eye, 1.0, 0.0) + n
            p = _mm3(n, n)
            yield
            for _ in range(4):
                r = _mm3(jnp.concatenate([t, p], axis=0), p)
                yield
                t = t + r[:CH]
                p = r[CH:]
            t = t + _mm3(t, p)
            yield
            for _ in range(3 * late):
                yield
            sol = _mm3(t, jnp.concatenate([rhs_u, rhs_w], axis=1))
            u, w = sol[:, :DH], sol[:, DH:]
            s = s_scr[d, h]
            sh_ref[c, h] = s
            th_ref[c, h] = t
            yield
            ws = _mm(jnp.concatenate([w, qs * egc], axis=0), s)
            yield
            v_new = u - ws[:CH]
            o_ref[rows, sl] = ws[CH:] + _mm(attn, v_new)
            s_scr[d, h] = s * jnp.exp(gl) + _mm(kh * etail, v_new, TN)

        chains = []
        for d, (q_r, k_r, v_r, ab_r, abt_r, o_ref, sh_ref, th_ref) in enumerate(
                ((qf, kf, vf, abf, abtf, of_ref, sf_ref, tf_ref), (qb, kb_, vb, abb, abtb, ob_ref, sb_ref, tb_ref))):
            masks = _masks(d)
            for pos, c in enumerate(range(SUB) if d == 0 else reversed(range(SUB))):
                _, beta_full, gc_full, gl_full, _, gct = _decays(
                    d, ab_r[c * CH:(c + 1) * CH, :], abt_r[c], ar[...], dr[...], ac[...], dc[...], masks[0], masks[2])
                for h in range(NH):
                    chains.append(chain(d, h, c, pos, q_r, k_r, v_r, o_ref, sh_ref, th_ref, masks,
                                        (beta_full, gc_full, gl_full, gct)))
        _lockstep(chains)

        @pl.when(pl.program_id(0) == nch // 2)
        def _():
            xc.forward(x_in, x_out, sems)

        @pl.when(pl.program_id(0) == nch - 1)
        def _():
            xc.wait(x_in, x_out, sems)

    def row(c, col):
        return pl.BlockSpec((SUB * CH, D), lambda s: (c(s), col))

    def chunk_in(c):
        return [row(c, 0), row(c, 1), row(c, 2), pl.BlockSpec((SUB * CH, LANE), lambda s: (c(s), 0)),
                pl.BlockSpec((SUB, 4 * NH, CH), lambda s: (c(s), 0, 0))]

    def hist(c, n):
        return pl.BlockSpec((SUB, NH, n, n), lambda s: (c(s), 0, 0, 0))

    small = [_full((1, LANE)), _full((1, LANE)), _full((4 * NH, 1)), _full((4 * NH, 1))]
    return pl.pallas_call(
        body, name="gdn_fwd", grid=(nch,),
        out_shape=(_sds((lt, D)), _sds((lt, D)), _sds((lt // CH, NH, DH, DH)), _sds((lt // CH, NH, DH, DH)),
                   _sds((lt // CH, NH, CH, CH)), _sds((lt // CH, NH, CH, CH))) + xc.out_shape,
        in_specs=chunk_in(cf) + chunk_in(cb) + small + [_ANY] * nx,
        out_specs=(pl.BlockSpec((SUB * CH, D), lambda s: (cf(s), 0)), pl.BlockSpec((SUB * CH, D), lambda s: (cb(s), 0)),
                   hist(cf, DH), hist(cb, DH), hist(cf, CH), hist(cb, CH)) + tuple([_ANY] * nx),
        scratch_shapes=[pltpu.VMEM((2, NH, DH, DH), F32)] + xc.scratch,
        compiler_params=_params(("arbitrary",)),
    )(qkv, qkv, qkv, pab, abt, qkv, qkv, qkv, pab, abt, alog_r, dtb_r, alog_c, dtb_c, *xc_arrays)


def _gdn_bwd(qkv, pab, abt, alog_r, dtb_r, alog_c, dtb_c, s_f, s_b, t_f, t_b, do, lc, xc, xc_arrays):
    lt = qkv.shape[0]
    nch, cf, cb = _scan_specs(lt, lc, True)
    nx = xc.n

    def body(*refs):
        qf, kf, vf, abf, abtf, sf_ref, tf_ref, dof, qb, kb_, vb, abb, abtb, sb_ref, tb_ref, dob, ar, dr, ac, dc = refs[:20]
        x_in = refs[20:20 + nx]
        dqf_ref, dqb_ref, dcf_ref, dcb_ref, drf_ref, drb_ref, vcol_ref, vrow_ref = refs[20 + nx:28 + nx]
        x_out = refs[28 + nx:28 + 2 * nx]
        ds_scr = refs[28 + 2 * nx]
        sems = refs[29 + 2 * nx:]

        @pl.when(pl.program_id(0) == 0)
        def _():
            ds_scr[...] = jnp.zeros_like(ds_scr)
            vcol_ref[...] = jnp.zeros_like(vcol_ref)
            vrow_ref[...] = jnp.zeros_like(vrow_ref)
            xc.start(x_in, x_out, sems)

        alog_r_, dtb_r_, alog_c_, dtb_c_ = ar[...], dr[...], ac[...], dc[...]
        lane2 = lax.broadcasted_iota(jnp.int32, (1, LANE), 1)
        acc = {}

        def chain(d, h, c, late, q_r, k_r, v_r, sh_ref, th_ref, do_r, dq_ref, masks, decs):
            incl, strict, _, _ = masks
            idx = d * NH + h
            sl = slice(h * DH, (h + 1) * DH)
            rows = slice(c * CH, (c + 1) * CH)
            qh, kh, vh = q_r[rows, sl], k_r[rows, sl], v_r[rows, sl]
            doh = do_r[rows, sl]
            gcol, grow, bcol, gl = _head_scalars(d, h, *decs)
            decay, kb, a, egc, rhs_u, rhs_w, qs, attn, etail = _chunk_local(qh, kh, vh, gcol, grow, bcol, gl, incl, strict)
            t = th_ref[c, h]
            s = sh_ref[c, h]
            sol = _mm3(t, jnp.concatenate([rhs_u, rhs_w], axis=1))
            u, w = sol[:, :DH], sol[:, DH:]
            q_dec = qs * egc
            k_tail = kh * etail
            egl = jnp.exp(gl)
            dq_dec = _mm(doh, s, NT)
            yield
            for _ in range(2 * late):
                yield
            ds_new = ds_scr[d, h]
            dv_new = _mm(attn, doh, TN) + _mm(k_tail, ds_new)
            dgl = jnp.sum(jnp.sum(ds_new * s, axis=0, keepdims=True), axis=-1, keepdims=True) * egl
            yield
            v_new = u - _mm(w, s)
            dw = -_mm(dv_new, s, NT)
            ds_scr[d, h] = ds_new * egl + _mm(q_dec, doh, TN) - _mm(w, dv_new, TN)
            yield
            dattn = jnp.where(incl, _mm(doh, v_new, NT), 0.0)
            dk_tail = _mm(v_new, ds_new, NT)
            dr = _mm3(t, jnp.concatenate([dv_new, dw], axis=1), TN)
            dr_u, dr_w = dr[:, :DH], dr[:, DH:]
            yield
            da = -jnp.where(strict, _mm3(dr, sol, NT), 0.0)
            nq = dattn * decay
            dqs = _mm(nq, kh) + dq_dec * egc
            dk = _mm(nq, qs, TN)
            yield
            dv = dr_u * bcol
            dbeta = jnp.sum(dr_u * vh, axis=-1, keepdims=True)
            dgc = jnp.sum(dr_w * rhs_w, axis=-1, keepdims=True)
            m = da * decay
            dkb = dr_w * egc + _mm(m, kh)
            dk = dk + _mm(m, kb, TN)
            pq = da * a + dattn * attn
            dgc = dgc + jnp.sum(pq, axis=-1, keepdims=True) + jnp.sum(dq_dec * q_dec, axis=-1, keepdims=True)
            dgr = -jnp.sum(pq, axis=0, keepdims=True)
            tt = jnp.sum(dk_tail * k_tail, axis=-1, keepdims=True)
            dk = dk + dk_tail * etail + dkb * bcol
            dgc = dgc - tt
            dgl = dgl + jnp.sum(tt, axis=0, keepdims=True)
            dbeta = dbeta + jnp.sum(dkb * kh, axis=-1, keepdims=True)
            dq_ref[rows, sl] = dqs * (DH ** -0.5)
            dq_ref[rows, D + h * DH:D + (h + 1) * DH] = dk
            dq_ref[rows, 2 * D + h * DH:2 * D + (h + 1) * DH] = dv
            acc.setdefault((d, c), []).append((idx, dgc, dgl, dbeta, dgr))

        dirs = ((qf, kf, vf, abf, abtf, sf_ref, tf_ref, dof, dqf_ref, dcf_ref, drf_ref),
                (qb, kb_, vb, abb, abtb, sb_ref, tb_ref, dob, dqb_ref, dcb_ref, drb_ref))
        chains, ctx_d = [], {}
        for d, (q_r, k_r, v_r, ab_r, abt_r, sh_ref, th_ref, do_r, dq_ref, _, _) in enumerate(dirs):
            masks = _masks(d)
            for pos, c in enumerate(reversed(range(SUB)) if d == 0 else range(SUB)):
                ab, abt = ab_r[c * CH:(c + 1) * CH, :], abt_r[c]
                g_full, beta_full, gc_full, gl_full, gt_full, gct = _decays(
                    d, ab, abt, alog_r_, dtb_r_, alog_c_, dtb_c_, masks[0], masks[2])
                ctx_d[(d, c)] = (masks, ab, abt, g_full, beta_full, gt_full)
                for h in range(NH):
                    chains.append(chain(d, h, c, pos, q_r, k_r, v_r, sh_ref, th_ref, do_r, dq_ref, masks,
                                        (beta_full, gc_full, gl_full, gct)))
        _lockstep(chains)
        for d, c in sorted(ctx_d):
            (incl, _, incl_t, _), ab, abt, g_full, beta_full, gt_full = ctx_d[(d, c)]
            dcol_ref, drow_ref = dirs[d][9], dirs[d][10]
            dgc_col = jnp.zeros((CH, LANE), F32)
            dgl_row = jnp.zeros((1, LANE), F32)
            dbeta_col = jnp.zeros((CH, LANE), F32)
            dgc_row = jnp.zeros((4 * NH, CH), F32)
            for idx, dgc, dgl, dbeta, dgr in acc[(d, c)]:
                oh = _lane_onehot(idx)
                dgc_col = dgc_col + dgc * oh
                dgl_row = dgl_row + dgl * oh
                dbeta_col = dbeta_col + dbeta * _lane_onehot(2 * NH + idx)
                ohc = (lax.broadcasted_iota(jnp.int32, (4 * NH, 1), 0) == idx).astype(F32)
                dgc_row = dgc_row + ohc * dgr
            dg_col = _mmh(incl_t.astype(F32), dgc_col) + dgl_row
            dg_row = _mmh(dgc_row, incl.astype(F32))
            sg_col = _sigmoid(ab + dtb_r_)
            da_col = dg_col * (-jnp.exp(alog_r_)) * sg_col
            dcol_ref[c * CH:(c + 1) * CH, :] = da_col + dbeta_col * beta_full * (1.0 - beta_full)
            da_row = dg_row * (-jnp.exp(alog_c_)) * _sigmoid(abt + dtb_c_)
            drow_ref[c] = da_row
            vcol_ref[0:1, :] += jnp.sum(dg_col * g_full, axis=0, keepdims=True)
            vcol_ref[1:2, :] += jnp.sum(da_col, axis=0, keepdims=True)
            rl = jnp.sum(dg_row * gt_full, axis=-1, keepdims=True)
            rd = jnp.sum(da_row, axis=-1, keepdims=True)
            vrow_ref[...] += jnp.where(lane2 == 0, rl, 0.0) + jnp.where(lane2 == 1, rd, 0.0)

        @pl.when(pl.program_id(0) == nch // 2)
        def _():
            xc.forward(x_in, x_out, sems)

        @pl.when(pl.program_id(0) == nch - 1)
        def _():
            xc.wait(x_in, x_out, sems)

    def row(c, col):
        return pl.BlockSpec((SUB * CH, D), lambda s: (c(s), col))

    def hist(c, n):
        return pl.BlockSpec((SUB, NH, n, n), lambda s: (c(s), 0, 0, 0))

    def chunk_in(c):
        return [row(c, 0), row(c, 1), row(c, 2), pl.BlockSpec((SUB * CH, LANE), lambda s: (c(s), 0)),
                pl.BlockSpec((SUB, 4 * NH, CH), lambda s: (c(s), 0, 0)), hist(c, DH), hist(c, CH), row(c, 0)]

    small = [_full((1, LANE)), _full((1, LANE)), _full((4 * NH, 1)), _full((4 * NH, 1))]
    return pl.pallas_call(
        body, name="gdn_bwd", grid=(nch,),
        out_shape=(_sds((lt, 3 * D)), _sds((lt, 3 * D)), _sds((lt, LANE)), _sds((lt, LANE)),
                   _sds((lt // CH, 4 * NH, CH)), _sds((lt // CH, 4 * NH, CH)), _sds((8, LANE)), _sds((4 * NH, LANE))) + xc.out_shape,
        in_specs=chunk_in(cf) + chunk_in(cb) + small + [_ANY] * nx,
        out_specs=(pl.BlockSpec((SUB * CH, 3 * D), lambda s: (cf(s), 0)), pl.BlockSpec((SUB * CH, 3 * D), lambda s: (cb(s), 0)),
                   pl.BlockSpec((SUB * CH, LANE), lambda s: (cf(s), 0)), pl.BlockSpec((SUB * CH, LANE), lambda s: (cb(s), 0)),
                   pl.BlockSpec((SUB, 4 * NH, CH), lambda s: (cf(s), 0, 0)), pl.BlockSpec((SUB, 4 * NH, CH), lambda s: (cb(s), 0, 0)),
                   _full((8, LANE)), _full((4 * NH, LANE))) + tuple([_ANY] * nx),
        scratch_shapes=[pltpu.VMEM((2, NH, DH, DH), F32)] + xc.scratch,
        compiler_params=_params(("arbitrary",)),
    )(qkv, qkv, qkv, pab, abt, s_f, t_f, do, qkv, qkv, qkv, pab, abt, s_b, t_b, do, alog_r, dtb_r, alog_c, dtb_c,
      *xc_arrays)


def _post(p, o_f, o_b, x, tgt, w_pa, w_pb, w_out, w_sp, w_spt, b_spb, ln_g, ln_b, g_on, g_post, gate_x, lc):
    lt = p.shape[0]
    l = x.shape[0]
    tm = GC
    nct = lc // tm

    def body(p_ref, of_ref, ob_ref, x_ref, t_ref, wpa, wpb, wout, wsp, wspt, bspb, lng_ref, lnb_ref, gon_ref, gpost_ref, gate_ref,
             dp_ref, do_ref, dy_ref, ya_ref, yb_ref, mg_ref, da_ref, db_ref, dout_ref, dwsp_ref, dbsp_ref, vec_ref):
        i = pl.program_id(0)

        @pl.when(i == 0)
        def _():
            dwsp_ref[...] = jnp.zeros_like(dwsp_ref)
            dbsp_ref[...] = jnp.zeros_like(dbsp_ref)
            vec_ref[...] = jnp.zeros_like(vec_ref)

        @pl.when(i < nct)
        def _():
            dp_ref[...] = jnp.zeros_like(dp_ref)
            do_ref[...] = jnp.zeros_like(do_ref)

        @pl.when(i >= nct)
        def _():
            lng, lnb, gon, gpost, gate = lng_ref[...], lnb_ref[...], gon_ref[...], gpost_ref[...], gate_ref[...]
            zb, ua, va, za, ga, gb = [p_ref[:, j * D:(j + 1) * D] for j in range(6)]
            o = of_ref[...] + ob_ref[...]
            szb, dszb = _silu_g(zb)
            nh_l, r_l = [], []
            for h in range(NH):
                oh = o[:, h * DH:(h + 1) * DH]
                r = lax.rsqrt(jnp.mean(oh * oh, axis=-1, keepdims=True) + EPS)
                nh_l.append(oh * r)
                r_l.append(r)
            nrm_b = jnp.concatenate(nh_l, axis=-1)
            gon_t = jnp.concatenate([gon] * NH, axis=-1)
            y_b = nrm_b * gon_t * szb
            u, du_dua = _gelu_g(ua)
            gv, dgv_dva = _gelu_g(va)
            xc = gv - jnp.mean(gv, axis=-1, keepdims=True)
            rs_ln = lax.rsqrt(jnp.mean(xc * xc, axis=-1, keepdims=True) + EPS)
            vhat = xc * rs_ln
            v = vhat * lng + lnb
            s_sp = jnp.concatenate(
                [_mm(wsp[g], v[:, g * DH:(g + 1) * DH]) + bspb[g] for g in range(NH)], axis=-1)
            sza, dsza = _silu_g(za)
            y_a = u * s_sp * sza
            a_pr = _mm(y_a, wpa[...])
            b_pr = _mm(y_b, wpb[...])
            sga = _sigmoid(ga)
            sgb = _sigmoid(gb)
            merged = sga * a_pr + sgb * b_pr
            out = _mm(merged, wout[...])
            rs_o = lax.rsqrt(jnp.mean(out * out, axis=-1, keepdims=True) + EPS)
            n_o = out * rs_o
            rr = n_o * gpost
            diff = x_ref[...] + gate * rr - t_ref[...]
            vec_ref[5:6, :] += jnp.sum(diff * diff, axis=0, keepdims=True)
            dy = diff * (1.0 / D)
            dy_ref[...] = dy
            vec_ref[0:1, :] += jnp.sum(dy * rr, axis=0, keepdims=True)
            dr = dy * gate
            vec_ref[1:2, :] += jnp.sum(dr * n_o, axis=0, keepdims=True)
            dn_o = dr * gpost
            dout = rs_o * (dn_o - n_o * jnp.mean(dn_o * n_o, axis=-1, keepdims=True))
            dmerged = _mm(dout, wout[...], NT)
            d_a = dmerged * sga
            d_b = dmerged * sgb
            dga = dmerged * a_pr * sga * (1.0 - sga)
            dgb = dmerged * b_pr * sgb * (1.0 - sgb)
            dy_a = _mm(d_a, wpa[...], NT)
            dy_b = _mm(d_b, wpb[...], NT)
            ya_ref[...] = y_a.astype(ya_ref.dtype).T
            yb_ref[...] = y_b.astype(yb_ref.dtype).T
            mg_ref[...] = merged.astype(mg_ref.dtype).T
            da_ref[...] = d_a.astype(da_ref.dtype)
            db_ref[...] = d_b.astype(db_ref.dtype)
            dout_ref[...] = dout.astype(dout_ref.dtype)
            dua = dy_a * s_sp * sza * du_dua
            ds_sp = dy_a * u * sza
            dza = dy_a * u * s_sp * dsza
            dv_l = []
            for g in range(NH):
                ds_g = ds_sp[:, g * DH:(g + 1) * DH]
                dv_l.append(_mm(wspt[g], ds_g))
                dwsp_ref[g] += _mm(ds_g, v[:, g * DH:(g + 1) * DH], NT)
                dbsp_ref[g] += ds_g
            dv = jnp.concatenate(dv_l, axis=-1)
            vec_ref[2:3, :] += jnp.sum(dv * vhat, axis=0, keepdims=True)
            vec_ref[3:4, :] += jnp.sum(dv, axis=0, keepdims=True)
            dvh = dv * lng
            dgv = rs_ln * (dvh - jnp.mean(dvh, axis=-1, keepdims=True) - vhat * jnp.mean(dvh * vhat, axis=-1, keepdims=True))
            dva = dgv * dgv_dva
            dzb = dy_b * nrm_b * gon_t * dszb
            dgon_full = jnp.sum(dy_b * nrm_b * szb, axis=0, keepdims=True)
            dgon = dgon_full[:, 0:DH]
            for h in range(1, NH):
                dgon = dgon + dgon_full[:, h * DH:(h + 1) * DH]
            vec_ref[4:5, 0:DH] += dgon
            dnb = dy_b * gon_t * szb
            do_l = []
            for h in range(NH):
                sl = slice(h * DH, (h + 1) * DH)
                dn_h = dnb[:, sl]
                do_l.append(r_l[h] * (dn_h - nh_l[h] * jnp.mean(dn_h * nh_l[h], axis=-1, keepdims=True)))
            do_ref[...] = jnp.concatenate(do_l, axis=-1)
            for j, val in enumerate((dzb, dua, dva, dza, dga, dgb)):
                dp_ref[:, j * D:(j + 1) * D] = val.astype(dp_ref.dtype)

    xrow = lambda i: (jnp.maximum(i - nct, 0), 0)
    wspec = _full((D, D))
    gspec = _full((NH, GC, GC))
    vspec = _full((1, D))
    bf_out = _sds((l, D), _BF)
    bf_out_t = _sds((D, l), _BF)
    xcol = lambda i: (0, jnp.maximum(i - nct, 0))
    return pl.pallas_call(
        body, name="post", grid=(lt // tm,),
        out_shape=(_sds((lt, NREST), _BF), _sds((lt, D)), _sds((l, D)), bf_out_t, bf_out_t, bf_out_t, bf_out, bf_out, bf_out,
                   _sds((NH, GC, GC)), _sds((NH, GC, GC)), _sds((8, D))),
        in_specs=[pl.BlockSpec((tm, NREST), lambda i: (i, 0)), pl.BlockSpec((tm, D), lambda i: (i, 0)),
                  pl.BlockSpec((tm, D), lambda i: (i, 0)), pl.BlockSpec((tm, D), xrow), pl.BlockSpec((tm, D), xrow),
                  wspec, wspec, wspec, gspec, gspec, gspec, vspec, vspec, _full((1, DH)), vspec, vspec],
        out_specs=(pl.BlockSpec((tm, NREST), lambda i: (i, 0)), pl.BlockSpec((tm, D), lambda i: (i, 0)),
                   pl.BlockSpec((tm, D), xrow), pl.BlockSpec((D, tm), xcol), pl.BlockSpec((D, tm), xcol),
                   pl.BlockSpec((D, tm), xcol), pl.BlockSpec((tm, D), xrow), pl.BlockSpec((tm, D), xrow),
                   pl.BlockSpec((tm, D), xrow), gspec, gspec, _full((8, D))),
        compiler_params=_params(("arbitrary",)),
    )(p, o_f, o_b, x, tgt, w_pa, w_pb, w_out, w_sp, w_spt, b_spb, ln_g, ln_b, g_on, g_post, gate_x)


def _sum_parts(parts, name):
    r = parts.shape[1]
    tr = r if NDEV * r * LANE * 4 <= (8 << 20) else _tile(r, (512, 256, 128, 64, 32, 16, 8))

    def body(p_ref, o_ref):
        acc = p_ref[0]
        for s in range(1, NDEV):
            acc = acc + p_ref[s]
        o_ref[...] = acc

    return pl.pallas_call(
        body, name=name, out_shape=_sds((r, LANE)), grid=(r // tr,),
        in_specs=[pl.BlockSpec((NDEV, tr, LANE), lambda i: (0, i, 0))],
        out_specs=pl.BlockSpec((tr, LANE), lambda i: (i, 0)),
        compiler_params=_params(("parallel",)),
    )(parts)


def _mod_bwd(c_all, c_ctx, dmx, dmc, w_mod_g):
    ws = w_mod_g.shape[2]

    def body(ca_ref, cc_ref, dsh_ref, dmx_ref, dmc_ref, dmc_sh_ref, w_ref, gw_ref, gc_ref, gb_ref):
        sc, _ = _silu_g(ca_ref[...])
        scc, dscc = _silu_g(cc_ref[...])
        dmc_tot = jnp.sum(dmc_ref[...], axis=0, keepdims=True)
        gb_ref[...] = jnp.sum(dmx_ref[...], axis=0, keepdims=True) + dmc_tot
        lhs = jnp.concatenate([sc, jnp.broadcast_to(scc, (8, D))], axis=0)
        rhs = jnp.concatenate([dsh_ref[...], dmc_sh_ref[...]], axis=0)
        gw_ref[...] = _mmh(lhs, rhs, TN)
        acc = jnp.zeros((8, D), F32)
        tot8 = jnp.broadcast_to(dmc_tot, (8, 3 * D))
        for j in range(NDEV):
            acc = acc + _mm(tot8[:, j * ws:(j + 1) * ws], w_ref[j], NT)
        gc_ref[...] = acc[0:1, :] * dscc

    return pl.pallas_call(
        body, name="mod_bwd", out_shape=(_sds((D, ws)), _sds((1, D)), _sds((1, 3 * D))),
        compiler_params=_params(),
    )(c_all, c_ctx, _my_cols(dmx, ws), dmx, dmc, _my_cols(dmc, ws), w_mod_g)


def _my_cols(a, ws):
    me = 4 * lax.axis_index("x") + 2 * lax.axis_index("y") + lax.axis_index("c")
    return lax.dynamic_slice_in_dim(a, me * ws, ws, axis=1)


def _pair_sum(mine, other, name):
    n, r, c = mine.shape
    tr = _tile(r, (256, 128, 64, 32, 16, 8))

    def body(a_ref, b_ref, o_ref):
        o_ref[...] = (a_ref[...].astype(F32) + b_ref[...].astype(F32)).astype(o_ref.dtype)

    blk = pl.BlockSpec((1, tr, c), lambda j, i: (j, i, 0))
    return pl.pallas_call(
        body, name=name, out_shape=_sds((n, r, c), mine.dtype), grid=(n, r // tr),
        in_specs=[blk, blk], out_specs=blk, compiler_params=_params(("parallel", "parallel")),
    )(mine, other)


def _adamw(parts, w, m, v, name, chip_sums_below=None):
    s_, r, c = parts.shape
    tr = _tile(r, (128, 64, 32, 16, 8)) if r * c * 4 > (1 << 20) else r
    c1 = 1.0 / (1.0 - ADAM_B1 ** ADAM_STEP)
    c2 = 1.0 / (1.0 - ADAM_B2 ** ADAM_STEP)

    def body(p_ref, w_ref, m_ref, v_ref, g_ref, d_ref, nm_ref, nv_ref):
        if chip_sums_below is None:
            part = lambda s: p_ref[s].astype(F32)
        else:
            core = lax.axis_index("c")
            me = 4 * lax.axis_index("x") + 2 * lax.axis_index("y") + core
            every = me >= chip_sums_below
            part = lambda s: jnp.where(every | (core == s % 2), p_ref[s].astype(F32), 0.0)
        g = part(0)
        for s in range(1, s_):
            g = g + part(s)
        m_new = ADAM_B1 * m_ref[...] + (1.0 - ADAM_B1) * g
        v_new = ADAM_B2 * v_ref[...] + (1.0 - ADAM_B2) * (g * g)
        g_ref[...] = g
        nm_ref[...] = m_new
        nv_ref[...] = v_new
        d_ref[...] = -ADAM_LR * ((m_new * c1) / (jnp.sqrt(v_new * c2) + ADAM_EPS) + ADAM_WD * w_ref[...])

    blk = pl.BlockSpec((tr, c), lambda i: (i, 0))
    o = _sds((r, c))
    return pl.pallas_call(
        body, name=name, out_shape=(o, o, o, o), grid=(r // tr,),
        in_specs=[pl.BlockSpec((s_, tr, c), lambda i: (0, i, 0)), blk, blk, blk],
        out_specs=(blk, blk, blk, blk),
        compiler_params=_params(("parallel",)),
    )(parts, w, m, v)


def _adamw_many(gs, ws, ms, vs, name):
    n = len(gs)
    c1 = 1.0 / (1.0 - ADAM_B1 ** ADAM_STEP)
    c2 = 1.0 / (1.0 - ADAM_B2 ** ADAM_STEP)

    def body(*refs):
        g_in, w_in_, m_in, v_in = (refs[k * n:(k + 1) * n] for k in range(4))
        g_out, d_out, m_out, v_out = (refs[(4 + k) * n:(5 + k) * n] for k in range(4))
        for p in range(n):
            g = g_in[p][...]
            m_new = ADAM_B1 * m_in[p][...] + (1.0 - ADAM_B1) * g
            v_new = ADAM_B2 * v_in[p][...] + (1.0 - ADAM_B2) * (g * g)
            g_out[p][...] = g
            m_out[p][...] = m_new
            v_out[p][...] = v_new
            d_out[p][...] = -ADAM_LR * ((m_new * c1) / (jnp.sqrt(v_new * c2) + ADAM_EPS) + ADAM_WD * w_in_[p][...])

    shapes = tuple(_sds(g.shape) for g in gs)
    res = pl.pallas_call(body, name=name, out_shape=shapes * 4, compiler_params=_params())(*gs, *ws, *ms, *vs)
    return [res[k * n:(k + 1) * n] for k in range(4)]


def _rows(a):
    flat = a.reshape(-1)
    n = flat.shape[0]
    r = -(-n // (8 * LANE)) * 8
    return jnp.pad(flat, (0, r * LANE - n)).reshape(r, LANE)


def _pack(items):
    parts, layout, at = [], [], 0
    for name, a in items:
        rws = _rows(a.astype(F32))
        layout.append((name, at, rws.shape[0], a.shape))
        parts.append(rws)
        at += rws.shape[0]
    return jnp.concatenate(parts, axis=0), layout


def _unpack(packed, layout):
    out = {}
    for name, at, r, shape in layout:
        n = 1
        for s in shape:
            n *= s
        out[name] = packed[at:at + r].reshape(-1)[:n].reshape(shape)
    return out


def kernel(x, c, ctx, c_ctx, w_mod, b_mod, g_pre, g_post, w_in, w_conv, a_log, dt_bias, g_onorm, gm_ln_g, gm_ln_b, w_sp, b_sp, w_pa, w_pb, w_out, loss_target, m_c_ctx, m_w_mod, m_b_mod, m_g_pre, m_g_post, m_w_in, m_w_conv, m_a_log, m_dt_bias, m_g_onorm, m_gm_ln_g, m_gm_ln_b, m_w_sp, m_b_sp, m_w_pa, m_w_pb, m_w_out, v_c_ctx, v_w_mod, v_b_mod, v_g_pre, v_g_post, v_w_in, v_w_conv, v_a_log, v_dt_bias, v_g_onorm, v_gm_ln_g, v_gm_ln_b, v_w_sp, v_b_sp, v_w_pa, v_w_pb, v_w_out):
    l = x.shape[1]
    lc = ctx.shape[1]
    lt = l + lc
    nch = lt // CH
    me = 4 * lax.axis_index("x") + 2 * lax.axis_index("y") + lax.axis_index("c")
    wsh = w_in.shape[2]
    off_a = 3 * D
    n_ab = 4 * NH
    jb = off_a // wsh
    o1 = off_a - jb * wsh
    o2 = o1 + n_ab
    assert o2 <= wsh and NREST == (NDEV - jb) * wsh - o2
    split = jb + 1

    w_in_bf = w_in[0].astype(_BF)
    wg_lo, wg_mod, wg_conv, c_all = _exchange(
        [w_in_bf, w_mod[0].astype(_BF), w_conv[0], c], ["gather_lo", "gather", "gather", "gather"],
        "gather_first", split)
    w_qkv = jnp.concatenate([wg_lo[j][:, :wsh] for j in range(jb)] + [wg_lo[jb][:, :o1]], axis=1)
    w_ab = jnp.pad(wg_lo[jb][:, o1:o2], ((0, 0), (0, LANE - n_ab)))
    wconv_full = jnp.moveaxis(wg_conv, 0, 1).reshape(3, 3 * D)
    c_all = c_all.reshape(NDEV, D)

    cc = jnp.concatenate([c, c_ctx.reshape(1, D), jnp.zeros((6, D), F32)], axis=0)
    mods = _modulation(cc, wg_mod, b_mod)
    h, h_t = _prenorm(ctx[0], x[0], mods, g_pre)
    p_qkv = _matmul_nn(h, w_qkv, "in_proj_qkv")
    pab = _matmul_nn(h, w_ab, "in_proj_ab")
    abt = jnp.swapaxes(pab[:, :n_ab].reshape(nch, CH, n_ab), 1, 2)
    alog16, dtb16 = a_log.reshape(1, 2 * NH), dt_bias.reshape(1, 2 * NH)
    alog_r = jnp.pad(alog16, ((0, 0), (0, LANE - 2 * NH)))
    dtb_r = jnp.pad(dtb16, ((0, 0), (0, LANE - 2 * NH)))
    alog_c = jnp.pad(alog16.reshape(2 * NH, 1), ((0, 2 * NH), (0, 0)))
    dtb_c = jnp.pad(dtb16.reshape(2 * NH, 1), ((0, 2 * NH), (0, 0)))
    qkv = _qkv_fwd(p_qkv, wconv_full, lc)
    late = [w_in_bf, w_pa[0].astype(_BF), w_pb[0].astype(_BF), w_out[0].astype(_BF)]
    xc_late = _Exchange(zip(late, ["gather_hi", "gather", "gather", "gather"]), split)
    o_f, o_b, s_f, s_b, t_f, t_b, wg_hi, wg_pa, wg_pb, wg_out = _gdn_fwd(
        qkv, pab, abt, alog_r, dtb_r, alog_c, dtb_c, lc, xc_late, late)
    w_rest = jnp.concatenate([wg_lo[jb][:, o2:wsh]] + [wg_hi[j][:, :wsh] for j in range(split, NDEV)], axis=1)
    wf_pa, wf_pb, wf_out = wg_pa.reshape(D, D), wg_pb.reshape(D, D), wg_out.reshape(D, D)
    p_rest = _matmul_nn(h, w_rest, "in_proj_rest")

    w_spt = jnp.swapaxes(w_sp[0], 1, 2)
    b_spb = jnp.broadcast_to(b_sp[0][:, :, None], (NH, GC, GC))
    gate_x = mods[0:1, 2 * D:]
    dp_rest, do, dy, ya, yb, mg, d_a, d_b, dout, dwsp, dbsp_l, pvec = _post(
        p_rest, o_f, o_b, x[0], loss_target[0], wf_pa, wf_pb, wf_out, w_sp[0], w_spt, b_spb, gm_ln_g, gm_ln_b,
        g_onorm, g_post, gate_x, lc)

    dw_rest = _matmul_nn(h_t, dp_rest, "dw_in_rest", _BF)
    o3 = wsh - o2
    chunks_hi = jnp.moveaxis(dw_rest[:, o3:].reshape(D, NDEV - split, wsh), 1, 0)
    dw_pa = _matmul_nn(ya, d_a, "dw_pa", _BF).reshape(NDEV, D // NDEV, D)
    dw_pb = _matmul_nn(yb, d_b, "dw_pb", _BF).reshape(NDEV, D // NDEV, D)
    dw_out = _matmul_nn(mg, dout, "dw_out", _BF).reshape(NDEV, D // NDEV, D)
    small_a, lay_a = _pack([
        ("g_post", pvec[1]), ("g_onorm", pvec[4, :DH]), ("gm_ln_g", pvec[2]), ("gm_ln_b", pvec[3]), ("w_sp", dwsp),
        ("b_sp", jnp.sum(dbsp_l, axis=-1)), ("loss", pvec[5]), ("dgate", pvec[0])])
    (theirs_hi,) = _exchange([chunks_hi], ["sibling"], "pair_swap_hi")
    chip_hi = _pair_sum(chunks_hi, theirs_hi[0], "pair_sum_hi")
    early = [chip_hi, dw_pa, dw_pb, dw_out, small_a]
    xc_early = _Exchange(zip(early, ["scatter_par_hi", "scatter", "scatter", "scatter", "gather"]), split)

    dqkv_f, dqkv_b, dcol_f, dcol_b, drow_f, drow_b, gvec_c, gvec_r, r_in, r_pa, r_pb, r_out, small_a_all = _gdn_bwd(
        qkv, pab, abt, alog_r, dtb_r, alog_c, dtb_c, s_f, s_b, t_f, t_b, do, lc, xc_early, early)
    dp_qkv, dwconv = _qkv_bwd(p_qkv, wconv_full, dqkv_f, dqkv_b, lc)
    drow = jnp.swapaxes(drow_f + drow_b, 1, 2).reshape(lt, n_ab)
    dpab = (dcol_f + dcol_b + jnp.pad(drow, ((0, 0), (0, LANE - n_ab)))).astype(_BF)

    dw_qkv = _matmul_nn(h_t, dp_qkv, "dw_in_qkv", _BF)
    dw_ab = _matmul_nn(h_t, dpab, "dw_in_ab", _BF)
    dw_lo = jnp.concatenate([dw_qkv, dw_ab[:, :n_ab], dw_rest[:, :o3]], axis=1)
    chunks_lo = jnp.moveaxis(dw_lo.reshape(D, split, wsh), 1, 0)
    (theirs,) = _exchange([chunks_lo], ["sibling"], "pair_swap")
    chip_lo = _pair_sum(chunks_lo, theirs[0], "pair_sum")
    xc_last = _Exchange([(chip_lo, "scatter_par_lo")], split)
    dh, r_in = _dh_matmul(dp_rest, dp_qkv, dpab, w_rest, w_qkv, w_ab, xc_last, [chip_lo], {0: r_in})
    grad_x, nvec = _prenorm_bwd(ctx[0], x[0], dh, dy, mods, g_pre)

    dalog = gvec_c[0, :2 * NH] + gvec_r[:2 * NH, 0]
    ddtb = gvec_c[1, :2 * NH] + gvec_r[:2 * NH, 1]
    small_b, lay_b = _pack([
        ("g_pre", nvec[4]), ("a_log", dalog), ("dt_bias", ddtb), ("w_conv", dwconv),
        ("dshift", nvec[0]), ("dscale", nvec[1]), ("dshift_c", nvec[2]), ("dscale_c", nvec[3])])
    (small_b_all,) = _exchange([small_b], ["gather"], "gather_small")
    tot = _unpack(_sum_parts(small_a_all, "sum_small_a"), lay_a)
    tot.update(_unpack(_sum_parts(small_b_all, "sum_small_b"), lay_b))

    def per_device(packed_all, layout, name):
        at, r = [(a_, r_) for nm, a_, r_, _ in layout if nm == name][0]
        return packed_all[:, at:at + r].reshape(NDEV, -1)

    dmx_all = jnp.concatenate([per_device(small_b_all, lay_b, "dshift"), per_device(small_b_all, lay_b, "dscale"),
                               per_device(small_a_all, lay_a, "dgate")], axis=1)
    dmc_all = jnp.concatenate([per_device(small_b_all, lay_b, "dshift_c"), per_device(small_b_all, lay_b, "dscale_c"),
                               jnp.zeros((NDEV, D), F32)], axis=1)
    g_wmod, g_cctx, g_bmod = _mod_bwd(c_all, c_ctx.reshape(1, D), dmx_all, dmc_all, wg_mod)
    loss = 0.5 / D * jnp.sum(tot["loss"])
    ws_conv = w_conv.shape[2]
    g_wconv = lax.dynamic_slice_in_dim(tot["w_conv"], me * ws_conv, ws_conv, axis=1)

    small_names = ["c_ctx", "b_mod", "g_pre", "g_post", "a_log", "dt_bias", "g_onorm", "gm_ln_g", "gm_ln_b",
                   "w_sp", "b_sp", "w_conv"]
    wts = dict(c_ctx=c_ctx, b_mod=b_mod, g_pre=g_pre, g_post=g_post, a_log=a_log, dt_bias=dt_bias, g_onorm=g_onorm,
               gm_ln_g=gm_ln_g, gm_ln_b=gm_ln_b, w_sp=w_sp, b_sp=b_sp, w_conv=w_conv)
    ms = dict(c_ctx=m_c_ctx, b_mod=m_b_mod, g_pre=m_g_pre, g_post=m_g_post, a_log=m_a_log, dt_bias=m_dt_bias,
              g_onorm=m_g_onorm, gm_ln_g=m_gm_ln_g, gm_ln_b=m_gm_ln_b, w_sp=m_w_sp, b_sp=m_b_sp, w_conv=m_w_conv)
    vs = dict(c_ctx=v_c_ctx, b_mod=v_b_mod, g_pre=v_g_pre, g_post=v_g_post, a_log=v_a_log, dt_bias=v_dt_bias,
              g_onorm=v_g_onorm, gm_ln_g=v_gm_ln_g, gm_ln_b=v_gm_ln_b, w_sp=v_w_sp, b_sp=v_b_sp, w_conv=v_w_conv)
    gs = dict(tot)
    gs.update(c_ctx=g_cctx, b_mod=g_bmod, w_conv=g_wconv)
    flat = lambda a: a.reshape(-1, a.shape[-1])
    res_small = [
        {nm: a.reshape(wts[nm].shape) for nm, a in zip(small_names, arrays)}
        for arrays in _adamw_many([flat(gs[nm].reshape(wts[nm].shape)) for nm in small_names],
                                  [flat(wts[nm]) for nm in small_names], [flat(ms[nm]) for nm in small_names],
                                  [flat(vs[nm]) for nm in small_names], "adamw_small")]
    res_big = {
        "w_mod": _adamw(g_wmod[None], w_mod[0], m_w_mod[0], v_w_mod[0], "adamw_w_mod"),
        "w_in": _adamw(r_in, w_in[0], m_w_in[0], v_w_in[0], "adamw_w_in", chip_sums_below=NDEV),
        "w_pa": _adamw(r_pa, w_pa[0], m_w_pa[0], v_w_pa[0], "adamw_w_pa"),
        "w_pb": _adamw(r_pb, w_pb[0], m_w_pb[0], v_w_pb[0], "adamw_w_pb"),
        "w_out": _adamw(r_out, w_out[0], m_w_out[0], v_w_out[0], "adamw_w_out"),
    }
    order = ["c_ctx", "w_mod", "b_mod", "g_pre", "g_post", "w_in", "w_conv", "a_log", "dt_bias", "g_onorm",
             "gm_ln_g", "gm_ln_b", "w_sp", "b_sp", "w_pa", "w_pb", "w_out"]
    outs = [loss, grad_x[None]]
    for k in range(4):
        for nm in order:
            if nm in res_big:
                outs.append(res_big[nm][k][None])
            else:
                outs.append(res_small[k][nm])
    return tuple(outs)
```

```python
import functools

import jax
import jax.numpy as jnp
from jax import lax
from jax.experimental import pallas as pl
from jax.experimental.pallas import tpu as pltpu

F32 = jnp.float32
_BF = jnp.bfloat16
_HI = lax.Precision.HIGHEST
D = 1024
NH = 8
DH = 128
CH = 64
SUB = 2
GC = 128
NREST = 6 * D
NMAIN = NREST + 3 * D
EPS = 1e-6
LANE = 128
NDEV = 8
VMEM_LIMIT = 56 * 1024 * 1024
MESH = pl.DeviceIdType.MESH

ADAM_LR, ADAM_B1, ADAM_B2, ADAM_EPS, ADAM_WD, ADAM_STEP = 0.001, 0.9, 0.999, 1e-08, 0.01, 10

NN = ((1,), (0,))
NT = ((1,), (1,))
TN = ((0,), (0,))


def _dot(a, b, dims=NN, prec=None):
    return lax.dot_general(a, b, (dims, ((), ())), precision=prec, preferred_element_type=F32)


def _mm(a, b, dims=NN):
    return _dot(a.astype(_BF), b.astype(_BF), dims)


def _mmh(a, b, dims=NN):
    return _dot(a.astype(F32), b.astype(F32), dims, _HI)


def _split(a):
    hi = a.astype(_BF)
    return hi, (a - hi.astype(F32)).astype(_BF)


def _mm3(a, b, dims=NN):
    ah, al = _split(a)
    bh, bl = _split(b)
    return _dot(ah, bh, dims) + (_dot(ah, bl, dims) + _dot(al, bh, dims))


def _sigmoid(x):
    return 1.0 / (1.0 + jnp.exp(-x))


def _silu_g(x):
    s = _sigmoid(x)
    return x * s, s * (1.0 + x * (1.0 - s))


def _gelu_g(x):
    c = 0.7978845608028654
    t = jnp.tanh(c * (x + 0.044715 * (x * x * x)))
    cdf = 0.5 * (1.0 + t)
    return x * cdf, cdf + 0.5 * x * (1.0 - t * t) * c * (1.0 + 3 * 0.044715 * x * x)


def _softplus(x):
    return jnp.maximum(x, 0.0) + jnp.log(1.0 + jnp.exp(-jnp.abs(x)))


def _params(sem=None):
    return pltpu.CompilerParams(dimension_semantics=sem, vmem_limit_bytes=VMEM_LIMIT)


def _tile(n, pref):
    for t in pref:
        if n % t == 0:
            return t
    return n


def _full(shape):
    nd = len(shape)
    return pl.BlockSpec(shape, lambda *_: (0,) * nd)


def _sds(shape, dt=F32):
    return jax.ShapeDtypeStruct(shape, dt)


MAX_PIECES = 12
PIECE_BYTES = 256 * 1024


def _piece_slices(shape, itemsize):
    total = itemsize
    for d in shape:
        total *= d
    want = min(MAX_PIECES, total // PIECE_BYTES)
    lead = shape[0] if len(shape) >= 3 else 1
    rows = shape[-2] if len(shape) >= 2 else 1
    if want < 2 or lead > want:
        return [()]
    m = max([n for n in (8, 4, 2, 1) if n * lead <= want and rows % (16 * n) == 0], default=1)
    if m * lead < 2:
        return [()]
    rs = rows // m
    mid = (slice(None),) * max(len(shape) - 3, 0)
    if len(shape) >= 3:
        return [(i,) + mid + (pl.ds(j * rs, rs),) for i in range(lead) for j in range(m)]
    return [(pl.ds(j * rs, rs),) for j in range(m)]


class _Pieces:
    def __init__(self, copies):
        self.copies = copies

    def start(self):
        for cp in self.copies:
            cp.start()

    def wait_send(self):
        for cp in self.copies:
            cp.wait_send()

    def wait_recv(self):
        for cp in self.copies:
            cp.wait_recv()

    def wait(self):
        for cp in self.copies:
            cp.wait()


class _Exchange:
    def __init__(self, specs, split):
        self.specs = list(specs)
        self.split = split
        self.n = len(self.specs)
        def out(a, k):
            if k == "sibling":
                return (1,) + tuple(a.shape)
            return (NDEV,) + (tuple(a.shape) if k.startswith("gather") else tuple(a.shape[1:]))

        self.out_shape = tuple(_sds(out(a, k), a.dtype) for a, k in self.specs)
        self.pieces = [_piece_slices(o.shape[1:], jnp.dtype(o.dtype).itemsize) for o in self.out_shape]
        self.sem_base = [(NDEV - 1) * sum(len(p) for p in self.pieces[:a]) for a in range(self.n + 1)]
        self.scratch = [pltpu.SemaphoreType.DMA((self.sem_base[-1],)), pltpu.SemaphoreType.DMA((self.sem_base[-1],)),
                        pltpu.SemaphoreType.DMA((self.sem_base[-1] // (NDEV - 1),))]

    def _local(self, sems, a, src, dst):
        base = self.sem_base[a] // (NDEV - 1)
        return _Pieces([pltpu.make_async_copy(src.at[sl] if sl else src, dst.at[sl] if sl else dst, sems[2].at[base + p])
                        for p, sl in enumerate(self.pieces[a])])

    def _remote(self, sems, a, k, src, dst, to):
        send_sems, recv_sems, _ = sems
        base = self.sem_base[a] + k * len(self.pieces[a])
        return _Pieces([
            pltpu.make_async_remote_copy(
                src_ref=src.at[sl] if sl else src, dst_ref=dst.at[sl] if sl else dst, send_sem=send_sems.at[base + p],
                recv_sem=recv_sems.at[base + p], device_id=to, device_id_type=MESH)
            for p, sl in enumerate(self.pieces[a])])

    def _ok(self, kind, idx):
        if kind.endswith("_lo"):
            return idx < self.split
        if kind.endswith("_hi"):
            return idx >= self.split
        return True

    def _phases(self, ins, outs, sems):
        x, y, c = lax.axis_index("x"), lax.axis_index("y"), lax.axis_index("c")
        me = 4 * x + 2 * y + c
        sib = (x, y, 1 - c)
        sib_idx = 4 * x + 2 * y + (1 - c)
        chips = [(1 - x, y), (x, 1 - y), (1 - x, 1 - y)]
        starts, forwards, waits = [], [], []
        for a, (_, kind) in enumerate(self.specs):
            ok = functools.partial(self._ok, kind)
            if kind.startswith("gather"):
                def copy(k, block, to, src=None, a=a):
                    rows = outs[a].at[block]
                    return self._remote(sems, a, k, rows if src is None else src, rows, to)

                loc = self._local(sems, a, ins[a], outs[a].at[me])
                first = [copy(0, me, sib, ins[a])] + [copy(1 + j, me, (*chip, c), ins[a]) for j, chip in enumerate(chips)]
                starts += [(ok(me), loc.start)] + [(ok(me), cp.start) for cp in first]
                waits += [(ok(me), loc.wait)] + [(ok(me), cp.wait_send) for cp in first]
                for j, chip in enumerate(chips):
                    origin = 4 * chip[0] + 2 * chip[1] + c
                    passed = copy(4 + j, origin, sib)
                    forwards += [(ok(origin), copy(1 + j, origin, sib).wait_recv), (ok(origin), passed.start)]
                    waits.append((ok(origin), passed.wait_send))
                    other = 4 * chip[0] + 2 * chip[1] + (1 - c)
                    waits.append((ok(other), copy(4 + j, other, sib).wait_recv))
                waits.append((ok(sib_idx), copy(0, sib_idx, sib).wait_recv))
            elif kind == "sibling":
                swap = self._remote(sems, a, 0, ins[a], outs[a].at[0], sib)
                starts.append((True, swap.start))
                waits += [(True, swap.wait_send), (True, swap.wait_recv)]
            else:
                base = self.split if kind.endswith("_hi") else 0
                same_core_only = "_par" in kind

                def src(idx, a=a, base=base):
                    return ins[a].at[jnp.clip(idx - base, 0, ins[a].shape[0] - 1)]

                loc = self._local(sems, a, src(me), outs[a].at[me])
                starts.append((ok(me), loc.start))
                waits.append((ok(me), loc.wait))
                for k in range(1, NDEV):
                    if same_core_only and k & 1:
                        continue
                    px = 1 - x if (k >> 2) & 1 else x
                    py = 1 - y if (k >> 1) & 1 else y
                    pc = 1 - c if k & 1 else c
                    pidx = 4 * px + 2 * py + pc
                    send = self._remote(sems, a, k - 1, src(pidx), outs[a].at[me], (px, py, pc))
                    arrive = self._remote(sems, a, k - 1, src(pidx), outs[a].at[pidx], (px, py, pc))
                    starts.append((ok(pidx), send.start))
                    waits += [(ok(pidx), send.wait_send), (ok(me), arrive.wait_recv)]
        return starts, forwards, waits

    @staticmethod
    def _run(actions):
        for cond, fn in actions:
            if cond is True:
                fn()
            else:
                pl.when(cond)(fn)

    def start(self, ins, outs, sems):
        self._run(self._phases(ins, outs, sems)[0])

    def forward(self, ins, outs, sems):
        self._run(self._phases(ins, outs, sems)[1])

    def wait(self, ins, outs, sems):
        self._run(self._phases(ins, outs, sems)[2])


_ANY = pl.BlockSpec(memory_space=pl.ANY)


def _exchange(arrays, kinds, name, split=0, into=None):
    xc = _Exchange(zip(arrays, kinds), split)
    n = xc.n
    into = into or {}
    ni = len(into)

    def body(*refs):
        ins, outs, sems = refs[:n], refs[n + ni:2 * n + ni], refs[2 * n + ni:]
        xc.start(ins, outs, sems)
        xc.forward(ins, outs, sems)
        xc.wait(ins, outs, sems)

    return pl.pallas_call(
        body, name=name, out_shape=xc.out_shape, in_specs=[_ANY] * (n + ni), out_specs=tuple([_ANY] * n),
        scratch_shapes=xc.scratch, input_output_aliases={n + t: a for t, a in enumerate(into)},
    )(*arrays, *into.values())


def _matmul_nn(a, b, name, out_dtype=F32):
    m, kk = a.shape
    n = b.shape[1]
    tm = m if m * kk * a.dtype.itemsize <= (12 << 20) else _tile(m, (1088, 1024, 640, 512, 256, 128))
    tn = _tile(n, (512, 256, 128))

    def body(a_ref, b_ref, o_ref):
        o_ref[...] = _mm(a_ref[...], b_ref[...]).astype(o_ref.dtype)

    return pl.pallas_call(
        body, name=name, out_shape=_sds((m, n), out_dtype), grid=(n // tn, m // tm),
        in_specs=[pl.BlockSpec((tm, kk), lambda j, i: (i, 0)), pl.BlockSpec((kk, tn), lambda j, i: (0, j))],
        out_specs=pl.BlockSpec((tm, tn), lambda j, i: (i, j)),
        compiler_params=_params(("parallel", "parallel")),
    )(a, b)


def _dh_matmul(dp_rest, dp_qkv, dpab, w_rest, w_qkv, w_ab, xc, xc_arrays, xc_into):
    lt = dp_rest.shape[0]
    tm = _tile(lt, (1088, 1024, 640, 512, 256, 128))
    nr, nq = dp_rest.shape[1] // D, dp_qkv.shape[1] // D
    nx, ni = xc.n, len(xc_into)
    ni_steps = lt // tm

    def body(*refs):
        dr_ref, dq_ref, ab_ref, wr_ref, wq_ref, wab_ref = refs[:6]
        x_in = refs[6:6 + nx]
        o_ref = refs[6 + nx + ni]
        x_out = refs[7 + nx + ni:7 + 2 * nx + ni]
        sems = refs[7 + 2 * nx + ni:]
        i = pl.program_id(0)
        k = pl.program_id(1)

        @pl.when((i == 0) & (k == 0))
        def _():
            xc.start(x_in, x_out, sems)

        @pl.when(k == 0)
        def _():
            o_ref[...] = _mm(ab_ref[...], wab_ref[...], NT)

        @pl.when(k < nr)
        def _():
            o_ref[...] += _mm(dr_ref[...], wr_ref[...], NT)

        @pl.when(k >= nr)
        def _():
            o_ref[...] += _mm(dq_ref[...], wq_ref[...], NT)

        @pl.when((i == ni_steps - 1) & (k == nr + nq - 1))
        def _():
            xc.wait(x_in, x_out, sems)

    rk = lambda k: jnp.minimum(k, nr - 1)
    qk = lambda k: jnp.maximum(k - nr, 0)
    return pl.pallas_call(
        body, name="dh_matmul", out_shape=(_sds((lt, D)),) + xc.out_shape, grid=(lt // tm, nr + nq),
        in_specs=[pl.BlockSpec((tm, D), lambda i, k: (i, rk(k))), pl.BlockSpec((tm, D), lambda i, k: (i, qk(k))),
                  pl.BlockSpec((tm, LANE), lambda i, k: (i, 0)),
                  pl.BlockSpec((D, D), lambda i, k: (0, rk(k))), pl.BlockSpec((D, D), lambda i, k: (0, qk(k))),
                  _full((D, LANE))] + [_ANY] * (nx + ni),
        out_specs=(pl.BlockSpec((tm, D), lambda i, k: (i, 0)),) + tuple([_ANY] * nx),
        scratch_shapes=xc.scratch, input_output_aliases={6 + nx + t: 1 + a for t, a in enumerate(xc_into)},
        compiler_params=_params(("arbitrary", "arbitrary")),
    )(dp_rest, dp_qkv, dpab, w_rest, w_qkv, w_ab, *xc_arrays, *xc_into.values())


def _modulation(cc, w_mod_g, b_mod):
    ws = w_mod_g.shape[2]

    def body(c_ref, w_ref, b_ref, o_ref):
        s, _ = _silu_g(c_ref[...])
        o_ref[...] = _mm(s, w_ref[0]) + b_ref[...]

    return pl.pallas_call(
        body, name="modulation", out_shape=_sds((8, 3 * D)), grid=(NDEV,),
        in_specs=[_full((8, D)), pl.BlockSpec((1, D, ws), lambda j: (j, 0, 0)), pl.BlockSpec((1, ws), lambda j: (0, j))],
        out_specs=pl.BlockSpec((8, ws), lambda j: (0, j)),
        compiler_params=_params(("parallel",)),
    )(cc, w_mod_g, b_mod)


def _prenorm(ctx, x, mods, g_pre):
    lc = ctx.shape[0]
    lt = lc + x.shape[0]
    tm = _tile(lc, (256, 128))
    nct = lc // tm

    def body(c_ref, x_ref, m_ref, g_ref, o_ref, ot_ref):
        is_ctx = pl.program_id(0) < nct
        x = jnp.where(is_ctx, c_ref[...], x_ref[...])
        shift = jnp.where(is_ctx, m_ref[1:2, 0:D], m_ref[0:1, 0:D])
        scale = jnp.where(is_ctx, m_ref[1:2, D:2 * D], m_ref[0:1, D:2 * D])
        r = lax.rsqrt(jnp.mean(x * x, axis=-1, keepdims=True) + EPS)
        h = ((x * r * g_ref[...]) * (1.0 + scale) + shift).astype(o_ref.dtype)
        o_ref[...] = h
        ot_ref[...] = h.T

    return pl.pallas_call(
        body, name="prenorm", out_shape=(_sds((lt, D), _BF), _sds((D, lt), _BF)), grid=(lt // tm,),
        in_specs=[pl.BlockSpec((tm, D), lambda i: (jnp.minimum(i, nct - 1), 0)),
                  pl.BlockSpec((tm, D), lambda i: (jnp.maximum(i - nct, 0), 0)), _full((8, 3 * D)), _full((1, D))],
        out_specs=(pl.BlockSpec((tm, D), lambda i: (i, 0)), pl.BlockSpec((D, tm), lambda i: (0, i))),
        compiler_params=_params(("parallel",)),
    )(ctx, x, mods, g_pre)


def _prenorm_bwd(ctx, x, dh, dy, mods, g_pre):
    lc = ctx.shape[0]
    lt = lc + x.shape[0]
    tm = _tile(lc, (256, 128))
    nct = lc // tm
    nl = (lt - lc) // tm

    def body(c_ref, x_ref, dh_ref, dy_ref, m_ref, g_ref, gx_ref, vec_ref):
        i = pl.program_id(0)

        @pl.when(i == 0)
        def _():
            vec_ref[...] = jnp.zeros_like(vec_ref)

        is_ctx = i < nct
        x = jnp.where(is_ctx, c_ref[...], x_ref[...])
        dh = dh_ref[...]
        g = g_ref[...]
        scale = jnp.where(is_ctx, m_ref[1:2, D:2 * D], m_ref[0:1, D:2 * D])
        r = lax.rsqrt(jnp.mean(x * x, axis=-1, keepdims=True) + EPS)
        n = x * r
        hn = n * g
        dsh = jnp.sum(dh, axis=0, keepdims=True)
        dsc = jnp.sum(dh * hn, axis=0, keepdims=True)
        dhn = dh * (1.0 + scale)
        vec_ref[4:5, :] += jnp.sum(dhn * n, axis=0, keepdims=True)
        dn = dhn * g
        dx = r * (dn - n * jnp.mean(dn * n, axis=-1, keepdims=True))

        @pl.when(is_ctx)
        def _():
            vec_ref[2:3, :] += dsh
            vec_ref[3:4, :] += dsc

        @pl.when(jnp.logical_not(is_ctx))
        def _():
            vec_ref[0:1, :] += dsh
            vec_ref[1:2, :] += dsc
            gx_ref[...] = dy_ref[...] + dx

    xrow = lambda i: (jnp.maximum(i - nct, 0), 0)
    return pl.pallas_call(
        body, name="prenorm_bwd", out_shape=(_sds((nl * tm, D)), _sds((8, D))), grid=(lt // tm,),
        in_specs=[pl.BlockSpec((tm, D), lambda i: (jnp.minimum(i, nct - 1), 0)), pl.BlockSpec((tm, D), xrow),
                  pl.BlockSpec((tm, D), lambda i: (i, 0)), pl.BlockSpec((tm, D), xrow), _full((8, 3 * D)), _full((1, D))],
        out_specs=(pl.BlockSpec((tm, D), xrow), _full((8, D))),
        compiler_params=_params(("arbitrary",)),
    )(ctx, x, dh, dy, mods, g_pre)


def _conv_parts(x, w, lc):
    lt = x.shape[0]
    row = lax.broadcasted_iota(jnp.int32, x.shape, 0)
    first = (row == 0) | (row == lc)
    last = (row == lc - 1) | (row == lt - 1)
    xp = jnp.where(first, 0.0, pltpu.roll(x, 1, 0))
    xn = jnp.where(last, 0.0, pltpu.roll(x, lt - 1, 0))
    y = w[0:1, :] * xp + w[1:2, :] * x + w[2:3, :] * xn
    return xp, xn, y, first, last


def _qkv_fwd(p, w_conv, lc):
    lt = p.shape[0]

    def body(p_ref, w_ref, o_ref):
        _, _, y, _, _ = _conv_parts(p_ref[...], w_ref[...], lc)
        s, _ = _silu_g(y)
        rs = lax.rsqrt(jnp.sum(s * s, axis=-1, keepdims=True) + EPS)
        o_ref[...] = s * jnp.where(pl.program_id(0) < 2 * NH, rs, 1.0)

    return pl.pallas_call(
        body, name="qkv_fwd", out_shape=_sds((lt, 3 * D)), grid=(3 * NH,),
        in_specs=[pl.BlockSpec((lt, DH), lambda j: (0, j)), pl.BlockSpec((3, DH), lambda j: (0, j))],
        out_specs=pl.BlockSpec((lt, DH), lambda j: (0, j)),
        compiler_params=_params(("parallel",)),
    )(p, w_conv)


def _qkv_bwd(p, w_conv, dqkv_f, dqkv_b, lc):
    lt = p.shape[0]

    def body(p_ref, w_ref, df_ref, db_ref, dp_ref, dw_ref):
        w = w_ref[...]
        xp, xn, y, first, last = _conv_parts(p_ref[...], w, lc)
        s, ds_dy = _silu_g(y)
        dn = df_ref[...] + db_ref[...]
        rs = lax.rsqrt(jnp.sum(s * s, axis=-1, keepdims=True) + EPS)
        nrm = s * rs
        ds_n = rs * (dn - nrm * jnp.sum(dn * nrm, axis=-1, keepdims=True))
        ds = jnp.where(pl.program_id(0) < 2 * NH, ds_n, dn)
        dy = ds * ds_dy
        dw_ref[0:1, :] = jnp.sum(dy * xp, axis=0, keepdims=True)
        dw_ref[1:2, :] = jnp.sum(dy * p_ref[...], axis=0, keepdims=True)
        dw_ref[2:3, :] = jnp.sum(dy * xn, axis=0, keepdims=True)
        dyn = jnp.where(last, 0.0, pltpu.roll(dy, lt - 1, 0))
        dyp = jnp.where(first, 0.0, pltpu.roll(dy, 1, 0))
        dp_ref[...] = (w[1:2, :] * dy + w[0:1, :] * dyn + w[2:3, :] * dyp).astype(dp_ref.dtype)

    return pl.pallas_call(
        body, name="qkv_bwd", out_shape=(_sds((lt, 3 * D), _BF), _sds((3, 3 * D))), grid=(3 * NH,),
        in_specs=[pl.BlockSpec((lt, DH), lambda j: (0, j)), pl.BlockSpec((3, DH), lambda j: (0, j)),
                  pl.BlockSpec((lt, DH), lambda j: (0, j)), pl.BlockSpec((lt, DH), lambda j: (0, j))],
        out_specs=(pl.BlockSpec((lt, DH), lambda j: (0, j)), pl.BlockSpec((3, DH), lambda j: (0, j))),
        compiler_params=_params(("parallel",)),
    )(p, w_conv, dqkv_f, dqkv_b)


def _masks(d):
    ri = lax.broadcasted_iota(jnp.int32, (CH, CH), 0)
    ci = lax.broadcasted_iota(jnp.int32, (CH, CH), 1)
    incl = (ri >= ci) if d == 0 else (ri <= ci)
    strict = (ri > ci) if d == 0 else (ri < ci)
    incl_t = (ri <= ci) if d == 0 else (ri >= ci)
    return incl, strict, incl_t, ri == ci


def _decays(d, ab, abt, alog_r, dtb_r, alog_c, dtb_c, incl, incl_t):
    g_full = -jnp.exp(alog_r) * _softplus(ab + dtb_r)
    beta_full = _sigmoid(ab)
    gc_full = _mmh(incl.astype(F32), g_full)
    gl_full = jnp.sum(g_full, axis=0, keepdims=True)
    gt_full = -jnp.exp(alog_c) * _softplus(abt + dtb_c)
    gct = _mmh(gt_full, incl_t.astype(F32))
    return g_full, beta_full, gc_full, gl_full, gt_full, gct


def _lane_onehot(idx, n=LANE):
    return (lax.broadcasted_iota(jnp.int32, (1, n), 1) == idx).astype(F32)


def _head_scalars(d, h, beta_full, gc_full, gl_full, gct):
    idx = d * NH + h
    oh = _lane_onehot(idx)
    gcol = jnp.sum(gc_full * oh, axis=-1, keepdims=True)
    bcol = jnp.sum(beta_full * _lane_onehot(2 * NH + idx), axis=-1, keepdims=True)
    gl = jnp.sum(gl_full * oh, axis=-1, keepdims=True)
    grow = gct[idx:idx + 1, :]
    return gcol, grow, bcol, gl


def _lockstep(gens):
    live = list(gens)
    while live:
        nxt = []
        for g in live:
            try:
                next(g)
                nxt.append(g)
            except StopIteration:
                pass
        live = nxt


def _chunk_local(qh, kh, vh, gcol, grow, bcol, gl, incl, strict):
    decay = jnp.where(incl, jnp.exp(gcol - grow), 0.0)
    kb = kh * bcol
    qs = qh * (DH ** -0.5)
    both = _mm(jnp.concatenate([kb, qs], axis=0), kh, NT)
    a = jnp.where(strict, both[:CH] * decay, 0.0)
    egc = jnp.exp(gcol)
    rhs_u = vh * bcol
    rhs_w = kb * egc
    attn = jnp.where(incl, both[CH:] * decay, 0.0)
    etail = jnp.exp(gl - gcol)
    return decay, kb, a, egc, rhs_u, rhs_w, qs, attn, etail


def _scan_specs(lt, lc, bwd_pass):
    assert lt % (SUB * CH) == 0 and lc % (SUB * CH) == 0
    nch = lt // (SUB * CH)
    ncc = lc // (SUB * CH)
    if not bwd_pass:
        cf = lambda s: s
        cb = lambda s: jnp.where(s < ncc, ncc - 1 - s, nch + ncc - 1 - s)
    else:
        cf = lambda s: nch - 1 - s
        cb = lambda s: jnp.where(s < nch - ncc, ncc + s, s - (nch - ncc))
    return nch, cf, cb


def _gdn_fwd(qkv, pab, abt, alog_r, dtb_r, alog_c, dtb_c, lc, xc, xc_arrays):
    lt = qkv.shape[0]
    nch, cf, cb = _scan_specs(lt, lc, False)
    nx = xc.n

    def body(*refs):
        qf, kf, vf, abf, abtf, qb, kb_, vb, abb, abtb, ar, dr, ac, dc = refs[:14]
        x_in = refs[14:14 + nx]
        of_ref, ob_ref, sf_ref, sb_ref, tf_ref, tb_ref = refs[14 + nx:20 + nx]
        x_out = refs[20 + nx:20 + 2 * nx]
        s_scr = refs[20 + 2 * nx]
        sems = refs[21 + 2 * nx:]

        @pl.when(pl.program_id(0) == 0)
        def _():
            s_scr[...] = jnp.zeros_like(s_scr)
            xc.start(x_in, x_out, sems)

        def chain(d, h, c, late, q_r, k_r, v_r, o_ref, sh_ref, th_ref, masks, decs):
            incl, strict, _, eye = masks
            sl = slice(h * DH, (h + 1) * DH)
            rows = slice(c * CH, (c + 1) * CH)
            qh, kh, vh = q_r[rows, sl], k_r[rows, sl], v_r[rows, sl]
            gcol, grow, bcol, gl = _head_scalars(d, h, *decs)
            _, _, a, egc, rhs_u, rhs_w, qs, attn, etail = _chunk_local(qh, kh, vh, gcol, grow, bcol, gl, incl, strict)
            yield
            n = -a
            t = jnp.where(eye, 1.0, 0.0) + n
            p = _mm3(n, n)
            yield
            for _ in range(4):
                r = _mm3(jnp.concatenate([t, p], axis=0), p)
                yield
                t = t + r[:CH]
                p = r[CH:]
            t = t + _mm3(t, p)
            yield
            for _ in range(3 * late):
                yield
            sol = _mm3(t, jnp.concatenate([rhs_u, rhs_w], axis=1))
            u, w = sol[:, :DH], sol[:, DH:]
            s = s_scr[d, h]
            sh_ref[c, h] = s
            th_ref[c, h] = t
            yield
            ws = _mm(jnp.concatenate([w, qs * egc], axis=0), s)
            yield
            v_new = u - ws[:CH]
            o_ref[rows, sl] = ws[CH:] + _mm(attn, v_new)
            s_scr[d, h] = s * jnp.exp(gl) + _mm(kh * etail, v_new, TN)

        chains = []
        for d, (q_r, k_r, v_r, ab_r, abt_r, o_ref, sh_ref, th_ref) in enumerate(
                ((qf, kf, vf, abf, abtf, of_ref, sf_ref, tf_ref), (qb, kb_, vb, abb, abtb, ob_ref, sb_ref, tb_ref))):
            masks = _masks(d)
            for pos, c in enumerate(range(SUB) if d == 0 else reversed(range(SUB))):
                _, beta_full, gc_full, gl_full, _, gct = _decays(
                    d, ab_r[c * CH:(c + 1) * CH, :], abt_r[c], ar[...], dr[...], ac[...], dc[...], masks[0], masks[2])
                for h in range(NH):
                    chains.append(chain(d, h, c, pos, q_r, k_r, v_r, o_ref, sh_ref, th_ref, masks,
                                        (beta_full, gc_full, gl_full, gct)))
        _lockstep(chains)

        @pl.when(pl.program_id(0) == nch // 2)
        def _():
            xc.forward(x_in, x_out, sems)

        @pl.when(pl.program_id(0) == nch - 1)
        def _():
            xc.wait(x_in, x_out, sems)

    def row(c, col):
        return pl.BlockSpec((SUB * CH, D), lambda s: (c(s), col))

    def chunk_in(c):
        return [row(c, 0), row(c, 1), row(c, 2), pl.BlockSpec((SUB * CH, LANE), lambda s: (c(s), 0)),
                pl.BlockSpec((SUB, 4 * NH, CH), lambda s: (c(s), 0, 0))]

    def hist(c, n):
        return pl.BlockSpec((SUB, NH, n, n), lambda s: (c(s), 0, 0, 0))

    small = [_full((1, LANE)), _full((1, LANE)), _full((4 * NH, 1)), _full((4 * NH, 1))]
    return pl.pallas_call(
        body, name="gdn_fwd", grid=(nch,),
        out_shape=(_sds((lt, D)), _sds((lt, D)), _sds((lt // CH, NH, DH, DH)), _sds((lt // CH, NH, DH, DH)),
                   _sds((lt // CH, NH, CH, CH)), _sds((lt // CH, NH, CH, CH))) + xc.out_shape,
        in_specs=chunk_in(cf) + chunk_in(cb) + small + [_ANY] * nx,
        out_specs=(pl.BlockSpec((SUB * CH, D), lambda s: (cf(s), 0)), pl.BlockSpec((SUB * CH, D), lambda s: (cb(s), 0)),
                   hist(cf, DH), hist(cb, DH), hist(cf, CH), hist(cb, CH)) + tuple([_ANY] * nx),
        scratch_shapes=[pltpu.VMEM((2, NH, DH, DH), F32)] + xc.scratch,
        compiler_params=_params(("arbitrary",)),
    )(qkv, qkv, qkv, pab, abt, qkv, qkv, qkv, pab, abt, alog_r, dtb_r, alog_c, dtb_c, *xc_arrays)


def _gdn_bwd(qkv, pab, abt, alog_r, dtb_r, alog_c, dtb_c, s_f, s_b, t_f, t_b, do, lc, xc, xc_arrays):
    lt = qkv.shape[0]
    nch, cf, cb = _scan_specs(lt, lc, True)
    nx = xc.n

    def body(*refs):
        qf, kf, vf, abf, abtf, sf_ref, tf_ref, dof, qb, kb_, vb, abb, abtb, sb_ref, tb_ref, dob, ar, dr, ac, dc = refs[:20]
        x_in = refs[20:20 + nx]
        dqf_ref, dqb_ref, dcf_ref, dcb_ref, drf_ref, drb_ref, vcol_ref, vrow_ref = refs[20 + nx:28 + nx]
        x_out = refs[28 + nx:28 + 2 * nx]
        ds_scr = refs[28 + 2 * nx]
        sems = refs[29 + 2 * nx:]

        @pl.when(pl.program_id(0) == 0)
        def _():
            ds_scr[...] = jnp.zeros_like(ds_scr)
            vcol_ref[...] = jnp.zeros_like(vcol_ref)
            vrow_ref[...] = jnp.zeros_like(vrow_ref)
            xc.start(x_in, x_out, sems)

        alog_r_, dtb_r_, alog_c_, dtb_c_ = ar[...], dr[...], ac[...], dc[...]
        lane2 = lax.broadcasted_iota(jnp.int32, (1, LANE), 1)
        acc = {}

        def chain(d, h, c, late, q_r, k_r, v_r, sh_ref, th_ref, do_r, dq_ref, masks, decs):
            incl, strict, _, _ = masks
            idx = d * NH + h
            sl = slice(h * DH, (h + 1) * DH)
            rows = slice(c * CH, (c + 1) * CH)
            qh, kh, vh = q_r[rows, sl], k_r[rows, sl], v_r[rows, sl]
            doh = do_r[rows, sl]
            gcol, grow, bcol, gl = _head_scalars(d, h, *decs)
            decay, kb, a, egc, rhs_u, rhs_w, qs, attn, etail = _chunk_local(qh, kh, vh, gcol, grow, bcol, gl, incl, strict)
            t = th_ref[c, h]
            s = sh_ref[c, h]
            sol = _mm3(t, jnp.concatenate([rhs_u, rhs_w], axis=1))
            u, w = sol[:, :DH], sol[:, DH:]
            q_dec = qs * egc
            k_tail = kh * etail
            egl = jnp.exp(gl)
            yield
            for _ in range(2 * late):
                yield
            ds_new = ds_scr[d, h]
            dv_new = _mm(attn, doh, TN) + _mm(k_tail, ds_new)
            dgl = jnp.sum(jnp.sum(ds_new * s, axis=0, keepdims=True), axis=-1, keepdims=True) * egl
            yield
            v_new = u - _mm(w, s)
            by_s = _mm(jnp.concatenate([doh, dv_new], axis=0), s, NT)
            dq_dec, dw = by_s[:CH], -by_s[CH:]
            ds_scr[d, h] = ds_new * egl + _mm(q_dec, doh, TN) - _mm(w, dv_new, TN)
            yield
            dattn = jnp.where(incl, _mm(doh, v_new, NT), 0.0)
            dk_tail = _mm(v_new, ds_new, NT)
            dr = _mm3(t, jnp.concatenate([dv_new, dw], axis=1), TN)
            dr_u, dr_w = dr[:, :DH], dr[:, DH:]
            yield
            da = -jnp.where(strict, _mm3(dr, sol, NT), 0.0)
            nq = dattn * decay
            dk = _mm(nq, qs, TN)
            yield
            dv = dr_u * bcol
            dbeta = jnp.sum(dr_u * vh, axis=-1, keepdims=True)
            dgc = jnp.sum(dr_w * rhs_w, axis=-1, keepdims=True)
            m = da * decay
            by_k = _mm(jnp.concatenate([m, nq], axis=0), kh)
            dkb = dr_w * egc + by_k[:CH]
            dqs = by_k[CH:] + dq_dec * egc
            dk = dk + _mm(m, kb, TN)
            pq = da * a + dattn * attn
            dgc = dgc + jnp.sum(pq, axis=-1, keepdims=True) + jnp.sum(dq_dec * q_dec, axis=-1, keepdims=True)
            dgr = -jnp.sum(pq, axis=0, keepdims=True)
            tt = jnp.sum(dk_tail * k_tail, axis=-1, keepdims=True)
            dk = dk + dk_tail * etail + dkb * bcol
            dgc = dgc - tt
            dgl = dgl + jnp.sum(tt, axis=0, keepdims=True)
            dbeta = dbeta + jnp.sum(dkb * kh, axis=-1, keepdims=True)
            dq_ref[rows, sl] = dqs * (DH ** -0.5)
            dq_ref[rows, D + h * DH:D + (h + 1) * DH] = dk
            dq_ref[rows, 2 * D + h * DH:2 * D + (h + 1) * DH] = dv
            acc.setdefault((d, c), []).append((idx, dgc, dgl, dbeta, dgr))

        dirs = ((qf, kf, vf, abf, abtf, sf_ref, tf_ref, dof, dqf_ref, dcf_ref, drf_ref),
                (qb, kb_, vb, abb, abtb, sb_ref, tb_ref, dob, dqb_ref, dcb_ref, drb_ref))
        chains, ctx_d = [], {}
        for d, (q_r, k_r, v_r, ab_r, abt_r, sh_ref, th_ref, do_r, dq_ref, _, _) in enumerate(dirs):
            masks = _masks(d)
            for pos, c in enumerate(reversed(range(SUB)) if d == 0 else range(SUB)):
                ab, abt = ab_r[c * CH:(c + 1) * CH, :], abt_r[c]
                g_full, beta_full, gc_full, gl_full, gt_full, gct = _decays(
                    d, ab, abt, alog_r_, dtb_r_, alog_c_, dtb_c_, masks[0], masks[2])
                ctx_d[(d, c)] = (masks, ab, abt, g_full, beta_full, gt_full)
                for h in range(NH):
                    chains.append(chain(d, h, c, pos, q_r, k_r, v_r, sh_ref, th_ref, do_r, dq_ref, masks,
                                        (beta_full, gc_full, gl_full, gct)))
        _lockstep(chains)
        for d, c in sorted(ctx_d):
            (incl, _, incl_t, _), ab, abt, g_full, beta_full, gt_full = ctx_d[(d, c)]
            dcol_ref, drow_ref = dirs[d][9], dirs[d][10]
            dgc_col = jnp.zeros((CH, LANE), F32)
            dgl_row = jnp.zeros((1, LANE), F32)
            dbeta_col = jnp.zeros((CH, LANE), F32)
            dgc_row = jnp.zeros((4 * NH, CH), F32)
            for idx, dgc, dgl, dbeta, dgr in acc[(d, c)]:
                oh = _lane_onehot(idx)
                dgc_col = dgc_col + dgc * oh
                dgl_row = dgl_row + dgl * oh
                dbeta_col = dbeta_col + dbeta * _lane_onehot(2 * NH + idx)
                ohc = (lax.broadcasted_iota(jnp.int32, (4 * NH, 1), 0) == idx).astype(F32)
                dgc_row = dgc_row + ohc * dgr
            dg_col = _mmh(incl_t.astype(F32), dgc_col) + dgl_row
            dg_row = _mmh(dgc_row, incl.astype(F32))
            sg_col = _sigmoid(ab + dtb_r_)
            da_col = dg_col * (-jnp.exp(alog_r_)) * sg_col
            dcol_ref[c * CH:(c + 1) * CH, :] = da_col + dbeta_col * beta_full * (1.0 - beta_full)
            da_row = dg_row * (-jnp.exp(alog_c_)) * _sigmoid(abt + dtb_c_)
            drow_ref[c] = da_row
            vcol_ref[0:1, :] += jnp.sum(dg_col * g_full, axis=0, keepdims=True)
            vcol_ref[1:2, :] += jnp.sum(da_col, axis=0, keepdims=True)
            rl = jnp.sum(dg_row * gt_full, axis=-1, keepdims=True)
            rd = jnp.sum(da_row, axis=-1, keepdims=True)
            vrow_ref[...] += jnp.where(lane2 == 0, rl, 0.0) + jnp.where(lane2 == 1, rd, 0.0)

        @pl.when(pl.program_id(0) == nch // 2)
        def _():
            xc.forward(x_in, x_out, sems)

        @pl.when(pl.program_id(0) == nch - 1)
        def _():
            xc.wait(x_in, x_out, sems)

    def row(c, col):
        return pl.BlockSpec((SUB * CH, D), lambda s: (c(s), col))

    def hist(c, n):
        return pl.BlockSpec((SUB, NH, n, n), lambda s: (c(s), 0, 0, 0))

    def chunk_in(c):
        return [row(c, 0), row(c, 1), row(c, 2), pl.BlockSpec((SUB * CH, LANE), lambda s: (c(s), 0)),
                pl.BlockSpec((SUB, 4 * NH, CH), lambda s: (c(s), 0, 0)), hist(c, DH), hist(c, CH), row(c, 0)]

    small = [_full((1, LANE)), _full((1, LANE)), _full((4 * NH, 1)), _full((4 * NH, 1))]
    return pl.pallas_call(
        body, name="gdn_bwd", grid=(nch,),
        out_shape=(_sds((lt, 3 * D)), _sds((lt, 3 * D)), _sds((lt, LANE)), _sds((lt, LANE)),
                   _sds((lt // CH, 4 * NH, CH)), _sds((lt // CH, 4 * NH, CH)), _sds((8, LANE)), _sds((4 * NH, LANE))) + xc.out_shape,
        in_specs=chunk_in(cf) + chunk_in(cb) + small + [_ANY] * nx,
        out_specs=(pl.BlockSpec((SUB * CH, 3 * D), lambda s: (cf(s), 0)), pl.BlockSpec((SUB * CH, 3 * D), lambda s: (cb(s), 0)),
                   pl.BlockSpec((SUB * CH, LANE), lambda s: (cf(s), 0)), pl.BlockSpec((SUB * CH, LANE), lambda s: (cb(s), 0)),
                   pl.BlockSpec((SUB, 4 * NH, CH), lambda s: (cf(s), 0, 0)), pl.BlockSpec((SUB, 4 * NH, CH), lambda s: (cb(s), 0, 0)),
                   _full((8, LANE)), _full((4 * NH, LANE))) + tuple([_ANY] * nx),
        scratch_shapes=[pltpu.VMEM((2, NH, DH, DH), F32)] + xc.scratch,
        compiler_params=_params(("arbitrary",)),
    )(qkv, qkv, qkv, pab, abt, s_f, t_f, do, qkv, qkv, qkv, pab, abt, s_b, t_b, do, alog_r, dtb_r, alog_c, dtb_c,
      *xc_arrays)


def _post(p, o_f, o_b, x, tgt, w_pa, w_pb, w_out, w_sp, w_spt, b_spb, ln_g, ln_b, g_on, g_post, gate_x, lc):
    lt = p.shape[0]
    l = x.shape[0]
    tm = GC
    nct = lc // tm

    def body(p_ref, of_ref, ob_ref, x_ref, t_ref, wpa, wpb, wout, wsp, wspt, bspb, lng_ref, lnb_ref, gon_ref, gpost_ref, gate_ref,
             dp_ref, do_ref, dy_ref, ya_ref, yb_ref, mg_ref, da_ref, db_ref, dout_ref, dwsp_ref, dbsp_ref, vec_ref):
        i = pl.program_id(0)

        @pl.when(i == 0)
        def _():
            dwsp_ref[...] = jnp.zeros_like(dwsp_ref)
            dbsp_ref[...] = jnp.zeros_like(dbsp_ref)
            vec_ref[...] = jnp.zeros_like(vec_ref)

        @pl.when(i < nct)
        def _():
            dp_ref[...] = jnp.zeros_like(dp_ref)
            do_ref[...] = jnp.zeros_like(do_ref)

        @pl.when(i >= nct)
        def _():
            lng, lnb, gon, gpost, gate = lng_ref[...], lnb_ref[...], gon_ref[...], gpost_ref[...], gate_ref[...]
            zb, ua, va, za, ga, gb = [p_ref[:, j * D:(j + 1) * D] for j in range(6)]
            o = of_ref[...] + ob_ref[...]
            szb, dszb = _silu_g(zb)
            nh_l, r_l = [], []
            for h in range(NH):
                oh = o[:, h * DH:(h + 1) * DH]
                r = lax.rsqrt(jnp.mean(oh * oh, axis=-1, keepdims=True) + EPS)
                nh_l.append(oh * r)
                r_l.append(r)
            nrm_b = jnp.concatenate(nh_l, axis=-1)
            gon_t = jnp.concatenate([gon] * NH, axis=-1)
            y_b = nrm_b * gon_t * szb
            u, du_dua = _gelu_g(ua)
            gv, dgv_dva = _gelu_g(va)
            xc = gv - jnp.mean(gv, axis=-1, keepdims=True)
            rs_ln = lax.rsqrt(jnp.mean(xc * xc, axis=-1, keepdims=True) + EPS)
            vhat = xc * rs_ln
            v = vhat * lng + lnb
            s_sp = jnp.concatenate(
                [_mm(wsp[g], v[:, g * DH:(g + 1) * DH]) + bspb[g] for g in range(NH)], axis=-1)
            sza, dsza = _silu_g(za)
            y_a = u * s_sp * sza
            a_pr = _mm(y_a, wpa[...])
            b_pr = _mm(y_b, wpb[...])
            sga = _sigmoid(ga)
            sgb = _sigmoid(gb)
            merged = sga * a_pr + sgb * b_pr
            out = _mm(merged, wout[...])
            rs_o = lax.rsqrt(jnp.mean(out * out, axis=-1, keepdims=True) + EPS)
            n_o = out * rs_o
            rr = n_o * gpost
            diff = x_ref[...] + gate * rr - t_ref[...]
            vec_ref[5:6, :] += jnp.sum(diff * diff, axis=0, keepdims=True)
            dy = diff * (1.0 / D)
            dy_ref[...] = dy
            vec_ref[0:1, :] += jnp.sum(dy * rr, axis=0, keepdims=True)
            dr = dy * gate
            vec_ref[1:2, :] += jnp.sum(dr * n_o, axis=0, keepdims=True)
            dn_o = dr * gpost
            dout = rs_o * (dn_o - n_o * jnp.mean(dn_o * n_o, axis=-1, keepdims=True))
            dmerged = _mm(dout, wout[...], NT)
            d_a = dmerged * sga
            d_b = dmerged * sgb
            dga = dmerged * a_pr * sga * (1.0 - sga)
            dgb = dmerged * b_pr * sgb * (1.0 - sgb)
            dy_a = _mm(d_a, wpa[...], NT)
            dy_b = _mm(d_b, wpb[...], NT)
            ya_ref[...] = y_a.astype(ya_ref.dtype).T
            yb_ref[...] = y_b.astype(yb_ref.dtype).T
            mg_ref[...] = merged.astype(mg_ref.dtype).T
            da_ref[...] = d_a.astype(da_ref.dtype)
            db_ref[...] = d_b.astype(db_ref.dtype)
            dout_ref[...] = dout.astype(dout_ref.dtype)
            dua = dy_a * s_sp * sza * du_dua
            ds_sp = dy_a * u * sza
            dza = dy_a * u * s_sp * dsza
            dv_l = []
            for g in range(NH):
                ds_g = ds_sp[:, g * DH:(g + 1) * DH]
                dv_l.append(_mm(wspt[g], ds_g))
                dwsp_ref[g] += _mm(ds_g, v[:, g * DH:(g + 1) * DH], NT)
                dbsp_ref[g] += ds_g
            dv = jnp.concatenate(dv_l, axis=-1)
            vec_ref[2:3, :] += jnp.sum(dv * vhat, axis=0, keepdims=True)
            vec_ref[3:4, :] += jnp.sum(dv, axis=0, keepdims=True)
            dvh = dv * lng
            dgv = rs_ln * (dvh - jnp.mean(dvh, axis=-1, keepdims=True) - vhat * jnp.mean(dvh * vhat, axis=-1, keepdims=True))
            dva = dgv * dgv_dva
            dzb = dy_b * nrm_b * gon_t * dszb
            dgon_full = jnp.sum(dy_b * nrm_b * szb, axis=0, keepdims=True)
            dgon = dgon_full[:, 0:DH]
            for h in range(1, NH):
                dgon = dgon + dgon_full[:, h * DH:(h + 1) * DH]
            vec_ref[4:5, 0:DH] += dgon
            dnb = dy_b * gon_t * szb
            do_l = []
            for h in range(NH):
                sl = slice(h * DH, (h + 1) * DH)
                dn_h = dnb[:, sl]
                do_l.append(r_l[h] * (dn_h - nh_l[h] * jnp.mean(dn_h * nh_l[h], axis=-1, keepdims=True)))
            do_ref[...] = jnp.concatenate(do_l, axis=-1)
            for j, val in enumerate((dzb, dua, dva, dza, dga, dgb)):
                dp_ref[:, j * D:(j + 1) * D] = val.astype(dp_ref.dtype)

    xrow = lambda i: (jnp.maximum(i - nct, 0), 0)
    wspec = _full((D, D))
    gspec = _full((NH, GC, GC))
    vspec = _full((1, D))
    bf_out = _sds((l, D), _BF)
    bf_out_t = _sds((D, l), _BF)
    xcol = lambda i: (0, jnp.maximum(i - nct, 0))
    return pl.pallas_call(
        body, name="post", grid=(lt // tm,),
        out_shape=(_sds((lt, NREST), _BF), _sds((lt, D)), _sds((l, D)), bf_out_t, bf_out_t, bf_out_t, bf_out, bf_out, bf_out,
                   _sds((NH, GC, GC)), _sds((NH, GC, GC)), _sds((8, D))),
        in_specs=[pl.BlockSpec((tm, NREST), lambda i: (i, 0)), pl.BlockSpec((tm, D), lambda i: (i, 0)),
                  pl.BlockSpec((tm, D), lambda i: (i, 0)), pl.BlockSpec((tm, D), xrow), pl.BlockSpec((tm, D), xrow),
                  wspec, wspec, wspec, gspec, gspec, gspec, vspec, vspec, _full((1, DH)), vspec, vspec],
        out_specs=(pl.BlockSpec((tm, NREST), lambda i: (i, 0)), pl.BlockSpec((tm, D), lambda i: (i, 0)),
                   pl.BlockSpec((tm, D), xrow), pl.BlockSpec((D, tm), xcol), pl.BlockSpec((D, tm), xcol),
                   pl.BlockSpec((D, tm), xcol), pl.BlockSpec((tm, D), xrow), pl.BlockSpec((tm, D), xrow),
                   pl.BlockSpec((tm, D), xrow), gspec, gspec, _full((8, D))),
        compiler_params=_params(("arbitrary",)),
    )(p, o_f, o_b, x, tgt, w_pa, w_pb, w_out, w_sp, w_spt, b_spb, ln_g, ln_b, g_on, g_post, gate_x)


def _sum_parts(parts, name):
    r = parts.shape[1]
    tr = r if NDEV * r * LANE * 4 <= (8 << 20) else _tile(r, (512, 256, 128, 64, 32, 16, 8))

    def body(p_ref, o_ref):
        acc = p_ref[0]
        for s in range(1, NDEV):
            acc = acc + p_ref[s]
        o_ref[...] = acc

    return pl.pallas_call(
        body, name=name, out_shape=_sds((r, LANE)), grid=(r // tr,),
        in_specs=[pl.BlockSpec((NDEV, tr, LANE), lambda i: (0, i, 0))],
        out_specs=pl.BlockSpec((tr, LANE), lambda i: (i, 0)),
        compiler_params=_params(("parallel",)),
    )(parts)


def _mod_bwd(c_all, c_ctx, dmx, dmc, w_mod_g):
    ws = w_mod_g.shape[2]

    def body(ca_ref, cc_ref, dsh_ref, dmx_ref, dmc_ref, dmc_sh_ref, w_ref, gw_ref, gc_ref, gb_ref):
        sc, _ = _silu_g(ca_ref[...])
        scc, dscc = _silu_g(cc_ref[...])
        dmc_tot = jnp.sum(dmc_ref[...], axis=0, keepdims=True)
        gb_ref[...] = jnp.sum(dmx_ref[...], axis=0, keepdims=True) + dmc_tot
        lhs = jnp.concatenate([sc, jnp.broadcast_to(scc, (8, D))], axis=0)
        rhs = jnp.concatenate([dsh_ref[...], dmc_sh_ref[...]], axis=0)
        gw_ref[...] = _mmh(lhs, rhs, TN)
        acc = jnp.zeros((8, D), F32)
        tot8 = jnp.broadcast_to(dmc_tot, (8, 3 * D))
        for j in range(NDEV):
            acc = acc + _mm(tot8[:, j * ws:(j + 1) * ws], w_ref[j], NT)
        gc_ref[...] = acc[0:1, :] * dscc

    return pl.pallas_call(
        body, name="mod_bwd", out_shape=(_sds((D, ws)), _sds((1, D)), _sds((1, 3 * D))),
        compiler_params=_params(),
    )(c_all, c_ctx, _my_cols(dmx, ws), dmx, dmc, _my_cols(dmc, ws), w_mod_g)


def _my_cols(a, ws):
    me = 4 * lax.axis_index("x") + 2 * lax.axis_index("y") + lax.axis_index("c")
    return lax.dynamic_slice_in_dim(a, me * ws, ws, axis=1)


def _pair_sum(mine, other, name):
    n, r, c = mine.shape
    tr = _tile(r, (256, 128, 64, 32, 16, 8))

    def body(a_ref, b_ref, o_ref):
        o_ref[...] = (a_ref[...].astype(F32) + b_ref[...].astype(F32)).astype(o_ref.dtype)

    blk = pl.BlockSpec((1, tr, c), lambda j, i: (j, i, 0))
    return pl.pallas_call(
        body, name=name, out_shape=_sds((n, r, c), mine.dtype), grid=(n, r // tr),
        in_specs=[blk, blk], out_specs=blk, compiler_params=_params(("parallel", "parallel")),
    )(mine, other)


def _adamw(parts, w, m, v, name, chip_sums_below=None):
    s_, r, c = parts.shape
    tr = _tile(r, (128, 64, 32, 16, 8)) if r * c * 4 > (1 << 20) else r
    c1 = 1.0 / (1.0 - ADAM_B1 ** ADAM_STEP)
    c2 = 1.0 / (1.0 - ADAM_B2 ** ADAM_STEP)

    def body(p_ref, w_ref, m_ref, v_ref, g_ref, d_ref, nm_ref, nv_ref):
        if chip_sums_below is None:
            part = lambda s: p_ref[s].astype(F32)
        else:
            core = lax.axis_index("c")
            me = 4 * lax.axis_index("x") + 2 * lax.axis_index("y") + core
            every = me >= chip_sums_below
            part = lambda s: jnp.where(every | (core == s % 2), p_ref[s].astype(F32), 0.0)
        g = part(0)
        for s in range(1, s_):
            g = g + part(s)
        m_new = ADAM_B1 * m_ref[...] + (1.0 - ADAM_B1) * g
        v_new = ADAM_B2 * v_ref[...] + (1.0 - ADAM_B2) * (g * g)
        g_ref[...] = g
        nm_ref[...] = m_new
        nv_ref[...] = v_new
        d_ref[...] = -ADAM_LR * ((m_new * c1) / (jnp.sqrt(v_new * c2) + ADAM_EPS) + ADAM_WD * w_ref[...])

    blk = pl.BlockSpec((tr, c), lambda i: (i, 0))
    o = _sds((r, c))
    return pl.pallas_call(
        body, name=name, out_shape=(o, o, o, o), grid=(r // tr,),
        in_specs=[pl.BlockSpec((s_, tr, c), lambda i: (0, i, 0)), blk, blk, blk],
        out_specs=(blk, blk, blk, blk),
        compiler_params=_params(("parallel",)),
    )(parts, w, m, v)


def _adamw_many(gs, ws, ms, vs, name):
    n = len(gs)
    c1 = 1.0 / (1.0 - ADAM_B1 ** ADAM_STEP)
    c2 = 1.0 / (1.0 - ADAM_B2 ** ADAM_STEP)

    def body(*refs):
        g_in, w_in_, m_in, v_in = (refs[k * n:(k + 1) * n] for k in range(4))
        g_out, d_out, m_out, v_out = (refs[(4 + k) * n:(5 + k) * n] for k in range(4))
        for p in range(n):
            g = g_in[p][...]
            m_new = ADAM_B1 * m_in[p][...] + (1.0 - ADAM_B1) * g
            v_new = ADAM_B2 * v_in[p][...] + (1.0 - ADAM_B2) * (g * g)
            g_out[p][...] = g
            m_out[p][...] = m_new
            v_out[p][...] = v_new
            d_out[p][...] = -ADAM_LR * ((m_new * c1) / (jnp.sqrt(v_new * c2) + ADAM_EPS) + ADAM_WD * w_in_[p][...])

    shapes = tuple(_sds(g.shape) for g in gs)
    res = pl.pallas_call(body, name=name, out_shape=shapes * 4, compiler_params=_params())(*gs, *ws, *ms, *vs)
    return [res[k * n:(k + 1) * n] for k in range(4)]


def _rows(a):
    flat = a.reshape(-1)
    n = flat.shape[0]
    r = -(-n // (8 * LANE)) * 8
    return jnp.pad(flat, (0, r * LANE - n)).reshape(r, LANE)


def _pack(items):
    parts, layout, at = [], [], 0
    for name, a in items:
        rws = _rows(a.astype(F32))
        layout.append((name, at, rws.shape[0], a.shape))
        parts.append(rws)
        at += rws.shape[0]
    return jnp.concatenate(parts, axis=0), layout


def _unpack(packed, layout):
    out = {}
    for name, at, r, shape in layout:
        n = 1
        for s in shape:
            n *= s
        out[name] = packed[at:at + r].reshape(-1)[:n].reshape(shape)
    return out


def kernel(x, c, ctx, c_ctx, w_mod, b_mod, g_pre, g_post, w_in, w_conv, a_log, dt_bias, g_onorm, gm_ln_g, gm_ln_b, w_sp, b_sp, w_pa, w_pb, w_out, loss_target, m_c_ctx, m_w_mod, m_b_mod, m_g_pre, m_g_post, m_w_in, m_w_conv, m_a_log, m_dt_bias, m_g_onorm, m_gm_ln_g, m_gm_ln_b, m_w_sp, m_b_sp, m_w_pa, m_w_pb, m_w_out, v_c_ctx, v_w_mod, v_b_mod, v_g_pre, v_g_post, v_w_in, v_w_conv, v_a_log, v_dt_bias, v_g_onorm, v_gm_ln_g, v_gm_ln_b, v_w_sp, v_b_sp, v_w_pa, v_w_pb, v_w_out):
    l = x.shape[1]
    lc = ctx.shape[1]
    lt = l + lc
    nch = lt // CH
    me = 4 * lax.axis_index("x") + 2 * lax.axis_index("y") + lax.axis_index("c")
    wsh = w_in.shape[2]
    off_a = 3 * D
    n_ab = 4 * NH
    jb = off_a // wsh
    o1 = off_a - jb * wsh
    o2 = o1 + n_ab
    assert o2 <= wsh and NREST == (NDEV - jb) * wsh - o2
    split = jb + 1

    w_in_bf = w_in[0].astype(_BF)
    wg_lo, wg_mod, wg_conv, c_all = _exchange(
        [w_in_bf, w_mod[0].astype(_BF), w_conv[0], c], ["gather_lo", "gather", "gather", "gather"],
        "gather_first", split)
    w_qkv = jnp.concatenate([wg_lo[j][:, :wsh] for j in range(jb)] + [wg_lo[jb][:, :o1]], axis=1)
    w_ab = jnp.pad(wg_lo[jb][:, o1:o2], ((0, 0), (0, LANE - n_ab)))
    wconv_full = jnp.moveaxis(wg_conv, 0, 1).reshape(3, 3 * D)
    c_all = c_all.reshape(NDEV, D)

    cc = jnp.concatenate([c, c_ctx.reshape(1, D), jnp.zeros((6, D), F32)], axis=0)
    mods = _modulation(cc, wg_mod, b_mod)
    h, h_t = _prenorm(ctx[0], x[0], mods, g_pre)
    p_qkv = _matmul_nn(h, w_qkv, "in_proj_qkv")
    pab = _matmul_nn(h, w_ab, "in_proj_ab")
    abt = jnp.swapaxes(pab[:, :n_ab].reshape(nch, CH, n_ab), 1, 2)
    alog16, dtb16 = a_log.reshape(1, 2 * NH), dt_bias.reshape(1, 2 * NH)
    alog_r = jnp.pad(alog16, ((0, 0), (0, LANE - 2 * NH)))
    dtb_r = jnp.pad(dtb16, ((0, 0), (0, LANE - 2 * NH)))
    alog_c = jnp.pad(alog16.reshape(2 * NH, 1), ((0, 2 * NH), (0, 0)))
    dtb_c = jnp.pad(dtb16.reshape(2 * NH, 1), ((0, 2 * NH), (0, 0)))
    qkv = _qkv_fwd(p_qkv, wconv_full, lc)
    late = [w_in_bf, w_pa[0].astype(_BF), w_pb[0].astype(_BF), w_out[0].astype(_BF)]
    xc_late = _Exchange(zip(late, ["gather_hi", "gather", "gather", "gather"]), split)
    o_f, o_b, s_f, s_b, t_f, t_b, wg_hi, wg_pa, wg_pb, wg_out = _gdn_fwd(
        qkv, pab, abt, alog_r, dtb_r, alog_c, dtb_c, lc, xc_late, late)
    w_rest = jnp.concatenate([wg_lo[jb][:, o2:wsh]] + [wg_hi[j][:, :wsh] for j in range(split, NDEV)], axis=1)
    wf_pa, wf_pb, wf_out = wg_pa.reshape(D, D), wg_pb.reshape(D, D), wg_out.reshape(D, D)
    p_rest = _matmul_nn(h, w_rest, "in_proj_rest")

    w_spt = jnp.swapaxes(w_sp[0], 1, 2)
    b_spb = jnp.broadcast_to(b_sp[0][:, :, None], (NH, GC, GC))
    gate_x = mods[0:1, 2 * D:]
    dp_rest, do, dy, ya, yb, mg, d_a, d_b, dout, dwsp, dbsp_l, pvec = _post(
        p_rest, o_f, o_b, x[0], loss_target[0], wf_pa, wf_pb, wf_out, w_sp[0], w_spt, b_spb, gm_ln_g, gm_ln_b,
        g_onorm, g_post, gate_x, lc)

    dw_rest = _matmul_nn(h_t, dp_rest, "dw_in_rest", _BF)
    o3 = wsh - o2
    chunks_hi = jnp.moveaxis(dw_rest[:, o3:].reshape(D, NDEV - split, wsh), 1, 0)
    dw_pa = _matmul_nn(ya, d_a, "dw_pa", _BF).reshape(NDEV, D // NDEV, D)
    dw_pb = _matmul_nn(yb, d_b, "dw_pb", _BF).reshape(NDEV, D // NDEV, D)
    dw_out = _matmul_nn(mg, dout, "dw_out", _BF).reshape(NDEV, D // NDEV, D)
    small_a, lay_a = _pack([
        ("g_post", pvec[1]), ("g_onorm", pvec[4, :DH]), ("gm_ln_g", pvec[2]), ("gm_ln_b", pvec[3]), ("w_sp", dwsp),
        ("b_sp", jnp.sum(dbsp_l, axis=-1)), ("loss", pvec[5]), ("dgate", pvec[0])])
    (theirs_hi,) = _exchange([chunks_hi], ["sibling"], "pair_swap_hi")
    chip_hi = _pair_sum(chunks_hi, theirs_hi[0], "pair_sum_hi")
    early = [chip_hi, dw_pa, dw_pb, dw_out, small_a]
    xc_early = _Exchange(zip(early, ["scatter_par_hi", "scatter", "scatter", "scatter", "gather"]), split)

    dqkv_f, dqkv_b, dcol_f, dcol_b, drow_f, drow_b, gvec_c, gvec_r, r_in, r_pa, r_pb, r_out, small_a_all = _gdn_bwd(
        qkv, pab, abt, alog_r, dtb_r, alog_c, dtb_c, s_f, s_b, t_f, t_b, do, lc, xc_early, early)
    dp_qkv, dwconv = _qkv_bwd(p_qkv, wconv_full, dqkv_f, dqkv_b, lc)
    drow = jnp.swapaxes(drow_f + drow_b, 1, 2).reshape(lt, n_ab)
    dpab = (dcol_f + dcol_b + jnp.pad(drow, ((0, 0), (0, LANE - n_ab)))).astype(_BF)

    dw_qkv = _matmul_nn(h_t, dp_qkv, "dw_in_qkv", _BF)
    dw_ab = _matmul_nn(h_t, dpab, "dw_in_ab", _BF)
    dw_lo = jnp.concatenate([dw_qkv, dw_ab[:, :n_ab], dw_rest[:, :o3]], axis=1)
    chunks_lo = jnp.moveaxis(dw_lo.reshape(D, split, wsh), 1, 0)
    (theirs,) = _exchange([chunks_lo], ["sibling"], "pair_swap")
    chip_lo = _pair_sum(chunks_lo, theirs[0], "pair_sum")
    xc_last = _Exchange([(chip_lo, "scatter_par_lo")], split)
    dh, r_in = _dh_matmul(dp_rest, dp_qkv, dpab, w_rest, w_qkv, w_ab, xc_last, [chip_lo], {0: r_in})
    grad_x, nvec = _prenorm_bwd(ctx[0], x[0], dh, dy, mods, g_pre)

    dalog = gvec_c[0, :2 * NH] + gvec_r[:2 * NH, 0]
    ddtb = gvec_c[1, :2 * NH] + gvec_r[:2 * NH, 1]
    small_b, lay_b = _pack([
        ("g_pre", nvec[4]), ("a_log", dalog), ("dt_bias", ddtb), ("w_conv", dwconv),
        ("dshift", nvec[0]), ("dscale", nvec[1]), ("dshift_c", nvec[2]), ("dscale_c", nvec[3])])
    (small_b_all,) = _exchange([small_b], ["gather"], "gather_small")
    tot = _unpack(_sum_parts(small_a_all, "sum_small_a"), lay_a)
    tot.update(_unpack(_sum_parts(small_b_all, "sum_small_b"), lay_b))

    def per_device(packed_all, layout, name):
        at, r = [(a_, r_) for nm, a_, r_, _ in layout if nm == name][0]
        return packed_all[:, at:at + r].reshape(NDEV, -1)

    dmx_all = jnp.concatenate([per_device(small_b_all, lay_b, "dshift"), per_device(small_b_all, lay_b, "dscale"),
                               per_device(small_a_all, lay_a, "dgate")], axis=1)
    dmc_all = jnp.concatenate([per_device(small_b_all, lay_b, "dshift_c"), per_device(small_b_all, lay_b, "dscale_c"),
                               jnp.zeros((NDEV, D), F32)], axis=1)
    g_wmod, g_cctx, g_bmod = _mod_bwd(c_all, c_ctx.reshape(1, D), dmx_all, dmc_all, wg_mod)
    loss = 0.5 / D * jnp.sum(tot["loss"])
    ws_conv = w_conv.shape[2]
    g_wconv = lax.dynamic_slice_in_dim(tot["w_conv"], me * ws_conv, ws_conv, axis=1)

    small_names = ["c_ctx", "b_mod", "g_pre", "g_post", "a_log", "dt_bias", "g_onorm", "gm_ln_g", "gm_ln_b",
                   "w_sp", "b_sp", "w_conv"]
    wts = dict(c_ctx=c_ctx, b_mod=b_mod, g_pre=g_pre, g_post=g_post, a_log=a_log, dt_bias=dt_bias, g_onorm=g_onorm,
               gm_ln_g=gm_ln_g, gm_ln_b=gm_ln_b, w_sp=w_sp, b_sp=b_sp, w_conv=w_conv)
    ms = dict(c_ctx=m_c_ctx, b_mod=m_b_mod, g_pre=m_g_pre, g_post=m_g_post, a_log=m_a_log, dt_bias=m_dt_bias,
              g_onorm=m_g_onorm, gm_ln_g=m_gm_ln_g, gm_ln_b=m_gm_ln_b, w_sp=m_w_sp, b_sp=m_b_sp, w_conv=m_w_conv)
    vs = dict(c_ctx=v_c_ctx, b_mod=v_b_mod, g_pre=v_g_pre, g_post=v_g_post, a_log=v_a_log, dt_bias=v_dt_bias,
              g_onorm=v_g_onorm, gm_ln_g=v_gm_ln_g, gm_ln_b=v_gm_ln_b, w_sp=v_w_sp, b_sp=v_b_sp, w_conv=v_w_conv)
    gs = dict(tot)
    gs.update(c_ctx=g_cctx, b_mod=g_bmod, w_conv=g_wconv)
    flat = lambda a: a.reshape(-1, a.shape[-1])
    res_small = [
        {nm: a.reshape(wts[nm].shape) for nm, a in zip(small_names, arrays)}
        for arrays in _adamw_many([flat(gs[nm].reshape(wts[nm].shape)) for nm in small_names],
                                  [flat(wts[nm]) for nm in small_names], [flat(ms[nm]) for nm in small_names],
                                  [flat(vs[nm]) for nm in small_names], "adamw_small")]
    res_big = {
        "w_mod": _adamw(g_wmod[None], w_mod[0], m_w_mod[0], v_w_mod[0], "adamw_w_mod"),
        "w_in": _adamw(r_in, w_in[0], m_w_in[0], v_w_in[0], "adamw_w_in", chip_sums_below=NDEV),
        "w_pa": _adamw(r_pa, w_pa[0], m_w_pa[0], v_w_pa[0], "adamw_w_pa"),
        "w_pb": _adamw(r_pb, w_pb[0], m_w_pb[0], v_w_pb[0], "adamw_w_pb"),
        "w_out": _adamw(r_out, w_out[0], m_w_out[0], v_w_out[0], "adamw_w_out"),
    }
    order = ["c_ctx", "w_mod", "b_mod", "g_pre", "g_post", "w_in", "w_conv", "a_log", "dt_bias", "g_onorm",
             "gm_ln_g", "gm_ln_b", "w_sp", "b_sp", "w_pa", "w_pb", "w_out"]
    outs = [loss, grad_x[None]]
    for k in range(4):
        for nm in order:
            if nm in res_big:
                outs.append(res_big[nm][k][None])
            else:
                outs.append(res_small[k][nm])
    return tuple(outs)
```

```python
import functools

import jax
import jax.numpy as jnp
from jax import lax
from jax.experimental import pallas as pl
from jax.experimental.pallas import tpu as pltpu

F32 = jnp.float32
_BF = jnp.bfloat16
_HI = lax.Precision.HIGHEST
D = 1024
NH = 8
DH = 128
CH = 64
SUB = 2
GC = 128
NREST = 6 * D
NMAIN = NREST + 3 * D
EPS = 1e-6
LANE = 128
NDEV = 8
VMEM_LIMIT = 56 * 1024 * 1024
MESH = pl.DeviceIdType.MESH

ADAM_LR, ADAM_B1, ADAM_B2, ADAM_EPS, ADAM_WD, ADAM_STEP = 0.001, 0.9, 0.999, 1e-08, 0.01, 10

NN = ((1,), (0,))
NT = ((1,), (1,))
TN = ((0,), (0,))


def _dot(a, b, dims=NN, prec=None):
    return lax.dot_general(a, b, (dims, ((), ())), precision=prec, preferred_element_type=F32)


def _mm(a, b, dims=NN):
    return _dot(a.astype(_BF), b.astype(_BF), dims)


def _mmh(a, b, dims=NN):
    return _dot(a.astype(F32), b.astype(F32), dims, _HI)


def _split(a):
    hi = a.astype(_BF)
    return hi, (a - hi.astype(F32)).astype(_BF)


def _mm3(a, b, dims=NN):
    ah, al = _split(a)
    bh, bl = _split(b)
    return _dot(ah, bh, dims) + (_dot(ah, bl, dims) + _dot(al, bh, dims))


def _sigmoid(x):
    return 1.0 / (1.0 + jnp.exp(-x))


def _silu_g(x):
    s = _sigmoid(x)
    return x * s, s * (1.0 + x * (1.0 - s))


def _gelu_g(x):
    c = 0.7978845608028654
    t = jnp.tanh(c * (x + 0.044715 * (x * x * x)))
    cdf = 0.5 * (1.0 + t)
    return x * cdf, cdf + 0.5 * x * (1.0 - t * t) * c * (1.0 + 3 * 0.044715 * x * x)


def _softplus(x):
    return jnp.maximum(x, 0.0) + jnp.log(1.0 + jnp.exp(-jnp.abs(x)))


def _params(sem=None):
    return pltpu.CompilerParams(dimension_semantics=sem, vmem_limit_bytes=VMEM_LIMIT)


def _tile(n, pref):
    for t in pref:
        if n % t == 0:
            return t
    return n


def _full(shape):
    nd = len(shape)
    return pl.BlockSpec(shape, lambda *_: (0,) * nd)


def _sds(shape, dt=F32):
    return jax.ShapeDtypeStruct(shape, dt)


MAX_PIECES = 12
PIECE_BYTES = 256 * 1024


def _piece_slices(shape, itemsize):
    total = itemsize
    for d in shape:
        total *= d
    want = min(MAX_PIECES, total // PIECE_BYTES)
    lead = shape[0] if len(shape) >= 3 else 1
    rows = shape[-2] if len(shape) >= 2 else 1
    if want < 2 or lead > want:
        return [()]
    m = max([n for n in (8, 4, 2, 1) if n * lead <= want and rows % (16 * n) == 0], default=1)
    if m * lead < 2:
        return [()]
    rs = rows // m
    mid = (slice(None),) * max(len(shape) - 3, 0)
    if len(shape) >= 3:
        return [(i,) + mid + (pl.ds(j * rs, rs),) for i in range(lead) for j in range(m)]
    return [(pl.ds(j * rs, rs),) for j in range(m)]


class _Pieces:
    def __init__(self, copies):
        self.copies = copies

    def start(self):
        for cp in self.copies:
            cp.start()

    def wait_send(self):
        for cp in self.copies:
            cp.wait_send()

    def wait_recv(self):
        for cp in self.copies:
            cp.wait_recv()

    def wait(self):
        for cp in self.copies:
            cp.wait()


class _Exchange:
    def __init__(self, specs, split):
        self.specs = list(specs)
        self.split = split
        self.n = len(self.specs)
        def out(a, k):
            if k == "sibling":
                return (1,) + tuple(a.shape)
            return (NDEV,) + (tuple(a.shape) if k.startswith("gather") else tuple(a.shape[1:]))

        self.out_shape = tuple(_sds(out(a, k), a.dtype) for a, k in self.specs)
        self.pieces = [_piece_slices(o.shape[1:], jnp.dtype(o.dtype).itemsize) for o in self.out_shape]
        self.sem_base = [(NDEV - 1) * sum(len(p) for p in self.pieces[:a]) for a in range(self.n + 1)]
        self.scratch = [pltpu.SemaphoreType.DMA((self.sem_base[-1],)), pltpu.SemaphoreType.DMA((self.sem_base[-1],)),
                        pltpu.SemaphoreType.DMA((self.sem_base[-1] // (NDEV - 1),))]

    def _local(self, sems, a, src, dst):
        base = self.sem_base[a] // (NDEV - 1)
        return _Pieces([pltpu.make_async_copy(src.at[sl] if sl else src, dst.at[sl] if sl else dst, sems[2].at[base + p])
                        for p, sl in enumerate(self.pieces[a])])

    def _remote(self, sems, a, k, src, dst, to):
        send_sems, recv_sems, _ = sems
        base = self.sem_base[a] + k * len(self.pieces[a])
        return _Pieces([
            pltpu.make_async_remote_copy(
                src_ref=src.at[sl] if sl else src, dst_ref=dst.at[sl] if sl else dst, send_sem=send_sems.at[base + p],
                recv_sem=recv_sems.at[base + p], device_id=to, device_id_type=MESH)
            for p, sl in enumerate(self.pieces[a])])

    def _ok(self, kind, idx):
        if kind.endswith("_lo"):
            return idx < self.split
        if kind.endswith("_hi"):
            return idx >= self.split
        return True

    def _phases(self, ins, outs, sems):
        x, y, c = lax.axis_index("x"), lax.axis_index("y"), lax.axis_index("c")
        me = 4 * x + 2 * y + c
        sib = (x, y, 1 - c)
        sib_idx = 4 * x + 2 * y + (1 - c)
        chips = [(1 - x, y), (x, 1 - y), (1 - x, 1 - y)]
        starts, forwards, waits = [], [], []
        for a, (_, kind) in enumerate(self.specs):
            ok = functools.partial(self._ok, kind)
            if kind.startswith("gather"):
                def copy(k, block, to, src=None, a=a):
                    rows = outs[a].at[block]
                    return self._remote(sems, a, k, rows if src is None else src, rows, to)

                loc = self._local(sems, a, ins[a], outs[a].at[me])
                first = [copy(0, me, sib, ins[a])] + [copy(1 + j, me, (*chip, c), ins[a]) for j, chip in enumerate(chips)]
                starts += [(ok(me), loc.start)] + [(ok(me), cp.start) for cp in first]
                waits += [(ok(me), loc.wait)] + [(ok(me), cp.wait_send) for cp in first]
                for j, chip in enumerate(chips):
                    origin = 4 * chip[0] + 2 * chip[1] + c
                    passed = copy(4 + j, origin, sib)
                    forwards += [(ok(origin), copy(1 + j, origin, sib).wait_recv), (ok(origin), passed.start)]
                    waits.append((ok(origin), passed.wait_send))
                    other = 4 * chip[0] + 2 * chip[1] + (1 - c)
                    waits.append((ok(other), copy(4 + j, other, sib).wait_recv))
                waits.append((ok(sib_idx), copy(0, sib_idx, sib).wait_recv))
            elif kind == "sibling":
                swap = self._remote(sems, a, 0, ins[a], outs[a].at[0], sib)
                starts.append((True, swap.start))
                waits += [(True, swap.wait_send), (True, swap.wait_recv)]
            else:
                base = self.split if kind.endswith("_hi") else 0
                same_core_only = "_par" in kind

                def src(idx, a=a, base=base):
                    return ins[a].at[jnp.clip(idx - base, 0, ins[a].shape[0] - 1)]

                loc = self._local(sems, a, src(me), outs[a].at[me])
                starts.append((ok(me), loc.start))
                waits.append((ok(me), loc.wait))
                for k in range(1, NDEV):
                    if same_core_only and k & 1:
                        continue
                    px = 1 - x if (k >> 2) & 1 else x
                    py = 1 - y if (k >> 1) & 1 else y
                    pc = 1 - c if k & 1 else c
                    pidx = 4 * px + 2 * py + pc
                    send = self._remote(sems, a, k - 1, src(pidx), outs[a].at[me], (px, py, pc))
                    arrive = self._remote(sems, a, k - 1, src(pidx), outs[a].at[pidx], (px, py, pc))
                    starts.append((ok(pidx), send.start))
                    waits += [(ok(pidx), send.wait_send), (ok(me), arrive.wait_recv)]
        return starts, forwards, waits

    @staticmethod
    def _run(actions):
        for cond, fn in actions:
            if cond is True:
                fn()
            else:
                pl.when(cond)(fn)

    def start(self, ins, outs, sems):
        self._run(self._phases(ins, outs, sems)[0])

    def forward(self, ins, outs, sems):
        self._run(self._phases(ins, outs, sems)[1])

    def wait(self, ins, outs, sems):
        self._run(self._phases(ins, outs, sems)[2])


_ANY = pl.BlockSpec(memory_space=pl.ANY)


def _exchange(arrays, kinds, name, split=0, into=None):
    xc = _Exchange(zip(arrays, kinds), split)
    n = xc.n
    into = into or {}
    ni = len(into)

    def body(*refs):
        ins, outs, sems = refs[:n], refs[n + ni:2 * n + ni], refs[2 * n + ni:]
        xc.start(ins, outs, sems)
        xc.forward(ins, outs, sems)
        xc.wait(ins, outs, sems)

    return pl.pallas_call(
        body, name=name, out_shape=xc.out_shape, in_specs=[_ANY] * (n + ni), out_specs=tuple([_ANY] * n),
        scratch_shapes=xc.scratch, input_output_aliases={n + t: a for t, a in enumerate(into)},
    )(*arrays, *into.values())


def _matmul_nn(a, b, name, out_dtype=F32):
    m, kk = a.shape
    n = b.shape[1]
    tm = m if m * kk * a.dtype.itemsize <= (12 << 20) else _tile(m, (1088, 1024, 640, 512, 256, 128))
    tn = _tile(n, (512, 256, 128))

    def body(a_ref, b_ref, o_ref):
        o_ref[...] = _mm(a_ref[...], b_ref[...]).astype(o_ref.dtype)

    return pl.pallas_call(
        body, name=name, out_shape=_sds((m, n), out_dtype), grid=(n // tn, m // tm),
        in_specs=[pl.BlockSpec((tm, kk), lambda j, i: (i, 0)), pl.BlockSpec((kk, tn), lambda j, i: (0, j))],
        out_specs=pl.BlockSpec((tm, tn), lambda j, i: (i, j)),
        compiler_params=_params(("parallel", "parallel")),
    )(a, b)


def _dh_matmul(dp_rest, dp_qkv, dpab, w_rest, w_qkv, w_ab, xc, xc_arrays, xc_into):
    lt = dp_rest.shape[0]
    tm = _tile(lt, (2176, 1088, 1024, 640, 512, 256, 128))
    nr, nq = dp_rest.shape[1] // D, dp_qkv.shape[1] // D
    nx, ni = xc.n, len(xc_into)
    ni_steps = lt // tm

    def body(*refs):
        dr_ref, dq_ref, ab_ref, wr_ref, wq_ref, wab_ref = refs[:6]
        x_in = refs[6:6 + nx]
        o_ref = refs[6 + nx + ni]
        x_out = refs[7 + nx + ni:7 + 2 * nx + ni]
        sems = refs[7 + 2 * nx + ni:]
        i = pl.program_id(0)
        k = pl.program_id(1)

        @pl.when((i == 0) & (k == 0))
        def _():
            xc.start(x_in, x_out, sems)

        @pl.when(k == 0)
        def _():
            o_ref[...] = _mm(ab_ref[...], wab_ref[...], NT)

        @pl.when(k < nr)
        def _():
            o_ref[...] += _mm(dr_ref[...], wr_ref[...], NT)

        @pl.when(k >= nr)
        def _():
            o_ref[...] += _mm(dq_ref[...], wq_ref[...], NT)

        @pl.when((i == ni_steps - 1) & (k == nr + nq - 1))
        def _():
            xc.wait(x_in, x_out, sems)

    rk = lambda k: jnp.minimum(k, nr - 1)
    qk = lambda k: jnp.maximum(k - nr, 0)
    return pl.pallas_call(
        body, name="dh_matmul", out_shape=(_sds((lt, D)),) + xc.out_shape, grid=(lt // tm, nr + nq),
        in_specs=[pl.BlockSpec((tm, D), lambda i, k: (i, rk(k))), pl.BlockSpec((tm, D), lambda i, k: (i, qk(k))),
                  pl.BlockSpec((tm, LANE), lambda i, k: (i, 0)),
                  pl.BlockSpec((D, D), lambda i, k: (0, rk(k))), pl.BlockSpec((D, D), lambda i, k: (0, qk(k))),
                  _full((D, LANE))] + [_ANY] * (nx + ni),
        out_specs=(pl.BlockSpec((tm, D), lambda i, k: (i, 0)),) + tuple([_ANY] * nx),
        scratch_shapes=xc.scratch, input_output_aliases={6 + nx + t: 1 + a for t, a in enumerate(xc_into)},
        compiler_params=_params(("arbitrary", "arbitrary")),
    )(dp_rest, dp_qkv, dpab, w_rest, w_qkv, w_ab, *xc_arrays, *xc_into.values())


def _modulation(cc, w_mod_g, b_mod):
    ws = w_mod_g.shape[2]

    def body(c_ref, w_ref, b_ref, o_ref):
        s, _ = _silu_g(c_ref[...])
        o_ref[...] = _mm(s, w_ref[0]) + b_ref[...]

    return pl.pallas_call(
        body, name="modulation", out_shape=_sds((8, 3 * D)), grid=(NDEV,),
        in_specs=[_full((8, D)), pl.BlockSpec((1, D, ws), lambda j: (j, 0, 0)), pl.BlockSpec((1, ws), lambda j: (0, j))],
        out_specs=pl.BlockSpec((8, ws), lambda j: (0, j)),
        compiler_params=_params(("parallel",)),
    )(cc, w_mod_g, b_mod)


def _prenorm(ctx, x, mods, g_pre):
    lc = ctx.shape[0]
    lt = lc + x.shape[0]
    tm = _tile(lc, (256, 128))
    nct = lc // tm

    def body(c_ref, x_ref, m_ref, g_ref, o_ref, ot_ref):
        is_ctx = pl.program_id(0) < nct
        x = jnp.where(is_ctx, c_ref[...], x_ref[...])
        shift = jnp.where(is_ctx, m_ref[1:2, 0:D], m_ref[0:1, 0:D])
        scale = jnp.where(is_ctx, m_ref[1:2, D:2 * D], m_ref[0:1, D:2 * D])
        r = lax.rsqrt(jnp.mean(x * x, axis=-1, keepdims=True) + EPS)
        h = ((x * r * g_ref[...]) * (1.0 + scale) + shift).astype(o_ref.dtype)
        o_ref[...] = h
        ot_ref[...] = h.T

    return pl.pallas_call(
        body, name="prenorm", out_shape=(_sds((lt, D), _BF), _sds((D, lt), _BF)), grid=(lt // tm,),
        in_specs=[pl.BlockSpec((tm, D), lambda i: (jnp.minimum(i, nct - 1), 0)),
                  pl.BlockSpec((tm, D), lambda i: (jnp.maximum(i - nct, 0), 0)), _full((8, 3 * D)), _full((1, D))],
        out_specs=(pl.BlockSpec((tm, D), lambda i: (i, 0)), pl.BlockSpec((D, tm), lambda i: (0, i))),
        compiler_params=_params(("parallel",)),
    )(ctx, x, mods, g_pre)


def _prenorm_bwd(ctx, x, dh, dy, mods, g_pre):
    lc = ctx.shape[0]
    lt = lc + x.shape[0]
    tm = _tile(lc, (256, 128))
    nct = lc // tm
    nl = (lt - lc) // tm

    def body(c_ref, x_ref, dh_ref, dy_ref, m_ref, g_ref, gx_ref, vec_ref):
        i = pl.program_id(0)

        @pl.when(i == 0)
        def _():
            vec_ref[...] = jnp.zeros_like(vec_ref)

        is_ctx = i < nct
        x = jnp.where(is_ctx, c_ref[...], x_ref[...])
        dh = dh_ref[...]
        g = g_ref[...]
        scale = jnp.where(is_ctx, m_ref[1:2, D:2 * D], m_ref[0:1, D:2 * D])
        r = lax.rsqrt(jnp.mean(x * x, axis=-1, keepdims=True) + EPS)
        n = x * r
        hn = n * g
        dsh = jnp.sum(dh, axis=0, keepdims=True)
        dsc = jnp.sum(dh * hn, axis=0, keepdims=True)
        dhn = dh * (1.0 + scale)
        vec_ref[4:5, :] += jnp.sum(dhn * n, axis=0, keepdims=True)
        dn = dhn * g
        dx = r * (dn - n * jnp.mean(dn * n, axis=-1, keepdims=True))

        @pl.when(is_ctx)
        def _():
            vec_ref[2:3, :] += dsh
            vec_ref[3:4, :] += dsc

        @pl.when(jnp.logical_not(is_ctx))
        def _():
            vec_ref[0:1, :] += dsh
            vec_ref[1:2, :] += dsc
            gx_ref[...] = dy_ref[...] + dx

    xrow = lambda i: (jnp.maximum(i - nct, 0), 0)
    return pl.pallas_call(
        body, name="prenorm_bwd", out_shape=(_sds((nl * tm, D)), _sds((8, D))), grid=(lt // tm,),
        in_specs=[pl.BlockSpec((tm, D), lambda i: (jnp.minimum(i, nct - 1), 0)), pl.BlockSpec((tm, D), xrow),
                  pl.BlockSpec((tm, D), lambda i: (i, 0)), pl.BlockSpec((tm, D), xrow), _full((8, 3 * D)), _full((1, D))],
        out_specs=(pl.BlockSpec((tm, D), xrow), _full((8, D))),
        compiler_params=_params(("arbitrary",)),
    )(ctx, x, dh, dy, mods, g_pre)


def _conv_parts(x, w, lc):
    lt = x.shape[0]
    row = lax.broadcasted_iota(jnp.int32, x.shape, 0)
    first = (row == 0) | (row == lc)
    last = (row == lc - 1) | (row == lt - 1)
    xp = jnp.where(first, 0.0, pltpu.roll(x, 1, 0))
    xn = jnp.where(last, 0.0, pltpu.roll(x, lt - 1, 0))
    y = w[0:1, :] * xp + w[1:2, :] * x + w[2:3, :] * xn
    return xp, xn, y, first, last


def _qkv_fwd(p, w_conv, lc):
    lt = p.shape[0]

    def body(p_ref, w_ref, o_ref):
        _, _, y, _, _ = _conv_parts(p_ref[...], w_ref[...], lc)
        s, _ = _silu_g(y)
        rs = lax.rsqrt(jnp.sum(s * s, axis=-1, keepdims=True) + EPS)
        o_ref[...] = s * jnp.where(pl.program_id(0) < 2 * NH, rs, 1.0)

    return pl.pallas_call(
        body, name="qkv_fwd", out_shape=_sds((lt, 3 * D)), grid=(3 * NH,),
        in_specs=[pl.BlockSpec((lt, DH), lambda j: (0, j)), pl.BlockSpec((3, DH), lambda j: (0, j))],
        out_specs=pl.BlockSpec((lt, DH), lambda j: (0, j)),
        compiler_params=_params(("parallel",)),
    )(p, w_conv)


def _qkv_bwd(p, w_conv, dqkv_f, dqkv_b, lc):
    lt = p.shape[0]

    def body(p_ref, w_ref, df_ref, db_ref, dp_ref, dw_ref):
        w = w_ref[...]
        xp, xn, y, first, last = _conv_parts(p_ref[...], w, lc)
        s, ds_dy = _silu_g(y)
        dn = df_ref[...] + db_ref[...]
        rs = lax.rsqrt(jnp.sum(s * s, axis=-1, keepdims=True) + EPS)
        nrm = s * rs
        ds_n = rs * (dn - nrm * jnp.sum(dn * nrm, axis=-1, keepdims=True))
        ds = jnp.where(pl.program_id(0) < 2 * NH, ds_n, dn)
        dy = ds * ds_dy
        dw_ref[0:1, :] = jnp.sum(dy * xp, axis=0, keepdims=True)
        dw_ref[1:2, :] = jnp.sum(dy * p_ref[...], axis=0, keepdims=True)
        dw_ref[2:3, :] = jnp.sum(dy * xn, axis=0, keepdims=True)
        dyn = jnp.where(last, 0.0, pltpu.roll(dy, lt - 1, 0))
        dyp = jnp.where(first, 0.0, pltpu.roll(dy, 1, 0))
        dp_ref[...] = (w[1:2, :] * dy + w[0:1, :] * dyn + w[2:3, :] * dyp).astype(dp_ref.dtype)

    return pl.pallas_call(
        body, name="qkv_bwd", out_shape=(_sds((lt, 3 * D), _BF), _sds((3, 3 * D))), grid=(3 * NH,),
        in_specs=[pl.BlockSpec((lt, DH), lambda j: (0, j)), pl.BlockSpec((3, DH), lambda j: (0, j)),
                  pl.BlockSpec((lt, DH), lambda j: (0, j)), pl.BlockSpec((lt, DH), lambda j: (0, j))],
        out_specs=(pl.BlockSpec((lt, DH), lambda j: (0, j)), pl.BlockSpec((3, DH), lambda j: (0, j))),
        compiler_params=_params(("parallel",)),
    )(p, w_conv, dqkv_f, dqkv_b)


def _masks(d):
    ri = lax.broadcasted_iota(jnp.int32, (CH, CH), 0)
    ci = lax.broadcasted_iota(jnp.int32, (CH, CH), 1)
    incl = (ri >= ci) if d == 0 else (ri <= ci)
    strict = (ri > ci) if d == 0 else (ri < ci)
    incl_t = (ri <= ci) if d == 0 else (ri >= ci)
    return incl, strict, incl_t, ri == ci


def _decays(d, ab, abt, alog_r, dtb_r, alog_c, dtb_c, incl, incl_t):
    g_full = -jnp.exp(alog_r) * _softplus(ab + dtb_r)
    beta_full = _sigmoid(ab)
    gc_full = _mmh(incl.astype(F32), g_full)
    gl_full = jnp.sum(g_full, axis=0, keepdims=True)
    gt_full = -jnp.exp(alog_c) * _softplus(abt + dtb_c)
    gct = _mmh(gt_full, incl_t.astype(F32))
    return g_full, beta_full, gc_full, gl_full, gt_full, gct


def _lane_onehot(idx, n=LANE):
    return (lax.broadcasted_iota(jnp.int32, (1, n), 1) == idx).astype(F32)


def _head_scalars(d, h, beta_full, gc_full, gl_full, gct):
    idx = d * NH + h
    oh = _lane_onehot(idx)
    gcol = jnp.sum(gc_full * oh, axis=-1, keepdims=True)
    bcol = jnp.sum(beta_full * _lane_onehot(2 * NH + idx), axis=-1, keepdims=True)
    gl = jnp.sum(gl_full * oh, axis=-1, keepdims=True)
    grow = gct[idx:idx + 1, :]
    return gcol, grow, bcol, gl


def _lockstep(gens):
    live = list(gens)
    while live:
        nxt = []
        for g in live:
            try:
                next(g)
                nxt.append(g)
            except StopIteration:
                pass
        live = nxt


def _chunk_local(qh, kh, vh, gcol, grow, bcol, gl, incl, strict):
    decay = jnp.where(incl, jnp.exp(gcol - grow), 0.0)
    kb = kh * bcol
    qs = qh * (DH ** -0.5)
    both = _mm(jnp.concatenate([kb, qs], axis=0), kh, NT)
    a = jnp.where(strict, both[:CH] * decay, 0.0)
    egc = jnp.exp(gcol)
    rhs_u = vh * bcol
    rhs_w = kb * egc
    attn = jnp.where(incl, both[CH:] * decay, 0.0)
    etail = jnp.exp(gl - gcol)
    return decay, kb, a, egc, rhs_u, rhs_w, qs, attn, etail


def _scan_specs(lt, lc, bwd_pass):
    assert lt % (SUB * CH) == 0 and lc % (SUB * CH) == 0
    nch = lt // (SUB * CH)
    ncc = lc // (SUB * CH)
    if not bwd_pass:
        cf = lambda s: s
        cb = lambda s: jnp.where(s < ncc, ncc - 1 - s, nch + ncc - 1 - s)
    else:
        cf = lambda s: nch - 1 - s
        cb = lambda s: jnp.where(s < nch - ncc, ncc + s, s - (nch - ncc))
    return nch, cf, cb


def _gdn_fwd(qkv, pab, abt, alog_r, dtb_r, alog_c, dtb_c, lc, xc, xc_arrays):
    lt = qkv.shape[0]
    nch, cf, cb = _scan_specs(lt, lc, False)
    nx = xc.n

    def body(*refs):
        qf, kf, vf, abf, abtf, qb, kb_, vb, abb, abtb, ar, dr, ac, dc = refs[:14]
        x_in = refs[14:14 + nx]
        of_ref, ob_ref, sf_ref, sb_ref, tf_ref, tb_ref = refs[14 + nx:20 + nx]
        x_out = refs[20 + nx:20 + 2 * nx]
        s_scr = refs[20 + 2 * nx]
        sems = refs[21 + 2 * nx:]

        @pl.when(pl.program_id(0) == 0)
        def _():
            s_scr[...] = jnp.zeros_like(s_scr)
            xc.start(x_in, x_out, sems)

        def chain(d, h, c, late, q_r, k_r, v_r, o_ref, sh_ref, th_ref, masks, decs):
            incl, strict, _, eye = masks
            sl = slice(h * DH, (h + 1) * DH)
            rows = slice(c * CH, (c + 1) * CH)
            qh, kh, vh = q_r[rows, sl], k_r[rows, sl], v_r[rows, sl]
            gcol, grow, bcol, gl = _head_scalars(d, h, *decs)
            _, _, a, egc, rhs_u, rhs_w, qs, attn, etail = _chunk_local(qh, kh, vh, gcol, grow, bcol, gl, incl, strict)
            yield
            n = -a
            t = jnp.where(eye, 1.0, 0.0) + n
            p = _mm3(n, n)
            yield
            for _ in range(4):
                r = _mm3(jnp.concatenate([t, p], axis=0), p)
                yield
                t = t + r[:CH]
                p = r[CH:]
            t = t + _mm3(t, p)
            yield
            for _ in range(3 * late):
                yield
            sol = _mm3(t, jnp.concatenate([rhs_u, rhs_w], axis=1))
            u, w = sol[:, :DH], sol[:, DH:]
            s = s_scr[d, h]
            sh_ref[c, h] = s
            th_ref[c, h] = t
            yield
            ws = _mm(jnp.concatenate([w, qs * egc], axis=0), s)
            yield
            v_new = u - ws[:CH]
            o_ref[rows, sl] = ws[CH:] + _mm(attn, v_new)
            s_scr[d, h] = s * jnp.exp(gl) + _mm(kh * etail, v_new, TN)

        chains = []
        for d, (q_r, k_r, v_r, ab_r, abt_r, o_ref, sh_ref, th_ref) in enumerate(
                ((qf, kf, vf, abf, abtf, of_ref, sf_ref, tf_ref), (qb, kb_, vb, abb, abtb, ob_ref, sb_ref, tb_ref))):
            masks = _masks(d)
            for pos, c in enumerate(range(SUB) if d == 0 else reversed(range(SUB))):
                _, beta_full, gc_full, gl_full, _, gct = _decays(
                    d, ab_r[c * CH:(c + 1) * CH, :], abt_r[c], ar[...], dr[...], ac[...], dc[...], masks[0], masks[2])
                for h in range(NH):
                    chains.append(chain(d, h, c, pos, q_r, k_r, v_r, o_ref, sh_ref, th_ref, masks,
                                        (beta_full, gc_full, gl_full, gct)))
        _lockstep(chains)

        @pl.when(pl.program_id(0) == nch // 2)
        def _():
            xc.forward(x_in, x_out, sems)

        @pl.when(pl.program_id(0) == nch - 1)
        def _():
            xc.wait(x_in, x_out, sems)

    def row(c, col):
        return pl.BlockSpec((SUB * CH, D), lambda s: (c(s), col))

    def chunk_in(c):
        return [row(c, 0), row(c, 1), row(c, 2), pl.BlockSpec((SUB * CH, LANE), lambda s: (c(s), 0)),
                pl.BlockSpec((SUB, 4 * NH, CH), lambda s: (c(s), 0, 0))]

    def hist(c, n):
        return pl.BlockSpec((SUB, NH, n, n), lambda s: (c(s), 0, 0, 0))

    small = [_full((1, LANE)), _full((1, LANE)), _full((4 * NH, 1)), _full((4 * NH, 1))]
    return pl.pallas_call(
        body, name="gdn_fwd", grid=(nch,),
        out_shape=(_sds((lt, D)), _sds((lt, D)), _sds((lt // CH, NH, DH, DH)), _sds((lt // CH, NH, DH, DH)),
                   _sds((lt // CH, NH, CH, CH)), _sds((lt // CH, NH, CH, CH))) + xc.out_shape,
        in_specs=chunk_in(cf) + chunk_in(cb) + small + [_ANY] * nx,
        out_specs=(pl.BlockSpec((SUB * CH, D), lambda s: (cf(s), 0)), pl.BlockSpec((SUB * CH, D), lambda s: (cb(s), 0)),
                   hist(cf, DH), hist(cb, DH), hist(cf, CH), hist(cb, CH)) + tuple([_ANY] * nx),
        scratch_shapes=[pltpu.VMEM((2, NH, DH, DH), F32)] + xc.scratch,
        compiler_params=_params(("arbitrary",)),
    )(qkv, qkv, qkv, pab, abt, qkv, qkv, qkv, pab, abt, alog_r, dtb_r, alog_c, dtb_c, *xc_arrays)


def _gdn_bwd(qkv, pab, abt, alog_r, dtb_r, alog_c, dtb_c, s_f, s_b, t_f, t_b, do, lc, xc, xc_arrays):
    lt = qkv.shape[0]
    nch, cf, cb = _scan_specs(lt, lc, True)
    nx = xc.n

    def body(*refs):
        qf, kf, vf, abf, abtf, sf_ref, tf_ref, dof, qb, kb_, vb, abb, abtb, sb_ref, tb_ref, dob, ar, dr, ac, dc = refs[:20]
        x_in = refs[20:20 + nx]
        dqf_ref, dqb_ref, dcf_ref, dcb_ref, drf_ref, drb_ref, vcol_ref, vrow_ref = refs[20 + nx:28 + nx]
        x_out = refs[28 + nx:28 + 2 * nx]
        ds_scr = refs[28 + 2 * nx]
        sems = refs[29 + 2 * nx:]

        @pl.when(pl.program_id(0) == 0)
        def _():
            ds_scr[...] = jnp.zeros_like(ds_scr)
            vcol_ref[...] = jnp.zeros_like(vcol_ref)
            vrow_ref[...] = jnp.zeros_like(vrow_ref)
            xc.start(x_in, x_out, sems)

        alog_r_, dtb_r_, alog_c_, dtb_c_ = ar[...], dr[...], ac[...], dc[...]
        lane2 = lax.broadcasted_iota(jnp.int32, (1, LANE), 1)
        acc = {}

        def chain(d, h, c, late, q_r, k_r, v_r, sh_ref, th_ref, do_r, dq_ref, masks, decs):
            incl, strict, _, _ = masks
            idx = d * NH + h
            sl = slice(h * DH, (h + 1) * DH)
            rows = slice(c * CH, (c + 1) * CH)
            qh, kh, vh = q_r[rows, sl], k_r[rows, sl], v_r[rows, sl]
            doh = do_r[rows, sl]
            gcol, grow, bcol, gl = _head_scalars(d, h, *decs)
            decay, kb, a, egc, rhs_u, rhs_w, qs, attn, etail = _chunk_local(qh, kh, vh, gcol, grow, bcol, gl, incl, strict)
            t = th_ref[c, h]
            s = sh_ref[c, h]
            sol = _mm3(t, jnp.concatenate([rhs_u, rhs_w], axis=1))
            u, w = sol[:, :DH], sol[:, DH:]
            q_dec = qs * egc
            k_tail = kh * etail
            egl = jnp.exp(gl)
            dq_dec = _mm(doh, s, NT)
            yield
            for _ in range(2 * late):
                yield
            ds_new = ds_scr[d, h]
            dv_new = _mm(attn, doh, TN) + _mm(k_tail, ds_new)
            dgl = jnp.sum(jnp.sum(ds_new * s, axis=0, keepdims=True), axis=-1, keepdims=True) * egl
            yield
            v_new = u - _mm(w, s)
            dw = -_mm(dv_new, s, NT)
            ds_scr[d, h] = ds_new * egl + _mm(q_dec, doh, TN) - _mm(w, dv_new, TN)
            yield
            dattn = jnp.where(incl, _mm(doh, v_new, NT), 0.0)
            dk_tail = _mm(v_new, ds_new, NT)
            dr = _mm3(t, jnp.concatenate([dv_new, dw], axis=1), TN)
            dr_u, dr_w = dr[:, :DH], dr[:, DH:]
            yield
            da = -jnp.where(strict, _mm3(dr, sol, NT), 0.0)
            nq = dattn * decay
            dqs = _mm(nq, kh) + dq_dec * egc
            dk = _mm(nq, qs, TN)
            yield
            dv = dr_u * bcol
            dbeta = jnp.sum(dr_u * vh, axis=-1, keepdims=True)
            dgc = jnp.sum(dr_w * rhs_w, axis=-1, keepdims=True)
            m = da * decay
            dkb = dr_w * egc + _mm(m, kh)
            dk = dk + _mm(m, kb, TN)
            pq = da * a + dattn * attn
            dgc = dgc + jnp.sum(pq, axis=-1, keepdims=True) + jnp.sum(dq_dec * q_dec, axis=-1, keepdims=True)
            dgr = -jnp.sum(pq, axis=0, keepdims=True)
            tt = jnp.sum(dk_tail * k_tail, axis=-1, keepdims=True)
            dk = dk + dk_tail * etail + dkb * bcol
            dgc = dgc - tt
            dgl = dgl + jnp.sum(tt, axis=0, keepdims=True)
            dbeta = dbeta + jnp.sum(dkb * kh, axis=-1, keepdims=True)
            dq_ref[rows, sl] = dqs * (DH ** -0.5)
            dq_ref[rows, D + h * DH:D + (h + 1) * DH] = dk
            dq_ref[rows, 2 * D + h * DH:2 * D + (h + 1) * DH] = dv
            acc.setdefault((d, c), []).append((idx, dgc, dgl, dbeta, dgr))

        dirs = ((qf, kf, vf, abf, abtf, sf_ref, tf_ref, dof, dqf_ref, dcf_ref, drf_ref),
                (qb, kb_, vb, abb, abtb, sb_ref, tb_ref, dob, dqb_ref, dcb_ref, drb_ref))
        chains, ctx_d = [], {}
        for d, (q_r, k_r, v_r, ab_r, abt_r, sh_ref, th_ref, do_r, dq_ref, _, _) in enumerate(dirs):
            masks = _masks(d)
            for pos, c in enumerate(reversed(range(SUB)) if d == 0 else range(SUB)):
                ab, abt = ab_r[c * CH:(c + 1) * CH, :], abt_r[c]
                g_full, beta_full, gc_full, gl_full, gt_full, gct = _decays(
                    d, ab, abt, alog_r_, dtb_r_, alog_c_, dtb_c_, masks[0], masks[2])
                ctx_d[(d, c)] = (masks, ab, abt, g_full, beta_full, gt_full)
                for h in range(NH):
                    chains.append(chain(d, h, c, pos, q_r, k_r, v_r, sh_ref, th_ref, do_r, dq_ref, masks,
                                        (beta_full, gc_full, gl_full, gct)))
        _lockstep(chains)
        for d, c in sorted(ctx_d):
            (incl, _, incl_t, _), ab, abt, g_full, beta_full, gt_full = ctx_d[(d, c)]
            dcol_ref, drow_ref = dirs[d][9], dirs[d][10]
            dgc_col = jnp.zeros((CH, LANE), F32)
            dgl_row = jnp.zeros((1, LANE), F32)
            dbeta_col = jnp.zeros((CH, LANE), F32)
            dgc_row = jnp.zeros((4 * NH, CH), F32)
            for idx, dgc, dgl, dbeta, dgr in acc[(d, c)]:
                oh = _lane_onehot(idx)
                dgc_col = dgc_col + dgc * oh
                dgl_row = dgl_row + dgl * oh
                dbeta_col = dbeta_col + dbeta * _lane_onehot(2 * NH + idx)
                ohc = (lax.broadcasted_iota(jnp.int32, (4 * NH, 1), 0) == idx).astype(F32)
                dgc_row = dgc_row + ohc * dgr
            dg_col = _mmh(incl_t.astype(F32), dgc_col) + dgl_row
            dg_row = _mmh(dgc_row, incl.astype(F32))
            sg_col = _sigmoid(ab + dtb_r_)
            da_col = dg_col * (-jnp.exp(alog_r_)) * sg_col
            dcol_ref[c * CH:(c + 1) * CH, :] = da_col + dbeta_col * beta_full * (1.0 - beta_full)
            da_row = dg_row * (-jnp.exp(alog_c_)) * _sigmoid(abt + dtb_c_)
            drow_ref[c] = da_row
            vcol_ref[0:1, :] += jnp.sum(dg_col * g_full, axis=0, keepdims=True)
            vcol_ref[1:2, :] += jnp.sum(da_col, axis=0, keepdims=True)
            rl = jnp.sum(dg_row * gt_full, axis=-1, keepdims=True)
            rd = jnp.sum(da_row, axis=-1, keepdims=True)
            vrow_ref[...] += jnp.where(lane2 == 0, rl, 0.0) + jnp.where(lane2 == 1, rd, 0.0)

        @pl.when(pl.program_id(0) == nch // 2)
        def _():
            xc.forward(x_in, x_out, sems)

        @pl.when(pl.program_id(0) == nch - 1)
        def _():
            xc.wait(x_in, x_out, sems)

    def row(c, col):
        return pl.BlockSpec((SUB * CH, D), lambda s: (c(s), col))

    def hist(c, n):
        return pl.BlockSpec((SUB, NH, n, n), lambda s: (c(s), 0, 0, 0))

    def chunk_in(c):
        return [row(c, 0), row(c, 1), row(c, 2), pl.BlockSpec((SUB * CH, LANE), lambda s: (c(s), 0)),
                pl.BlockSpec((SUB, 4 * NH, CH), lambda s: (c(s), 0, 0)), hist(c, DH), hist(c, CH), row(c, 0)]

    small = [_full((1, LANE)), _full((1, LANE)), _full((4 * NH, 1)), _full((4 * NH, 1))]
    return pl.pallas_call(
        body, name="gdn_bwd", grid=(nch,),
        out_shape=(_sds((lt, 3 * D)), _sds((lt, 3 * D)), _sds((lt, LANE)), _sds((lt, LANE)),
                   _sds((lt // CH, 4 * NH, CH)), _sds((lt // CH, 4 * NH, CH)), _sds((8, LANE)), _sds((4 * NH, LANE))) + xc.out_shape,
        in_specs=chunk_in(cf) + chunk_in(cb) + small + [_ANY] * nx,
        out_specs=(pl.BlockSpec((SUB * CH, 3 * D), lambda s: (cf(s), 0)), pl.BlockSpec((SUB * CH, 3 * D), lambda s: (cb(s), 0)),
                   pl.BlockSpec((SUB * CH, LANE), lambda s: (cf(s), 0)), pl.BlockSpec((SUB * CH, LANE), lambda s: (cb(s), 0)),
                   pl.BlockSpec((SUB, 4 * NH, CH), lambda s: (cf(s), 0, 0)), pl.BlockSpec((SUB, 4 * NH, CH), lambda s: (cb(s), 0, 0)),
                   _full((8, LANE)), _full((4 * NH, LANE))) + tuple([_ANY] * nx),
        scratch_shapes=[pltpu.VMEM((2, NH, DH, DH), F32)] + xc.scratch,
        compiler_params=_params(("arbitrary",)),
    )(qkv, qkv, qkv, pab, abt, s_f, t_f, do, qkv, qkv, qkv, pab, abt, s_b, t_b, do, alog_r, dtb_r, alog_c, dtb_c,
      *xc_arrays)


def _post(p, o_f, o_b, x, tgt, w_pa, w_pb, w_out, w_sp, w_spt, b_spb, ln_g, ln_b, g_on, g_post, gate_x, lc):
    lt = p.shape[0]
    l = x.shape[0]
    tm = GC
    nct = lc // tm

    def body(p_ref, of_ref, ob_ref, x_ref, t_ref, wpa, wpb, wout, wsp, wspt, bspb, lng_ref, lnb_ref, gon_ref, gpost_ref, gate_ref,
             dp_ref, do_ref, dy_ref, ya_ref, yb_ref, mg_ref, da_ref, db_ref, dout_ref, dwsp_ref, dbsp_ref, vec_ref):
        i = pl.program_id(0)

        @pl.when(i == 0)
        def _():
            dwsp_ref[...] = jnp.zeros_like(dwsp_ref)
            dbsp_ref[...] = jnp.zeros_like(dbsp_ref)
            vec_ref[...] = jnp.zeros_like(vec_ref)

        @pl.when(i < nct)
        def _():
            dp_ref[...] = jnp.zeros_like(dp_ref)
            do_ref[...] = jnp.zeros_like(do_ref)

        @pl.when(i >= nct)
        def _():
            lng, lnb, gon, gpost, gate = lng_ref[...], lnb_ref[...], gon_ref[...], gpost_ref[...], gate_ref[...]
            zb, ua, va, za, ga, gb = [p_ref[:, j * D:(j + 1) * D] for j in range(6)]
            o = of_ref[...] + ob_ref[...]
            szb, dszb = _silu_g(zb)
            nh_l, r_l = [], []
            for h in range(NH):
                oh = o[:, h * DH:(h + 1) * DH]
                r = lax.rsqrt(jnp.mean(oh * oh, axis=-1, keepdims=True) + EPS)
                nh_l.append(oh * r)
                r_l.append(r)
            nrm_b = jnp.concatenate(nh_l, axis=-1)
            gon_t = jnp.concatenate([gon] * NH, axis=-1)
            y_b = nrm_b * gon_t * szb
            u, du_dua = _gelu_g(ua)
            gv, dgv_dva = _gelu_g(va)
            xc = gv - jnp.mean(gv, axis=-1, keepdims=True)
            rs_ln = lax.rsqrt(jnp.mean(xc * xc, axis=-1, keepdims=True) + EPS)
            vhat = xc * rs_ln
            v = vhat * lng + lnb
            s_sp = jnp.concatenate(
                [_mm(wsp[g], v[:, g * DH:(g + 1) * DH]) + bspb[g] for g in range(NH)], axis=-1)
            sza, dsza = _silu_g(za)
            y_a = u * s_sp * sza
            a_pr = _mm(y_a, wpa[...])
            b_pr = _mm(y_b, wpb[...])
            sga = _sigmoid(ga)
            sgb = _sigmoid(gb)
            merged = sga * a_pr + sgb * b_pr
            out = _mm(merged, wout[...])
            rs_o = lax.rsqrt(jnp.mean(out * out, axis=-1, keepdims=True) + EPS)
            n_o = out * rs_o
            rr = n_o * gpost
            diff = x_ref[...] + gate * rr - t_ref[...]
            vec_ref[5:6, :] += jnp.sum(diff * diff, axis=0, keepdims=True)
            dy = diff * (1.0 / D)
            dy_ref[...] = dy
            vec_ref[0:1, :] += jnp.sum(dy * rr, axis=0, keepdims=True)
            dr = dy * gate
            vec_ref[1:2, :] += jnp.sum(dr * n_o, axis=0, keepdims=True)
            dn_o = dr * gpost
            dout = rs_o * (dn_o - n_o * jnp.mean(dn_o * n_o, axis=-1, keepdims=True))
            dmerged = _mm(dout, wout[...], NT)
            d_a = dmerged * sga
            d_b = dmerged * sgb
            dga = dmerged * a_pr * sga * (1.0 - sga)
            dgb = dmerged * b_pr * sgb * (1.0 - sgb)
            dy_a = _mm(d_a, wpa[...], NT)
            dy_b = _mm(d_b, wpb[...], NT)
            ya_ref[...] = y_a.astype(ya_ref.dtype).T
            yb_ref[...] = y_b.astype(yb_ref.dtype).T
            mg_ref[...] = merged.astype(mg_ref.dtype).T
            da_ref[...] = d_a.astype(da_ref.dtype)
            db_ref[...] = d_b.astype(db_ref.dtype)
            dout_ref[...] = dout.astype(dout_ref.dtype)
            dua = dy_a * s_sp * sza * du_dua
            ds_sp = dy_a * u * sza
            dza = dy_a * u * s_sp * dsza
            dv_l = []
            for g in range(NH):
                ds_g = ds_sp[:, g * DH:(g + 1) * DH]
                dv_l.append(_mm(wspt[g], ds_g))
                dwsp_ref[g] += _mm(ds_g, v[:, g * DH:(g + 1) * DH], NT)
                dbsp_ref[g] += ds_g
            dv = jnp.concatenate(dv_l, axis=-1)
            vec_ref[2:3, :] += jnp.sum(dv * vhat, axis=0, keepdims=True)
            vec_ref[3:4, :] += jnp.sum(dv, axis=0, keepdims=True)
            dvh = dv * lng
            dgv = rs_ln * (dvh - jnp.mean(dvh, axis=-1, keepdims=True) - vhat * jnp.mean(dvh * vhat, axis=-1, keepdims=True))
            dva = dgv * dgv_dva
            dzb = dy_b * nrm_b * gon_t * dszb
            dgon_full = jnp.sum(dy_b * nrm_b * szb, axis=0, keepdims=True)
            dgon = dgon_full[:, 0:DH]
            for h in range(1, NH):
                dgon = dgon + dgon_full[:, h * DH:(h + 1) * DH]
            vec_ref[4:5, 0:DH] += dgon
            dnb = dy_b * gon_t * szb
            do_l = []
            for h in range(NH):
                sl = slice(h * DH, (h + 1) * DH)
                dn_h = dnb[:, sl]
                do_l.append(r_l[h] * (dn_h - nh_l[h] * jnp.mean(dn_h * nh_l[h], axis=-1, keepdims=True)))
            do_ref[...] = jnp.concatenate(do_l, axis=-1)
            for j, val in enumerate((dzb, dua, dva, dza, dga, dgb)):
                dp_ref[:, j * D:(j + 1) * D] = val.astype(dp_ref.dtype)

    xrow = lambda i: (jnp.maximum(i - nct, 0), 0)
    wspec = _full((D, D))
    gspec = _full((NH, GC, GC))
    vspec = _full((1, D))
    bf_out = _sds((l, D), _BF)
    bf_out_t = _sds((D, l), _BF)
    xcol = lambda i: (0, jnp.maximum(i - nct, 0))
    return pl.pallas_call(
        body, name="post", grid=(lt // tm,),
        out_shape=(_sds((lt, NREST), _BF), _sds((lt, D)), _sds((l, D)), bf_out_t, bf_out_t, bf_out_t, bf_out, bf_out, bf_out,
                   _sds((NH, GC, GC)), _sds((NH, GC, GC)), _sds((8, D))),
        in_specs=[pl.BlockSpec((tm, NREST), lambda i: (i, 0)), pl.BlockSpec((tm, D), lambda i: (i, 0)),
                  pl.BlockSpec((tm, D), lambda i: (i, 0)), pl.BlockSpec((tm, D), xrow), pl.BlockSpec((tm, D), xrow),
                  wspec, wspec, wspec, gspec, gspec, gspec, vspec, vspec, _full((1, DH)), vspec, vspec],
        out_specs=(pl.BlockSpec((tm, NREST), lambda i: (i, 0)), pl.BlockSpec((tm, D), lambda i: (i, 0)),
                   pl.BlockSpec((tm, D), xrow), pl.BlockSpec((D, tm), xcol), pl.BlockSpec((D, tm), xcol),
                   pl.BlockSpec((D, tm), xcol), pl.BlockSpec((tm, D), xrow), pl.BlockSpec((tm, D), xrow),
                   pl.BlockSpec((tm, D), xrow), gspec, gspec, _full((8, D))),
        compiler_params=_params(("arbitrary",)),
    )(p, o_f, o_b, x, tgt, w_pa, w_pb, w_out, w_sp, w_spt, b_spb, ln_g, ln_b, g_on, g_post, gate_x)


def _sum_parts(parts, name):
    r = parts.shape[1]
    tr = r if NDEV * r * LANE * 4 <= (8 << 20) else _tile(r, (512, 256, 128, 64, 32, 16, 8))

    def body(p_ref, o_ref):
        acc = p_ref[0]
        for s in range(1, NDEV):
            acc = acc + p_ref[s]
        o_ref[...] = acc

    return pl.pallas_call(
        body, name=name, out_shape=_sds((r, LANE)), grid=(r // tr,),
        in_specs=[pl.BlockSpec((NDEV, tr, LANE), lambda i: (0, i, 0))],
        out_specs=pl.BlockSpec((tr, LANE), lambda i: (i, 0)),
        compiler_params=_params(("parallel",)),
    )(parts)


def _mod_bwd(c_all, c_ctx, dmx, dmc, w_mod_g):
    ws = w_mod_g.shape[2]

    def body(ca_ref, cc_ref, dsh_ref, dmx_ref, dmc_ref, dmc_sh_ref, w_ref, gw_ref, gc_ref, gb_ref):
        sc, _ = _silu_g(ca_ref[...])
        scc, dscc = _silu_g(cc_ref[...])
        dmc_tot = jnp.sum(dmc_ref[...], axis=0, keepdims=True)
        gb_ref[...] = jnp.sum(dmx_ref[...], axis=0, keepdims=True) + dmc_tot
        lhs = jnp.concatenate([sc, jnp.broadcast_to(scc, (8, D))], axis=0)
        rhs = jnp.concatenate([dsh_ref[...], dmc_sh_ref[...]], axis=0)
        gw_ref[...] = _mmh(lhs, rhs, TN)
        acc = jnp.zeros((8, D), F32)
        tot8 = jnp.broadcast_to(dmc_tot, (8, 3 * D))
        for j in range(NDEV):
            acc = acc + _mm(tot8[:, j * ws:(j + 1) * ws], w_ref[j], NT)
        gc_ref[...] = acc[0:1, :] * dscc

    return pl.pallas_call(
        body, name="mod_bwd", out_shape=(_sds((D, ws)), _sds((1, D)), _sds((1, 3 * D))),
        compiler_params=_params(),
    )(c_all, c_ctx, _my_cols(dmx, ws), dmx, dmc, _my_cols(dmc, ws), w_mod_g)


def _my_cols(a, ws):
    me = 4 * lax.axis_index("x") + 2 * lax.axis_index("y") + lax.axis_index("c")
    return lax.dynamic_slice_in_dim(a, me * ws, ws, axis=1)


def _pair_sum(mine, other, name):
    n, r, c = mine.shape
    tr = _tile(r, (256, 128, 64, 32, 16, 8))

    def body(a_ref, b_ref, o_ref):
        o_ref[...] = (a_ref[...].astype(F32) + b_ref[...].astype(F32)).astype(o_ref.dtype)

    blk = pl.BlockSpec((1, tr, c), lambda j, i: (j, i, 0))
    return pl.pallas_call(
        body, name=name, out_shape=_sds((n, r, c), mine.dtype), grid=(n, r // tr),
        in_specs=[blk, blk], out_specs=blk, compiler_params=_params(("parallel", "parallel")),
    )(mine, other)


def _adamw(parts, w, m, v, name, chip_sums_below=None):
    s_, r, c = parts.shape
    tr = _tile(r, (128, 64, 32, 16, 8)) if r * c * 4 > (1 << 20) else r
    c1 = 1.0 / (1.0 - ADAM_B1 ** ADAM_STEP)
    c2 = 1.0 / (1.0 - ADAM_B2 ** ADAM_STEP)

    def body(p_ref, w_ref, m_ref, v_ref, g_ref, d_ref, nm_ref, nv_ref):
        if chip_sums_below is None:
            part = lambda s: p_ref[s].astype(F32)
        else:
            core = lax.axis_index("c")
            me = 4 * lax.axis_index("x") + 2 * lax.axis_index("y") + core
            every = me >= chip_sums_below
            part = lambda s: jnp.where(every | (core == s % 2), p_ref[s].astype(F32), 0.0)
        g = part(0)
        for s in range(1, s_):
            g = g + part(s)
        m_new = ADAM_B1 * m_ref[...] + (1.0 - ADAM_B1) * g
        v_new = ADAM_B2 * v_ref[...] + (1.0 - ADAM_B2) * (g * g)
        g_ref[...] = g
        nm_ref[...] = m_new
        nv_ref[...] = v_new
        d_ref[...] = -ADAM_LR * ((m_new * c1) / (jnp.sqrt(v_new * c2) + ADAM_EPS) + ADAM_WD * w_ref[...])

    blk = pl.BlockSpec((tr, c), lambda i: (i, 0))
    o = _sds((r, c))
    return pl.pallas_call(
        body, name=name, out_shape=(o, o, o, o), grid=(r // tr,),
        in_specs=[pl.BlockSpec((s_, tr, c), lambda i: (0, i, 0)), blk, blk, blk],
        out_specs=(blk, blk, blk, blk),
        compiler_params=_params(("parallel",)),
    )(parts, w, m, v)


def _adamw_many(gs, ws, ms, vs, name):
    n = len(gs)
    c1 = 1.0 / (1.0 - ADAM_B1 ** ADAM_STEP)
    c2 = 1.0 / (1.0 - ADAM_B2 ** ADAM_STEP)

    def body(*refs):
        g_in, w_in_, m_in, v_in = (refs[k * n:(k + 1) * n] for k in range(4))
        g_out, d_out, m_out, v_out = (refs[(4 + k) * n:(5 + k) * n] for k in range(4))
        for p in range(n):
            g = g_in[p][...]
            m_new = ADAM_B1 * m_in[p][...] + (1.0 - ADAM_B1) * g
            v_new = ADAM_B2 * v_in[p][...] + (1.0 - ADAM_B2) * (g * g)
            g_out[p][...] = g
            m_out[p][...] = m_new
            v_out[p][...] = v_new
            d_out[p][...] = -ADAM_LR * ((m_new * c1) / (jnp.sqrt(v_new * c2) + ADAM_EPS) + ADAM_WD * w_in_[p][...])

    shapes = tuple(_sds(g.shape) for g in gs)
    res = pl.pallas_call(body, name=name, out_shape=shapes * 4, compiler_params=_params())(*gs, *ws, *ms, *vs)
    return [res[k * n:(k + 1) * n] for k in range(4)]


def _rows(a):
    flat = a.reshape(-1)
    n = flat.shape[0]
    r = -(-n // (8 * LANE)) * 8
    return jnp.pad(flat, (0, r * LANE - n)).reshape(r, LANE)


def _pack(items):
    parts, layout, at = [], [], 0
    for name, a in items:
        rws = _rows(a.astype(F32))
        layout.append((name, at, rws.shape[0], a.shape))
        parts.append(rws)
        at += rws.shape[0]
    return jnp.concatenate(parts, axis=0), layout


def _unpack(packed, layout):
    out = {}
    for name, at, r, shape in layout:
        n = 1
        for s in shape:
            n *= s
        out[name] = packed[at:at + r].reshape(-1)[:n].reshape(shape)
    return out


def kernel(x, c, ctx, c_ctx, w_mod, b_mod, g_pre, g_post, w_in, w_conv, a_log, dt_bias, g_onorm, gm_ln_g, gm_ln_b, w_sp, b_sp, w_pa, w_pb, w_out, loss_target, m_c_ctx, m_w_mod, m_b_mod, m_g_pre, m_g_post, m_w_in, m_w_conv, m_a_log, m_dt_bias, m_g_onorm, m_gm_ln_g, m_gm_ln_b, m_w_sp, m_b_sp, m_w_pa, m_w_pb, m_w_out, v_c_ctx, v_w_mod, v_b_mod, v_g_pre, v_g_post, v_w_in, v_w_conv, v_a_log, v_dt_bias, v_g_onorm, v_gm_ln_g, v_gm_ln_b, v_w_sp, v_b_sp, v_w_pa, v_w_pb, v_w_out):
    l = x.shape[1]
    lc = ctx.shape[1]
    lt = l + lc
    nch = lt // CH
    me = 4 * lax.axis_index("x") + 2 * lax.axis_index("y") + lax.axis_index("c")
    wsh = w_in.shape[2]
    off_a = 3 * D
    n_ab = 4 * NH
    jb = off_a // wsh
    o1 = off_a - jb * wsh
    o2 = o1 + n_ab
    assert o2 <= wsh and NREST == (NDEV - jb) * wsh - o2
    split = jb + 1

    w_in_bf = w_in[0].astype(_BF)
    wg_lo, wg_mod, wg_conv, c_all = _exchange(
        [w_in_bf, w_mod[0].astype(_BF), w_conv[0], c], ["gather_lo", "gather", "gather", "gather"],
        "gather_first", split)
    w_qkv = jnp.concatenate([wg_lo[j][:, :wsh] for j in range(jb)] + [wg_lo[jb][:, :o1]], axis=1)
    w_ab = jnp.pad(wg_lo[jb][:, o1:o2], ((0, 0), (0, LANE - n_ab)))
    wconv_full = jnp.moveaxis(wg_conv, 0, 1).reshape(3, 3 * D)
    c_all = c_all.reshape(NDEV, D)

    cc = jnp.concatenate([c, c_ctx.reshape(1, D), jnp.zeros((6, D), F32)], axis=0)
    mods = _modulation(cc, wg_mod, b_mod)
    h, h_t = _prenorm(ctx[0], x[0], mods, g_pre)
    p_qkv = _matmul_nn(h, w_qkv, "in_proj_qkv")
    pab = _matmul_nn(h, w_ab, "in_proj_ab")
    abt = jnp.swapaxes(pab[:, :n_ab].reshape(nch, CH, n_ab), 1, 2)
    alog16, dtb16 = a_log.reshape(1, 2 * NH), dt_bias.reshape(1, 2 * NH)
    alog_r = jnp.pad(alog16, ((0, 0), (0, LANE - 2 * NH)))
    dtb_r = jnp.pad(dtb16, ((0, 0), (0, LANE - 2 * NH)))
    alog_c = jnp.pad(alog16.reshape(2 * NH, 1), ((0, 2 * NH), (0, 0)))
    dtb_c = jnp.pad(dtb16.reshape(2 * NH, 1), ((0, 2 * NH), (0, 0)))
    qkv = _qkv_fwd(p_qkv, wconv_full, lc)
    late = [w_in_bf, w_pa[0].astype(_BF), w_pb[0].astype(_BF), w_out[0].astype(_BF)]
    xc_late = _Exchange(zip(late, ["gather_hi", "gather", "gather", "gather"]), split)
    o_f, o_b, s_f, s_b, t_f, t_b, wg_hi, wg_pa, wg_pb, wg_out = _gdn_fwd(
        qkv, pab, abt, alog_r, dtb_r, alog_c, dtb_c, lc, xc_late, late)
    w_rest = jnp.concatenate([wg_lo[jb][:, o2:wsh]] + [wg_hi[j][:, :wsh] for j in range(split, NDEV)], axis=1)
    wf_pa, wf_pb, wf_out = wg_pa.reshape(D, D), wg_pb.reshape(D, D), wg_out.reshape(D, D)
    p_rest = _matmul_nn(h, w_rest, "in_proj_rest")

    w_spt = jnp.swapaxes(w_sp[0], 1, 2)
    b_spb = jnp.broadcast_to(b_sp[0][:, :, None], (NH, GC, GC))
    gate_x = mods[0:1, 2 * D:]
    dp_rest, do, dy, ya, yb, mg, d_a, d_b, dout, dwsp, dbsp_l, pvec = _post(
        p_rest, o_f, o_b, x[0], loss_target[0], wf_pa, wf_pb, wf_out, w_sp[0], w_spt, b_spb, gm_ln_g, gm_ln_b,
        g_onorm, g_post, gate_x, lc)

    dw_rest = _matmul_nn(h_t, dp_rest, "dw_in_rest", _BF)
    o3 = wsh - o2
    chunks_hi = jnp.moveaxis(dw_rest[:, o3:].reshape(D, NDEV - split, wsh), 1, 0)
    dw_pa = _matmul_nn(ya, d_a, "dw_pa", _BF).reshape(NDEV, D // NDEV, D)
    dw_pb = _matmul_nn(yb, d_b, "dw_pb", _BF).reshape(NDEV, D // NDEV, D)
    dw_out = _matmul_nn(mg, dout, "dw_out", _BF).reshape(NDEV, D // NDEV, D)
    small_a, lay_a = _pack([
        ("g_post", pvec[1]), ("g_onorm", pvec[4, :DH]), ("gm_ln_g", pvec[2]), ("gm_ln_b", pvec[3]), ("w_sp", dwsp),
        ("b_sp", jnp.sum(dbsp_l, axis=-1)), ("loss", pvec[5]), ("dgate", pvec[0])])
    (theirs_hi,) = _exchange([chunks_hi], ["sibling"], "pair_swap_hi")
    chip_hi = _pair_sum(chunks_hi, theirs_hi[0], "pair_sum_hi")
    early = [chip_hi, dw_pa, dw_pb, dw_out, small_a]
    xc_early = _Exchange(zip(early, ["scatter_par_hi", "scatter", "scatter", "scatter", "gather"]), split)

    dqkv_f, dqkv_b, dcol_f, dcol_b, drow_f, drow_b, gvec_c, gvec_r, r_in, r_pa, r_pb, r_out, small_a_all = _gdn_bwd(
        qkv, pab, abt, alog_r, dtb_r, alog_c, dtb_c, s_f, s_b, t_f, t_b, do, lc, xc_early, early)
    dp_qkv, dwconv = _qkv_bwd(p_qkv, wconv_full, dqkv_f, dqkv_b, lc)
    drow = jnp.swapaxes(drow_f + drow_b, 1, 2).reshape(lt, n_ab)
    dpab = (dcol_f + dcol_b + jnp.pad(drow, ((0, 0), (0, LANE - n_ab)))).astype(_BF)

    dw_qkv = _matmul_nn(h_t, dp_qkv, "dw_in_qkv", _BF)
    dw_ab = _matmul_nn(h_t, dpab, "dw_in_ab", _BF)
    dw_lo = jnp.concatenate([dw_qkv, dw_ab[:, :n_ab], dw_rest[:, :o3]], axis=1)
    chunks_lo = jnp.moveaxis(dw_lo.reshape(D, split, wsh), 1, 0)
    (theirs,) = _exchange([chunks_lo], ["sibling"], "pair_swap")
    chip_lo = _pair_sum(chunks_lo, theirs[0], "pair_sum")
    xc_last = _Exchange([(chip_lo, "scatter_par_lo")], split)
    dh, r_in = _dh_matmul(dp_rest, dp_qkv, dpab, w_rest, w_qkv, w_ab, xc_last, [chip_lo], {0: r_in})
    grad_x, nvec = _prenorm_bwd(ctx[0], x[0], dh, dy, mods, g_pre)

    dalog = gvec_c[0, :2 * NH] + gvec_r[:2 * NH, 0]
    ddtb = gvec_c[1, :2 * NH] + gvec_r[:2 * NH, 1]
    small_b, lay_b = _pack([
        ("g_pre", nvec[4]), ("a_log", dalog), ("dt_bias", ddtb), ("w_conv", dwconv),
        ("dshift", nvec[0]), ("dscale", nvec[1]), ("dshift_c", nvec[2]), ("dscale_c", nvec[3])])
    (small_b_all,) = _exchange([small_b], ["gather"], "gather_small")
    tot = _unpack(_sum_parts(small_a_all, "sum_small_a"), lay_a)
    tot.update(_unpack(_sum_parts(small_b_all, "sum_small_b"), lay_b))

    def per_device(packed_all, layout, name):
        at, r = [(a_, r_) for nm, a_, r_, _ in layout if nm == name][0]
        return packed_all[:, at:at + r].reshape(NDEV, -1)

    dmx_all = jnp.concatenate([per_device(small_b_all, lay_b, "dshift"), per_device(small_b_all, lay_b, "dscale"),
                               per_device(small_a_all, lay_a, "dgate")], axis=1)
    dmc_all = jnp.concatenate([per_device(small_b_all, lay_b, "dshift_c"), per_device(small_b_all, lay_b, "dscale_c"),
                               jnp.zeros((NDEV, D), F32)], axis=1)
    g_wmod, g_cctx, g_bmod = _mod_bwd(c_all, c_ctx.reshape(1, D), dmx_all, dmc_all, wg_mod)
    loss = 0.5 / D * jnp.sum(tot["loss"])
    ws_conv = w_conv.shape[2]
    g_wconv = lax.dynamic_slice_in_dim(tot["w_conv"], me * ws_conv, ws_conv, axis=1)

    small_names = ["c_ctx", "b_mod", "g_pre", "g_post", "a_log", "dt_bias", "g_onorm", "gm_ln_g", "gm_ln_b",
                   "w_sp", "b_sp", "w_conv"]
    wts = dict(c_ctx=c_ctx, b_mod=b_mod, g_pre=g_pre, g_post=g_post, a_log=a_log, dt_bias=dt_bias, g_onorm=g_onorm,
               gm_ln_g=gm_ln_g, gm_ln_b=gm_ln_b, w_sp=w_sp, b_sp=b_sp, w_conv=w_conv)
    ms = dict(c_ctx=m_c_ctx, b_mod=m_b_mod, g_pre=m_g_pre, g_post=m_g_post, a_log=m_a_log, dt_bias=m_dt_bias,
              g_onorm=m_g_onorm, gm_ln_g=m_gm_ln_g, gm_ln_b=m_gm_ln_b, w_sp=m_w_sp, b_sp=m_b_sp, w_conv=m_w_conv)
    vs = dict(c_ctx=v_c_ctx, b_mod=v_b_mod, g_pre=v_g_pre, g_post=v_g_post, a_log=v_a_log, dt_bias=v_dt_bias,
              g_onorm=v_g_onorm, gm_ln_g=v_gm_ln_g, gm_ln_b=v_gm_ln_b, w_sp=v_w_sp, b_sp=v_b_sp, w_conv=v_w_conv)
    gs = dict(tot)
    gs.update(c_ctx=g_cctx, b_mod=g_bmod, w_conv=g_wconv)
    flat = lambda a: a.reshape(-1, a.shape[-1])
    res_small = [
        {nm: a.reshape(wts[nm].shape) for nm, a in zip(small_names, arrays)}
        for arrays in _adamw_many([flat(gs[nm].reshape(wts[nm].shape)) for nm in small_names],
                                  [flat(wts[nm]) for nm in small_names], [flat(ms[nm]) for nm in small_names],
                                  [flat(vs[nm]) for nm in small_names], "adamw_small")]
    res_big = {
        "w_mod": _adamw(g_wmod[None], w_mod[0], m_w_mod[0], v_w_mod[0], "adamw_w_mod"),
        "w_in": _adamw(r_in, w_in[0], m_w_in[0], v_w_in[0], "adamw_w_in", chip_sums_below=NDEV),
        "w_pa": _adamw(r_pa, w_pa[0], m_w_pa[0], v_w_pa[0], "adamw_w_pa"),
        "w_pb": _adamw(r_pb, w_pb[0], m_w_pb[0], v_w_pb[0], "adamw_w_pb"),
        "w_out": _adamw(r_out, w_out[0], m_w_out[0], v_w_out[0], "adamw_w_out"),
    }
    order = ["c_ctx", "w_mod", "b_mod", "g_pre", "g_post", "w_in", "w_conv", "a_log", "dt_bias", "g_onorm",
             "gm_ln_g", "gm_ln_b", "w_sp", "b_sp", "w_pa", "w_pb", "w_out"]
    outs = [loss, grad_x[None]]
    for k in range(4):
        for nm in order:
            if nm in res_big:
                outs.append(res_big[nm][k][None])
            else:
                outs.append(res_small[k][nm])
    return tuple(outs)
```

```python
import functools

import jax
import jax.numpy as jnp
from jax import lax
from jax.experimental import pallas as pl
from jax.experimental.pallas import tpu as pltpu

F32 = jnp.float32
_BF = jnp.bfloat16
_HI = lax.Precision.HIGHEST
D = 1024
NH = 8
DH = 128
CH = 64
SUB = 2
GC = 128
NREST = 6 * D
NMAIN = NREST + 3 * D
EPS = 1e-6
LANE = 128
NDEV = 8
VMEM_LIMIT = 56 * 1024 * 1024
MESH = pl.DeviceIdType.MESH

ADAM_LR, ADAM_B1, ADAM_B2, ADAM_EPS, ADAM_WD, ADAM_STEP = 0.001, 0.9, 0.999, 1e-08, 0.01, 10

NN = ((1,), (0,))
NT = ((1,), (1,))
TN = ((0,), (0,))


def _dot(a, b, dims=NN, prec=None):
    return lax.dot_general(a, b, (dims, ((), ())), precision=prec, preferred_element_type=F32)


def _mm(a, b, dims=NN):
    return _dot(a.astype(_BF), b.astype(_BF), dims)


def _mmh(a, b, dims=NN):
    return _dot(a.astype(F32), b.astype(F32), dims, _HI)


def _split(a):
    hi = a.astype(_BF)
    return hi, (a - hi.astype(F32)).astype(_BF)


def _mm3(a, b, dims=NN):
    ah, al = _split(a)
    bh, bl = _split(b)
    return _dot(ah, bh, dims) + (_dot(ah, bl, dims) + _dot(al, bh, dims))


def _sigmoid(x):
    return 1.0 / (1.0 + jnp.exp(-x))


def _silu_g(x):
    s = _sigmoid(x)
    return x * s, s * (1.0 + x * (1.0 - s))


def _gelu_g(x):
    c = 0.7978845608028654
    t = jnp.tanh(c * (x + 0.044715 * (x * x * x)))
    cdf = 0.5 * (1.0 + t)
    return x * cdf, cdf + 0.5 * x * (1.0 - t * t) * c * (1.0 + 3 * 0.044715 * x * x)


def _softplus(x):
    return jnp.maximum(x, 0.0) + jnp.log(1.0 + jnp.exp(-jnp.abs(x)))


def _params(sem=None):
    return pltpu.CompilerParams(dimension_semantics=sem, vmem_limit_bytes=VMEM_LIMIT)


def _tile(n, pref):
    for t in pref:
        if n % t == 0:
            return t
    return n


def _full(shape):
    nd = len(shape)
    return pl.BlockSpec(shape, lambda *_: (0,) * nd)


def _sds(shape, dt=F32):
    return jax.ShapeDtypeStruct(shape, dt)


MAX_PIECES = 12
PIECE_BYTES = 256 * 1024


def _piece_slices(shape, itemsize):
    total = itemsize
    for d in shape:
        total *= d
    want = min(MAX_PIECES, total // PIECE_BYTES)
    lead = shape[0] if len(shape) >= 3 else 1
    rows = shape[-2] if len(shape) >= 2 else 1
    if want < 2 or lead > want:
        return [()]
    m = max([n for n in (8, 4, 2, 1) if n * lead <= want and rows % (16 * n) == 0], default=1)
    if m * lead < 2:
        return [()]
    rs = rows // m
    mid = (slice(None),) * max(len(shape) - 3, 0)
    if len(shape) >= 3:
        return [(i,) + mid + (pl.ds(j * rs, rs),) for i in range(lead) for j in range(m)]
    return [(pl.ds(j * rs, rs),) for j in range(m)]


class _Pieces:
    def __init__(self, copies):
        self.copies = copies

    def start(self):
        for cp in self.copies:
            cp.start()

    def wait_send(self):
        for cp in self.copies:
            cp.wait_send()

    def wait_recv(self):
        for cp in self.copies:
            cp.wait_recv()

    def wait(self):
        for cp in self.copies:
            cp.wait()


class _Exchange:
    def __init__(self, specs, split):
        self.specs = list(specs)
        self.split = split
        self.n = len(self.specs)
        def out(a, k):
            if k == "sibling":
                return (1,) + tuple(a.shape)
            return (NDEV,) + (tuple(a.shape) if k.startswith("gather") else tuple(a.shape[1:]))

        self.out_shape = tuple(_sds(out(a, k), a.dtype) for a, k in self.specs)
        self.pieces = [_piece_slices(o.shape[1:], jnp.dtype(o.dtype).itemsize) for o in self.out_shape]
        self.sem_base = [(NDEV - 1) * sum(len(p) for p in self.pieces[:a]) for a in range(self.n + 1)]
        self.scratch = [pltpu.SemaphoreType.DMA((self.sem_base[-1],)), pltpu.SemaphoreType.DMA((self.sem_base[-1],)),
                        pltpu.SemaphoreType.DMA((self.sem_base[-1] // (NDEV - 1),))]

    def _local(self, sems, a, src, dst):
        base = self.sem_base[a] // (NDEV - 1)
        return _Pieces([pltpu.make_async_copy(src.at[sl] if sl else src, dst.at[sl] if sl else dst, sems[2].at[base + p])
                        for p, sl in enumerate(self.pieces[a])])

    def _remote(self, sems, a, k, src, dst, to):
        send_sems, recv_sems, _ = sems
        base = self.sem_base[a] + k * len(self.pieces[a])
        return _Pieces([
            pltpu.make_async_remote_copy(
                src_ref=src.at[sl] if sl else src, dst_ref=dst.at[sl] if sl else dst, send_sem=send_sems.at[base + p],
                recv_sem=recv_sems.at[base + p], device_id=to, device_id_type=MESH)
            for p, sl in enumerate(self.pieces[a])])

    def _ok(self, kind, idx):
        if kind.endswith("_lo"):
            return idx < self.split
        if kind.endswith("_hi"):
            return idx >= self.split
        return True

    def _phases(self, ins, outs, sems):
        x, y, c = lax.axis_index("x"), lax.axis_index("y"), lax.axis_index("c")
        me = 4 * x + 2 * y + c
        sib = (x, y, 1 - c)
        sib_idx = 4 * x + 2 * y + (1 - c)
        chips = [(1 - x, y), (x, 1 - y), (1 - x, 1 - y)]
        starts, forwards, waits = [], [], []
        for a, (_, kind) in enumerate(self.specs):
            ok = functools.partial(self._ok, kind)
            if kind.startswith("gather"):
                def copy(k, block, to, src=None, a=a):
                    rows = outs[a].at[block]
                    return self._remote(sems, a, k, rows if src is None else src, rows, to)

                loc = self._local(sems, a, ins[a], outs[a].at[me])
                first = [copy(0, me, sib, ins[a])] + [copy(1 + j, me, (*chip, c), ins[a]) for j, chip in enumerate(chips)]
                starts += [(ok(me), loc.start)] + [(ok(me), cp.start) for cp in first]
                waits += [(ok(me), loc.wait)] + [(ok(me), cp.wait_send) for cp in first]
                for j, chip in enumerate(chips):
                    origin = 4 * chip[0] + 2 * chip[1] + c
                    passed = copy(4 + j, origin, sib)
                    forwards += [(ok(origin), copy(1 + j, origin, sib).wait_recv), (ok(origin), passed.start)]
                    waits.append((ok(origin), passed.wait_send))
                    other = 4 * chip[0] + 2 * chip[1] + (1 - c)
                    waits.append((ok(other), copy(4 + j, other, sib).wait_recv))
                waits.append((ok(sib_idx), copy(0, sib_idx, sib).wait_recv))
            elif kind == "sibling":
                swap = self._remote(sems, a, 0, ins[a], outs[a].at[0], sib)
                starts.append((True, swap.start))
                waits += [(True, swap.wait_send), (True, swap.wait_recv)]
            else:
                base = self.split if kind.endswith("_hi") else 0
                same_core_only = "_par" in kind

                def src(idx, a=a, base=base):
                    return ins[a].at[jnp.clip(idx - base, 0, ins[a].shape[0] - 1)]

                loc = self._local(sems, a, src(me), outs[a].at[me])
                starts.append((ok(me), loc.start))
                waits.append((ok(me), loc.wait))
                for k in range(1, NDEV):
                    if same_core_only and k & 1:
                        continue
                    px = 1 - x if (k >> 2) & 1 else x
                    py = 1 - y if (k >> 1) & 1 else y
                    pc = 1 - c if k & 1 else c
                    pidx = 4 * px + 2 * py + pc
                    send = self._remote(sems, a, k - 1, src(pidx), outs[a].at[me], (px, py, pc))
                    arrive = self._remote(sems, a, k - 1, src(pidx), outs[a].at[pidx], (px, py, pc))
                    starts.append((ok(pidx), send.start))
                    waits += [(ok(pidx), send.wait_send), (ok(me), arrive.wait_recv)]
        return starts, forwards, waits

    @staticmethod
    def _run(actions):
        for cond, fn in actions:
            if cond is True:
                fn()
            else:
                pl.when(cond)(fn)

    def start(self, ins, outs, sems):
        self._run(self._phases(ins, outs, sems)[0])

    def forward(self, ins, outs, sems):
        self._run(self._phases(ins, outs, sems)[1])

    def wait(self, ins, outs, sems):
        self._run(self._phases(ins, outs, sems)[2])


_ANY = pl.BlockSpec(memory_space=pl.ANY)


def _exchange(arrays, kinds, name, split=0, into=None):
    xc = _Exchange(zip(arrays, kinds), split)
    n = xc.n
    into = into or {}
    ni = len(into)

    def body(*refs):
        ins, outs, sems = refs[:n], refs[n + ni:2 * n + ni], refs[2 * n + ni:]
        xc.start(ins, outs, sems)
        xc.forward(ins, outs, sems)
        xc.wait(ins, outs, sems)

    return pl.pallas_call(
        body, name=name, out_shape=xc.out_shape, in_specs=[_ANY] * (n + ni), out_specs=tuple([_ANY] * n),
        scratch_shapes=xc.scratch, input_output_aliases={n + t: a for t, a in enumerate(into)},
    )(*arrays, *into.values())


def _matmul_nn(a, b, name, out_dtype=F32):
    m, kk = a.shape
    n = b.shape[1]
    tm = m if m * kk * a.dtype.itemsize <= (12 << 20) else _tile(m, (1088, 1024, 640, 512, 256, 128))
    tn = _tile(n, (512, 256, 128))

    def body(a_ref, b_ref, o_ref):
        o_ref[...] = _mm(a_ref[...], b_ref[...]).astype(o_ref.dtype)

    return pl.pallas_call(
        body, name=name, out_shape=_sds((m, n), out_dtype), grid=(n // tn, m // tm),
        in_specs=[pl.BlockSpec((tm, kk), lambda j, i: (i, 0)), pl.BlockSpec((kk, tn), lambda j, i: (0, j))],
        out_specs=pl.BlockSpec((tm, tn), lambda j, i: (i, j)),
        compiler_params=_params(("parallel", "parallel")),
    )(a, b)


def _dh_matmul(dp_rest, dp_qkv, dpab, w_rest, w_qkv, w_ab, xc, xc_arrays, xc_into):
    lt = dp_rest.shape[0]
    tm = _tile(lt, (1088, 1024, 640, 512, 256, 128))
    nr, nq = dp_rest.shape[1] // D, dp_qkv.shape[1] // D
    nx, ni = xc.n, len(xc_into)
    ni_steps = lt // tm

    def body(*refs):
        dr_ref, dq_ref, ab_ref, wr_ref, wq_ref, wab_ref = refs[:6]
        x_in = refs[6:6 + nx]
        o_ref = refs[6 + nx + ni]
        x_out = refs[7 + nx + ni:7 + 2 * nx + ni]
        sems = refs[7 + 2 * nx + ni:]
        i = pl.program_id(0)
        k = pl.program_id(1)

        @pl.when((i == 0) & (k == 0))
        def _():
            xc.start(x_in, x_out, sems)

        @pl.when(k == 0)
        def _():
            o_ref[...] = _mm(ab_ref[...], wab_ref[...], NT)

        @pl.when(k < nr)
        def _():
            o_ref[...] += _mm(dr_ref[...], wr_ref[...], NT)

        @pl.when(k >= nr)
        def _():
            o_ref[...] += _mm(dq_ref[...], wq_ref[...], NT)

        @pl.when((i == ni_steps - 1) & (k == nr + nq - 1))
        def _():
            xc.wait(x_in, x_out, sems)

    rk = lambda k: jnp.minimum(k, nr - 1)
    qk = lambda k: jnp.maximum(k - nr, 0)
    return pl.pallas_call(
        body, name="dh_matmul", out_shape=(_sds((lt, D)),) + xc.out_shape, grid=(lt // tm, nr + nq),
        in_specs=[pl.BlockSpec((tm, D), lambda i, k: (i, rk(k))), pl.BlockSpec((tm, D), lambda i, k: (i, qk(k))),
                  pl.BlockSpec((tm, LANE), lambda i, k: (i, 0)),
                  pl.BlockSpec((D, D), lambda i, k: (0, rk(k))), pl.BlockSpec((D, D), lambda i, k: (0, qk(k))),
                  _full((D, LANE))] + [_ANY] * (nx + ni),
        out_specs=(pl.BlockSpec((tm, D), lambda i, k: (i, 0)),) + tuple([_ANY] * nx),
        scratch_shapes=xc.scratch, input_output_aliases={6 + nx + t: 1 + a for t, a in enumerate(xc_into)},
        compiler_params=_params(("arbitrary", "arbitrary")),
    )(dp_rest, dp_qkv, dpab, w_rest, w_qkv, w_ab, *xc_arrays, *xc_into.values())


def _modulation(cc, w_mod_g, b_mod):
    ws = w_mod_g.shape[2]

    def body(c_ref, w_ref, b_ref, o_ref):
        s, _ = _silu_g(c_ref[...])
        o_ref[...] = _mm(s, w_ref[0]) + b_ref[...]

    return pl.pallas_call(
        body, name="modulation", out_shape=_sds((8, 3 * D)), grid=(NDEV,),
        in_specs=[_full((8, D)), pl.BlockSpec((1, D, ws), lambda j: (j, 0, 0)), pl.BlockSpec((1, ws), lambda j: (0, j))],
        out_specs=pl.BlockSpec((8, ws), lambda j: (0, j)),
        compiler_params=_params(("parallel",)),
    )(cc, w_mod_g, b_mod)


def _prenorm(ctx, x, mods, g_pre):
    lc = ctx.shape[0]
    lt = lc + x.shape[0]
    tm = _tile(lc, (256, 128))
    nct = lc // tm

    def body(c_ref, x_ref, m_ref, g_ref, o_ref, ot_ref):
        is_ctx = pl.program_id(0) < nct
        x = jnp.where(is_ctx, c_ref[...], x_ref[...])
        shift = jnp.where(is_ctx, m_ref[1:2, 0:D], m_ref[0:1, 0:D])
        scale = jnp.where(is_ctx, m_ref[1:2, D:2 * D], m_ref[0:1, D:2 * D])
        r = lax.rsqrt(jnp.mean(x * x, axis=-1, keepdims=True) + EPS)
        h = ((x * r * g_ref[...]) * (1.0 + scale) + shift).astype(o_ref.dtype)
        o_ref[...] = h
        ot_ref[...] = h.T

    return pl.pallas_call(
        body, name="prenorm", out_shape=(_sds((lt, D), _BF), _sds((D, lt), _BF)), grid=(lt // tm,),
        in_specs=[pl.BlockSpec((tm, D), lambda i: (jnp.minimum(i, nct - 1), 0)),
                  pl.BlockSpec((tm, D), lambda i: (jnp.maximum(i - nct, 0), 0)), _full((8, 3 * D)), _full((1, D))],
        out_specs=(pl.BlockSpec((tm, D), lambda i: (i, 0)), pl.BlockSpec((D, tm), lambda i: (0, i))),
        compiler_params=_params(("parallel",)),
    )(ctx, x, mods, g_pre)


def _prenorm_bwd(ctx, x, dh, dy, mods, g_pre):
    lc = ctx.shape[0]
    lt = lc + x.shape[0]
    tm = _tile(lc, (256, 128))
    nct = lc // tm
    nl = (lt - lc) // tm

    def body(c_ref, x_ref, dh_ref, dy_ref, m_ref, g_ref, gx_ref, vec_ref):
        i = pl.program_id(0)

        @pl.when(i == 0)
        def _():
            vec_ref[...] = jnp.zeros_like(vec_ref)

        is_ctx = i < nct
        x = jnp.where(is_ctx, c_ref[...], x_ref[...])
        dh = dh_ref[...]
        g = g_ref[...]
        scale = jnp.where(is_ctx, m_ref[1:2, D:2 * D], m_ref[0:1, D:2 * D])
        r = lax.rsqrt(jnp.mean(x * x, axis=-1, keepdims=True) + EPS)
        n = x * r
        hn = n * g
        dsh = jnp.sum(dh, axis=0, keepdims=True)
        dsc = jnp.sum(dh * hn, axis=0, keepdims=True)
        dhn = dh * (1.0 + scale)
        vec_ref[4:5, :] += jnp.sum(dhn * n, axis=0, keepdims=True)
        dn = dhn * g
        dx = r * (dn - n * jnp.mean(dn * n, axis=-1, keepdims=True))

        @pl.when(is_ctx)
        def _():
            vec_ref[2:3, :] += dsh
            vec_ref[3:4, :] += dsc

        @pl.when(jnp.logical_not(is_ctx))
        def _():
            vec_ref[0:1, :] += dsh
            vec_ref[1:2, :] += dsc
            gx_ref[...] = dy_ref[...] + dx

    xrow = lambda i: (jnp.maximum(i - nct, 0), 0)
    return pl.pallas_call(
        body, name="prenorm_bwd", out_shape=(_sds((nl * tm, D)), _sds((8, D))), grid=(lt // tm,),
        in_specs=[pl.BlockSpec((tm, D), lambda i: (jnp.minimum(i, nct - 1), 0)), pl.BlockSpec((tm, D), xrow),
                  pl.BlockSpec((tm, D), lambda i: (i, 0)), pl.BlockSpec((tm, D), xrow), _full((8, 3 * D)), _full((1, D))],
        out_specs=(pl.BlockSpec((tm, D), xrow), _full((8, D))),
        compiler_params=_params(("arbitrary",)),
    )(ctx, x, dh, dy, mods, g_pre)


def _conv_parts(x, w, lc):
    lt = x.shape[0]
    row = lax.broadcasted_iota(jnp.int32, x.shape, 0)
    first = (row == 0) | (row == lc)
    last = (row == lc - 1) | (row == lt - 1)
    xp = jnp.where(first, 0.0, pltpu.roll(x, 1, 0))
    xn = jnp.where(last, 0.0, pltpu.roll(x, lt - 1, 0))
    y = w[0:1, :] * xp + w[1:2, :] * x + w[2:3, :] * xn
    return xp, xn, y, first, last


def _qkv_fwd(p, w_conv, lc):
    lt = p.shape[0]

    def body(p_ref, w_ref, o_ref):
        _, _, y, _, _ = _conv_parts(p_ref[...], w_ref[...], lc)
        s, _ = _silu_g(y)
        rs = lax.rsqrt(jnp.sum(s * s, axis=-1, keepdims=True) + EPS)
        o_ref[...] = s * jnp.where(pl.program_id(0) < 2 * NH, rs, 1.0)

    return pl.pallas_call(
        body, name="qkv_fwd", out_shape=_sds((lt, 3 * D)), grid=(3 * NH,),
        in_specs=[pl.BlockSpec((lt, DH), lambda j: (0, j)), pl.BlockSpec((3, DH), lambda j: (0, j))],
        out_specs=pl.BlockSpec((lt, DH), lambda j: (0, j)),
        compiler_params=_params(("parallel",)),
    )(p, w_conv)


def _qkv_bwd(p, w_conv, dqkv_f, dqkv_b, lc):
    lt = p.shape[0]

    def body(p_ref, w_ref, df_ref, db_ref, dp_ref, dw_ref):
        w = w_ref[...]
        xp, xn, y, first, last = _conv_parts(p_ref[...], w, lc)
        s, ds_dy = _silu_g(y)
        dn = df_ref[...] + db_ref[...]
        rs = lax.rsqrt(jnp.sum(s * s, axis=-1, keepdims=True) + EPS)
        nrm = s * rs
        ds_n = rs * (dn - nrm * jnp.sum(dn * nrm, axis=-1, keepdims=True))
        ds = jnp.where(pl.program_id(0) < 2 * NH, ds_n, dn)
        dy = ds * ds_dy
        dw_ref[0:1, :] = jnp.sum(dy * xp, axis=0, keepdims=True)
        dw_ref[1:2, :] = jnp.sum(dy * p_ref[...], axis=0, keepdims=True)
        dw_ref[2:3, :] = jnp.sum(dy * xn, axis=0, keepdims=True)
        dyn = jnp.where(last, 0.0, pltpu.roll(dy, lt - 1, 0))
        dyp = jnp.where(first, 0.0, pltpu.roll(dy, 1, 0))
        dp_ref[...] = (w[1:2, :] * dy + w[0:1, :] * dyn + w[2:3, :] * dyp).astype(dp_ref.dtype)

    return pl.pallas_call(
        body, name="qkv_bwd", out_shape=(_sds((lt, 3 * D), _BF), _sds((3, 3 * D))), grid=(3 * NH,),
        in_specs=[pl.BlockSpec((lt, DH), lambda j: (0, j)), pl.BlockSpec((3, DH), lambda j: (0, j)),
                  pl.BlockSpec((lt, DH), lambda j: (0, j)), pl.BlockSpec((lt, DH), lambda j: (0, j))],
        out_specs=(pl.BlockSpec((lt, DH), lambda j: (0, j)), pl.BlockSpec((3, DH), lambda j: (0, j))),
        compiler_params=_params(("parallel",)),
    )(p, w_conv, dqkv_f, dqkv_b)


def _masks(d):
    ri = lax.broadcasted_iota(jnp.int32, (CH, CH), 0)
    ci = lax.broadcasted_iota(jnp.int32, (CH, CH), 1)
    incl = (ri >= ci) if d == 0 else (ri <= ci)
    strict = (ri > ci) if d == 0 else (ri < ci)
    incl_t = (ri <= ci) if d == 0 else (ri >= ci)
    return incl, strict, incl_t, ri == ci


def _decays(d, ab, abt, alog_r, dtb_r, alog_c, dtb_c, incl, incl_t):
    g_full = -jnp.exp(alog_r) * _softplus(ab + dtb_r)
    beta_full = _sigmoid(ab)
    gc_full = _mmh(incl.astype(F32), g_full)
    gl_full = jnp.sum(g_full, axis=0, keepdims=True)
    gt_full = -jnp.exp(alog_c) * _softplus(abt + dtb_c)
    gct = _mmh(gt_full, incl_t.astype(F32))
    return g_full, beta_full, gc_full, gl_full, gt_full, gct


def _lane_onehot(idx, n=LANE):
    return (lax.broadcasted_iota(jnp.int32, (1, n), 1) == idx).astype(F32)


def _head_scalars(d, h, beta_full, gc_full, gl_full, gct):
    idx = d * NH + h
    oh = _lane_onehot(idx)
    gcol = jnp.sum(gc_full * oh, axis=-1, keepdims=True)
    bcol = jnp.sum(beta_full * _lane_onehot(2 * NH + idx), axis=-1, keepdims=True)
    gl = jnp.sum(gl_full * oh, axis=-1, keepdims=True)
    grow = gct[idx:idx + 1, :]
    return gcol, grow, bcol, gl


def _lockstep(gens):
    live = list(gens)
    while live:
        nxt = []
        for g in live:
            try:
                next(g)
                nxt.append(g)
            except StopIteration:
                pass
        live = nxt


def _chunk_local(qh, kh, vh, gcol, grow, bcol, gl, incl, strict):
    decay = jnp.where(incl, jnp.exp(gcol - grow), 0.0)
    kb = kh * bcol
    qs = qh * (DH ** -0.5)
    both = _mm(jnp.concatenate([kb, qs], axis=0), kh, NT)
    a = jnp.where(strict, both[:CH] * decay, 0.0)
    egc = jnp.exp(gcol)
    rhs_u = vh * bcol
    rhs_w = kb * egc
    attn = jnp.where(incl, both[CH:] * decay, 0.0)
    etail = jnp.exp(gl - gcol)
    return decay, kb, a, egc, rhs_u, rhs_w, qs, attn, etail


def _scan_specs(lt, lc, bwd_pass):
    assert lt % (SUB * CH) == 0 and lc % (SUB * CH) == 0
    nch = lt // (SUB * CH)
    ncc = lc // (SUB * CH)
    if not bwd_pass:
        cf = lambda s: s
        cb = lambda s: jnp.where(s < ncc, ncc - 1 - s, nch + ncc - 1 - s)
    else:
        cf = lambda s: nch - 1 - s
        cb = lambda s: jnp.where(s < nch - ncc, ncc + s, s - (nch - ncc))
    return nch, cf, cb


def _gdn_fwd(qkv, pab, abt, alog_r, dtb_r, alog_c, dtb_c, lc, xc, xc_arrays):
    lt = qkv.shape[0]
    nch, cf, cb = _scan_specs(lt, lc, False)
    nx = xc.n

    def body(*refs):
        qf, kf, vf, abf, abtf, qb, kb_, vb, abb, abtb, ar, dr, ac, dc = refs[:14]
        x_in = refs[14:14 + nx]
        of_ref, ob_ref, sf_ref, sb_ref, tf_ref, tb_ref = refs[14 + nx:20 + nx]
        x_out = refs[20 + nx:20 + 2 * nx]
        s_scr = refs[20 + 2 * nx]
        sems = refs[21 + 2 * nx:]

        @pl.when(pl.program_id(0) == 0)
        def _():
            s_scr[...] = jnp.zeros_like(s_scr)
            xc.start(x_in, x_out, sems)

        def chain(d, h, c, late, q_r, k_r, v_r, o_ref, sh_ref, th_ref, masks, decs):
            incl, strict, _, eye = masks
            sl = slice(h * DH, (h + 1) * DH)
            rows = slice(c * CH, (c + 1) * CH)
            qh, kh, vh = q_r[rows, sl], k_r[rows, sl], v_r[rows, sl]
            gcol, grow, bcol, gl = _head_scalars(d, h, *decs)
            _, _, a, egc, rhs_u, rhs_w, qs, attn, etail = _chunk_local(qh, kh, vh, gcol, grow, bcol, gl, incl, strict)
            yield
            n = -a
            t = jnp.where(eye, 1.0, 0.0) + n
            p = _mm3(n, n)
            yield
            for _ in range(4):
                r = _mm3(jnp.concatenate([t, p], axis=0), p)
                yield
                t = t + r[:CH]
                p = r[CH:]
            t = t + _mm3(t, p)
            yield
            for _ in range(3 * late):
                yield
            sol = _mm3(t, jnp.concatenate([rhs_u, rhs_w], axis=1))
            u, w = sol[:, :DH], sol[:, DH:]
            s = s_scr[d, h]
            sh_ref[c, h] = s
            th_ref[c, h] = t
            yield
            ws = _mm(jnp.concatenate([w, qs * egc], axis=0), s)
            yield
            v_new = u - ws[:CH]
            o_ref[rows, sl] = ws[CH:] + _mm(attn, v_new)
            s_scr[d, h] = s * jnp.exp(gl) + _mm(kh * etail, v_new, TN)

        chains = []
        for d, (q_r, k_r, v_r, ab_r, abt_r, o_ref, sh_ref, th_ref) in enumerate(
                ((qf, kf, vf, abf, abtf, of_ref, sf_ref, tf_ref), (qb, kb_, vb, abb, abtb, ob_ref, sb_ref, tb_ref))):
            masks = _masks(d)
            for pos, c in enumerate(range(SUB) if d == 0 else reversed(range(SUB))):
                _, beta_full, gc_full, gl_full, _, gct = _decays(
                    d, ab_r[c * CH:(c + 1) * CH, :], abt_r[c], ar[...], dr[...], ac[...], dc[...], masks[0], masks[2])
                for h in range(NH):
                    chains.append(chain(d, h, c, pos, q_r, k_r, v_r, o_ref, sh_ref, th_ref, masks,
                                        (beta_full, gc_full, gl_full, gct)))
        _lockstep(chains)

        @pl.when(pl.program_id(0) == nch // 2)
        def _():
            xc.forward(x_in, x_out, sems)

        @pl.when(pl.program_id(0) == nch - 1)
        def _():
            xc.wait(x_in, x_out, sems)

    def row(c, col):
        return pl.BlockSpec((SUB * CH, D), lambda s: (c(s), col))

    def chunk_in(c):
        return [row(c, 0), row(c, 1), row(c, 2), pl.BlockSpec((SUB * CH, LANE), lambda s: (c(s), 0)),
                pl.BlockSpec((SUB, 4 * NH, CH), lambda s: (c(s), 0, 0))]

    def hist(c, n):
        return pl.BlockSpec((SUB, NH, n, n), lambda s: (c(s), 0, 0, 0))

    small = [_full((1, LANE)), _full((1, LANE)), _full((4 * NH, 1)), _full((4 * NH, 1))]
    return pl.pallas_call(
        body, name="gdn_fwd", grid=(nch,),
        out_shape=(_sds((lt, D)), _sds((lt, D)), _sds((lt // CH, NH, DH, DH)), _sds((lt // CH, NH, DH, DH)),
                   _sds((lt // CH, NH, CH, CH)), _sds((lt // CH, NH, CH, CH))) + xc.out_shape,
        in_specs=chunk_in(cf) + chunk_in(cb) + small + [_ANY] * nx,
        out_specs=(pl.BlockSpec((SUB * CH, D), lambda s: (cf(s), 0)), pl.BlockSpec((SUB * CH, D), lambda s: (cb(s), 0)),
                   hist(cf, DH), hist(cb, DH), hist(cf, CH), hist(cb, CH)) + tuple([_ANY] * nx),
        scratch_shapes=[pltpu.VMEM((2, NH, DH, DH), F32)] + xc.scratch,
        compiler_params=_params(("arbitrary",)),
    )(qkv, qkv, qkv, pab, abt, qkv, qkv, qkv, pab, abt, alog_r, dtb_r, alog_c, dtb_c, *xc_arrays)


def _gdn_bwd(qkv, pab, abt, alog_r, dtb_r, alog_c, dtb_c, s_f, s_b, t_f, t_b, do, lc, xc, xc_arrays):
    lt = qkv.shape[0]
    nch, cf, cb = _scan_specs(lt, lc, True)
    nx = xc.n

    def body(*refs):
        qf, kf, vf, abf, abtf, sf_ref, tf_ref, dof, qb, kb_, vb, abb, abtb, sb_ref, tb_ref, dob, ar, dr, ac, dc = refs[:20]
        x_in = refs[20:20 + nx]
        dqf_ref, dqb_ref, dcf_ref, dcb_ref, drf_ref, drb_ref, vcol_ref, vrow_ref = refs[20 + nx:28 + nx]
        x_out = refs[28 + nx:28 + 2 * nx]
        ds_scr = refs[28 + 2 * nx]
        sems = refs[29 + 2 * nx:]

        @pl.when(pl.program_id(0) == 0)
        def _():
            ds_scr[...] = jnp.zeros_like(ds_scr)
            vcol_ref[...] = jnp.zeros_like(vcol_ref)
            vrow_ref[...] = jnp.zeros_like(vrow_ref)
            xc.start(x_in, x_out, sems)

        alog_r_, dtb_r_, alog_c_, dtb_c_ = ar[...], dr[...], ac[...], dc[...]
        lane2 = lax.broadcasted_iota(jnp.int32, (1, LANE), 1)
        acc = {}

        def chain(d, h, c, late, q_r, k_r, v_r, sh_ref, th_ref, do_r, dq_ref, masks, decs):
            incl, strict, _, _ = masks
            idx = d * NH + h
            sl = slice(h * DH, (h + 1) * DH)
            rows = slice(c * CH, (c + 1) * CH)
            qh, kh, vh = q_r[rows, sl], k_r[rows, sl], v_r[rows, sl]
            doh = do_r[rows, sl]
            gcol, grow, bcol, gl = _head_scalars(d, h, *decs)
            decay, kb, a, egc, rhs_u, rhs_w, qs, attn, etail = _chunk_local(qh, kh, vh, gcol, grow, bcol, gl, incl, strict)
            t = th_ref[c, h]
            s = sh_ref[c, h]
            sol = _mm3(t, jnp.concatenate([rhs_u, rhs_w], axis=1))
            u, w = sol[:, :DH], sol[:, DH:]
            q_dec = qs * egc
            k_tail = kh * etail
            egl = jnp.exp(gl)
            dq_dec = _mm(doh, s, NT)
            yield
            for _ in range(2 * late):
                yield
            ds_new = ds_scr[d, h]
            dv_new = _mm(attn, doh, TN) + _mm(k_tail, ds_new)
            dgl = jnp.sum(jnp.sum(ds_new * s, axis=0, keepdims=True), axis=-1, keepdims=True) * egl
            yield
            v_new = u - _mm(w, s)
            dw = -_mm(dv_new, s, NT)
            ds_scr[d, h] = ds_new * egl + _mm(q_dec, doh, TN) - _mm(w, dv_new, TN)
            yield
            dattn = jnp.where(incl, _mm(doh, v_new, NT), 0.0)
            dk_tail = _mm(v_new, ds_new, NT)
            dr = _mm3(t, jnp.concatenate([dv_new, dw], axis=1), TN)
            dr_u, dr_w = dr[:, :DH], dr[:, DH:]
            yield
            da = -jnp.where(strict, _mm3(dr, sol, NT), 0.0)
            nq = dattn * decay
            dqs = _mm(nq, kh) + dq_dec * egc
            dk = _mm(nq, qs, TN)
            yield
            dv = dr_u * bcol
            dbeta = jnp.sum(dr_u * vh, axis=-1, keepdims=True)
            dgc = jnp.sum(dr_w * rhs_w, axis=-1, keepdims=True)
            m = da * decay
            dkb = dr_w * egc + _mm(m, kh)
            dk = dk + _mm(m, kb, TN)
            pq = da * a + dattn * attn
            dgc = dgc + jnp.sum(pq, axis=-1, keepdims=True) + jnp.sum(dq_dec * q_dec, axis=-1, keepdims=True)
            dgr = -jnp.sum(pq, axis=0, keepdims=True)
            tt = jnp.sum(dk_tail * k_tail, axis=-1, keepdims=True)
            dk = dk + dk_tail * etail + dkb * bcol
            dgc = dgc - tt
            dgl = dgl + jnp.sum(tt, axis=0, keepdims=True)
            dbeta = dbeta + jnp.sum(dkb * kh, axis=-1, keepdims=True)
            dq_ref[rows, sl] = dqs * (DH ** -0.5)
            dq_ref[rows, D + h * DH:D + (h + 1) * DH] = dk
            dq_ref[rows, 2 * D + h * DH:2 * D + (h + 1) * DH] = dv
            acc.setdefault((d, c), []).append((idx, dgc, dgl, dbeta, dgr))

        dirs = ((qf, kf, vf, abf, abtf, sf_ref, tf_ref, dof, dqf_ref, dcf_ref, drf_ref),
                (qb, kb_, vb, abb, abtb, sb_ref, tb_ref, dob, dqb_ref, dcb_ref, drb_ref))
        chains, ctx_d = [], {}
        for d, (q_r, k_r, v_r, ab_r, abt_r, sh_ref, th_ref, do_r, dq_ref, _, _) in enumerate(dirs):
            masks = _masks(d)
            for pos, c in enumerate(reversed(range(SUB)) if d == 0 else range(SUB)):
                ab, abt = ab_r[c * CH:(c + 1) * CH, :], abt_r[c]
                g_full, beta_full, gc_full, gl_full, gt_full, gct = _decays(
                    d, ab, abt, alog_r_, dtb_r_, alog_c_, dtb_c_, masks[0], masks[2])
                ctx_d[(d, c)] = (masks, ab, abt, g_full, beta_full, gt_full)
                for h in range(NH):
                    chains.append(chain(d, h, c, pos, q_r, k_r, v_r, sh_ref, th_ref, do_r, dq_ref, masks,
                                        (beta_full, gc_full, gl_full, gct)))
        _lockstep(chains)
        for d, c in sorted(ctx_d):
            (incl, _, incl_t, _), ab, abt, g_full, beta_full, gt_full = ctx_d[(d, c)]
            dcol_ref, drow_ref = dirs[d][9], dirs[d][10]
            dgc_col = jnp.zeros((CH, LANE), F32)
            dgl_row = jnp.zeros((1, LANE), F32)
            dbeta_col = jnp.zeros((CH, LANE), F32)
            dgc_row = jnp.zeros((4 * NH, CH), F32)
            for idx, dgc, dgl, dbeta, dgr in acc[(d, c)]:
                oh = _lane_onehot(idx)
                dgc_col = dgc_col + dgc * oh
                dgl_row = dgl_row + dgl * oh
                dbeta_col = dbeta_col + dbeta * _lane_onehot(2 * NH + idx)
                ohc = (lax.broadcasted_iota(jnp.int32, (4 * NH, 1), 0) == idx).astype(F32)
                dgc_row = dgc_row + ohc * dgr
            dg_col = _mmh(incl_t.astype(F32), dgc_col) + dgl_row
            dg_row = _mmh(dgc_row, incl.astype(F32))
            sg_col = _sigmoid(ab + dtb_r_)
            da_col = dg_col * (-jnp.exp(alog_r_)) * sg_col
            dcol_ref[c * CH:(c + 1) * CH, :] = da_col + dbeta_col * beta_full * (1.0 - beta_full)
            da_row = dg_row * (-jnp.exp(alog_c_)) * _sigmoid(abt + dtb_c_)
            drow_ref[c] = da_row
            vcol_ref[0:1, :] += jnp.sum(dg_col * g_full, axis=0, keepdims=True)
            vcol_ref[1:2, :] += jnp.sum(da_col, axis=0, keepdims=True)
            rl = jnp.sum(dg_row * gt_full, axis=-1, keepdims=True)
            rd = jnp.sum(da_row, axis=-1, keepdims=True)
            vrow_ref[...] += jnp.where(lane2 == 0, rl, 0.0) + jnp.where(lane2 == 1, rd, 0.0)

        @pl.when(pl.program_id(0) == nch // 2)
        def _():
            xc.forward(x_in, x_out, sems)

        @pl.when(pl.program_id(0) == nch - 1)
        def _():
            xc.wait(x_in, x_out, sems)

    def row(c, col):
        return pl.BlockSpec((SUB * CH, D), lambda s: (c(s), col))

    def hist(c, n):
        return pl.BlockSpec((SUB, NH, n, n), lambda s: (c(s), 0, 0, 0))

    def chunk_in(c):
        return [row(c, 0), row(c, 1), row(c, 2), pl.BlockSpec((SUB * CH, LANE), lambda s: (c(s), 0)),
                pl.BlockSpec((SUB, 4 * NH, CH), lambda s: (c(s), 0, 0)), hist(c, DH), hist(c, CH), row(c, 0)]

    small = [_full((1, LANE)), _full((1, LANE)), _full((4 * NH, 1)), _full((4 * NH, 1))]
    return pl.pallas_call(
        body, name="gdn_bwd", grid=(nch,),
        out_shape=(_sds((lt, 3 * D)), _sds((lt, 3 * D)), _sds((lt, LANE)), _sds((lt, LANE)),
                   _sds((lt // CH, 4 * NH, CH)), _sds((lt // CH, 4 * NH, CH)), _sds((8, LANE)), _sds((4 * NH, LANE))) + xc.out_shape,
        in_specs=chunk_in(cf) + chunk_in(cb) + small + [_ANY] * nx,
        out_specs=(pl.BlockSpec((SUB * CH, 3 * D), lambda s: (cf(s), 0)), pl.BlockSpec((SUB * CH, 3 * D), lambda s: (cb(s), 0)),
                   pl.BlockSpec((SUB * CH, LANE), lambda s: (cf(s), 0)), pl.BlockSpec((SUB * CH, LANE), lambda s: (cb(s), 0)),
                   pl.BlockSpec((SUB, 4 * NH, CH), lambda s: (cf(s), 0, 0)), pl.BlockSpec((SUB, 4 * NH, CH), lambda s: (cb(s), 0, 0)),
                   _full((8, LANE)), _full((4 * NH, LANE))) + tuple([_ANY] * nx),
        scratch_shapes=[pltpu.VMEM((2, NH, DH, DH), F32)] + xc.scratch,
        compiler_params=_params(("arbitrary",)),
    )(qkv, qkv, qkv, pab, abt, s_f, t_f, do, qkv, qkv, qkv, pab, abt, s_b, t_b, do, alog_r, dtb_r, alog_c, dtb_c,
      *xc_arrays)


def _post(p, o_f, o_b, x, tgt, w_pa, w_pb, w_out, w_sp, w_spt, b_spb, ln_g, ln_b, g_on, g_post, gate_x, lc):
    lt = p.shape[0]
    l = x.shape[0]
    tm = GC
    nct = lc // tm

    def body(p_ref, of_ref, ob_ref, x_ref, t_ref, wpa, wpb, wout, wsp, wspt, bspb, lng_ref, lnb_ref, gon_ref, gpost_ref, gate_ref,
             dp_ref, do_ref, dy_ref, ya_ref, yb_ref, mg_ref, da_ref, db_ref, dout_ref, dwsp_ref, dbsp_ref, vec_ref):
        i = pl.program_id(0)

        @pl.when(i == 0)
        def _():
            dwsp_ref[...] = jnp.zeros_like(dwsp_ref)
            dbsp_ref[...] = jnp.zeros_like(dbsp_ref)
            vec_ref[...] = jnp.zeros_like(vec_ref)

        @pl.when(i < nct)
        def _():
            dp_ref[...] = jnp.zeros_like(dp_ref)
            do_ref[...] = jnp.zeros_like(do_ref)

        @pl.when(i >= nct)
        def _():
            lng, lnb, gon, gpost, gate = lng_ref[...], lnb_ref[...], gon_ref[...], gpost_ref[...], gate_ref[...]
            zb, ua, va, za, ga, gb = [p_ref[:, j * D:(j + 1) * D] for j in range(6)]
            o = of_ref[...] + ob_ref[...]
            szb, dszb = _silu_g(zb)
            nh_l, r_l = [], []
            for h in range(NH):
                oh = o[:, h * DH:(h + 1) * DH]
                r = lax.rsqrt(jnp.mean(oh * oh, axis=-1, keepdims=True) + EPS)
                nh_l.append(oh * r)
                r_l.append(r)
            nrm_b = jnp.concatenate(nh_l, axis=-1)
            gon_t = jnp.concatenate([gon] * NH, axis=-1)
            y_b = nrm_b * gon_t * szb
            u, du_dua = _gelu_g(ua)
            gv, dgv_dva = _gelu_g(va)
            xc = gv - jnp.mean(gv, axis=-1, keepdims=True)
            rs_ln = lax.rsqrt(jnp.mean(xc * xc, axis=-1, keepdims=True) + EPS)
            vhat = xc * rs_ln
            v = vhat * lng + lnb
            s_sp = jnp.concatenate(
                [_mm(wsp[g], v[:, g * DH:(g + 1) * DH]) + bspb[g] for g in range(NH)], axis=-1)
            sza, dsza = _silu_g(za)
            y_a = u * s_sp * sza
            a_pr = _mm(y_a, wpa[...])
            b_pr = _mm(y_b, wpb[...])
            sga = _sigmoid(ga)
            sgb = _sigmoid(gb)
            merged = sga * a_pr + sgb * b_pr
            out = _mm(merged, wout[...])
            rs_o = lax.rsqrt(jnp.mean(out * out, axis=-1, keepdims=True) + EPS)
            n_o = out * rs_o
            rr = n_o * gpost
            diff = x_ref[...] + gate * rr - t_ref[...]
            vec_ref[5:6, :] += jnp.sum(diff * diff, axis=0, keepdims=True)
            dy = diff * (1.0 / D)
            dy_ref[...] = dy
            vec_ref[0:1, :] += jnp.sum(dy * rr, axis=0, keepdims=True)
            dr = dy * gate
            vec_ref[1:2, :] += jnp.sum(dr * n_o, axis=0, keepdims=True)
            dn_o = dr * gpost
            dout = rs_o * (dn_o - n_o * jnp.mean(dn_o * n_o, axis=-1, keepdims=True))
            dmerged = _mm(dout, wout[...], NT)
            d_a = dmerged * sga
            d_b = dmerged * sgb
            dga = dmerged * a_pr * sga * (1.0 - sga)
            dgb = dmerged * b_pr * sgb * (1.0 - sgb)
            dy_a = _mm(d_a, wpa[...], NT)
            dy_b = _mm(d_b, wpb[...], NT)
            ya_ref[...] = y_a.astype(ya_ref.dtype).T
            yb_ref[...] = y_b.astype(yb_ref.dtype).T
            mg_ref[...] = merged.astype(mg_ref.dtype).T
            da_ref[...] = d_a.astype(da_ref.dtype)
            db_ref[...] = d_b.astype(db_ref.dtype)
            dout_ref[...] = dout.astype(dout_ref.dtype)
            dua = dy_a * s_sp * sza * du_dua
            ds_sp = dy_a * u * sza
            dza = dy_a * u * s_sp * dsza
            dv_l = []
            for g in range(NH):
                ds_g = ds_sp[:, g * DH:(g + 1) * DH]
                dv_l.append(_mm(wspt[g], ds_g))
                dwsp_ref[g] += _mm(ds_g, v[:, g * DH:(g + 1) * DH], NT)
                dbsp_ref[g] += ds_g
            dv = jnp.concatenate(dv_l, axis=-1)
            vec_ref[2:3, :] += jnp.sum(dv * vhat, axis=0, keepdims=True)
            vec_ref[3:4, :] += jnp.sum(dv, axis=0, keepdims=True)
            dvh = dv * lng
            dgv = rs_ln * (dvh - jnp.mean(dvh, axis=-1, keepdims=True) - vhat * jnp.mean(dvh * vhat, axis=-1, keepdims=True))
            dva = dgv * dgv_dva
            dzb = dy_b * nrm_b * gon_t * dszb
            dgon_full = jnp.sum(dy_b * nrm_b * szb, axis=0, keepdims=True)
            dgon = dgon_full[:, 0:DH]
            for h in range(1, NH):
                dgon = dgon + dgon_full[:, h * DH:(h + 1) * DH]
            vec_ref[4:5, 0:DH] += dgon
            dnb = dy_b * gon_t * szb
            do_l = []
            for h in range(NH):
                sl = slice(h * DH, (h + 1) * DH)
                dn_h = dnb[:, sl]
                do_l.append(r_l[h] * (dn_h - nh_l[h] * jnp.mean(dn_h * nh_l[h], axis=-1, keepdims=True)))
            do_ref[...] = jnp.concatenate(do_l, axis=-1)
            for j, val in enumerate((dzb, dua, dva, dza, dga, dgb)):
                dp_ref[:, j * D:(j + 1) * D] = val.astype(dp_ref.dtype)

    xrow = lambda i: (jnp.maximum(i - nct, 0), 0)
    wspec = _full((D, D))
    gspec = _full((NH, GC, GC))
    vspec = _full((1, D))
    bf_out = _sds((l, D), _BF)
    bf_out_t = _sds((D, l), _BF)
    xcol = lambda i: (0, jnp.maximum(i - nct, 0))
    return pl.pallas_call(
        body, name="post", grid=(lt // tm,),
        out_shape=(_sds((lt, NREST), _BF), _sds((lt, D)), _sds((l, D)), bf_out_t, bf_out_t, bf_out_t, bf_out, bf_out, bf_out,
                   _sds((NH, GC, GC)), _sds((NH, GC, GC)), _sds((8, D))),
        in_specs=[pl.BlockSpec((tm, NREST), lambda i: (i, 0)), pl.BlockSpec((tm, D), lambda i: (i, 0)),
                  pl.BlockSpec((tm, D), lambda i: (i, 0)), pl.BlockSpec((tm, D), xrow), pl.BlockSpec((tm, D), xrow),
                  wspec, wspec, wspec, gspec, gspec, gspec, vspec, vspec, _full((1, DH)), vspec, vspec],
        out_specs=(pl.BlockSpec((tm, NREST), lambda i: (i, 0)), pl.BlockSpec((tm, D), lambda i: (i, 0)),
                   pl.BlockSpec((tm, D), xrow), pl.BlockSpec((D, tm), xcol), pl.BlockSpec((D, tm), xcol),
                   pl.BlockSpec((D, tm), xcol), pl.BlockSpec((tm, D), xrow), pl.BlockSpec((tm, D), xrow),
                   pl.BlockSpec((tm, D), xrow), gspec, gspec, _full((8, D))),
        compiler_params=_params(("arbitrary",)),
    )(p, o_f, o_b, x, tgt, w_pa, w_pb, w_out, w_sp, w_spt, b_spb, ln_g, ln_b, g_on, g_post, gate_x)


def _sum_parts(parts, name):
    r = parts.shape[1]
    tr = r if NDEV * r * LANE * 4 <= (8 << 20) else _tile(r, (512, 256, 128, 64, 32, 16, 8))

    def body(p_ref, o_ref):
        acc = p_ref[0]
        for s in range(1, NDEV):
            acc = acc + p_ref[s]
        o_ref[...] = acc

    return pl.pallas_call(
        body, name=name, out_shape=_sds((r, LANE)), grid=(r // tr,),
        in_specs=[pl.BlockSpec((NDEV, tr, LANE), lambda i: (0, i, 0))],
        out_specs=pl.BlockSpec((tr, LANE), lambda i: (i, 0)),
        compiler_params=_params(("parallel",)),
    )(parts)


def _mod_bwd(c_all, c_ctx, dmx, dmc, w_mod_g):
    ws = w_mod_g.shape[2]

    def body(ca_ref, cc_ref, dsh_ref, dmx_ref, dmc_ref, dmc_sh_ref, w_ref, gw_ref, gc_ref, gb_ref):
        sc, _ = _silu_g(ca_ref[...])
        scc, dscc = _silu_g(cc_ref[...])
        dmc_tot = jnp.sum(dmc_ref[...], axis=0, keepdims=True)
        gb_ref[...] = jnp.sum(dmx_ref[...], axis=0, keepdims=True) + dmc_tot
        lhs = jnp.concatenate([sc, jnp.broadcast_to(scc, (8, D))], axis=0)
        rhs = jnp.concatenate([dsh_ref[...], dmc_sh_ref[...]], axis=0)
        gw_ref[...] = _mmh(lhs, rhs, TN)
        acc = jnp.zeros((8, D), F32)
        tot8 = jnp.broadcast_to(dmc_tot, (8, 3 * D))
        for j in range(NDEV):
            acc = acc + _mm(tot8[:, j * ws:(j + 1) * ws], w_ref[j], NT)
        gc_ref[...] = acc[0:1, :] * dscc

    return pl.pallas_call(
        body, name="mod_bwd", out_shape=(_sds((D, ws)), _sds((1, D)), _sds((1, 3 * D))),
        compiler_params=_params(),
    )(c_all, c_ctx, _my_cols(dmx, ws), dmx, dmc, _my_cols(dmc, ws), w_mod_g)


def _my_cols(a, ws):
    me = 4 * lax.axis_index("x") + 2 * lax.axis_index("y") + lax.axis_index("c")
    return lax.dynamic_slice_in_dim(a, me * ws, ws, axis=1)


def _pair_sum(mine, other, name):
    n, r, c = mine.shape
    tr = _tile(r, (256, 128, 64, 32, 16, 8))

    def body(a_ref, b_ref, o_ref):
        o_ref[...] = (a_ref[...].astype(F32) + b_ref[...].astype(F32)).astype(o_ref.dtype)

    blk = pl.BlockSpec((1, tr, c), lambda j, i: (j, i, 0))
    return pl.pallas_call(
        body, name=name, out_shape=_sds((n, r, c), mine.dtype), grid=(n, r // tr),
        in_specs=[blk, blk], out_specs=blk, compiler_params=_params(("parallel", "parallel")),
    )(mine, other)


def _adamw(parts, w, m, v, name, chip_sums_below=None):
    s_, r, c = parts.shape
    tr = _tile(r, (128, 64, 32, 16, 8)) if r * c * 4 > (1 << 20) else r
    c1 = 1.0 / (1.0 - ADAM_B1 ** ADAM_STEP)
    c2 = 1.0 / (1.0 - ADAM_B2 ** ADAM_STEP)

    def body(p_ref, w_ref, m_ref, v_ref, g_ref, d_ref, nm_ref, nv_ref):
        if chip_sums_below is None:
            part = lambda s: p_ref[s].astype(F32)
        else:
            core = lax.axis_index("c")
            me = 4 * lax.axis_index("x") + 2 * lax.axis_index("y") + core
            every = me >= chip_sums_below
            part = lambda s: jnp.where(every | (core == s % 2), p_ref[s].astype(F32), 0.0)
        g = part(0)
        for s in range(1, s_):
            g = g + part(s)
        m_new = ADAM_B1 * m_ref[...] + (1.0 - ADAM_B1) * g
        v_new = ADAM_B2 * v_ref[...] + (1.0 - ADAM_B2) * (g * g)
        g_ref[...] = g
        nm_ref[...] = m_new
        nv_ref[...] = v_new
        d_ref[...] = -ADAM_LR * ((m_new * c1) / (jnp.sqrt(v_new * c2) + ADAM_EPS) + ADAM_WD * w_ref[...])

    blk = pl.BlockSpec((tr, c), lambda i: (i, 0))
    o = _sds((r, c))
    return pl.pallas_call(
        body, name=name, out_shape=(o, o, o, o), grid=(r // tr,),
        in_specs=[pl.BlockSpec((s_, tr, c), lambda i: (0, i, 0)), blk, blk, blk],
        out_specs=(blk, blk, blk, blk),
        compiler_params=_params(("parallel",)),
    )(parts, w, m, v)


def _adamw_many(gs, ws, ms, vs, name):
    n = len(gs)
    c1 = 1.0 / (1.0 - ADAM_B1 ** ADAM_STEP)
    c2 = 1.0 / (1.0 - ADAM_B2 ** ADAM_STEP)

    def body(*refs):
        g_in, w_in_, m_in, v_in = (refs[k * n:(k + 1) * n] for k in range(4))
        g_out, d_out, m_out, v_out = (refs[(4 + k) * n:(5 + k) * n] for k in range(4))
        for p in range(n):
            g = g_in[p][...]
            m_new = ADAM_B1 * m_in[p][...] + (1.0 - ADAM_B1) * g
            v_new = ADAM_B2 * v_in[p][...] + (1.0 - ADAM_B2) * (g * g)
            g_out[p][...] = g
            m_out[p][...] = m_new
            v_out[p][...] = v_new
            d_out[p][...] = -ADAM_LR * ((m_new * c1) / (jnp.sqrt(v_new * c2) + ADAM_EPS) + ADAM_WD * w_in_[p][...])

    shapes = tuple(_sds(g.shape) for g in gs)
    res = pl.pallas_call(body, name=name, out_shape=shapes * 4, compiler_params=_params())(*gs, *ws, *ms, *vs)
    return [res[k * n:(k + 1) * n] for k in range(4)]


def _rows(a):
    flat = a.reshape(-1)
    n = flat.shape[0]
    r = -(-n // (8 * LANE)) * 8
    return jnp.pad(flat, (0, r * LANE - n)).reshape(r, LANE)


def _pack(items):
    parts, layout, at = [], [], 0
    for name, a in items:
        rws = _rows(a.astype(F32))
        layout.append((name, at, rws.shape[0], a.shape))
        parts.append(rws)
        at += rws.shape[0]
    return jnp.concatenate(parts, axis=0), layout


def _unpack(packed, layout):
    out = {}
    for name, at, r, shape in layout:
        n = 1
        for s in shape:
            n *= s
        out[name] = packed[at:at + r].reshape(-1)[:n].reshape(shape)
    return out


def kernel(x, c, ctx, c_ctx, w_mod, b_mod, g_pre, g_post, w_in, w_conv, a_log, dt_bias, g_onorm, gm_ln_g, gm_ln_b, w_sp, b_sp, w_pa, w_pb, w_out, loss_target, m_c_ctx, m_w_mod, m_b_mod, m_g_pre, m_g_post, m_w_in, m_w_conv, m_a_log, m_dt_bias, m_g_onorm, m_gm_ln_g, m_gm_ln_b, m_w_sp, m_b_sp, m_w_pa, m_w_pb, m_w_out, v_c_ctx, v_w_mod, v_b_mod, v_g_pre, v_g_post, v_w_in, v_w_conv, v_a_log, v_dt_bias, v_g_onorm, v_gm_ln_g, v_gm_ln_b, v_w_sp, v_b_sp, v_w_pa, v_w_pb, v_w_out):
    l = x.shape[1]
    lc = ctx.shape[1]
    lt = l + lc
    nch = lt // CH
    me = 4 * lax.axis_index("x") + 2 * lax.axis_index("y") + lax.axis_index("c")
    wsh = w_in.shape[2]
    off_a = 3 * D
    n_ab = 4 * NH
    jb = off_a // wsh
    o1 = off_a - jb * wsh
    o2 = o1 + n_ab
    assert o2 <= wsh and NREST == (NDEV - jb) * wsh - o2
    split = jb + 1

    w_in_bf = w_in[0].astype(_BF)
    wg_lo, wg_mod, wg_conv, c_all = _exchange(
        [w_in_bf, w_mod[0].astype(_BF), w_conv[0], c], ["gather_lo", "gather", "gather", "gather"],
        "gather_first", split)
    w_qkv = jnp.concatenate([wg_lo[j][:, :wsh] for j in range(jb)] + [wg_lo[jb][:, :o1]], axis=1)
    w_ab = jnp.pad(wg_lo[jb][:, o1:o2], ((0, 0), (0, LANE - n_ab)))
    wconv_full = jnp.moveaxis(wg_conv, 0, 1).reshape(3, 3 * D)
    c_all = c_all.reshape(NDEV, D)

    cc = jnp.concatenate([c, c_ctx.reshape(1, D), jnp.zeros((6, D), F32)], axis=0)
    mods = _modulation(cc, wg_mod, b_mod)
    h, h_t = _prenorm(ctx[0], x[0], mods, g_pre)
    p_qkv = _matmul_nn(h, w_qkv, "in_proj_qkv")
    pab = _matmul_nn(h, w_ab, "in_proj_ab")
    abt = jnp.swapaxes(pab[:, :n_ab].reshape(nch, CH, n_ab), 1, 2)
    alog16, dtb16 = a_log.reshape(1, 2 * NH), dt_bias.reshape(1, 2 * NH)
    alog_r = jnp.pad(alog16, ((0, 0), (0, LANE - 2 * NH)))
    dtb_r = jnp.pad(dtb16, ((0, 0), (0, LANE - 2 * NH)))
    alog_c = jnp.pad(alog16.reshape(2 * NH, 1), ((0, 2 * NH), (0, 0)))
    dtb_c = jnp.pad(dtb16.reshape(2 * NH, 1), ((0, 2 * NH), (0, 0)))
    qkv = _qkv_fwd(p_qkv, wconv_full, lc)
    late = [w_in_bf, w_pa[0].astype(_BF), w_pb[0].astype(_BF), w_out[0].astype(_BF)]
    xc_late = _Exchange(zip(late, ["gather_hi", "gather", "gather", "gather"]), split)
    o_f, o_b, s_f, s_b, t_f, t_b, wg_hi, wg_pa, wg_pb, wg_out = _gdn_fwd(
        qkv, pab, abt, alog_r, dtb_r, alog_c, dtb_c, lc, xc_late, late)
    w_rest = jnp.concatenate([wg_lo[jb][:, o2:wsh]] + [wg_hi[j][:, :wsh] for j in range(split, NDEV)], axis=1)
    wf_pa, wf_pb, wf_out = wg_pa.reshape(D, D), wg_pb.reshape(D, D), wg_out.reshape(D, D)
    p_rest = _matmul_nn(h, w_rest, "in_proj_rest")

    w_spt = jnp.swapaxes(w_sp[0], 1, 2)
    b_spb = jnp.broadcast_to(b_sp[0][:, :, None], (NH, GC, GC))
    gate_x = mods[0:1, 2 * D:]
    dp_rest, do, dy, ya, yb, mg, d_a, d_b, dout, dwsp, dbsp_l, pvec = _post(
        p_rest, o_f, o_b, x[0], loss_target[0], wf_pa, wf_pb, wf_out, w_sp[0], w_spt, b_spb, gm_ln_g, gm_ln_b,
        g_onorm, g_post, gate_x, lc)

    dw_rest = _matmul_nn(h_t, dp_rest, "dw_in_rest", _BF)
    o3 = wsh - o2
    chunks_hi = jnp.moveaxis(dw_rest[:, o3:].reshape(D, NDEV - split, wsh), 1, 0)
    dw_pa = _matmul_nn(ya, d_a, "dw_pa", _BF).reshape(NDEV, D // NDEV, D)
    dw_pb = _matmul_nn(yb, d_b, "dw_pb", _BF).reshape(NDEV, D // NDEV, D)
    dw_out = _matmul_nn(mg, dout, "dw_out", _BF).reshape(NDEV, D // NDEV, D)
    small_a, lay_a = _pack([
        ("g_post", pvec[1]), ("g_onorm", pvec[4, :DH]), ("gm_ln_g", pvec[2]), ("gm_ln_b", pvec[3]), ("w_sp", dwsp),
        ("b_sp", jnp.sum(dbsp_l, axis=-1)), ("loss", pvec[5]), ("dgate", pvec[0])])
    theirs_hi, theirs_pa, theirs_pb, theirs_out = _exchange(
        [chunks_hi, dw_pa, dw_pb, dw_out], ["sibling"] * 4, "pair_swap_hi")
    chip_hi = _pair_sum(chunks_hi, theirs_hi[0], "pair_sum_hi")
    chip_pa = _pair_sum(dw_pa, theirs_pa[0], "pair_sum_pa")
    chip_pb = _pair_sum(dw_pb, theirs_pb[0], "pair_sum_pb")
    chip_out = _pair_sum(dw_out, theirs_out[0], "pair_sum_out")
    early = [chip_hi, chip_pa, chip_pb, chip_out, small_a]
    xc_early = _Exchange(zip(early, ["scatter_par_hi", "scatter_par", "scatter_par", "scatter_par", "gather"]), split)

    dqkv_f, dqkv_b, dcol_f, dcol_b, drow_f, drow_b, gvec_c, gvec_r, r_in, r_pa, r_pb, r_out, small_a_all = _gdn_bwd(
        qkv, pab, abt, alog_r, dtb_r, alog_c, dtb_c, s_f, s_b, t_f, t_b, do, lc, xc_early, early)
    dp_qkv, dwconv = _qkv_bwd(p_qkv, wconv_full, dqkv_f, dqkv_b, lc)
    drow = jnp.swapaxes(drow_f + drow_b, 1, 2).reshape(lt, n_ab)
    dpab = (dcol_f + dcol_b + jnp.pad(drow, ((0, 0), (0, LANE - n_ab)))).astype(_BF)

    dw_qkv = _matmul_nn(h_t, dp_qkv, "dw_in_qkv", _BF)
    dw_ab = _matmul_nn(h_t, dpab, "dw_in_ab", _BF)
    dw_lo = jnp.concatenate([dw_qkv, dw_ab[:, :n_ab], dw_rest[:, :o3]], axis=1)
    chunks_lo = jnp.moveaxis(dw_lo.reshape(D, split, wsh), 1, 0)
    (theirs,) = _exchange([chunks_lo], ["sibling"], "pair_swap")
    chip_lo = _pair_sum(chunks_lo, theirs[0], "pair_sum")
    xc_last = _Exchange([(chip_lo, "scatter_par_lo")], split)
    dh, r_in = _dh_matmul(dp_rest, dp_qkv, dpab, w_rest, w_qkv, w_ab, xc_last, [chip_lo], {0: r_in})
    grad_x, nvec = _prenorm_bwd(ctx[0], x[0], dh, dy, mods, g_pre)

    dalog = gvec_c[0, :2 * NH] + gvec_r[:2 * NH, 0]
    ddtb = gvec_c[1, :2 * NH] + gvec_r[:2 * NH, 1]
    small_b, lay_b = _pack([
        ("g_pre", nvec[4]), ("a_log", dalog), ("dt_bias", ddtb), ("w_conv", dwconv),
        ("dshift", nvec[0]), ("dscale", nvec[1]), ("dshift_c", nvec[2]), ("dscale_c", nvec[3])])
    (small_b_all,) = _exchange([small_b], ["gather"], "gather_small")
    tot = _unpack(_sum_parts(small_a_all, "sum_small_a"), lay_a)
    tot.update(_unpack(_sum_parts(small_b_all, "sum_small_b"), lay_b))

    def per_device(packed_all, layout, name):
        at, r = [(a_, r_) for nm, a_, r_, _ in layout if nm == name][0]
        return packed_all[:, at:at + r].reshape(NDEV, -1)

    dmx_all = jnp.concatenate([per_device(small_b_all, lay_b, "dshift"), per_device(small_b_all, lay_b, "dscale"),
                               per_device(small_a_all, lay_a, "dgate")], axis=1)
    dmc_all = jnp.concatenate([per_device(small_b_all, lay_b, "dshift_c"), per_device(small_b_all, lay_b, "dscale_c"),
                               jnp.zeros((NDEV, D), F32)], axis=1)
    g_wmod, g_cctx, g_bmod = _mod_bwd(c_all, c_ctx.reshape(1, D), dmx_all, dmc_all, wg_mod)
    loss = 0.5 / D * jnp.sum(tot["loss"])
    ws_conv = w_conv.shape[2]
    g_wconv = lax.dynamic_slice_in_dim(tot["w_conv"], me * ws_conv, ws_conv, axis=1)

    small_names = ["c_ctx", "b_mod", "g_pre", "g_post", "a_log", "dt_bias", "g_onorm", "gm_ln_g", "gm_ln_b",
                   "w_sp", "b_sp", "w_conv"]
    wts = dict(c_ctx=c_ctx, b_mod=b_mod, g_pre=g_pre, g_post=g_post, a_log=a_log, dt_bias=dt_bias, g_onorm=g_onorm,
               gm_ln_g=gm_ln_g, gm_ln_b=gm_ln_b, w_sp=w_sp, b_sp=b_sp, w_conv=w_conv)
    ms = dict(c_ctx=m_c_ctx, b_mod=m_b_mod, g_pre=m_g_pre, g_post=m_g_post, a_log=m_a_log, dt_bias=m_dt_bias,
              g_onorm=m_g_onorm, gm_ln_g=m_gm_ln_g, gm_ln_b=m_gm_ln_b, w_sp=m_w_sp, b_sp=m_b_sp, w_conv=m_w_conv)
    vs = dict(c_ctx=v_c_ctx, b_mod=v_b_mod, g_pre=v_g_pre, g_post=v_g_post, a_log=v_a_log, dt_bias=v_dt_bias,
              g_onorm=v_g_onorm, gm_ln_g=v_gm_ln_g, gm_ln_b=v_gm_ln_b, w_sp=v_w_sp, b_sp=v_b_sp, w_conv=v_w_conv)
    gs = dict(tot)
    gs.update(c_ctx=g_cctx, b_mod=g_bmod, w_conv=g_wconv)
    flat = lambda a: a.reshape(-1, a.shape[-1])
    res_small = [
        {nm: a.reshape(wts[nm].shape) for nm, a in zip(small_names, arrays)}
        for arrays in _adamw_many([flat(gs[nm].reshape(wts[nm].shape)) for nm in small_names],
                                  [flat(wts[nm]) for nm in small_names], [flat(ms[nm]) for nm in small_names],
                                  [flat(vs[nm]) for nm in small_names], "adamw_small")]
    res_big = {
        "w_mod": _adamw(g_wmod[None], w_mod[0], m_w_mod[0], v_w_mod[0], "adamw_w_mod"),
        "w_in": _adamw(r_in, w_in[0], m_w_in[0], v_w_in[0], "adamw_w_in", chip_sums_below=NDEV),
        "w_pa": _adamw(r_pa, w_pa[0], m_w_pa[0], v_w_pa[0], "adamw_w_pa", chip_sums_below=NDEV),
        "w_pb": _adamw(r_pb, w_pb[0], m_w_pb[0], v_w_pb[0], "adamw_w_pb", chip_sums_below=NDEV),
        "w_out": _adamw(r_out, w_out[0], m_w_out[0], v_w_out[0], "adamw_w_out", chip_sums_below=NDEV),
    }
    order = ["c_ctx", "w_mod", "b_mod", "g_pre", "g_post", "w_in", "w_conv", "a_log", "dt_bias", "g_onorm",
             "gm_ln_g", "gm_ln_b", "w_sp", "b_sp", "w_pa", "w_pb", "w_out"]
    outs = [loss, grad_x[None]]
    for k in range(4):
        for nm in order:
            if nm in res_big:
                outs.append(res_big[nm][k][None])
            else:
                outs.append(res_small[k][nm])
    return tuple(outs)
```

```python
import functools

import jax
import jax.numpy as jnp
from jax import lax
from jax.experimental import pallas as pl
from jax.experimental.pallas import tpu as pltpu

F32 = jnp.float32
_BF = jnp.bfloat16
_HI = lax.Precision.HIGHEST
D = 1024
NH = 8
DH = 128
CH = 64
SUB = 2
GC = 128
NREST = 6 * D
NMAIN = NREST + 3 * D
EPS = 1e-6
LANE = 128
NDEV = 8
VMEM_LIMIT = 56 * 1024 * 1024
MESH = pl.DeviceIdType.MESH

ADAM_LR, ADAM_B1, ADAM_B2, ADAM_EPS, ADAM_WD, ADAM_STEP = 0.001, 0.9, 0.999, 1e-08, 0.01, 10

NN = ((1,), (0,))
NT = ((1,), (1,))
TN = ((0,), (0,))


def _dot(a, b, dims=NN, prec=None):
    return lax.dot_general(a, b, (dims, ((), ())), precision=prec, preferred_element_type=F32)


def _mm(a, b, dims=NN):
    return _dot(a.astype(_BF), b.astype(_BF), dims)


def _mmh(a, b, dims=NN):
    return _dot(a.astype(F32), b.astype(F32), dims, _HI)


def _split(a):
    hi = a.astype(_BF)
    return hi, (a - hi.astype(F32)).astype(_BF)


def _mm3(a, b, dims=NN):
    ah, al = _split(a)
    bh, bl = _split(b)
    return _dot(ah, bh, dims) + (_dot(ah, bl, dims) + _dot(al, bh, dims))


def _sigmoid(x):
    return 1.0 / (1.0 + jnp.exp(-x))


def _silu_g(x):
    s = _sigmoid(x)
    return x * s, s * (1.0 + x * (1.0 - s))


def _gelu_g(x):
    c = 0.7978845608028654
    t = jnp.tanh(c * (x + 0.044715 * (x * x * x)))
    cdf = 0.5 * (1.0 + t)
    return x * cdf, cdf + 0.5 * x * (1.0 - t * t) * c * (1.0 + 3 * 0.044715 * x * x)


def _softplus(x):
    return jnp.maximum(x, 0.0) + jnp.log(1.0 + jnp.exp(-jnp.abs(x)))


def _params(sem=None):
    return pltpu.CompilerParams(dimension_semantics=sem, vmem_limit_bytes=VMEM_LIMIT)


def _tile(n, pref):
    for t in pref:
        if n % t == 0:
            return t
    return n


def _full(shape):
    nd = len(shape)
    return pl.BlockSpec(shape, lambda *_: (0,) * nd)


def _sds(shape, dt=F32):
    return jax.ShapeDtypeStruct(shape, dt)


MAX_PIECES = 12
PIECE_BYTES = 256 * 1024


def _piece_slices(shape, itemsize):
    total = itemsize
    for d in shape:
        total *= d
    want = min(MAX_PIECES, total // PIECE_BYTES)
    lead = shape[0] if len(shape) >= 3 else 1
    rows = shape[-2] if len(shape) >= 2 else 1
    if want < 2 or lead > want:
        return [()]
    m = max([n for n in (8, 4, 2, 1) if n * lead <= want and rows % (16 * n) == 0], default=1)
    if m * lead < 2:
        return [()]
    rs = rows // m
    mid = (slice(None),) * max(len(shape) - 3, 0)
    if len(shape) >= 3:
        return [(i,) + mid + (pl.ds(j * rs, rs),) for i in range(lead) for j in range(m)]
    return [(pl.ds(j * rs, rs),) for j in range(m)]


class _Pieces:
    def __init__(self, copies):
        self.copies = copies

    def start(self):
        for cp in self.copies:
            cp.start()

    def wait_send(self):
        for cp in self.copies:
            cp.wait_send()

    def wait_recv(self):
        for cp in self.copies:
            cp.wait_recv()

    def wait(self):
        for cp in self.copies:
            cp.wait()


class _Exchange:
    def __init__(self, specs, split):
        self.specs = list(specs)
        self.split = split
        self.n = len(self.specs)
        def out(a, k):
            if k == "sibling":
                return (1,) + tuple(a.shape)
            return (NDEV,) + (tuple(a.shape) if k.startswith("gather") else tuple(a.shape[1:]))

        self.out_shape = tuple(_sds(out(a, k), a.dtype) for a, k in self.specs)
        self.pieces = [_piece_slices(o.shape[1:], jnp.dtype(o.dtype).itemsize) for o in self.out_shape]
        self.sem_base = [(NDEV - 1) * sum(len(p) for p in self.pieces[:a]) for a in range(self.n + 1)]
        self.scratch = [pltpu.SemaphoreType.DMA((self.sem_base[-1],)), pltpu.SemaphoreType.DMA((self.sem_base[-1],)),
                        pltpu.SemaphoreType.DMA((self.sem_base[-1] // (NDEV - 1),))]

    def _local(self, sems, a, src, dst):
        base = self.sem_base[a] // (NDEV - 1)
        return _Pieces([pltpu.make_async_copy(src.at[sl] if sl else src, dst.at[sl] if sl else dst, sems[2].at[base + p])
                        for p, sl in enumerate(self.pieces[a])])

    def _remote(self, sems, a, k, src, dst, to):
        send_sems, recv_sems, _ = sems
        base = self.sem_base[a] + k * len(self.pieces[a])
        return _Pieces([
            pltpu.make_async_remote_copy(
                src_ref=src.at[sl] if sl else src, dst_ref=dst.at[sl] if sl else dst, send_sem=send_sems.at[base + p],
                recv_sem=recv_sems.at[base + p], device_id=to, device_id_type=MESH)
            for p, sl in enumerate(self.pieces[a])])

    def _ok(self, kind, idx):
        if kind.endswith("_lo"):
            return idx < self.split
        if kind.endswith("_hi"):
            return idx >= self.split
        return True

    def _phases(self, ins, outs, sems):
        x, y, c = lax.axis_index("x"), lax.axis_index("y"), lax.axis_index("c")
        me = 4 * x + 2 * y + c
        sib = (x, y, 1 - c)
        sib_idx = 4 * x + 2 * y + (1 - c)
        chips = [(1 - x, y), (x, 1 - y), (1 - x, 1 - y)]
        starts, forwards, waits = [], [], []
        for a, (_, kind) in enumerate(self.specs):
            ok = functools.partial(self._ok, kind)
            if kind.startswith("gather"):
                def copy(k, block, to, src=None, a=a):
                    rows = outs[a].at[block]
                    return self._remote(sems, a, k, rows if src is None else src, rows, to)

                loc = self._local(sems, a, ins[a], outs[a].at[me])
                first = [copy(0, me, sib, ins[a])] + [copy(1 + j, me, (*chip, c), ins[a]) for j, chip in enumerate(chips)]
                starts += [(ok(me), loc.start)] + [(ok(me), cp.start) for cp in first]
                waits += [(ok(me), loc.wait)] + [(ok(me), cp.wait_send) for cp in first]
                for j, chip in enumerate(chips):
                    origin = 4 * chip[0] + 2 * chip[1] + c
                    passed = copy(4 + j, origin, sib)
                    forwards += [(ok(origin), copy(1 + j, origin, sib).wait_recv), (ok(origin), passed.start)]
                    waits.append((ok(origin), passed.wait_send))
                    other = 4 * chip[0] + 2 * chip[1] + (1 - c)
                    waits.append((ok(other), copy(4 + j, other, sib).wait_recv))
                waits.append((ok(sib_idx), copy(0, sib_idx, sib).wait_recv))
            elif kind == "sibling":
                swap = self._remote(sems, a, 0, ins[a], outs[a].at[0], sib)
                starts.append((True, swap.start))
                waits += [(True, swap.wait_send), (True, swap.wait_recv)]
            else:
                base = self.split if kind.endswith("_hi") else 0
                same_core_only = "_par" in kind

                def src(idx, a=a, base=base):
                    return ins[a].at[jnp.clip(idx - base, 0, ins[a].shape[0] - 1)]

                loc = self._local(sems, a, src(me), outs[a].at[me])
                starts.append((ok(me), loc.start))
                waits.append((ok(me), loc.wait))
                for k in range(1, NDEV):
                    if same_core_only and k & 1:
                        continue
                    px = 1 - x if (k >> 2) & 1 else x
                    py = 1 - y if (k >> 1) & 1 else y
                    pc = 1 - c if k & 1 else c
                    pidx = 4 * px + 2 * py + pc
                    send = self._remote(sems, a, k - 1, src(pidx), outs[a].at[me], (px, py, pc))
                    arrive = self._remote(sems, a, k - 1, src(pidx), outs[a].at[pidx], (px, py, pc))
                    starts.append((ok(pidx), send.start))
                    waits += [(ok(pidx), send.wait_send), (ok(me), arrive.wait_recv)]
        return starts, forwards, waits

    @staticmethod
    def _run(actions):
        for cond, fn in actions:
            if cond is True:
                fn()
            else:
                pl.when(cond)(fn)

    def start(self, ins, outs, sems):
        self._run(self._phases(ins, outs, sems)[0])

    def forward(self, ins, outs, sems):
        self._run(self._phases(ins, outs, sems)[1])

    def wait(self, ins, outs, sems):
        self._run(self._phases(ins, outs, sems)[2])


_ANY = pl.BlockSpec(memory_space=pl.ANY)


def _exchange(arrays, kinds, name, split=0, into=None):
    xc = _Exchange(zip(arrays, kinds), split)
    n = xc.n
    into = into or {}
    ni = len(into)

    def body(*refs):
        ins, outs, sems = refs[:n], refs[n + ni:2 * n + ni], refs[2 * n + ni:]
        xc.start(ins, outs, sems)
        xc.forward(ins, outs, sems)
        xc.wait(ins, outs, sems)

    return pl.pallas_call(
        body, name=name, out_shape=xc.out_shape, in_specs=[_ANY] * (n + ni), out_specs=tuple([_ANY] * n),
        scratch_shapes=xc.scratch, input_output_aliases={n + t: a for t, a in enumerate(into)},
    )(*arrays, *into.values())


def _matmul_nn(a, b, name, out_dtype=F32):
    m, kk = a.shape
    n = b.shape[1]
    tm = m if m * kk * a.dtype.itemsize <= (12 << 20) else _tile(m, (1088, 1024, 640, 512, 256, 128))
    tn = _tile(n, (512, 256, 128))

    def body(a_ref, b_ref, o_ref):
        o_ref[...] = _mm(a_ref[...], b_ref[...]).astype(o_ref.dtype)

    return pl.pallas_call(
        body, name=name, out_shape=_sds((m, n), out_dtype), grid=(n // tn, m // tm),
        in_specs=[pl.BlockSpec((tm, kk), lambda j, i: (i, 0)), pl.BlockSpec((kk, tn), lambda j, i: (0, j))],
        out_specs=pl.BlockSpec((tm, tn), lambda j, i: (i, j)),
        compiler_params=_params(("parallel", "parallel")),
    )(a, b)


def _dh_matmul(dp_rest, dp_qkv, dpab, w_rest, w_qkv, w_ab, xc, xc_arrays, xc_into):
    lt = dp_rest.shape[0]
    tm = _tile(lt, (1088, 1024, 640, 512, 256, 128))
    nr, nq = dp_rest.shape[1] // D, dp_qkv.shape[1] // D
    nx, ni = xc.n, len(xc_into)
    ni_steps = lt // tm

    def body(*refs):
        dr_ref, dq_ref, ab_ref, wr_ref, wq_ref, wab_ref = refs[:6]
        x_in = refs[6:6 + nx]
        o_ref = refs[6 + nx + ni]
        x_out = refs[7 + nx + ni:7 + 2 * nx + ni]
        sems = refs[7 + 2 * nx + ni:]
        i = pl.program_id(0)
        k = pl.program_id(1)

        @pl.when((i == 0) & (k == 0))
        def _():
            xc.start(x_in, x_out, sems)

        @pl.when(k == 0)
        def _():
            o_ref[...] = _mm(ab_ref[...], wab_ref[...], NT)

        @pl.when(k < nr)
        def _():
            o_ref[...] += _mm(dr_ref[...], wr_ref[...], NT)

        @pl.when(k >= nr)
        def _():
            o_ref[...] += _mm(dq_ref[...], wq_ref[...], NT)

        @pl.when((i == ni_steps - 1) & (k == nr + nq - 1))
        def _():
            xc.wait(x_in, x_out, sems)

    rk = lambda k: jnp.minimum(k, nr - 1)
    qk = lambda k: jnp.maximum(k - nr, 0)
    return pl.pallas_call(
        body, name="dh_matmul", out_shape=(_sds((lt, D)),) + xc.out_shape, grid=(lt // tm, nr + nq),
        in_specs=[pl.BlockSpec((tm, D), lambda i, k: (i, rk(k))), pl.BlockSpec((tm, D), lambda i, k: (i, qk(k))),
                  pl.BlockSpec((tm, LANE), lambda i, k: (i, 0)),
                  pl.BlockSpec((D, D), lambda i, k: (0, rk(k))), pl.BlockSpec((D, D), lambda i, k: (0, qk(k))),
                  _full((D, LANE))] + [_ANY] * (nx + ni),
        out_specs=(pl.BlockSpec((tm, D), lambda i, k: (i, 0)),) + tuple([_ANY] * nx),
        scratch_shapes=xc.scratch, input_output_aliases={6 + nx + t: 1 + a for t, a in enumerate(xc_into)},
        compiler_params=_params(("arbitrary", "arbitrary")),
    )(dp_rest, dp_qkv, dpab, w_rest, w_qkv, w_ab, *xc_arrays, *xc_into.values())


def _modulation(cc, w_mod_g, b_mod):
    ws = w_mod_g.shape[2]

    def body(c_ref, w_ref, b_ref, o_ref):
        s, _ = _silu_g(c_ref[...])
        o_ref[...] = _mm(s, w_ref[0]) + b_ref[...]

    return pl.pallas_call(
        body, name="modulation", out_shape=_sds((8, 3 * D)), grid=(NDEV,),
        in_specs=[_full((8, D)), pl.BlockSpec((1, D, ws), lambda j: (j, 0, 0)), pl.BlockSpec((1, ws), lambda j: (0, j))],
        out_specs=pl.BlockSpec((8, ws), lambda j: (0, j)),
        compiler_params=_params(("parallel",)),
    )(cc, w_mod_g, b_mod)


def _prenorm(ctx, x, mods, g_pre):
    lc = ctx.shape[0]
    lt = lc + x.shape[0]
    tm = _tile(lc, (256, 128))
    nct = lc // tm

    def body(c_ref, x_ref, m_ref, g_ref, o_ref, ot_ref):
        is_ctx = pl.program_id(0) < nct
        x = jnp.where(is_ctx, c_ref[...], x_ref[...])
        shift = jnp.where(is_ctx, m_ref[1:2, 0:D], m_ref[0:1, 0:D])
        scale = jnp.where(is_ctx, m_ref[1:2, D:2 * D], m_ref[0:1, D:2 * D])
        r = lax.rsqrt(jnp.mean(x * x, axis=-1, keepdims=True) + EPS)
        h = ((x * r * g_ref[...]) * (1.0 + scale) + shift).astype(o_ref.dtype)
        o_ref[...] = h
        ot_ref[...] = h.T

    return pl.pallas_call(
        body, name="prenorm", out_shape=(_sds((lt, D), _BF), _sds((D, lt), _BF)), grid=(lt // tm,),
        in_specs=[pl.BlockSpec((tm, D), lambda i: (jnp.minimum(i, nct - 1), 0)),
                  pl.BlockSpec((tm, D), lambda i: (jnp.maximum(i - nct, 0), 0)), _full((8, 3 * D)), _full((1, D))],
        out_specs=(pl.BlockSpec((tm, D), lambda i: (i, 0)), pl.BlockSpec((D, tm), lambda i: (0, i))),
        compiler_params=_params(("parallel",)),
    )(ctx, x, mods, g_pre)


def _prenorm_bwd(ctx, x, dh, dy, mods, g_pre):
    lc = ctx.shape[0]
    lt = lc + x.shape[0]
    tm = _tile(lc, (256, 128))
    nct = lc // tm
    nl = (lt - lc) // tm

    def body(c_ref, x_ref, dh_ref, dy_ref, m_ref, g_ref, gx_ref, vec_ref):
        i = pl.program_id(0)

        @pl.when(i == 0)
        def _():
            vec_ref[...] = jnp.zeros_like(vec_ref)

        is_ctx = i < nct
        x = jnp.where(is_ctx, c_ref[...], x_ref[...])
        dh = dh_ref[...]
        g = g_ref[...]
        scale = jnp.where(is_ctx, m_ref[1:2, D:2 * D], m_ref[0:1, D:2 * D])
        r = lax.rsqrt(jnp.mean(x * x, axis=-1, keepdims=True) + EPS)
        n = x * r
        hn = n * g
        dsh = jnp.sum(dh, axis=0, keepdims=True)
        dsc = jnp.sum(dh * hn, axis=0, keepdims=True)
        dhn = dh * (1.0 + scale)
        vec_ref[4:5, :] += jnp.sum(dhn * n, axis=0, keepdims=True)
        dn = dhn * g
        dx = r * (dn - n * jnp.mean(dn * n, axis=-1, keepdims=True))

        @pl.when(is_ctx)
        def _():
            vec_ref[2:3, :] += dsh
            vec_ref[3:4, :] += dsc

        @pl.when(jnp.logical_not(is_ctx))
        def _():
            vec_ref[0:1, :] += dsh
            vec_ref[1:2, :] += dsc
            gx_ref[...] = dy_ref[...] + dx

    xrow = lambda i: (jnp.maximum(i - nct, 0), 0)
    return pl.pallas_call(
        body, name="prenorm_bwd", out_shape=(_sds((nl * tm, D)), _sds((8, D))), grid=(lt // tm,),
        in_specs=[pl.BlockSpec((tm, D), lambda i: (jnp.minimum(i, nct - 1), 0)), pl.BlockSpec((tm, D), xrow),
                  pl.BlockSpec((tm, D), lambda i: (i, 0)), pl.BlockSpec((tm, D), xrow), _full((8, 3 * D)), _full((1, D))],
        out_specs=(pl.BlockSpec((tm, D), xrow), _full((8, D))),
        compiler_params=_params(("arbitrary",)),
    )(ctx, x, dh, dy, mods, g_pre)


def _conv_parts(x, w, lc):
    lt = x.shape[0]
    row = lax.broadcasted_iota(jnp.int32, x.shape, 0)
    first = (row == 0) | (row == lc)
    last = (row == lc - 1) | (row == lt - 1)
    xp = jnp.where(first, 0.0, pltpu.roll(x, 1, 0))
    xn = jnp.where(last, 0.0, pltpu.roll(x, lt - 1, 0))
    y = w[0:1, :] * xp + w[1:2, :] * x + w[2:3, :] * xn
    return xp, xn, y, first, last


def _qkv_fwd(p, w_conv, lc):
    lt = p.shape[0]

    def body(p_ref, w_ref, o_ref):
        _, _, y, _, _ = _conv_parts(p_ref[...], w_ref[...], lc)
        s = y * _sigmoid(y)

        @pl.when(pl.program_id(0) < 2 * NH)
        def _():
            o_ref[...] = s * lax.rsqrt(jnp.sum(s * s, axis=-1, keepdims=True) + EPS)

        @pl.when(pl.program_id(0) >= 2 * NH)
        def _():
            o_ref[...] = s

    return pl.pallas_call(
        body, name="qkv_fwd", out_shape=_sds((lt, 3 * D)), grid=(3 * NH,),
        in_specs=[pl.BlockSpec((lt, DH), lambda j: (0, j)), pl.BlockSpec((3, DH), lambda j: (0, j))],
        out_specs=pl.BlockSpec((lt, DH), lambda j: (0, j)),
        compiler_params=_params(("parallel",)),
    )(p, w_conv)


def _qkv_bwd(p, w_conv, dqkv_f, dqkv_b, lc):
    lt = p.shape[0]

    def body(p_ref, w_ref, df_ref, db_ref, dp_ref, dw_ref):
        w = w_ref[...]
        xp, xn, y, first, last = _conv_parts(p_ref[...], w, lc)
        s, ds_dy = _silu_g(y)
        dn = df_ref[...] + db_ref[...]
        rs = lax.rsqrt(jnp.sum(s * s, axis=-1, keepdims=True) + EPS)
        nrm = s * rs
        ds_n = rs * (dn - nrm * jnp.sum(dn * nrm, axis=-1, keepdims=True))
        ds = jnp.where(pl.program_id(0) < 2 * NH, ds_n, dn)
        dy = ds * ds_dy
        dw_ref[0:1, :] = jnp.sum(dy * xp, axis=0, keepdims=True)
        dw_ref[1:2, :] = jnp.sum(dy * p_ref[...], axis=0, keepdims=True)
        dw_ref[2:3, :] = jnp.sum(dy * xn, axis=0, keepdims=True)
        dyn = jnp.where(last, 0.0, pltpu.roll(dy, lt - 1, 0))
        dyp = jnp.where(first, 0.0, pltpu.roll(dy, 1, 0))
        dp_ref[...] = (w[1:2, :] * dy + w[0:1, :] * dyn + w[2:3, :] * dyp).astype(dp_ref.dtype)

    return pl.pallas_call(
        body, name="qkv_bwd", out_shape=(_sds((lt, 3 * D), _BF), _sds((3, 3 * D))), grid=(3 * NH,),
        in_specs=[pl.BlockSpec((lt, DH), lambda j: (0, j)), pl.BlockSpec((3, DH), lambda j: (0, j)),
                  pl.BlockSpec((lt, DH), lambda j: (0, j)), pl.BlockSpec((lt, DH), lambda j: (0, j))],
        out_specs=(pl.BlockSpec((lt, DH), lambda j: (0, j)), pl.BlockSpec((3, DH), lambda j: (0, j))),
        compiler_params=_params(("parallel",)),
    )(p, w_conv, dqkv_f, dqkv_b)


def _masks(d):
    ri = lax.broadcasted_iota(jnp.int32, (CH, CH), 0)
    ci = lax.broadcasted_iota(jnp.int32, (CH, CH), 1)
    incl = (ri >= ci) if d == 0 else (ri <= ci)
    strict = (ri > ci) if d == 0 else (ri < ci)
    incl_t = (ri <= ci) if d == 0 else (ri >= ci)
    return incl, strict, incl_t, ri == ci


def _decays(d, ab, abt, alog_r, dtb_r, alog_c, dtb_c, incl, incl_t):
    g_full = -jnp.exp(alog_r) * _softplus(ab + dtb_r)
    beta_full = _sigmoid(ab)
    gc_full = _mmh(incl.astype(F32), g_full)
    gl_full = jnp.sum(g_full, axis=0, keepdims=True)
    gt_full = -jnp.exp(alog_c) * _softplus(abt + dtb_c)
    gct = _mmh(gt_full, incl_t.astype(F32))
    return g_full, beta_full, gc_full, gl_full, gt_full, gct


def _lane_onehot(idx, n=LANE):
    return (lax.broadcasted_iota(jnp.int32, (1, n), 1) == idx).astype(F32)


def _head_scalars(d, h, beta_full, gc_full, gl_full, gct):
    idx = d * NH + h
    oh = _lane_onehot(idx)
    gcol = jnp.sum(gc_full * oh, axis=-1, keepdims=True)
    bcol = jnp.sum(beta_full * _lane_onehot(2 * NH + idx), axis=-1, keepdims=True)
    gl = jnp.sum(gl_full * oh, axis=-1, keepdims=True)
    grow = gct[idx:idx + 1, :]
    return gcol, grow, bcol, gl


def _lockstep(gens):
    live = list(gens)
    while live:
        nxt = []
        for g in live:
            try:
                next(g)
                nxt.append(g)
            except StopIteration:
                pass
        live = nxt


def _chunk_local(qh, kh, vh, gcol, grow, bcol, gl, incl, strict):
    decay = jnp.where(incl, jnp.exp(gcol - grow), 0.0)
    kb = kh * bcol
    qs = qh * (DH ** -0.5)
    both = _mm(jnp.concatenate([kb, qs], axis=0), kh, NT)
    a = jnp.where(strict, both[:CH] * decay, 0.0)
    egc = jnp.exp(gcol)
    rhs_u = vh * bcol
    rhs_w = kb * egc
    attn = jnp.where(incl, both[CH:] * decay, 0.0)
    etail = jnp.exp(gl - gcol)
    return decay, kb, a, egc, rhs_u, rhs_w, qs, attn, etail


def _scan_specs(lt, lc, bwd_pass):
    assert lt % (SUB * CH) == 0 and lc % (SUB * CH) == 0
    nch = lt // (SUB * CH)
    ncc = lc // (SUB * CH)
    if not bwd_pass:
        cf = lambda s: s
        cb = lambda s: jnp.where(s < ncc, ncc - 1 - s, nch + ncc - 1 - s)
    else:
        cf = lambda s: nch - 1 - s
        cb = lambda s: jnp.where(s < nch - ncc, ncc + s, s - (nch - ncc))
    return nch, cf, cb


def _gdn_fwd(qkv, pab, abt, alog_r, dtb_r, alog_c, dtb_c, lc, xc, xc_arrays):
    lt = qkv.shape[0]
    nch, cf, cb = _scan_specs(lt, lc, False)
    nx = xc.n

    def body(*refs):
        qf, kf, vf, abf, abtf, qb, kb_, vb, abb, abtb, ar, dr, ac, dc = refs[:14]
        x_in = refs[14:14 + nx]
        of_ref, ob_ref, sf_ref, sb_ref, tf_ref, tb_ref = refs[14 + nx:20 + nx]
        x_out = refs[20 + nx:20 + 2 * nx]
        s_scr = refs[20 + 2 * nx]
        sems = refs[21 + 2 * nx:]

        @pl.when(pl.program_id(0) == 0)
        def _():
            s_scr[...] = jnp.zeros_like(s_scr)
            xc.start(x_in, x_out, sems)

        def chain(d, h, c, late, q_r, k_r, v_r, o_ref, sh_ref, th_ref, masks, decs):
            incl, strict, _, eye = masks
            sl = slice(h * DH, (h + 1) * DH)
            rows = slice(c * CH, (c + 1) * CH)
            qh, kh, vh = q_r[rows, sl], k_r[rows, sl], v_r[rows, sl]
            gcol, grow, bcol, gl = _head_scalars(d, h, *decs)
            _, _, a, egc, rhs_u, rhs_w, qs, attn, etail = _chunk_local(qh, kh, vh, gcol, grow, bcol, gl, incl, strict)
            yield
            n = -a
            t = jnp.where(eye, 1.0, 0.0) + n
            p = _mm3(n, n)
            yield
            for _ in range(4):
                r = _mm3(jnp.concatenate([t, p], axis=0), p)
                yield
                t = t + r[:CH]
                p = r[CH:]
            t = t + _mm3(t, p)
            yield
            for _ in range(3 * late):
                yield
            sol = _mm3(t, jnp.concatenate([rhs_u, rhs_w], axis=1))
            u, w = sol[:, :DH], sol[:, DH:]
            s = s_scr[d, h]
            sh_ref[c, h] = s
            th_ref[c, h] = t
            yield
            ws = _mm(jnp.concatenate([w, qs * egc], axis=0), s)
            yield
            v_new = u - ws[:CH]
            o_ref[rows, sl] = ws[CH:] + _mm(attn, v_new)
            s_scr[d, h] = s * jnp.exp(gl) + _mm(kh * etail, v_new, TN)

        chains = []
        for d, (q_r, k_r, v_r, ab_r, abt_r, o_ref, sh_ref, th_ref) in enumerate(
                ((qf, kf, vf, abf, abtf, of_ref, sf_ref, tf_ref), (qb, kb_, vb, abb, abtb, ob_ref, sb_ref, tb_ref))):
            masks = _masks(d)
            for pos, c in enumerate(range(SUB) if d == 0 else reversed(range(SUB))):
                _, beta_full, gc_full, gl_full, _, gct = _decays(
                    d, ab_r[c * CH:(c + 1) * CH, :], abt_r[c], ar[...], dr[...], ac[...], dc[...], masks[0], masks[2])
                for h in range(NH):
                    chains.append(chain(d, h, c, pos, q_r, k_r, v_r, o_ref, sh_ref, th_ref, masks,
                                        (beta_full, gc_full, gl_full, gct)))
        _lockstep(chains)

        @pl.when(pl.program_id(0) == nch // 2)
        def _():
            xc.forward(x_in, x_out, sems)

        @pl.when(pl.program_id(0) == nch - 1)
        def _():
            xc.wait(x_in, x_out, sems)

    def row(c, col):
        return pl.BlockSpec((SUB * CH, D), lambda s: (c(s), col))

    def chunk_in(c):
        return [row(c, 0), row(c, 1), row(c, 2), pl.BlockSpec((SUB * CH, LANE), lambda s: (c(s), 0)),
                pl.BlockSpec((SUB, 4 * NH, CH), lambda s: (c(s), 0, 0))]

    def hist(c, n):
        return pl.BlockSpec((SUB, NH, n, n), lambda s: (c(s), 0, 0, 0))

    small = [_full((1, LANE)), _full((1, LANE)), _full((4 * NH, 1)), _full((4 * NH, 1))]
    return pl.pallas_call(
        body, name="gdn_fwd", grid=(nch,),
        out_shape=(_sds((lt, D)), _sds((lt, D)), _sds((lt // CH, NH, DH, DH)), _sds((lt // CH, NH, DH, DH)),
                   _sds((lt // CH, NH, CH, CH)), _sds((lt // CH, NH, CH, CH))) + xc.out_shape,
        in_specs=chunk_in(cf) + chunk_in(cb) + small + [_ANY] * nx,
        out_specs=(pl.BlockSpec((SUB * CH, D), lambda s: (cf(s), 0)), pl.BlockSpec((SUB * CH, D), lambda s: (cb(s), 0)),
                   hist(cf, DH), hist(cb, DH), hist(cf, CH), hist(cb, CH)) + tuple([_ANY] * nx),
        scratch_shapes=[pltpu.VMEM((2, NH, DH, DH), F32)] + xc.scratch,
        compiler_params=_params(("arbitrary",)),
    )(qkv, qkv, qkv, pab, abt, qkv, qkv, qkv, pab, abt, alog_r, dtb_r, alog_c, dtb_c, *xc_arrays)


def _gdn_bwd(qkv, pab, abt, alog_r, dtb_r, alog_c, dtb_c, s_f, s_b, t_f, t_b, do, lc, xc, xc_arrays):
    lt = qkv.shape[0]
    nch, cf, cb = _scan_specs(lt, lc, True)
    nx = xc.n

    def body(*refs):
        qf, kf, vf, abf, abtf, sf_ref, tf_ref, dof, qb, kb_, vb, abb, abtb, sb_ref, tb_ref, dob, ar, dr, ac, dc = refs[:20]
        x_in = refs[20:20 + nx]
        dqf_ref, dqb_ref, dcf_ref, dcb_ref, drf_ref, drb_ref, vcol_ref, vrow_ref = refs[20 + nx:28 + nx]
        x_out = refs[28 + nx:28 + 2 * nx]
        ds_scr = refs[28 + 2 * nx]
        sems = refs[29 + 2 * nx:]

        @pl.when(pl.program_id(0) == 0)
        def _():
            ds_scr[...] = jnp.zeros_like(ds_scr)
            vcol_ref[...] = jnp.zeros_like(vcol_ref)
            vrow_ref[...] = jnp.zeros_like(vrow_ref)
            xc.start(x_in, x_out, sems)

        alog_r_, dtb_r_, alog_c_, dtb_c_ = ar[...], dr[...], ac[...], dc[...]
        lane2 = lax.broadcasted_iota(jnp.int32, (1, LANE), 1)
        acc = {}

        def chain(d, h, c, late, q_r, k_r, v_r, sh_ref, th_ref, do_r, dq_ref, masks, decs):
            incl, strict, _, _ = masks
            idx = d * NH + h
            sl = slice(h * DH, (h + 1) * DH)
            rows = slice(c * CH, (c + 1) * CH)
            qh, kh, vh = q_r[rows, sl], k_r[rows, sl], v_r[rows, sl]
            doh = do_r[rows, sl]
            gcol, grow, bcol, gl = _head_scalars(d, h, *decs)
            decay, kb, a, egc, rhs_u, rhs_w, qs, attn, etail = _chunk_local(qh, kh, vh, gcol, grow, bcol, gl, incl, strict)
            t = th_ref[c, h]
            s = sh_ref[c, h]
            sol = _mm3(t, jnp.concatenate([rhs_u, rhs_w], axis=1))
            u, w = sol[:, :DH], sol[:, DH:]
            q_dec = qs * egc
            k_tail = kh * etail
            egl = jnp.exp(gl)
            dq_dec = _mm(doh, s, NT)
            yield
            for _ in range(2 * late):
                yield
            ds_new = ds_scr[d, h]
            dv_new = _mm(attn, doh, TN) + _mm(k_tail, ds_new)
            dgl = jnp.sum(jnp.sum(ds_new * s, axis=0, keepdims=True), axis=-1, keepdims=True) * egl
            yield
            v_new = u - _mm(w, s)
            dw = -_mm(dv_new, s, NT)
            ds_scr[d, h] = ds_new * egl + _mm(q_dec, doh, TN) - _mm(w, dv_new, TN)
            yield
            dattn = jnp.where(incl, _mm(doh, v_new, NT), 0.0)
            dk_tail = _mm(v_new, ds_new, NT)
            dr = _mm3(t, jnp.concatenate([dv_new, dw], axis=1), TN)
            dr_u, dr_w = dr[:, :DH], dr[:, DH:]
            yield
            da = -jnp.where(strict, _mm3(dr, sol, NT), 0.0)
            nq = dattn * decay
            dqs = _mm(nq, kh) + dq_dec * egc
            dk = _mm(nq, qs, TN)
            yield
            dv = dr_u * bcol
            dbeta = jnp.sum(dr_u * vh, axis=-1, keepdims=True)
            dgc = jnp.sum(dr_w * rhs_w, axis=-1, keepdims=True)
            m = da * decay
            dkb = dr_w * egc + _mm(m, kh)
            dk = dk + _mm(m, kb, TN)
            pq = da * a + dattn * attn
            dgc = dgc + jnp.sum(pq, axis=-1, keepdims=True) + jnp.sum(dq_dec * q_dec, axis=-1, keepdims=True)
            dgr = -jnp.sum(pq, axis=0, keepdims=True)
            tt = jnp.sum(dk_tail * k_tail, axis=-1, keepdims=True)
            dk = dk + dk_tail * etail + dkb * bcol
            dgc = dgc - tt
            dgl = dgl + jnp.sum(tt, axis=0, keepdims=True)
            dbeta = dbeta + jnp.sum(dkb * kh, axis=-1, keepdims=True)
            dq_ref[rows, sl] = dqs * (DH ** -0.5)
            dq_ref[rows, D + h * DH:D + (h + 1) * DH] = dk
            dq_ref[rows, 2 * D + h * DH:2 * D + (h + 1) * DH] = dv
            acc.setdefault((d, c), []).append((idx, dgc, dgl, dbeta, dgr))

        dirs = ((qf, kf, vf, abf, abtf, sf_ref, tf_ref, dof, dqf_ref, dcf_ref, drf_ref),
                (qb, kb_, vb, abb, abtb, sb_ref, tb_ref, dob, dqb_ref, dcb_ref, drb_ref))
        chains, ctx_d = [], {}
        for d, (q_r, k_r, v_r, ab_r, abt_r, sh_ref, th_ref, do_r, dq_ref, _, _) in enumerate(dirs):
            masks = _masks(d)
            for pos, c in enumerate(reversed(range(SUB)) if d == 0 else range(SUB)):
                ab, abt = ab_r[c * CH:(c + 1) * CH, :], abt_r[c]
                g_full, beta_full, gc_full, gl_full, gt_full, gct = _decays(
                    d, ab, abt, alog_r_, dtb_r_, alog_c_, dtb_c_, masks[0], masks[2])
                ctx_d[(d, c)] = (masks, ab, abt, g_full, beta_full, gt_full)
                for h in range(NH):
                    chains.append(chain(d, h, c, pos, q_r, k_r, v_r, sh_ref, th_ref, do_r, dq_ref, masks,
                                        (beta_full, gc_full, gl_full, gct)))
        _lockstep(chains)
        for d, c in sorted(ctx_d):
            (incl, _, incl_t, _), ab, abt, g_full, beta_full, gt_full = ctx_d[(d, c)]
            dcol_ref, drow_ref = dirs[d][9], dirs[d][10]
            dgc_col = jnp.zeros((CH, LANE), F32)
            dgl_row = jnp.zeros((1, LANE), F32)
            dbeta_col = jnp.zeros((CH, LANE), F32)
            dgc_row = jnp.zeros((4 * NH, CH), F32)
            for idx, dgc, dgl, dbeta, dgr in acc[(d, c)]:
                oh = _lane_onehot(idx)
                dgc_col = dgc_col + dgc * oh
                dgl_row = dgl_row + dgl * oh
                dbeta_col = dbeta_col + dbeta * _lane_onehot(2 * NH + idx)
                ohc = (lax.broadcasted_iota(jnp.int32, (4 * NH, 1), 0) == idx).astype(F32)
                dgc_row = dgc_row + ohc * dgr
            dg_col = _mmh(incl_t.astype(F32), dgc_col) + dgl_row
            dg_row = _mmh(dgc_row, incl.astype(F32))
            sg_col = _sigmoid(ab + dtb_r_)
            da_col = dg_col * (-jnp.exp(alog_r_)) * sg_col
            dcol_ref[c * CH:(c + 1) * CH, :] = da_col + dbeta_col * beta_full * (1.0 - beta_full)
            da_row = dg_row * (-jnp.exp(alog_c_)) * _sigmoid(abt + dtb_c_)
            drow_ref[c] = da_row
            vcol_ref[0:1, :] += jnp.sum(dg_col * g_full, axis=0, keepdims=True)
            vcol_ref[1:2, :] += jnp.sum(da_col, axis=0, keepdims=True)
            rl = jnp.sum(dg_row * gt_full, axis=-1, keepdims=True)
            rd = jnp.sum(da_row, axis=-1, keepdims=True)
            vrow_ref[...] += jnp.where(lane2 == 0, rl, 0.0) + jnp.where(lane2 == 1, rd, 0.0)

        @pl.when(pl.program_id(0) == nch // 2)
        def _():
            xc.forward(x_in, x_out, sems)

        @pl.when(pl.program_id(0) == nch - 1)
        def _():
            xc.wait(x_in, x_out, sems)

    def row(c, col):
        return pl.BlockSpec((SUB * CH, D), lambda s: (c(s), col))

    def hist(c, n):
        return pl.BlockSpec((SUB, NH, n, n), lambda s: (c(s), 0, 0, 0))

    def chunk_in(c):
        return [row(c, 0), row(c, 1), row(c, 2), pl.BlockSpec((SUB * CH, LANE), lambda s: (c(s), 0)),
                pl.BlockSpec((SUB, 4 * NH, CH), lambda s: (c(s), 0, 0)), hist(c, DH), hist(c, CH), row(c, 0)]

    small = [_full((1, LANE)), _full((1, LANE)), _full((4 * NH, 1)), _full((4 * NH, 1))]
    return pl.pallas_call(
        body, name="gdn_bwd", grid=(nch,),
        out_shape=(_sds((lt, 3 * D)), _sds((lt, 3 * D)), _sds((lt, LANE)), _sds((lt, LANE)),
                   _sds((lt // CH, 4 * NH, CH)), _sds((lt // CH, 4 * NH, CH)), _sds((8, LANE)), _sds((4 * NH, LANE))) + xc.out_shape,
        in_specs=chunk_in(cf) + chunk_in(cb) + small + [_ANY] * nx,
        out_specs=(pl.BlockSpec((SUB * CH, 3 * D), lambda s: (cf(s), 0)), pl.BlockSpec((SUB * CH, 3 * D), lambda s: (cb(s), 0)),
                   pl.BlockSpec((SUB * CH, LANE), lambda s: (cf(s), 0)), pl.BlockSpec((SUB * CH, LANE), lambda s: (cb(s), 0)),
                   pl.BlockSpec((SUB, 4 * NH, CH), lambda s: (cf(s), 0, 0)), pl.BlockSpec((SUB, 4 * NH, CH), lambda s: (cb(s), 0, 0)),
                   _full((8, LANE)), _full((4 * NH, LANE))) + tuple([_ANY] * nx),
        scratch_shapes=[pltpu.VMEM((2, NH, DH, DH), F32)] + xc.scratch,
        compiler_params=_params(("arbitrary",)),
    )(qkv, qkv, qkv, pab, abt, s_f, t_f, do, qkv, qkv, qkv, pab, abt, s_b, t_b, do, alog_r, dtb_r, alog_c, dtb_c,
      *xc_arrays)


def _post(p, o_f, o_b, x, tgt, w_pa, w_pb, w_out, w_sp, w_spt, b_spb, ln_g, ln_b, g_on, g_post, gate_x, lc):
    lt = p.shape[0]
    l = x.shape[0]
    tm = GC
    nct = lc // tm

    def body(p_ref, of_ref, ob_ref, x_ref, t_ref, wpa, wpb, wout, wsp, wspt, bspb, lng_ref, lnb_ref, gon_ref, gpost_ref, gate_ref,
             dp_ref, do_ref, dy_ref, ya_ref, yb_ref, mg_ref, da_ref, db_ref, dout_ref, dwsp_ref, dbsp_ref, vec_ref):
        i = pl.program_id(0)

        @pl.when(i == 0)
        def _():
            dwsp_ref[...] = jnp.zeros_like(dwsp_ref)
            dbsp_ref[...] = jnp.zeros_like(dbsp_ref)
            vec_ref[...] = jnp.zeros_like(vec_ref)

        @pl.when(i < nct)
        def _():
            dp_ref[...] = jnp.zeros_like(dp_ref)
            do_ref[...] = jnp.zeros_like(do_ref)

        @pl.when(i >= nct)
        def _():
            lng, lnb, gon, gpost, gate = lng_ref[...], lnb_ref[...], gon_ref[...], gpost_ref[...], gate_ref[...]
            zb, ua, va, za, ga, gb = [p_ref[:, j * D:(j + 1) * D] for j in range(6)]
            o = of_ref[...] + ob_ref[...]
            szb, dszb = _silu_g(zb)
            nh_l, r_l = [], []
            for h in range(NH):
                oh = o[:, h * DH:(h + 1) * DH]
                r = lax.rsqrt(jnp.mean(oh * oh, axis=-1, keepdims=True) + EPS)
                nh_l.append(oh * r)
                r_l.append(r)
            nrm_b = jnp.concatenate(nh_l, axis=-1)
            gon_t = jnp.concatenate([gon] * NH, axis=-1)
            y_b = nrm_b * gon_t * szb
            u, du_dua = _gelu_g(ua)
            gv, dgv_dva = _gelu_g(va)
            xc = gv - jnp.mean(gv, axis=-1, keepdims=True)
            rs_ln = lax.rsqrt(jnp.mean(xc * xc, axis=-1, keepdims=True) + EPS)
            vhat = xc * rs_ln
            v = vhat * lng + lnb
            s_sp = jnp.concatenate(
                [_mm(wsp[g], v[:, g * DH:(g + 1) * DH]) + bspb[g] for g in range(NH)], axis=-1)
            sza, dsza = _silu_g(za)
            y_a = u * s_sp * sza
            a_pr = _mm(y_a, wpa[...])
            b_pr = _mm(y_b, wpb[...])
            sga = _sigmoid(ga)
            sgb = _sigmoid(gb)
            merged = sga * a_pr + sgb * b_pr
            out = _mm(merged, wout[...])
            rs_o = lax.rsqrt(jnp.mean(out * out, axis=-1, keepdims=True) + EPS)
            n_o = out * rs_o
            rr = n_o * gpost
            diff = x_ref[...] + gate * rr - t_ref[...]
            vec_ref[5:6, :] += jnp.sum(diff * diff, axis=0, keepdims=True)
            dy = diff * (1.0 / D)
            dy_ref[...] = dy
            vec_ref[0:1, :] += jnp.sum(dy * rr, axis=0, keepdims=True)
            dr = dy * gate
            vec_ref[1:2, :] += jnp.sum(dr * n_o, axis=0, keepdims=True)
            dn_o = dr * gpost
            dout = rs_o * (dn_o - n_o * jnp.mean(dn_o * n_o, axis=-1, keepdims=True))
            dmerged = _mm(dout, wout[...], NT)
            d_a = dmerged * sga
            d_b = dmerged * sgb
            dga = dmerged * a_pr * sga * (1.0 - sga)
            dgb = dmerged * b_pr * sgb * (1.0 - sgb)
            dy_a = _mm(d_a, wpa[...], NT)
            dy_b = _mm(d_b, wpb[...], NT)
            ya_ref[...] = y_a.astype(ya_ref.dtype).T
            yb_ref[...] = y_b.astype(yb_ref.dtype).T
            mg_ref[...] = merged.astype(mg_ref.dtype).T
            da_ref[...] = d_a.astype(da_ref.dtype)
            db_ref[...] = d_b.astype(db_ref.dtype)
            dout_ref[...] = dout.astype(dout_ref.dtype)
            dua = dy_a * s_sp * sza * du_dua
            ds_sp = dy_a * u * sza
            dza = dy_a * u * s_sp * dsza
            dv_l = []
            for g in range(NH):
                ds_g = ds_sp[:, g * DH:(g + 1) * DH]
                dv_l.append(_mm(wspt[g], ds_g))
                dwsp_ref[g] += _mm(ds_g, v[:, g * DH:(g + 1) * DH], NT)
                dbsp_ref[g] += ds_g
            dv = jnp.concatenate(dv_l, axis=-1)
            vec_ref[2:3, :] += jnp.sum(dv * vhat, axis=0, keepdims=True)
            vec_ref[3:4, :] += jnp.sum(dv, axis=0, keepdims=True)
            dvh = dv * lng
            dgv = rs_ln * (dvh - jnp.mean(dvh, axis=-1, keepdims=True) - vhat * jnp.mean(dvh * vhat, axis=-1, keepdims=True))
            dva = dgv * dgv_dva
            dzb = dy_b * nrm_b * gon_t * dszb
            dgon_full = jnp.sum(dy_b * nrm_b * szb, axis=0, keepdims=True)
            dgon = dgon_full[:, 0:DH]
            for h in range(1, NH):
                dgon = dgon + dgon_full[:, h * DH:(h + 1) * DH]
            vec_ref[4:5, 0:DH] += dgon
            dnb = dy_b * gon_t * szb
            do_l = []
            for h in range(NH):
                sl = slice(h * DH, (h + 1) * DH)
                dn_h = dnb[:, sl]
                do_l.append(r_l[h] * (dn_h - nh_l[h] * jnp.mean(dn_h * nh_l[h], axis=-1, keepdims=True)))
            do_ref[...] = jnp.concatenate(do_l, axis=-1)
            for j, val in enumerate((dzb, dua, dva, dza, dga, dgb)):
                dp_ref[:, j * D:(j + 1) * D] = val.astype(dp_ref.dtype)

    xrow = lambda i: (jnp.maximum(i - nct, 0), 0)
    wspec = _full((D, D))
    gspec = _full((NH, GC, GC))
    vspec = _full((1, D))
    bf_out = _sds((l, D), _BF)
    bf_out_t = _sds((D, l), _BF)
    xcol = lambda i: (0, jnp.maximum(i - nct, 0))
    return pl.pallas_call(
        body, name="post", grid=(lt // tm,),
        out_shape=(_sds((lt, NREST), _BF), _sds((lt, D)), _sds((l, D)), bf_out_t, bf_out_t, bf_out_t, bf_out, bf_out, bf_out,
                   _sds((NH, GC, GC)), _sds((NH, GC, GC)), _sds((8, D))),
        in_specs=[pl.BlockSpec((tm, NREST), lambda i: (i, 0)), pl.BlockSpec((tm, D), lambda i: (i, 0)),
                  pl.BlockSpec((tm, D), lambda i: (i, 0)), pl.BlockSpec((tm, D), xrow), pl.BlockSpec((tm, D), xrow),
                  wspec, wspec, wspec, gspec, gspec, gspec, vspec, vspec, _full((1, DH)), vspec, vspec],
        out_specs=(pl.BlockSpec((tm, NREST), lambda i: (i, 0)), pl.BlockSpec((tm, D), lambda i: (i, 0)),
                   pl.BlockSpec((tm, D), xrow), pl.BlockSpec((D, tm), xcol), pl.BlockSpec((D, tm), xcol),
                   pl.BlockSpec((D, tm), xcol), pl.BlockSpec((tm, D), xrow), pl.BlockSpec((tm, D), xrow),
                   pl.BlockSpec((tm, D), xrow), gspec, gspec, _full((8, D))),
        compiler_params=_params(("arbitrary",)),
    )(p, o_f, o_b, x, tgt, w_pa, w_pb, w_out, w_sp, w_spt, b_spb, ln_g, ln_b, g_on, g_post, gate_x)


def _sum_parts(parts, name):
    r = parts.shape[1]
    tr = r if NDEV * r * LANE * 4 <= (8 << 20) else _tile(r, (512, 256, 128, 64, 32, 16, 8))

    def body(p_ref, o_ref):
        acc = p_ref[0]
        for s in range(1, NDEV):
            acc = acc + p_ref[s]
        o_ref[...] = acc

    return pl.pallas_call(
        body, name=name, out_shape=_sds((r, LANE)), grid=(r // tr,),
        in_specs=[pl.BlockSpec((NDEV, tr, LANE), lambda i: (0, i, 0))],
        out_specs=pl.BlockSpec((tr, LANE), lambda i: (i, 0)),
        compiler_params=_params(("parallel",)),
    )(parts)


def _mod_bwd(c_all, c_ctx, dmx, dmc, w_mod_g):
    ws = w_mod_g.shape[2]

    def body(ca_ref, cc_ref, dsh_ref, dmx_ref, dmc_ref, dmc_sh_ref, w_ref, gw_ref, gc_ref, gb_ref):
        sc, _ = _silu_g(ca_ref[...])
        scc, dscc = _silu_g(cc_ref[...])
        dmc_tot = jnp.sum(dmc_ref[...], axis=0, keepdims=True)
        gb_ref[...] = jnp.sum(dmx_ref[...], axis=0, keepdims=True) + dmc_tot
        lhs = jnp.concatenate([sc, jnp.broadcast_to(scc, (8, D))], axis=0)
        rhs = jnp.concatenate([dsh_ref[...], dmc_sh_ref[...]], axis=0)
        gw_ref[...] = _mmh(lhs, rhs, TN)
        acc = jnp.zeros((8, D), F32)
        tot8 = jnp.broadcast_to(dmc_tot, (8, 3 * D))
        for j in range(NDEV):
            acc = acc + _mm(tot8[:, j * ws:(j + 1) * ws], w_ref[j], NT)
        gc_ref[...] = acc[0:1, :] * dscc

    return pl.pallas_call(
        body, name="mod_bwd", out_shape=(_sds((D, ws)), _sds((1, D)), _sds((1, 3 * D))),
        compiler_params=_params(),
    )(c_all, c_ctx, _my_cols(dmx, ws), dmx, dmc, _my_cols(dmc, ws), w_mod_g)


def _my_cols(a, ws):
    me = 4 * lax.axis_index("x") + 2 * lax.axis_index("y") + lax.axis_index("c")
    return lax.dynamic_slice_in_dim(a, me * ws, ws, axis=1)


def _pair_sum(mine, other, name):
    n, r, c = mine.shape
    tr = _tile(r, (256, 128, 64, 32, 16, 8))

    def body(a_ref, b_ref, o_ref):
        o_ref[...] = (a_ref[...].astype(F32) + b_ref[...].astype(F32)).astype(o_ref.dtype)

    blk = pl.BlockSpec((1, tr, c), lambda j, i: (j, i, 0))
    return pl.pallas_call(
        body, name=name, out_shape=_sds((n, r, c), mine.dtype), grid=(n, r // tr),
        in_specs=[blk, blk], out_specs=blk, compiler_params=_params(("parallel", "parallel")),
    )(mine, other)


def _adamw(parts, w, m, v, name, chip_sums_below=None):
    s_, r, c = parts.shape
    tr = _tile(r, (128, 64, 32, 16, 8)) if r * c * 4 > (1 << 20) else r
    c1 = 1.0 / (1.0 - ADAM_B1 ** ADAM_STEP)
    c2 = 1.0 / (1.0 - ADAM_B2 ** ADAM_STEP)

    def body(p_ref, w_ref, m_ref, v_ref, g_ref, d_ref, nm_ref, nv_ref):
        if chip_sums_below is None:
            part = lambda s: p_ref[s].astype(F32)
        else:
            core = lax.axis_index("c")
            me = 4 * lax.axis_index("x") + 2 * lax.axis_index("y") + core
            every = me >= chip_sums_below
            part = lambda s: jnp.where(every | (core == s % 2), p_ref[s].astype(F32), 0.0)
        g = part(0)
        for s in range(1, s_):
            g = g + part(s)
        m_new = ADAM_B1 * m_ref[...] + (1.0 - ADAM_B1) * g
        v_new = ADAM_B2 * v_ref[...] + (1.0 - ADAM_B2) * (g * g)
        g_ref[...] = g
        nm_ref[...] = m_new
        nv_ref[...] = v_new
        d_ref[...] = -ADAM_LR * ((m_new * c1) / (jnp.sqrt(v_new * c2) + ADAM_EPS) + ADAM_WD * w_ref[...])

    blk = pl.BlockSpec((tr, c), lambda i: (i, 0))
    o = _sds((r, c))
    return pl.pallas_call(
        body, name=name, out_shape=(o, o, o, o), grid=(r // tr,),
        in_specs=[pl.BlockSpec((s_, tr, c), lambda i: (0, i, 0)), blk, blk, blk],
        out_specs=(blk, blk, blk, blk),
        compiler_params=_params(("parallel",)),
    )(parts, w, m, v)


def _adamw_many(gs, ws, ms, vs, name):
    n = len(gs)
    c1 = 1.0 / (1.0 - ADAM_B1 ** ADAM_STEP)
    c2 = 1.0 / (1.0 - ADAM_B2 ** ADAM_STEP)

    def body(*refs):
        g_in, w_in_, m_in, v_in = (refs[k * n:(k + 1) * n] for k in range(4))
        g_out, d_out, m_out, v_out = (refs[(4 + k) * n:(5 + k) * n] for k in range(4))
        for p in range(n):
            g = g_in[p][...]
            m_new = ADAM_B1 * m_in[p][...] + (1.0 - ADAM_B1) * g
            v_new = ADAM_B2 * v_in[p][...] + (1.0 - ADAM_B2) * (g * g)
            g_out[p][...] = g
            m_out[p][...] = m_new
            v_out[p][...] = v_new
            d_out[p][...] = -ADAM_LR * ((m_new * c1) / (jnp.sqrt(v_new * c2) + ADAM_EPS) + ADAM_WD * w_in_[p][...])

    shapes = tuple(_sds(g.shape) for g in gs)
    res = pl.pallas_call(body, name=name, out_shape=shapes * 4, compiler_params=_params())(*gs, *ws, *ms, *vs)
    return [res[k * n:(k + 1) * n] for k in range(4)]


def _rows(a):
    flat = a.reshape(-1)
    n = flat.shape[0]
    r = -(-n // (8 * LANE)) * 8
    return jnp.pad(flat, (0, r * LANE - n)).reshape(r, LANE)


def _pack(items):
    parts, layout, at = [], [], 0
    for name, a in items:
        rws = _rows(a.astype(F32))
        layout.append((name, at, rws.shape[0], a.shape))
        parts.append(rws)
        at += rws.shape[0]
    return jnp.concatenate(parts, axis=0), layout


def _unpack(packed, layout):
    out = {}
    for name, at, r, shape in layout:
        n = 1
        for s in shape:
            n *= s
        out[name] = packed[at:at + r].reshape(-1)[:n].reshape(shape)
    return out


def kernel(x, c, ctx, c_ctx, w_mod, b_mod, g_pre, g_post, w_in, w_conv, a_log, dt_bias, g_onorm, gm_ln_g, gm_ln_b, w_sp, b_sp, w_pa, w_pb, w_out, loss_target, m_c_ctx, m_w_mod, m_b_mod, m_g_pre, m_g_post, m_w_in, m_w_conv, m_a_log, m_dt_bias, m_g_onorm, m_gm_ln_g, m_gm_ln_b, m_w_sp, m_b_sp, m_w_pa, m_w_pb, m_w_out, v_c_ctx, v_w_mod, v_b_mod, v_g_pre, v_g_post, v_w_in, v_w_conv, v_a_log, v_dt_bias, v_g_onorm, v_gm_ln_g, v_gm_ln_b, v_w_sp, v_b_sp, v_w_pa, v_w_pb, v_w_out):
    l = x.shape[1]
    lc = ctx.shape[1]
    lt = l + lc
    nch = lt // CH
    me = 4 * lax.axis_index("x") + 2 * lax.axis_index("y") + lax.axis_index("c")
    wsh = w_in.shape[2]
    off_a = 3 * D
    n_ab = 4 * NH
    jb = off_a // wsh
    o1 = off_a - jb * wsh
    o2 = o1 + n_ab
    assert o2 <= wsh and NREST == (NDEV - jb) * wsh - o2
    split = jb + 1

    w_in_bf = w_in[0].astype(_BF)
    wg_lo, wg_mod, wg_conv, c_all = _exchange(
        [w_in_bf, w_mod[0].astype(_BF), w_conv[0], c], ["gather_lo", "gather", "gather", "gather"],
        "gather_first", split)
    w_qkv = jnp.concatenate([wg_lo[j][:, :wsh] for j in range(jb)] + [wg_lo[jb][:, :o1]], axis=1)
    w_ab = jnp.pad(wg_lo[jb][:, o1:o2], ((0, 0), (0, LANE - n_ab)))
    wconv_full = jnp.moveaxis(wg_conv, 0, 1).reshape(3, 3 * D)
    c_all = c_all.reshape(NDEV, D)

    cc = jnp.concatenate([c, c_ctx.reshape(1, D), jnp.zeros((6, D), F32)], axis=0)
    mods = _modulation(cc, wg_mod, b_mod)
    h, h_t = _prenorm(ctx[0], x[0], mods, g_pre)
    p_qkv = _matmul_nn(h, w_qkv, "in_proj_qkv")
    pab = _matmul_nn(h, w_ab, "in_proj_ab")
    abt = jnp.swapaxes(pab[:, :n_ab].reshape(nch, CH, n_ab), 1, 2)
    alog16, dtb16 = a_log.reshape(1, 2 * NH), dt_bias.reshape(1, 2 * NH)
    alog_r = jnp.pad(alog16, ((0, 0), (0, LANE - 2 * NH)))
    dtb_r = jnp.pad(dtb16, ((0, 0), (0, LANE - 2 * NH)))
    alog_c = jnp.pad(alog16.reshape(2 * NH, 1), ((0, 2 * NH), (0, 0)))
    dtb_c = jnp.pad(dtb16.reshape(2 * NH, 1), ((0, 2 * NH), (0, 0)))
    qkv = _qkv_fwd(p_qkv, wconv_full, lc)
    late = [w_in_bf, w_pa[0].astype(_BF), w_pb[0].astype(_BF), w_out[0].astype(_BF)]
    xc_late = _Exchange(zip(late, ["gather_hi", "gather", "gather", "gather"]), split)
    o_f, o_b, s_f, s_b, t_f, t_b, wg_hi, wg_pa, wg_pb, wg_out = _gdn_fwd(
        qkv, pab, abt, alog_r, dtb_r, alog_c, dtb_c, lc, xc_late, late)
    w_rest = jnp.concatenate([wg_lo[jb][:, o2:wsh]] + [wg_hi[j][:, :wsh] for j in range(split, NDEV)], axis=1)
    wf_pa, wf_pb, wf_out = wg_pa.reshape(D, D), wg_pb.reshape(D, D), wg_out.reshape(D, D)
    p_rest = _matmul_nn(h, w_rest, "in_proj_rest")

    w_spt = jnp.swapaxes(w_sp[0], 1, 2)
    b_spb = jnp.broadcast_to(b_sp[0][:, :, None], (NH, GC, GC))
    gate_x = mods[0:1, 2 * D:]
    dp_rest, do, dy, ya, yb, mg, d_a, d_b, dout, dwsp, dbsp_l, pvec = _post(
        p_rest, o_f, o_b, x[0], loss_target[0], wf_pa, wf_pb, wf_out, w_sp[0], w_spt, b_spb, gm_ln_g, gm_ln_b,
        g_onorm, g_post, gate_x, lc)

    dw_rest = _matmul_nn(h_t, dp_rest, "dw_in_rest", _BF)
    o3 = wsh - o2
    chunks_hi = jnp.moveaxis(dw_rest[:, o3:].reshape(D, NDEV - split, wsh), 1, 0)
    dw_pa = _matmul_nn(ya, d_a, "dw_pa", _BF).reshape(NDEV, D // NDEV, D)
    dw_pb = _matmul_nn(yb, d_b, "dw_pb", _BF).reshape(NDEV, D // NDEV, D)
    dw_out = _matmul_nn(mg, dout, "dw_out", _BF).reshape(NDEV, D // NDEV, D)
    small_a, lay_a = _pack([
        ("g_post", pvec[1]), ("g_onorm", pvec[4, :DH]), ("gm_ln_g", pvec[2]), ("gm_ln_b", pvec[3]), ("w_sp", dwsp),
        ("b_sp", jnp.sum(dbsp_l, axis=-1)), ("loss", pvec[5]), ("dgate", pvec[0])])
    (theirs_hi,) = _exchange([chunks_hi], ["sibling"], "pair_swap_hi")
    chip_hi = _pair_sum(chunks_hi, theirs_hi[0], "pair_sum_hi")
    early = [chip_hi, dw_pa, dw_pb, dw_out, small_a]
    xc_early = _Exchange(zip(early, ["scatter_par_hi", "scatter", "scatter", "scatter", "gather"]), split)

    dqkv_f, dqkv_b, dcol_f, dcol_b, drow_f, drow_b, gvec_c, gvec_r, r_in, r_pa, r_pb, r_out, small_a_all = _gdn_bwd(
        qkv, pab, abt, alog_r, dtb_r, alog_c, dtb_c, s_f, s_b, t_f, t_b, do, lc, xc_early, early)
    dp_qkv, dwconv = _qkv_bwd(p_qkv, wconv_full, dqkv_f, dqkv_b, lc)
    drow = jnp.swapaxes(drow_f + drow_b, 1, 2).reshape(lt, n_ab)
    dpab = (dcol_f + dcol_b + jnp.pad(drow, ((0, 0), (0, LANE - n_ab)))).astype(_BF)

    dw_qkv = _matmul_nn(h_t, dp_qkv, "dw_in_qkv", _BF)
    dw_ab = _matmul_nn(h_t, dpab, "dw_in_ab", _BF)
    dw_lo = jnp.concatenate([dw_qkv, dw_ab[:, :n_ab], dw_rest[:, :o3]], axis=1)
    chunks_lo = jnp.moveaxis(dw_lo.reshape(D, split, wsh), 1, 0)
    (theirs,) = _exchange([chunks_lo], ["sibling"], "pair_swap")
    chip_lo = _pair_sum(chunks_lo, theirs[0], "pair_sum")
    xc_last = _Exchange([(chip_lo, "scatter_par_lo")], split)
    dh, r_in = _dh_matmul(dp_rest, dp_qkv, dpab, w_rest, w_qkv, w_ab, xc_last, [chip_lo], {0: r_in})
    grad_x, nvec = _prenorm_bwd(ctx[0], x[0], dh, dy, mods, g_pre)

    dalog = gvec_c[0, :2 * NH] + gvec_r[:2 * NH, 0]
    ddtb = gvec_c[1, :2 * NH] + gvec_r[:2 * NH, 1]
    small_b, lay_b = _pack([
        ("g_pre", nvec[4]), ("a_log", dalog), ("dt_bias", ddtb), ("w_conv", dwconv),
        ("dshift", nvec[0]), ("dscale", nvec[1]), ("dshift_c", nvec[2]), ("dscale_c", nvec[3])])
    (small_b_all,) = _exchange([small_b], ["gather"], "gather_small")
    tot = _unpack(_sum_parts(small_a_all, "sum_small_a"), lay_a)
    tot.update(_unpack(_sum_parts(small_b_all, "sum_small_b"), lay_b))

    def per_device(packed_all, layout, name):
        at, r = [(a_, r_) for nm, a_, r_, _ in layout if nm == name][0]
        return packed_all[:, at:at + r].reshape(NDEV, -1)

    dmx_all = jnp.concatenate([per_device(small_b_all, lay_b, "dshift"), per_device(small_b_all, lay_b, "dscale"),
                               per_device(small_a_all, lay_a, "dgate")], axis=1)
    dmc_all = jnp.concatenate([per_device(small_b_all, lay_b, "dshift_c"), per_device(small_b_all, lay_b, "dscale_c"),
                               jnp.zeros((NDEV, D), F32)], axis=1)
    g_wmod, g_cctx, g_bmod = _mod_bwd(c_all, c_ctx.reshape(1, D), dmx_all, dmc_all, wg_mod)
    loss = 0.5 / D * jnp.sum(tot["loss"])
    ws_conv = w_conv.shape[2]
    g_wconv = lax.dynamic_slice_in_dim(tot["w_conv"], me * ws_conv, ws_conv, axis=1)

    small_names = ["c_ctx", "b_mod", "g_pre", "g_post", "a_log", "dt_bias", "g_onorm", "gm_ln_g", "gm_ln_b",
                   "w_sp", "b_sp", "w_conv"]
    wts = dict(c_ctx=c_ctx, b_mod=b_mod, g_pre=g_pre, g_post=g_post, a_log=a_log, dt_bias=dt_bias, g_onorm=g_onorm,
               gm_ln_g=gm_ln_g, gm_ln_b=gm_ln_b, w_sp=w_sp, b_sp=b_sp, w_conv=w_conv)
    ms = dict(c_ctx=m_c_ctx, b_mod=m_b_mod, g_pre=m_g_pre, g_post=m_g_post, a_log=m_a_log, dt_bias=m_dt_bias,
              g_onorm=m_g_onorm, gm_ln_g=m_gm_ln_g, gm_ln_b=m_gm_ln_b, w_sp=m_w_sp, b_sp=m_b_sp, w_conv=m_w_conv)
    vs = dict(c_ctx=v_c_ctx, b_mod=v_b_mod, g_pre=v_g_pre, g_post=v_g_post, a_log=v_a_log, dt_bias=v_dt_bias,
              g_onorm=v_g_onorm, gm_ln_g=v_gm_ln_g, gm_ln_b=v_gm_ln_b, w_sp=v_w_sp, b_sp=v_b_sp, w_conv=v_w_conv)
    gs = dict(tot)
    gs.update(c_ctx=g_cctx, b_mod=g_bmod, w_conv=g_wconv)
    flat = lambda a: a.reshape(-1, a.shape[-1])
    res_small = [
        {nm: a.reshape(wts[nm].shape) for nm, a in zip(small_names, arrays)}
        for arrays in _adamw_many([flat(gs[nm].reshape(wts[nm].shape)) for nm in small_names],
                                  [flat(wts[nm]) for nm in small_names], [flat(ms[nm]) for nm in small_names],
                                  [flat(vs[nm]) for nm in small_names], "adamw_small")]
    res_big = {
        "w_mod": _adamw(g_wmod[None], w_mod[0], m_w_mod[0], v_w_mod[0], "adamw_w_mod"),
        "w_in": _adamw(r_in, w_in[0], m_w_in[0], v_w_in[0], "adamw_w_in", chip_sums_below=NDEV),
        "w_pa": _adamw(r_pa, w_pa[0], m_w_pa[0], v_w_pa[0], "adamw_w_pa"),
        "w_pb": _adamw(r_pb, w_pb[0], m_w_pb[0], v_w_pb[0], "adamw_w_pb"),
        "w_out": _adamw(r_out, w_out[0], m_w_out[0], v_w_out[0], "adamw_w_out"),
    }
    order = ["c_ctx", "w_mod", "b_mod", "g_pre", "g_post", "w_in", "w_conv", "a_log", "dt_bias", "g_onorm",
             "gm_ln_g", "gm_ln_b", "w_sp", "b_sp", "w_pa", "w_pb", "w_out"]
    outs = [loss, grad_x[None]]
    for k in range(4):
        for nm in order:
            if nm in res_big:
                outs.append(res_big[nm][k][None])
            else:
                outs.append(res_small[k][nm])
    return tuple(outs)
```
